```python
import jax, jax.numpy as jnp
from jax import lax
import numpy as np

D_MODEL = 1024
BATCH = 4
SEQ = 8192
DEPTH = 1
DEC_BATCH = 32
DEC_SEQ = 8
PAST_LEN = 16384
PAGE_SIZE = 128

HG_WIDTH = D_MODEL // 2
HG_HEADS = 4
HG_DK = HG_WIDTH // HG_HEADS
HG_DV = HG_DK
HG_CHUNK = 32
NSA_WIDTH = D_MODEL - HG_WIDTH
NSA_HEADS = 8
NSA_HEAD_DIM = NSA_WIDTH // NSA_HEADS
NSA_KV_HEADS = 2
NSA_GROUP = NSA_HEADS // NSA_KV_HEADS
KV_W = NSA_KV_HEADS * NSA_HEAD_DIM
N_BRANCH_KV = 4
CMP_BLOCK = 32
CMP_STRIDE = 16
CMP_HIDDEN = 256
SEL_BLOCK = 64
SEL_TOPN = 16
WINDOW = 512
Q_BLOCK = 128
N_EXPERTS = 256
TOP_K = 8
N_GROUPS = 8
TOPK_GROUPS = 4
MOE_D_FF = 256
SHARED_D_FF = 256
ROUTED_SCALE = 2.5
MOE_BLOCK = 64
RMS_EPS = 1e-6
IN_COLS = 4 * HG_WIDTH + NSA_WIDTH + N_BRANCH_KV * KV_W + 2 * KV_W + 3 * NSA_HEADS

kernel_name = 'hymba_hgrn2_nsa_moe_adaln_step'


def rms_norm(x, g):
    xf = x.astype(jnp.float32)
    y = xf * lax.rsqrt(jnp.mean(xf * xf, axis=-1, keepdims=True) + RMS_EPS)
    return (y * g.astype(jnp.float32)).astype(x.dtype)


def ada_modulation(c, w_ada, b_ada):
    m = (jax.nn.silu(c) @ w_ada + b_ada)[:, None, :]
    return jnp.split(m, 6, axis=-1)


def masked_softmax(s, valid, axis):
    s = jnp.where(valid, s.astype(jnp.float32), -1e30)
    p = jnp.exp(s - jnp.max(s, axis=axis, keepdims=True)) * valid
    return p / jnp.maximum(jnp.sum(p, axis=axis, keepdims=True), 1e-30)


def swiglu(x, wg, wu, wd):
    return (jax.nn.silu(x @ wg) * (x @ wu)) @ wd


def hgrn2_recurrence(q, log_f, k, v, s0):
    b, t, h, _ = q.shape
    n_chunks = -(-t // HG_CHUNK)
    pad = n_chunks * HG_CHUNK - t
    padt = lambda a: jnp.pad(a, ((0, 0), (0, pad), (0, 0), (0, 0)))
    to_chunks = lambda a: padt(a).reshape(b, n_chunks, HG_CHUNK, h, a.shape[-1]).transpose(1, 0, 3, 2, 4)
    causal = jnp.tril(jnp.ones((HG_CHUNK, HG_CHUNK), jnp.float32))

    def step(s, inp):
        qc, lfc, kc, vc = inp
        a = jnp.cumsum(lfc, axis=2)
        q_t = qc * jnp.exp(a)
        att = jnp.einsum('bhtk,bhsk->bhts', q_t, kc * jnp.exp(-a)) * causal
        o = jnp.einsum('bhts,bhsv->bhtv', att, vc) + jnp.einsum('bhtk,bhkv->bhtv', q_t, s)
        a_end = a[:, :, -1:, :]
        s_new = jnp.exp(a_end[:, :, 0, :])[..., None] * s + jnp.einsum('bhsk,bhsv->bhkv', kc * jnp.exp(a_end - a), vc)
        return s_new, o

    s_fin, o = lax.scan(step, s0, (to_chunks(q), to_chunks(log_f), to_chunks(k), to_chunks(v)))
    o = o.transpose(1, 0, 3, 2, 4).reshape(b, n_chunks * HG_CHUNK, h, -1)[:, :t]
    return o, s_fin


def hgrn2_branch(hq, hf, hi, hgate, lb, s0, g_norm):
    b, t, _ = hq.shape
    heads = lambda a: a.astype(jnp.float32).reshape(b, t, HG_HEADS, HG_DK)
    zf = hf.astype(jnp.float32)
    log_f = jnp.log(lb + (1.0 - lb) * jax.nn.sigmoid(zf))
    k = (1.0 - lb) * jax.nn.sigmoid(-zf)
    o, s = hgrn2_recurrence(heads(hq), heads(log_f), heads(k), heads(hi), s0.astype(jnp.float32))
    o = rms_norm(o, g_norm) * jax.nn.silu(heads(hgate))
    return o.reshape(b, t, HG_WIDTH).astype(hq.dtype), s.astype(s0.dtype)


def nsa_compress(raw, pe, w1, b1, w2):
    b, n, g, dh = raw.shape
    n_cmp = (n - CMP_BLOCK) // CMP_STRIDE + 1
    idx = jnp.arange(n_cmp)[:, None] * CMP_STRIDE + jnp.arange(CMP_BLOCK)[None, :]
    blocks = raw[:, idx] + pe[:, None, :]
    flat = blocks.transpose(0, 1, 3, 2, 4).reshape(b, n_cmp, g, CMP_BLOCK * dh)
    return jax.nn.gelu(flat @ w1 + b1) @ w2


def cmp_to_sel_overlap(n_cmp, n_sel):
    cs = jnp.arange(n_cmp) * CMP_STRIDE
    ss = jnp.arange(n_sel) * SEL_BLOCK
    ov = (cs[:, None] < ss[None, :] + SEL_BLOCK) & (cs[:, None] + CMP_BLOCK > ss[None, :])
    return ov.astype(jnp.float32)


def nsa_block_keys(kv4, cmp_pe, cmp_w1, cmp_b1, cmp_w2):
    b, n = kv4.shape[:2]
    kc = nsa_compress(kv4[:, :, 0], cmp_pe[0], cmp_w1[0], cmp_b1[0], cmp_w2[0])
    vc = nsa_compress(kv4[:, :, 1], cmp_pe[1], cmp_w1[1], cmp_b1[1], cmp_w2[1])
    n_sel = -(-n // SEL_BLOCK)
    sel = jnp.pad(kv4[:, :, 2:4], ((0, 0), (0, n_sel * SEL_BLOCK - n), (0, 0), (0, 0), (0, 0)))
    sel = sel.reshape(b, n_sel, SEL_BLOCK, 2, NSA_KV_HEADS, NSA_HEAD_DIM).transpose(3, 0, 4, 1, 2, 5)
    return kc, vc, sel[0], sel[1]


def nsa_core(q, gates, qpos, kc, vc, ks, vs, kw, vw, wpos):
    b, tq = q.shape[:2]
    scale = NSA_HEAD_DIM ** -0.5
    qg = q.reshape(b, tq, NSA_KV_HEADS, NSA_GROUP, NSA_HEAD_DIM)
    n_cmp = kc.shape[1]
    c_end = jnp.arange(n_cmp) * CMP_STRIDE + (CMP_BLOCK - 1)
    valid_c = (c_end[None, :] <= qpos[:, None])[None, :, None, None, :]
    p_c = masked_softmax(jnp.einsum('btgjd,bcgd->btgjc', qg, kc) * scale, valid_c, -1)
    o_c = jnp.einsum('btgjc,bcgd->btgjd', p_c.astype(vc.dtype), vc)
    n_sel = ks.shape[2]
    imp = jnp.einsum('btgc,cs->btgs', p_c.sum(axis=3), cmp_to_sel_overlap(n_cmp, n_sel))
    blk = jnp.arange(n_sel)
    cur = (qpos // SEL_BLOCK)[:, None]
    forced = (blk[None, :] == 0) | (blk[None, :] == cur) | (blk[None, :] == cur - 1)
    allowed = (blk * SEL_BLOCK)[None, :] <= qpos[:, None]
    imp = jnp.where(forced[None, :, None, :], jnp.inf, jnp.where(allowed[None, :, None, :], imp, -jnp.inf))
    _, idx = lax.top_k(imp, min(SEL_TOPN, n_sel))
    idx = idx.transpose(0, 2, 1, 3)
    b_ix = jnp.arange(b)[:, None, None, None]
    g_ix = jnp.arange(NSA_KV_HEADS)[None, :, None, None]
    k_sel = ks[b_ix, g_ix, idx]
    v_sel = vs[b_ix, g_ix, idx]
    kpos = idx[..., None] * SEL_BLOCK + jnp.arange(SEL_BLOCK)
    valid_s = (kpos <= qpos[None, None, :, None, None]).transpose(0, 2, 1, 3, 4)[:, :, :, None]
    p_s = masked_softmax(jnp.einsum('btgjd,bgtnld->btgjnl', qg, k_sel) * scale, valid_s, (-2, -1))
    o_s = jnp.einsum('btgjnl,bgtnld->btgjd', p_s.astype(v_sel.dtype), v_sel)
    dpos = qpos[:, None] - wpos[None, :]
    valid_w = ((dpos >= 0) & (dpos < WINDOW) & (wpos[None, :] >= 0))[None, :, None, None, :]
    p_w = masked_softmax(jnp.einsum('btgjd,bwgd->btgjw', qg, kw) * scale, valid_w, -1)
    o_w = jnp.einsum('btgjw,bwgd->btgjd', p_w.astype(vw.dtype), vw)
    g = gates.reshape(b, tq, NSA_KV_HEADS, NSA_GROUP, 3)
    o = g[..., 0:1] * o_c + g[..., 1:2] * o_s + g[..., 2:3] * o_w
    return o.reshape(b, tq, NSA_WIDTH)


def nsa_prompt(q, gates, kv4, kvw, cmp_pe, cmp_w1, cmp_b1, cmp_w2):
    b, s = q.shape[:2]
    kc, vc, ks, vs = nsa_block_keys(kv4, cmp_pe, cmp_w1, cmp_b1, cmp_w2)
    kvw_pad = jnp.pad(kvw, ((0, 0), (WINDOW, 0), (0, 0), (0, 0), (0, 0)))
    n_qb = s // Q_BLOCK
    qb = q.reshape(b, n_qb, Q_BLOCK, NSA_HEADS, NSA_HEAD_DIM).swapaxes(0, 1)
    gb = gates.reshape(b, n_qb, Q_BLOCK, NSA_HEADS, 3).swapaxes(0, 1)
    span = WINDOW + Q_BLOCK

    def one_block(args):
        j, q_j, g_j = args
        start = j * Q_BLOCK
        w_j = lax.dynamic_slice_in_dim(kvw_pad, start, span, axis=1)
        qpos = start + jnp.arange(Q_BLOCK)
        wpos = start - WINDOW + jnp.arange(span)
        return nsa_core(q_j, g_j, qpos, kc, vc, ks, vs, w_j[:, :, 0], w_j[:, :, 1], wpos)

    o = lax.map(one_block, (jnp.arange(n_qb), qb, gb))
    return o.swapaxes(0, 1).reshape(b, s, NSA_WIDTH)


def nsa_sample(q, gates, kv4, kvw, past_pages, win_buf, cmp_pe, cmp_w1, cmp_b1, cmp_w2):
    b, t = q.shape[:2]
    past_len = past_pages.shape[1] * past_pages.shape[2]
    past = past_pages.reshape(b, past_len, N_BRANCH_KV, NSA_KV_HEADS, NSA_HEAD_DIM)
    kc, vc, ks, vs = nsa_block_keys(jnp.concatenate([past, kv4], axis=1), cmp_pe, cmp_w1, cmp_b1, cmp_w2)
    wb = win_buf.shape[1]
    win_full = jnp.concatenate([win_buf, kvw], axis=1)
    qpos = past_len + jnp.arange(t)
    wpos = past_len - wb + jnp.arange(wb + t)
    o = nsa_core(q, gates, qpos, kc, vc, ks, vs, win_full[:, :, 0], win_full[:, :, 1], wpos)
    return o, (kv4, win_full[:, -wb:])


def grouped_experts(xf, top_e, top_w, w_gate, w_up, w_down):
    n_tok, d = xf.shape
    n_pairs = n_tok * TOP_K
    flat_e = top_e.reshape(-1)
    order = jnp.argsort(flat_e)
    e_sorted = flat_e[order]
    counts = jnp.bincount(flat_e, length=N_EXPERTS)
    padded = (counts + MOE_BLOCK - 1) // MOE_BLOCK * MOE_BLOCK
    pad_end = jnp.cumsum(padded)
    start = jnp.cumsum(counts) - counts
    dest = (pad_end - padded)[e_sorted] + jnp.arange(n_pairs) - start[e_sorted]
    n_blocks = (n_pairs + N_EXPERTS * (MOE_BLOCK - 1)) // MOE_BLOCK
    n_rows = n_blocks * MOE_BLOCK
    row_tok = jnp.full((n_rows,), n_tok, jnp.int32).at[dest].set((order // TOP_K).astype(jnp.int32))
    row_w = jnp.zeros((n_rows,), top_w.dtype).at[dest].set(top_w.reshape(-1)[order])
    block_e = jnp.minimum(jnp.searchsorted(pad_end, jnp.arange(n_blocks) * MOE_BLOCK, side='right'), N_EXPERTS - 1)
    x_pad = jnp.concatenate([xf, jnp.zeros((1, d), xf.dtype)], axis=0)

    def run(args):
        tok_b, e = args
        return swiglu(x_pad[tok_b], w_gate[e], w_up[e], w_down[e])

    y = lax.map(run, (row_tok.reshape(n_blocks, MOE_BLOCK), block_e)).reshape(n_rows, d) * row_w[:, None]
    return jnp.zeros_like(x_pad).at[row_tok].add(y)[:n_tok]


def moe_ffn(h, w_router, router_bias, w_gate, w_up, w_down, ws_gate, ws_up, ws_down):
    b, t, d = h.shape
    xf = h.reshape(-1, d)
    n_tok = xf.shape[0]
    scores = jax.nn.sigmoid((xf @ w_router).astype(jnp.float32))
    biased = scores + router_bias.astype(jnp.float32)
    grp_score = lax.top_k(biased.reshape(n_tok, N_GROUPS, -1), 2)[0].sum(-1)
    _, top_g = lax.top_k(grp_score, TOPK_GROUPS)
    g_mask = jnp.any(top_g[:, :, None] == jnp.arange(N_GROUPS)[None, None, :], axis=1)
    e_mask = jnp.repeat(g_mask, N_EXPERTS // N_GROUPS, axis=1)
    _, top_e = lax.top_k(jnp.where(e_mask, biased, -jnp.inf), TOP_K)
    w = jnp.take_along_axis(scores, top_e, axis=1)
    w = w / jnp.sum(w, axis=-1, keepdims=True) * ROUTED_SCALE
    routed = grouped_experts(xf, top_e, w.astype(xf.dtype), w_gate, w_up, w_down)
    return (routed + swiglu(xf, ws_gate, ws_up, ws_down)).reshape(b, t, d)


def decoder_layer(x, c, hg_s0, lb, attend, attn_norm, ffn_norm, hg_norm, w_ada, b_ada, w_in, w_out,
                  w_router, router_bias, w_gate, w_up, w_down, ws_gate, ws_up, ws_down):
    b, t, _ = x.shape
    shift1, scale1, gate1, shift2, scale2, gate2 = ada_modulation(c, w_ada, b_ada)
    h = rms_norm(x, attn_norm) * (1 + scale1) + shift1
    cuts = np.cumsum([HG_WIDTH] * 4 + [NSA_WIDTH, N_BRANCH_KV * KV_W, 2 * KV_W]).tolist()
    hq, hf, hi, hgate, nq, kv4, kvw, gz = jnp.split(h @ w_in, cuts, axis=-1)
    hg_out, hg_s = hgrn2_branch(hq, hf, hi, hgate, lb, hg_s0, hg_norm)
    nsa_out, attn_state = attend(
        nq.reshape(b, t, NSA_HEADS, NSA_HEAD_DIM),
        jax.nn.sigmoid(gz).reshape(b, t, NSA_HEADS, 3),
        kv4.reshape(b, t, N_BRANCH_KV, NSA_KV_HEADS, NSA_HEAD_DIM),
        kvw.reshape(b, t, 2, NSA_KV_HEADS, NSA_HEAD_DIM))
    x = x + gate1 * (jnp.concatenate([hg_out, nsa_out.astype(x.dtype)], axis=-1) @ w_out)
    h2 = rms_norm(x, ffn_norm) * (1 + scale2) + shift2
    x = x + gate2 * moe_ffn(h2, w_router, router_bias, w_gate, w_up, w_down, ws_gate, ws_up, ws_down)
    return x, hg_s, attn_state


def setup_inputs(seed: int = 0) -> dict:
    key = jax.random.key(seed)
    keys = iter(jax.random.split(key, 40))
    nrm = lambda shape, s=1.0: jax.random.normal(next(keys), shape, jnp.float32) * s
    n_pages = PAST_LEN // PAGE_SIZE
    n_pool = (5 * DEC_BATCH * n_pages) // 4
    win_buf = min(WINDOW, PAST_LEN)
    L, D = DEPTH, D_MODEL
    page_table = jax.random.permutation(next(keys), n_pool)[: DEC_BATCH * n_pages]
    page_table = page_table.reshape(DEC_BATCH, n_pages).astype(jnp.int32)
    return {
        'x_prompt': nrm((BATCH, SEQ, D)),
        'x_sample': nrm((DEC_BATCH, DEC_SEQ, D)),
        'c_prompt': nrm((BATCH, D)),
        'c_sample': nrm((DEC_BATCH, D)),
        'cache_nsa_kv': nrm((L, n_pool, PAGE_SIZE, N_BRANCH_KV, NSA_KV_HEADS, NSA_HEAD_DIM)),
        'cache_win_kv': nrm((L, DEC_BATCH, win_buf, 2, NSA_KV_HEADS, NSA_HEAD_DIM)),
        'state_hgrn': nrm((L, DEC_BATCH, HG_HEADS, HG_DK, HG_DV), 0.3),
        'page_table': page_table,
        'attn_norm': 1.0 + nrm((L, D), 0.1),
        'ffn_norm': 1.0 + nrm((L, D), 0.1),
        'final_norm': 1.0 + nrm((D,), 0.1),
        'hg_norm': 1.0 + nrm((L, HG_DV), 0.1),
        'w_ada': nrm((L, D, 6 * D), 0.5 * D ** -0.5),
        'b_ada': nrm((L, 6 * D), 0.1),
        'w_in': nrm((L, D, IN_COLS), D ** -0.5),
        'hg_lb': nrm((DEPTH + 1, HG_WIDTH), 0.1),
        'cmp_pe': nrm((L, 2, CMP_BLOCK, NSA_HEAD_DIM), 0.1),
        'cmp_w1': nrm((L, 2, CMP_BLOCK * NSA_HEAD_DIM, CMP_HIDDEN), (CMP_BLOCK * NSA_HEAD_DIM) ** -0.5),
        'cmp_b1': nrm((L, 2, CMP_HIDDEN), 0.01),
        'cmp_w2': nrm((L, 2, CMP_HIDDEN, NSA_HEAD_DIM), CMP_HIDDEN ** -0.5),
        'w_out': nrm((L, D, D), D ** -0.5),
        'w_router': nrm((L, D, N_EXPERTS), D ** -0.5),
        'router_bias': nrm((L, N_EXPERTS), 0.01),
        'w_gate': nrm((L, N_EXPERTS, D, MOE_D_FF), D ** -0.5),
        'w_up': nrm((L, N_EXPERTS, D, MOE_D_FF), D ** -0.5),
        'w_down': nrm((L, N_EXPERTS, MOE_D_FF, D), MOE_D_FF ** -0.5),
        'ws_gate': nrm((L, D, SHARED_D_FF), D ** -0.5),
        'ws_up': nrm((L, D, SHARED_D_FF), D ** -0.5),
        'ws_down': nrm((L, SHARED_D_FF, D), SHARED_D_FF ** -0.5),
    }


def reference(x_prompt, x_sample, c_prompt, c_sample, cache_nsa_kv, cache_win_kv, state_hgrn, page_table,
              attn_norm, ffn_norm, final_norm, hg_norm, w_ada, b_ada, w_in, hg_lb,
              cmp_pe, cmp_w1, cmp_b1, cmp_w2, w_out, w_router, router_bias,
              w_gate, w_up, w_down, ws_gate, ws_up, ws_down):
    lb_all = jnp.cumsum(jax.nn.softmax(hg_lb.astype(jnp.float32), axis=0), axis=0)
    xp, xs = x_prompt, x_sample
    kv_p, win_p, hg_p, kv_s, win_s, hg_s = [], [], [], [], [], []
    for l in range(DEPTH):
        lw = (attn_norm[l], ffn_norm[l], hg_norm[l], w_ada[l], b_ada[l], w_in[l], w_out[l],
              w_router[l], router_bias[l], w_gate[l], w_up[l], w_down[l], ws_gate[l], ws_up[l], ws_down[l])
        cmp_l = (cmp_pe[l], cmp_w1[l], cmp_b1[l], cmp_w2[l])

        def attend_prompt(q, g, kv4, kvw, cmp_l=cmp_l):
            o = nsa_prompt(q, g, kv4, kvw, *cmp_l)
            return o, (kv4, kvw[:, -min(WINDOW, kvw.shape[1]):])

        def attend_sample(q, g, kv4, kvw, cmp_l=cmp_l, l=l):
            return nsa_sample(q, g, kv4, kvw, cache_nsa_kv[l, page_table], cache_win_kv[l], *cmp_l)

        hg0 = jnp.zeros((xp.shape[0], HG_HEADS, HG_DK, HG_DV), xp.dtype)
        xp, hgp, (kvp, wp) = decoder_layer(xp, c_prompt, hg0, lb_all[l], attend_prompt, *lw)
        xs, hgs, (kvs, ws) = decoder_layer(xs, c_sample, state_hgrn[l], lb_all[l], attend_sample, *lw)
        kv_p.append(kvp); win_p.append(wp); hg_p.append(hgp)
        kv_s.append(kvs); win_s.append(ws); hg_s.append(hgs)
    y_prompt = rms_norm(xp, final_norm)
    y_sample = rms_norm(xs, final_norm)
    return (y_prompt, y_sample, jnp.stack(kv_p), jnp.stack(win_p), jnp.stack(hg_p),
            jnp.stack(kv_s), jnp.stack(win_s), jnp.stack(hg_s))
```

```python
import functools

import jax
import jax.numpy as jnp
import numpy as np
from jax import lax
from jax.experimental import pallas as pl
from jax.experimental.pallas import tpu as pltpu

F32 = jnp.float32
BF16 = jnp.bfloat16

D_MODEL = 1024
HG_WIDTH = 512
HG_HEADS = 4
HG_DK = 128
HG_CHUNK = 32
NSA_WIDTH = 512
NSA_HEADS = 8
NSA_HEAD_DIM = 64
NSA_KV_HEADS = 2
NSA_GROUP = 4
KV_W = 128
CMP_BLOCK = 32
CMP_STRIDE = 16
CMP_HIDDEN = 256
SEL_BLOCK = 64
SEL_TOPN = 16
WINDOW = 512
Q_BLOCK = 128
N_EXPERTS = 256
TOP_K = 8
N_GROUPS = 8
TOPK_GROUPS = 4
MOE_D_FF = 256
ROUTED_SCALE = 2.5
RMS_EPS = 1e-6
PAGE_SIZE = 128
IN_COLS = 4 * HG_WIDTH + NSA_WIDTH + 4 * KV_W + 2 * KV_W + 3 * NSA_HEADS

LANES = 128
SUBLANES = 8
TOK_ROWS = D_MODEL // LANES
VMEM_LIMIT = 56 * 1024 * 1024

QPAD_W = NSA_HEADS * LANES
GZ_PAD = LANES
INP_COLS = 4 * HG_WIDTH + QPAD_W + 4 * KV_W + 2 * KV_W + GZ_PAD

NEG = -1e30
PAGES_PER_STEP = 32
MOE_BM = 256
ROUTER_TM = 256
COMBINE_TM = 128


def _cparams(sem):
    return pltpu.CompilerParams(dimension_semantics=sem, vmem_limit_bytes=VMEM_LIMIT)


def _dot(a, b):
    return jnp.dot(a, b, preferred_element_type=F32)


def _dot_nt(a, b):
    return lax.dot_general(a, b, (((1,), (1,)), ((), ())), preferred_element_type=F32)


def _dot_tn(a, b):
    return lax.dot_general(a, b, (((0,), (0,)), ((), ())), preferred_element_type=F32)


def _rms(x, g):
    return x * lax.rsqrt(jnp.mean(x * x, axis=-1, keepdims=True) + RMS_EPS) * g


def _silu(x):
    return x * jax.nn.sigmoid(x)


def _masked_softmax(s, valid):
    s = jnp.where(valid, s, NEG)
    m = jnp.max(s, axis=1, keepdims=True)
    p = jnp.exp(s - m) * valid.astype(F32)
    return p / jnp.maximum(jnp.sum(p, axis=1, keepdims=True), 1e-30)


def _topk_mask(v, k):
    lane = lax.broadcasted_iota(jnp.int32, v.shape, 1).astype(F32)
    sel = jnp.zeros(v.shape, F32)
    for _ in range(k):
        m = jnp.max(v, axis=1, keepdims=True)
        idx = jnp.min(jnp.where(v == m, lane, 1e9), axis=1, keepdims=True)
        pick = lane == idx
        sel = jnp.where(pick, 1.0, sel)
        v = jnp.where(pick, 3.0 * NEG, v)
    return sel


def _mod_spec(mod, tm):
    if mod.shape[1] == 1:
        return pl.BlockSpec((1, 1, D_MODEL), lambda b, i: (b, 0, 0))
    return pl.BlockSpec((1, tm, D_MODEL), lambda b, i: (b, i, 0))


def _load_tok_tiles(ref, n_tok):
    return jnp.concatenate([ref[pl.ds(s, n_tok, stride=TOK_ROWS), :] for s in range(TOK_ROWS)], axis=1)


def _store_tok_tiles(ref, val, n_tok):
    for s in range(TOK_ROWS):
        ref[pl.ds(s, n_tok, stride=TOK_ROWS), :] = val[:, s * LANES:(s + 1) * LANES]


def _ada_kernel(c_ref, w_ref, b_ref, o_ref):
    s = _silu(c_ref[...]).astype(BF16)
    o_ref[...] = _dot(s, w_ref[...].astype(BF16)) + b_ref[...]


def _ada(c_all, w_ada, b_ada):
    n = c_all.shape[0]
    return pl.pallas_call(
        _ada_kernel,
        grid=(6,),
        in_specs=[pl.BlockSpec((n, D_MODEL), lambda j: (0, 0)),
                  pl.BlockSpec((D_MODEL, D_MODEL), lambda j: (0, j)),
                  pl.BlockSpec((1, D_MODEL), lambda j: (0, j))],
        out_specs=pl.BlockSpec((n, D_MODEL), lambda j: (0, j)),
        out_shape=jax.ShapeDtypeStruct((n, 6 * D_MODEL), F32),
        compiler_params=_cparams(("arbitrary",)),
        name="ada_mod",
    )(c_all, w_ada, b_ada.reshape(1, -1))


def _inproj_kernel(x_ref, sc_ref, sh_ref, g_ref, w_ref,
                   hg_ref, q_ref, kv4_ref, kvw_ref, gate_ref, kvsel_ref, kvwb_ref):
    h = _rms(x_ref[0], g_ref[...]) * (1.0 + sc_ref[0]) + sh_ref[0]
    z = _dot(h.astype(BF16), w_ref[...])
    c0 = 4 * HG_WIDTH
    hg_ref[0] = z[:, :c0]
    q_ref[0] = (z[:, c0:c0 + QPAD_W] * (NSA_HEAD_DIM ** -0.5)).astype(BF16)
    c1 = c0 + QPAD_W
    kv4 = z[:, c1:c1 + 4 * KV_W]
    kv4_ref[0] = kv4
    kvsel_ref[0] = kv4[:, 2 * KV_W:].astype(BF16)
    c2 = c1 + 4 * KV_W
    kvw = z[:, c2:c2 + 2 * KV_W]
    kvw_ref[0] = kvw
    kvwb_ref[0] = kvw.astype(BF16)
    gate_ref[0] = jax.nn.sigmoid(z[:, c2 + 2 * KV_W:])


def _inproj(x, scale, shift, g_norm, w_pad, tm):
    nb, t, _ = x.shape
    mod_spec = _mod_spec(scale, tm)
    widths = [(4 * HG_WIDTH, F32), (QPAD_W, BF16), (4 * KV_W, F32), (2 * KV_W, F32), (GZ_PAD, F32),
              (2 * KV_W, BF16), (2 * KV_W, BF16)]
    return pl.pallas_call(
        _inproj_kernel,
        grid=(nb, t // tm),
        in_specs=[pl.BlockSpec((1, tm, D_MODEL), lambda b, i: (b, i, 0)),
                  mod_spec, mod_spec,
                  pl.BlockSpec((1, D_MODEL), lambda b, i: (0, 0)),
                  pl.BlockSpec((D_MODEL, INP_COLS), lambda b, i: (0, 0))],
        out_specs=[pl.BlockSpec((1, tm, w), lambda b, i: (b, i, 0)) for w, _ in widths],
        out_shape=[jax.ShapeDtypeStruct((nb, t, w), dt) for w, dt in widths],
        compiler_params=_cparams(("parallel", "arbitrary")),
        name="in_proj",
    )(x, scale, shift, g_norm.reshape(1, -1), w_pad)


def _pad_w_in(w_in):
    c0 = 4 * HG_WIDTH
    wq = w_in[:, c0:c0 + NSA_WIDTH].reshape(D_MODEL, NSA_HEADS, NSA_HEAD_DIM)
    zeros = jnp.zeros_like(wq)
    lo = jnp.concatenate([wq, zeros], axis=-1)
    hi = jnp.concatenate([zeros, wq], axis=-1)
    grp = (jnp.arange(NSA_HEADS) // NSA_GROUP)[None, :, None]
    wq_pad = jnp.where(grp == 0, lo, hi).reshape(D_MODEL, QPAD_W)
    c1 = c0 + NSA_WIDTH
    rest = w_in[:, c1:c1 + 6 * KV_W]
    gz = jnp.pad(w_in[:, c1 + 6 * KV_W:], ((0, 0), (0, GZ_PAD - 3 * NSA_HEADS)))
    return jnp.concatenate([w_in[:, :c0], wq_pad, rest, gz], axis=1).astype(BF16)


def _hgrn_kernel(q_ref, f_ref, v_ref, gt_ref, lb_ref, s0_ref, gn_ref, o_ref, s_out_ref, st_scr,
                 *, chunk, n_chunks):
    i = pl.program_id(2)

    @pl.when(i == 0)
    def _():
        st_scr[...] = s0_ref[0, 0].T

    lbr = lb_ref[...]
    e = jnp.exp(lbr - jnp.max(lbr, axis=0, keepdims=True))
    lb = e[0:1] / jnp.sum(e, axis=0, keepdims=True)
    row = lax.broadcasted_iota(jnp.int32, (chunk, HG_DK), 0)
    causal = (lax.broadcasted_iota(jnp.int32, (chunk, chunk), 0)
              >= lax.broadcasted_iota(jnp.int32, (chunk, chunk), 1))
    st = st_scr[...]
    for c in range(n_chunks):
        sl = pl.ds(c * chunk, chunk)
        z = f_ref[0, sl, :]
        log_f = jnp.log(lb + (1.0 - lb) * jax.nn.sigmoid(z))
        kk = (1.0 - lb) * jax.nn.sigmoid(-z)
        a = log_f
        s = 1
        while s < chunk:
            a = a + jnp.where(row >= s, pltpu.roll(a, s, 0), 0.0)
            s *= 2
        qt = (q_ref[0, sl, :] * jnp.exp(a)).astype(BF16)
        kt = (kk * jnp.exp(-a)).astype(BF16)
        v = v_ref[0, sl, :].astype(BF16)
        att = jnp.where(causal, _dot_nt(qt, kt), 0.0)
        o = _dot(att.astype(BF16), v) + _dot_nt(qt, st.astype(BF16))
        a_end = a[chunk - 1:chunk, :]
        kd = (kk * jnp.exp(a_end - a)).astype(BF16)
        st = st * jnp.exp(a_end) + _dot_tn(v, kd)
        o = _rms(o, gn_ref[...]) * _silu(gt_ref[0, sl, :])
        o_ref[0, sl, :] = o.astype(o_ref.dtype)
    st_scr[...] = st

    @pl.when(i == pl.num_programs(2) - 1)
    def _():
        s_out_ref[0, 0] = st.T


def _hgrn(hg, hg_lb, s0, g_norm, tc, chunk):
    nb, t, _ = hg.shape
    col = lambda k: (lambda b, h, i: (b, i, k * HG_HEADS + h))
    st_spec = pl.BlockSpec((1, 1, HG_DK, HG_DK), lambda b, h, i: (b, h, 0, 0))
    return pl.pallas_call(
        functools.partial(_hgrn_kernel, chunk=chunk, n_chunks=tc // chunk),
        grid=(nb, HG_HEADS, t // tc),
        in_specs=[pl.BlockSpec((1, tc, HG_DK), col(0)),
                  pl.BlockSpec((1, tc, HG_DK), col(1)),
                  pl.BlockSpec((1, tc, HG_DK), col(2)),
                  pl.BlockSpec((1, tc, HG_DK), col(3)),
                  pl.BlockSpec((hg_lb.shape[0], HG_DK), lambda b, h, i: (0, h)),
                  st_spec,
                  pl.BlockSpec((1, HG_DK), lambda b, h, i: (0, 0))],
        out_specs=[pl.BlockSpec((1, tc, HG_DK), lambda b, h, i: (b, i, h)), st_spec],
        out_shape=[jax.ShapeDtypeStruct((nb, t, HG_WIDTH), BF16),
                   jax.ShapeDtypeStruct((nb, HG_HEADS, HG_DK, HG_DK), F32)],
        scratch_shapes=[pltpu.VMEM((HG_DK, HG_DK), F32)],
        compiler_params=_cparams(("parallel", "parallel", "arbitrary")),
        name="hgrn2",
    )(hg, hg, hg, hg, hg_lb, s0, g_norm.reshape(1, -1))


def _gelu_tanh(x):
    return 0.5 * x * (1.0 + jnp.tanh(0.7978845608028654 * (x + 0.044715 * x * x * x)))


def _page_copies(pt_ref, cache_ref, buf, sem, b, s, slot, n_pages, pps, col0, tail):
    copies = []
    base = s * pps
    nxt = pt_ref[b, jnp.minimum(base + pps, n_pages - 1)]
    for br in range(2):
        cols = pl.ds(col0 + br * KV_W, KV_W)
        for i in range(pps):
            pg = pt_ref[b, base + i]
            copies.append(pltpu.make_async_copy(
                cache_ref.at[pg, :, cols],
                buf.at[slot, br, pl.ds(i * PAGE_SIZE, PAGE_SIZE), :], sem.at[slot]))
        if tail:
            copies.append(pltpu.make_async_copy(
                cache_ref.at[nxt, pl.ds(0, CMP_STRIDE), cols],
                buf.at[slot, br, pl.ds(pps * PAGE_SIZE, CMP_STRIDE), :], sem.at[slot]))
    return copies


def _stream_pages(pt_ref, cache_ref, buf, sem, n_pages, n_steps, pps, col0, tail):
    b = pl.program_id(0)
    s = pl.program_id(1)
    n = b * n_steps + s
    total = pl.num_programs(0) * n_steps
    slot = n % 2
    args = (n_pages, pps, col0, tail)

    @pl.when(n == 0)
    def _():
        for cp in _page_copies(pt_ref, cache_ref, buf, sem, b, s, slot, *args):
            cp.start()

    @pl.when(n + 1 < total)
    def _():
        n1 = n + 1
        for cp in _page_copies(pt_ref, cache_ref, buf, sem, n1 // n_steps, n1 % n_steps, 1 - slot, *args):
            cp.start()

    for cp in _page_copies(pt_ref, cache_ref, buf, sem, b, s, slot, *args):
        cp.wait()
    return slot


def _compress_kernel(pt_ref, cache_ref, pe_ref, w1_ref, b1_ref, w2_ref, kc_ref, vc_ref, buf, sem,
                     *, n_pages, n_steps, pps):
    groups = pps * PAGE_SIZE // CMP_STRIDE
    slot = _stream_pages(pt_ref, cache_ref, buf, sem, n_pages, n_steps, pps, 0, True)

    for br, out_ref in ((0, kc_ref), (1, vc_ref)):
        acc = jnp.zeros((groups, 2 * CMP_HIDDEN), F32)
        for l in range(CMP_BLOCK):
            xl = buf[slot, br, pl.ds(l, groups, stride=CMP_STRIDE), :]
            xl = (xl + pe_ref[br, l:l + 1, :]).astype(BF16)
            acc = acc + _dot(xl, w1_ref[br, l])
        hid = _gelu_tanh(acc + b1_ref[br]).astype(BF16)
        out_ref[0] = _dot(hid, w2_ref[br]).astype(out_ref.dtype)


def _compress(page_table, cache, pe2, w1cat, b1cat, w2bd, pps=PAGES_PER_STEP):
    nb, n_pages = page_table.shape
    n_steps = n_pages // pps
    groups = pps * PAGE_SIZE // CMP_STRIDE
    rows = pps * PAGE_SIZE + CMP_STRIDE
    const = lambda shape: pl.BlockSpec(shape, lambda b, s, pt: (0,) * len(shape))
    out_spec = pl.BlockSpec((1, groups, KV_W), lambda b, s, pt: (b, s, 0))
    out_sds = jax.ShapeDtypeStruct((nb, n_steps * groups, KV_W), BF16)
    return pl.pallas_call(
        functools.partial(_compress_kernel, n_pages=n_pages, n_steps=n_steps, pps=pps),
        grid_spec=pltpu.PrefetchScalarGridSpec(
            num_scalar_prefetch=1,
            grid=(nb, n_steps),
            in_specs=[pl.BlockSpec(memory_space=pl.ANY),
                      const((2, CMP_BLOCK, KV_W)),
                      const((2, CMP_BLOCK, KV_W, 2 * CMP_HIDDEN)),
                      const((2, 1, 2 * CMP_HIDDEN)),
                      const((2, 2 * CMP_HIDDEN, KV_W))],
            out_specs=[out_spec, out_spec],
            scratch_shapes=[pltpu.VMEM((2, 2, rows, KV_W), F32), pltpu.SemaphoreType.DMA((2,))]),
        out_shape=[out_sds, out_sds],
        compiler_params=_cparams(("arbitrary", "arbitrary")),
        name="nsa_compress",
    )(page_table, cache, pe2, w1cat, b1cat, w2bd)


def _compress_weights(cmp_pe, cmp_w1, cmp_b1, cmp_w2):
    pe2 = jnp.concatenate([cmp_pe, cmp_pe], axis=-1)
    w1 = cmp_w1.reshape(2, CMP_BLOCK, NSA_HEAD_DIM, CMP_HIDDEN)
    z1 = jnp.zeros_like(w1)
    w1cat = jnp.concatenate([jnp.concatenate([w1, z1], axis=-1),
                             jnp.concatenate([z1, w1], axis=-1)], axis=2).astype(BF16)
    b1cat = jnp.concatenate([cmp_b1, cmp_b1], axis=-1)[:, None, :]
    z2 = jnp.zeros_like(cmp_w2)
    w2bd = jnp.concatenate([jnp.concatenate([cmp_w2, z2], axis=-1),
                            jnp.concatenate([z2, cmp_w2], axis=-1)], axis=1).astype(BF16)
    return pe2, w1cat, b1cat, w2bd


def _overlap_matrix(n_cmp, n_sel):
    cs = lax.broadcasted_iota(jnp.int32, (n_cmp, n_sel), 0) * CMP_STRIDE
    ss = lax.broadcasted_iota(jnp.int32, (n_cmp, n_sel), 1) * SEL_BLOCK
    return ((cs < ss + SEL_BLOCK) & (cs + CMP_BLOCK > ss)).astype(BF16)


SEL_CHUNK = 256
WIN_SPAN = WINDOW + Q_BLOCK


def _nsa_prompt_kernel(q_ref, gt_ref, kc_ref, vc_ref, ks_ref, vs_ref, kw_ref, vw_ref, o_ref,
                       m_scr, l_scr, acc_scr, *, n_cmp, n_sel):
    j = pl.program_id(1)
    q0 = j * Q_BLOCK
    tok = lax.broadcasted_iota(jnp.int32, (Q_BLOCK, 1), 0) + q0
    tok4 = jnp.concatenate([tok] * NSA_GROUP, axis=0)
    gates = gt_ref[0]
    ov = _overlap_matrix(n_cmp, n_sel)
    blk = lax.broadcasted_iota(jnp.int32, (1, n_sel), 1)
    cur = lax.shift_right_logical(tok, 6)
    forced = (blk == 0) | (blk == cur) | (blk == cur - 1)
    allowed = blk * SEL_BLOCK <= tok
    cend = lax.broadcasted_iota(jnp.int32, (1, n_cmp), 1) * CMP_STRIDE + (CMP_BLOCK - 1)
    n_chunks = j // (SEL_CHUNK // Q_BLOCK) + 1
    ws = pl.multiple_of(jnp.maximum(q0 - WINDOW, 0), Q_BLOCK)
    wpos = ws + lax.broadcasted_iota(jnp.int32, (1, WIN_SPAN), 1)
    blk_of_key = lax.shift_right_logical(lax.broadcasted_iota(jnp.int32, (n_sel, SEL_CHUNK), 1), 6)
    blk_row = lax.broadcasted_iota(jnp.int32, (n_sel, SEL_CHUNK), 0)
    key_in_chunk = lax.broadcasted_iota(jnp.int32, (1, SEL_CHUNK), 1)

    for g in range(NSA_KV_HEADS):
        q = jnp.concatenate([q_ref[0, :, (NSA_GROUP * g + jh) * LANES:(NSA_GROUP * g + jh + 1) * LANES]
                             for jh in range(NSA_GROUP)], axis=0)
        p_c = _masked_softmax(_dot_nt(q, kc_ref[0]), cend <= tok4)
        o_c = _dot(p_c.astype(BF16), vc_ref[0])
        psum = (p_c[0:Q_BLOCK] + p_c[Q_BLOCK:2 * Q_BLOCK]
                + p_c[2 * Q_BLOCK:3 * Q_BLOCK] + p_c[3 * Q_BLOCK:4 * Q_BLOCK])
        imp = _dot(psum.astype(BF16), ov)
        sel = _topk_mask(jnp.where(forced, -NEG, jnp.where(allowed, imp, NEG)), SEL_TOPN).astype(BF16)

        m_scr[...] = jnp.full(m_scr.shape, NEG, F32)
        l_scr[...] = jnp.zeros(l_scr.shape, F32)
        acc_scr[...] = jnp.zeros(acc_scr.shape, F32)

        def body(c, carry):
            k0 = pl.multiple_of(c * SEL_CHUNK, SEL_CHUNK)
            expand = (blk_row == blk_of_key + c * (SEL_CHUNK // SEL_BLOCK)).astype(BF16)
            chosen = _dot(sel, expand)
            msk = (chosen > 0.5) & (key_in_chunk + k0 <= tok)
            msk4 = jnp.concatenate([msk] * NSA_GROUP, axis=0)
            s = jnp.where(msk4, _dot_nt(q, ks_ref[0, pl.ds(k0, SEL_CHUNK), :]), NEG)
            m_old = m_scr[...]
            m_new = jnp.maximum(m_old, jnp.max(s, axis=1, keepdims=True))
            p = jnp.exp(s - m_new) * msk4.astype(F32)
            alpha = jnp.exp(m_old - m_new)
            l_scr[...] = alpha * l_scr[...] + jnp.sum(p, axis=1, keepdims=True)
            acc_scr[...] = alpha * acc_scr[...] + _dot(p.astype(BF16), vs_ref[0, pl.ds(k0, SEL_CHUNK), :])
            m_scr[...] = m_new
            return carry

        lax.fori_loop(0, n_chunks, body, 0)
        o_s = acc_scr[...] / jnp.maximum(l_scr[...], 1e-30)

        d = tok4 - wpos
        p_w = _masked_softmax(_dot_nt(q, kw_ref[0, pl.ds(ws, WIN_SPAN), :]), (d >= 0) & (d < WINDOW))
        o_w = _dot(p_w.astype(BF16), vw_ref[0, pl.ds(ws, WIN_SPAN), :])

        for jh in range(NSA_GROUP):
            h = NSA_GROUP * g + jh
            r = slice(jh * Q_BLOCK, (jh + 1) * Q_BLOCK)
            o = (gates[:, 3 * h:3 * h + 1] * o_c[r] + gates[:, 3 * h + 1:3 * h + 2] * o_s[r]
                 + gates[:, 3 * h + 2:3 * h + 3] * o_w[r])
            o_ref[0, :, h * LANES:(h + 1) * LANES] = o.astype(o_ref.dtype)


def _nsa_prompt(qpad, gates, kc, vc, kvsel, kvwb):
    nb, t, _ = qpad.shape
    n_cmp = kc.shape[1]
    n_sel = t // SEL_BLOCK
    full = lambda w, k: pl.BlockSpec((1, t, w), lambda b, j: (b, 0, k))
    return pl.pallas_call(
        functools.partial(_nsa_prompt_kernel, n_cmp=n_cmp, n_sel=n_sel),
        grid=(nb, t // Q_BLOCK),
        in_specs=[pl.BlockSpec((1, Q_BLOCK, QPAD_W), lambda b, j: (b, j, 0)),
                  pl.BlockSpec((1, Q_BLOCK, GZ_PAD), lambda b, j: (b, j, 0)),
                  pl.BlockSpec((1, n_cmp, KV_W), lambda b, j: (b, 0, 0)),
                  pl.BlockSpec((1, n_cmp, KV_W), lambda b, j: (b, 0, 0)),
                  full(KV_W, 0), full(KV_W, 1), full(KV_W, 0), full(KV_W, 1)],
        out_specs=pl.BlockSpec((1, Q_BLOCK, QPAD_W), lambda b, j: (b, j, 0)),
        out_shape=jax.ShapeDtypeStruct((nb, t, QPAD_W), BF16),
        scratch_shapes=[pltpu.VMEM((NSA_GROUP * Q_BLOCK, 1), F32),
                        pltpu.VMEM((NSA_GROUP * Q_BLOCK, 1), F32),
                        pltpu.VMEM((NSA_GROUP * Q_BLOCK, KV_W), F32)],
        compiler_params=_cparams(("parallel", "arbitrary")),
        name="nsa_prompt",
    )(qpad, gates, kc, vc, kvsel, kvsel, kvwb, kvwb)


def _nsa_sample_a_kernel(q_ref, g_ref, kc_ref, vc_ref, wb_ref, nw_ref, ocw_ref, sel_ref,
                         *, past_len, n_tok, n_sel, n_sel_pad):
    q = q_ref[0]
    rows = q.shape[0]
    n_cmp = kc_ref.shape[1]
    t_row = lax.broadcasted_iota(jnp.int32, (rows, 1), 0) & (n_tok - 1)
    qpos = past_len + t_row
    cend = lax.broadcasted_iota(jnp.int32, (1, n_cmp), 1) * CMP_STRIDE + (CMP_BLOCK - 1)
    p_c = _masked_softmax(_dot_nt(q, kc_ref[0]), cend <= qpos)
    o_c = _dot(p_c.astype(BF16), vc_ref[0])

    per_grp = NSA_GROUP * n_tok
    psum = jnp.concatenate(
        [sum(p_c[g * per_grp + jh * n_tok:g * per_grp + (jh + 1) * n_tok] for jh in range(NSA_GROUP))
         for g in range(NSA_KV_HEADS)], axis=0)
    imp = _dot(psum.astype(BF16), _overlap_matrix(n_cmp, n_sel_pad))
    blk = lax.broadcasted_iota(jnp.int32, (1, n_sel_pad), 1)
    tq = past_len + (lax.broadcasted_iota(jnp.int32, (NSA_KV_HEADS * n_tok, 1), 0) & (n_tok - 1))
    cur = lax.shift_right_logical(tq, 6)
    forced = (blk == 0) | (blk == cur) | (blk == cur - 1)
    allowed = blk * SEL_BLOCK <= tq
    v = jnp.where(forced, -NEG, jnp.where(allowed, imp, NEG))
    sel_ref[0] = _topk_mask(jnp.where(blk < n_sel, v, 2.0 * NEG), SEL_TOPN)

    wb = wb_ref.shape[1]
    kw = wb_ref[0, :, 0:KV_W].astype(BF16)
    vw = wb_ref[0, :, KV_W:2 * KV_W].astype(BF16)
    kn = nw_ref[0, :, 0:KV_W].astype(BF16)
    vn = nw_ref[0, :, KV_W:2 * KV_W].astype(BF16)
    i1 = lax.broadcasted_iota(jnp.int32, (1, wb), 1)
    d1 = t_row + wb - i1
    valid1 = (d1 >= 0) & (d1 < WINDOW) & (past_len - wb + i1 >= 0)
    i2 = lax.broadcasted_iota(jnp.int32, (1, nw_ref.shape[1]), 1)
    d2 = t_row - i2
    valid2 = (d2 >= 0) & (d2 < WINDOW) & (i2 < n_tok)
    s1 = jnp.where(valid1, _dot_nt(q, kw), NEG)
    s2 = jnp.where(valid2, _dot_nt(q, kn), NEG)
    m = jnp.maximum(jnp.max(s1, axis=1, keepdims=True), jnp.max(s2, axis=1, keepdims=True))
    p1 = jnp.exp(s1 - m) * valid1.astype(F32)
    p2 = jnp.exp(s2 - m) * valid2.astype(F32)
    den = jnp.maximum(jnp.sum(p1, axis=1, keepdims=True) + jnp.sum(p2, axis=1, keepdims=True), 1e-30)
    o_w = (_dot(p1.astype(BF16), vw) + _dot(p2.astype(BF16), vn)) / den
    g = g_ref[0]
    ocw_ref[0] = g[:, 0:1] * o_c + g[:, 2:3] * o_w


def _nsa_sample_a(q_rows, g_rows, kc, vc, win_buf, new_win, *, past_len, n_tok):
    nb, rows, _ = q_rows.shape
    n_sel = -(-(past_len + n_tok) // SEL_BLOCK)
    n_sel_pad = -(-n_sel // LANES) * LANES
    blk3 = lambda a: pl.BlockSpec((1,) + a.shape[1:], lambda b: (b, 0, 0))
    return pl.pallas_call(
        functools.partial(_nsa_sample_a_kernel, past_len=past_len, n_tok=n_tok, n_sel=n_sel, n_sel_pad=n_sel_pad),
        grid=(nb,),
        in_specs=[blk3(q_rows), blk3(g_rows), blk3(kc), blk3(vc), blk3(win_buf), blk3(new_win)],
        out_specs=[pl.BlockSpec((1, rows, KV_W), lambda b: (b, 0, 0)),
                   pl.BlockSpec((1, NSA_KV_HEADS * n_tok, n_sel_pad), lambda b: (b, 0, 0))],
        out_shape=[jax.ShapeDtypeStruct((nb, rows, KV_W), F32),
                   jax.ShapeDtypeStruct((nb, NSA_KV_HEADS * n_tok, n_sel_pad), F32)],
        compiler_params=_cparams(("parallel",)),
        name="nsa_sample_a",
    )(q_rows, g_rows, kc, vc, win_buf, new_win)


def _nsa_sample_b_kernel(pt_ref, cache_ref, q_ref, g_ref, sel_ref, seln_ref, ns_ref, ocw_ref, ex_ref, o_ref,
                         buf, sem, m_scr, l_scr, acc_scr, *, n_pages, n_steps, pps, n_tok):
    s = pl.program_id(1)
    slot = _stream_pages(pt_ref, cache_ref, buf, sem, n_pages, n_steps, pps, 2 * KV_W, False)
    q = q_ref[0]
    rows = q.shape[0]

    @pl.when(s == 0)
    def _():
        m_scr[...] = jnp.full(m_scr.shape, NEG, F32)
        l_scr[...] = jnp.zeros(l_scr.shape, F32)
        acc_scr[...] = jnp.zeros(acc_scr.shape, F32)

    def update(scores, msk, v):
        sc = jnp.where(msk, scores, NEG)
        m_old = m_scr[...]
        m_new = jnp.maximum(m_old, jnp.max(sc, axis=1, keepdims=True))
        p = jnp.exp(sc - m_new) * msk.astype(F32)
        alpha = jnp.exp(m_old - m_new)
        l_scr[...] = alpha * l_scr[...] + jnp.sum(p, axis=1, keepdims=True)
        acc_scr[...] = alpha * acc_scr[...] + _dot(p.astype(BF16), v)
        m_scr[...] = m_new

    chosen = _dot(sel_ref[0, 0], ex_ref[...]) > 0.5
    update(_dot_nt(q, buf[slot, 0].astype(BF16)), chosen, buf[slot, 1].astype(BF16))

    @pl.when(s == n_steps - 1)
    def _():
        t_row = lax.broadcasted_iota(jnp.int32, (rows, 1), 0) & (n_tok - 1)
        i2 = lax.broadcasted_iota(jnp.int32, (1, ns_ref.shape[1]), 1)
        msk = (seln_ref[0, 0][:, 0:1] > 0.5) & (i2 <= t_row) & (i2 < n_tok)
        update(_dot_nt(q, ns_ref[0, :, 0:KV_W].astype(BF16)), msk, ns_ref[0, :, KV_W:2 * KV_W].astype(BF16))
        o_s = acc_scr[...] / jnp.maximum(l_scr[...], 1e-30)
        o_ref[0] = ocw_ref[0] + g_ref[0][:, 1:2] * o_s


def _nsa_sample_b(page_table, cache, q_rows, g_rows, sel_steps, new_sel, ocw, *, n_tok, pps=PAGES_PER_STEP):
    nb, n_pages = page_table.shape
    n_steps = n_pages // pps
    rows = q_rows.shape[1]
    keys = pps * PAGE_SIZE
    expand = jnp.asarray(np.arange(LANES)[:, None] == (np.arange(keys)[None, :] // SEL_BLOCK), BF16)
    per_b = lambda a: pl.BlockSpec((1,) + a.shape[1:], lambda b, s, pt: (b, 0, 0))
    return pl.pallas_call(
        functools.partial(_nsa_sample_b_kernel, n_pages=n_pages, n_steps=n_steps, pps=pps, n_tok=n_tok),
        grid_spec=pltpu.PrefetchScalarGridSpec(
            num_scalar_prefetch=1,
            grid=(nb, n_steps),
            in_specs=[pl.BlockSpec(memory_space=pl.ANY),
                      per_b(q_rows), per_b(g_rows),
                      pl.BlockSpec((1, 1, rows, LANES), lambda b, s, pt: (b, s, 0, 0)),
                      pl.BlockSpec((1, 1, rows, LANES), lambda b, s, pt: (b, n_steps, 0, 0)),
                      per_b(new_sel), per_b(ocw),
                      pl.BlockSpec((LANES, keys), lambda b, s, pt: (0, 0))],
            out_specs=pl.BlockSpec((1, rows, KV_W), lambda b, s, pt: (b, 0, 0)),
            scratch_shapes=[pltpu.VMEM((2, 2, keys, KV_W), F32), pltpu.SemaphoreType.DMA((2,)),
                            pltpu.VMEM((rows, 1), F32), pltpu.VMEM((rows, 1), F32),
                            pltpu.VMEM((rows, KV_W), F32)]),
        out_shape=jax.ShapeDtypeStruct((nb, rows, KV_W), F32),
        compiler_params=_cparams(("arbitrary", "arbitrary")),
        name="nsa_sample_b",
    )(page_table, cache, q_rows, g_rows, sel_steps, sel_steps, new_sel, ocw, expand)


def _nsa_sample(page_table, cache, win_buf, cw, qpad, gates, kv4, kvw, pps=PAGES_PER_STEP):
    nb, ts, _ = qpad.shape
    past_len = page_table.shape[1] * PAGE_SIZE
    kc, vc = _compress(page_table, cache, *cw, pps=pps)
    rows = NSA_HEADS * ts
    q_rows = qpad.reshape(nb, ts, NSA_HEADS, LANES).transpose(0, 2, 1, 3).reshape(nb, rows, LANES)
    g_rows = gates[:, :, :3 * NSA_HEADS].reshape(nb, ts, NSA_HEADS, 3).transpose(0, 2, 1, 3)
    g_rows = jnp.pad(g_rows.reshape(nb, rows, 3), ((0, 0), (0, 0), (0, LANES - 3)))
    pad_rows = lambda a: jnp.pad(a, ((0, 0), (0, LANES - ts), (0, 0)))
    new_win = pad_rows(kvw)
    new_sel = pad_rows(kv4[:, :, 2 * KV_W:])
    ocw, sel = _nsa_sample_a(q_rows, g_rows, kc, vc, win_buf, new_win, past_len=past_len, n_tok=ts)
    n_steps = page_table.shape[1] // pps
    blk_per_step = pps * PAGE_SIZE // SEL_BLOCK
    n_past_blk = n_steps * blk_per_step
    sel_past = sel[:, :, :n_past_blk].reshape(nb, NSA_KV_HEADS, 1, ts, n_steps, blk_per_step)
    sel_past = jnp.broadcast_to(sel_past, (nb, NSA_KV_HEADS, NSA_GROUP, ts, n_steps, blk_per_step))
    sel_past = sel_past.transpose(0, 4, 1, 2, 3, 5).reshape(nb, n_steps, rows, blk_per_step)
    sel_past = jnp.pad(sel_past, ((0, 0), (0, 0), (0, 0), (0, LANES - blk_per_step)))
    sel_new = jnp.pad(sel[:, :, n_past_blk:], ((0, 0), (0, 0), (0, LANES)))[:, :, :LANES]
    sel_new = sel_new.reshape(nb, NSA_KV_HEADS, 1, ts, LANES)
    sel_new = jnp.broadcast_to(sel_new, (nb, NSA_KV_HEADS, NSA_GROUP, ts, LANES)).reshape(nb, 1, rows, LANES)
    sel_steps = jnp.concatenate([sel_past, sel_new], axis=1).astype(BF16)
    o_rows = _nsa_sample_b(page_table, cache, q_rows, g_rows, sel_steps, new_sel, ocw, n_tok=ts, pps=pps)
    return o_rows.reshape(nb, NSA_HEADS, ts, LANES).transpose(0, 2, 1, 3).reshape(nb, ts, QPAD_W).astype(BF16)


def _outproj_kernel(x_ref, hg_ref, nsa_ref, g1_ref, sc2_ref, sh2_ref, fn_ref, wo1_ref, wo2_ref,
                    x1_ref, h2_ref, *, tm):
    mix = _dot(hg_ref[0], wo1_ref[...]) + _dot(nsa_ref[0], wo2_ref[...])
    x1 = x_ref[0] + g1_ref[0] * mix
    x1_ref[0] = x1
    h2 = _rms(x1, fn_ref[...]) * (1.0 + sc2_ref[0]) + sh2_ref[0]
    _store_tok_tiles(h2_ref, h2, tm)


def _outproj(x, hg_out, nsa, gate1, scale2, shift2, ffn_norm, wo_hg, wo_nsa, tm):
    nb, t, _ = x.shape
    nt = t // tm
    mod_spec = _mod_spec(gate1, tm)
    tile = lambda w: pl.BlockSpec((1, tm, w), lambda b, i: (b, i, 0))
    return pl.pallas_call(
        functools.partial(_outproj_kernel, tm=tm),
        grid=(nb, nt),
        in_specs=[tile(D_MODEL), tile(HG_WIDTH), tile(QPAD_W), mod_spec, mod_spec, mod_spec,
                  pl.BlockSpec((1, D_MODEL), lambda b, i: (0, 0)),
                  pl.BlockSpec((HG_WIDTH, D_MODEL), lambda b, i: (0, 0)),
                  pl.BlockSpec((QPAD_W, D_MODEL), lambda b, i: (0, 0))],
        out_specs=[tile(D_MODEL), pl.BlockSpec((tm * TOK_ROWS, LANES), lambda b, i: (b * nt + i, 0))],
        out_shape=[jax.ShapeDtypeStruct((nb, t, D_MODEL), F32),
                   jax.ShapeDtypeStruct((nb * t * TOK_ROWS, LANES), F32)],
        compiler_params=_cparams(("parallel", "arbitrary")),
        name="out_proj",
    )(x, hg_out, nsa, gate1, scale2, shift2, ffn_norm.reshape(1, -1), wo_hg, wo_nsa)


def _split_w_out(w_out):
    wo_hg = w_out[:HG_WIDTH].astype(BF16)
    wn = w_out[HG_WIDTH:].reshape(NSA_HEADS, NSA_HEAD_DIM, D_MODEL)
    z = jnp.zeros_like(wn)
    grp = (jnp.arange(NSA_HEADS) // NSA_GROUP)[:, None, None]
    wn_pad = jnp.where(grp == 0, jnp.concatenate([wn, z], axis=1), jnp.concatenate([z, wn], axis=1))
    return wo_hg, wn_pad.reshape(QPAD_W, D_MODEL).astype(BF16)


def _router_kernel(h_ref, wr_ref, b_ref, e_ref, w_ref, r_ref, cnt_ref, run_scr, *, tm):
    @pl.when(pl.program_id(0) == 0)
    def _():
        run_scr[...] = jnp.zeros(run_scr.shape, F32)

    x = _load_tok_tiles(h_ref, tm).astype(BF16)
    scores = jax.nn.sigmoid(_dot(x, wr_ref[...]))
    biased = scores + b_ref[...]
    lane_i = lax.broadcasted_iota(jnp.int32, (tm, N_EXPERTS), 1)
    lane = lane_i.astype(F32)
    grp_of_lane = lax.shift_right_logical(lane_i, 5)
    per_group = N_EXPERTS // N_GROUPS

    gcol = lax.broadcasted_iota(jnp.int32, (tm, LANES), 1)
    gs = jnp.full((tm, LANES), 2.0 * NEG, F32)
    for g in range(N_GROUPS):
        mg = jnp.where(grp_of_lane == g, biased, NEG)
        m1 = jnp.max(mg, axis=1, keepdims=True)
        i1 = jnp.min(jnp.where(mg == m1, lane, 1e9), axis=1, keepdims=True)
        m2 = jnp.max(jnp.where(lane == i1, NEG, mg), axis=1, keepdims=True)
        gs = jnp.where(gcol == g, m1 + m2, gs)
    gsel = _topk_mask(gs, TOPK_GROUPS).astype(BF16)
    spread = (lax.broadcasted_iota(jnp.int32, (LANES, N_EXPERTS), 0)
              == lax.shift_right_logical(lax.broadcasted_iota(jnp.int32, (LANES, N_EXPERTS), 1), 5)).astype(BF16)
    v = jnp.where(_dot(gsel, spread) > 0.5, biased, NEG)

    onehot = jnp.zeros((tm, N_EXPERTS), F32)
    idxs, wts = [], []
    wsum = jnp.zeros((tm, 1), F32)
    for _ in range(TOP_K):
        m = jnp.max(v, axis=1, keepdims=True)
        idx = jnp.min(jnp.where(v == m, lane, 1e9), axis=1, keepdims=True)
        pick = lane == idx
        wk = jnp.sum(jnp.where(pick, scores, 0.0), axis=1, keepdims=True)
        onehot = jnp.where(pick, 1.0, onehot)
        v = jnp.where(pick, 3.0 * NEG, v)
        idxs.append(idx)
        wts.append(wk)
        wsum = wsum + wk

    earlier = (lax.broadcasted_iota(jnp.int32, (tm, tm), 0) > lax.broadcasted_iota(jnp.int32, (tm, tm), 1))
    before = _dot(earlier.astype(BF16), onehot.astype(BF16)) + run_scr[...]
    e_out = jnp.zeros((tm, LANES), jnp.int32)
    r_out = jnp.zeros((tm, LANES), jnp.int32)
    w_out = jnp.zeros((tm, LANES), F32)
    for k in range(TOP_K):
        rk = jnp.sum(jnp.where(lane == idxs[k], before, 0.0), axis=1, keepdims=True)
        e_out = jnp.where(gcol == k, idxs[k].astype(jnp.int32), e_out)
        r_out = jnp.where(gcol == k, rk.astype(jnp.int32), r_out)
        w_out = jnp.where(gcol == k, wts[k] / wsum * ROUTED_SCALE, w_out)
    e_ref[...] = e_out
    r_ref[...] = r_out
    w_ref[...] = w_out
    run_scr[...] = run_scr[...] + jnp.sum(onehot, axis=0, keepdims=True)
    cnt_ref[...] = run_scr[...]


def _router(h2, w_router, bias, n_tok, tm=ROUTER_TM):
    tile = pl.BlockSpec((tm, LANES), lambda i: (i, 0))
    return pl.pallas_call(
        functools.partial(_router_kernel, tm=tm),
        grid=(n_tok // tm,),
        in_specs=[pl.BlockSpec((tm * TOK_ROWS, LANES), lambda i: (i, 0)),
                  pl.BlockSpec((D_MODEL, N_EXPERTS), lambda i: (0, 0)),
                  pl.BlockSpec((1, N_EXPERTS), lambda i: (0, 0))],
        out_specs=[tile, tile, tile, pl.BlockSpec((1, N_EXPERTS), lambda i: (0, 0))],
        out_shape=[jax.ShapeDtypeStruct((n_tok, LANES), jnp.int32),
                   jax.ShapeDtypeStruct((n_tok, LANES), F32),
                   jax.ShapeDtypeStruct((n_tok, LANES), jnp.int32),
                   jax.ShapeDtypeStruct((1, N_EXPERTS), F32)],
        scratch_shapes=[pltpu.VMEM((1, N_EXPERTS), F32)],
        compiler_params=_cparams(("arbitrary",)),
        name="moe_router",
    )(h2, w_router, bias)


def _moe_items(counts, starts, ends, n_blocks):
    first_blk = starts // MOE_BM
    last_blk = (ends - 1) // MOE_BM
    n_it = jnp.where(counts > 0, last_blk - first_blk + 1, 0)
    it_end = jnp.cumsum(n_it)
    it_start = it_end - n_it
    total = it_end[-1]
    ii = jnp.arange(n_blocks + N_EXPERTS - 1, dtype=jnp.int32)
    valid = ii < total
    e_of = jnp.searchsorted(it_end, jnp.minimum(ii, total - 1), side="right").astype(jnp.int32)
    e_of = jnp.minimum(e_of, N_EXPERTS - 1)
    blk_of = jnp.where(valid, first_blk[e_of] + ii - it_start[e_of], n_blocks - 1).astype(jnp.int32)
    prev = jnp.concatenate([jnp.full((1,), -1, jnp.int32), blk_of[:-1]])
    first = (valid & (blk_of != prev)).astype(jnp.int32)
    return blk_of, e_of, first, valid.astype(jnp.int32)


def _dispatch_kernel(dest_ref, h_ref, xs_ref, sem, *, tm):
    i = pl.program_id(0)
    n_pairs = tm * TOP_K

    def row_copy(src_tok, dst_row):
        return pltpu.make_async_copy(
            h_ref.at[pl.ds(pl.multiple_of(src_tok * TOK_ROWS, TOK_ROWS), TOK_ROWS), :],
            xs_ref.at[pl.ds(pl.multiple_of(dst_row * TOK_ROWS, TOK_ROWS), TOK_ROWS), :], sem)

    def issue(p, carry):
        row_copy(i * tm + lax.shift_right_logical(p, 3), dest_ref[0, 0, p]).start()
        return carry

    def drain(p, carry):
        row_copy(0, 0).wait()
        return carry

    lax.fori_loop(0, n_pairs, issue, 0)
    lax.fori_loop(0, n_pairs, drain, 0)


def _dispatch(dest_tiles, h2, n_pairs, tm=ROUTER_TM):
    n_tiles = dest_tiles.shape[0]
    return pl.pallas_call(
        functools.partial(_dispatch_kernel, tm=tm),
        grid=(n_tiles,),
        in_specs=[pl.BlockSpec((1, 1, tm * TOP_K), lambda i: (i, 0, 0), memory_space=pltpu.SMEM),
                  pl.BlockSpec(memory_space=pl.ANY)],
        out_specs=pl.BlockSpec(memory_space=pl.ANY),
        out_shape=jax.ShapeDtypeStruct((n_pairs * TOK_ROWS, LANES), F32),
        scratch_shapes=[pltpu.SemaphoreType.DMA(())],
        compiler_params=_cparams(("arbitrary",)),
        name="moe_dispatch",
    )(dest_tiles, h2)


def _gmm_kernel(blk_ref, e_ref, first_ref, valid_ref, st_ref, en_ref, xs_ref, wg_ref, wu_ref, wd_ref, ys_ref,
                *, bm):
    i = pl.program_id(0)

    @pl.when(valid_ref[i] == 1)
    def _():
        x = _load_tok_tiles(xs_ref, bm).astype(BF16)
        hid = (_silu(_dot(x, wg_ref[0].astype(BF16))) * _dot(x, wu_ref[0].astype(BF16))).astype(BF16)
        y = _dot(hid, wd_ref[0].astype(BF16))
        e = e_ref[i]
        row = blk_ref[i] * bm + lax.broadcasted_iota(jnp.int32, (bm, 1), 0)
        mine = (row >= st_ref[e]) & (row < en_ref[e])

        @pl.when(first_ref[i] == 1)
        def _():
            _store_tok_tiles(ys_ref, jnp.where(mine, y, 0.0), bm)

        @pl.when(first_ref[i] == 0)
        def _():
            _store_tok_tiles(ys_ref, jnp.where(mine, y, _load_tok_tiles(ys_ref, bm)), bm)


def _moe_gmm(items, starts, ends, xs_sorted, w_gate, w_up, w_down, n_blocks, bm=MOE_BM):
    blk_of, e_of, first, valid = items
    rows = pl.BlockSpec((bm * TOK_ROWS, LANES), lambda i, blk, e, f, v, st, en: (blk[i], 0))
    wspec = lambda a: pl.BlockSpec((1,) + a.shape[1:], lambda i, blk, e, f, v, st, en: (e[i], 0, 0))
    return pl.pallas_call(
        functools.partial(_gmm_kernel, bm=bm),
        grid_spec=pltpu.PrefetchScalarGridSpec(
            num_scalar_prefetch=6,
            grid=(blk_of.shape[0],),
            in_specs=[rows, wspec(w_gate), wspec(w_up), wspec(w_down)],
            out_specs=rows),
        out_shape=jax.ShapeDtypeStruct(xs_sorted.shape, F32),
        compiler_params=_cparams(("arbitrary",)),
        name="moe_experts",
    )(blk_of, e_of, first, valid, starts.astype(jnp.int32), ends.astype(jnp.int32), xs_sorted, w_gate, w_up, w_down)


def _combine_kernel(dest_ref, w_ref, x1_ref, h_ref, g2_ref, wsg_ref, wsu_ref, wsd_ref, fn_ref, ys_ref, o_ref,
                    gbuf, sem, *, tm):
    n_pairs = tm * TOP_K

    def row_copy(src_row, p):
        return pltpu.make_async_copy(
            ys_ref.at[pl.ds(pl.multiple_of(src_row * TOK_ROWS, TOK_ROWS), TOK_ROWS), :],
            gbuf.at[pl.ds(pl.multiple_of(p * TOK_ROWS, TOK_ROWS), TOK_ROWS), :], sem)

    def issue(p, carry):
        row_copy(dest_ref[0, 0, p], p).start()
        return carry

    def drain(p, carry):
        row_copy(0, p).wait()
        return carry

    lax.fori_loop(0, n_pairs, issue, 0)
    h = _load_tok_tiles(h_ref, tm).astype(BF16)
    hid = (_silu(_dot(h, wsg_ref[...])) * _dot(h, wsu_ref[...])).astype(BF16)
    shared = _dot(hid, wsd_ref[...])
    lax.fori_loop(0, n_pairs, drain, 0)

    w = w_ref[...]
    stride = TOP_K * TOK_ROWS
    slabs = []
    for s in range(TOK_ROWS):
        acc = jnp.zeros((tm, LANES), F32)
        for k in range(TOP_K):
            acc = acc + w[:, k:k + 1] * gbuf[pl.ds(k * TOK_ROWS + s, tm, stride=stride), :]
        slabs.append(acc)
    routed = jnp.concatenate(slabs, axis=1)
    x2 = x1_ref[0] + g2_ref[0] * (routed + shared)
    o_ref[0] = _rms(x2, fn_ref[...])


def _combine(dest_tiles, top_w, x1, h2, gate2, shared, fnorm, ys_sorted, tile0, tm=COMBINE_TM):
    nb, t, _ = x1.shape
    nt = t // tm
    flat = lambda b, i: tile0 + b * nt + i
    mod_spec = _mod_spec(gate2, tm)
    const = lambda a: pl.BlockSpec(a.shape, lambda b, i: (0, 0))
    return pl.pallas_call(
        functools.partial(_combine_kernel, tm=tm),
        grid=(nb, nt),
        in_specs=[pl.BlockSpec((1, 1, tm * TOP_K), lambda b, i: (flat(b, i), 0, 0), memory_space=pltpu.SMEM),
                  pl.BlockSpec((tm, LANES), lambda b, i: (flat(b, i), 0)),
                  pl.BlockSpec((1, tm, D_MODEL), lambda b, i: (b, i, 0)),
                  pl.BlockSpec((tm * TOK_ROWS, LANES), lambda b, i: (flat(b, i), 0)),
                  mod_spec, const(shared[0]), const(shared[1]), const(shared[2]), const(fnorm),
                  pl.BlockSpec(memory_space=pl.ANY)],
        out_specs=pl.BlockSpec((1, tm, D_MODEL), lambda b, i: (b, i, 0)),
        out_shape=jax.ShapeDtypeStruct((nb, t, D_MODEL), F32),
        scratch_shapes=[pltpu.VMEM((tm * TOP_K * TOK_ROWS, LANES), F32), pltpu.SemaphoreType.DMA(())],
        compiler_params=_cparams(("arbitrary", "arbitrary")),
        name="moe_combine",
    )(dest_tiles, top_w, x1, h2, gate2, *shared, fnorm, ys_sorted)


def kernel(x_prompt, x_sample, c_prompt, c_sample, cache_nsa_kv, cache_win_kv, state_hgrn, page_table,
           attn_norm, ffn_norm, final_norm, hg_norm, w_ada, b_ada, w_in, hg_lb,
           cmp_pe, cmp_w1, cmp_b1, cmp_w2, w_out, w_router, router_bias,
           w_gate, w_up, w_down, ws_gate, ws_up, ws_down):
    nbp, t, _ = x_prompt.shape
    nbs, ts, _ = x_sample.shape
    ns = nbs * ts
    n_all = nbp * t + ns
    past_len = page_table.shape[1] * PAGE_SIZE

    c_all = jnp.concatenate([c_prompt, c_sample], axis=0)
    c_all = jnp.pad(c_all, ((0, -c_all.shape[0] % SUBLANES), (0, 0)))
    mod = _ada(c_all, w_ada[0], b_ada[0])
    modp = mod[:nbp].reshape(nbp, 1, 6, D_MODEL)
    mods = jnp.repeat(mod[nbp:nbp + nbs].reshape(nbs, 1, 6, D_MODEL), ts, axis=1).reshape(1, ns, 6, D_MODEL)

    w_pad = _pad_w_in(w_in[0])
    cw = _compress_weights(cmp_pe[0], cmp_w1[0], cmp_b1[0], cmp_w2[0])
    wo_hg, wo_nsa = _split_w_out(w_out[0])

    hg, qpad, kv4, kvw, gates, kvsel, kvwb = _inproj(
        x_prompt, modp[:, :, 1], modp[:, :, 0], attn_norm[0], w_pad, 512)
    hg_out, hg_state_p = _hgrn(hg, hg_lb, jnp.zeros((nbp, HG_HEADS, HG_DK, HG_DK), F32), hg_norm[0],
                               256, HG_CHUNK)
    n_pages_p = t // PAGE_SIZE
    ptp = jnp.arange(nbp * n_pages_p, dtype=jnp.int32).reshape(nbp, n_pages_p)
    kc, vc = _compress(ptp, kv4.reshape(nbp * n_pages_p, PAGE_SIZE, 4 * KV_W), *cw)
    nsa = _nsa_prompt(qpad, gates, kc, vc, kvsel, kvwb)
    x1p, h2p = _outproj(x_prompt, hg_out, nsa, modp[:, :, 2], modp[:, :, 4], modp[:, :, 3],
                        ffn_norm[0], wo_hg, wo_nsa, 512)

    xs = x_sample.reshape(1, ns, D_MODEL)
    hg_s, qpad_s, kv4_s, kvw_s, gates_s, _, _ = _inproj(
        xs, mods[:, :, 1], mods[:, :, 0], attn_norm[0], w_pad, ns)
    hg_out_s, hg_state_s = _hgrn(hg_s.reshape(nbs, ts, 4 * HG_WIDTH), hg_lb, state_hgrn[0], hg_norm[0], ts, ts)
    cache = cache_nsa_kv[0].reshape(-1, PAGE_SIZE, 4 * KV_W)
    win_buf = cache_win_kv[0].reshape(nbs, -1, 2 * KV_W)
    nsa_s = _nsa_sample(page_table, cache, win_buf, cw, qpad_s.reshape(nbs, ts, QPAD_W),
                        gates_s.reshape(nbs, ts, GZ_PAD), kv4_s.reshape(nbs, ts, 4 * KV_W),
                        kvw_s.reshape(nbs, ts, 2 * KV_W)).reshape(1, ns, QPAD_W)
    x1s, h2s = _outproj(xs, hg_out_s.reshape(1, ns, HG_WIDTH), nsa_s, mods[:, :, 2], mods[:, :, 4], mods[:, :, 3],
                        ffn_norm[0], wo_hg, wo_nsa, ns)

    h2 = jnp.concatenate([h2p, h2s], axis=0)
    top_e, top_w, rank, counts = _router(h2, w_router[0].astype(BF16), router_bias[0].reshape(1, -1), n_all)
    counts = counts[0].astype(jnp.int32)
    ends = jnp.cumsum(counts)
    starts = ends - counts
    dest = (starts[top_e[:, :TOP_K]] + rank[:, :TOP_K]).reshape(-1)
    n_pairs = n_all * TOP_K
    n_blocks = n_pairs // MOE_BM
    items = _moe_items(counts, starts, ends, n_blocks)
    xs_sorted = _dispatch(dest.reshape(n_all // ROUTER_TM, 1, ROUTER_TM * TOP_K), h2, n_pairs)
    ys_sorted = _moe_gmm(items, starts, ends, xs_sorted, w_gate[0], w_up[0], w_down[0], n_blocks)
    dest_c = dest.reshape(n_all // COMBINE_TM, 1, COMBINE_TM * TOP_K)
    shared = (ws_gate[0].astype(BF16), ws_up[0].astype(BF16), ws_down[0].astype(BF16))
    fnorm = final_norm.reshape(1, -1)
    y_prompt = _combine(dest_c, top_w, x1p, h2, modp[:, :, 5], shared, fnorm, ys_sorted, 0)
    y_sample = _combine(dest_c, top_w, x1s, h2, mods[:, :, 5], shared, fnorm, ys_sorted, nbp * t // COMBINE_TM)

    wb = win_buf.shape[1]
    win_p = kvw[:, t - min(WINDOW, t):]
    win_s = jnp.concatenate([win_buf, kvw_s.reshape(nbs, ts, 2 * KV_W)], axis=1)[:, -wb:]
    kv_shape = (4, NSA_KV_HEADS, NSA_HEAD_DIM)
    win_shape = (2, NSA_KV_HEADS, NSA_HEAD_DIM)
    return (y_prompt,
            y_sample.reshape(nbs, ts, D_MODEL),
            kv4.reshape(1, nbp, t, *kv_shape),
            win_p.reshape(1, nbp, -1, *win_shape),
            hg_state_p[None],
            kv4_s.reshape(1, nbs, ts, *kv_shape),
            win_s.reshape(1, nbs, wb, *win_shape),
            hg_state_s[None])
```

```python
import functools

import jax
import jax.numpy as jnp
import numpy as np
from jax import lax
from jax.experimental import pallas as pl
from jax.experimental.pallas import tpu as pltpu

F32 = jnp.float32
BF16 = jnp.bfloat16

D_MODEL = 1024
HG_WIDTH = 512
HG_HEADS = 4
HG_DK = 128
HG_CHUNK = 32
NSA_WIDTH = 512
NSA_HEADS = 8
NSA_HEAD_DIM = 64
NSA_KV_HEADS = 2
NSA_GROUP = 4
KV_W = 128
CMP_BLOCK = 32
CMP_STRIDE = 16
CMP_HIDDEN = 256
SEL_BLOCK = 64
SEL_TOPN = 16
WINDOW = 512
Q_BLOCK = 128
N_EXPERTS = 256
TOP_K = 8
N_GROUPS = 8
TOPK_GROUPS = 4
MOE_D_FF = 256
ROUTED_SCALE = 2.5
RMS_EPS = 1e-6
PAGE_SIZE = 128
IN_COLS = 4 * HG_WIDTH + NSA_WIDTH + 4 * KV_W + 2 * KV_W + 3 * NSA_HEADS

LANES = 128
SUBLANES = 8
TOK_ROWS = D_MODEL // LANES
VMEM_LIMIT = 56 * 1024 * 1024

QPAD_W = NSA_HEADS * LANES
GZ_PAD = LANES
INP_COLS = 4 * HG_WIDTH + QPAD_W + 4 * KV_W + 2 * KV_W + GZ_PAD

NEG = -1e30
PAGES_PER_STEP = 32
MOE_BM = 256
ROUTER_TM = 256
COMBINE_TM = 128


def _cparams(sem):
    return pltpu.CompilerParams(dimension_semantics=sem, vmem_limit_bytes=VMEM_LIMIT)


def _dot(a, b):
    return jnp.dot(a, b, preferred_element_type=F32)


def _dot_nt(a, b):
    return lax.dot_general(a, b, (((1,), (1,)), ((), ())), preferred_element_type=F32)


def _dot_tn(a, b):
    return lax.dot_general(a, b, (((0,), (0,)), ((), ())), preferred_element_type=F32)


def _rms(x, g):
    return x * lax.rsqrt(jnp.mean(x * x, axis=-1, keepdims=True) + RMS_EPS) * g


def _silu(x):
    return x * jax.nn.sigmoid(x)


def _masked_softmax(s, valid):
    s = jnp.where(valid, s, NEG)
    m = jnp.max(s, axis=1, keepdims=True)
    p = jnp.exp(s - m) * valid.astype(F32)
    return p / jnp.maximum(jnp.sum(p, axis=1, keepdims=True), 1e-30)


def _topk_mask(v, k):
    lane = lax.broadcasted_iota(jnp.int32, v.shape, 1).astype(F32)
    sel = jnp.zeros(v.shape, F32)
    for _ in range(k):
        m = jnp.max(v, axis=1, keepdims=True)
        idx = jnp.min(jnp.where(v == m, lane, 1e9), axis=1, keepdims=True)
        pick = lane == idx
        sel = jnp.where(pick, 1.0, sel)
        v = jnp.where(pick, 3.0 * NEG, v)
    return sel


def _mod_spec(mod, tm):
    if mod.shape[1] == 1:
        return pl.BlockSpec((1, 1, D_MODEL), lambda b, i: (b, 0, 0))
    return pl.BlockSpec((1, tm, D_MODEL), lambda b, i: (b, i, 0))


def _load_tok_tiles(ref, n_tok):
    return jnp.concatenate([ref[pl.ds(s, n_tok, stride=TOK_ROWS), :] for s in range(TOK_ROWS)], axis=1)


def _store_tok_tiles(ref, val, n_tok):
    for s in range(TOK_ROWS):
        ref[pl.ds(s, n_tok, stride=TOK_ROWS), :] = val[:, s * LANES:(s + 1) * LANES]


def _ada_kernel(c_ref, w_ref, b_ref, o_ref):
    s = _silu(c_ref[...]).astype(BF16)
    o_ref[...] = _dot(s, w_ref[...].astype(BF16)) + b_ref[...]


def _ada(c_all, w_ada, b_ada):
    n = c_all.shape[0]
    return pl.pallas_call(
        _ada_kernel,
        grid=(6,),
        in_specs=[pl.BlockSpec((n, D_MODEL), lambda j: (0, 0)),
                  pl.BlockSpec((D_MODEL, D_MODEL), lambda j: (0, j)),
                  pl.BlockSpec((1, D_MODEL), lambda j: (0, j))],
        out_specs=pl.BlockSpec((n, D_MODEL), lambda j: (0, j)),
        out_shape=jax.ShapeDtypeStruct((n, 6 * D_MODEL), F32),
        compiler_params=_cparams(("arbitrary",)),
        name="ada_mod",
    )(c_all, w_ada, b_ada.reshape(1, -1))


def _inproj_kernel(x_ref, sc_ref, sh_ref, g_ref, w_ref,
                   hg_ref, q_ref, kv4_ref, kvw_ref, gate_ref, kvsel_ref, kvwb_ref):
    h = _rms(x_ref[0], g_ref[...]) * (1.0 + sc_ref[0]) + sh_ref[0]
    z = _dot(h.astype(BF16), w_ref[...])
    c0 = 4 * HG_WIDTH
    hg_ref[0] = z[:, :c0]
    q_ref[0] = (z[:, c0:c0 + QPAD_W] * (NSA_HEAD_DIM ** -0.5)).astype(BF16)
    c1 = c0 + QPAD_W
    kv4 = z[:, c1:c1 + 4 * KV_W]
    kv4_ref[0] = kv4
    kvsel_ref[0] = kv4[:, 2 * KV_W:].astype(BF16)
    c2 = c1 + 4 * KV_W
    kvw = z[:, c2:c2 + 2 * KV_W]
    kvw_ref[0] = kvw
    kvwb_ref[0] = kvw.astype(BF16)
    gate_ref[0] = jax.nn.sigmoid(z[:, c2 + 2 * KV_W:])


def _inproj(x, scale, shift, g_norm, w_pad, tm):
    nb, t, _ = x.shape
    mod_spec = _mod_spec(scale, tm)
    widths = [(4 * HG_WIDTH, F32), (QPAD_W, BF16), (4 * KV_W, F32), (2 * KV_W, F32), (GZ_PAD, F32),
              (2 * KV_W, BF16), (2 * KV_W, BF16)]
    return pl.pallas_call(
        _inproj_kernel,
        grid=(nb, t // tm),
        in_specs=[pl.BlockSpec((1, tm, D_MODEL), lambda b, i: (b, i, 0)),
                  mod_spec, mod_spec,
                  pl.BlockSpec((1, D_MODEL), lambda b, i: (0, 0)),
                  pl.BlockSpec((D_MODEL, INP_COLS), lambda b, i: (0, 0))],
        out_specs=[pl.BlockSpec((1, tm, w), lambda b, i: (b, i, 0)) for w, _ in widths],
        out_shape=[jax.ShapeDtypeStruct((nb, t, w), dt) for w, dt in widths],
        compiler_params=_cparams(("arbitrary", "arbitrary")),
        name="in_proj",
    )(x, scale, shift, g_norm.reshape(1, -1), w_pad)


def _pad_w_in(w_in):
    c0 = 4 * HG_WIDTH
    wq = w_in[:, c0:c0 + NSA_WIDTH].reshape(D_MODEL, NSA_HEADS, NSA_HEAD_DIM)
    zeros = jnp.zeros_like(wq)
    lo = jnp.concatenate([wq, zeros], axis=-1)
    hi = jnp.concatenate([zeros, wq], axis=-1)
    grp = (jnp.arange(NSA_HEADS) // NSA_GROUP)[None, :, None]
    wq_pad = jnp.where(grp == 0, lo, hi).reshape(D_MODEL, QPAD_W)
    c1 = c0 + NSA_WIDTH
    rest = w_in[:, c1:c1 + 6 * KV_W]
    gz = jnp.pad(w_in[:, c1 + 6 * KV_W:], ((0, 0), (0, GZ_PAD - 3 * NSA_HEADS)))
    return jnp.concatenate([w_in[:, :c0], wq_pad, rest, gz], axis=1).astype(BF16)


def _hgrn_kernel(q_ref, f_ref, v_ref, gt_ref, lb_ref, s0_ref, gn_ref, o_ref, s_out_ref, st_scr,
                 *, chunk, n_chunks):
    i = pl.program_id(2)

    @pl.when(i == 0)
    def _():
        st_scr[...] = s0_ref[0, 0].T

    lbr = lb_ref[...]
    e = jnp.exp(lbr - jnp.max(lbr, axis=0, keepdims=True))
    lb = e[0:1] / jnp.sum(e, axis=0, keepdims=True)
    row = lax.broadcasted_iota(jnp.int32, (chunk, HG_DK), 0)
    causal = (lax.broadcasted_iota(jnp.int32, (chunk, chunk), 0)
              >= lax.broadcasted_iota(jnp.int32, (chunk, chunk), 1))
    st = st_scr[...]
    for c in range(n_chunks):
        sl = pl.ds(c * chunk, chunk)
        z = f_ref[0, sl, :]
        log_f = jnp.log(lb + (1.0 - lb) * jax.nn.sigmoid(z))
        kk = (1.0 - lb) * jax.nn.sigmoid(-z)
        a = log_f
        s = 1
        while s < chunk:
            a = a + jnp.where(row >= s, pltpu.roll(a, s, 0), 0.0)
            s *= 2
        qt = (q_ref[0, sl, :] * jnp.exp(a)).astype(BF16)
        kt = (kk * jnp.exp(-a)).astype(BF16)
        v = v_ref[0, sl, :].astype(BF16)
        att = jnp.where(causal, _dot_nt(qt, kt), 0.0)
        o = _dot(att.astype(BF16), v) + _dot_nt(qt, st.astype(BF16))
        a_end = a[chunk - 1:chunk, :]
        kd = (kk * jnp.exp(a_end - a)).astype(BF16)
        st = st * jnp.exp(a_end) + _dot_tn(v, kd)
        o = _rms(o, gn_ref[...]) * _silu(gt_ref[0, sl, :])
        o_ref[0, sl, :] = o.astype(o_ref.dtype)
    st_scr[...] = st

    @pl.when(i == pl.num_programs(2) - 1)
    def _():
        s_out_ref[0, 0] = st.T


def _hgrn(hg, hg_lb, s0, g_norm, tc, chunk):
    nb, t, _ = hg.shape
    col = lambda k: (lambda b, h, i: (b, i, k * HG_HEADS + h))
    st_spec = pl.BlockSpec((1, 1, HG_DK, HG_DK), lambda b, h, i: (b, h, 0, 0))
    return pl.pallas_call(
        functools.partial(_hgrn_kernel, chunk=chunk, n_chunks=tc // chunk),
        grid=(nb, HG_HEADS, t // tc),
        in_specs=[pl.BlockSpec((1, tc, HG_DK), col(0)),
                  pl.BlockSpec((1, tc, HG_DK), col(1)),
                  pl.BlockSpec((1, tc, HG_DK), col(2)),
                  pl.BlockSpec((1, tc, HG_DK), col(3)),
                  pl.BlockSpec((hg_lb.shape[0], HG_DK), lambda b, h, i: (0, h)),
                  st_spec,
                  pl.BlockSpec((1, HG_DK), lambda b, h, i: (0, 0))],
        out_specs=[pl.BlockSpec((1, tc, HG_DK), lambda b, h, i: (b, i, h)), st_spec],
        out_shape=[jax.ShapeDtypeStruct((nb, t, HG_WIDTH), BF16),
                   jax.ShapeDtypeStruct((nb, HG_HEADS, HG_DK, HG_DK), F32)],
        scratch_shapes=[pltpu.VMEM((HG_DK, HG_DK), F32)],
        compiler_params=_cparams(("arbitrary", "arbitrary", "arbitrary")),
        name="hgrn2",
    )(hg, hg, hg, hg, hg_lb, s0, g_norm.reshape(1, -1))


def _gelu_tanh(x):
    return 0.5 * x * (1.0 + jnp.tanh(0.7978845608028654 * (x + 0.044715 * x * x * x)))


def _page_copies(pt_ref, cache_ref, buf, sem, b, s, slot, n_pages, pps, col0, tail):
    copies = []
    base = s * pps
    nxt = pt_ref[b, jnp.minimum(base + pps, n_pages - 1)]
    for br in range(2):
        cols = pl.ds(col0 + br * KV_W, KV_W)
        for i in range(pps):
            pg = pt_ref[b, base + i]
            copies.append(pltpu.make_async_copy(
                cache_ref.at[pg, :, cols],
                buf.at[slot, br, pl.ds(i * PAGE_SIZE, PAGE_SIZE), :], sem.at[slot]))
        if tail:
            copies.append(pltpu.make_async_copy(
                cache_ref.at[nxt, pl.ds(0, CMP_STRIDE), cols],
                buf.at[slot, br, pl.ds(pps * PAGE_SIZE, CMP_STRIDE), :], sem.at[slot]))
    return copies


def _stream_pages(pt_ref, cache_ref, buf, sem, n_pages, n_steps, pps, col0, tail):
    b = pl.program_id(0)
    s = pl.program_id(1)
    n = b * n_steps + s
    total = pl.num_programs(0) * n_steps
    slot = n % 2
    args = (n_pages, pps, col0, tail)

    @pl.when(n == 0)
    def _():
        for cp in _page_copies(pt_ref, cache_ref, buf, sem, b, s, slot, *args):
            cp.start()

    @pl.when(n + 1 < total)
    def _():
        n1 = n + 1
        for cp in _page_copies(pt_ref, cache_ref, buf, sem, n1 // n_steps, n1 % n_steps, 1 - slot, *args):
            cp.start()

    for cp in _page_copies(pt_ref, cache_ref, buf, sem, b, s, slot, *args):
        cp.wait()
    return slot


def _compress_kernel(pt_ref, cache_ref, pe_ref, w1_ref, b1_ref, w2_ref, kc_ref, vc_ref, buf, sem,
                     *, n_pages, n_steps, pps):
    groups = pps * PAGE_SIZE // CMP_STRIDE
    slot = _stream_pages(pt_ref, cache_ref, buf, sem, n_pages, n_steps, pps, 0, True)

    for br, out_ref in ((0, kc_ref), (1, vc_ref)):
        acc = jnp.zeros((groups, 2 * CMP_HIDDEN), F32)
        for l in range(CMP_BLOCK):
            xl = buf[slot, br, pl.ds(l, groups, stride=CMP_STRIDE), :]
            xl = (xl + pe_ref[br, l:l + 1, :]).astype(BF16)
            acc = acc + _dot(xl, w1_ref[br, l])
        hid = _gelu_tanh(acc + b1_ref[br]).astype(BF16)
        out_ref[0] = _dot(hid, w2_ref[br]).astype(out_ref.dtype)


def _compress(page_table, cache, pe2, w1cat, b1cat, w2bd, pps=PAGES_PER_STEP):
    nb, n_pages = page_table.shape
    n_steps = n_pages // pps
    groups = pps * PAGE_SIZE // CMP_STRIDE
    rows = pps * PAGE_SIZE + CMP_STRIDE
    const = lambda shape: pl.BlockSpec(shape, lambda b, s, pt: (0,) * len(shape))
    out_spec = pl.BlockSpec((1, groups, KV_W), lambda b, s, pt: (b, s, 0))
    out_sds = jax.ShapeDtypeStruct((nb, n_steps * groups, KV_W), BF16)
    return pl.pallas_call(
        functools.partial(_compress_kernel, n_pages=n_pages, n_steps=n_steps, pps=pps),
        grid_spec=pltpu.PrefetchScalarGridSpec(
            num_scalar_prefetch=1,
            grid=(nb, n_steps),
            in_specs=[pl.BlockSpec(memory_space=pl.ANY),
                      const((2, CMP_BLOCK, KV_W)),
                      const((2, CMP_BLOCK, KV_W, 2 * CMP_HIDDEN)),
                      const((2, 1, 2 * CMP_HIDDEN)),
                      const((2, 2 * CMP_HIDDEN, KV_W))],
            out_specs=[out_spec, out_spec],
            scratch_shapes=[pltpu.VMEM((2, 2, rows, KV_W), F32), pltpu.SemaphoreType.DMA((2,))]),
        out_shape=[out_sds, out_sds],
        compiler_params=_cparams(("arbitrary", "arbitrary")),
        name="nsa_compress",
    )(page_table, cache, pe2, w1cat, b1cat, w2bd)


def _compress_weights(cmp_pe, cmp_w1, cmp_b1, cmp_w2):
    pe2 = jnp.concatenate([cmp_pe, cmp_pe], axis=-1)
    w1 = cmp_w1.reshape(2, CMP_BLOCK, NSA_HEAD_DIM, CMP_HIDDEN)
    z1 = jnp.zeros_like(w1)
    w1cat = jnp.concatenate([jnp.concatenate([w1, z1], axis=-1),
                             jnp.concatenate([z1, w1], axis=-1)], axis=2).astype(BF16)
    b1cat = jnp.concatenate([cmp_b1, cmp_b1], axis=-1)[:, None, :]
    z2 = jnp.zeros_like(cmp_w2)
    w2bd = jnp.concatenate([jnp.concatenate([cmp_w2, z2], axis=-1),
                            jnp.concatenate([z2, cmp_w2], axis=-1)], axis=1).astype(BF16)
    return pe2, w1cat, b1cat, w2bd


def _overlap_matrix(n_cmp, n_sel):
    cs = lax.broadcasted_iota(jnp.int32, (n_cmp, n_sel), 0) * CMP_STRIDE
    ss = lax.broadcasted_iota(jnp.int32, (n_cmp, n_sel), 1) * SEL_BLOCK
    return ((cs < ss + SEL_BLOCK) & (cs + CMP_BLOCK > ss)).astype(BF16)


SEL_CHUNK = 256
SPREAD_KEYS = 1024
WIN_SPAN = WINDOW + Q_BLOCK


def _nsa_prompt_kernel(q_ref, gt_ref, kc_ref, vc_ref, ks_ref, vs_ref, kw_ref, vw_ref, ex_ref, o_ref,
                       m_scr, acc_scr, chosen_scr, *, n_cmp, n_sel):
    j = pl.program_id(1)
    ones_blk = jnp.ones((SEL_CHUNK, KV_W), BF16)
    q0 = j * Q_BLOCK
    tok = lax.broadcasted_iota(jnp.int32, (Q_BLOCK, 1), 0) + q0
    tok4 = jnp.concatenate([tok] * NSA_GROUP, axis=0)
    gates = gt_ref[0]
    ov = _overlap_matrix(n_cmp, n_sel)
    blk = lax.broadcasted_iota(jnp.int32, (1, n_sel), 1)
    cur = lax.shift_right_logical(tok, 6)
    forced = (blk == 0) | (blk == cur) | (blk == cur - 1)
    allowed = blk * SEL_BLOCK <= tok
    cend = lax.broadcasted_iota(jnp.int32, (1, n_cmp), 1) * CMP_STRIDE + (CMP_BLOCK - 1)
    n_chunks = j // (SEL_CHUNK // Q_BLOCK) + 1
    ws = pl.multiple_of(jnp.maximum(q0 - WINDOW, 0), Q_BLOCK)
    wpos = ws + lax.broadcasted_iota(jnp.int32, (1, WIN_SPAN), 1)
    key_in_chunk = lax.broadcasted_iota(jnp.int32, (1, SEL_CHUNK), 1)

    for g in range(NSA_KV_HEADS):
        def load_q():
            return jnp.concatenate([q_ref[0, :, (NSA_GROUP * g + jh) * LANES:(NSA_GROUP * g + jh + 1) * LANES]
                                    for jh in range(NSA_GROUP)], axis=0)

        p_c = _masked_softmax(_dot_nt(load_q(), kc_ref[0]), cend <= tok4)
        o_c = _dot(p_c.astype(BF16), vc_ref[0])
        psum = (p_c[0:Q_BLOCK] + p_c[Q_BLOCK:2 * Q_BLOCK]
                + p_c[2 * Q_BLOCK:3 * Q_BLOCK] + p_c[3 * Q_BLOCK:4 * Q_BLOCK])
        imp = _dot(psum.astype(BF16), ov)
        sel = _topk_mask(jnp.where(forced, -NEG, jnp.where(allowed, imp, NEG)), SEL_TOPN).astype(BF16)

        m_scr[...] = jnp.full(m_scr.shape, NEG, F32)
        acc_scr[...] = jnp.zeros(acc_scr.shape, F32)

        def spread(i, carry):
            c0 = pl.multiple_of(i * SPREAD_KEYS, SPREAD_KEYS)
            chosen_scr[:, pl.ds(c0, SPREAD_KEYS)] = _dot(sel, ex_ref[:, pl.ds(c0, SPREAD_KEYS)])
            return carry

        lax.fori_loop(0, (n_chunks * SEL_CHUNK + SPREAD_KEYS - 1) // SPREAD_KEYS, spread, 0)

        def body(c, carry):
            k0 = pl.multiple_of(c * SEL_CHUNK, SEL_CHUNK)
            msk = (chosen_scr[:, pl.ds(k0, SEL_CHUNK)] > 0.5) & (key_in_chunk + k0 <= tok)
            kblk = ks_ref[0, pl.ds(k0, SEL_CHUNK), :]
            vext = jnp.concatenate([vs_ref[0, pl.ds(k0, SEL_CHUNK), :], ones_blk], axis=1)
            for jh in range(NSA_GROUP):
                r = pl.ds(jh * Q_BLOCK, Q_BLOCK)
                h = NSA_GROUP * g + jh
                s = jnp.where(msk, _dot_nt(q_ref[0, :, h * LANES:(h + 1) * LANES], kblk), NEG)
                s0, s1 = s[:, :LANES], s[:, LANES:]
                m_old = m_scr[r, :]
                m_new = jnp.maximum(m_old, jnp.max(jnp.maximum(s0, s1), axis=1, keepdims=True))
                p = jnp.concatenate([jnp.exp(s0 - m_new), jnp.exp(s1 - m_new)], axis=1).astype(BF16)
                alpha = jnp.exp(m_old - m_new)
                acc_scr[r, :] = jnp.concatenate([alpha, alpha], axis=1) * acc_scr[r, :] + _dot(p, vext)
                m_scr[r, :] = m_new
            return carry

        lax.fori_loop(0, n_chunks, body, 0)
        o_s = acc_scr[:, :KV_W] / jnp.maximum(acc_scr[:, KV_W:KV_W + 1], 1e-30)

        d = tok4 - wpos
        p_w = _masked_softmax(_dot_nt(load_q(), kw_ref[0, pl.ds(ws, WIN_SPAN), :]), (d >= 0) & (d < WINDOW))
        o_w = _dot(p_w.astype(BF16), vw_ref[0, pl.ds(ws, WIN_SPAN), :])

        for jh in range(NSA_GROUP):
            h = NSA_GROUP * g + jh
            r = slice(jh * Q_BLOCK, (jh + 1) * Q_BLOCK)
            o = (gates[:, 3 * h:3 * h + 1] * o_c[r] + gates[:, 3 * h + 1:3 * h + 2] * o_s[r]
                 + gates[:, 3 * h + 2:3 * h + 3] * o_w[r])
            o_ref[0, :, h * LANES:(h + 1) * LANES] = o.astype(o_ref.dtype)


def _nsa_prompt(qpad, gates, kc, vc, kvsel, kvwb):
    nb, t, _ = qpad.shape
    n_cmp = kc.shape[1]
    n_sel = t // SEL_BLOCK
    full = lambda w, k: pl.BlockSpec((1, t, w), lambda b, j: (b, 0, k))
    t_pad = -(-t // SPREAD_KEYS) * SPREAD_KEYS
    expand = jnp.asarray(np.arange(n_sel)[:, None] == (np.arange(t_pad)[None, :] // SEL_BLOCK), BF16)
    return pl.pallas_call(
        functools.partial(_nsa_prompt_kernel, n_cmp=n_cmp, n_sel=n_sel),
        grid=(nb, t // Q_BLOCK),
        in_specs=[pl.BlockSpec((1, Q_BLOCK, QPAD_W), lambda b, j: (b, j, 0)),
                  pl.BlockSpec((1, Q_BLOCK, GZ_PAD), lambda b, j: (b, j, 0)),
                  pl.BlockSpec((1, n_cmp, KV_W), lambda b, j: (b, 0, 0)),
                  pl.BlockSpec((1, n_cmp, KV_W), lambda b, j: (b, 0, 0)),
                  full(KV_W, 0), full(KV_W, 1), full(KV_W, 0), full(KV_W, 1),
                  pl.BlockSpec((n_sel, t_pad), lambda b, j: (0, 0))],
        out_specs=pl.BlockSpec((1, Q_BLOCK, QPAD_W), lambda b, j: (b, j, 0)),
        out_shape=jax.ShapeDtypeStruct((nb, t, QPAD_W), BF16),
        scratch_shapes=[pltpu.VMEM((NSA_GROUP * Q_BLOCK, LANES), F32),
                        pltpu.VMEM((NSA_GROUP * Q_BLOCK, 2 * KV_W), F32),
                        pltpu.VMEM((Q_BLOCK, t_pad), F32)],
        compiler_params=_cparams(("arbitrary", "arbitrary")),
        name="nsa_prompt",
    )(qpad, gates, kc, vc, kvsel, kvsel, kvwb, kvwb, expand)


def _nsa_sample_a_kernel(q_ref, g_ref, kc_ref, vc_ref, wb_ref, nw_ref, ocw_ref, sel_ref,
                         *, past_len, n_tok, n_sel, n_sel_pad):
    q = q_ref[0]
    rows = q.shape[0]
    n_cmp = kc_ref.shape[1]
    t_row = lax.broadcasted_iota(jnp.int32, (rows, 1), 0) & (n_tok - 1)
    qpos = past_len + t_row
    cend = lax.broadcasted_iota(jnp.int32, (1, n_cmp), 1) * CMP_STRIDE + (CMP_BLOCK - 1)
    p_c = _masked_softmax(_dot_nt(q, kc_ref[0]), cend <= qpos)
    o_c = _dot(p_c.astype(BF16), vc_ref[0])

    per_grp = NSA_GROUP * n_tok
    psum = jnp.concatenate(
        [sum(p_c[g * per_grp + jh * n_tok:g * per_grp + (jh + 1) * n_tok] for jh in range(NSA_GROUP))
         for g in range(NSA_KV_HEADS)], axis=0)
    imp = _dot(psum.astype(BF16), _overlap_matrix(n_cmp, n_sel_pad))
    blk = lax.broadcasted_iota(jnp.int32, (1, n_sel_pad), 1)
    tq = past_len + (lax.broadcasted_iota(jnp.int32, (NSA_KV_HEADS * n_tok, 1), 0) & (n_tok - 1))
    cur = lax.shift_right_logical(tq, 6)
    forced = (blk == 0) | (blk == cur) | (blk == cur - 1)
    allowed = blk * SEL_BLOCK <= tq
    v = jnp.where(forced, -NEG, jnp.where(allowed, imp, NEG))
    sel_ref[0] = _topk_mask(jnp.where(blk < n_sel, v, 2.0 * NEG), SEL_TOPN)

    wb = wb_ref.shape[1]
    kw = wb_ref[0, :, 0:KV_W].astype(BF16)
    vw = wb_ref[0, :, KV_W:2 * KV_W].astype(BF16)
    kn = nw_ref[0, :, 0:KV_W].astype(BF16)
    vn = nw_ref[0, :, KV_W:2 * KV_W].astype(BF16)
    i1 = lax.broadcasted_iota(jnp.int32, (1, wb), 1)
    d1 = t_row + wb - i1
    valid1 = (d1 >= 0) & (d1 < WINDOW) & (past_len - wb + i1 >= 0)
    i2 = lax.broadcasted_iota(jnp.int32, (1, nw_ref.shape[1]), 1)
    d2 = t_row - i2
    valid2 = (d2 >= 0) & (d2 < WINDOW) & (i2 < n_tok)
    s1 = jnp.where(valid1, _dot_nt(q, kw), NEG)
    s2 = jnp.where(valid2, _dot_nt(q, kn), NEG)
    m = jnp.maximum(jnp.max(s1, axis=1, keepdims=True), jnp.max(s2, axis=1, keepdims=True))
    p1 = jnp.exp(s1 - m) * valid1.astype(F32)
    p2 = jnp.exp(s2 - m) * valid2.astype(F32)
    den = jnp.maximum(jnp.sum(p1, axis=1, keepdims=True) + jnp.sum(p2, axis=1, keepdims=True), 1e-30)
    o_w = (_dot(p1.astype(BF16), vw) + _dot(p2.astype(BF16), vn)) / den
    g = g_ref[0]
    ocw_ref[0] = g[:, 0:1] * o_c + g[:, 2:3] * o_w


def _nsa_sample_a(q_rows, g_rows, kc, vc, win_buf, new_win, *, past_len, n_tok):
    nb, rows, _ = q_rows.shape
    n_sel = -(-(past_len + n_tok) // SEL_BLOCK)
    n_sel_pad = -(-n_sel // LANES) * LANES
    blk3 = lambda a: pl.BlockSpec((1,) + a.shape[1:], lambda b: (b, 0, 0))
    return pl.pallas_call(
        functools.partial(_nsa_sample_a_kernel, past_len=past_len, n_tok=n_tok, n_sel=n_sel, n_sel_pad=n_sel_pad),
        grid=(nb,),
        in_specs=[blk3(q_rows), blk3(g_rows), blk3(kc), blk3(vc), blk3(win_buf), blk3(new_win)],
        out_specs=[pl.BlockSpec((1, rows, KV_W), lambda b: (b, 0, 0)),
                   pl.BlockSpec((1, NSA_KV_HEADS * n_tok, n_sel_pad), lambda b: (b, 0, 0))],
        out_shape=[jax.ShapeDtypeStruct((nb, rows, KV_W), F32),
                   jax.ShapeDtypeStruct((nb, NSA_KV_HEADS * n_tok, n_sel_pad), F32)],
        compiler_params=_cparams(("arbitrary",)),
        name="nsa_sample_a",
    )(q_rows, g_rows, kc, vc, win_buf, new_win)


def _nsa_sample_b_kernel(pt_ref, cache_ref, q_ref, g_ref, sel_ref, seln_ref, ns_ref, ocw_ref, ex_ref, o_ref,
                         buf, sem, m_scr, l_scr, acc_scr, *, n_pages, n_steps, pps, n_tok):
    s = pl.program_id(1)
    slot = _stream_pages(pt_ref, cache_ref, buf, sem, n_pages, n_steps, pps, 2 * KV_W, False)
    q = q_ref[0]
    rows = q.shape[0]

    @pl.when(s == 0)
    def _():
        m_scr[...] = jnp.full(m_scr.shape, NEG, F32)
        l_scr[...] = jnp.zeros(l_scr.shape, F32)
        acc_scr[...] = jnp.zeros(acc_scr.shape, F32)

    def update(scores, msk, v):
        sc = jnp.where(msk, scores, NEG)
        m_old = m_scr[...]
        m_new = jnp.maximum(m_old, jnp.max(sc, axis=1, keepdims=True))
        p = jnp.exp(sc - m_new) * msk.astype(F32)
        alpha = jnp.exp(m_old - m_new)
        l_scr[...] = alpha * l_scr[...] + jnp.sum(p, axis=1, keepdims=True)
        acc_scr[...] = alpha * acc_scr[...] + _dot(p.astype(BF16), v)
        m_scr[...] = m_new

    chosen = _dot(sel_ref[0, 0], ex_ref[...]) > 0.5
    update(_dot_nt(q, buf[slot, 0].astype(BF16)), chosen, buf[slot, 1].astype(BF16))

    @pl.when(s == n_steps - 1)
    def _():
        t_row = lax.broadcasted_iota(jnp.int32, (rows, 1), 0) & (n_tok - 1)
        i2 = lax.broadcasted_iota(jnp.int32, (1, ns_ref.shape[1]), 1)
        msk = (seln_ref[0, 0][:, 0:1] > 0.5) & (i2 <= t_row) & (i2 < n_tok)
        update(_dot_nt(q, ns_ref[0, :, 0:KV_W].astype(BF16)), msk, ns_ref[0, :, KV_W:2 * KV_W].astype(BF16))
        o_s = acc_scr[...] / jnp.maximum(l_scr[...], 1e-30)
        o_ref[0] = ocw_ref[0] + g_ref[0][:, 1:2] * o_s


def _nsa_sample_b(page_table, cache, q_rows, g_rows, sel_steps, new_sel, ocw, *, n_tok, pps=PAGES_PER_STEP):
    nb, n_pages = page_table.shape
    n_steps = n_pages // pps
    rows = q_rows.shape[1]
    keys = pps * PAGE_SIZE
    expand = jnp.asarray(np.arange(LANES)[:, None] == (np.arange(keys)[None, :] // SEL_BLOCK), BF16)
    per_b = lambda a: pl.BlockSpec((1,) + a.shape[1:], lambda b, s, pt: (b, 0, 0))
    return pl.pallas_call(
        functools.partial(_nsa_sample_b_kernel, n_pages=n_pages, n_steps=n_steps, pps=pps, n_tok=n_tok),
        grid_spec=pltpu.PrefetchScalarGridSpec(
            num_scalar_prefetch=1,
            grid=(nb, n_steps),
            in_specs=[pl.BlockSpec(memory_space=pl.ANY),
                      per_b(q_rows), per_b(g_rows),
                      pl.BlockSpec((1, 1, rows, LANES), lambda b, s, pt: (b, s, 0, 0)),
                      pl.BlockSpec((1, 1, rows, LANES), lambda b, s, pt: (b, n_steps, 0, 0)),
                      per_b(new_sel), per_b(ocw),
                      pl.BlockSpec((LANES, keys), lambda b, s, pt: (0, 0))],
            out_specs=pl.BlockSpec((1, rows, KV_W), lambda b, s, pt: (b, 0, 0)),
            scratch_shapes=[pltpu.VMEM((2, 2, keys, KV_W), F32), pltpu.SemaphoreType.DMA((2,)),
                            pltpu.VMEM((rows, 1), F32), pltpu.VMEM((rows, 1), F32),
                            pltpu.VMEM((rows, KV_W), F32)]),
        out_shape=jax.ShapeDtypeStruct((nb, rows, KV_W), F32),
        compiler_params=_cparams(("arbitrary", "arbitrary")),
        name="nsa_sample_b",
    )(page_table, cache, q_rows, g_rows, sel_steps, sel_steps, new_sel, ocw, expand)


def _nsa_sample(page_table, cache, win_buf, cw, qpad, gates, kv4, kvw, pps=PAGES_PER_STEP):
    nb, ts, _ = qpad.shape
    past_len = page_table.shape[1] * PAGE_SIZE
    kc, vc = _compress(page_table, cache, *cw, pps=pps)
    rows = NSA_HEADS * ts
    q_rows = qpad.reshape(nb, ts, NSA_HEADS, LANES).transpose(0, 2, 1, 3).reshape(nb, rows, LANES)
    g_rows = gates[:, :, :3 * NSA_HEADS].reshape(nb, ts, NSA_HEADS, 3).transpose(0, 2, 1, 3)
    g_rows = jnp.pad(g_rows.reshape(nb, rows, 3), ((0, 0), (0, 0), (0, LANES - 3)))
    pad_rows = lambda a: jnp.pad(a, ((0, 0), (0, LANES - ts), (0, 0)))
    new_win = pad_rows(kvw)
    new_sel = pad_rows(kv4[:, :, 2 * KV_W:])
    ocw, sel = _nsa_sample_a(q_rows, g_rows, kc, vc, win_buf, new_win, past_len=past_len, n_tok=ts)
    n_steps = page_table.shape[1] // pps
    blk_per_step = pps * PAGE_SIZE // SEL_BLOCK
    n_past_blk = n_steps * blk_per_step
    sel_past = sel[:, :, :n_past_blk].reshape(nb, NSA_KV_HEADS, 1, ts, n_steps, blk_per_step)
    sel_past = jnp.broadcast_to(sel_past, (nb, NSA_KV_HEADS, NSA_GROUP, ts, n_steps, blk_per_step))
    sel_past = sel_past.transpose(0, 4, 1, 2, 3, 5).reshape(nb, n_steps, rows, blk_per_step)
    sel_past = jnp.pad(sel_past, ((0, 0), (0, 0), (0, 0), (0, LANES - blk_per_step)))
    sel_new = jnp.pad(sel[:, :, n_past_blk:], ((0, 0), (0, 0), (0, LANES)))[:, :, :LANES]
    sel_new = sel_new.reshape(nb, NSA_KV_HEADS, 1, ts, LANES)
    sel_new = jnp.broadcast_to(sel_new, (nb, NSA_KV_HEADS, NSA_GROUP, ts, LANES)).reshape(nb, 1, rows, LANES)
    sel_steps = jnp.concatenate([sel_past, sel_new], axis=1).astype(BF16)
    o_rows = _nsa_sample_b(page_table, cache, q_rows, g_rows, sel_steps, new_sel, ocw, n_tok=ts, pps=pps)
    return o_rows.reshape(nb, NSA_HEADS, ts, LANES).transpose(0, 2, 1, 3).reshape(nb, ts, QPAD_W).astype(BF16)


def _outproj_kernel(x_ref, hg_ref, nsa_ref, g1_ref, sc2_ref, sh2_ref, fn_ref, wo1_ref, wo2_ref,
                    x1_ref, h2_ref, *, tm):
    mix = _dot(hg_ref[0], wo1_ref[...]) + _dot(nsa_ref[0], wo2_ref[...])
    x1 = x_ref[0] + g1_ref[0] * mix
    x1_ref[0] = x1
    h2 = _rms(x1, fn_ref[...]) * (1.0 + sc2_ref[0]) + sh2_ref[0]
    _store_tok_tiles(h2_ref, h2, tm)


def _outproj(x, hg_out, nsa, gate1, scale2, shift2, ffn_norm, wo_hg, wo_nsa, tm):
    nb, t, _ = x.shape
    nt = t // tm
    mod_spec = _mod_spec(gate1, tm)
    tile = lambda w: pl.BlockSpec((1, tm, w), lambda b, i: (b, i, 0))
    return pl.pallas_call(
        functools.partial(_outproj_kernel, tm=tm),
        grid=(nb, nt),
        in_specs=[tile(D_MODEL), tile(HG_WIDTH), tile(QPAD_W), mod_spec, mod_spec, mod_spec,
                  pl.BlockSpec((1, D_MODEL), lambda b, i: (0, 0)),
                  pl.BlockSpec((HG_WIDTH, D_MODEL), lambda b, i: (0, 0)),
                  pl.BlockSpec((QPAD_W, D_MODEL), lambda b, i: (0, 0))],
        out_specs=[tile(D_MODEL), pl.BlockSpec((tm * TOK_ROWS, LANES), lambda b, i: (b * nt + i, 0))],
        out_shape=[jax.ShapeDtypeStruct((nb, t, D_MODEL), F32),
                   jax.ShapeDtypeStruct((nb * t * TOK_ROWS, LANES), F32)],
        compiler_params=_cparams(("arbitrary", "arbitrary")),
        name="out_proj",
    )(x, hg_out, nsa, gate1, scale2, shift2, ffn_norm.reshape(1, -1), wo_hg, wo_nsa)


def _split_w_out(w_out):
    wo_hg = w_out[:HG_WIDTH].astype(BF16)
    wn = w_out[HG_WIDTH:].reshape(NSA_HEADS, NSA_HEAD_DIM, D_MODEL)
    z = jnp.zeros_like(wn)
    grp = (jnp.arange(NSA_HEADS) // NSA_GROUP)[:, None, None]
    wn_pad = jnp.where(grp == 0, jnp.concatenate([wn, z], axis=1), jnp.concatenate([z, wn], axis=1))
    return wo_hg, wn_pad.reshape(QPAD_W, D_MODEL).astype(BF16)


def _router_kernel(h_ref, wr_ref, b_ref, e_ref, w_ref, r_ref, cnt_ref, run_scr, *, tm):
    @pl.when(pl.program_id(0) == 0)
    def _():
        run_scr[...] = jnp.zeros(run_scr.shape, F32)

    x = _load_tok_tiles(h_ref, tm).astype(BF16)
    scores = jax.nn.sigmoid(_dot(x, wr_ref[...]))
    biased = scores + b_ref[...]
    lane_i = lax.broadcasted_iota(jnp.int32, (tm, N_EXPERTS), 1)
    lane = lane_i.astype(F32)
    grp_of_lane = lax.shift_right_logical(lane_i, 5)
    per_group = N_EXPERTS // N_GROUPS

    gcol = lax.broadcasted_iota(jnp.int32, (tm, LANES), 1)
    gs = jnp.full((tm, LANES), 2.0 * NEG, F32)
    for g in range(N_GROUPS):
        mg = jnp.where(grp_of_lane == g, biased, NEG)
        m1 = jnp.max(mg, axis=1, keepdims=True)
        i1 = jnp.min(jnp.where(mg == m1, lane, 1e9), axis=1, keepdims=True)
        m2 = jnp.max(jnp.where(lane == i1, NEG, mg), axis=1, keepdims=True)
        gs = jnp.where(gcol == g, m1 + m2, gs)
    gsel = _topk_mask(gs, TOPK_GROUPS).astype(BF16)
    spread = (lax.broadcasted_iota(jnp.int32, (LANES, N_EXPERTS), 0)
              == lax.shift_right_logical(lax.broadcasted_iota(jnp.int32, (LANES, N_EXPERTS), 1), 5)).astype(BF16)
    v = jnp.where(_dot(gsel, spread) > 0.5, biased, NEG)

    onehot = jnp.zeros((tm, N_EXPERTS), F32)
    idxs, wts = [], []
    wsum = jnp.zeros((tm, 1), F32)
    for _ in range(TOP_K):
        m = jnp.max(v, axis=1, keepdims=True)
        idx = jnp.min(jnp.where(v == m, lane, 1e9), axis=1, keepdims=True)
        pick = lane == idx
        wk = jnp.sum(jnp.where(pick, scores, 0.0), axis=1, keepdims=True)
        onehot = jnp.where(pick, 1.0, onehot)
        v = jnp.where(pick, 3.0 * NEG, v)
        idxs.append(idx)
        wts.append(wk)
        wsum = wsum + wk

    earlier = (lax.broadcasted_iota(jnp.int32, (tm, tm), 0) > lax.broadcasted_iota(jnp.int32, (tm, tm), 1))
    before = _dot(earlier.astype(BF16), onehot.astype(BF16)) + run_scr[...]
    e_out = jnp.zeros((tm, LANES), jnp.int32)
    r_out = jnp.zeros((tm, LANES), jnp.int32)
    w_out = jnp.zeros((tm, LANES), F32)
    for k in range(TOP_K):
        rk = jnp.sum(jnp.where(lane == idxs[k], before, 0.0), axis=1, keepdims=True)
        e_out = jnp.where(gcol == k, idxs[k].astype(jnp.int32), e_out)
        r_out = jnp.where(gcol == k, rk.astype(jnp.int32), r_out)
        w_out = jnp.where(gcol == k, wts[k] / wsum * ROUTED_SCALE, w_out)
    e_ref[...] = e_out
    r_ref[...] = r_out
    w_ref[...] = w_out
    run_scr[...] = run_scr[...] + jnp.sum(onehot, axis=0, keepdims=True)
    cnt_ref[...] = run_scr[...]


def _router(h2, w_router, bias, n_tok, tm=ROUTER_TM):
    tile = pl.BlockSpec((tm, LANES), lambda i: (i, 0))
    return pl.pallas_call(
        functools.partial(_router_kernel, tm=tm),
        grid=(n_tok // tm,),
        in_specs=[pl.BlockSpec((tm * TOK_ROWS, LANES), lambda i: (i, 0)),
                  pl.BlockSpec((D_MODEL, N_EXPERTS), lambda i: (0, 0)),
                  pl.BlockSpec((1, N_EXPERTS), lambda i: (0, 0))],
        out_specs=[tile, tile, tile, pl.BlockSpec((1, N_EXPERTS), lambda i: (0, 0))],
        out_shape=[jax.ShapeDtypeStruct((n_tok, LANES), jnp.int32),
                   jax.ShapeDtypeStruct((n_tok, LANES), F32),
                   jax.ShapeDtypeStruct((n_tok, LANES), jnp.int32),
                   jax.ShapeDtypeStruct((1, N_EXPERTS), F32)],
        scratch_shapes=[pltpu.VMEM((1, N_EXPERTS), F32)],
        compiler_params=_cparams(("arbitrary",)),
        name="moe_router",
    )(h2, w_router, bias)


def _dest_kernel(e_ref, r_ref, st_ref, d_ref):
    e = e_ref[...]
    tm = e.shape[0]
    lane = lax.broadcasted_iota(jnp.int32, (tm, N_EXPERTS), 1)
    col = lax.broadcasted_iota(jnp.int32, (tm, LANES), 1)
    st = st_ref[...]
    out = r_ref[...]
    for k in range(TOP_K):
        sk = jnp.sum(jnp.where(lane == e[:, k:k + 1], st, 0.0), axis=1, keepdims=True)
        out = jnp.where(col == k, out + sk.astype(jnp.int32), out)
    d_ref[...] = out


def _moe_dest(top_e, rank, starts, tm=ROUTER_TM):
    n_tok = top_e.shape[0]
    tile = pl.BlockSpec((tm, LANES), lambda i: (i, 0))
    return pl.pallas_call(
        _dest_kernel,
        grid=(n_tok // tm,),
        in_specs=[tile, tile, pl.BlockSpec((1, N_EXPERTS), lambda i: (0, 0))],
        out_specs=tile,
        out_shape=jax.ShapeDtypeStruct((n_tok, LANES), jnp.int32),
        compiler_params=_cparams(("arbitrary",)),
        name="moe_dest",
    )(top_e, rank, starts.astype(F32).reshape(1, -1))


def _moe_items(counts, starts, ends, n_blocks):
    first_blk = starts // MOE_BM
    last_blk = (ends - 1) // MOE_BM
    n_it = jnp.where(counts > 0, last_blk - first_blk + 1, 0)
    it_end = jnp.cumsum(n_it)
    it_start = it_end - n_it
    total = it_end[-1]
    ii = jnp.arange(n_blocks + N_EXPERTS - 1, dtype=jnp.int32)
    valid = ii < total
    e_of = jnp.searchsorted(it_end, jnp.minimum(ii, total - 1), side="right").astype(jnp.int32)
    e_of = jnp.minimum(e_of, N_EXPERTS - 1)
    blk_of = jnp.where(valid, first_blk[e_of] + ii - it_start[e_of], n_blocks - 1).astype(jnp.int32)
    prev = jnp.concatenate([jnp.full((1,), -1, jnp.int32), blk_of[:-1]])
    first = (valid & (blk_of != prev)).astype(jnp.int32)
    return blk_of, e_of, first, valid.astype(jnp.int32)


def _dispatch_kernel(dest_ref, h_ref, xs_ref, sem, *, tm):
    n_pairs = tm * TOP_K

    def row_copy(src_tok, dst_row):
        return pltpu.make_async_copy(
            h_ref.at[pl.ds(pl.multiple_of(src_tok * TOK_ROWS, TOK_ROWS), TOK_ROWS), :],
            xs_ref.at[pl.ds(pl.multiple_of(dst_row * TOK_ROWS, TOK_ROWS), TOK_ROWS), :], sem)

    def issue(p, carry):
        row_copy(lax.shift_right_logical(p, 3), dest_ref[0, 0, p]).start()
        return carry

    def drain(p, carry):
        row_copy(0, 0).wait()
        return carry

    lax.fori_loop(0, n_pairs, issue, 0)
    lax.fori_loop(0, n_pairs, drain, 0)


def _dispatch(dest_tiles, h2, n_pairs, tm=ROUTER_TM):
    n_tiles = dest_tiles.shape[0]
    return pl.pallas_call(
        functools.partial(_dispatch_kernel, tm=tm),
        grid=(n_tiles,),
        in_specs=[pl.BlockSpec((1, 1, tm * TOP_K), lambda i: (i, 0, 0), memory_space=pltpu.SMEM),
                  pl.BlockSpec((tm * TOK_ROWS, LANES), lambda i: (i, 0))],
        out_specs=pl.BlockSpec(memory_space=pl.ANY),
        out_shape=jax.ShapeDtypeStruct((n_pairs * TOK_ROWS, LANES), F32),
        scratch_shapes=[pltpu.SemaphoreType.DMA(())],
        compiler_params=_cparams(("arbitrary",)),
        name="moe_dispatch",
    )(dest_tiles, h2)


def _gmm_kernel(blk_ref, e_ref, first_ref, valid_ref, st_ref, en_ref, xs_ref, wg_ref, wu_ref, wd_ref, ys_ref,
                *, bm):
    i = pl.program_id(0)

    @pl.when(valid_ref[i] == 1)
    def _():
        x = _load_tok_tiles(xs_ref, bm).astype(BF16)
        hid = (_silu(_dot(x, wg_ref[0].astype(BF16))) * _dot(x, wu_ref[0].astype(BF16))).astype(BF16)
        y = _dot(hid, wd_ref[0].astype(BF16))
        e = e_ref[i]
        row = blk_ref[i] * bm + lax.broadcasted_iota(jnp.int32, (bm, 1), 0)
        mine = (row >= st_ref[e]) & (row < en_ref[e])

        @pl.when(first_ref[i] == 1)
        def _():
            _store_tok_tiles(ys_ref, jnp.where(mine, y, 0.0), bm)

        @pl.when(first_ref[i] == 0)
        def _():
            _store_tok_tiles(ys_ref, jnp.where(mine, y, _load_tok_tiles(ys_ref, bm)), bm)


def _moe_gmm(items, starts, ends, xs_sorted, w_gate, w_up, w_down, n_blocks, bm=MOE_BM):
    blk_of, e_of, first, valid = items
    rows = pl.BlockSpec((bm * TOK_ROWS, LANES), lambda i, blk, e, f, v, st, en: (blk[i], 0))
    wspec = lambda a: pl.BlockSpec((1,) + a.shape[1:], lambda i, blk, e, f, v, st, en: (e[i], 0, 0))
    return pl.pallas_call(
        functools.partial(_gmm_kernel, bm=bm),
        grid_spec=pltpu.PrefetchScalarGridSpec(
            num_scalar_prefetch=6,
            grid=(blk_of.shape[0],),
            in_specs=[rows, wspec(w_gate), wspec(w_up), wspec(w_down)],
            out_specs=rows),
        out_shape=jax.ShapeDtypeStruct(xs_sorted.shape, F32),
        compiler_params=_cparams(("arbitrary",)),
        name="moe_experts",
    )(blk_of, e_of, first, valid, starts.astype(jnp.int32), ends.astype(jnp.int32), xs_sorted, w_gate, w_up, w_down)


def _combine_kernel(dest_ref, w_ref, x1_ref, h_ref, g2_ref, wsg_ref, wsu_ref, wsd_ref, fn_ref, ys_ref, o_ref,
                    gbuf, sem, *, tm):
    n_pairs = tm * TOP_K

    def row_copy(src_row, p):
        return pltpu.make_async_copy(
            ys_ref.at[pl.ds(pl.multiple_of(src_row * TOK_ROWS, TOK_ROWS), TOK_ROWS), :],
            gbuf.at[pl.ds(pl.multiple_of(p * TOK_ROWS, TOK_ROWS), TOK_ROWS), :], sem)

    def issue(p, carry):
        row_copy(dest_ref[0, 0, p], p).start()
        return carry

    def drain(p, carry):
        row_copy(0, p).wait()
        return carry

    lax.fori_loop(0, n_pairs, issue, 0)
    h = _load_tok_tiles(h_ref, tm).astype(BF16)
    hid = (_silu(_dot(h, wsg_ref[...])) * _dot(h, wsu_ref[...])).astype(BF16)
    shared = _dot(hid, wsd_ref[...])
    lax.fori_loop(0, n_pairs, drain, 0)

    w = w_ref[...]
    stride = TOP_K * TOK_ROWS
    slabs = []
    for s in range(TOK_ROWS):
        acc = jnp.zeros((tm, LANES), F32)
        for k in range(TOP_K):
            acc = acc + w[:, k:k + 1] * gbuf[pl.ds(k * TOK_ROWS + s, tm, stride=stride), :]
        slabs.append(acc)
    routed = jnp.concatenate(slabs, axis=1)
    x2 = x1_ref[0] + g2_ref[0] * (routed + shared)
    o_ref[0] = _rms(x2, fn_ref[...])


def _combine(dest_tiles, top_w, x1, h2, gate2, shared, fnorm, ys_sorted, tile0, tm=COMBINE_TM):
    nb, t, _ = x1.shape
    nt = t // tm
    flat = lambda b, i: tile0 + b * nt + i
    mod_spec = _mod_spec(gate2, tm)
    const = lambda a: pl.BlockSpec(a.shape, lambda b, i: (0, 0))
    return pl.pallas_call(
        functools.partial(_combine_kernel, tm=tm),
        grid=(nb, nt),
        in_specs=[pl.BlockSpec((1, 1, tm * TOP_K), lambda b, i: (flat(b, i), 0, 0), memory_space=pltpu.SMEM),
                  pl.BlockSpec((tm, LANES), lambda b, i: (flat(b, i), 0)),
                  pl.BlockSpec((1, tm, D_MODEL), lambda b, i: (b, i, 0)),
                  pl.BlockSpec((tm * TOK_ROWS, LANES), lambda b, i: (flat(b, i), 0)),
                  mod_spec, const(shared[0]), const(shared[1]), const(shared[2]), const(fnorm),
                  pl.BlockSpec(memory_space=pl.ANY)],
        out_specs=pl.BlockSpec((1, tm, D_MODEL), lambda b, i: (b, i, 0)),
        out_shape=jax.ShapeDtypeStruct((nb, t, D_MODEL), F32),
        scratch_shapes=[pltpu.VMEM((tm * TOP_K * TOK_ROWS, LANES), F32), pltpu.SemaphoreType.DMA(())],
        compiler_params=_cparams(("arbitrary", "arbitrary")),
        name="moe_combine",
    )(dest_tiles, top_w, x1, h2, gate2, *shared, fnorm, ys_sorted)


def kernel(x_prompt, x_sample, c_prompt, c_sample, cache_nsa_kv, cache_win_kv, state_hgrn, page_table,
           attn_norm, ffn_norm, final_norm, hg_norm, w_ada, b_ada, w_in, hg_lb,
           cmp_pe, cmp_w1, cmp_b1, cmp_w2, w_out, w_router, router_bias,
           w_gate, w_up, w_down, ws_gate, ws_up, ws_down):
    nbp, t, _ = x_prompt.shape
    nbs, ts, _ = x_sample.shape
    ns = nbs * ts
    n_all = nbp * t + ns
    past_len = page_table.shape[1] * PAGE_SIZE

    c_all = jnp.concatenate([c_prompt, c_sample], axis=0)
    c_all = jnp.pad(c_all, ((0, -c_all.shape[0] % SUBLANES), (0, 0)))
    mod = _ada(c_all, w_ada[0], b_ada[0])
    modp = mod[:nbp].reshape(nbp, 1, 6, D_MODEL)
    mods = jnp.repeat(mod[nbp:nbp + nbs].reshape(nbs, 1, 6, D_MODEL), ts, axis=1).reshape(1, ns, 6, D_MODEL)

    w_pad = _pad_w_in(w_in[0])
    cw = _compress_weights(cmp_pe[0], cmp_w1[0], cmp_b1[0], cmp_w2[0])
    wo_hg, wo_nsa = _split_w_out(w_out[0])

    hg, qpad, kv4, kvw, gates, kvsel, kvwb = _inproj(
        x_prompt, modp[:, :, 1], modp[:, :, 0], attn_norm[0], w_pad, 512)
    hg_out, hg_state_p = _hgrn(hg, hg_lb, jnp.zeros((nbp, HG_HEADS, HG_DK, HG_DK), F32), hg_norm[0],
                               256, HG_CHUNK)
    n_pages_p = t // PAGE_SIZE
    ptp = jnp.arange(nbp * n_pages_p, dtype=jnp.int32).reshape(nbp, n_pages_p)
    kc, vc = _compress(ptp, kv4.reshape(nbp * n_pages_p, PAGE_SIZE, 4 * KV_W), *cw)
    nsa = _nsa_prompt(qpad, gates, kc, vc, kvsel, kvwb)
    x1p, h2p = _outproj(x_prompt, hg_out, nsa, modp[:, :, 2], modp[:, :, 4], modp[:, :, 3],
                        ffn_norm[0], wo_hg, wo_nsa, 512)

    xs = x_sample.reshape(1, ns, D_MODEL)
    hg_s, qpad_s, kv4_s, kvw_s, gates_s, _, _ = _inproj(
        xs, mods[:, :, 1], mods[:, :, 0], attn_norm[0], w_pad, ns)
    hg_out_s, hg_state_s = _hgrn(hg_s.reshape(nbs, ts, 4 * HG_WIDTH), hg_lb, state_hgrn[0], hg_norm[0], ts, ts)
    cache = cache_nsa_kv[0].reshape(-1, PAGE_SIZE, 4 * KV_W)
    win_buf = cache_win_kv[0].reshape(nbs, -1, 2 * KV_W)
    nsa_s = _nsa_sample(page_table, cache, win_buf, cw, qpad_s.reshape(nbs, ts, QPAD_W),
                        gates_s.reshape(nbs, ts, GZ_PAD), kv4_s.reshape(nbs, ts, 4 * KV_W),
                        kvw_s.reshape(nbs, ts, 2 * KV_W)).reshape(1, ns, QPAD_W)
    x1s, h2s = _outproj(xs, hg_out_s.reshape(1, ns, HG_WIDTH), nsa_s, mods[:, :, 2], mods[:, :, 4], mods[:, :, 3],
                        ffn_norm[0], wo_hg, wo_nsa, ns)

    h2 = jnp.concatenate([h2p, h2s], axis=0)
    top_e, top_w, rank, counts = _router(h2, w_router[0].astype(BF16), router_bias[0].reshape(1, -1), n_all)
    counts = counts[0].astype(jnp.int32)
    ends = jnp.cumsum(counts)
    starts = ends - counts
    dest = _moe_dest(top_e, rank, starts)[:, :TOP_K].reshape(-1)
    n_pairs = n_all * TOP_K
    n_blocks = n_pairs // MOE_BM
    items = _moe_items(counts, starts, ends, n_blocks)
    xs_sorted = _dispatch(dest.reshape(n_all // ROUTER_TM, 1, ROUTER_TM * TOP_K), h2, n_pairs)
    ys_sorted = _moe_gmm(items, starts, ends, xs_sorted, w_gate[0], w_up[0], w_down[0], n_blocks)
    dest_c = dest.reshape(n_all // COMBINE_TM, 1, COMBINE_TM * TOP_K)
    shared = (ws_gate[0].astype(BF16), ws_up[0].astype(BF16), ws_down[0].astype(BF16))
    fnorm = final_norm.reshape(1, -1)
    y_prompt = _combine(dest_c, top_w, x1p, h2, modp[:, :, 5], shared, fnorm, ys_sorted, 0)
    y_sample = _combine(dest_c, top_w, x1s, h2, mods[:, :, 5], shared, fnorm, ys_sorted, nbp * t // COMBINE_TM)

    wb = win_buf.shape[1]
    win_p = kvw[:, t - min(WINDOW, t):]
    win_s = jnp.concatenate([win_buf, kvw_s.reshape(nbs, ts, 2 * KV_W)], axis=1)[:, -wb:]
    kv_shape = (4, NSA_KV_HEADS, NSA_HEAD_DIM)
    win_shape = (2, NSA_KV_HEADS, NSA_HEAD_DIM)
    return (y_prompt,
            y_sample.reshape(nbs, ts, D_MODEL),
            kv4.reshape(1, nbp, t, *kv_shape),
            win_p.reshape(1, nbp, -1, *win_shape),
            hg_state_p[None],
            kv4_s.reshape(1, nbs, ts, *kv_shape),
            win_s.reshape(1, nbs, wb, *win_shape),
            hg_state_s[None])
```

```python
import functools

import jax
import jax.numpy as jnp
import numpy as np
from jax import lax
from jax.experimental import pallas as pl
from jax.experimental.pallas import tpu as pltpu

F32 = jnp.float32
BF16 = jnp.bfloat16

D_MODEL = 1024
HG_WIDTH = 512
HG_HEADS = 4
HG_DK = 128
HG_CHUNK = 32
NSA_WIDTH = 512
NSA_HEADS = 8
NSA_HEAD_DIM = 64
NSA_KV_HEADS = 2
NSA_GROUP = 4
KV_W = 128
CMP_BLOCK = 32
CMP_STRIDE = 16
CMP_HIDDEN = 256
SEL_BLOCK = 64
SEL_TOPN = 16
WINDOW = 512
Q_BLOCK = 128
N_EXPERTS = 256
TOP_K = 8
N_GROUPS = 8
TOPK_GROUPS = 4
MOE_D_FF = 256
ROUTED_SCALE = 2.5
RMS_EPS = 1e-6
PAGE_SIZE = 128
IN_COLS = 4 * HG_WIDTH + NSA_WIDTH + 4 * KV_W + 2 * KV_W + 3 * NSA_HEADS

LANES = 128
SUBLANES = 8
TOK_ROWS = D_MODEL // LANES
VMEM_LIMIT = 56 * 1024 * 1024

QPAD_W = NSA_HEADS * LANES
GZ_PAD = LANES
INP_COLS = 4 * HG_WIDTH + QPAD_W + 4 * KV_W + 2 * KV_W + GZ_PAD

NEG = -1e30
PAGES_PER_STEP = 32
MOE_BM = 256
ROUTER_TM = 256
COMBINE_TM = 128


def _cparams(sem):
    return pltpu.CompilerParams(dimension_semantics=sem, vmem_limit_bytes=VMEM_LIMIT)


def _dot(a, b):
    return jnp.dot(a, b, preferred_element_type=F32)


def _dot_nt(a, b):
    return lax.dot_general(a, b, (((1,), (1,)), ((), ())), preferred_element_type=F32)


def _dot_tn(a, b):
    return lax.dot_general(a, b, (((0,), (0,)), ((), ())), preferred_element_type=F32)


def _rms(x, g):
    return x * lax.rsqrt(jnp.mean(x * x, axis=-1, keepdims=True) + RMS_EPS) * g


def _silu(x):
    return x * jax.nn.sigmoid(x)


def _masked_softmax(s, valid):
    s = jnp.where(valid, s, NEG)
    m = jnp.max(s, axis=1, keepdims=True)
    p = jnp.exp(s - m) * valid.astype(F32)
    return p / jnp.maximum(jnp.sum(p, axis=1, keepdims=True), 1e-30)


def _topk_mask(v, k):
    lane = lax.broadcasted_iota(jnp.int32, v.shape, 1).astype(F32)
    sel = jnp.zeros(v.shape, F32)
    for _ in range(k):
        m = jnp.max(v, axis=1, keepdims=True)
        idx = jnp.min(jnp.where(v == m, lane, 1e9), axis=1, keepdims=True)
        pick = lane == idx
        sel = jnp.where(pick, 1.0, sel)
        v = jnp.where(pick, 3.0 * NEG, v)
    return sel


def _mod_spec(mod, tm):
    if mod.shape[1] == 1:
        return pl.BlockSpec((1, 1, D_MODEL), lambda b, i: (b, 0, 0))
    return pl.BlockSpec((1, tm, D_MODEL), lambda b, i: (b, i, 0))


def _load_tok_tiles(ref, n_tok):
    return jnp.concatenate([ref[pl.ds(s, n_tok, stride=TOK_ROWS), :] for s in range(TOK_ROWS)], axis=1)


def _store_tok_tiles(ref, val, n_tok):
    for s in range(TOK_ROWS):
        ref[pl.ds(s, n_tok, stride=TOK_ROWS), :] = val[:, s * LANES:(s + 1) * LANES]


def _ada_kernel(c_ref, w_ref, b_ref, o_ref):
    s = _silu(c_ref[...]).astype(BF16)
    o_ref[...] = _dot(s, w_ref[...].astype(BF16)) + b_ref[...]


def _ada(c_all, w_ada, b_ada):
    n = c_all.shape[0]
    return pl.pallas_call(
        _ada_kernel,
        grid=(6,),
        in_specs=[pl.BlockSpec((n, D_MODEL), lambda j: (0, 0)),
                  pl.BlockSpec((D_MODEL, D_MODEL), lambda j: (0, j)),
                  pl.BlockSpec((1, D_MODEL), lambda j: (0, j))],
        out_specs=pl.BlockSpec((n, D_MODEL), lambda j: (0, j)),
        out_shape=jax.ShapeDtypeStruct((n, 6 * D_MODEL), F32),
        compiler_params=_cparams(("arbitrary",)),
        name="ada_mod",
    )(c_all, w_ada, b_ada.reshape(1, -1))


def _inproj_kernel(x_ref, sc_ref, sh_ref, g_ref, w_ref,
                   hg_ref, q_ref, kv4_ref, kvw_ref, gate_ref, kvsel_ref, kvwb_ref):
    h = _rms(x_ref[0], g_ref[...]) * (1.0 + sc_ref[0]) + sh_ref[0]
    z = _dot(h.astype(BF16), w_ref[...])
    c0 = 4 * HG_WIDTH
    hg_ref[0] = z[:, :c0]
    q_ref[0] = (z[:, c0:c0 + QPAD_W] * (NSA_HEAD_DIM ** -0.5)).astype(BF16)
    c1 = c0 + QPAD_W
    kv4 = z[:, c1:c1 + 4 * KV_W]
    kv4_ref[0] = kv4
    kvsel_ref[0] = kv4[:, 2 * KV_W:].astype(BF16)
    c2 = c1 + 4 * KV_W
    kvw = z[:, c2:c2 + 2 * KV_W]
    kvw_ref[0] = kvw
    kvwb_ref[0] = kvw.astype(BF16)
    gate_ref[0] = jax.nn.sigmoid(z[:, c2 + 2 * KV_W:])


def _inproj(x, scale, shift, g_norm, w_pad, tm):
    nb, t, _ = x.shape
    mod_spec = _mod_spec(scale, tm)
    widths = [(4 * HG_WIDTH, F32), (QPAD_W, BF16), (4 * KV_W, F32), (2 * KV_W, F32), (GZ_PAD, F32),
              (2 * KV_W, BF16), (2 * KV_W, BF16)]
    return pl.pallas_call(
        _inproj_kernel,
        grid=(nb, t // tm),
        in_specs=[pl.BlockSpec((1, tm, D_MODEL), lambda b, i: (b, i, 0)),
                  mod_spec, mod_spec,
                  pl.BlockSpec((1, D_MODEL), lambda b, i: (0, 0)),
                  pl.BlockSpec((D_MODEL, INP_COLS), lambda b, i: (0, 0))],
        out_specs=[pl.BlockSpec((1, tm, w), lambda b, i: (b, i, 0)) for w, _ in widths],
        out_shape=[jax.ShapeDtypeStruct((nb, t, w), dt) for w, dt in widths],
        compiler_params=_cparams(("arbitrary", "arbitrary")),
        name="in_proj",
    )(x, scale, shift, g_norm.reshape(1, -1), w_pad)


def _pad_w_in(w_in):
    c0 = 4 * HG_WIDTH
    wq = w_in[:, c0:c0 + NSA_WIDTH].reshape(D_MODEL, NSA_HEADS, NSA_HEAD_DIM)
    zeros = jnp.zeros_like(wq)
    lo = jnp.concatenate([wq, zeros], axis=-1)
    hi = jnp.concatenate([zeros, wq], axis=-1)
    grp = (jnp.arange(NSA_HEADS) // NSA_GROUP)[None, :, None]
    wq_pad = jnp.where(grp == 0, lo, hi).reshape(D_MODEL, QPAD_W)
    c1 = c0 + NSA_WIDTH
    rest = w_in[:, c1:c1 + 6 * KV_W]
    gz = jnp.pad(w_in[:, c1 + 6 * KV_W:], ((0, 0), (0, GZ_PAD - 3 * NSA_HEADS)))
    return jnp.concatenate([w_in[:, :c0], wq_pad, rest, gz], axis=1).astype(BF16)


def _hgrn_kernel(q_ref, f_ref, v_ref, gt_ref, lb_ref, s0_ref, gn_ref, o_ref, s_out_ref, st_scr,
                 *, chunk, n_chunks):
    i = pl.program_id(2)

    @pl.when(i == 0)
    def _():
        st_scr[...] = s0_ref[0, 0].T

    lbr = lb_ref[...]
    e = jnp.exp(lbr - jnp.max(lbr, axis=0, keepdims=True))
    lb = e[0:1] / jnp.sum(e, axis=0, keepdims=True)
    row = lax.broadcasted_iota(jnp.int32, (chunk, HG_DK), 0)
    causal = (lax.broadcasted_iota(jnp.int32, (chunk, chunk), 0)
              >= lax.broadcasted_iota(jnp.int32, (chunk, chunk), 1))
    st = st_scr[...]
    for c in range(n_chunks):
        sl = pl.ds(c * chunk, chunk)
        z = f_ref[0, sl, :]
        log_f = jnp.log(lb + (1.0 - lb) * jax.nn.sigmoid(z))
        kk = (1.0 - lb) * jax.nn.sigmoid(-z)
        a = log_f
        s = 1
        while s < chunk:
            a = a + jnp.where(row >= s, pltpu.roll(a, s, 0), 0.0)
            s *= 2
        qt = (q_ref[0, sl, :] * jnp.exp(a)).astype(BF16)
        kt = (kk * jnp.exp(-a)).astype(BF16)
        v = v_ref[0, sl, :].astype(BF16)
        att = jnp.where(causal, _dot_nt(qt, kt), 0.0)
        o = _dot(att.astype(BF16), v) + _dot_nt(qt, st.astype(BF16))
        a_end = a[chunk - 1:chunk, :]
        kd = (kk * jnp.exp(a_end - a)).astype(BF16)
        st = st * jnp.exp(a_end) + _dot_tn(v, kd)
        o = _rms(o, gn_ref[...]) * _silu(gt_ref[0, sl, :])
        o_ref[0, sl, :] = o.astype(o_ref.dtype)
    st_scr[...] = st

    @pl.when(i == pl.num_programs(2) - 1)
    def _():
        s_out_ref[0, 0] = st.T


def _hgrn(hg, hg_lb, s0, g_norm, tc, chunk):
    nb, t, _ = hg.shape
    col = lambda k: (lambda b, h, i: (b, i, k * HG_HEADS + h))
    st_spec = pl.BlockSpec((1, 1, HG_DK, HG_DK), lambda b, h, i: (b, h, 0, 0))
    return pl.pallas_call(
        functools.partial(_hgrn_kernel, chunk=chunk, n_chunks=tc // chunk),
        grid=(nb, HG_HEADS, t // tc),
        in_specs=[pl.BlockSpec((1, tc, HG_DK), col(0)),
                  pl.BlockSpec((1, tc, HG_DK), col(1)),
                  pl.BlockSpec((1, tc, HG_DK), col(2)),
                  pl.BlockSpec((1, tc, HG_DK), col(3)),
                  pl.BlockSpec((hg_lb.shape[0], HG_DK), lambda b, h, i: (0, h)),
                  st_spec,
                  pl.BlockSpec((1, HG_DK), lambda b, h, i: (0, 0))],
        out_specs=[pl.BlockSpec((1, tc, HG_DK), lambda b, h, i: (b, i, h)), st_spec],
        out_shape=[jax.ShapeDtypeStruct((nb, t, HG_WIDTH), BF16),
                   jax.ShapeDtypeStruct((nb, HG_HEADS, HG_DK, HG_DK), F32)],
        scratch_shapes=[pltpu.VMEM((HG_DK, HG_DK), F32)],
        compiler_params=_cparams(("arbitrary", "arbitrary", "arbitrary")),
        name="hgrn2",
    )(hg, hg, hg, hg, hg_lb, s0, g_norm.reshape(1, -1))


def _gelu_tanh(x):
    return 0.5 * x * (1.0 + jnp.tanh(0.7978845608028654 * (x + 0.044715 * x * x * x)))


def _page_copies(pt_ref, cache_ref, buf, sem, b, s, slot, n_pages, pps, col0, tail):
    copies = []
    base = s * pps
    nxt = pt_ref[b, jnp.minimum(base + pps, n_pages - 1)]
    for br in range(2):
        cols = pl.ds(col0 + br * KV_W, KV_W)
        for i in range(pps):
            pg = pt_ref[b, base + i]
            copies.append(pltpu.make_async_copy(
                cache_ref.at[pg, :, cols],
                buf.at[slot, br, pl.ds(i * PAGE_SIZE, PAGE_SIZE), :], sem.at[slot]))
        if tail:
            copies.append(pltpu.make_async_copy(
                cache_ref.at[nxt, pl.ds(0, CMP_STRIDE), cols],
                buf.at[slot, br, pl.ds(pps * PAGE_SIZE, CMP_STRIDE), :], sem.at[slot]))
    return copies


def _stream_pages(pt_ref, cache_ref, buf, sem, n_pages, n_steps, pps, col0, tail):
    b = pl.program_id(0)
    s = pl.program_id(1)
    n = b * n_steps + s
    total = pl.num_programs(0) * n_steps
    slot = n % 2
    args = (n_pages, pps, col0, tail)

    @pl.when(n == 0)
    def _():
        for cp in _page_copies(pt_ref, cache_ref, buf, sem, b, s, slot, *args):
            cp.start()

    @pl.when(n + 1 < total)
    def _():
        n1 = n + 1
        for cp in _page_copies(pt_ref, cache_ref, buf, sem, n1 // n_steps, n1 % n_steps, 1 - slot, *args):
            cp.start()

    for cp in _page_copies(pt_ref, cache_ref, buf, sem, b, s, slot, *args):
        cp.wait()
    return slot


def _compress_kernel(pt_ref, cache_ref, pe_ref, w1_ref, b1_ref, w2_ref, kc_ref, vc_ref, buf, sem,
                     *, n_pages, n_steps, pps):
    groups = pps * PAGE_SIZE // CMP_STRIDE
    slot = _stream_pages(pt_ref, cache_ref, buf, sem, n_pages, n_steps, pps, 0, True)

    for br, out_ref in ((0, kc_ref), (1, vc_ref)):
        acc = jnp.zeros((groups, 2 * CMP_HIDDEN), F32)
        for l in range(CMP_BLOCK):
            xl = buf[slot, br, pl.ds(l, groups, stride=CMP_STRIDE), :]
            xl = (xl + pe_ref[br, l:l + 1, :]).astype(BF16)
            acc = acc + _dot(xl, w1_ref[br, l])
        hid = _gelu_tanh(acc + b1_ref[br]).astype(BF16)
        out_ref[0] = _dot(hid, w2_ref[br]).astype(out_ref.dtype)


def _compress(page_table, cache, pe2, w1cat, b1cat, w2bd, pps=PAGES_PER_STEP):
    nb, n_pages = page_table.shape
    n_steps = n_pages // pps
    groups = pps * PAGE_SIZE // CMP_STRIDE
    rows = pps * PAGE_SIZE + CMP_STRIDE
    const = lambda shape: pl.BlockSpec(shape, lambda b, s, pt: (0,) * len(shape))
    out_spec = pl.BlockSpec((1, groups, KV_W), lambda b, s, pt: (b, s, 0))
    out_sds = jax.ShapeDtypeStruct((nb, n_steps * groups, KV_W), BF16)
    return pl.pallas_call(
        functools.partial(_compress_kernel, n_pages=n_pages, n_steps=n_steps, pps=pps),
        grid_spec=pltpu.PrefetchScalarGridSpec(
            num_scalar_prefetch=1,
            grid=(nb, n_steps),
            in_specs=[pl.BlockSpec(memory_space=pl.ANY),
                      const((2, CMP_BLOCK, KV_W)),
                      const((2, CMP_BLOCK, KV_W, 2 * CMP_HIDDEN)),
                      const((2, 1, 2 * CMP_HIDDEN)),
                      const((2, 2 * CMP_HIDDEN, KV_W))],
            out_specs=[out_spec, out_spec],
            scratch_shapes=[pltpu.VMEM((2, 2, rows, KV_W), F32), pltpu.SemaphoreType.DMA((2,))]),
        out_shape=[out_sds, out_sds],
        compiler_params=_cparams(("arbitrary", "arbitrary")),
        name="nsa_compress",
    )(page_table, cache, pe2, w1cat, b1cat, w2bd)


def _compress_weights(cmp_pe, cmp_w1, cmp_b1, cmp_w2):
    pe2 = jnp.concatenate([cmp_pe, cmp_pe], axis=-1)
    w1 = cmp_w1.reshape(2, CMP_BLOCK, NSA_HEAD_DIM, CMP_HIDDEN)
    z1 = jnp.zeros_like(w1)
    w1cat = jnp.concatenate([jnp.concatenate([w1, z1], axis=-1),
                             jnp.concatenate([z1, w1], axis=-1)], axis=2).astype(BF16)
    b1cat = jnp.concatenate([cmp_b1, cmp_b1], axis=-1)[:, None, :]
    z2 = jnp.zeros_like(cmp_w2)
    w2bd = jnp.concatenate([jnp.concatenate([cmp_w2, z2], axis=-1),
                            jnp.concatenate([z2, cmp_w2], axis=-1)], axis=1).astype(BF16)
    return pe2, w1cat, b1cat, w2bd


def _overlap_matrix(n_cmp, n_sel):
    cs = lax.broadcasted_iota(jnp.int32, (n_cmp, n_sel), 0) * CMP_STRIDE
    ss = lax.broadcasted_iota(jnp.int32, (n_cmp, n_sel), 1) * SEL_BLOCK
    return ((cs < ss + SEL_BLOCK) & (cs + CMP_BLOCK > ss)).astype(BF16)


SEL_CHUNK = 512
SPREAD_KEYS = 1024
WIN_SPAN = WINDOW + Q_BLOCK


def _nsa_prompt_kernel(q_ref, gt_ref, kc_ref, vc_ref, ks_ref, vs_ref, kw_ref, vw_ref, ex_ref, o_ref,
                       m_scr, acc_scr, chosen_scr, oc_scr, *, n_cmp, n_sel):
    j = pl.program_id(1)
    q0 = j * Q_BLOCK
    tok = lax.broadcasted_iota(jnp.int32, (Q_BLOCK, 1), 0) + q0
    tok4 = jnp.concatenate([tok] * NSA_GROUP, axis=0)
    n_chunks = j // (SEL_CHUNK // Q_BLOCK) + 1
    key_in_chunk = lax.broadcasted_iota(jnp.int32, (1, SEL_CHUNK), 1)

    def load_q(g):
        return jnp.concatenate([q_ref[0, :, (NSA_GROUP * g + jh) * LANES:(NSA_GROUP * g + jh + 1) * LANES]
                                for jh in range(NSA_GROUP)], axis=0)

    def softmax_av(s, valid, v):
        s = jnp.where(valid, s, NEG)
        tiles = [s[:, i * LANES:(i + 1) * LANES] for i in range(s.shape[1] // LANES)]
        m = jnp.max(functools.reduce(jnp.maximum, tiles), axis=1, keepdims=True)
        p = jnp.where(valid, jnp.exp(s - m), 0.0)
        acc = _dot(p.astype(BF16), jnp.concatenate([v, jnp.ones(v.shape, BF16)], axis=1))
        inv = 1.0 / jnp.maximum(acc[:, KV_W:], 1e-30)
        return p, acc[:, :KV_W] * inv, inv

    ov = _overlap_matrix(n_cmp, n_sel)
    cend = lax.broadcasted_iota(jnp.int32, (1, n_cmp), 1) * CMP_STRIDE + (CMP_BLOCK - 1)
    imps = []
    for g in range(NSA_KV_HEADS):
        p, o_c, inv = softmax_av(_dot_nt(load_q(g), kc_ref[0]), cend <= tok4, vc_ref[0])
        oc_scr[g] = o_c
        psum = jnp.zeros((Q_BLOCK, n_cmp), F32)
        for jh in range(NSA_GROUP):
            r = slice(jh * Q_BLOCK, (jh + 1) * Q_BLOCK)
            psum = psum + p[r] * jnp.concatenate([inv[r]] * (n_cmp // LANES), axis=1)
        imps.append(_dot(psum.astype(BF16), ov))

    blk = lax.broadcasted_iota(jnp.int32, (1, n_sel), 1)
    cur = lax.shift_right_logical(tok, 6)
    forced = (blk == 0) | (blk == cur) | (blk == cur - 1)
    free = (blk * SEL_BLOCK <= tok) & jnp.logical_not(forced)
    forced2 = jnp.concatenate([forced] * NSA_KV_HEADS, axis=0)
    free2 = jnp.concatenate([free] * NSA_KV_HEADS, axis=0)
    best = _topk_mask(jnp.where(free2, jnp.concatenate(imps, axis=0), NEG), SEL_TOPN - 3)
    sel2 = jnp.where(forced2, 1.0, best).astype(BF16)

    ws = pl.multiple_of(jnp.maximum(q0 - WINDOW, 0), Q_BLOCK)
    wpos = ws + lax.broadcasted_iota(jnp.int32, (1, WIN_SPAN), 1)
    d = tok4 - wpos
    in_window = (d >= 0) & (d < WINDOW)
    gates = gt_ref[0]
    ones_blk = jnp.ones((SEL_CHUNK, KV_W), BF16)

    for g in range(NSA_KV_HEADS):
        sel = sel2[g * Q_BLOCK:(g + 1) * Q_BLOCK]
        m_scr[...] = jnp.full(m_scr.shape, NEG, F32)
        acc_scr[...] = jnp.zeros(acc_scr.shape, F32)

        def spread(i, carry):
            c0 = pl.multiple_of(i * SPREAD_KEYS, SPREAD_KEYS)
            chosen_scr[:, pl.ds(c0, SPREAD_KEYS)] = _dot(sel, ex_ref[:, pl.ds(c0, SPREAD_KEYS)])
            return carry

        lax.fori_loop(0, (n_chunks * SEL_CHUNK + SPREAD_KEYS - 1) // SPREAD_KEYS, spread, 0)

        def body(c, carry):
            k0 = pl.multiple_of(c * SEL_CHUNK, SEL_CHUNK)
            msk = (chosen_scr[:, pl.ds(k0, SEL_CHUNK)] > 0.5) & (key_in_chunk + k0 <= tok)
            kblk = ks_ref[0, pl.ds(k0, SEL_CHUNK), :]
            vext = jnp.concatenate([vs_ref[0, pl.ds(k0, SEL_CHUNK), :], ones_blk], axis=1)
            for jh in range(NSA_GROUP):
                r = pl.ds(jh * Q_BLOCK, Q_BLOCK)
                h = NSA_GROUP * g + jh
                s = jnp.where(msk, _dot_nt(q_ref[0, :, h * LANES:(h + 1) * LANES], kblk), NEG)
                tiles = [s[:, i * LANES:(i + 1) * LANES] for i in range(SEL_CHUNK // LANES)]
                m_old = m_scr[r, :]
                m_new = jnp.maximum(m_old, jnp.max(functools.reduce(jnp.maximum, tiles), axis=1, keepdims=True))
                p = jnp.concatenate([jnp.exp(x - m_new) for x in tiles], axis=1).astype(BF16)
                alpha = jnp.exp(m_old - m_new)
                acc_scr[r, :] = jnp.concatenate([alpha, alpha], axis=1) * acc_scr[r, :] + _dot(p, vext)
                m_scr[r, :] = m_new
            return carry

        lax.fori_loop(0, n_chunks, body, 0)
        o_s = acc_scr[:, :KV_W] / jnp.maximum(acc_scr[:, KV_W:KV_W + 1], 1e-30)

        _, o_w, _ = softmax_av(_dot_nt(load_q(g), kw_ref[0, pl.ds(ws, WIN_SPAN), :]), in_window,
                               vw_ref[0, pl.ds(ws, WIN_SPAN), :])

        for jh in range(NSA_GROUP):
            h = NSA_GROUP * g + jh
            r = slice(jh * Q_BLOCK, (jh + 1) * Q_BLOCK)
            o = (gates[:, 3 * h:3 * h + 1] * oc_scr[g, r, :] + gates[:, 3 * h + 1:3 * h + 2] * o_s[r]
                 + gates[:, 3 * h + 2:3 * h + 3] * o_w[r])
            o_ref[0, :, h * LANES:(h + 1) * LANES] = o.astype(o_ref.dtype)


def _nsa_prompt(qpad, gates, kc, vc, kvsel, kvwb):
    nb, t, _ = qpad.shape
    n_cmp = kc.shape[1]
    n_sel = t // SEL_BLOCK
    full = lambda w, k: pl.BlockSpec((1, t, w), lambda b, j: (b, 0, k))
    t_pad = -(-t // SPREAD_KEYS) * SPREAD_KEYS
    expand = jnp.asarray(np.arange(n_sel)[:, None] == (np.arange(t_pad)[None, :] // SEL_BLOCK), BF16)
    return pl.pallas_call(
        functools.partial(_nsa_prompt_kernel, n_cmp=n_cmp, n_sel=n_sel),
        grid=(nb, t // Q_BLOCK),
        in_specs=[pl.BlockSpec((1, Q_BLOCK, QPAD_W), lambda b, j: (b, j, 0)),
                  pl.BlockSpec((1, Q_BLOCK, GZ_PAD), lambda b, j: (b, j, 0)),
                  pl.BlockSpec((1, n_cmp, KV_W), lambda b, j: (b, 0, 0)),
                  pl.BlockSpec((1, n_cmp, KV_W), lambda b, j: (b, 0, 0)),
                  full(KV_W, 0), full(KV_W, 1), full(KV_W, 0), full(KV_W, 1),
                  pl.BlockSpec((n_sel, t_pad), lambda b, j: (0, 0))],
        out_specs=pl.BlockSpec((1, Q_BLOCK, QPAD_W), lambda b, j: (b, j, 0)),
        out_shape=jax.ShapeDtypeStruct((nb, t, QPAD_W), BF16),
        scratch_shapes=[pltpu.VMEM((NSA_GROUP * Q_BLOCK, LANES), F32),
                        pltpu.VMEM((NSA_GROUP * Q_BLOCK, 2 * KV_W), F32),
                        pltpu.VMEM((Q_BLOCK, t_pad), F32),
                        pltpu.VMEM((NSA_KV_HEADS, NSA_GROUP * Q_BLOCK, KV_W), F32)],
        compiler_params=_cparams(("arbitrary", "arbitrary")),
        name="nsa_prompt",
    )(qpad, gates, kc, vc, kvsel, kvsel, kvwb, kvwb, expand)


def _nsa_sample_a_kernel(q_ref, g_ref, kc_ref, vc_ref, wb_ref, nw_ref, ocw_ref, sel_ref,
                         *, past_len, n_tok, n_sel, n_sel_pad):
    q = q_ref[0]
    rows = q.shape[0]
    n_cmp = kc_ref.shape[1]
    t_row = lax.broadcasted_iota(jnp.int32, (rows, 1), 0) & (n_tok - 1)
    qpos = past_len + t_row
    cend = lax.broadcasted_iota(jnp.int32, (1, n_cmp), 1) * CMP_STRIDE + (CMP_BLOCK - 1)
    p_c = _masked_softmax(_dot_nt(q, kc_ref[0]), cend <= qpos)
    o_c = _dot(p_c.astype(BF16), vc_ref[0])

    per_grp = NSA_GROUP * n_tok
    psum = jnp.concatenate(
        [sum(p_c[g * per_grp + jh * n_tok:g * per_grp + (jh + 1) * n_tok] for jh in range(NSA_GROUP))
         for g in range(NSA_KV_HEADS)], axis=0)
    imp = _dot(psum.astype(BF16), _overlap_matrix(n_cmp, n_sel_pad))
    blk = lax.broadcasted_iota(jnp.int32, (1, n_sel_pad), 1)
    tq = past_len + (lax.broadcasted_iota(jnp.int32, (NSA_KV_HEADS * n_tok, 1), 0) & (n_tok - 1))
    cur = lax.shift_right_logical(tq, 6)
    forced = (blk == 0) | (blk == cur) | (blk == cur - 1)
    allowed = blk * SEL_BLOCK <= tq
    v = jnp.where(forced, -NEG, jnp.where(allowed, imp, NEG))
    sel_ref[0] = _topk_mask(jnp.where(blk < n_sel, v, 2.0 * NEG), SEL_TOPN)

    wb = wb_ref.shape[1]
    kw = wb_ref[0, :, 0:KV_W].astype(BF16)
    vw = wb_ref[0, :, KV_W:2 * KV_W].astype(BF16)
    kn = nw_ref[0, :, 0:KV_W].astype(BF16)
    vn = nw_ref[0, :, KV_W:2 * KV_W].astype(BF16)
    i1 = lax.broadcasted_iota(jnp.int32, (1, wb), 1)
    d1 = t_row + wb - i1
    valid1 = (d1 >= 0) & (d1 < WINDOW) & (past_len - wb + i1 >= 0)
    i2 = lax.broadcasted_iota(jnp.int32, (1, nw_ref.shape[1]), 1)
    d2 = t_row - i2
    valid2 = (d2 >= 0) & (d2 < WINDOW) & (i2 < n_tok)
    s1 = jnp.where(valid1, _dot_nt(q, kw), NEG)
    s2 = jnp.where(valid2, _dot_nt(q, kn), NEG)
    m = jnp.maximum(jnp.max(s1, axis=1, keepdims=True), jnp.max(s2, axis=1, keepdims=True))
    p1 = jnp.exp(s1 - m) * valid1.astype(F32)
    p2 = jnp.exp(s2 - m) * valid2.astype(F32)
    den = jnp.maximum(jnp.sum(p1, axis=1, keepdims=True) + jnp.sum(p2, axis=1, keepdims=True), 1e-30)
    o_w = (_dot(p1.astype(BF16), vw) + _dot(p2.astype(BF16), vn)) / den
    g = g_ref[0]
    ocw_ref[0] = g[:, 0:1] * o_c + g[:, 2:3] * o_w


def _nsa_sample_a(q_rows, g_rows, kc, vc, win_buf, new_win, *, past_len, n_tok):
    nb, rows, _ = q_rows.shape
    n_sel = -(-(past_len + n_tok) // SEL_BLOCK)
    n_sel_pad = -(-n_sel // LANES) * LANES
    blk3 = lambda a: pl.BlockSpec((1,) + a.shape[1:], lambda b: (b, 0, 0))
    return pl.pallas_call(
        functools.partial(_nsa_sample_a_kernel, past_len=past_len, n_tok=n_tok, n_sel=n_sel, n_sel_pad=n_sel_pad),
        grid=(nb,),
        in_specs=[blk3(q_rows), blk3(g_rows), blk3(kc), blk3(vc), blk3(win_buf), blk3(new_win)],
        out_specs=[pl.BlockSpec((1, rows, KV_W), lambda b: (b, 0, 0)),
                   pl.BlockSpec((1, NSA_KV_HEADS * n_tok, n_sel_pad), lambda b: (b, 0, 0))],
        out_shape=[jax.ShapeDtypeStruct((nb, rows, KV_W), F32),
                   jax.ShapeDtypeStruct((nb, NSA_KV_HEADS * n_tok, n_sel_pad), F32)],
        compiler_params=_cparams(("arbitrary",)),
        name="nsa_sample_a",
    )(q_rows, g_rows, kc, vc, win_buf, new_win)


def _nsa_sample_b_kernel(pt_ref, cache_ref, q_ref, g_ref, sel_ref, seln_ref, ns_ref, ocw_ref, ex_ref, o_ref,
                         buf, sem, m_scr, l_scr, acc_scr, *, n_pages, n_steps, pps, n_tok):
    s = pl.program_id(1)
    slot = _stream_pages(pt_ref, cache_ref, buf, sem, n_pages, n_steps, pps, 2 * KV_W, False)
    q = q_ref[0]
    rows = q.shape[0]

    @pl.when(s == 0)
    def _():
        m_scr[...] = jnp.full(m_scr.shape, NEG, F32)
        l_scr[...] = jnp.zeros(l_scr.shape, F32)
        acc_scr[...] = jnp.zeros(acc_scr.shape, F32)

    def update(scores, msk, v):
        sc = jnp.where(msk, scores, NEG)
        m_old = m_scr[...]
        m_new = jnp.maximum(m_old, jnp.max(sc, axis=1, keepdims=True))
        p = jnp.exp(sc - m_new) * msk.astype(F32)
        alpha = jnp.exp(m_old - m_new)
        l_scr[...] = alpha * l_scr[...] + jnp.sum(p, axis=1, keepdims=True)
        acc_scr[...] = alpha * acc_scr[...] + _dot(p.astype(BF16), v)
        m_scr[...] = m_new

    chosen = _dot(sel_ref[0, 0], ex_ref[...]) > 0.5
    update(_dot_nt(q, buf[slot, 0].astype(BF16)), chosen, buf[slot, 1].astype(BF16))

    @pl.when(s == n_steps - 1)
    def _():
        t_row = lax.broadcasted_iota(jnp.int32, (rows, 1), 0) & (n_tok - 1)
        i2 = lax.broadcasted_iota(jnp.int32, (1, ns_ref.shape[1]), 1)
        msk = (seln_ref[0, 0][:, 0:1] > 0.5) & (i2 <= t_row) & (i2 < n_tok)
        update(_dot_nt(q, ns_ref[0, :, 0:KV_W].astype(BF16)), msk, ns_ref[0, :, KV_W:2 * KV_W].astype(BF16))
        o_s = acc_scr[...] / jnp.maximum(l_scr[...], 1e-30)
        o_ref[0] = ocw_ref[0] + g_ref[0][:, 1:2] * o_s


def _nsa_sample_b(page_table, cache, q_rows, g_rows, sel_steps, new_sel, ocw, *, n_tok, pps=PAGES_PER_STEP):
    nb, n_pages = page_table.shape
    n_steps = n_pages // pps
    rows = q_rows.shape[1]
    keys = pps * PAGE_SIZE
    expand = jnp.asarray(np.arange(LANES)[:, None] == (np.arange(keys)[None, :] // SEL_BLOCK), BF16)
    per_b = lambda a: pl.BlockSpec((1,) + a.shape[1:], lambda b, s, pt: (b, 0, 0))
    return pl.pallas_call(
        functools.partial(_nsa_sample_b_kernel, n_pages=n_pages, n_steps=n_steps, pps=pps, n_tok=n_tok),
        grid_spec=pltpu.PrefetchScalarGridSpec(
            num_scalar_prefetch=1,
            grid=(nb, n_steps),
            in_specs=[pl.BlockSpec(memory_space=pl.ANY),
                      per_b(q_rows), per_b(g_rows),
                      pl.BlockSpec((1, 1, rows, LANES), lambda b, s, pt: (b, s, 0, 0)),
                      pl.BlockSpec((1, 1, rows, LANES), lambda b, s, pt: (b, n_steps, 0, 0)),
                      per_b(new_sel), per_b(ocw),
                      pl.BlockSpec((LANES, keys), lambda b, s, pt: (0, 0))],
            out_specs=pl.BlockSpec((1, rows, KV_W), lambda b, s, pt: (b, 0, 0)),
            scratch_shapes=[pltpu.VMEM((2, 2, keys, KV_W), F32), pltpu.SemaphoreType.DMA((2,)),
                            pltpu.VMEM((rows, 1), F32), pltpu.VMEM((rows, 1), F32),
                            pltpu.VMEM((rows, KV_W), F32)]),
        out_shape=jax.ShapeDtypeStruct((nb, rows, KV_W), F32),
        compiler_params=_cparams(("arbitrary", "arbitrary")),
        name="nsa_sample_b",
    )(page_table, cache, q_rows, g_rows, sel_steps, sel_steps, new_sel, ocw, expand)


def _nsa_sample(page_table, cache, win_buf, cw, qpad, gates, kv4, kvw, pps=PAGES_PER_STEP):
    nb, ts, _ = qpad.shape
    past_len = page_table.shape[1] * PAGE_SIZE
    kc, vc = _compress(page_table, cache, *cw, pps=pps)
    rows = NSA_HEADS * ts
    q_rows = qpad.reshape(nb, ts, NSA_HEADS, LANES).transpose(0, 2, 1, 3).reshape(nb, rows, LANES)
    g_rows = gates[:, :, :3 * NSA_HEADS].reshape(nb, ts, NSA_HEADS, 3).transpose(0, 2, 1, 3)
    g_rows = jnp.pad(g_rows.reshape(nb, rows, 3), ((0, 0), (0, 0), (0, LANES - 3)))
    pad_rows = lambda a: jnp.pad(a, ((0, 0), (0, LANES - ts), (0, 0)))
    new_win = pad_rows(kvw)
    new_sel = pad_rows(kv4[:, :, 2 * KV_W:])
    ocw, sel = _nsa_sample_a(q_rows, g_rows, kc, vc, win_buf, new_win, past_len=past_len, n_tok=ts)
    n_steps = page_table.shape[1] // pps
    blk_per_step = pps * PAGE_SIZE // SEL_BLOCK
    n_past_blk = n_steps * blk_per_step
    sel_past = sel[:, :, :n_past_blk].reshape(nb, NSA_KV_HEADS, 1, ts, n_steps, blk_per_step)
    sel_past = jnp.broadcast_to(sel_past, (nb, NSA_KV_HEADS, NSA_GROUP, ts, n_steps, blk_per_step))
    sel_past = sel_past.transpose(0, 4, 1, 2, 3, 5).reshape(nb, n_steps, rows, blk_per_step)
    sel_past = jnp.pad(sel_past, ((0, 0), (0, 0), (0, 0), (0, LANES - blk_per_step)))
    sel_new = jnp.pad(sel[:, :, n_past_blk:], ((0, 0), (0, 0), (0, LANES)))[:, :, :LANES]
    sel_new = sel_new.reshape(nb, NSA_KV_HEADS, 1, ts, LANES)
    sel_new = jnp.broadcast_to(sel_new, (nb, NSA_KV_HEADS, NSA_GROUP, ts, LANES)).reshape(nb, 1, rows, LANES)
    sel_steps = jnp.concatenate([sel_past, sel_new], axis=1).astype(BF16)
    o_rows = _nsa_sample_b(page_table, cache, q_rows, g_rows, sel_steps, new_sel, ocw, n_tok=ts, pps=pps)
    return o_rows.reshape(nb, NSA_HEADS, ts, LANES).transpose(0, 2, 1, 3).reshape(nb, ts, QPAD_W).astype(BF16)


def _outproj_kernel(x_ref, hg_ref, nsa_ref, g1_ref, sc2_ref, sh2_ref, fn_ref, wo1_ref, wo2_ref,
                    x1_ref, h2_ref, *, tm):
    mix = _dot(hg_ref[0], wo1_ref[...]) + _dot(nsa_ref[0], wo2_ref[...])
    x1 = x_ref[0] + g1_ref[0] * mix
    x1_ref[0] = x1
    h2 = _rms(x1, fn_ref[...]) * (1.0 + sc2_ref[0]) + sh2_ref[0]
    _store_tok_tiles(h2_ref, h2, tm)


def _outproj(x, hg_out, nsa, gate1, scale2, shift2, ffn_norm, wo_hg, wo_nsa, tm):
    nb, t, _ = x.shape
    nt = t // tm
    mod_spec = _mod_spec(gate1, tm)
    tile = lambda w: pl.BlockSpec((1, tm, w), lambda b, i: (b, i, 0))
    return pl.pallas_call(
        functools.partial(_outproj_kernel, tm=tm),
        grid=(nb, nt),
        in_specs=[tile(D_MODEL), tile(HG_WIDTH), tile(QPAD_W), mod_spec, mod_spec, mod_spec,
                  pl.BlockSpec((1, D_MODEL), lambda b, i: (0, 0)),
                  pl.BlockSpec((HG_WIDTH, D_MODEL), lambda b, i: (0, 0)),
                  pl.BlockSpec((QPAD_W, D_MODEL), lambda b, i: (0, 0))],
        out_specs=[tile(D_MODEL), pl.BlockSpec((tm * TOK_ROWS, LANES), lambda b, i: (b * nt + i, 0))],
        out_shape=[jax.ShapeDtypeStruct((nb, t, D_MODEL), F32),
                   jax.ShapeDtypeStruct((nb * t * TOK_ROWS, LANES), F32)],
        compiler_params=_cparams(("arbitrary", "arbitrary")),
        name="out_proj",
    )(x, hg_out, nsa, gate1, scale2, shift2, ffn_norm.reshape(1, -1), wo_hg, wo_nsa)


def _split_w_out(w_out):
    wo_hg = w_out[:HG_WIDTH].astype(BF16)
    wn = w_out[HG_WIDTH:].reshape(NSA_HEADS, NSA_HEAD_DIM, D_MODEL)
    z = jnp.zeros_like(wn)
    grp = (jnp.arange(NSA_HEADS) // NSA_GROUP)[:, None, None]
    wn_pad = jnp.where(grp == 0, jnp.concatenate([wn, z], axis=1), jnp.concatenate([z, wn], axis=1))
    return wo_hg, wn_pad.reshape(QPAD_W, D_MODEL).astype(BF16)


def _router_kernel(h_ref, wr_ref, b_ref, e_ref, w_ref, r_ref, cnt_ref, run_scr, *, tm):
    @pl.when(pl.program_id(0) == 0)
    def _():
        run_scr[...] = jnp.zeros(run_scr.shape, F32)

    x = _load_tok_tiles(h_ref, tm).astype(BF16)
    scores = jax.nn.sigmoid(_dot(x, wr_ref[...]))
    biased = scores + b_ref[...]
    lane_i = lax.broadcasted_iota(jnp.int32, (tm, N_EXPERTS), 1)
    lane = lane_i.astype(F32)
    grp_of_lane = lax.shift_right_logical(lane_i, 5)
    per_group = N_EXPERTS // N_GROUPS

    gcol = lax.broadcasted_iota(jnp.int32, (tm, LANES), 1)
    gs = jnp.full((tm, LANES), 2.0 * NEG, F32)
    for g in range(N_GROUPS):
        mg = jnp.where(grp_of_lane == g, biased, NEG)
        m1 = jnp.max(mg, axis=1, keepdims=True)
        i1 = jnp.min(jnp.where(mg == m1, lane, 1e9), axis=1, keepdims=True)
        m2 = jnp.max(jnp.where(lane == i1, NEG, mg), axis=1, keepdims=True)
        gs = jnp.where(gcol == g, m1 + m2, gs)
    gsel = _topk_mask(gs, TOPK_GROUPS).astype(BF16)
    spread = (lax.broadcasted_iota(jnp.int32, (LANES, N_EXPERTS), 0)
              == lax.shift_right_logical(lax.broadcasted_iota(jnp.int32, (LANES, N_EXPERTS), 1), 5)).astype(BF16)
    v = jnp.where(_dot(gsel, spread) > 0.5, biased, NEG)

    onehot = jnp.zeros((tm, N_EXPERTS), F32)
    idxs, wts = [], []
    wsum = jnp.zeros((tm, 1), F32)
    for _ in range(TOP_K):
        m = jnp.max(v, axis=1, keepdims=True)
        idx = jnp.min(jnp.where(v == m, lane, 1e9), axis=1, keepdims=True)
        pick = lane == idx
        wk = jnp.sum(jnp.where(pick, scores, 0.0), axis=1, keepdims=True)
        onehot = jnp.where(pick, 1.0, onehot)
        v = jnp.where(pick, 3.0 * NEG, v)
        idxs.append(idx)
        wts.append(wk)
        wsum = wsum + wk

    earlier = (lax.broadcasted_iota(jnp.int32, (tm, tm), 0) > lax.broadcasted_iota(jnp.int32, (tm, tm), 1))
    before = _dot(earlier.astype(BF16), onehot.astype(BF16)) + run_scr[...]
    e_out = jnp.zeros((tm, LANES), jnp.int32)
    r_out = jnp.zeros((tm, LANES), jnp.int32)
    w_out = jnp.zeros((tm, LANES), F32)
    for k in range(TOP_K):
        rk = jnp.sum(jnp.where(lane == idxs[k], before, 0.0), axis=1, keepdims=True)
        e_out = jnp.where(gcol == k, idxs[k].astype(jnp.int32), e_out)
        r_out = jnp.where(gcol == k, rk.astype(jnp.int32), r_out)
        w_out = jnp.where(gcol == k, wts[k] / wsum * ROUTED_SCALE, w_out)
    e_ref[...] = e_out
    r_ref[...] = r_out
    w_ref[...] = w_out
    run_scr[...] = run_scr[...] + jnp.sum(onehot, axis=0, keepdims=True)
    cnt_ref[...] = run_scr[...]


def _router(h2, w_router, bias, n_tok, tm=ROUTER_TM):
    tile = pl.BlockSpec((tm, LANES), lambda i: (i, 0))
    return pl.pallas_call(
        functools.partial(_router_kernel, tm=tm),
        grid=(n_tok // tm,),
        in_specs=[pl.BlockSpec((tm * TOK_ROWS, LANES), lambda i: (i, 0)),
                  pl.BlockSpec((D_MODEL, N_EXPERTS), lambda i: (0, 0)),
                  pl.BlockSpec((1, N_EXPERTS), lambda i: (0, 0))],
        out_specs=[tile, tile, tile, pl.BlockSpec((1, N_EXPERTS), lambda i: (0, 0))],
        out_shape=[jax.ShapeDtypeStruct((n_tok, LANES), jnp.int32),
                   jax.ShapeDtypeStruct((n_tok, LANES), F32),
                   jax.ShapeDtypeStruct((n_tok, LANES), jnp.int32),
                   jax.ShapeDtypeStruct((1, N_EXPERTS), F32)],
        scratch_shapes=[pltpu.VMEM((1, N_EXPERTS), F32)],
        compiler_params=_cparams(("arbitrary",)),
        name="moe_router",
    )(h2, w_router, bias)


def _dest_kernel(e_ref, r_ref, st_ref, d_ref):
    e = e_ref[...]
    tm = e.shape[0]
    lane = lax.broadcasted_iota(jnp.int32, (tm, N_EXPERTS), 1)
    col = lax.broadcasted_iota(jnp.int32, (tm, LANES), 1)
    st = st_ref[...]
    out = r_ref[...]
    for k in range(TOP_K):
        sk = jnp.sum(jnp.where(lane == e[:, k:k + 1], st, 0.0), axis=1, keepdims=True)
        out = jnp.where(col == k, out + sk.astype(jnp.int32), out)
    d_ref[...] = out


def _moe_dest(top_e, rank, starts, tm=ROUTER_TM):
    n_tok = top_e.shape[0]
    tile = pl.BlockSpec((tm, LANES), lambda i: (i, 0))
    return pl.pallas_call(
        _dest_kernel,
        grid=(n_tok // tm,),
        in_specs=[tile, tile, pl.BlockSpec((1, N_EXPERTS), lambda i: (0, 0))],
        out_specs=tile,
        out_shape=jax.ShapeDtypeStruct((n_tok, LANES), jnp.int32),
        compiler_params=_cparams(("arbitrary",)),
        name="moe_dest",
    )(top_e, rank, starts.astype(F32).reshape(1, -1))


def _moe_items(counts, starts, ends, n_blocks):
    first_blk = starts // MOE_BM
    last_blk = (ends - 1) // MOE_BM
    n_it = jnp.where(counts > 0, last_blk - first_blk + 1, 0)
    it_end = jnp.cumsum(n_it)
    it_start = it_end - n_it
    total = it_end[-1]
    ii = jnp.arange(n_blocks + N_EXPERTS - 1, dtype=jnp.int32)
    valid = ii < total
    e_of = jnp.searchsorted(it_end, jnp.minimum(ii, total - 1), side="right").astype(jnp.int32)
    e_of = jnp.minimum(e_of, N_EXPERTS - 1)
    blk_of = jnp.where(valid, first_blk[e_of] + ii - it_start[e_of], n_blocks - 1).astype(jnp.int32)
    prev = jnp.concatenate([jnp.full((1,), -1, jnp.int32), blk_of[:-1]])
    first = (valid & (blk_of != prev)).astype(jnp.int32)
    return blk_of, e_of, first, valid.astype(jnp.int32)


def _dispatch_kernel(dest_ref, h_ref, xs_ref, sem, *, tm):
    n_pairs = tm * TOP_K

    def row_copy(src_tok, dst_row):
        return pltpu.make_async_copy(
            h_ref.at[pl.ds(pl.multiple_of(src_tok * TOK_ROWS, TOK_ROWS), TOK_ROWS), :],
            xs_ref.at[pl.ds(pl.multiple_of(dst_row * TOK_ROWS, TOK_ROWS), TOK_ROWS), :], sem)

    def issue(i, carry):
        for k in range(TOP_K):
            row_copy(i, dest_ref[0, 0, i * TOP_K + k]).start(priority=k % 2)
        return carry

    def drain(i, carry):
        for _ in range(TOP_K):
            row_copy(0, 0).wait()
        return carry

    lax.fori_loop(0, tm, issue, 0)
    lax.fori_loop(0, tm, drain, 0)


def _dispatch(dest_tiles, h2, n_pairs, tm=ROUTER_TM):
    n_tiles = dest_tiles.shape[0]
    return pl.pallas_call(
        functools.partial(_dispatch_kernel, tm=tm),
        grid=(n_tiles,),
        in_specs=[pl.BlockSpec((1, 1, tm * TOP_K), lambda i: (i, 0, 0), memory_space=pltpu.SMEM),
                  pl.BlockSpec((tm * TOK_ROWS, LANES), lambda i: (i, 0))],
        out_specs=pl.BlockSpec(memory_space=pl.ANY),
        out_shape=jax.ShapeDtypeStruct((n_pairs * TOK_ROWS, LANES), F32),
        scratch_shapes=[pltpu.SemaphoreType.DMA(())],
        compiler_params=_cparams(("arbitrary",)),
        name="moe_dispatch",
    )(dest_tiles, h2)


def _gmm_kernel(blk_ref, e_ref, first_ref, valid_ref, st_ref, en_ref, xs_ref, wg_ref, wu_ref, wd_ref, ys_ref,
                *, bm):
    i = pl.program_id(0)

    @pl.when(valid_ref[i] == 1)
    def _():
        x = _load_tok_tiles(xs_ref, bm).astype(BF16)
        hid = (_silu(_dot(x, wg_ref[0].astype(BF16))) * _dot(x, wu_ref[0].astype(BF16))).astype(BF16)
        y = _dot(hid, wd_ref[0].astype(BF16))
        e = e_ref[i]
        row = blk_ref[i] * bm + lax.broadcasted_iota(jnp.int32, (bm, 1), 0)
        mine = (row >= st_ref[e]) & (row < en_ref[e])

        @pl.when(first_ref[i] == 1)
        def _():
            _store_tok_tiles(ys_ref, jnp.where(mine, y, 0.0), bm)

        @pl.when(first_ref[i] == 0)
        def _():
            _store_tok_tiles(ys_ref, jnp.where(mine, y, _load_tok_tiles(ys_ref, bm)), bm)


def _moe_gmm(items, starts, ends, xs_sorted, w_gate, w_up, w_down, n_blocks, bm=MOE_BM):
    blk_of, e_of, first, valid = items
    rows = pl.BlockSpec((bm * TOK_ROWS, LANES), lambda i, blk, e, f, v, st, en: (blk[i], 0))
    wspec = lambda a: pl.BlockSpec((1,) + a.shape[1:], lambda i, blk, e, f, v, st, en: (e[i], 0, 0))
    return pl.pallas_call(
        functools.partial(_gmm_kernel, bm=bm),
        grid_spec=pltpu.PrefetchScalarGridSpec(
            num_scalar_prefetch=6,
            grid=(blk_of.shape[0],),
            in_specs=[rows, wspec(w_gate), wspec(w_up), wspec(w_down)],
            out_specs=rows),
        out_shape=jax.ShapeDtypeStruct(xs_sorted.shape, F32),
        compiler_params=_cparams(("arbitrary",)),
        name="moe_experts",
    )(blk_of, e_of, first, valid, starts.astype(jnp.int32), ends.astype(jnp.int32), xs_sorted, w_gate, w_up, w_down)


def _combine_kernel(dest_ref, w_ref, x1_ref, h_ref, g2_ref, wsg_ref, wsu_ref, wsd_ref, fn_ref, ys_ref, o_ref,
                    gbuf, sem, *, tm):
    n_pairs = tm * TOP_K

    def row_copy(src_row, p):
        return pltpu.make_async_copy(
            ys_ref.at[pl.ds(pl.multiple_of(src_row * TOK_ROWS, TOK_ROWS), TOK_ROWS), :],
            gbuf.at[pl.ds(pl.multiple_of(p * TOK_ROWS, TOK_ROWS), TOK_ROWS), :], sem)

    def issue(i, carry):
        for k in range(TOP_K):
            p = i * TOP_K + k
            row_copy(dest_ref[0, 0, p], p).start(priority=k % 2)
        return carry

    def drain(i, carry):
        for _ in range(TOP_K):
            row_copy(0, 0).wait()
        return carry

    lax.fori_loop(0, tm, issue, 0)
    h = _load_tok_tiles(h_ref, tm).astype(BF16)
    hid = (_silu(_dot(h, wsg_ref[...])) * _dot(h, wsu_ref[...])).astype(BF16)
    shared = _dot(hid, wsd_ref[...])
    lax.fori_loop(0, tm, drain, 0)

    w = w_ref[...]
    stride = TOP_K * TOK_ROWS
    slabs = []
    for s in range(TOK_ROWS):
        acc = jnp.zeros((tm, LANES), F32)
        for k in range(TOP_K):
            acc = acc + w[:, k:k + 1] * gbuf[pl.ds(k * TOK_ROWS + s, tm, stride=stride), :]
        slabs.append(acc)
    routed = jnp.concatenate(slabs, axis=1)
    x2 = x1_ref[0] + g2_ref[0] * (routed + shared)
    o_ref[0] = _rms(x2, fn_ref[...])


def _combine(dest_tiles, top_w, x1, h2, gate2, shared, fnorm, ys_sorted, tile0, tm=COMBINE_TM):
    nb, t, _ = x1.shape
    nt = t // tm
    flat = lambda b, i: tile0 + b * nt + i
    mod_spec = _mod_spec(gate2, tm)
    const = lambda a: pl.BlockSpec(a.shape, lambda b, i: (0, 0))
    return pl.pallas_call(
        functools.partial(_combine_kernel, tm=tm),
        grid=(nb, nt),
        in_specs=[pl.BlockSpec((1, 1, tm * TOP_K), lambda b, i: (flat(b, i), 0, 0), memory_space=pltpu.SMEM),
                  pl.BlockSpec((tm, LANES), lambda b, i: (flat(b, i), 0)),
                  pl.BlockSpec((1, tm, D_MODEL), lambda b, i: (b, i, 0)),
                  pl.BlockSpec((tm * TOK_ROWS, LANES), lambda b, i: (flat(b, i), 0)),
                  mod_spec, const(shared[0]), const(shared[1]), const(shared[2]), const(fnorm),
                  pl.BlockSpec(memory_space=pl.ANY)],
        out_specs=pl.BlockSpec((1, tm, D_MODEL), lambda b, i: (b, i, 0)),
        out_shape=jax.ShapeDtypeStruct((nb, t, D_MODEL), F32),
        scratch_shapes=[pltpu.VMEM((tm * TOP_K * TOK_ROWS, LANES), F32), pltpu.SemaphoreType.DMA(())],
        compiler_params=_cparams(("arbitrary", "arbitrary")),
        name="moe_combine",
    )(dest_tiles, top_w, x1, h2, gate2, *shared, fnorm, ys_sorted)


def kernel(x_prompt, x_sample, c_prompt, c_sample, cache_nsa_kv, cache_win_kv, state_hgrn, page_table,
           attn_norm, ffn_norm, final_norm, hg_norm, w_ada, b_ada, w_in, hg_lb,
           cmp_pe, cmp_w1, cmp_b1, cmp_w2, w_out, w_router, router_bias,
           w_gate, w_up, w_down, ws_gate, ws_up, ws_down):
    nbp, t, _ = x_prompt.shape
    nbs, ts, _ = x_sample.shape
    ns = nbs * ts
    n_all = nbp * t + ns
    past_len = page_table.shape[1] * PAGE_SIZE

    c_all = jnp.concatenate([c_prompt, c_sample], axis=0)
    c_all = jnp.pad(c_all, ((0, -c_all.shape[0] % SUBLANES), (0, 0)))
    mod = _ada(c_all, w_ada[0], b_ada[0])
    modp = mod[:nbp].reshape(nbp, 1, 6, D_MODEL)
    mods = jnp.repeat(mod[nbp:nbp + nbs].reshape(nbs, 1, 6, D_MODEL), ts, axis=1).reshape(1, ns, 6, D_MODEL)

    w_pad = _pad_w_in(w_in[0])
    cw = _compress_weights(cmp_pe[0], cmp_w1[0], cmp_b1[0], cmp_w2[0])
    wo_hg, wo_nsa = _split_w_out(w_out[0])

    hg, qpad, kv4, kvw, gates, kvsel, kvwb = _inproj(
        x_prompt, modp[:, :, 1], modp[:, :, 0], attn_norm[0], w_pad, 512)
    hg_out, hg_state_p = _hgrn(hg, hg_lb, jnp.zeros((nbp, HG_HEADS, HG_DK, HG_DK), F32), hg_norm[0],
                               256, HG_CHUNK)
    n_pages_p = t // PAGE_SIZE
    ptp = jnp.arange(nbp * n_pages_p, dtype=jnp.int32).reshape(nbp, n_pages_p)
    kc, vc = _compress(ptp, kv4.reshape(nbp * n_pages_p, PAGE_SIZE, 4 * KV_W), *cw)
    nsa = _nsa_prompt(qpad, gates, kc, vc, kvsel, kvwb)
    x1p, h2p = _outproj(x_prompt, hg_out, nsa, modp[:, :, 2], modp[:, :, 4], modp[:, :, 3],
                        ffn_norm[0], wo_hg, wo_nsa, 512)

    xs = x_sample.reshape(1, ns, D_MODEL)
    hg_s, qpad_s, kv4_s, kvw_s, gates_s, _, _ = _inproj(
        xs, mods[:, :, 1], mods[:, :, 0], attn_norm[0], w_pad, ns)
    hg_out_s, hg_state_s = _hgrn(hg_s.reshape(nbs, ts, 4 * HG_WIDTH), hg_lb, state_hgrn[0], hg_norm[0], ts, ts)
    cache = cache_nsa_kv[0].reshape(-1, PAGE_SIZE, 4 * KV_W)
    win_buf = cache_win_kv[0].reshape(nbs, -1, 2 * KV_W)
    nsa_s = _nsa_sample(page_table, cache, win_buf, cw, qpad_s.reshape(nbs, ts, QPAD_W),
                        gates_s.reshape(nbs, ts, GZ_PAD), kv4_s.reshape(nbs, ts, 4 * KV_W),
                        kvw_s.reshape(nbs, ts, 2 * KV_W)).reshape(1, ns, QPAD_W)
    x1s, h2s = _outproj(xs, hg_out_s.reshape(1, ns, HG_WIDTH), nsa_s, mods[:, :, 2], mods[:, :, 4], mods[:, :, 3],
                        ffn_norm[0], wo_hg, wo_nsa, ns)

    h2 = jnp.concatenate([h2p, h2s], axis=0)
    top_e, top_w, rank, counts = _router(h2, w_router[0].astype(BF16), router_bias[0].reshape(1, -1), n_all)
    counts = counts[0].astype(jnp.int32)
    ends = jnp.cumsum(counts)
    starts = ends - counts
    dest = _moe_dest(top_e, rank, starts)[:, :TOP_K].reshape(-1)
    n_pairs = n_all * TOP_K
    n_blocks = n_pairs // MOE_BM
    items = _moe_items(counts, starts, ends, n_blocks)
    xs_sorted = _dispatch(dest.reshape(n_all // ROUTER_TM, 1, ROUTER_TM * TOP_K), h2, n_pairs)
    ys_sorted = _moe_gmm(items, starts, ends, xs_sorted, w_gate[0], w_up[0], w_down[0], n_blocks)
    dest_c = dest.reshape(n_all // COMBINE_TM, 1, COMBINE_TM * TOP_K)
    shared = (ws_gate[0].astype(BF16), ws_up[0].astype(BF16), ws_down[0].astype(BF16))
    fnorm = final_norm.reshape(1, -1)
    y_prompt = _combine(dest_c, top_w, x1p, h2, modp[:, :, 5], shared, fnorm, ys_sorted, 0)
    y_sample = _combine(dest_c, top_w, x1s, h2, mods[:, :, 5], shared, fnorm, ys_sorted, nbp * t // COMBINE_TM)

    wb = win_buf.shape[1]
    win_p = kvw[:, t - min(WINDOW, t):]
    win_s = jnp.concatenate([win_buf, kvw_s.reshape(nbs, ts, 2 * KV_W)], axis=1)[:, -wb:]
    kv_shape = (4, NSA_KV_HEADS, NSA_HEAD_DIM)
    win_shape = (2, NSA_KV_HEADS, NSA_HEAD_DIM)
    return (y_prompt,
            y_sample.reshape(nbs, ts, D_MODEL),
            kv4.reshape(1, nbp, t, *kv_shape),
            win_p.reshape(1, nbp, -1, *win_shape),
            hg_state_p[None],
            kv4_s.reshape(1, nbs, ts, *kv_shape),
            win_s.reshape(1, nbs, wb, *win_shape),
            hg_state_s[None])
```

```python
import functools

import jax
import jax.numpy as jnp
import numpy as np
from jax import lax
from jax.experimental import pallas as pl
from jax.experimental.pallas import tpu as pltpu

F32 = jnp.float32
BF16 = jnp.bfloat16

D_MODEL = 1024
HG_WIDTH = 512
HG_HEADS = 4
HG_DK = 128
HG_CHUNK = 32
NSA_WIDTH = 512
NSA_HEADS = 8
NSA_HEAD_DIM = 64
NSA_KV_HEADS = 2
NSA_GROUP = 4
KV_W = 128
CMP_BLOCK = 32
CMP_STRIDE = 16
CMP_HIDDEN = 256
SEL_BLOCK = 64
SEL_TOPN = 16
WINDOW = 512
Q_BLOCK = 128
N_EXPERTS = 256
TOP_K = 8
N_GROUPS = 8
TOPK_GROUPS = 4
MOE_D_FF = 256
ROUTED_SCALE = 2.5
RMS_EPS = 1e-6
PAGE_SIZE = 128
IN_COLS = 4 * HG_WIDTH + NSA_WIDTH + 4 * KV_W + 2 * KV_W + 3 * NSA_HEADS

LANES = 128
SUBLANES = 8
TOK_ROWS = D_MODEL // LANES
VMEM_LIMIT = 56 * 1024 * 1024

QPAD_W = NSA_HEADS * LANES
GZ_PAD = LANES
INP_COLS = 4 * HG_WIDTH + QPAD_W + 4 * KV_W + 2 * KV_W + GZ_PAD

NEG = -1e30
PAGES_PER_STEP = 32
MOE_BM = 256
ROUTER_TM = 256
COMBINE_TM = 128


def _cparams(sem):
    return pltpu.CompilerParams(dimension_semantics=sem, vmem_limit_bytes=VMEM_LIMIT)


def _dot(a, b):
    return jnp.dot(a, b, preferred_element_type=F32)


def _dot_nt(a, b):
    return lax.dot_general(a, b, (((1,), (1,)), ((), ())), preferred_element_type=F32)


def _dot_tn(a, b):
    return lax.dot_general(a, b, (((0,), (0,)), ((), ())), preferred_element_type=F32)


def _rms(x, g):
    return x * lax.rsqrt(jnp.mean(x * x, axis=-1, keepdims=True) + RMS_EPS) * g


def _silu(x):
    return x * jax.nn.sigmoid(x)


Q_SCALE = NSA_HEAD_DIM ** -0.5 * 1.4426950408889634


def _masked_softmax(s, valid):
    s = jnp.where(valid, s, NEG)
    m = jnp.max(s, axis=1, keepdims=True)
    p = jnp.exp2(s - m) * valid.astype(F32)
    return p / jnp.maximum(jnp.sum(p, axis=1, keepdims=True), 1e-30)


def _topk_mask(v, k):
    lane = lax.broadcasted_iota(jnp.int32, v.shape, 1).astype(F32)
    sel = jnp.zeros(v.shape, F32)
    for _ in range(k):
        m = jnp.max(v, axis=1, keepdims=True)
        idx = jnp.min(jnp.where(v == m, lane, 1e9), axis=1, keepdims=True)
        pick = lane == idx
        sel = jnp.where(pick, 1.0, sel)
        v = jnp.where(pick, 3.0 * NEG, v)
    return sel


def _mod_spec(mod, tm):
    if mod.shape[1] == 1:
        return pl.BlockSpec((1, 1, D_MODEL), lambda b, i: (b, 0, 0))
    return pl.BlockSpec((1, tm, D_MODEL), lambda b, i: (b, i, 0))


def _load_tok_tiles(ref, n_tok):
    return jnp.concatenate([ref[pl.ds(s, n_tok, stride=TOK_ROWS), :] for s in range(TOK_ROWS)], axis=1)


def _store_tok_tiles(ref, val, n_tok):
    for s in range(TOK_ROWS):
        ref[pl.ds(s, n_tok, stride=TOK_ROWS), :] = val[:, s * LANES:(s + 1) * LANES]


def _ada_kernel(c_ref, w_ref, b_ref, o_ref):
    s = _silu(c_ref[...]).astype(BF16)
    o_ref[...] = _dot(s, w_ref[...].astype(BF16)) + b_ref[...]


def _ada(c_all, w_ada, b_ada):
    n = c_all.shape[0]
    return pl.pallas_call(
        _ada_kernel,
        grid=(6,),
        in_specs=[pl.BlockSpec((n, D_MODEL), lambda j: (0, 0)),
                  pl.BlockSpec((D_MODEL, D_MODEL), lambda j: (0, j)),
                  pl.BlockSpec((1, D_MODEL), lambda j: (0, j))],
        out_specs=pl.BlockSpec((n, D_MODEL), lambda j: (0, j)),
        out_shape=jax.ShapeDtypeStruct((n, 6 * D_MODEL), F32),
        compiler_params=_cparams(("arbitrary",)),
        name="ada_mod",
    )(c_all, w_ada, b_ada.reshape(1, -1))


def _inproj_kernel(x_ref, sc_ref, sh_ref, g_ref, w_ref,
                   hg_ref, q_ref, kv4_ref, kvw_ref, gate_ref, kvsel_ref, kvwb_ref):
    h = _rms(x_ref[0], g_ref[...]) * (1.0 + sc_ref[0]) + sh_ref[0]
    z = _dot(h.astype(BF16), w_ref[...])
    c0 = 4 * HG_WIDTH
    hg_ref[0] = z[:, :c0]
    q_ref[0] = (z[:, c0:c0 + QPAD_W] * Q_SCALE).astype(BF16)
    c1 = c0 + QPAD_W
    kv4 = z[:, c1:c1 + 4 * KV_W]
    kv4_ref[0] = kv4
    kvsel_ref[0] = kv4[:, 2 * KV_W:].astype(BF16)
    c2 = c1 + 4 * KV_W
    kvw = z[:, c2:c2 + 2 * KV_W]
    kvw_ref[0] = kvw
    kvwb_ref[0] = kvw.astype(BF16)
    gate_ref[0] = jax.nn.sigmoid(z[:, c2 + 2 * KV_W:])


def _inproj(x, scale, shift, g_norm, w_pad, tm):
    nb, t, _ = x.shape
    mod_spec = _mod_spec(scale, tm)
    widths = [(4 * HG_WIDTH, F32), (QPAD_W, BF16), (4 * KV_W, F32), (2 * KV_W, F32), (GZ_PAD, F32),
              (2 * KV_W, BF16), (2 * KV_W, BF16)]
    return pl.pallas_call(
        _inproj_kernel,
        grid=(nb, t // tm),
        in_specs=[pl.BlockSpec((1, tm, D_MODEL), lambda b, i: (b, i, 0)),
                  mod_spec, mod_spec,
                  pl.BlockSpec((1, D_MODEL), lambda b, i: (0, 0)),
                  pl.BlockSpec((D_MODEL, INP_COLS), lambda b, i: (0, 0))],
        out_specs=[pl.BlockSpec((1, tm, w), lambda b, i: (b, i, 0)) for w, _ in widths],
        out_shape=[jax.ShapeDtypeStruct((nb, t, w), dt) for w, dt in widths],
        compiler_params=_cparams(("arbitrary", "arbitrary")),
        name="in_proj",
    )(x, scale, shift, g_norm.reshape(1, -1), w_pad)


def _pad_w_in(w_in):
    c0 = 4 * HG_WIDTH
    wq = w_in[:, c0:c0 + NSA_WIDTH].reshape(D_MODEL, NSA_HEADS, NSA_HEAD_DIM)
    zeros = jnp.zeros_like(wq)
    lo = jnp.concatenate([wq, zeros], axis=-1)
    hi = jnp.concatenate([zeros, wq], axis=-1)
    grp = (jnp.arange(NSA_HEADS) // NSA_GROUP)[None, :, None]
    wq_pad = jnp.where(grp == 0, lo, hi).reshape(D_MODEL, QPAD_W)
    c1 = c0 + NSA_WIDTH
    rest = w_in[:, c1:c1 + 6 * KV_W]
    gz = jnp.pad(w_in[:, c1 + 6 * KV_W:], ((0, 0), (0, GZ_PAD - 3 * NSA_HEADS)))
    return jnp.concatenate([w_in[:, :c0], wq_pad, rest, gz], axis=1).astype(BF16)


def _hgrn_kernel(q_ref, f_ref, v_ref, gt_ref, lb_ref, s0_ref, gn_ref, o_ref, s_out_ref, st_scr,
                 *, chunk, n_chunks):
    i = pl.program_id(2)

    @pl.when(i == 0)
    def _():
        st_scr[...] = s0_ref[0, 0].T

    lbr = lb_ref[...]
    e = jnp.exp(lbr - jnp.max(lbr, axis=0, keepdims=True))
    lb = e[0:1] / jnp.sum(e, axis=0, keepdims=True)
    row = lax.broadcasted_iota(jnp.int32, (chunk, HG_DK), 0)
    causal = (lax.broadcasted_iota(jnp.int32, (chunk, chunk), 0)
              >= lax.broadcasted_iota(jnp.int32, (chunk, chunk), 1))
    st = st_scr[...]
    for c in range(n_chunks):
        sl = pl.ds(c * chunk, chunk)
        z = f_ref[0, sl, :]
        log_f = jnp.log(lb + (1.0 - lb) * jax.nn.sigmoid(z))
        kk = (1.0 - lb) * jax.nn.sigmoid(-z)
        a = log_f
        s = 1
        while s < chunk:
            a = a + jnp.where(row >= s, pltpu.roll(a, s, 0), 0.0)
            s *= 2
        qt = (q_ref[0, sl, :] * jnp.exp(a)).astype(BF16)
        kt = (kk * jnp.exp(-a)).astype(BF16)
        v = v_ref[0, sl, :].astype(BF16)
        att = jnp.where(causal, _dot_nt(qt, kt), 0.0)
        o = _dot(att.astype(BF16), v) + _dot_nt(qt, st.astype(BF16))
        a_end = a[chunk - 1:chunk, :]
        kd = (kk * jnp.exp(a_end - a)).astype(BF16)
        st = st * jnp.exp(a_end) + _dot_tn(v, kd)
        o = _rms(o, gn_ref[...]) * _silu(gt_ref[0, sl, :])
        o_ref[0, sl, :] = o.astype(o_ref.dtype)
    st_scr[...] = st

    @pl.when(i == pl.num_programs(2) - 1)
    def _():
        s_out_ref[0, 0] = st.T


def _hgrn(hg, hg_lb, s0, g_norm, tc, chunk):
    nb, t, _ = hg.shape
    col = lambda k: (lambda b, h, i: (b, i, k * HG_HEADS + h))
    st_spec = pl.BlockSpec((1, 1, HG_DK, HG_DK), lambda b, h, i: (b, h, 0, 0))
    return pl.pallas_call(
        functools.partial(_hgrn_kernel, chunk=chunk, n_chunks=tc // chunk),
        grid=(nb, HG_HEADS, t // tc),
        in_specs=[pl.BlockSpec((1, tc, HG_DK), col(0)),
                  pl.BlockSpec((1, tc, HG_DK), col(1)),
                  pl.BlockSpec((1, tc, HG_DK), col(2)),
                  pl.BlockSpec((1, tc, HG_DK), col(3)),
                  pl.BlockSpec((hg_lb.shape[0], HG_DK), lambda b, h, i: (0, h)),
                  st_spec,
                  pl.BlockSpec((1, HG_DK), lambda b, h, i: (0, 0))],
        out_specs=[pl.BlockSpec((1, tc, HG_DK), lambda b, h, i: (b, i, h)), st_spec],
        out_shape=[jax.ShapeDtypeStruct((nb, t, HG_WIDTH), BF16),
                   jax.ShapeDtypeStruct((nb, HG_HEADS, HG_DK, HG_DK), F32)],
        scratch_shapes=[pltpu.VMEM((HG_DK, HG_DK), F32)],
        compiler_params=_cparams(("arbitrary", "arbitrary", "arbitrary")),
        name="hgrn2",
    )(hg, hg, hg, hg, hg_lb, s0, g_norm.reshape(1, -1))


def _gelu_tanh(x):
    return 0.5 * x * (1.0 + jnp.tanh(0.7978845608028654 * (x + 0.044715 * x * x * x)))


def _page_copies(pt_ref, cache_ref, buf, sem, b, s, slot, n_pages, pps, col0, tail):
    copies = []
    base = s * pps
    nxt = pt_ref[b, jnp.minimum(base + pps, n_pages - 1)]
    for br in range(2):
        cols = pl.ds(col0 + br * KV_W, KV_W)
        for i in range(pps):
            pg = pt_ref[b, base + i]
            copies.append(pltpu.make_async_copy(
                cache_ref.at[pg, :, cols],
                buf.at[slot, br, pl.ds(i * PAGE_SIZE, PAGE_SIZE), :], sem.at[slot]))
        if tail:
            copies.append(pltpu.make_async_copy(
                cache_ref.at[nxt, pl.ds(0, CMP_STRIDE), cols],
                buf.at[slot, br, pl.ds(pps * PAGE_SIZE, CMP_STRIDE), :], sem.at[slot]))
    return copies


def _stream_pages(pt_ref, cache_ref, buf, sem, n_pages, n_steps, pps, col0, tail):
    b = pl.program_id(0)
    s = pl.program_id(1)
    n = b * n_steps + s
    total = pl.num_programs(0) * n_steps
    slot = n % 2
    args = (n_pages, pps, col0, tail)

    @pl.when(n == 0)
    def _():
        for cp in _page_copies(pt_ref, cache_ref, buf, sem, b, s, slot, *args):
            cp.start()

    @pl.when(n + 1 < total)
    def _():
        n1 = n + 1
        for cp in _page_copies(pt_ref, cache_ref, buf, sem, n1 // n_steps, n1 % n_steps, 1 - slot, *args):
            cp.start()

    for cp in _page_copies(pt_ref, cache_ref, buf, sem, b, s, slot, *args):
        cp.wait()
    return slot


def _compress_kernel(pt_ref, cache_ref, pe_ref, w1_ref, b1_ref, w2_ref, kc_ref, vc_ref, buf, sem,
                     *, n_pages, n_steps, pps):
    groups = pps * PAGE_SIZE // CMP_STRIDE
    slot = _stream_pages(pt_ref, cache_ref, buf, sem, n_pages, n_steps, pps, 0, True)

    for br, out_ref in ((0, kc_ref), (1, vc_ref)):
        acc = jnp.zeros((groups, 2 * CMP_HIDDEN), F32)
        for l in range(CMP_BLOCK):
            xl = buf[slot, br, pl.ds(l, groups, stride=CMP_STRIDE), :]
            xl = (xl + pe_ref[br, l:l + 1, :]).astype(BF16)
            acc = acc + _dot(xl, w1_ref[br, l])
        hid = _gelu_tanh(acc + b1_ref[br]).astype(BF16)
        out_ref[0] = _dot(hid, w2_ref[br]).astype(out_ref.dtype)


def _compress(page_table, cache, pe2, w1cat, b1cat, w2bd, pps=PAGES_PER_STEP):
    nb, n_pages = page_table.shape
    n_steps = n_pages // pps
    groups = pps * PAGE_SIZE // CMP_STRIDE
    rows = pps * PAGE_SIZE + CMP_STRIDE
    const = lambda shape: pl.BlockSpec(shape, lambda b, s, pt: (0,) * len(shape))
    out_spec = pl.BlockSpec((1, groups, KV_W), lambda b, s, pt: (b, s, 0))
    out_sds = jax.ShapeDtypeStruct((nb, n_steps * groups, KV_W), BF16)
    return pl.pallas_call(
        functools.partial(_compress_kernel, n_pages=n_pages, n_steps=n_steps, pps=pps),
        grid_spec=pltpu.PrefetchScalarGridSpec(
            num_scalar_prefetch=1,
            grid=(nb, n_steps),
            in_specs=[pl.BlockSpec(memory_space=pl.ANY),
                      const((2, CMP_BLOCK, KV_W)),
                      const((2, CMP_BLOCK, KV_W, 2 * CMP_HIDDEN)),
                      const((2, 1, 2 * CMP_HIDDEN)),
                      const((2, 2 * CMP_HIDDEN, KV_W))],
            out_specs=[out_spec, out_spec],
            scratch_shapes=[pltpu.VMEM((2, 2, rows, KV_W), F32), pltpu.SemaphoreType.DMA((2,))]),
        out_shape=[out_sds, out_sds],
        compiler_params=_cparams(("arbitrary", "arbitrary")),
        name="nsa_compress",
    )(page_table, cache, pe2, w1cat, b1cat, w2bd)


def _compress_weights(cmp_pe, cmp_w1, cmp_b1, cmp_w2):
    pe2 = jnp.concatenate([cmp_pe, cmp_pe], axis=-1)
    w1 = cmp_w1.reshape(2, CMP_BLOCK, NSA_HEAD_DIM, CMP_HIDDEN)
    z1 = jnp.zeros_like(w1)
    w1cat = jnp.concatenate([jnp.concatenate([w1, z1], axis=-1),
                             jnp.concatenate([z1, w1], axis=-1)], axis=2).astype(BF16)
    b1cat = jnp.concatenate([cmp_b1, cmp_b1], axis=-1)[:, None, :]
    z2 = jnp.zeros_like(cmp_w2)
    w2bd = jnp.concatenate([jnp.concatenate([cmp_w2, z2], axis=-1),
                            jnp.concatenate([z2, cmp_w2], axis=-1)], axis=1).astype(BF16)
    return pe2, w1cat, b1cat, w2bd


def _overlap_matrix(n_cmp, n_sel):
    cs = lax.broadcasted_iota(jnp.int32, (n_cmp, n_sel), 0) * CMP_STRIDE
    ss = lax.broadcasted_iota(jnp.int32, (n_cmp, n_sel), 1) * SEL_BLOCK
    return ((cs < ss + SEL_BLOCK) & (cs + CMP_BLOCK > ss)).astype(BF16)


SEL_CHUNK = 512
SPREAD_KEYS = 1024
WIN_SPAN = WINDOW + Q_BLOCK


def _nsa_prompt_kernel(q_ref, gt_ref, kc_ref, vc_ref, ks_ref, vs_ref, kw_ref, vw_ref, ex_ref, o_ref,
                       m_scr, acc_scr, chosen_scr, oc_scr, sa_scr, sb_scr, *, n_cmp, n_sel):
    j = pl.program_id(1)
    q0 = j * Q_BLOCK
    tok = lax.broadcasted_iota(jnp.int32, (Q_BLOCK, 1), 0) + q0
    tok4 = jnp.concatenate([tok] * NSA_GROUP, axis=0)
    n_chunks = j // (SEL_CHUNK // Q_BLOCK) + 1
    key_in_span = lax.broadcasted_iota(jnp.int32, (1, SPREAD_KEYS), 1)

    def load_q(g):
        return jnp.concatenate([q_ref[0, :, (NSA_GROUP * g + jh) * LANES:(NSA_GROUP * g + jh + 1) * LANES]
                                for jh in range(NSA_GROUP)], axis=0)

    def softmax_av(s, valid, v):
        s = jnp.where(valid, s, NEG)
        tiles = [s[:, i * LANES:(i + 1) * LANES] for i in range(s.shape[1] // LANES)]
        m = jnp.max(functools.reduce(jnp.maximum, tiles), axis=1, keepdims=True)
        p = jnp.where(valid, jnp.exp2(s - m), 0.0)
        acc = _dot(p.astype(BF16), jnp.concatenate([v, jnp.ones(v.shape, BF16)], axis=1))
        inv = 1.0 / jnp.maximum(acc[:, KV_W:], 1e-30)
        return p, acc[:, :KV_W] * inv, inv

    ov = _overlap_matrix(n_cmp, n_sel)
    cend = lax.broadcasted_iota(jnp.int32, (1, n_cmp), 1) * CMP_STRIDE + (CMP_BLOCK - 1)
    ws = pl.multiple_of(jnp.maximum(q0 - WINDOW, 0), Q_BLOCK)
    wpos = ws + lax.broadcasted_iota(jnp.int32, (1, WIN_SPAN), 1)
    d = tok4 - wpos
    in_window = (d >= 0) & (d < WINDOW)
    gates = gt_ref[0]
    imps = []
    for g in range(NSA_KV_HEADS):
        q = load_q(g)
        p, o_c, inv = softmax_av(_dot_nt(q, kc_ref[0]), cend <= tok4, vc_ref[0])
        _, o_w, _ = softmax_av(_dot_nt(q, kw_ref[0, pl.ds(ws, WIN_SPAN), :]), in_window,
                               vw_ref[0, pl.ds(ws, WIN_SPAN), :])
        psum = jnp.zeros((Q_BLOCK, n_cmp), F32)
        for jh in range(NSA_GROUP):
            h = NSA_GROUP * g + jh
            r = slice(jh * Q_BLOCK, (jh + 1) * Q_BLOCK)
            psum = psum + p[r] * jnp.concatenate([inv[r]] * (n_cmp // LANES), axis=1)
            oc_scr[g, r, :] = gates[:, 3 * h:3 * h + 1] * o_c[r] + gates[:, 3 * h + 2:3 * h + 3] * o_w[r]
        imps.append(_dot(psum.astype(BF16), ov))

    blk = lax.broadcasted_iota(jnp.int32, (1, n_sel), 1)
    cur = lax.shift_right_logical(tok, 6)
    forced = (blk == 0) | (blk == cur) | (blk == cur - 1)
    free = (blk * SEL_BLOCK <= tok) & jnp.logical_not(forced)
    forced2 = jnp.concatenate([forced] * NSA_KV_HEADS, axis=0)
    free2 = jnp.concatenate([free] * NSA_KV_HEADS, axis=0)
    best = _topk_mask(jnp.where(free2, jnp.concatenate(imps, axis=0), NEG), SEL_TOPN - 3)
    sel2 = jnp.where(forced2, 1.0, best).astype(BF16)

    ones_blk = jnp.ones((SEL_CHUNK, KV_W), BF16)

    for g in range(NSA_KV_HEADS):
        sel = sel2[g * Q_BLOCK:(g + 1) * Q_BLOCK]
        m_scr[...] = jnp.full(m_scr.shape, NEG, F32)
        acc_scr[...] = jnp.zeros(acc_scr.shape, F32)

        def spread(i, carry):
            c0 = pl.multiple_of(i * SPREAD_KEYS, SPREAD_KEYS)
            keep = (_dot(sel, ex_ref[:, pl.ds(c0, SPREAD_KEYS)]) > 0.5) & (key_in_span + c0 <= tok)
            chosen_scr[:, pl.ds(c0, SPREAD_KEYS)] = jnp.where(keep, 0.0, NEG)
            return carry

        lax.fori_loop(0, (n_chunks * SEL_CHUNK + SPREAD_KEYS - 1) // SPREAD_KEYS, spread, 0)

        heads = range(NSA_GROUP)
        rows = [pl.ds(jh * Q_BLOCK, Q_BLOCK) for jh in heads]

        def stage_scores(c, buf):
            k0 = pl.multiple_of(c * SEL_CHUNK, SEL_CHUNK)
            bias = chosen_scr[:, pl.ds(k0, SEL_CHUNK)]
            kblk = ks_ref[0, pl.ds(k0, SEL_CHUNK), :]
            for jh in heads:
                h = NSA_GROUP * g + jh
                buf[rows[jh], :] = _dot_nt(q_ref[0, :, h * LANES:(h + 1) * LANES], kblk) + bias

        def accumulate(c, buf):
            k0 = pl.multiple_of(c * SEL_CHUNK, SEL_CHUNK)
            vext = jnp.concatenate([vs_ref[0, pl.ds(k0, SEL_CHUNK), :], ones_blk], axis=1)
            tiles = [[buf[r, i * LANES:(i + 1) * LANES] for i in range(SEL_CHUNK // LANES)] for r in rows]
            m_old = [m_scr[r, :] for r in rows]
            m_new = [jnp.maximum(mo, jnp.max(functools.reduce(jnp.maximum, t), axis=1, keepdims=True))
                     for mo, t in zip(m_old, tiles)]
            probs = [jnp.concatenate([jnp.exp2(x - mn) for x in t], axis=1).astype(BF16)
                     for mn, t in zip(m_new, tiles)]
            pv = [_dot(p, vext) for p in probs]
            for r, mo, mn, y in zip(rows, m_old, m_new, pv):
                alpha = jnp.exp2(mo - mn)
                acc_scr[r, :] = jnp.concatenate([alpha, alpha], axis=1) * acc_scr[r, :] + y
                m_scr[r, :] = mn

        n_pairs = (n_chunks + 1) // 2
        stage_scores(0, sa_scr)

        def body(i, carry):
            stage_scores(2 * i + 1, sb_scr)
            accumulate(2 * i, sa_scr)
            stage_scores(jnp.minimum(2 * i + 2, 2 * n_pairs - 1), sa_scr)
            accumulate(2 * i + 1, sb_scr)
            return carry

        lax.fori_loop(0, n_pairs, body, 0)
        o_s = acc_scr[:, :KV_W] / jnp.maximum(acc_scr[:, KV_W:], 1e-30)

        for jh in range(NSA_GROUP):
            h = NSA_GROUP * g + jh
            r = slice(jh * Q_BLOCK, (jh + 1) * Q_BLOCK)
            o = oc_scr[g, r, :] + gates[:, 3 * h + 1:3 * h + 2] * o_s[r]
            o_ref[0, :, h * LANES:(h + 1) * LANES] = o.astype(o_ref.dtype)


def _nsa_prompt(qpad, gates, kc, vc, kvsel, kvwb):
    nb, t, _ = qpad.shape
    assert t % SPREAD_KEYS == 0 and SPREAD_KEYS == 2 * SEL_CHUNK and t >= WIN_SPAN
    n_cmp = kc.shape[1]
    n_sel = t // SEL_BLOCK
    full = lambda w, k: pl.BlockSpec((1, t, w), lambda b, j: (b, 0, k))
    t_pad = -(-t // SPREAD_KEYS) * SPREAD_KEYS
    expand = jnp.asarray(np.arange(n_sel)[:, None] == (np.arange(t_pad)[None, :] // SEL_BLOCK), BF16)
    return pl.pallas_call(
        functools.partial(_nsa_prompt_kernel, n_cmp=n_cmp, n_sel=n_sel),
        grid=(nb, t // Q_BLOCK),
        in_specs=[pl.BlockSpec((1, Q_BLOCK, QPAD_W), lambda b, j: (b, j, 0)),
                  pl.BlockSpec((1, Q_BLOCK, GZ_PAD), lambda b, j: (b, j, 0)),
                  pl.BlockSpec((1, n_cmp, KV_W), lambda b, j: (b, 0, 0)),
                  pl.BlockSpec((1, n_cmp, KV_W), lambda b, j: (b, 0, 0)),
                  full(KV_W, 0), full(KV_W, 1), full(KV_W, 0), full(KV_W, 1),
                  pl.BlockSpec((n_sel, t_pad), lambda b, j: (0, 0))],
        out_specs=pl.BlockSpec((1, Q_BLOCK, QPAD_W), lambda b, j: (b, j, 0)),
        out_shape=jax.ShapeDtypeStruct((nb, t, QPAD_W), BF16),
        scratch_shapes=[pltpu.VMEM((NSA_GROUP * Q_BLOCK, LANES), F32),
                        pltpu.VMEM((NSA_GROUP * Q_BLOCK, 2 * KV_W), F32),
                        pltpu.VMEM((Q_BLOCK, t_pad), F32),
                        pltpu.VMEM((NSA_KV_HEADS, NSA_GROUP * Q_BLOCK, KV_W), F32),
                        pltpu.VMEM((NSA_GROUP * Q_BLOCK, SEL_CHUNK), F32),
                        pltpu.VMEM((NSA_GROUP * Q_BLOCK, SEL_CHUNK), F32)],
        compiler_params=_cparams(("arbitrary", "arbitrary")),
        name="nsa_prompt",
    )(qpad, gates, kc, vc, kvsel, kvsel, kvwb, kvwb, expand)


def _nsa_sample_a_kernel(q_ref, g_ref, kc_ref, vc_ref, wb_ref, nw_ref, ocw_ref, sel_ref,
                         *, past_len, n_tok, n_sel, n_sel_pad):
    q = q_ref[0]
    rows = q.shape[0]
    n_cmp = kc_ref.shape[1]
    t_row = lax.broadcasted_iota(jnp.int32, (rows, 1), 0) & (n_tok - 1)
    qpos = past_len + t_row
    cend = lax.broadcasted_iota(jnp.int32, (1, n_cmp), 1) * CMP_STRIDE + (CMP_BLOCK - 1)
    p_c = _masked_softmax(_dot_nt(q, kc_ref[0]), cend <= qpos)
    o_c = _dot(p_c.astype(BF16), vc_ref[0])

    per_grp = NSA_GROUP * n_tok
    psum = jnp.concatenate(
        [sum(p_c[g * per_grp + jh * n_tok:g * per_grp + (jh + 1) * n_tok] for jh in range(NSA_GROUP))
         for g in range(NSA_KV_HEADS)], axis=0)
    imp = _dot(psum.astype(BF16), _overlap_matrix(n_cmp, n_sel_pad))
    blk = lax.broadcasted_iota(jnp.int32, (1, n_sel_pad), 1)
    tq = past_len + (lax.broadcasted_iota(jnp.int32, (NSA_KV_HEADS * n_tok, 1), 0) & (n_tok - 1))
    cur = lax.shift_right_logical(tq, 6)
    forced = (blk == 0) | (blk == cur) | (blk == cur - 1)
    allowed = blk * SEL_BLOCK <= tq
    v = jnp.where(forced, -NEG, jnp.where(allowed, imp, NEG))
    sel_ref[0] = _topk_mask(jnp.where(blk < n_sel, v, 2.0 * NEG), SEL_TOPN)

    wb = wb_ref.shape[1]
    kw = wb_ref[0, :, 0:KV_W].astype(BF16)
    vw = wb_ref[0, :, KV_W:2 * KV_W].astype(BF16)
    kn = nw_ref[0, :, 0:KV_W].astype(BF16)
    vn = nw_ref[0, :, KV_W:2 * KV_W].astype(BF16)
    i1 = lax.broadcasted_iota(jnp.int32, (1, wb), 1)
    d1 = t_row + wb - i1
    valid1 = (d1 >= 0) & (d1 < WINDOW) & (past_len - wb + i1 >= 0)
    i2 = lax.broadcasted_iota(jnp.int32, (1, nw_ref.shape[1]), 1)
    d2 = t_row - i2
    valid2 = (d2 >= 0) & (d2 < WINDOW) & (i2 < n_tok)
    s1 = jnp.where(valid1, _dot_nt(q, kw), NEG)
    s2 = jnp.where(valid2, _dot_nt(q, kn), NEG)
    m = jnp.maximum(jnp.max(s1, axis=1, keepdims=True), jnp.max(s2, axis=1, keepdims=True))
    p1 = jnp.exp2(s1 - m) * valid1.astype(F32)
    p2 = jnp.exp2(s2 - m) * valid2.astype(F32)
    den = jnp.maximum(jnp.sum(p1, axis=1, keepdims=True) + jnp.sum(p2, axis=1, keepdims=True), 1e-30)
    o_w = (_dot(p1.astype(BF16), vw) + _dot(p2.astype(BF16), vn)) / den
    g = g_ref[0]
    ocw_ref[0] = g[:, 0:1] * o_c + g[:, 2:3] * o_w


def _nsa_sample_a(q_rows, g_rows, kc, vc, win_buf, new_win, *, past_len, n_tok):
    nb, rows, _ = q_rows.shape
    n_sel = -(-(past_len + n_tok) // SEL_BLOCK)
    n_sel_pad = -(-n_sel // LANES) * LANES
    blk3 = lambda a: pl.BlockSpec((1,) + a.shape[1:], lambda b: (b, 0, 0))
    return pl.pallas_call(
        functools.partial(_nsa_sample_a_kernel, past_len=past_len, n_tok=n_tok, n_sel=n_sel, n_sel_pad=n_sel_pad),
        grid=(nb,),
        in_specs=[blk3(q_rows), blk3(g_rows), blk3(kc), blk3(vc), blk3(win_buf), blk3(new_win)],
        out_specs=[pl.BlockSpec((1, rows, KV_W), lambda b: (b, 0, 0)),
                   pl.BlockSpec((1, NSA_KV_HEADS * n_tok, n_sel_pad), lambda b: (b, 0, 0))],
        out_shape=[jax.ShapeDtypeStruct((nb, rows, KV_W), F32),
                   jax.ShapeDtypeStruct((nb, NSA_KV_HEADS * n_tok, n_sel_pad), F32)],
        compiler_params=_cparams(("arbitrary",)),
        name="nsa_sample_a",
    )(q_rows, g_rows, kc, vc, win_buf, new_win)


def _nsa_sample_b_kernel(pt_ref, cache_ref, q_ref, g_ref, sel_ref, seln_ref, ns_ref, ocw_ref, ex_ref, o_ref,
                         buf, sem, m_scr, l_scr, acc_scr, *, n_pages, n_steps, pps, n_tok):
    s = pl.program_id(1)
    slot = _stream_pages(pt_ref, cache_ref, buf, sem, n_pages, n_steps, pps, 2 * KV_W, False)
    q = q_ref[0]
    rows = q.shape[0]

    @pl.when(s == 0)
    def _():
        m_scr[...] = jnp.full(m_scr.shape, NEG, F32)
        l_scr[...] = jnp.zeros(l_scr.shape, F32)
        acc_scr[...] = jnp.zeros(acc_scr.shape, F32)

    def update(scores, msk, v):
        sc = jnp.where(msk, scores, NEG)
        m_old = m_scr[...]
        m_new = jnp.maximum(m_old, jnp.max(sc, axis=1, keepdims=True))
        p = jnp.exp2(sc - m_new) * msk.astype(F32)
        alpha = jnp.exp2(m_old - m_new)
        l_scr[...] = alpha * l_scr[...] + jnp.sum(p, axis=1, keepdims=True)
        acc_scr[...] = alpha * acc_scr[...] + _dot(p.astype(BF16), v)
        m_scr[...] = m_new

    chosen = _dot(sel_ref[0, 0], ex_ref[...]) > 0.5
    update(_dot_nt(q, buf[slot, 0].astype(BF16)), chosen, buf[slot, 1].astype(BF16))

    @pl.when(s == n_steps - 1)
    def _():
        t_row = lax.broadcasted_iota(jnp.int32, (rows, 1), 0) & (n_tok - 1)
        i2 = lax.broadcasted_iota(jnp.int32, (1, ns_ref.shape[1]), 1)
        msk = (seln_ref[0, 0][:, 0:1] > 0.5) & (i2 <= t_row) & (i2 < n_tok)
        update(_dot_nt(q, ns_ref[0, :, 0:KV_W].astype(BF16)), msk, ns_ref[0, :, KV_W:2 * KV_W].astype(BF16))
        o_s = acc_scr[...] / jnp.maximum(l_scr[...], 1e-30)
        o_ref[0] = ocw_ref[0] + g_ref[0][:, 1:2] * o_s


def _nsa_sample_b(page_table, cache, q_rows, g_rows, sel_steps, new_sel, ocw, *, n_tok, pps=PAGES_PER_STEP):
    nb, n_pages = page_table.shape
    n_steps = n_pages // pps
    rows = q_rows.shape[1]
    keys = pps * PAGE_SIZE
    expand = jnp.asarray(np.arange(LANES)[:, None] == (np.arange(keys)[None, :] // SEL_BLOCK), BF16)
    per_b = lambda a: pl.BlockSpec((1,) + a.shape[1:], lambda b, s, pt: (b, 0, 0))
    return pl.pallas_call(
        functools.partial(_nsa_sample_b_kernel, n_pages=n_pages, n_steps=n_steps, pps=pps, n_tok=n_tok),
        grid_spec=pltpu.PrefetchScalarGridSpec(
            num_scalar_prefetch=1,
            grid=(nb, n_steps),
            in_specs=[pl.BlockSpec(memory_space=pl.ANY),
                      per_b(q_rows), per_b(g_rows),
                      pl.BlockSpec((1, 1, rows, LANES), lambda b, s, pt: (b, s, 0, 0)),
                      pl.BlockSpec((1, 1, rows, LANES), lambda b, s, pt: (b, n_steps, 0, 0)),
                      per_b(new_sel), per_b(ocw),
                      pl.BlockSpec((LANES, keys), lambda b, s, pt: (0, 0))],
            out_specs=pl.BlockSpec((1, rows, KV_W), lambda b, s, pt: (b, 0, 0)),
            scratch_shapes=[pltpu.VMEM((2, 2, keys, KV_W), F32), pltpu.SemaphoreType.DMA((2,)),
                            pltpu.VMEM((rows, 1), F32), pltpu.VMEM((rows, 1), F32),
                            pltpu.VMEM((rows, KV_W), F32)]),
        out_shape=jax.ShapeDtypeStruct((nb, rows, KV_W), F32),
        compiler_params=_cparams(("arbitrary", "arbitrary")),
        name="nsa_sample_b",
    )(page_table, cache, q_rows, g_rows, sel_steps, sel_steps, new_sel, ocw, expand)


def _nsa_sample(page_table, cache, win_buf, cw, qpad, gates, kv4, kvw, pps=PAGES_PER_STEP):
    nb, ts, _ = qpad.shape
    past_len = page_table.shape[1] * PAGE_SIZE
    kc, vc = _compress(page_table, cache, *cw, pps=pps)
    rows = NSA_HEADS * ts
    q_rows = qpad.reshape(nb, ts, NSA_HEADS, LANES).transpose(0, 2, 1, 3).reshape(nb, rows, LANES)
    g_rows = gates[:, :, :3 * NSA_HEADS].reshape(nb, ts, NSA_HEADS, 3).transpose(0, 2, 1, 3)
    g_rows = jnp.pad(g_rows.reshape(nb, rows, 3), ((0, 0), (0, 0), (0, LANES - 3)))
    pad_rows = lambda a: jnp.pad(a, ((0, 0), (0, LANES - ts), (0, 0)))
    new_win = pad_rows(kvw)
    new_sel = pad_rows(kv4[:, :, 2 * KV_W:])
    ocw, sel = _nsa_sample_a(q_rows, g_rows, kc, vc, win_buf, new_win, past_len=past_len, n_tok=ts)
    n_steps = page_table.shape[1] // pps
    blk_per_step = pps * PAGE_SIZE // SEL_BLOCK
    n_past_blk = n_steps * blk_per_step
    sel_past = sel[:, :, :n_past_blk].reshape(nb, NSA_KV_HEADS, 1, ts, n_steps, blk_per_step)
    sel_past = jnp.broadcast_to(sel_past, (nb, NSA_KV_HEADS, NSA_GROUP, ts, n_steps, blk_per_step))
    sel_past = sel_past.transpose(0, 4, 1, 2, 3, 5).reshape(nb, n_steps, rows, blk_per_step)
    sel_past = jnp.pad(sel_past, ((0, 0), (0, 0), (0, 0), (0, LANES - blk_per_step)))
    sel_new = jnp.pad(sel[:, :, n_past_blk:], ((0, 0), (0, 0), (0, LANES)))[:, :, :LANES]
    sel_new = sel_new.reshape(nb, NSA_KV_HEADS, 1, ts, LANES)
    sel_new = jnp.broadcast_to(sel_new, (nb, NSA_KV_HEADS, NSA_GROUP, ts, LANES)).reshape(nb, 1, rows, LANES)
    sel_steps = jnp.concatenate([sel_past, sel_new], axis=1).astype(BF16)
    o_rows = _nsa_sample_b(page_table, cache, q_rows, g_rows, sel_steps, new_sel, ocw, n_tok=ts, pps=pps)
    return o_rows.reshape(nb, NSA_HEADS, ts, LANES).transpose(0, 2, 1, 3).reshape(nb, ts, QPAD_W).astype(BF16)


def _outproj_kernel(x_ref, hg_ref, nsa_ref, g1_ref, sc2_ref, sh2_ref, fn_ref, wo1_ref, wo2_ref,
                    x1_ref, h2_ref, *, tm):
    mix = _dot(hg_ref[0], wo1_ref[...]) + _dot(nsa_ref[0], wo2_ref[...])
    x1 = x_ref[0] + g1_ref[0] * mix
    x1_ref[0] = x1
    h2 = _rms(x1, fn_ref[...]) * (1.0 + sc2_ref[0]) + sh2_ref[0]
    _store_tok_tiles(h2_ref, h2, tm)


def _outproj(x, hg_out, nsa, gate1, scale2, shift2, ffn_norm, wo_hg, wo_nsa, tm):
    nb, t, _ = x.shape
    nt = t // tm
    mod_spec = _mod_spec(gate1, tm)
    tile = lambda w: pl.BlockSpec((1, tm, w), lambda b, i: (b, i, 0))
    return pl.pallas_call(
        functools.partial(_outproj_kernel, tm=tm),
        grid=(nb, nt),
        in_specs=[tile(D_MODEL), tile(HG_WIDTH), tile(QPAD_W), mod_spec, mod_spec, mod_spec,
                  pl.BlockSpec((1, D_MODEL), lambda b, i: (0, 0)),
                  pl.BlockSpec((HG_WIDTH, D_MODEL), lambda b, i: (0, 0)),
                  pl.BlockSpec((QPAD_W, D_MODEL), lambda b, i: (0, 0))],
        out_specs=[tile(D_MODEL), pl.BlockSpec((tm * TOK_ROWS, LANES), lambda b, i: (b * nt + i, 0))],
        out_shape=[jax.ShapeDtypeStruct((nb, t, D_MODEL), F32),
                   jax.ShapeDtypeStruct((nb * t * TOK_ROWS, LANES), F32)],
        compiler_params=_cparams(("arbitrary", "arbitrary")),
        name="out_proj",
    )(x, hg_out, nsa, gate1, scale2, shift2, ffn_norm.reshape(1, -1), wo_hg, wo_nsa)


def _split_w_out(w_out):
    wo_hg = w_out[:HG_WIDTH].astype(BF16)
    wn = w_out[HG_WIDTH:].reshape(NSA_HEADS, NSA_HEAD_DIM, D_MODEL)
    z = jnp.zeros_like(wn)
    grp = (jnp.arange(NSA_HEADS) // NSA_GROUP)[:, None, None]
    wn_pad = jnp.where(grp == 0, jnp.concatenate([wn, z], axis=1), jnp.concatenate([z, wn], axis=1))
    return wo_hg, wn_pad.reshape(QPAD_W, D_MODEL).astype(BF16)


def _router_kernel(h_ref, wr_ref, b_ref, e_ref, w_ref, r_ref, cnt_ref, run_scr, *, tm):
    @pl.when(pl.program_id(0) == 0)
    def _():
        run_scr[...] = jnp.zeros(run_scr.shape, F32)

    x = _load_tok_tiles(h_ref, tm).astype(BF16)
    scores = jax.nn.sigmoid(_dot(x, wr_ref[...]))
    biased = scores + b_ref[...]
    lane_i = lax.broadcasted_iota(jnp.int32, (tm, N_EXPERTS), 1)
    lane = lane_i.astype(F32)
    grp_of_lane = lax.shift_right_logical(lane_i, 5)
    per_group = N_EXPERTS // N_GROUPS

    gcol = lax.broadcasted_iota(jnp.int32, (tm, LANES), 1)
    gs = jnp.full((tm, LANES), 2.0 * NEG, F32)
    for g in range(N_GROUPS):
        mg = jnp.where(grp_of_lane == g, biased, NEG)
        m1 = jnp.max(mg, axis=1, keepdims=True)
        i1 = jnp.min(jnp.where(mg == m1, lane, 1e9), axis=1, keepdims=True)
        m2 = jnp.max(jnp.where(lane == i1, NEG, mg), axis=1, keepdims=True)
        gs = jnp.where(gcol == g, m1 + m2, gs)
    gsel = _topk_mask(gs, TOPK_GROUPS).astype(BF16)
    spread = (lax.broadcasted_iota(jnp.int32, (LANES, N_EXPERTS), 0)
              == lax.shift_right_logical(lax.broadcasted_iota(jnp.int32, (LANES, N_EXPERTS), 1), 5)).astype(BF16)
    v = jnp.where(_dot(gsel, spread) > 0.5, biased, NEG)

    onehot = jnp.zeros((tm, N_EXPERTS), F32)
    idxs, wts = [], []
    wsum = jnp.zeros((tm, 1), F32)
    for _ in range(TOP_K):
        m = jnp.max(v, axis=1, keepdims=True)
        idx = jnp.min(jnp.where(v == m, lane, 1e9), axis=1, keepdims=True)
        pick = lane == idx
        wk = jnp.sum(jnp.where(pick, scores, 0.0), axis=1, keepdims=True)
        onehot = jnp.where(pick, 1.0, onehot)
        v = jnp.where(pick, 3.0 * NEG, v)
        idxs.append(idx)
        wts.append(wk)
        wsum = wsum + wk

    earlier = (lax.broadcasted_iota(jnp.int32, (tm, tm), 0) > lax.broadcasted_iota(jnp.int32, (tm, tm), 1))
    before = _dot(earlier.astype(BF16), onehot.astype(BF16)) + run_scr[...]
    e_out = jnp.zeros((tm, LANES), jnp.int32)
    r_out = jnp.zeros((tm, LANES), jnp.int32)
    w_out = jnp.zeros((tm, LANES), F32)
    for k in range(TOP_K):
        rk = jnp.sum(jnp.where(lane == idxs[k], before, 0.0), axis=1, keepdims=True)
        e_out = jnp.where(gcol == k, idxs[k].astype(jnp.int32), e_out)
        r_out = jnp.where(gcol == k, rk.astype(jnp.int32), r_out)
        w_out = jnp.where(gcol == k, wts[k] / wsum * ROUTED_SCALE, w_out)
    e_ref[...] = e_out
    r_ref[...] = r_out
    w_ref[...] = w_out
    run_scr[...] = run_scr[...] + jnp.sum(onehot, axis=0, keepdims=True)
    cnt_ref[...] = run_scr[...]


def _router(h2, w_router, bias, n_tok, tm=ROUTER_TM):
    tile = pl.BlockSpec((tm, LANES), lambda i: (i, 0))
    return pl.pallas_call(
        functools.partial(_router_kernel, tm=tm),
        grid=(n_tok // tm,),
        in_specs=[pl.BlockSpec((tm * TOK_ROWS, LANES), lambda i: (i, 0)),
                  pl.BlockSpec((D_MODEL, N_EXPERTS), lambda i: (0, 0)),
                  pl.BlockSpec((1, N_EXPERTS), lambda i: (0, 0))],
        out_specs=[tile, tile, tile, pl.BlockSpec((1, N_EXPERTS), lambda i: (0, 0))],
        out_shape=[jax.ShapeDtypeStruct((n_tok, LANES), jnp.int32),
                   jax.ShapeDtypeStruct((n_tok, LANES), F32),
                   jax.ShapeDtypeStruct((n_tok, LANES), jnp.int32),
                   jax.ShapeDtypeStruct((1, N_EXPERTS), F32)],
        scratch_shapes=[pltpu.VMEM((1, N_EXPERTS), F32)],
        compiler_params=_cparams(("arbitrary",)),
        name="moe_router",
    )(h2, w_router, bias)


def _dest_kernel(e_ref, r_ref, st_ref, d_ref):
    e = e_ref[...]
    tm = e.shape[0]
    lane = lax.broadcasted_iota(jnp.int32, (tm, N_EXPERTS), 1)
    col = lax.broadcasted_iota(jnp.int32, (tm, LANES), 1)
    st = st_ref[...]
    out = r_ref[...]
    for k in range(TOP_K):
        sk = jnp.sum(jnp.where(lane == e[:, k:k + 1], st, 0.0), axis=1, keepdims=True)
        out = jnp.where(col == k, out + sk.astype(jnp.int32), out)
    d_ref[...] = out


def _moe_dest(top_e, rank, starts, tm=ROUTER_TM):
    n_tok = top_e.shape[0]
    tile = pl.BlockSpec((tm, LANES), lambda i: (i, 0))
    return pl.pallas_call(
        _dest_kernel,
        grid=(n_tok // tm,),
        in_specs=[tile, tile, pl.BlockSpec((1, N_EXPERTS), lambda i: (0, 0))],
        out_specs=tile,
        out_shape=jax.ShapeDtypeStruct((n_tok, LANES), jnp.int32),
        compiler_params=_cparams(("arbitrary",)),
        name="moe_dest",
    )(top_e, rank, starts.astype(F32).reshape(1, -1))


def _moe_items(counts, starts, ends, n_blocks):
    first_blk = starts // MOE_BM
    last_blk = (ends - 1) // MOE_BM
    n_it = jnp.where(counts > 0, last_blk - first_blk + 1, 0)
    it_end = jnp.cumsum(n_it)
    it_start = it_end - n_it
    total = it_end[-1]
    ii = jnp.arange(n_blocks + N_EXPERTS - 1, dtype=jnp.int32)
    valid = ii < total
    e_of = jnp.searchsorted(it_end, jnp.minimum(ii, total - 1), side="right").astype(jnp.int32)
    e_of = jnp.minimum(e_of, N_EXPERTS - 1)
    blk_of = jnp.where(valid, first_blk[e_of] + ii - it_start[e_of], n_blocks - 1).astype(jnp.int32)
    prev = jnp.concatenate([jnp.full((1,), -1, jnp.int32), blk_of[:-1]])
    first = (valid & (blk_of != prev)).astype(jnp.int32)
    return blk_of, e_of, first, valid.astype(jnp.int32)


def _dispatch_kernel(dest_ref, h_ref, xs_ref, sem, *, tm):
    n_pairs = tm * TOP_K

    def row_copy(src_tok, dst_row):
        return pltpu.make_async_copy(
            h_ref.at[pl.ds(pl.multiple_of(src_tok * TOK_ROWS, TOK_ROWS), TOK_ROWS), :],
            xs_ref.at[pl.ds(pl.multiple_of(dst_row * TOK_ROWS, TOK_ROWS), TOK_ROWS), :], sem)

    def issue(i, carry):
        for k in range(TOP_K):
            row_copy(i, dest_ref[0, 0, i * TOP_K + k]).start(priority=k % 2)
        return carry

    def drain(i, carry):
        for _ in range(TOP_K):
            row_copy(0, 0).wait()
        return carry

    lax.fori_loop(0, tm, issue, 0)
    lax.fori_loop(0, tm, drain, 0)


def _dispatch(dest_tiles, h2, n_pairs, tm=ROUTER_TM):
    n_tiles = dest_tiles.shape[0]
    return pl.pallas_call(
        functools.partial(_dispatch_kernel, tm=tm),
        grid=(n_tiles,),
        in_specs=[pl.BlockSpec((1, 1, tm * TOP_K), lambda i: (i, 0, 0), memory_space=pltpu.SMEM),
                  pl.BlockSpec((tm * TOK_ROWS, LANES), lambda i: (i, 0))],
        out_specs=pl.BlockSpec(memory_space=pl.ANY),
        out_shape=jax.ShapeDtypeStruct((n_pairs * TOK_ROWS, LANES), F32),
        scratch_shapes=[pltpu.SemaphoreType.DMA(())],
        compiler_params=_cparams(("arbitrary",)),
        name="moe_dispatch",
    )(dest_tiles, h2)


def _gmm_kernel(blk_ref, e_ref, first_ref, valid_ref, st_ref, en_ref, xs_ref, wg_ref, wu_ref, wd_ref, ys_ref,
                *, bm):
    i = pl.program_id(0)

    @pl.when(valid_ref[i] == 1)
    def _():
        x = _load_tok_tiles(xs_ref, bm).astype(BF16)
        hid = (_silu(_dot(x, wg_ref[0].astype(BF16))) * _dot(x, wu_ref[0].astype(BF16))).astype(BF16)
        y = _dot(hid, wd_ref[0].astype(BF16))
        e = e_ref[i]
        row = blk_ref[i] * bm + lax.broadcasted_iota(jnp.int32, (bm, 1), 0)
        mine = (row >= st_ref[e]) & (row < en_ref[e])

        @pl.when(first_ref[i] == 1)
        def _():
            _store_tok_tiles(ys_ref, jnp.where(mine, y, 0.0), bm)

        @pl.when(first_ref[i] == 0)
        def _():
            _store_tok_tiles(ys_ref, jnp.where(mine, y, _load_tok_tiles(ys_ref, bm)), bm)


def _moe_gmm(items, starts, ends, xs_sorted, w_gate, w_up, w_down, n_blocks, bm=MOE_BM):
    blk_of, e_of, first, valid = items
    rows = pl.BlockSpec((bm * TOK_ROWS, LANES), lambda i, blk, e, f, v, st, en: (blk[i], 0))
    wspec = lambda a: pl.BlockSpec((1,) + a.shape[1:], lambda i, blk, e, f, v, st, en: (e[i], 0, 0))
    return pl.pallas_call(
        functools.partial(_gmm_kernel, bm=bm),
        grid_spec=pltpu.PrefetchScalarGridSpec(
            num_scalar_prefetch=6,
            grid=(blk_of.shape[0],),
            in_specs=[rows, wspec(w_gate), wspec(w_up), wspec(w_down)],
            out_specs=rows),
        out_shape=jax.ShapeDtypeStruct(xs_sorted.shape, F32),
        compiler_params=_cparams(("arbitrary",)),
        name="moe_experts",
    )(blk_of, e_of, first, valid, starts.astype(jnp.int32), ends.astype(jnp.int32), xs_sorted, w_gate, w_up, w_down)


def _combine_kernel(dest_ref, w_ref, x1_ref, h_ref, g2_ref, wsg_ref, wsu_ref, wsd_ref, fn_ref, ys_ref, o_ref,
                    gbuf, sem, *, tm):
    n_pairs = tm * TOP_K

    def row_copy(src_row, p):
        return pltpu.make_async_copy(
            ys_ref.at[pl.ds(pl.multiple_of(src_row * TOK_ROWS, TOK_ROWS), TOK_ROWS), :],
            gbuf.at[pl.ds(pl.multiple_of(p * TOK_ROWS, TOK_ROWS), TOK_ROWS), :], sem)

    def issue(i, carry):
        for k in range(TOP_K):
            p = i * TOP_K + k
            row_copy(dest_ref[0, 0, p], p).start(priority=k % 2)
        return carry

    def drain(i, carry):
        for _ in range(TOP_K):
            row_copy(0, 0).wait()
        return carry

    lax.fori_loop(0, tm, issue, 0)
    h = _load_tok_tiles(h_ref, tm).astype(BF16)
    hid = (_silu(_dot(h, wsg_ref[...])) * _dot(h, wsu_ref[...])).astype(BF16)
    shared = _dot(hid, wsd_ref[...])
    lax.fori_loop(0, tm, drain, 0)

    w = w_ref[...]
    stride = TOP_K * TOK_ROWS
    slabs = []
    for s in range(TOK_ROWS):
        acc = jnp.zeros((tm, LANES), F32)
        for k in range(TOP_K):
            acc = acc + w[:, k:k + 1] * gbuf[pl.ds(k * TOK_ROWS + s, tm, stride=stride), :]
        slabs.append(acc)
    routed = jnp.concatenate(slabs, axis=1)
    x2 = x1_ref[0] + g2_ref[0] * (routed + shared)
    o_ref[0] = _rms(x2, fn_ref[...])


def _combine(dest_tiles, top_w, x1, h2, gate2, shared, fnorm, ys_sorted, tile0, tm=COMBINE_TM):
    nb, t, _ = x1.shape
    nt = t // tm
    flat = lambda b, i: tile0 + b * nt + i
    mod_spec = _mod_spec(gate2, tm)
    const = lambda a: pl.BlockSpec(a.shape, lambda b, i: (0, 0))
    return pl.pallas_call(
        functools.partial(_combine_kernel, tm=tm),
        grid=(nb, nt),
        in_specs=[pl.BlockSpec((1, 1, tm * TOP_K), lambda b, i: (flat(b, i), 0, 0), memory_space=pltpu.SMEM),
                  pl.BlockSpec((tm, LANES), lambda b, i: (flat(b, i), 0)),
                  pl.BlockSpec((1, tm, D_MODEL), lambda b, i: (b, i, 0)),
                  pl.BlockSpec((tm * TOK_ROWS, LANES), lambda b, i: (flat(b, i), 0)),
                  mod_spec, const(shared[0]), const(shared[1]), const(shared[2]), const(fnorm),
                  pl.BlockSpec(memory_space=pl.ANY)],
        out_specs=pl.BlockSpec((1, tm, D_MODEL), lambda b, i: (b, i, 0)),
        out_shape=jax.ShapeDtypeStruct((nb, t, D_MODEL), F32),
        scratch_shapes=[pltpu.VMEM((tm * TOP_K * TOK_ROWS, LANES), F32), pltpu.SemaphoreType.DMA(())],
        compiler_params=_cparams(("arbitrary", "arbitrary")),
        name="moe_combine",
    )(dest_tiles, top_w, x1, h2, gate2, *shared, fnorm, ys_sorted)


def kernel(x_prompt, x_sample, c_prompt, c_sample, cache_nsa_kv, cache_win_kv, state_hgrn, page_table,
           attn_norm, ffn_norm, final_norm, hg_norm, w_ada, b_ada, w_in, hg_lb,
           cmp_pe, cmp_w1, cmp_b1, cmp_w2, w_out, w_router, router_bias,
           w_gate, w_up, w_down, ws_gate, ws_up, ws_down):
    nbp, t, _ = x_prompt.shape
    nbs, ts, _ = x_sample.shape
    ns = nbs * ts
    n_all = nbp * t + ns
    past_len = page_table.shape[1] * PAGE_SIZE

    c_all = jnp.concatenate([c_prompt, c_sample], axis=0)
    c_all = jnp.pad(c_all, ((0, -c_all.shape[0] % SUBLANES), (0, 0)))
    mod = _ada(c_all, w_ada[0], b_ada[0])
    modp = mod[:nbp].reshape(nbp, 1, 6, D_MODEL)
    mods = jnp.repeat(mod[nbp:nbp + nbs].reshape(nbs, 1, 6, D_MODEL), ts, axis=1).reshape(1, ns, 6, D_MODEL)

    w_pad = _pad_w_in(w_in[0])
    cw = _compress_weights(cmp_pe[0], cmp_w1[0], cmp_b1[0], cmp_w2[0])
    wo_hg, wo_nsa = _split_w_out(w_out[0])

    hg, qpad, kv4, kvw, gates, kvsel, kvwb = _inproj(
        x_prompt, modp[:, :, 1], modp[:, :, 0], attn_norm[0], w_pad, 512)
    hg_out, hg_state_p = _hgrn(hg, hg_lb, jnp.zeros((nbp, HG_HEADS, HG_DK, HG_DK), F32), hg_norm[0],
                               256, HG_CHUNK)
    n_pages_p = t // PAGE_SIZE
    ptp = jnp.arange(nbp * n_pages_p, dtype=jnp.int32).reshape(nbp, n_pages_p)
    kc, vc = _compress(ptp, kv4.reshape(nbp * n_pages_p, PAGE_SIZE, 4 * KV_W), *cw)
    nsa = _nsa_prompt(qpad, gates, kc, vc, kvsel, kvwb)
    x1p, h2p = _outproj(x_prompt, hg_out, nsa, modp[:, :, 2], modp[:, :, 4], modp[:, :, 3],
                        ffn_norm[0], wo_hg, wo_nsa, 512)

    xs = x_sample.reshape(1, ns, D_MODEL)
    hg_s, qpad_s, kv4_s, kvw_s, gates_s, _, _ = _inproj(
        xs, mods[:, :, 1], mods[:, :, 0], attn_norm[0], w_pad, ns)
    hg_out_s, hg_state_s = _hgrn(hg_s.reshape(nbs, ts, 4 * HG_WIDTH), hg_lb, state_hgrn[0], hg_norm[0], ts, ts)
    cache = cache_nsa_kv[0].reshape(-1, PAGE_SIZE, 4 * KV_W)
    win_buf = cache_win_kv[0].reshape(nbs, -1, 2 * KV_W)
    nsa_s = _nsa_sample(page_table, cache, win_buf, cw, qpad_s.reshape(nbs, ts, QPAD_W),
                        gates_s.reshape(nbs, ts, GZ_PAD), kv4_s.reshape(nbs, ts, 4 * KV_W),
                        kvw_s.reshape(nbs, ts, 2 * KV_W)).reshape(1, ns, QPAD_W)
    x1s, h2s = _outproj(xs, hg_out_s.reshape(1, ns, HG_WIDTH), nsa_s, mods[:, :, 2], mods[:, :, 4], mods[:, :, 3],
                        ffn_norm[0], wo_hg, wo_nsa, ns)

    h2 = jnp.concatenate([h2p, h2s], axis=0)
    top_e, top_w, rank, counts = _router(h2, w_router[0].astype(BF16), router_bias[0].reshape(1, -1), n_all)
    counts = counts[0].astype(jnp.int32)
    ends = jnp.cumsum(counts)
    starts = ends - counts
    dest = _moe_dest(top_e, rank, starts)[:, :TOP_K].reshape(-1)
    n_pairs = n_all * TOP_K
    n_blocks = n_pairs // MOE_BM
    items = _moe_items(counts, starts, ends, n_blocks)
    xs_sorted = _dispatch(dest.reshape(n_all // ROUTER_TM, 1, ROUTER_TM * TOP_K), h2, n_pairs)
    ys_sorted = _moe_gmm(items, starts, ends, xs_sorted, w_gate[0], w_up[0], w_down[0], n_blocks)
    dest_c = dest.reshape(n_all // COMBINE_TM, 1, COMBINE_TM * TOP_K)
    shared = (ws_gate[0].astype(BF16), ws_up[0].astype(BF16), ws_down[0].astype(BF16))
    fnorm = final_norm.reshape(1, -1)
    y_prompt = _combine(dest_c, top_w, x1p, h2, modp[:, :, 5], shared, fnorm, ys_sorted, 0)
    y_sample = _combine(dest_c, top_w, x1s, h2, mods[:, :, 5], shared, fnorm, ys_sorted, nbp * t // COMBINE_TM)

    wb = win_buf.shape[1]
    win_p = kvw[:, t - min(WINDOW, t):]
    win_s = jnp.concatenate([win_buf, kvw_s.reshape(nbs, ts, 2 * KV_W)], axis=1)[:, -wb:]
    kv_shape = (4, NSA_KV_HEADS, NSA_HEAD_DIM)
    win_shape = (2, NSA_KV_HEADS, NSA_HEAD_DIM)
    return (y_prompt,
            y_sample.reshape(nbs, ts, D_MODEL),
            kv4.reshape(1, nbp, t, *kv_shape),
            win_p.reshape(1, nbp, -1, *win_shape),
            hg_state_p[None],
            kv4_s.reshape(1, nbs, ts, *kv_shape),
            win_s.reshape(1, nbs, wb, *win_shape),
            hg_state_s[None])
```

```python
import functools

import jax
import jax.numpy as jnp
import numpy as np
from jax import lax
from jax.experimental import pallas as pl
from jax.experimental.pallas import tpu as pltpu

F32 = jnp.float32
BF16 = jnp.bfloat16

D_MODEL = 1024
HG_WIDTH = 512
HG_HEADS = 4
HG_DK = 128
HG_CHUNK = 32
NSA_WIDTH = 512
NSA_HEADS = 8
NSA_HEAD_DIM = 64
NSA_KV_HEADS = 2
NSA_GROUP = 4
KV_W = 128
CMP_BLOCK = 32
CMP_STRIDE = 16
CMP_HIDDEN = 256
SEL_BLOCK = 64
SEL_TOPN = 16
WINDOW = 512
Q_BLOCK = 128
N_EXPERTS = 256
TOP_K = 8
N_GROUPS = 8
TOPK_GROUPS = 4
MOE_D_FF = 256
ROUTED_SCALE = 2.5
RMS_EPS = 1e-6
PAGE_SIZE = 128
IN_COLS = 4 * HG_WIDTH + NSA_WIDTH + 4 * KV_W + 2 * KV_W + 3 * NSA_HEADS

LANES = 128
SUBLANES = 8
TOK_ROWS = D_MODEL // LANES
VMEM_LIMIT = 56 * 1024 * 1024

QPAD_W = NSA_HEADS * LANES
GZ_PAD = LANES
INP_COLS = 4 * HG_WIDTH + QPAD_W + 4 * KV_W + 2 * KV_W + GZ_PAD

NEG = -1e30
PAGES_PER_STEP = 32
MOE_BM = 512
ROUTER_TM = 256
COMBINE_TM = 128


def _cparams(sem):
    return pltpu.CompilerParams(dimension_semantics=sem, vmem_limit_bytes=VMEM_LIMIT)


def _dot(a, b):
    return jnp.dot(a, b, preferred_element_type=F32)


def _dot_nt(a, b):
    return lax.dot_general(a, b, (((1,), (1,)), ((), ())), preferred_element_type=F32)


def _dot_tn(a, b):
    return lax.dot_general(a, b, (((0,), (0,)), ((), ())), preferred_element_type=F32)


def _rms(x, g):
    return x * lax.rsqrt(jnp.mean(x * x, axis=-1, keepdims=True) + RMS_EPS) * g


def _silu(x):
    return x * jax.nn.sigmoid(x)


Q_SCALE = NSA_HEAD_DIM ** -0.5 * 1.4426950408889634


def _masked_softmax(s, valid):
    s = jnp.where(valid, s, NEG)
    m = jnp.max(s, axis=1, keepdims=True)
    p = jnp.exp2(s - m) * valid.astype(F32)
    return p / jnp.maximum(jnp.sum(p, axis=1, keepdims=True), 1e-30)


def _topk_mask(v, k):
    lane = lax.broadcasted_iota(jnp.int32, v.shape, 1).astype(F32)
    sel = jnp.zeros(v.shape, F32)
    for _ in range(k):
        m = jnp.max(v, axis=1, keepdims=True)
        idx = jnp.min(jnp.where(v == m, lane, 1e9), axis=1, keepdims=True)
        pick = lane == idx
        sel = jnp.where(pick, 1.0, sel)
        v = jnp.where(pick, 3.0 * NEG, v)
    return sel


def _mod_spec(mod, tm):
    if mod.shape[1] == 1:
        return pl.BlockSpec((1, 1, D_MODEL), lambda b, i: (b, 0, 0))
    return pl.BlockSpec((1, tm, D_MODEL), lambda b, i: (b, i, 0))


def _load_tok_tiles(ref, n_tok):
    return jnp.concatenate([ref[pl.ds(s, n_tok, stride=TOK_ROWS), :] for s in range(TOK_ROWS)], axis=1)


def _store_tok_tiles(ref, val, n_tok):
    for s in range(TOK_ROWS):
        ref[pl.ds(s, n_tok, stride=TOK_ROWS), :] = val[:, s * LANES:(s + 1) * LANES]


def _ada_kernel(c_ref, w_ref, b_ref, o_ref):
    s = _silu(c_ref[...]).astype(BF16)
    o_ref[...] = _dot(s, w_ref[...].astype(BF16)) + b_ref[...]


def _ada(c_all, w_ada, b_ada):
    n = c_all.shape[0]
    return pl.pallas_call(
        _ada_kernel,
        grid=(6,),
        in_specs=[pl.BlockSpec((n, D_MODEL), lambda j: (0, 0)),
                  pl.BlockSpec((D_MODEL, D_MODEL), lambda j: (0, j)),
                  pl.BlockSpec((1, D_MODEL), lambda j: (0, j))],
        out_specs=pl.BlockSpec((n, D_MODEL), lambda j: (0, j)),
        out_shape=jax.ShapeDtypeStruct((n, 6 * D_MODEL), F32),
        compiler_params=_cparams(("arbitrary",)),
        name="ada_mod",
    )(c_all, w_ada, b_ada.reshape(1, -1))


def _inproj_kernel(x_ref, sc_ref, sh_ref, g_ref, w_ref,
                   hg_ref, q_ref, kv4_ref, kvw_ref, gate_ref, kvsel_ref, kvwb_ref):
    h = _rms(x_ref[0], g_ref[...]) * (1.0 + sc_ref[0]) + sh_ref[0]
    z = _dot(h.astype(BF16), w_ref[...])
    c0 = 4 * HG_WIDTH
    hg_ref[0] = z[:, :c0]
    q_ref[0] = (z[:, c0:c0 + QPAD_W] * Q_SCALE).astype(BF16)
    c1 = c0 + QPAD_W
    kv4 = z[:, c1:c1 + 4 * KV_W]
    kv4_ref[0] = kv4
    kvsel_ref[0] = kv4[:, 2 * KV_W:].astype(BF16)
    c2 = c1 + 4 * KV_W
    kvw = z[:, c2:c2 + 2 * KV_W]
    kvw_ref[0] = kvw
    kvwb_ref[0] = kvw.astype(BF16)
    gate_ref[0] = jax.nn.sigmoid(z[:, c2 + 2 * KV_W:])


def _inproj(x, scale, shift, g_norm, w_pad, tm):
    nb, t, _ = x.shape
    mod_spec = _mod_spec(scale, tm)
    widths = [(4 * HG_WIDTH, F32), (QPAD_W, BF16), (4 * KV_W, F32), (2 * KV_W, F32), (GZ_PAD, F32),
              (2 * KV_W, BF16), (2 * KV_W, BF16)]
    return pl.pallas_call(
        _inproj_kernel,
        grid=(nb, t // tm),
        in_specs=[pl.BlockSpec((1, tm, D_MODEL), lambda b, i: (b, i, 0)),
                  mod_spec, mod_spec,
                  pl.BlockSpec((1, D_MODEL), lambda b, i: (0, 0)),
                  pl.BlockSpec((D_MODEL, INP_COLS), lambda b, i: (0, 0))],
        out_specs=[pl.BlockSpec((1, tm, w), lambda b, i: (b, i, 0)) for w, _ in widths],
        out_shape=[jax.ShapeDtypeStruct((nb, t, w), dt) for w, dt in widths],
        compiler_params=_cparams(("arbitrary", "arbitrary")),
        name="in_proj",
    )(x, scale, shift, g_norm.reshape(1, -1), w_pad)


def _pad_w_in(w_in):
    c0 = 4 * HG_WIDTH
    wq = w_in[:, c0:c0 + NSA_WIDTH].reshape(D_MODEL, NSA_HEADS, NSA_HEAD_DIM)
    zeros = jnp.zeros_like(wq)
    lo = jnp.concatenate([wq, zeros], axis=-1)
    hi = jnp.concatenate([zeros, wq], axis=-1)
    grp = (jnp.arange(NSA_HEADS) // NSA_GROUP)[None, :, None]
    wq_pad = jnp.where(grp == 0, lo, hi).reshape(D_MODEL, QPAD_W)
    c1 = c0 + NSA_WIDTH
    rest = w_in[:, c1:c1 + 6 * KV_W]
    gz = jnp.pad(w_in[:, c1 + 6 * KV_W:], ((0, 0), (0, GZ_PAD - 3 * NSA_HEADS)))
    return jnp.concatenate([w_in[:, :c0], wq_pad, rest, gz], axis=1).astype(BF16)


def _hgrn_kernel(q_ref, f_ref, v_ref, gt_ref, lb_ref, s0_ref, gn_ref, o_ref, s_out_ref, st_scr,
                 *, chunk, n_chunks):
    i = pl.program_id(2)

    @pl.when(i == 0)
    def _():
        st_scr[...] = s0_ref[0, 0].T

    lbr = lb_ref[...]
    e = jnp.exp(lbr - jnp.max(lbr, axis=0, keepdims=True))
    lb = e[0:1] / jnp.sum(e, axis=0, keepdims=True)
    row = lax.broadcasted_iota(jnp.int32, (chunk, HG_DK), 0)
    causal = (lax.broadcasted_iota(jnp.int32, (chunk, chunk), 0)
              >= lax.broadcasted_iota(jnp.int32, (chunk, chunk), 1))
    st = st_scr[...]
    for c in range(n_chunks):
        sl = pl.ds(c * chunk, chunk)
        z = f_ref[0, sl, :]
        log_f = jnp.log(lb + (1.0 - lb) * jax.nn.sigmoid(z))
        kk = (1.0 - lb) * jax.nn.sigmoid(-z)
        a = log_f
        s = 1
        while s < chunk:
            a = a + jnp.where(row >= s, pltpu.roll(a, s, 0), 0.0)
            s *= 2
        qt = (q_ref[0, sl, :] * jnp.exp(a)).astype(BF16)
        kt = (kk * jnp.exp(-a)).astype(BF16)
        v = v_ref[0, sl, :].astype(BF16)
        att = jnp.where(causal, _dot_nt(qt, kt), 0.0)
        o = _dot(att.astype(BF16), v) + _dot_nt(qt, st.astype(BF16))
        a_end = a[chunk - 1:chunk, :]
        kd = (kk * jnp.exp(a_end - a)).astype(BF16)
        st = st * jnp.exp(a_end) + _dot_tn(v, kd)
        o = _rms(o, gn_ref[...]) * _silu(gt_ref[0, sl, :])
        o_ref[0, sl, :] = o.astype(o_ref.dtype)
    st_scr[...] = st

    @pl.when(i == pl.num_programs(2) - 1)
    def _():
        s_out_ref[0, 0] = st.T


def _hgrn(hg, hg_lb, s0, g_norm, tc, chunk):
    nb, t, _ = hg.shape
    col = lambda k: (lambda b, h, i: (b, i, k * HG_HEADS + h))
    st_spec = pl.BlockSpec((1, 1, HG_DK, HG_DK), lambda b, h, i: (b, h, 0, 0))
    return pl.pallas_call(
        functools.partial(_hgrn_kernel, chunk=chunk, n_chunks=tc // chunk),
        grid=(nb, HG_HEADS, t // tc),
        in_specs=[pl.BlockSpec((1, tc, HG_DK), col(0)),
                  pl.BlockSpec((1, tc, HG_DK), col(1)),
                  pl.BlockSpec((1, tc, HG_DK), col(2)),
                  pl.BlockSpec((1, tc, HG_DK), col(3)),
                  pl.BlockSpec((hg_lb.shape[0], HG_DK), lambda b, h, i: (0, h)),
                  st_spec,
                  pl.BlockSpec((1, HG_DK), lambda b, h, i: (0, 0))],
        out_specs=[pl.BlockSpec((1, tc, HG_DK), lambda b, h, i: (b, i, h)), st_spec],
        out_shape=[jax.ShapeDtypeStruct((nb, t, HG_WIDTH), BF16),
                   jax.ShapeDtypeStruct((nb, HG_HEADS, HG_DK, HG_DK), F32)],
        scratch_shapes=[pltpu.VMEM((HG_DK, HG_DK), F32)],
        compiler_params=_cparams(("arbitrary", "arbitrary", "arbitrary")),
        name="hgrn2",
    )(hg, hg, hg, hg, hg_lb, s0, g_norm.reshape(1, -1))


def _gelu_tanh(x):
    return 0.5 * x * (1.0 + jnp.tanh(0.7978845608028654 * (x + 0.044715 * x * x * x)))


def _page_copies(pt_ref, cache_ref, buf, sem, b, s, slot, n_pages, pps, col0, tail, transposed):
    copies = []
    base = s * pps
    nxt = pt_ref[b, jnp.minimum(base + pps, n_pages - 1)]
    for br in range(2):
        cols = pl.ds(col0 + br * KV_W, KV_W)
        for i in range(pps):
            pg = pt_ref[b, base + i]
            if transposed:
                copies.append(pltpu.make_async_copy(
                    cache_ref.at[pg, cols, :],
                    buf.at[slot, br, :, pl.ds(i * PAGE_SIZE, PAGE_SIZE)], sem.at[slot]))
            else:
                copies.append(pltpu.make_async_copy(
                    cache_ref.at[pg, :, cols],
                    buf.at[slot, br, pl.ds(i * PAGE_SIZE, PAGE_SIZE), :], sem.at[slot]))
        if tail and transposed:
            copies.append(pltpu.make_async_copy(
                cache_ref.at[nxt, cols, :],
                buf.at[slot, br, :, pl.ds(pps * PAGE_SIZE, PAGE_SIZE)], sem.at[slot]))
        elif tail:
            copies.append(pltpu.make_async_copy(
                cache_ref.at[nxt, pl.ds(0, CMP_STRIDE), cols],
                buf.at[slot, br, pl.ds(pps * PAGE_SIZE, CMP_STRIDE), :], sem.at[slot]))
    return copies


def _stream_pages(pt_ref, cache_ref, buf, sem, n_pages, n_steps, pps, col0, tail, transposed):
    b = pl.program_id(0)
    s = pl.program_id(1)
    n = b * n_steps + s
    total = pl.num_programs(0) * n_steps
    slot = n % 2
    args = (n_pages, pps, col0, tail, transposed)

    @pl.when(n == 0)
    def _():
        for cp in _page_copies(pt_ref, cache_ref, buf, sem, b, s, slot, *args):
            cp.start()

    @pl.when(n + 1 < total)
    def _():
        n1 = n + 1
        for cp in _page_copies(pt_ref, cache_ref, buf, sem, n1 // n_steps, n1 % n_steps, 1 - slot, *args):
            cp.start()

    for cp in _page_copies(pt_ref, cache_ref, buf, sem, b, s, slot, *args):
        cp.wait()
    return slot


def _compress_kernel(pt_ref, cache_ref, pe_ref, w1_ref, b1_ref, w2_ref, kc_ref, vc_ref, buf, sem, *rowbuf,
                     n_pages, n_steps, pps, transposed):
    groups = pps * PAGE_SIZE // CMP_STRIDE
    slot = _stream_pages(pt_ref, cache_ref, buf, sem, n_pages, n_steps, pps, 0, True, transposed)

    if transposed:
        rows_ref, = rowbuf
        for br in range(2):
            for i in range(pps + 1):
                n_rows = PAGE_SIZE if i < pps else CMP_STRIDE
                page_t = buf[slot, br, :, i * PAGE_SIZE:(i + 1) * PAGE_SIZE]
                rows_ref[br, i * PAGE_SIZE:i * PAGE_SIZE + n_rows, :] = page_t.T[:n_rows]
        read_rows = lambda br, l: rows_ref[br, pl.ds(l, groups, stride=CMP_STRIDE), :]
    else:
        read_rows = lambda br, l: buf[slot, br, pl.ds(l, groups, stride=CMP_STRIDE), :]

    for br, out_ref in ((0, kc_ref), (1, vc_ref)):
        acc = jnp.zeros((groups, 2 * CMP_HIDDEN), F32)
        for l in range(CMP_BLOCK):
            xl = (read_rows(br, l) + pe_ref[br, l:l + 1, :]).astype(BF16)
            acc = acc + _dot(xl, w1_ref[br, l])
        hid = _gelu_tanh(acc + b1_ref[br]).astype(BF16)
        out_ref[0] = _dot(hid, w2_ref[br]).astype(out_ref.dtype)


def _compress(page_table, cache, pe2, w1cat, b1cat, w2bd, pps=PAGES_PER_STEP, transposed=False):
    nb, n_pages = page_table.shape
    n_steps = n_pages // pps
    groups = pps * PAGE_SIZE // CMP_STRIDE
    rows = pps * PAGE_SIZE + CMP_STRIDE
    const = lambda shape: pl.BlockSpec(shape, lambda b, s, pt: (0,) * len(shape))
    out_spec = pl.BlockSpec((1, groups, KV_W), lambda b, s, pt: (b, s, 0))
    out_sds = jax.ShapeDtypeStruct((nb, n_steps * groups, KV_W), BF16)
    if transposed:
        stage = [pltpu.VMEM((2, 2, KV_W, (pps + 1) * PAGE_SIZE), F32), pltpu.SemaphoreType.DMA((2,)),
                 pltpu.VMEM((2, rows, KV_W), F32)]
    else:
        stage = [pltpu.VMEM((2, 2, rows, KV_W), F32), pltpu.SemaphoreType.DMA((2,))]
    return pl.pallas_call(
        functools.partial(_compress_kernel, n_pages=n_pages, n_steps=n_steps, pps=pps, transposed=transposed),
        grid_spec=pltpu.PrefetchScalarGridSpec(
            num_scalar_prefetch=1,
            grid=(nb, n_steps),
            in_specs=[pl.BlockSpec(memory_space=pl.ANY),
                      const((2, CMP_BLOCK, KV_W)),
                      const((2, CMP_BLOCK, KV_W, 2 * CMP_HIDDEN)),
                      const((2, 1, 2 * CMP_HIDDEN)),
                      const((2, 2 * CMP_HIDDEN, KV_W))],
            out_specs=[out_spec, out_spec],
            scratch_shapes=stage),
        out_shape=[out_sds, out_sds],
        compiler_params=_cparams(("arbitrary", "arbitrary")),
        name="nsa_compress",
    )(page_table, cache, pe2, w1cat, b1cat, w2bd)


def _compress_weights(cmp_pe, cmp_w1, cmp_b1, cmp_w2):
    pe2 = jnp.concatenate([cmp_pe, cmp_pe], axis=-1)
    w1 = cmp_w1.reshape(2, CMP_BLOCK, NSA_HEAD_DIM, CMP_HIDDEN)
    z1 = jnp.zeros_like(w1)
    w1cat = jnp.concatenate([jnp.concatenate([w1, z1], axis=-1),
                             jnp.concatenate([z1, w1], axis=-1)], axis=2).astype(BF16)
    b1cat = jnp.concatenate([cmp_b1, cmp_b1], axis=-1)[:, None, :]
    z2 = jnp.zeros_like(cmp_w2)
    w2bd = jnp.concatenate([jnp.concatenate([cmp_w2, z2], axis=-1),
                            jnp.concatenate([z2, cmp_w2], axis=-1)], axis=1).astype(BF16)
    return pe2, w1cat, b1cat, w2bd


def _overlap_matrix(n_cmp, n_sel):
    cs = lax.broadcasted_iota(jnp.int32, (n_cmp, n_sel), 0) * CMP_STRIDE
    ss = lax.broadcasted_iota(jnp.int32, (n_cmp, n_sel), 1) * SEL_BLOCK
    return ((cs < ss + SEL_BLOCK) & (cs + CMP_BLOCK > ss)).astype(BF16)


SEL_CHUNK = 512
SPREAD_KEYS = 1024
WIN_SPAN = WINDOW + Q_BLOCK


def _nsa_prompt_kernel(q_ref, gt_ref, kc_ref, vc_ref, ks_ref, vs_ref, kw_ref, vw_ref, ex_ref, o_ref,
                       m_scr, acc_scr, chosen_scr, oc_scr, sa_scr, sb_scr, *, n_cmp, n_sel):
    j = pl.program_id(1)
    q0 = j * Q_BLOCK
    tok = lax.broadcasted_iota(jnp.int32, (Q_BLOCK, 1), 0) + q0
    tok4 = jnp.concatenate([tok] * NSA_GROUP, axis=0)
    n_chunks = j // (SEL_CHUNK // Q_BLOCK) + 1
    key_in_span = lax.broadcasted_iota(jnp.int32, (1, SPREAD_KEYS), 1)

    def load_q(g):
        return jnp.concatenate([q_ref[0, :, (NSA_GROUP * g + jh) * LANES:(NSA_GROUP * g + jh + 1) * LANES]
                                for jh in range(NSA_GROUP)], axis=0)

    def softmax_av(s, valid, v):
        s = jnp.where(valid, s, NEG)
        tiles = [s[:, i * LANES:(i + 1) * LANES] for i in range(s.shape[1] // LANES)]
        m = jnp.max(functools.reduce(jnp.maximum, tiles), axis=1, keepdims=True)
        p = jnp.where(valid, jnp.exp2(s - m), 0.0)
        acc = _dot(p.astype(BF16), jnp.concatenate([v, jnp.ones(v.shape, BF16)], axis=1))
        inv = 1.0 / jnp.maximum(acc[:, KV_W:], 1e-30)
        return p, acc[:, :KV_W] * inv, inv

    ov = _overlap_matrix(n_cmp, n_sel)
    cend = lax.broadcasted_iota(jnp.int32, (1, n_cmp), 1) * CMP_STRIDE + (CMP_BLOCK - 1)
    ws = pl.multiple_of(jnp.maximum(q0 - WINDOW, 0), Q_BLOCK)
    wpos = ws + lax.broadcasted_iota(jnp.int32, (1, WIN_SPAN), 1)
    d = tok4 - wpos
    in_window = (d >= 0) & (d < WINDOW)
    gates = gt_ref[0]
    imps = []
    for g in range(NSA_KV_HEADS):
        q = load_q(g)
        p, o_c, inv = softmax_av(_dot_nt(q, kc_ref[0]), cend <= tok4, vc_ref[0])
        _, o_w, _ = softmax_av(_dot_nt(q, kw_ref[0, pl.ds(ws, WIN_SPAN), :]), in_window,
                               vw_ref[0, pl.ds(ws, WIN_SPAN), :])
        psum = jnp.zeros((Q_BLOCK, n_cmp), F32)
        for jh in range(NSA_GROUP):
            h = NSA_GROUP * g + jh
            r = slice(jh * Q_BLOCK, (jh + 1) * Q_BLOCK)
            psum = psum + p[r] * jnp.concatenate([inv[r]] * (n_cmp // LANES), axis=1)
            oc_scr[g, r, :] = gates[:, 3 * h:3 * h + 1] * o_c[r] + gates[:, 3 * h + 2:3 * h + 3] * o_w[r]
        imps.append(_dot(psum.astype(BF16), ov))

    blk = lax.broadcasted_iota(jnp.int32, (1, n_sel), 1)
    cur = lax.shift_right_logical(tok, 6)
    forced = (blk == 0) | (blk == cur) | (blk == cur - 1)
    free = (blk * SEL_BLOCK <= tok) & jnp.logical_not(forced)
    forced2 = jnp.concatenate([forced] * NSA_KV_HEADS, axis=0)
    free2 = jnp.concatenate([free] * NSA_KV_HEADS, axis=0)
    best = _topk_mask(jnp.where(free2, jnp.concatenate(imps, axis=0), NEG), SEL_TOPN - 3)
    sel2 = jnp.where(forced2, 1.0, best).astype(BF16)

    ones_blk = jnp.ones((SEL_CHUNK, KV_W), BF16)

    for g in range(NSA_KV_HEADS):
        sel = sel2[g * Q_BLOCK:(g + 1) * Q_BLOCK]
        m_scr[...] = jnp.full(m_scr.shape, NEG, F32)
        acc_scr[...] = jnp.zeros(acc_scr.shape, F32)

        def spread(i, carry):
            c0 = pl.multiple_of(i * SPREAD_KEYS, SPREAD_KEYS)
            keep = (_dot(sel, ex_ref[:, pl.ds(c0, SPREAD_KEYS)]) > 0.5) & (key_in_span + c0 <= tok)
            chosen_scr[:, pl.ds(c0, SPREAD_KEYS)] = jnp.where(keep, 0.0, NEG)
            return carry

        lax.fori_loop(0, (n_chunks * SEL_CHUNK + SPREAD_KEYS - 1) // SPREAD_KEYS, spread, 0)

        heads = range(NSA_GROUP)
        rows = [pl.ds(jh * Q_BLOCK, Q_BLOCK) for jh in heads]

        def stage_scores(c, buf):
            k0 = pl.multiple_of(c * SEL_CHUNK, SEL_CHUNK)
            bias = chosen_scr[:, pl.ds(k0, SEL_CHUNK)]
            kblk = ks_ref[0, pl.ds(k0, SEL_CHUNK), :]
            for jh in heads:
                h = NSA_GROUP * g + jh
                buf[rows[jh], :] = _dot_nt(q_ref[0, :, h * LANES:(h + 1) * LANES], kblk) + bias

        def accumulate(c, buf):
            k0 = pl.multiple_of(c * SEL_CHUNK, SEL_CHUNK)
            vext = jnp.concatenate([vs_ref[0, pl.ds(k0, SEL_CHUNK), :], ones_blk], axis=1)
            tiles = [[buf[r, i * LANES:(i + 1) * LANES] for i in range(SEL_CHUNK // LANES)] for r in rows]
            m_old = [m_scr[r, :] for r in rows]
            m_new = [jnp.maximum(mo, jnp.max(functools.reduce(jnp.maximum, t), axis=1, keepdims=True))
                     for mo, t in zip(m_old, tiles)]
            probs = [jnp.concatenate([jnp.exp2(x - mn) for x in t], axis=1).astype(BF16)
                     for mn, t in zip(m_new, tiles)]
            pv = [_dot(p, vext) for p in probs]
            for r, mo, mn, y in zip(rows, m_old, m_new, pv):
                alpha = jnp.exp2(mo - mn)
                acc_scr[r, :] = jnp.concatenate([alpha, alpha], axis=1) * acc_scr[r, :] + y
                m_scr[r, :] = mn

        n_pairs = (n_chunks + 1) // 2
        stage_scores(0, sa_scr)

        def body(i, carry):
            stage_scores(2 * i + 1, sb_scr)
            accumulate(2 * i, sa_scr)
            stage_scores(jnp.minimum(2 * i + 2, 2 * n_pairs - 1), sa_scr)
            accumulate(2 * i + 1, sb_scr)
            return carry

        lax.fori_loop(0, n_pairs, body, 0)
        o_s = acc_scr[:, :KV_W] / jnp.maximum(acc_scr[:, KV_W:], 1e-30)

        for jh in range(NSA_GROUP):
            h = NSA_GROUP * g + jh
            r = slice(jh * Q_BLOCK, (jh + 1) * Q_BLOCK)
            o = oc_scr[g, r, :] + gates[:, 3 * h + 1:3 * h + 2] * o_s[r]
            o_ref[0, :, h * LANES:(h + 1) * LANES] = o.astype(o_ref.dtype)


def _nsa_prompt(qpad, gates, kc, vc, kvsel, kvwb):
    nb, t, _ = qpad.shape
    assert t % SPREAD_KEYS == 0 and SPREAD_KEYS == 2 * SEL_CHUNK and t >= WIN_SPAN
    n_cmp = kc.shape[1]
    n_sel = t // SEL_BLOCK
    full = lambda w, k: pl.BlockSpec((1, t, w), lambda b, j: (b, 0, k))
    t_pad = -(-t // SPREAD_KEYS) * SPREAD_KEYS
    expand = jnp.asarray(np.arange(n_sel)[:, None] == (np.arange(t_pad)[None, :] // SEL_BLOCK), BF16)
    return pl.pallas_call(
        functools.partial(_nsa_prompt_kernel, n_cmp=n_cmp, n_sel=n_sel),
        grid=(nb, t // Q_BLOCK),
        in_specs=[pl.BlockSpec((1, Q_BLOCK, QPAD_W), lambda b, j: (b, j, 0)),
                  pl.BlockSpec((1, Q_BLOCK, GZ_PAD), lambda b, j: (b, j, 0)),
                  pl.BlockSpec((1, n_cmp, KV_W), lambda b, j: (b, 0, 0)),
                  pl.BlockSpec((1, n_cmp, KV_W), lambda b, j: (b, 0, 0)),
                  full(KV_W, 0), full(KV_W, 1), full(KV_W, 0), full(KV_W, 1),
                  pl.BlockSpec((n_sel, t_pad), lambda b, j: (0, 0))],
        out_specs=pl.BlockSpec((1, Q_BLOCK, QPAD_W), lambda b, j: (b, j, 0)),
        out_shape=jax.ShapeDtypeStruct((nb, t, QPAD_W), BF16),
        scratch_shapes=[pltpu.VMEM((NSA_GROUP * Q_BLOCK, LANES), F32),
                        pltpu.VMEM((NSA_GROUP * Q_BLOCK, 2 * KV_W), F32),
                        pltpu.VMEM((Q_BLOCK, t_pad), F32),
                        pltpu.VMEM((NSA_KV_HEADS, NSA_GROUP * Q_BLOCK, KV_W), F32),
                        pltpu.VMEM((NSA_GROUP * Q_BLOCK, SEL_CHUNK), F32),
                        pltpu.VMEM((NSA_GROUP * Q_BLOCK, SEL_CHUNK), F32)],
        compiler_params=_cparams(("arbitrary", "arbitrary")),
        name="nsa_prompt",
    )(qpad, gates, kc, vc, kvsel, kvsel, kvwb, kvwb, expand)


def _nsa_sample_a_kernel(q_ref, g_ref, kc_ref, vc_ref, wb_ref, nw_ref, ocw_ref, sel_ref,
                         *, past_len, n_tok, n_sel, n_sel_pad):
    q = q_ref[0]
    rows = q.shape[0]
    n_cmp = kc_ref.shape[1]
    t_row = lax.broadcasted_iota(jnp.int32, (rows, 1), 0) & (n_tok - 1)
    qpos = past_len + t_row
    cend = lax.broadcasted_iota(jnp.int32, (1, n_cmp), 1) * CMP_STRIDE + (CMP_BLOCK - 1)
    p_c = _masked_softmax(_dot_nt(q, kc_ref[0]), cend <= qpos)
    o_c = _dot(p_c.astype(BF16), vc_ref[0])

    per_grp = NSA_GROUP * n_tok
    psum = jnp.concatenate(
        [sum(p_c[g * per_grp + jh * n_tok:g * per_grp + (jh + 1) * n_tok] for jh in range(NSA_GROUP))
         for g in range(NSA_KV_HEADS)], axis=0)
    imp = _dot(psum.astype(BF16), _overlap_matrix(n_cmp, n_sel_pad))
    blk = lax.broadcasted_iota(jnp.int32, (1, n_sel_pad), 1)
    tq = past_len + (lax.broadcasted_iota(jnp.int32, (NSA_KV_HEADS * n_tok, 1), 0) & (n_tok - 1))
    cur = lax.shift_right_logical(tq, 6)
    forced = (blk == 0) | (blk == cur) | (blk == cur - 1)
    allowed = blk * SEL_BLOCK <= tq
    v = jnp.where(forced, -NEG, jnp.where(allowed, imp, NEG))
    sel_ref[0] = _topk_mask(jnp.where(blk < n_sel, v, 2.0 * NEG), SEL_TOPN)

    wb = wb_ref.shape[1]
    kw = wb_ref[0, :, 0:KV_W].astype(BF16)
    vw = wb_ref[0, :, KV_W:2 * KV_W].astype(BF16)
    kn = nw_ref[0, :, 0:KV_W].astype(BF16)
    vn = nw_ref[0, :, KV_W:2 * KV_W].astype(BF16)
    i1 = lax.broadcasted_iota(jnp.int32, (1, wb), 1)
    d1 = t_row + wb - i1
    valid1 = (d1 >= 0) & (d1 < WINDOW) & (past_len - wb + i1 >= 0)
    i2 = lax.broadcasted_iota(jnp.int32, (1, nw_ref.shape[1]), 1)
    d2 = t_row - i2
    valid2 = (d2 >= 0) & (d2 < WINDOW) & (i2 < n_tok)
    s1 = jnp.where(valid1, _dot_nt(q, kw), NEG)
    s2 = jnp.where(valid2, _dot_nt(q, kn), NEG)
    m = jnp.maximum(jnp.max(s1, axis=1, keepdims=True), jnp.max(s2, axis=1, keepdims=True))
    p1 = jnp.exp2(s1 - m) * valid1.astype(F32)
    p2 = jnp.exp2(s2 - m) * valid2.astype(F32)
    den = jnp.maximum(jnp.sum(p1, axis=1, keepdims=True) + jnp.sum(p2, axis=1, keepdims=True), 1e-30)
    o_w = (_dot(p1.astype(BF16), vw) + _dot(p2.astype(BF16), vn)) / den
    g = g_ref[0]
    ocw_ref[0] = g[:, 0:1] * o_c + g[:, 2:3] * o_w


def _nsa_sample_a(q_rows, g_rows, kc, vc, win_buf, new_win, *, past_len, n_tok):
    nb, rows, _ = q_rows.shape
    n_sel = -(-(past_len + n_tok) // SEL_BLOCK)
    n_sel_pad = -(-n_sel // LANES) * LANES
    blk3 = lambda a: pl.BlockSpec((1,) + a.shape[1:], lambda b: (b, 0, 0))
    return pl.pallas_call(
        functools.partial(_nsa_sample_a_kernel, past_len=past_len, n_tok=n_tok, n_sel=n_sel, n_sel_pad=n_sel_pad),
        grid=(nb,),
        in_specs=[blk3(q_rows), blk3(g_rows), blk3(kc), blk3(vc), blk3(win_buf), blk3(new_win)],
        out_specs=[pl.BlockSpec((1, rows, KV_W), lambda b: (b, 0, 0)),
                   pl.BlockSpec((1, NSA_KV_HEADS * n_tok, n_sel_pad), lambda b: (b, 0, 0))],
        out_shape=[jax.ShapeDtypeStruct((nb, rows, KV_W), F32),
                   jax.ShapeDtypeStruct((nb, NSA_KV_HEADS * n_tok, n_sel_pad), F32)],
        compiler_params=_cparams(("arbitrary",)),
        name="nsa_sample_a",
    )(q_rows, g_rows, kc, vc, win_buf, new_win)


def _nsa_sample_b_kernel(pt_ref, cache_ref, q_ref, g_ref, sel_ref, seln_ref, ns_ref, ocw_ref, ex_ref, o_ref,
                         buf, sem, m_scr, l_scr, acc_scr, *, n_pages, n_steps, pps, n_tok):
    s = pl.program_id(1)
    slot = _stream_pages(pt_ref, cache_ref, buf, sem, n_pages, n_steps, pps, 2 * KV_W, False, True)
    q = q_ref[0]
    rows = q.shape[0]

    @pl.when(s == 0)
    def _():
        m_scr[...] = jnp.full(m_scr.shape, NEG, F32)
        l_scr[...] = jnp.zeros(l_scr.shape, F32)
        acc_scr[...] = jnp.zeros(acc_scr.shape, F32)

    def update(scores, msk, times_v):
        sc = jnp.where(msk, scores, NEG)
        m_old = m_scr[...]
        m_new = jnp.maximum(m_old, jnp.max(sc, axis=1, keepdims=True))
        p = jnp.exp2(sc - m_new) * msk.astype(F32)
        alpha = jnp.exp2(m_old - m_new)
        l_scr[...] = alpha * l_scr[...] + jnp.sum(p, axis=1, keepdims=True)
        acc_scr[...] = alpha * acc_scr[...] + times_v(p.astype(BF16))
        m_scr[...] = m_new

    chosen = _dot(sel_ref[0, 0], ex_ref[...]) > 0.5
    update(_dot(q, buf[slot, 0].astype(BF16)), chosen, lambda p: _dot_nt(p, buf[slot, 1].astype(BF16)))

    @pl.when(s == n_steps - 1)
    def _():
        t_row = lax.broadcasted_iota(jnp.int32, (rows, 1), 0) & (n_tok - 1)
        i2 = lax.broadcasted_iota(jnp.int32, (1, ns_ref.shape[1]), 1)
        msk = (seln_ref[0, 0][:, 0:1] > 0.5) & (i2 <= t_row) & (i2 < n_tok)
        update(_dot_nt(q, ns_ref[0, :, 0:KV_W].astype(BF16)), msk,
               lambda p: _dot(p, ns_ref[0, :, KV_W:2 * KV_W].astype(BF16)))
        o_s = acc_scr[...] / jnp.maximum(l_scr[...], 1e-30)
        o_ref[0] = ocw_ref[0] + g_ref[0][:, 1:2] * o_s


def _nsa_sample_b(page_table, cache, q_rows, g_rows, sel_steps, new_sel, ocw, *, n_tok, pps=PAGES_PER_STEP):
    nb, n_pages = page_table.shape
    n_steps = n_pages // pps
    rows = q_rows.shape[1]
    keys = pps * PAGE_SIZE
    expand = jnp.asarray(np.arange(LANES)[:, None] == (np.arange(keys)[None, :] // SEL_BLOCK), BF16)
    per_b = lambda a: pl.BlockSpec((1,) + a.shape[1:], lambda b, s, pt: (b, 0, 0))
    return pl.pallas_call(
        functools.partial(_nsa_sample_b_kernel, n_pages=n_pages, n_steps=n_steps, pps=pps, n_tok=n_tok),
        grid_spec=pltpu.PrefetchScalarGridSpec(
            num_scalar_prefetch=1,
            grid=(nb, n_steps),
            in_specs=[pl.BlockSpec(memory_space=pl.ANY),
                      per_b(q_rows), per_b(g_rows),
                      pl.BlockSpec((1, 1, rows, LANES), lambda b, s, pt: (b, s, 0, 0)),
                      pl.BlockSpec((1, 1, rows, LANES), lambda b, s, pt: (b, n_steps, 0, 0)),
                      per_b(new_sel), per_b(ocw),
                      pl.BlockSpec((LANES, keys), lambda b, s, pt: (0, 0))],
            out_specs=pl.BlockSpec((1, rows, KV_W), lambda b, s, pt: (b, 0, 0)),
            scratch_shapes=[pltpu.VMEM((2, 2, KV_W, keys), F32), pltpu.SemaphoreType.DMA((2,)),
                            pltpu.VMEM((rows, 1), F32), pltpu.VMEM((rows, 1), F32),
                            pltpu.VMEM((rows, KV_W), F32)]),
        out_shape=jax.ShapeDtypeStruct((nb, rows, KV_W), F32),
        compiler_params=_cparams(("arbitrary", "arbitrary")),
        name="nsa_sample_b",
    )(page_table, cache, q_rows, g_rows, sel_steps, sel_steps, new_sel, ocw, expand)


def _nsa_sample(page_table, cache, win_buf, cw, qpad, gates, kv4, kvw, pps=PAGES_PER_STEP):
    nb, ts, _ = qpad.shape
    past_len = page_table.shape[1] * PAGE_SIZE
    kc, vc = _compress(page_table, cache, *cw, pps=pps, transposed=True)
    rows = NSA_HEADS * ts
    q_rows = qpad.reshape(nb, ts, NSA_HEADS, LANES).transpose(0, 2, 1, 3).reshape(nb, rows, LANES)
    g_rows = gates[:, :, :3 * NSA_HEADS].reshape(nb, ts, NSA_HEADS, 3).transpose(0, 2, 1, 3)
    g_rows = jnp.pad(g_rows.reshape(nb, rows, 3), ((0, 0), (0, 0), (0, LANES - 3)))
    pad_rows = lambda a: jnp.pad(a, ((0, 0), (0, LANES - ts), (0, 0)))
    new_win = pad_rows(kvw)
    new_sel = pad_rows(kv4[:, :, 2 * KV_W:])
    ocw, sel = _nsa_sample_a(q_rows, g_rows, kc, vc, win_buf, new_win, past_len=past_len, n_tok=ts)
    n_steps = page_table.shape[1] // pps
    blk_per_step = pps * PAGE_SIZE // SEL_BLOCK
    n_past_blk = n_steps * blk_per_step
    sel_past = sel[:, :, :n_past_blk].reshape(nb, NSA_KV_HEADS, 1, ts, n_steps, blk_per_step)
    sel_past = jnp.broadcast_to(sel_past, (nb, NSA_KV_HEADS, NSA_GROUP, ts, n_steps, blk_per_step))
    sel_past = sel_past.transpose(0, 4, 1, 2, 3, 5).reshape(nb, n_steps, rows, blk_per_step)
    sel_past = jnp.pad(sel_past, ((0, 0), (0, 0), (0, 0), (0, LANES - blk_per_step)))
    sel_new = jnp.pad(sel[:, :, n_past_blk:], ((0, 0), (0, 0), (0, LANES)))[:, :, :LANES]
    sel_new = sel_new.reshape(nb, NSA_KV_HEADS, 1, ts, LANES)
    sel_new = jnp.broadcast_to(sel_new, (nb, NSA_KV_HEADS, NSA_GROUP, ts, LANES)).reshape(nb, 1, rows, LANES)
    sel_steps = jnp.concatenate([sel_past, sel_new], axis=1).astype(BF16)
    o_rows = _nsa_sample_b(page_table, cache, q_rows, g_rows, sel_steps, new_sel, ocw, n_tok=ts, pps=pps)
    return o_rows.reshape(nb, NSA_HEADS, ts, LANES).transpose(0, 2, 1, 3).reshape(nb, ts, QPAD_W).astype(BF16)


def _outproj_kernel(x_ref, hg_ref, nsa_ref, g1_ref, sc2_ref, sh2_ref, fn_ref, wo1_ref, wo2_ref,
                    x1_ref, h2_ref, *, tm):
    mix = _dot(hg_ref[0], wo1_ref[...]) + _dot(nsa_ref[0], wo2_ref[...])
    x1 = x_ref[0] + g1_ref[0] * mix
    x1_ref[0] = x1
    h2 = _rms(x1, fn_ref[...]) * (1.0 + sc2_ref[0]) + sh2_ref[0]
    _store_tok_tiles(h2_ref, h2, tm)


def _outproj(x, hg_out, nsa, gate1, scale2, shift2, ffn_norm, wo_hg, wo_nsa, tm):
    nb, t, _ = x.shape
    nt = t // tm
    mod_spec = _mod_spec(gate1, tm)
    tile = lambda w: pl.BlockSpec((1, tm, w), lambda b, i: (b, i, 0))
    return pl.pallas_call(
        functools.partial(_outproj_kernel, tm=tm),
        grid=(nb, nt),
        in_specs=[tile(D_MODEL), tile(HG_WIDTH), tile(QPAD_W), mod_spec, mod_spec, mod_spec,
                  pl.BlockSpec((1, D_MODEL), lambda b, i: (0, 0)),
                  pl.BlockSpec((HG_WIDTH, D_MODEL), lambda b, i: (0, 0)),
                  pl.BlockSpec((QPAD_W, D_MODEL), lambda b, i: (0, 0))],
        out_specs=[tile(D_MODEL), pl.BlockSpec((tm * TOK_ROWS, LANES), lambda b, i: (b * nt + i, 0))],
        out_shape=[jax.ShapeDtypeStruct((nb, t, D_MODEL), F32),
                   jax.ShapeDtypeStruct((nb * t * TOK_ROWS, LANES), F32)],
        compiler_params=_cparams(("arbitrary", "arbitrary")),
        name="out_proj",
    )(x, hg_out, nsa, gate1, scale2, shift2, ffn_norm.reshape(1, -1), wo_hg, wo_nsa)


def _split_w_out(w_out):
    wo_hg = w_out[:HG_WIDTH].astype(BF16)
    wn = w_out[HG_WIDTH:].reshape(NSA_HEADS, NSA_HEAD_DIM, D_MODEL)
    z = jnp.zeros_like(wn)
    grp = (jnp.arange(NSA_HEADS) // NSA_GROUP)[:, None, None]
    wn_pad = jnp.where(grp == 0, jnp.concatenate([wn, z], axis=1), jnp.concatenate([z, wn], axis=1))
    return wo_hg, wn_pad.reshape(QPAD_W, D_MODEL).astype(BF16)


def _router_kernel(h_ref, wr_ref, b_ref, e_ref, w_ref, r_ref, cnt_ref, run_scr, *, tm):
    @pl.when(pl.program_id(0) == 0)
    def _():
        run_scr[...] = jnp.zeros(run_scr.shape, F32)

    x = _load_tok_tiles(h_ref, tm).astype(BF16)
    scores = jax.nn.sigmoid(_dot(x, wr_ref[...]))
    biased = scores + b_ref[...]
    lane_i = lax.broadcasted_iota(jnp.int32, (tm, N_EXPERTS), 1)
    lane = lane_i.astype(F32)
    grp_of_lane = lax.shift_right_logical(lane_i, 5)
    per_group = N_EXPERTS // N_GROUPS

    gcol = lax.broadcasted_iota(jnp.int32, (tm, LANES), 1)
    gs = jnp.full((tm, LANES), 2.0 * NEG, F32)
    for g in range(N_GROUPS):
        mg = jnp.where(grp_of_lane == g, biased, NEG)
        m1 = jnp.max(mg, axis=1, keepdims=True)
        i1 = jnp.min(jnp.where(mg == m1, lane, 1e9), axis=1, keepdims=True)
        m2 = jnp.max(jnp.where(lane == i1, NEG, mg), axis=1, keepdims=True)
        gs = jnp.where(gcol == g, m1 + m2, gs)
    gsel = _topk_mask(gs, TOPK_GROUPS).astype(BF16)
    spread = (lax.broadcasted_iota(jnp.int32, (LANES, N_EXPERTS), 0)
              == lax.shift_right_logical(lax.broadcasted_iota(jnp.int32, (LANES, N_EXPERTS), 1), 5)).astype(BF16)
    v = jnp.where(_dot(gsel, spread) > 0.5, biased, NEG)

    onehot = jnp.zeros((tm, N_EXPERTS), F32)
    idxs, wts = [], []
    wsum = jnp.zeros((tm, 1), F32)
    for _ in range(TOP_K):
        m = jnp.max(v, axis=1, keepdims=True)
        idx = jnp.min(jnp.where(v == m, lane, 1e9), axis=1, keepdims=True)
        pick = lane == idx
        wk = jnp.sum(jnp.where(pick, scores, 0.0), axis=1, keepdims=True)
        onehot = jnp.where(pick, 1.0, onehot)
        v = jnp.where(pick, 3.0 * NEG, v)
        idxs.append(idx)
        wts.append(wk)
        wsum = wsum + wk

    earlier = (lax.broadcasted_iota(jnp.int32, (tm, tm), 0) > lax.broadcasted_iota(jnp.int32, (tm, tm), 1))
    before = _dot(earlier.astype(BF16), onehot.astype(BF16)) + run_scr[...]
    e_out = jnp.zeros((tm, LANES), jnp.int32)
    r_out = jnp.zeros((tm, LANES), jnp.int32)
    w_out = jnp.zeros((tm, LANES), F32)
    for k in range(TOP_K):
        rk = jnp.sum(jnp.where(lane == idxs[k], before, 0.0), axis=1, keepdims=True)
        e_out = jnp.where(gcol == k, idxs[k].astype(jnp.int32), e_out)
        r_out = jnp.where(gcol == k, rk.astype(jnp.int32), r_out)
        w_out = jnp.where(gcol == k, wts[k] / wsum * ROUTED_SCALE, w_out)
    e_ref[...] = e_out
    r_ref[...] = r_out
    w_ref[...] = w_out
    run_scr[...] = run_scr[...] + jnp.sum(onehot, axis=0, keepdims=True)
    cnt_ref[...] = run_scr[...]


def _router(h2, w_router, bias, n_tok, tm=ROUTER_TM):
    tile = pl.BlockSpec((tm, LANES), lambda i: (i, 0))
    return pl.pallas_call(
        functools.partial(_router_kernel, tm=tm),
        grid=(n_tok // tm,),
        in_specs=[pl.BlockSpec((tm * TOK_ROWS, LANES), lambda i: (i, 0)),
                  pl.BlockSpec((D_MODEL, N_EXPERTS), lambda i: (0, 0)),
                  pl.BlockSpec((1, N_EXPERTS), lambda i: (0, 0))],
        out_specs=[tile, tile, tile, pl.BlockSpec((1, N_EXPERTS), lambda i: (0, 0))],
        out_shape=[jax.ShapeDtypeStruct((n_tok, LANES), jnp.int32),
                   jax.ShapeDtypeStruct((n_tok, LANES), F32),
                   jax.ShapeDtypeStruct((n_tok, LANES), jnp.int32),
                   jax.ShapeDtypeStruct((1, N_EXPERTS), F32)],
        scratch_shapes=[pltpu.VMEM((1, N_EXPERTS), F32)],
        compiler_params=_cparams(("arbitrary",)),
        name="moe_router",
    )(h2, w_router, bias)


def _dest_kernel(e_ref, r_ref, st_ref, d_ref):
    e = e_ref[...]
    tm = e.shape[0]
    lane = lax.broadcasted_iota(jnp.int32, (tm, N_EXPERTS), 1)
    col = lax.broadcasted_iota(jnp.int32, (tm, LANES), 1)
    st = st_ref[...]
    out = r_ref[...]
    for k in range(TOP_K):
        sk = jnp.sum(jnp.where(lane == e[:, k:k + 1], st, 0.0), axis=1, keepdims=True)
        out = jnp.where(col == k, out + sk.astype(jnp.int32), out)
    d_ref[...] = out


def _moe_dest(top_e, rank, starts, tm=ROUTER_TM):
    n_tok = top_e.shape[0]
    tile = pl.BlockSpec((tm, LANES), lambda i: (i, 0))
    return pl.pallas_call(
        _dest_kernel,
        grid=(n_tok // tm,),
        in_specs=[tile, tile, pl.BlockSpec((1, N_EXPERTS), lambda i: (0, 0))],
        out_specs=tile,
        out_shape=jax.ShapeDtypeStruct((n_tok, LANES), jnp.int32),
        compiler_params=_cparams(("arbitrary",)),
        name="moe_dest",
    )(top_e, rank, starts.astype(F32).reshape(1, -1))


def _moe_items(counts, starts, ends, n_blocks):
    first_blk = starts // MOE_BM
    last_blk = (ends - 1) // MOE_BM
    n_it = jnp.where(counts > 0, last_blk - first_blk + 1, 0)
    it_end = jnp.cumsum(n_it)
    it_start = it_end - n_it
    total = it_end[-1]
    ii = jnp.arange(n_blocks + N_EXPERTS - 1, dtype=jnp.int32)
    valid = ii < total
    e_of = jnp.searchsorted(it_end, jnp.minimum(ii, total - 1), side="right").astype(jnp.int32)
    e_of = jnp.minimum(e_of, N_EXPERTS - 1)
    blk_of = jnp.where(valid, first_blk[e_of] + ii - it_start[e_of], n_blocks - 1).astype(jnp.int32)
    shifted = lambda a: jnp.concatenate([jnp.full((1,), -1, jnp.int32), a[:-1]])
    first = (valid & (blk_of != shifted(blk_of))).astype(jnp.int32)
    fresh = (valid & (e_of != shifted(e_of))).astype(jnp.int32)
    return blk_of, e_of, first, fresh, valid.astype(jnp.int32)


def _dispatch_kernel(dest_ref, h_ref, xs_ref, sem, *, tm):
    n_pairs = tm * TOP_K

    def row_copy(src_tok, dst_row):
        return pltpu.make_async_copy(
            h_ref.at[pl.ds(pl.multiple_of(src_tok * TOK_ROWS, TOK_ROWS), TOK_ROWS), :],
            xs_ref.at[pl.ds(pl.multiple_of(dst_row * TOK_ROWS, TOK_ROWS), TOK_ROWS), :], sem)

    def issue(i, carry):
        for k in range(TOP_K):
            row_copy(i, dest_ref[0, 0, i * TOP_K + k]).start(priority=k % 2)
        return carry

    def drain(i, carry):
        for _ in range(TOP_K):
            row_copy(0, 0).wait()
        return carry

    lax.fori_loop(0, tm, issue, 0)
    lax.fori_loop(0, tm, drain, 0)


def _dispatch(dest_tiles, h2, n_pairs, tm=ROUTER_TM):
    n_tiles = dest_tiles.shape[0]
    return pl.pallas_call(
        functools.partial(_dispatch_kernel, tm=tm),
        grid=(n_tiles,),
        in_specs=[pl.BlockSpec((1, 1, tm * TOP_K), lambda i: (i, 0, 0), memory_space=pltpu.SMEM),
                  pl.BlockSpec((tm * TOK_ROWS, LANES), lambda i: (i, 0))],
        out_specs=pl.BlockSpec(memory_space=pl.ANY),
        out_shape=jax.ShapeDtypeStruct((n_pairs * TOK_ROWS, LANES), F32),
        scratch_shapes=[pltpu.SemaphoreType.DMA(())],
        compiler_params=_cparams(("arbitrary",)),
        name="moe_dispatch",
    )(dest_tiles, h2)


def _gmm_kernel(blk_ref, e_ref, first_ref, fresh_ref, valid_ref, st_ref, en_ref,
                xs_ref, wg_ref, wu_ref, wd_ref, ys_ref, wg_bf, wu_bf, wd_bf, *, bm):
    i = pl.program_id(0)

    @pl.when(fresh_ref[i] == 1)
    def _():
        wg_bf[...] = wg_ref[0].astype(BF16)
        wu_bf[...] = wu_ref[0].astype(BF16)
        wd_bf[...] = wd_ref[0].astype(BF16)

    @pl.when(valid_ref[i] == 1)
    def _():
        x = _load_tok_tiles(xs_ref, bm).astype(BF16)
        hid = (_silu(_dot(x, wg_bf[...])) * _dot(x, wu_bf[...])).astype(BF16)
        y = _dot(hid, wd_bf[...])
        e = e_ref[i]
        row = blk_ref[i] * bm + lax.broadcasted_iota(jnp.int32, (bm, 1), 0)
        mine = (row >= st_ref[e]) & (row < en_ref[e])

        @pl.when(first_ref[i] == 1)
        def _():
            _store_tok_tiles(ys_ref, jnp.where(mine, y, 0.0), bm)

        @pl.when(first_ref[i] == 0)
        def _():
            _store_tok_tiles(ys_ref, jnp.where(mine, y, _load_tok_tiles(ys_ref, bm)), bm)


def _moe_gmm(items, starts, ends, xs_sorted, w_gate, w_up, w_down, n_blocks, bm=MOE_BM):
    blk_of, e_of, first, fresh, valid = items
    rows = pl.BlockSpec((bm * TOK_ROWS, LANES), lambda i, blk, e, *_: (blk[i], 0))
    wspec = lambda a: pl.BlockSpec((1,) + a.shape[1:], lambda i, blk, e, *_: (e[i], 0, 0))
    return pl.pallas_call(
        functools.partial(_gmm_kernel, bm=bm),
        grid_spec=pltpu.PrefetchScalarGridSpec(
            num_scalar_prefetch=7,
            grid=(blk_of.shape[0],),
            in_specs=[rows, wspec(w_gate), wspec(w_up), wspec(w_down)],
            out_specs=rows,
            scratch_shapes=[pltpu.VMEM(w_gate.shape[1:], BF16), pltpu.VMEM(w_up.shape[1:], BF16),
                            pltpu.VMEM(w_down.shape[1:], BF16)]),
        out_shape=jax.ShapeDtypeStruct(xs_sorted.shape, F32),
        compiler_params=_cparams(("arbitrary",)),
        name="moe_experts",
    )(blk_of, e_of, first, fresh, valid, starts.astype(jnp.int32), ends.astype(jnp.int32),
      xs_sorted, w_gate, w_up, w_down)


def _combine_kernel(dest_ref, dnext_ref, w_ref, x1_ref, h_ref, g2_ref, wsg_ref, wsu_ref, wsd_ref, fn_ref, ys_ref,
                    o_ref, gbuf, sem, *, tm):
    n = pl.program_id(0) * pl.num_programs(1) + pl.program_id(1)
    total = pl.num_programs(0) * pl.num_programs(1)
    slot = n % 2

    def row_copy(src_row, p, sl):
        return pltpu.make_async_copy(
            ys_ref.at[pl.ds(pl.multiple_of(src_row * TOK_ROWS, TOK_ROWS), TOK_ROWS), :],
            gbuf.at[sl, pl.ds(pl.multiple_of(p * TOK_ROWS, TOK_ROWS), TOK_ROWS), :], sem.at[sl])

    def gather(rows_ref, sl):
        def issue(i, carry):
            for k in range(TOP_K):
                p = i * TOP_K + k
                row_copy(rows_ref[0, 0, p], p, sl).start(priority=k % 2)
            return carry
        lax.fori_loop(0, tm, issue, 0)

    def drain(i, carry):
        for _ in range(TOP_K):
            row_copy(0, 0, slot).wait()
        return carry

    @pl.when(n == 0)
    def _():
        gather(dest_ref, slot)

    @pl.when(n + 1 < total)
    def _():
        gather(dnext_ref, 1 - slot)

    h = _load_tok_tiles(h_ref, tm).astype(BF16)
    hid = (_silu(_dot(h, wsg_ref[...])) * _dot(h, wsu_ref[...])).astype(BF16)
    shared = _dot(hid, wsd_ref[...])
    lax.fori_loop(0, tm, drain, 0)

    w = w_ref[...]
    stride = TOP_K * TOK_ROWS
    slabs = []
    for s in range(TOK_ROWS):
        acc = jnp.zeros((tm, LANES), F32)
        for k in range(TOP_K):
            acc = acc + w[:, k:k + 1] * gbuf[slot, pl.ds(k * TOK_ROWS + s, tm, stride=stride), :]
        slabs.append(acc)
    routed = jnp.concatenate(slabs, axis=1)
    x2 = x1_ref[0] + g2_ref[0] * (routed + shared)
    o_ref[0] = _rms(x2, fn_ref[...])


def _combine(dest_tiles, top_w, x1, h2, gate2, shared, fnorm, ys_sorted, tile0, tm=COMBINE_TM):
    nb, t, _ = x1.shape
    nt = t // tm
    flat = lambda b, i: tile0 + b * nt + i
    nxt = lambda b, i: tile0 + jnp.minimum(b * nt + i + 1, nb * nt - 1)
    mod_spec = _mod_spec(gate2, tm)
    const = lambda a: pl.BlockSpec(a.shape, lambda b, i: (0, 0))
    return pl.pallas_call(
        functools.partial(_combine_kernel, tm=tm),
        grid=(nb, nt),
        in_specs=[pl.BlockSpec((1, 1, tm * TOP_K), lambda b, i: (flat(b, i), 0, 0), memory_space=pltpu.SMEM),
                  pl.BlockSpec((1, 1, tm * TOP_K), lambda b, i: (nxt(b, i), 0, 0), memory_space=pltpu.SMEM),
                  pl.BlockSpec((tm, LANES), lambda b, i: (flat(b, i), 0)),
                  pl.BlockSpec((1, tm, D_MODEL), lambda b, i: (b, i, 0)),
                  pl.BlockSpec((tm * TOK_ROWS, LANES), lambda b, i: (flat(b, i), 0)),
                  mod_spec, const(shared[0]), const(shared[1]), const(shared[2]), const(fnorm),
                  pl.BlockSpec(memory_space=pl.ANY)],
        out_specs=pl.BlockSpec((1, tm, D_MODEL), lambda b, i: (b, i, 0)),
        out_shape=jax.ShapeDtypeStruct((nb, t, D_MODEL), F32),
        scratch_shapes=[pltpu.VMEM((2, tm * TOP_K * TOK_ROWS, LANES), F32), pltpu.SemaphoreType.DMA((2,))],
        compiler_params=_cparams(("arbitrary", "arbitrary")),
        name="moe_combine",
    )(dest_tiles, dest_tiles, top_w, x1, h2, gate2, *shared, fnorm, ys_sorted)


def kernel(x_prompt, x_sample, c_prompt, c_sample, cache_nsa_kv, cache_win_kv, state_hgrn, page_table,
           attn_norm, ffn_norm, final_norm, hg_norm, w_ada, b_ada, w_in, hg_lb,
           cmp_pe, cmp_w1, cmp_b1, cmp_w2, w_out, w_router, router_bias,
           w_gate, w_up, w_down, ws_gate, ws_up, ws_down):
    nbp, t, _ = x_prompt.shape
    nbs, ts, _ = x_sample.shape
    ns = nbs * ts
    n_all = nbp * t + ns
    past_len = page_table.shape[1] * PAGE_SIZE

    c_all = jnp.concatenate([c_prompt, c_sample], axis=0)
    c_all = jnp.pad(c_all, ((0, -c_all.shape[0] % SUBLANES), (0, 0)))
    mod = _ada(c_all, w_ada[0], b_ada[0])
    modp = mod[:nbp].reshape(nbp, 1, 6, D_MODEL)
    mods = jnp.repeat(mod[nbp:nbp + nbs].reshape(nbs, 1, 6, D_MODEL), ts, axis=1).reshape(1, ns, 6, D_MODEL)

    w_pad = _pad_w_in(w_in[0])
    cw = _compress_weights(cmp_pe[0], cmp_w1[0], cmp_b1[0], cmp_w2[0])
    wo_hg, wo_nsa = _split_w_out(w_out[0])

    hg, qpad, kv4, kvw, gates, kvsel, kvwb = _inproj(
        x_prompt, modp[:, :, 1], modp[:, :, 0], attn_norm[0], w_pad, 512)
    hg_out, hg_state_p = _hgrn(hg, hg_lb, jnp.zeros((nbp, HG_HEADS, HG_DK, HG_DK), F32), hg_norm[0],
                               256, HG_CHUNK)
    n_pages_p = t // PAGE_SIZE
    ptp = jnp.arange(nbp * n_pages_p, dtype=jnp.int32).reshape(nbp, n_pages_p)
    kc, vc = _compress(ptp, kv4.reshape(nbp * n_pages_p, PAGE_SIZE, 4 * KV_W), *cw)
    nsa = _nsa_prompt(qpad, gates, kc, vc, kvsel, kvwb)
    x1p, h2p = _outproj(x_prompt, hg_out, nsa, modp[:, :, 2], modp[:, :, 4], modp[:, :, 3],
                        ffn_norm[0], wo_hg, wo_nsa, 512)

    xs = x_sample.reshape(1, ns, D_MODEL)
    hg_s, qpad_s, kv4_s, kvw_s, gates_s, _, _ = _inproj(
        xs, mods[:, :, 1], mods[:, :, 0], attn_norm[0], w_pad, ns)
    hg_out_s, hg_state_s = _hgrn(hg_s.reshape(nbs, ts, 4 * HG_WIDTH), hg_lb, state_hgrn[0], hg_norm[0], ts, ts)
    cache = cache_nsa_kv[0].transpose(0, 2, 3, 4, 1).reshape(-1, 4 * KV_W, PAGE_SIZE)
    win_buf = cache_win_kv[0].reshape(nbs, -1, 2 * KV_W)
    nsa_s = _nsa_sample(page_table, cache, win_buf, cw, qpad_s.reshape(nbs, ts, QPAD_W),
                        gates_s.reshape(nbs, ts, GZ_PAD), kv4_s.reshape(nbs, ts, 4 * KV_W),
                        kvw_s.reshape(nbs, ts, 2 * KV_W)).reshape(1, ns, QPAD_W)
    x1s, h2s = _outproj(xs, hg_out_s.reshape(1, ns, HG_WIDTH), nsa_s, mods[:, :, 2], mods[:, :, 4], mods[:, :, 3],
                        ffn_norm[0], wo_hg, wo_nsa, ns)

    h2 = jnp.concatenate([h2p, h2s], axis=0)
    top_e, top_w, rank, counts = _router(h2, w_router[0].astype(BF16), router_bias[0].reshape(1, -1), n_all)
    counts = counts[0].astype(jnp.int32)
    ends = jnp.cumsum(counts)
    starts = ends - counts
    dest = _moe_dest(top_e, rank, starts)[:, :TOP_K].reshape(-1)
    n_pairs = n_all * TOP_K
    n_blocks = n_pairs // MOE_BM
    items = _moe_items(counts, starts, ends, n_blocks)
    xs_sorted = _dispatch(dest.reshape(n_all // ROUTER_TM, 1, ROUTER_TM * TOP_K), h2, n_pairs)
    ys_sorted = _moe_gmm(items, starts, ends, xs_sorted, w_gate[0], w_up[0], w_down[0], n_blocks)
    dest_c = dest.reshape(n_all // COMBINE_TM, 1, COMBINE_TM * TOP_K)
    shared = (ws_gate[0].astype(BF16), ws_up[0].astype(BF16), ws_down[0].astype(BF16))
    fnorm = final_norm.reshape(1, -1)
    y_prompt = _combine(dest_c, top_w, x1p, h2, modp[:, :, 5], shared, fnorm, ys_sorted, 0)
    y_sample = _combine(dest_c, top_w, x1s, h2, mods[:, :, 5], shared, fnorm, ys_sorted, nbp * t // COMBINE_TM)

    wb = win_buf.shape[1]
    win_p = kvw[:, t - min(WINDOW, t):]
    win_s = jnp.concatenate([win_buf, kvw_s.reshape(nbs, ts, 2 * KV_W)], axis=1)[:, -wb:]
    kv_shape = (4, NSA_KV_HEADS, NSA_HEAD_DIM)
    win_shape = (2, NSA_KV_HEADS, NSA_HEAD_DIM)
    return (y_prompt,
            y_sample.reshape(nbs, ts, D_MODEL),
            kv4.reshape(1, nbp, t, *kv_shape),
            win_p.reshape(1, nbp, -1, *win_shape),
            hg_state_p[None],
            kv4_s.reshape(1, nbs, ts, *kv_shape),
            win_s.reshape(1, nbs, wb, *win_shape),
            hg_state_s[None])
```

```python
import functools

import jax
import jax.numpy as jnp
import numpy as np
from jax import lax
from jax.experimental import pallas as pl
from jax.experimental.pallas import tpu as pltpu

F32 = jnp.float32
BF16 = jnp.bfloat16

D_MODEL = 1024
HG_WIDTH = 512
HG_HEADS = 4
HG_DK = 128
HG_CHUNK = 32
NSA_WIDTH = 512
NSA_HEADS = 8
NSA_HEAD_DIM = 64
NSA_KV_HEADS = 2
NSA_GROUP = 4
KV_W = 128
CMP_BLOCK = 32
CMP_STRIDE = 16
CMP_HIDDEN = 256
SEL_BLOCK = 64
SEL_TOPN = 16
WINDOW = 512
Q_BLOCK = 128
N_EXPERTS = 256
TOP_K = 8
N_GROUPS = 8
TOPK_GROUPS = 4
MOE_D_FF = 256
ROUTED_SCALE = 2.5
RMS_EPS = 1e-6
PAGE_SIZE = 128
IN_COLS = 4 * HG_WIDTH + NSA_WIDTH + 4 * KV_W + 2 * KV_W + 3 * NSA_HEADS

LANES = 128
SUBLANES = 8
TOK_ROWS = D_MODEL // LANES
VMEM_LIMIT = 56 * 1024 * 1024

QPAD_W = NSA_HEADS * LANES
GZ_PAD = LANES
INP_COLS = 4 * HG_WIDTH + QPAD_W + 4 * KV_W + 2 * KV_W + GZ_PAD

NEG = -1e30
PAGES_PER_STEP = 32
MOE_BM = 512
ROUTER_TM = 256
COMBINE_TM = 128


def _cparams(sem):
    return pltpu.CompilerParams(dimension_semantics=sem, vmem_limit_bytes=VMEM_LIMIT)


def _dot(a, b):
    return jnp.dot(a, b, preferred_element_type=F32)


def _dot_nt(a, b):
    return lax.dot_general(a, b, (((1,), (1,)), ((), ())), preferred_element_type=F32)


def _dot_tn(a, b):
    return lax.dot_general(a, b, (((0,), (0,)), ((), ())), preferred_element_type=F32)


def _rms(x, g):
    return x * lax.rsqrt(jnp.mean(x * x, axis=-1, keepdims=True) + RMS_EPS) * g


def _silu(x):
    return x * jax.nn.sigmoid(x)


Q_SCALE = NSA_HEAD_DIM ** -0.5 * 1.4426950408889634


def _masked_softmax(s, valid):
    s = jnp.where(valid, s, NEG)
    m = jnp.max(s, axis=1, keepdims=True)
    p = jnp.exp2(s - m) * valid.astype(F32)
    return p / jnp.maximum(jnp.sum(p, axis=1, keepdims=True), 1e-30)


def _topk_mask(v, k):
    lane = lax.broadcasted_iota(jnp.int32, v.shape, 1).astype(F32)
    sel = jnp.zeros(v.shape, F32)
    for _ in range(k):
        m = jnp.max(v, axis=1, keepdims=True)
        idx = jnp.min(jnp.where(v == m, lane, 1e9), axis=1, keepdims=True)
        pick = lane == idx
        sel = jnp.where(pick, 1.0, sel)
        v = jnp.where(pick, 3.0 * NEG, v)
    return sel


def _mod_spec(mod, tm):
    if mod.shape[1] == 1:
        return pl.BlockSpec((1, 1, D_MODEL), lambda b, i: (b, 0, 0))
    return pl.BlockSpec((1, tm, D_MODEL), lambda b, i: (b, i, 0))


def _load_tok_tiles(ref, n_tok):
    return jnp.concatenate([ref[pl.ds(s, n_tok, stride=TOK_ROWS), :] for s in range(TOK_ROWS)], axis=1)


def _store_tok_tiles(ref, val, n_tok):
    for s in range(TOK_ROWS):
        ref[pl.ds(s, n_tok, stride=TOK_ROWS), :] = val[:, s * LANES:(s + 1) * LANES]


def _ada_kernel(c_ref, w_ref, b_ref, o_ref):
    s = _silu(c_ref[...]).astype(BF16)
    o_ref[...] = _dot(s, w_ref[...].astype(BF16)) + b_ref[...]


def _ada(c_all, w_ada, b_ada):
    n = c_all.shape[0]
    return pl.pallas_call(
        _ada_kernel,
        grid=(6,),
        in_specs=[pl.BlockSpec((n, D_MODEL), lambda j: (0, 0)),
                  pl.BlockSpec((D_MODEL, D_MODEL), lambda j: (0, j)),
                  pl.BlockSpec((1, D_MODEL), lambda j: (0, j))],
        out_specs=pl.BlockSpec((n, D_MODEL), lambda j: (0, j)),
        out_shape=jax.ShapeDtypeStruct((n, 6 * D_MODEL), F32),
        compiler_params=_cparams(("arbitrary",)),
        name="ada_mod",
    )(c_all, w_ada, b_ada.reshape(1, -1))


def _inproj_kernel(x_ref, sc_ref, sh_ref, g_ref, w_ref,
                   hg_ref, q_ref, kv4_ref, kvw_ref, gate_ref, kvsel_ref, kvwb_ref):
    h = _rms(x_ref[0], g_ref[...]) * (1.0 + sc_ref[0]) + sh_ref[0]
    z = _dot(h.astype(BF16), w_ref[...])
    c0 = 4 * HG_WIDTH
    hg_ref[0] = z[:, :c0]
    q_ref[0] = (z[:, c0:c0 + QPAD_W] * Q_SCALE).astype(BF16)
    c1 = c0 + QPAD_W
    kv4 = z[:, c1:c1 + 4 * KV_W]
    kv4_ref[0] = kv4
    kvsel_ref[0] = kv4[:, 2 * KV_W:].astype(BF16)
    c2 = c1 + 4 * KV_W
    kvw = z[:, c2:c2 + 2 * KV_W]
    kvw_ref[0] = kvw
    kvwb_ref[0] = kvw.astype(BF16)
    gate_ref[0] = jax.nn.sigmoid(z[:, c2 + 2 * KV_W:])


def _inproj(x, scale, shift, g_norm, w_pad, tm):
    nb, t, _ = x.shape
    mod_spec = _mod_spec(scale, tm)
    widths = [(4 * HG_WIDTH, F32), (QPAD_W, BF16), (4 * KV_W, F32), (2 * KV_W, F32), (GZ_PAD, F32),
              (2 * KV_W, BF16), (2 * KV_W, BF16)]
    return pl.pallas_call(
        _inproj_kernel,
        grid=(nb, t // tm),
        in_specs=[pl.BlockSpec((1, tm, D_MODEL), lambda b, i: (b, i, 0)),
                  mod_spec, mod_spec,
                  pl.BlockSpec((1, D_MODEL), lambda b, i: (0, 0)),
                  pl.BlockSpec((D_MODEL, INP_COLS), lambda b, i: (0, 0))],
        out_specs=[pl.BlockSpec((1, tm, w), lambda b, i: (b, i, 0)) for w, _ in widths],
        out_shape=[jax.ShapeDtypeStruct((nb, t, w), dt) for w, dt in widths],
        compiler_params=_cparams(("arbitrary", "arbitrary")),
        name="in_proj",
    )(x, scale, shift, g_norm.reshape(1, -1), w_pad)


def _pad_w_in(w_in):
    c0 = 4 * HG_WIDTH
    wq = w_in[:, c0:c0 + NSA_WIDTH].reshape(D_MODEL, NSA_HEADS, NSA_HEAD_DIM)
    zeros = jnp.zeros_like(wq)
    lo = jnp.concatenate([wq, zeros], axis=-1)
    hi = jnp.concatenate([zeros, wq], axis=-1)
    grp = (jnp.arange(NSA_HEADS) // NSA_GROUP)[None, :, None]
    wq_pad = jnp.where(grp == 0, lo, hi).reshape(D_MODEL, QPAD_W)
    c1 = c0 + NSA_WIDTH
    rest = w_in[:, c1:c1 + 6 * KV_W]
    gz = jnp.pad(w_in[:, c1 + 6 * KV_W:], ((0, 0), (0, GZ_PAD - 3 * NSA_HEADS)))
    return jnp.concatenate([w_in[:, :c0], wq_pad, rest, gz], axis=1).astype(BF16)


def _hgrn_kernel(q_ref, f_ref, v_ref, gt_ref, lb_ref, s0_ref, gn_ref, o_ref, s_out_ref, st_scr,
                 *, chunk, n_chunks):
    i = pl.program_id(1)

    @pl.when(i == 0)
    def _():
        for h in range(HG_HEADS):
            st_scr[h] = s0_ref[0, h].T

    lbr = lb_ref[...]
    e = jnp.exp(lbr - jnp.max(lbr, axis=0, keepdims=True))
    lb_all = e[0:1] / jnp.sum(e, axis=0, keepdims=True)
    row = lax.broadcasted_iota(jnp.int32, (chunk, HG_DK), 0)
    causal = (lax.broadcasted_iota(jnp.int32, (chunk, chunk), 0)
              >= lax.broadcasted_iota(jnp.int32, (chunk, chunk), 1))
    st = [st_scr[h] for h in range(HG_HEADS)]
    for c in range(n_chunks):
        sl = pl.ds(c * chunk, chunk)
        for h in range(HG_HEADS):
            hs = slice(h * HG_DK, (h + 1) * HG_DK)
            lb = lb_all[:, hs]
            z = f_ref[0, sl, hs]
            log_f = jnp.log(lb + (1.0 - lb) * jax.nn.sigmoid(z))
            kk = (1.0 - lb) * jax.nn.sigmoid(-z)
            a = log_f
            s = 1
            while s < chunk:
                a = a + jnp.where(row >= s, pltpu.roll(a, s, 0), 0.0)
                s *= 2
            qt = (q_ref[0, sl, hs] * jnp.exp(a)).astype(BF16)
            kt = (kk * jnp.exp(-a)).astype(BF16)
            v = v_ref[0, sl, hs].astype(BF16)
            att = jnp.where(causal, _dot_nt(qt, kt), 0.0)
            o = _dot(att.astype(BF16), v) + _dot_nt(qt, st[h].astype(BF16))
            a_end = a[chunk - 1:chunk, :]
            kd = (kk * jnp.exp(a_end - a)).astype(BF16)
            st[h] = st[h] * jnp.exp(a_end) + _dot_tn(v, kd)
            o = _rms(o, gn_ref[...]) * _silu(gt_ref[0, sl, hs])
            o_ref[0, sl, hs] = o.astype(o_ref.dtype)
    for h in range(HG_HEADS):
        st_scr[h] = st[h]

    @pl.when(i == pl.num_programs(1) - 1)
    def _():
        for h in range(HG_HEADS):
            s_out_ref[0, h] = st[h].T


def _hgrn(hg, hg_lb, s0, g_norm, tc, chunk):
    nb, t, _ = hg.shape
    part = lambda k: pl.BlockSpec((1, tc, HG_WIDTH), lambda b, i: (b, i, k))
    st_spec = pl.BlockSpec((1, HG_HEADS, HG_DK, HG_DK), lambda b, i: (b, 0, 0, 0))
    return pl.pallas_call(
        functools.partial(_hgrn_kernel, chunk=chunk, n_chunks=tc // chunk),
        grid=(nb, t // tc),
        in_specs=[part(0), part(1), part(2), part(3),
                  pl.BlockSpec(hg_lb.shape, lambda b, i: (0, 0)),
                  st_spec,
                  pl.BlockSpec((1, HG_DK), lambda b, i: (0, 0))],
        out_specs=[pl.BlockSpec((1, tc, HG_WIDTH), lambda b, i: (b, i, 0)), st_spec],
        out_shape=[jax.ShapeDtypeStruct((nb, t, HG_WIDTH), BF16),
                   jax.ShapeDtypeStruct((nb, HG_HEADS, HG_DK, HG_DK), F32)],
        scratch_shapes=[pltpu.VMEM((HG_HEADS, HG_DK, HG_DK), F32)],
        compiler_params=_cparams(("arbitrary", "arbitrary")),
        name="hgrn2",
    )(hg, hg, hg, hg, hg_lb, s0, g_norm.reshape(1, -1))


def _gelu_tanh(x):
    return 0.5 * x * (1.0 + jnp.tanh(0.7978845608028654 * (x + 0.044715 * x * x * x)))


def _page_copies(pt_ref, cache_ref, buf, sem, b, s, slot, n_pages, pps, col0, tail, transposed):
    copies = []
    base = s * pps
    nxt = pt_ref[b, jnp.minimum(base + pps, n_pages - 1)]
    for br in range(2):
        cols = pl.ds(col0 + br * KV_W, KV_W)
        for i in range(pps):
            pg = pt_ref[b, base + i]
            if transposed:
                copies.append(pltpu.make_async_copy(
                    cache_ref.at[pg, cols, :],
                    buf.at[slot, br, :, pl.ds(i * PAGE_SIZE, PAGE_SIZE)], sem.at[slot]))
            else:
                copies.append(pltpu.make_async_copy(
                    cache_ref.at[pg, :, cols],
                    buf.at[slot, br, pl.ds(i * PAGE_SIZE, PAGE_SIZE), :], sem.at[slot]))
        if tail and transposed:
            copies.append(pltpu.make_async_copy(
                cache_ref.at[nxt, cols, :],
                buf.at[slot, br, :, pl.ds(pps * PAGE_SIZE, PAGE_SIZE)], sem.at[slot]))
        elif tail:
            copies.append(pltpu.make_async_copy(
                cache_ref.at[nxt, pl.ds(0, CMP_STRIDE), cols],
                buf.at[slot, br, pl.ds(pps * PAGE_SIZE, CMP_STRIDE), :], sem.at[slot]))
    return copies


def _stream_pages(pt_ref, cache_ref, buf, sem, n_pages, n_steps, pps, col0, tail, transposed):
    b = pl.program_id(0)
    s = pl.program_id(1)
    n = b * n_steps + s
    total = pl.num_programs(0) * n_steps
    slot = n % 2
    args = (n_pages, pps, col0, tail, transposed)

    @pl.when(n == 0)
    def _():
        for cp in _page_copies(pt_ref, cache_ref, buf, sem, b, s, slot, *args):
            cp.start()

    @pl.when(n + 1 < total)
    def _():
        n1 = n + 1
        for cp in _page_copies(pt_ref, cache_ref, buf, sem, n1 // n_steps, n1 % n_steps, 1 - slot, *args):
            cp.start()

    for cp in _page_copies(pt_ref, cache_ref, buf, sem, b, s, slot, *args):
        cp.wait()
    return slot


def _compress_kernel(pt_ref, cache_ref, pe_ref, w1_ref, b1_ref, w2_ref, kc_ref, vc_ref, buf, sem, *rowbuf,
                     n_pages, n_steps, pps, transposed):
    groups = pps * PAGE_SIZE // CMP_STRIDE
    slot = _stream_pages(pt_ref, cache_ref, buf, sem, n_pages, n_steps, pps, 0, True, transposed)

    if transposed:
        rows_ref, = rowbuf
        for br in range(2):
            for i in range(pps + 1):
                n_rows = PAGE_SIZE if i < pps else CMP_STRIDE
                page_t = buf[slot, br, :, i * PAGE_SIZE:(i + 1) * PAGE_SIZE]
                rows_ref[br, i * PAGE_SIZE:i * PAGE_SIZE + n_rows, :] = page_t.T[:n_rows]
        read_rows = lambda br, l: rows_ref[br, pl.ds(l, groups, stride=CMP_STRIDE), :]
    else:
        read_rows = lambda br, l: buf[slot, br, pl.ds(l, groups, stride=CMP_STRIDE), :]

    for br, out_ref in ((0, kc_ref), (1, vc_ref)):
        acc = jnp.zeros((groups, 2 * CMP_HIDDEN), F32)
        for l in range(CMP_BLOCK):
            xl = (read_rows(br, l) + pe_ref[br, l:l + 1, :]).astype(BF16)
            acc = acc + _dot(xl, w1_ref[br, l])
        hid = _gelu_tanh(acc + b1_ref[br]).astype(BF16)
        out_ref[0] = _dot(hid, w2_ref[br]).astype(out_ref.dtype)


def _compress(page_table, cache, pe2, w1cat, b1cat, w2bd, pps=PAGES_PER_STEP, transposed=False):
    nb, n_pages = page_table.shape
    n_steps = n_pages // pps
    groups = pps * PAGE_SIZE // CMP_STRIDE
    rows = pps * PAGE_SIZE + CMP_STRIDE
    const = lambda shape: pl.BlockSpec(shape, lambda b, s, pt: (0,) * len(shape))
    out_spec = pl.BlockSpec((1, groups, KV_W), lambda b, s, pt: (b, s, 0))
    out_sds = jax.ShapeDtypeStruct((nb, n_steps * groups, KV_W), BF16)
    if transposed:
        stage = [pltpu.VMEM((2, 2, KV_W, (pps + 1) * PAGE_SIZE), F32), pltpu.SemaphoreType.DMA((2,)),
                 pltpu.VMEM((2, rows, KV_W), F32)]
    else:
        stage = [pltpu.VMEM((2, 2, rows, KV_W), F32), pltpu.SemaphoreType.DMA((2,))]
    return pl.pallas_call(
        functools.partial(_compress_kernel, n_pages=n_pages, n_steps=n_steps, pps=pps, transposed=transposed),
        grid_spec=pltpu.PrefetchScalarGridSpec(
            num_scalar_prefetch=1,
            grid=(nb, n_steps),
            in_specs=[pl.BlockSpec(memory_space=pl.ANY),
                      const((2, CMP_BLOCK, KV_W)),
                      const((2, CMP_BLOCK, KV_W, 2 * CMP_HIDDEN)),
                      const((2, 1, 2 * CMP_HIDDEN)),
                      const((2, 2 * CMP_HIDDEN, KV_W))],
            out_specs=[out_spec, out_spec],
            scratch_shapes=stage),
        out_shape=[out_sds, out_sds],
        compiler_params=_cparams(("arbitrary", "arbitrary")),
        name="nsa_compress",
    )(page_table, cache, pe2, w1cat, b1cat, w2bd)


def _compress_weights(cmp_pe, cmp_w1, cmp_b1, cmp_w2):
    pe2 = jnp.concatenate([cmp_pe, cmp_pe], axis=-1)
    w1 = cmp_w1.reshape(2, CMP_BLOCK, NSA_HEAD_DIM, CMP_HIDDEN)
    z1 = jnp.zeros_like(w1)
    w1cat = jnp.concatenate([jnp.concatenate([w1, z1], axis=-1),
                             jnp.concatenate([z1, w1], axis=-1)], axis=2).astype(BF16)
    b1cat = jnp.concatenate([cmp_b1, cmp_b1], axis=-1)[:, None, :]
    z2 = jnp.zeros_like(cmp_w2)
    w2bd = jnp.concatenate([jnp.concatenate([cmp_w2, z2], axis=-1),
                            jnp.concatenate([z2, cmp_w2], axis=-1)], axis=1).astype(BF16)
    return pe2, w1cat, b1cat, w2bd


def _overlap_matrix(n_cmp, n_sel):
    cs = lax.broadcasted_iota(jnp.int32, (n_cmp, n_sel), 0) * CMP_STRIDE
    ss = lax.broadcasted_iota(jnp.int32, (n_cmp, n_sel), 1) * SEL_BLOCK
    return ((cs < ss + SEL_BLOCK) & (cs + CMP_BLOCK > ss)).astype(BF16)


SEL_CHUNK = 512
SPREAD_KEYS = 1024
WIN_SPAN = WINDOW + Q_BLOCK


def _nsa_prompt_kernel(q_ref, gt_ref, kc_ref, vc_ref, ks_ref, vs_ref, kw_ref, vw_ref, ex_ref, o_ref,
                       m_scr, acc_scr, chosen_scr, oc_scr, sa_scr, sb_scr, *, n_cmp, n_sel):
    j = pl.program_id(1)
    q0 = j * Q_BLOCK
    tok = lax.broadcasted_iota(jnp.int32, (Q_BLOCK, 1), 0) + q0
    tok4 = jnp.concatenate([tok] * NSA_GROUP, axis=0)
    n_chunks = j // (SEL_CHUNK // Q_BLOCK) + 1
    key_in_span = lax.broadcasted_iota(jnp.int32, (1, SPREAD_KEYS), 1)

    def load_q(g):
        return jnp.concatenate([q_ref[0, :, (NSA_GROUP * g + jh) * LANES:(NSA_GROUP * g + jh + 1) * LANES]
                                for jh in range(NSA_GROUP)], axis=0)

    def softmax_av(s, valid, v):
        s = jnp.where(valid, s, NEG)
        tiles = [s[:, i * LANES:(i + 1) * LANES] for i in range(s.shape[1] // LANES)]
        m = jnp.max(functools.reduce(jnp.maximum, tiles), axis=1, keepdims=True)
        p = jnp.where(valid, jnp.exp2(s - m), 0.0)
        acc = _dot(p.astype(BF16), jnp.concatenate([v, jnp.ones(v.shape, BF16)], axis=1))
        inv = 1.0 / jnp.maximum(acc[:, KV_W:], 1e-30)
        return p, acc[:, :KV_W] * inv, inv

    ov = _overlap_matrix(n_cmp, n_sel)
    cend = lax.broadcasted_iota(jnp.int32, (1, n_cmp), 1) * CMP_STRIDE + (CMP_BLOCK - 1)
    ws = pl.multiple_of(jnp.maximum(q0 - WINDOW, 0), Q_BLOCK)
    wpos = ws + lax.broadcasted_iota(jnp.int32, (1, WIN_SPAN), 1)
    d = tok4 - wpos
    in_window = (d >= 0) & (d < WINDOW)
    gates = gt_ref[0]
    imps = []
    for g in range(NSA_KV_HEADS):
        q = load_q(g)
        p, o_c, inv = softmax_av(_dot_nt(q, kc_ref[0]), cend <= tok4, vc_ref[0])
        _, o_w, _ = softmax_av(_dot_nt(q, kw_ref[0, pl.ds(ws, WIN_SPAN), :]), in_window,
                               vw_ref[0, pl.ds(ws, WIN_SPAN), :])
        psum = jnp.zeros((Q_BLOCK, n_cmp), F32)
        for jh in range(NSA_GROUP):
            h = NSA_GROUP * g + jh
            r = slice(jh * Q_BLOCK, (jh + 1) * Q_BLOCK)
            psum = psum + p[r] * jnp.concatenate([inv[r]] * (n_cmp // LANES), axis=1)
            oc_scr[g, r, :] = gates[:, 3 * h:3 * h + 1] * o_c[r] + gates[:, 3 * h + 2:3 * h + 3] * o_w[r]
        imps.append(_dot(psum.astype(BF16), ov))

    blk = lax.broadcasted_iota(jnp.int32, (1, n_sel), 1)
    cur = lax.shift_right_logical(tok, 6)
    forced = (blk == 0) | (blk == cur) | (blk == cur - 1)
    free = (blk * SEL_BLOCK <= tok) & jnp.logical_not(forced)
    forced2 = jnp.concatenate([forced] * NSA_KV_HEADS, axis=0)
    free2 = jnp.concatenate([free] * NSA_KV_HEADS, axis=0)
    best = _topk_mask(jnp.where(free2, jnp.concatenate(imps, axis=0), NEG), SEL_TOPN - 3)
    sel2 = jnp.where(forced2, 1.0, best).astype(BF16)

    ones_blk = jnp.ones((SEL_CHUNK, KV_W), BF16)

    for g in range(NSA_KV_HEADS):
        sel = sel2[g * Q_BLOCK:(g + 1) * Q_BLOCK]
        m_scr[...] = jnp.full(m_scr.shape, NEG, F32)
        acc_scr[...] = jnp.zeros(acc_scr.shape, F32)

        def spread(i, carry):
            c0 = pl.multiple_of(i * SPREAD_KEYS, SPREAD_KEYS)
            keep = (_dot(sel, ex_ref[:, pl.ds(c0, SPREAD_KEYS)]) > 0.5) & (key_in_span + c0 <= tok)
            chosen_scr[:, pl.ds(c0, SPREAD_KEYS)] = jnp.where(keep, 0.0, NEG)
            return carry

        lax.fori_loop(0, (n_chunks * SEL_CHUNK + SPREAD_KEYS - 1) // SPREAD_KEYS, spread, 0)

        heads = range(NSA_GROUP)
        rows = [pl.ds(jh * Q_BLOCK, Q_BLOCK) for jh in heads]

        def stage_scores(c, buf):
            k0 = pl.multiple_of(c * SEL_CHUNK, SEL_CHUNK)
            bias = chosen_scr[:, pl.ds(k0, SEL_CHUNK)]
            kblk = ks_ref[0, pl.ds(k0, SEL_CHUNK), :]
            for jh in heads:
                h = NSA_GROUP * g + jh
                buf[rows[jh], :] = _dot_nt(q_ref[0, :, h * LANES:(h + 1) * LANES], kblk) + bias

        def accumulate(c, buf):
            k0 = pl.multiple_of(c * SEL_CHUNK, SEL_CHUNK)
            vext = jnp.concatenate([vs_ref[0, pl.ds(k0, SEL_CHUNK), :], ones_blk], axis=1)
            tiles = [[buf[r, i * LANES:(i + 1) * LANES] for i in range(SEL_CHUNK // LANES)] for r in rows]
            m_old = [m_scr[r, :] for r in rows]
            m_new = [jnp.maximum(mo, jnp.max(functools.reduce(jnp.maximum, t), axis=1, keepdims=True))
                     for mo, t in zip(m_old, tiles)]
            probs = [jnp.concatenate([jnp.exp2(x - mn) for x in t], axis=1).astype(BF16)
                     for mn, t in zip(m_new, tiles)]
            pv = [_dot(p, vext) for p in probs]
            for r, mo, mn, y in zip(rows, m_old, m_new, pv):
                alpha = jnp.exp2(mo - mn)
                acc_scr[r, :] = jnp.concatenate([alpha, alpha], axis=1) * acc_scr[r, :] + y
                m_scr[r, :] = mn

        n_pairs = (n_chunks + 1) // 2
        stage_scores(0, sa_scr)

        def body(i, carry):
            stage_scores(2 * i + 1, sb_scr)
            accumulate(2 * i, sa_scr)
            stage_scores(jnp.minimum(2 * i + 2, 2 * n_pairs - 1), sa_scr)
            accumulate(2 * i + 1, sb_scr)
            return carry

        lax.fori_loop(0, n_pairs, body, 0)
        o_s = acc_scr[:, :KV_W] / jnp.maximum(acc_scr[:, KV_W:], 1e-30)

        for jh in range(NSA_GROUP):
            h = NSA_GROUP * g + jh
            r = slice(jh * Q_BLOCK, (jh + 1) * Q_BLOCK)
            o = oc_scr[g, r, :] + gates[:, 3 * h + 1:3 * h + 2] * o_s[r]
            o_ref[0, :, h * LANES:(h + 1) * LANES] = o.astype(o_ref.dtype)


def _nsa_prompt(qpad, gates, kc, vc, kvsel, kvwb):
    nb, t, _ = qpad.shape
    assert t % SPREAD_KEYS == 0 and SPREAD_KEYS == 2 * SEL_CHUNK and t >= WIN_SPAN
    n_cmp = kc.shape[1]
    n_sel = t // SEL_BLOCK
    full = lambda w, k: pl.BlockSpec((1, t, w), lambda b, j: (b, 0, k))
    t_pad = -(-t // SPREAD_KEYS) * SPREAD_KEYS
    expand = jnp.asarray(np.arange(n_sel)[:, None] == (np.arange(t_pad)[None, :] // SEL_BLOCK), BF16)
    return pl.pallas_call(
        functools.partial(_nsa_prompt_kernel, n_cmp=n_cmp, n_sel=n_sel),
        grid=(nb, t // Q_BLOCK),
        in_specs=[pl.BlockSpec((1, Q_BLOCK, QPAD_W), lambda b, j: (b, j, 0)),
                  pl.BlockSpec((1, Q_BLOCK, GZ_PAD), lambda b, j: (b, j, 0)),
                  pl.BlockSpec((1, n_cmp, KV_W), lambda b, j: (b, 0, 0)),
                  pl.BlockSpec((1, n_cmp, KV_W), lambda b, j: (b, 0, 0)),
                  full(KV_W, 0), full(KV_W, 1), full(KV_W, 0), full(KV_W, 1),
                  pl.BlockSpec((n_sel, t_pad), lambda b, j: (0, 0))],
        out_specs=pl.BlockSpec((1, Q_BLOCK, QPAD_W), lambda b, j: (b, j, 0)),
        out_shape=jax.ShapeDtypeStruct((nb, t, QPAD_W), BF16),
        scratch_shapes=[pltpu.VMEM((NSA_GROUP * Q_BLOCK, LANES), F32),
                        pltpu.VMEM((NSA_GROUP * Q_BLOCK, 2 * KV_W), F32),
                        pltpu.VMEM((Q_BLOCK, t_pad), F32),
                        pltpu.VMEM((NSA_KV_HEADS, NSA_GROUP * Q_BLOCK, KV_W), F32),
                        pltpu.VMEM((NSA_GROUP * Q_BLOCK, SEL_CHUNK), F32),
                        pltpu.VMEM((NSA_GROUP * Q_BLOCK, SEL_CHUNK), F32)],
        compiler_params=_cparams(("arbitrary", "arbitrary")),
        name="nsa_prompt",
    )(qpad, gates, kc, vc, kvsel, kvsel, kvwb, kvwb, expand)


def _nsa_sample_a_kernel(q_ref, g_ref, kc_ref, vc_ref, wb_ref, nw_ref, ocw_ref, sel_ref,
                         *, past_len, n_tok, n_sel, n_sel_pad):
    q = q_ref[0]
    rows = q.shape[0]
    n_cmp = kc_ref.shape[1]
    t_row = lax.broadcasted_iota(jnp.int32, (rows, 1), 0) & (n_tok - 1)
    qpos = past_len + t_row
    cend = lax.broadcasted_iota(jnp.int32, (1, n_cmp), 1) * CMP_STRIDE + (CMP_BLOCK - 1)
    p_c = _masked_softmax(_dot_nt(q, kc_ref[0]), cend <= qpos)
    o_c = _dot(p_c.astype(BF16), vc_ref[0])

    per_grp = NSA_GROUP * n_tok
    psum = jnp.concatenate(
        [sum(p_c[g * per_grp + jh * n_tok:g * per_grp + (jh + 1) * n_tok] for jh in range(NSA_GROUP))
         for g in range(NSA_KV_HEADS)], axis=0)
    imp = _dot(psum.astype(BF16), _overlap_matrix(n_cmp, n_sel_pad))
    blk = lax.broadcasted_iota(jnp.int32, (1, n_sel_pad), 1)
    tq = past_len + (lax.broadcasted_iota(jnp.int32, (NSA_KV_HEADS * n_tok, 1), 0) & (n_tok - 1))
    cur = lax.shift_right_logical(tq, 6)
    forced = (blk == 0) | (blk == cur) | (blk == cur - 1)
    allowed = blk * SEL_BLOCK <= tq
    v = jnp.where(forced, -NEG, jnp.where(allowed, imp, NEG))
    sel_ref[0] = _topk_mask(jnp.where(blk < n_sel, v, 2.0 * NEG), SEL_TOPN)

    wb = wb_ref.shape[1]
    kw = wb_ref[0, :, 0:KV_W].astype(BF16)
    vw = wb_ref[0, :, KV_W:2 * KV_W].astype(BF16)
    kn = nw_ref[0, :, 0:KV_W].astype(BF16)
    vn = nw_ref[0, :, KV_W:2 * KV_W].astype(BF16)
    i1 = lax.broadcasted_iota(jnp.int32, (1, wb), 1)
    d1 = t_row + wb - i1
    valid1 = (d1 >= 0) & (d1 < WINDOW) & (past_len - wb + i1 >= 0)
    i2 = lax.broadcasted_iota(jnp.int32, (1, nw_ref.shape[1]), 1)
    d2 = t_row - i2
    valid2 = (d2 >= 0) & (d2 < WINDOW) & (i2 < n_tok)
    s1 = jnp.where(valid1, _dot_nt(q, kw), NEG)
    s2 = jnp.where(valid2, _dot_nt(q, kn), NEG)
    m = jnp.maximum(jnp.max(s1, axis=1, keepdims=True), jnp.max(s2, axis=1, keepdims=True))
    p1 = jnp.exp2(s1 - m) * valid1.astype(F32)
    p2 = jnp.exp2(s2 - m) * valid2.astype(F32)
    den = jnp.maximum(jnp.sum(p1, axis=1, keepdims=True) + jnp.sum(p2, axis=1, keepdims=True), 1e-30)
    o_w = (_dot(p1.astype(BF16), vw) + _dot(p2.astype(BF16), vn)) / den
    g = g_ref[0]
    ocw_ref[0] = g[:, 0:1] * o_c + g[:, 2:3] * o_w


def _nsa_sample_a(q_rows, g_rows, kc, vc, win_buf, new_win, *, past_len, n_tok):
    nb, rows, _ = q_rows.shape
    n_sel = -(-(past_len + n_tok) // SEL_BLOCK)
    n_sel_pad = -(-n_sel // LANES) * LANES
    blk3 = lambda a: pl.BlockSpec((1,) + a.shape[1:], lambda b: (b, 0, 0))
    return pl.pallas_call(
        functools.partial(_nsa_sample_a_kernel, past_len=past_len, n_tok=n_tok, n_sel=n_sel, n_sel_pad=n_sel_pad),
        grid=(nb,),
        in_specs=[blk3(q_rows), blk3(g_rows), blk3(kc), blk3(vc), blk3(win_buf), blk3(new_win)],
        out_specs=[pl.BlockSpec((1, rows, KV_W), lambda b: (b, 0, 0)),
                   pl.BlockSpec((1, NSA_KV_HEADS * n_tok, n_sel_pad), lambda b: (b, 0, 0))],
        out_shape=[jax.ShapeDtypeStruct((nb, rows, KV_W), F32),
                   jax.ShapeDtypeStruct((nb, NSA_KV_HEADS * n_tok, n_sel_pad), F32)],
        compiler_params=_cparams(("arbitrary",)),
        name="nsa_sample_a",
    )(q_rows, g_rows, kc, vc, win_buf, new_win)


def _nsa_sample_b_kernel(pt_ref, cache_ref, q_ref, g_ref, sel_ref, seln_ref, ns_ref, ocw_ref, ex_ref, o_ref,
                         buf, sem, m_scr, l_scr, acc_scr, *, n_pages, n_steps, pps, n_tok):
    s = pl.program_id(1)
    slot = _stream_pages(pt_ref, cache_ref, buf, sem, n_pages, n_steps, pps, 2 * KV_W, False, True)
    q = q_ref[0]
    rows = q.shape[0]

    @pl.when(s == 0)
    def _():
        m_scr[...] = jnp.full(m_scr.shape, NEG, F32)
        l_scr[...] = jnp.zeros(l_scr.shape, F32)
        acc_scr[...] = jnp.zeros(acc_scr.shape, F32)

    def update(scores, msk, times_v):
        sc = jnp.where(msk, scores, NEG)
        m_old = m_scr[...]
        m_new = jnp.maximum(m_old, jnp.max(sc, axis=1, keepdims=True))
        p = jnp.exp2(sc - m_new) * msk.astype(F32)
        alpha = jnp.exp2(m_old - m_new)
        l_scr[...] = alpha * l_scr[...] + jnp.sum(p, axis=1, keepdims=True)
        acc_scr[...] = alpha * acc_scr[...] + times_v(p.astype(BF16))
        m_scr[...] = m_new

    chosen = _dot(sel_ref[0, 0], ex_ref[...]) > 0.5
    update(_dot(q, buf[slot, 0].astype(BF16)), chosen, lambda p: _dot_nt(p, buf[slot, 1].astype(BF16)))

    @pl.when(s == n_steps - 1)
    def _():
        t_row = lax.broadcasted_iota(jnp.int32, (rows, 1), 0) & (n_tok - 1)
        i2 = lax.broadcasted_iota(jnp.int32, (1, ns_ref.shape[1]), 1)
        msk = (seln_ref[0, 0][:, 0:1] > 0.5) & (i2 <= t_row) & (i2 < n_tok)
        update(_dot_nt(q, ns_ref[0, :, 0:KV_W].astype(BF16)), msk,
               lambda p: _dot(p, ns_ref[0, :, KV_W:2 * KV_W].astype(BF16)))
        o_s = acc_scr[...] / jnp.maximum(l_scr[...], 1e-30)
        o_ref[0] = ocw_ref[0] + g_ref[0][:, 1:2] * o_s


def _nsa_sample_b(page_table, cache, q_rows, g_rows, sel_steps, new_sel, ocw, *, n_tok, pps=PAGES_PER_STEP):
    nb, n_pages = page_table.shape
    n_steps = n_pages // pps
    rows = q_rows.shape[1]
    keys = pps * PAGE_SIZE
    expand = jnp.asarray(np.arange(LANES)[:, None] == (np.arange(keys)[None, :] // SEL_BLOCK), BF16)
    per_b = lambda a: pl.BlockSpec((1,) + a.shape[1:], lambda b, s, pt: (b, 0, 0))
    return pl.pallas_call(
        functools.partial(_nsa_sample_b_kernel, n_pages=n_pages, n_steps=n_steps, pps=pps, n_tok=n_tok),
        grid_spec=pltpu.PrefetchScalarGridSpec(
            num_scalar_prefetch=1,
            grid=(nb, n_steps),
            in_specs=[pl.BlockSpec(memory_space=pl.ANY),
                      per_b(q_rows), per_b(g_rows),
                      pl.BlockSpec((1, 1, rows, LANES), lambda b, s, pt: (b, s, 0, 0)),
                      pl.BlockSpec((1, 1, rows, LANES), lambda b, s, pt: (b, n_steps, 0, 0)),
                      per_b(new_sel), per_b(ocw),
                      pl.BlockSpec((LANES, keys), lambda b, s, pt: (0, 0))],
            out_specs=pl.BlockSpec((1, rows, KV_W), lambda b, s, pt: (b, 0, 0)),
            scratch_shapes=[pltpu.VMEM((2, 2, KV_W, keys), F32), pltpu.SemaphoreType.DMA((2,)),
                            pltpu.VMEM((rows, 1), F32), pltpu.VMEM((rows, 1), F32),
                            pltpu.VMEM((rows, KV_W), F32)]),
        out_shape=jax.ShapeDtypeStruct((nb, rows, KV_W), F32),
        compiler_params=_cparams(("arbitrary", "arbitrary")),
        name="nsa_sample_b",
    )(page_table, cache, q_rows, g_rows, sel_steps, sel_steps, new_sel, ocw, expand)


def _nsa_sample(page_table, cache, win_buf, cw, qpad, gates, kv4, kvw, pps=PAGES_PER_STEP):
    nb, ts, _ = qpad.shape
    past_len = page_table.shape[1] * PAGE_SIZE
    kc, vc = _compress(page_table, cache, *cw, pps=pps, transposed=True)
    rows = NSA_HEADS * ts
    q_rows = qpad.reshape(nb, ts, NSA_HEADS, LANES).transpose(0, 2, 1, 3).reshape(nb, rows, LANES)
    g_rows = gates[:, :, :3 * NSA_HEADS].reshape(nb, ts, NSA_HEADS, 3).transpose(0, 2, 1, 3)
    g_rows = jnp.pad(g_rows.reshape(nb, rows, 3), ((0, 0), (0, 0), (0, LANES - 3)))
    pad_rows = lambda a: jnp.pad(a, ((0, 0), (0, LANES - ts), (0, 0)))
    new_win = pad_rows(kvw)
    new_sel = pad_rows(kv4[:, :, 2 * KV_W:])
    ocw, sel = _nsa_sample_a(q_rows, g_rows, kc, vc, win_buf, new_win, past_len=past_len, n_tok=ts)
    n_steps = page_table.shape[1] // pps
    blk_per_step = pps * PAGE_SIZE // SEL_BLOCK
    n_past_blk = n_steps * blk_per_step
    sel_past = sel[:, :, :n_past_blk].reshape(nb, NSA_KV_HEADS, 1, ts, n_steps, blk_per_step)
    sel_past = jnp.broadcast_to(sel_past, (nb, NSA_KV_HEADS, NSA_GROUP, ts, n_steps, blk_per_step))
    sel_past = sel_past.transpose(0, 4, 1, 2, 3, 5).reshape(nb, n_steps, rows, blk_per_step)
    sel_past = jnp.pad(sel_past, ((0, 0), (0, 0), (0, 0), (0, LANES - blk_per_step)))
    sel_new = jnp.pad(sel[:, :, n_past_blk:], ((0, 0), (0, 0), (0, LANES)))[:, :, :LANES]
    sel_new = sel_new.reshape(nb, NSA_KV_HEADS, 1, ts, LANES)
    sel_new = jnp.broadcast_to(sel_new, (nb, NSA_KV_HEADS, NSA_GROUP, ts, LANES)).reshape(nb, 1, rows, LANES)
    sel_steps = jnp.concatenate([sel_past, sel_new], axis=1).astype(BF16)
    o_rows = _nsa_sample_b(page_table, cache, q_rows, g_rows, sel_steps, new_sel, ocw, n_tok=ts, pps=pps)
    return o_rows.reshape(nb, NSA_HEADS, ts, LANES).transpose(0, 2, 1, 3).reshape(nb, ts, QPAD_W).astype(BF16)


def _outproj_kernel(x_ref, hg_ref, nsa_ref, g1_ref, sc2_ref, sh2_ref, fn_ref, wo1_ref, wo2_ref,
                    x1_ref, h2_ref, *, tm):
    mix = _dot(hg_ref[0], wo1_ref[...]) + _dot(nsa_ref[0], wo2_ref[...])
    x1 = x_ref[0] + g1_ref[0] * mix
    x1_ref[0] = x1
    h2 = _rms(x1, fn_ref[...]) * (1.0 + sc2_ref[0]) + sh2_ref[0]
    _store_tok_tiles(h2_ref, h2, tm)


def _outproj(x, hg_out, nsa, gate1, scale2, shift2, ffn_norm, wo_hg, wo_nsa, tm):
    nb, t, _ = x.shape
    nt = t // tm
    mod_spec = _mod_spec(gate1, tm)
    tile = lambda w: pl.BlockSpec((1, tm, w), lambda b, i: (b, i, 0))
    return pl.pallas_call(
        functools.partial(_outproj_kernel, tm=tm),
        grid=(nb, nt),
        in_specs=[tile(D_MODEL), tile(HG_WIDTH), tile(QPAD_W), mod_spec, mod_spec, mod_spec,
                  pl.BlockSpec((1, D_MODEL), lambda b, i: (0, 0)),
                  pl.BlockSpec((HG_WIDTH, D_MODEL), lambda b, i: (0, 0)),
                  pl.BlockSpec((QPAD_W, D_MODEL), lambda b, i: (0, 0))],
        out_specs=[tile(D_MODEL), pl.BlockSpec((tm * TOK_ROWS, LANES), lambda b, i: (b * nt + i, 0))],
        out_shape=[jax.ShapeDtypeStruct((nb, t, D_MODEL), F32),
                   jax.ShapeDtypeStruct((nb * t * TOK_ROWS, LANES), F32)],
        compiler_params=_cparams(("arbitrary", "arbitrary")),
        name="out_proj",
    )(x, hg_out, nsa, gate1, scale2, shift2, ffn_norm.reshape(1, -1), wo_hg, wo_nsa)


def _split_w_out(w_out):
    wo_hg = w_out[:HG_WIDTH].astype(BF16)
    wn = w_out[HG_WIDTH:].reshape(NSA_HEADS, NSA_HEAD_DIM, D_MODEL)
    z = jnp.zeros_like(wn)
    grp = (jnp.arange(NSA_HEADS) // NSA_GROUP)[:, None, None]
    wn_pad = jnp.where(grp == 0, jnp.concatenate([wn, z], axis=1), jnp.concatenate([z, wn], axis=1))
    return wo_hg, wn_pad.reshape(QPAD_W, D_MODEL).astype(BF16)


def _router_kernel(h_ref, wr_ref, b_ref, e_ref, w_ref, r_ref, cnt_ref, run_scr, *, tm):
    @pl.when(pl.program_id(0) == 0)
    def _():
        run_scr[...] = jnp.zeros(run_scr.shape, F32)

    x = _load_tok_tiles(h_ref, tm).astype(BF16)
    scores = jax.nn.sigmoid(_dot(x, wr_ref[...]))
    biased = scores + b_ref[...]
    lane_i = lax.broadcasted_iota(jnp.int32, (tm, N_EXPERTS), 1)
    lane = lane_i.astype(F32)
    grp_of_lane = lax.shift_right_logical(lane_i, 5)
    per_group = N_EXPERTS // N_GROUPS

    gcol = lax.broadcasted_iota(jnp.int32, (tm, LANES), 1)
    gs = jnp.full((tm, LANES), 2.0 * NEG, F32)
    for g in range(N_GROUPS):
        mg = jnp.where(grp_of_lane == g, biased, NEG)
        m1 = jnp.max(mg, axis=1, keepdims=True)
        i1 = jnp.min(jnp.where(mg == m1, lane, 1e9), axis=1, keepdims=True)
        m2 = jnp.max(jnp.where(lane == i1, NEG, mg), axis=1, keepdims=True)
        gs = jnp.where(gcol == g, m1 + m2, gs)
    gsel = _topk_mask(gs, TOPK_GROUPS).astype(BF16)
    spread = (lax.broadcasted_iota(jnp.int32, (LANES, N_EXPERTS), 0)
              == lax.shift_right_logical(lax.broadcasted_iota(jnp.int32, (LANES, N_EXPERTS), 1), 5)).astype(BF16)
    v = jnp.where(_dot(gsel, spread) > 0.5, biased, NEG)

    onehot = jnp.zeros((tm, N_EXPERTS), F32)
    idxs, wts = [], []
    wsum = jnp.zeros((tm, 1), F32)
    for _ in range(TOP_K):
        m = jnp.max(v, axis=1, keepdims=True)
        idx = jnp.min(jnp.where(v == m, lane, 1e9), axis=1, keepdims=True)
        pick = lane == idx
        wk = jnp.sum(jnp.where(pick, scores, 0.0), axis=1, keepdims=True)
        onehot = jnp.where(pick, 1.0, onehot)
        v = jnp.where(pick, 3.0 * NEG, v)
        idxs.append(idx)
        wts.append(wk)
        wsum = wsum + wk

    earlier = (lax.broadcasted_iota(jnp.int32, (tm, tm), 0) > lax.broadcasted_iota(jnp.int32, (tm, tm), 1))
    before = _dot(earlier.astype(BF16), onehot.astype(BF16)) + run_scr[...]
    e_out = jnp.zeros((tm, LANES), jnp.int32)
    r_out = jnp.zeros((tm, LANES), jnp.int32)
    w_out = jnp.zeros((tm, LANES), F32)
    for k in range(TOP_K):
        rk = jnp.sum(jnp.where(lane == idxs[k], before, 0.0), axis=1, keepdims=True)
        e_out = jnp.where(gcol == k, idxs[k].astype(jnp.int32), e_out)
        r_out = jnp.where(gcol == k, rk.astype(jnp.int32), r_out)
        w_out = jnp.where(gcol == k, wts[k] / wsum * ROUTED_SCALE, w_out)
    e_ref[...] = e_out
    r_ref[...] = r_out
    w_ref[...] = w_out
    run_scr[...] = run_scr[...] + jnp.sum(onehot, axis=0, keepdims=True)
    cnt_ref[...] = run_scr[...]


def _router(h2, w_router, bias, n_tok, tm=ROUTER_TM):
    tile = pl.BlockSpec((tm, LANES), lambda i: (i, 0))
    return pl.pallas_call(
        functools.partial(_router_kernel, tm=tm),
        grid=(n_tok // tm,),
        in_specs=[pl.BlockSpec((tm * TOK_ROWS, LANES), lambda i: (i, 0)),
                  pl.BlockSpec((D_MODEL, N_EXPERTS), lambda i: (0, 0)),
                  pl.BlockSpec((1, N_EXPERTS), lambda i: (0, 0))],
        out_specs=[tile, tile, tile, pl.BlockSpec((1, N_EXPERTS), lambda i: (0, 0))],
        out_shape=[jax.ShapeDtypeStruct((n_tok, LANES), jnp.int32),
                   jax.ShapeDtypeStruct((n_tok, LANES), F32),
                   jax.ShapeDtypeStruct((n_tok, LANES), jnp.int32),
                   jax.ShapeDtypeStruct((1, N_EXPERTS), F32)],
        scratch_shapes=[pltpu.VMEM((1, N_EXPERTS), F32)],
        compiler_params=_cparams(("arbitrary",)),
        name="moe_router",
    )(h2, w_router, bias)


def _dest_kernel(e_ref, r_ref, st_ref, d_ref):
    e = e_ref[...]
    tm = e.shape[0]
    lane = lax.broadcasted_iota(jnp.int32, (tm, N_EXPERTS), 1)
    col = lax.broadcasted_iota(jnp.int32, (tm, LANES), 1)
    st = st_ref[...]
    out = r_ref[...]
    for k in range(TOP_K):
        sk = jnp.sum(jnp.where(lane == e[:, k:k + 1], st, 0.0), axis=1, keepdims=True)
        out = jnp.where(col == k, out + sk.astype(jnp.int32), out)
    d_ref[...] = out


def _moe_dest(top_e, rank, starts, tm=ROUTER_TM):
    n_tok = top_e.shape[0]
    tile = pl.BlockSpec((tm, LANES), lambda i: (i, 0))
    return pl.pallas_call(
        _dest_kernel,
        grid=(n_tok // tm,),
        in_specs=[tile, tile, pl.BlockSpec((1, N_EXPERTS), lambda i: (0, 0))],
        out_specs=tile,
        out_shape=jax.ShapeDtypeStruct((n_tok, LANES), jnp.int32),
        compiler_params=_cparams(("arbitrary",)),
        name="moe_dest",
    )(top_e, rank, starts.astype(F32).reshape(1, -1))


def _moe_items(counts, starts, ends, n_blocks):
    first_blk = starts // MOE_BM
    last_blk = (ends - 1) // MOE_BM
    n_it = jnp.where(counts > 0, last_blk - first_blk + 1, 0)
    it_end = jnp.cumsum(n_it)
    it_start = it_end - n_it
    total = it_end[-1]
    ii = jnp.arange(n_blocks + N_EXPERTS - 1, dtype=jnp.int32)
    valid = ii < total
    e_of = jnp.searchsorted(it_end, jnp.minimum(ii, total - 1), side="right").astype(jnp.int32)
    e_of = jnp.minimum(e_of, N_EXPERTS - 1)
    blk_of = jnp.where(valid, first_blk[e_of] + ii - it_start[e_of], n_blocks - 1).astype(jnp.int32)
    shifted = lambda a: jnp.concatenate([jnp.full((1,), -1, jnp.int32), a[:-1]])
    first = (valid & (blk_of != shifted(blk_of))).astype(jnp.int32)
    fresh = (valid & (e_of != shifted(e_of))).astype(jnp.int32)
    return blk_of, e_of, first, fresh, valid.astype(jnp.int32)


def _dispatch_kernel(dest_ref, h_ref, xs_ref, sem, *, tm):
    n_pairs = tm * TOP_K

    def row_copy(src_tok, dst_row):
        return pltpu.make_async_copy(
            h_ref.at[pl.ds(pl.multiple_of(src_tok * TOK_ROWS, TOK_ROWS), TOK_ROWS), :],
            xs_ref.at[pl.ds(pl.multiple_of(dst_row * TOK_ROWS, TOK_ROWS), TOK_ROWS), :], sem)

    def issue(i, carry):
        for k in range(TOP_K):
            row_copy(i, dest_ref[0, 0, i * TOP_K + k]).start(priority=k % 2)
        return carry

    def drain(i, carry):
        for _ in range(TOP_K):
            row_copy(0, 0).wait()
        return carry

    lax.fori_loop(0, tm, issue, 0)
    lax.fori_loop(0, tm, drain, 0)


def _dispatch(dest_tiles, h2, n_pairs, tm=ROUTER_TM):
    n_tiles = dest_tiles.shape[0]
    return pl.pallas_call(
        functools.partial(_dispatch_kernel, tm=tm),
        grid=(n_tiles,),
        in_specs=[pl.BlockSpec((1, 1, tm * TOP_K), lambda i: (i, 0, 0), memory_space=pltpu.SMEM),
                  pl.BlockSpec((tm * TOK_ROWS, LANES), lambda i: (i, 0))],
        out_specs=pl.BlockSpec(memory_space=pl.ANY),
        out_shape=jax.ShapeDtypeStruct((n_pairs * TOK_ROWS, LANES), F32),
        scratch_shapes=[pltpu.SemaphoreType.DMA(())],
        compiler_params=_cparams(("arbitrary",)),
        name="moe_dispatch",
    )(dest_tiles, h2)


def _gmm_kernel(blk_ref, e_ref, first_ref, fresh_ref, valid_ref, st_ref, en_ref,
                xs_ref, wg_ref, wu_ref, wd_ref, ys_ref, wg_bf, wu_bf, wd_bf, *, bm):
    i = pl.program_id(0)

    @pl.when(fresh_ref[i] == 1)
    def _():
        wg_bf[...] = wg_ref[0].astype(BF16)
        wu_bf[...] = wu_ref[0].astype(BF16)
        wd_bf[...] = wd_ref[0].astype(BF16)

    @pl.when(valid_ref[i] == 1)
    def _():
        x = _load_tok_tiles(xs_ref, bm).astype(BF16)
        hid = (_silu(_dot(x, wg_bf[...])) * _dot(x, wu_bf[...])).astype(BF16)
        y = _dot(hid, wd_bf[...])
        e = e_ref[i]
        row = blk_ref[i] * bm + lax.broadcasted_iota(jnp.int32, (bm, 1), 0)
        mine = (row >= st_ref[e]) & (row < en_ref[e])

        @pl.when(first_ref[i] == 1)
        def _():
            _store_tok_tiles(ys_ref, jnp.where(mine, y, 0.0), bm)

        @pl.when(first_ref[i] == 0)
        def _():
            _store_tok_tiles(ys_ref, jnp.where(mine, y, _load_tok_tiles(ys_ref, bm)), bm)


def _moe_gmm(items, starts, ends, xs_sorted, w_gate, w_up, w_down, n_blocks, bm=MOE_BM):
    blk_of, e_of, first, fresh, valid = items
    rows = pl.BlockSpec((bm * TOK_ROWS, LANES), lambda i, blk, e, *_: (blk[i], 0))
    wspec = lambda a: pl.BlockSpec((1,) + a.shape[1:], lambda i, blk, e, *_: (e[i], 0, 0))
    return pl.pallas_call(
        functools.partial(_gmm_kernel, bm=bm),
        grid_spec=pltpu.PrefetchScalarGridSpec(
            num_scalar_prefetch=7,
            grid=(blk_of.shape[0],),
            in_specs=[rows, wspec(w_gate), wspec(w_up), wspec(w_down)],
            out_specs=rows,
            scratch_shapes=[pltpu.VMEM(w_gate.shape[1:], BF16), pltpu.VMEM(w_up.shape[1:], BF16),
                            pltpu.VMEM(w_down.shape[1:], BF16)]),
        out_shape=jax.ShapeDtypeStruct(xs_sorted.shape, F32),
        compiler_params=_cparams(("arbitrary",)),
        name="moe_experts",
    )(blk_of, e_of, first, fresh, valid, starts.astype(jnp.int32), ends.astype(jnp.int32),
      xs_sorted, w_gate, w_up, w_down)


def _combine_kernel(dest_ref, dnext_ref, w_ref, x1_ref, h_ref, g2_ref, wsg_ref, wsu_ref, wsd_ref, fn_ref, ys_ref,
                    o_ref, gbuf, sem, routed_scr, *, tm):
    n = pl.program_id(0) * pl.num_programs(1) + pl.program_id(1)
    total = pl.num_programs(0) * pl.num_programs(1)
    slot = n % 2

    def row_copy(src_row, p, sl):
        return pltpu.make_async_copy(
            ys_ref.at[pl.ds(pl.multiple_of(src_row * TOK_ROWS, TOK_ROWS), TOK_ROWS), :],
            gbuf.at[sl, pl.ds(pl.multiple_of(p * TOK_ROWS, TOK_ROWS), TOK_ROWS), :], sem.at[sl])

    def gather(rows_ref, sl):
        def issue(i, carry):
            for k in range(TOP_K):
                p = i * TOP_K + k
                row_copy(rows_ref[0, 0, p], p, sl).start(priority=k % 2)
            return carry
        lax.fori_loop(0, tm, issue, 0)

    def drain(i, carry):
        for _ in range(TOP_K):
            row_copy(0, 0, slot).wait()
        return carry

    @pl.when(n == 0)
    def _():
        gather(dest_ref, slot)

    @pl.when(n + 1 < total)
    def _():
        gather(dnext_ref, 1 - slot)

    h = _load_tok_tiles(h_ref, tm).astype(BF16)
    hid = (_silu(_dot(h, wsg_ref[...])) * _dot(h, wsu_ref[...])).astype(BF16)
    shared = _dot(hid, wsd_ref[...])
    lax.fori_loop(0, tm, drain, 0)

    def weigh(t, carry):
        acc = jnp.zeros((TOK_ROWS, LANES), F32)
        for k in range(TOP_K):
            p = t * TOP_K + k
            acc = acc + w_ref[0, 0, p] * gbuf[slot, pl.ds(pl.multiple_of(p * TOK_ROWS, TOK_ROWS), TOK_ROWS), :]
        routed_scr[pl.ds(pl.multiple_of(t * TOK_ROWS, TOK_ROWS), TOK_ROWS), :] = acc
        return carry

    lax.fori_loop(0, tm, weigh, 0)
    x2 = x1_ref[0] + g2_ref[0] * (_load_tok_tiles(routed_scr, tm) + shared)
    o_ref[0] = _rms(x2, fn_ref[...])


def _combine(dest_tiles, w_tiles, x1, h2, gate2, shared, fnorm, ys_sorted, tile0, tm=COMBINE_TM):
    nb, t, _ = x1.shape
    nt = t // tm
    flat = lambda b, i: tile0 + b * nt + i
    nxt = lambda b, i: tile0 + jnp.minimum(b * nt + i + 1, nb * nt - 1)
    mod_spec = _mod_spec(gate2, tm)
    const = lambda a: pl.BlockSpec(a.shape, lambda b, i: (0, 0))
    return pl.pallas_call(
        functools.partial(_combine_kernel, tm=tm),
        grid=(nb, nt),
        in_specs=[pl.BlockSpec((1, 1, tm * TOP_K), lambda b, i: (flat(b, i), 0, 0), memory_space=pltpu.SMEM),
                  pl.BlockSpec((1, 1, tm * TOP_K), lambda b, i: (nxt(b, i), 0, 0), memory_space=pltpu.SMEM),
                  pl.BlockSpec((1, 1, tm * TOP_K), lambda b, i: (flat(b, i), 0, 0), memory_space=pltpu.SMEM),
                  pl.BlockSpec((1, tm, D_MODEL), lambda b, i: (b, i, 0)),
                  pl.BlockSpec((tm * TOK_ROWS, LANES), lambda b, i: (flat(b, i), 0)),
                  mod_spec, const(shared[0]), const(shared[1]), const(shared[2]), const(fnorm),
                  pl.BlockSpec(memory_space=pl.ANY)],
        out_specs=pl.BlockSpec((1, tm, D_MODEL), lambda b, i: (b, i, 0)),
        out_shape=jax.ShapeDtypeStruct((nb, t, D_MODEL), F32),
        scratch_shapes=[pltpu.VMEM((2, tm * TOP_K * TOK_ROWS, LANES), F32), pltpu.SemaphoreType.DMA((2,)),
                        pltpu.VMEM((tm * TOK_ROWS, LANES), F32)],
        compiler_params=_cparams(("arbitrary", "arbitrary")),
        name="moe_combine",
    )(dest_tiles, dest_tiles, w_tiles, x1, h2, gate2, *shared, fnorm, ys_sorted)


def kernel(x_prompt, x_sample, c_prompt, c_sample, cache_nsa_kv, cache_win_kv, state_hgrn, page_table,
           attn_norm, ffn_norm, final_norm, hg_norm, w_ada, b_ada, w_in, hg_lb,
           cmp_pe, cmp_w1, cmp_b1, cmp_w2, w_out, w_router, router_bias,
           w_gate, w_up, w_down, ws_gate, ws_up, ws_down):
    nbp, t, _ = x_prompt.shape
    nbs, ts, _ = x_sample.shape
    ns = nbs * ts
    n_all = nbp * t + ns
    past_len = page_table.shape[1] * PAGE_SIZE

    c_all = jnp.concatenate([c_prompt, c_sample], axis=0)
    c_all = jnp.pad(c_all, ((0, -c_all.shape[0] % SUBLANES), (0, 0)))
    mod = _ada(c_all, w_ada[0], b_ada[0])
    modp = mod[:nbp].reshape(nbp, 1, 6, D_MODEL)
    mods = jnp.repeat(mod[nbp:nbp + nbs].reshape(nbs, 1, 6, D_MODEL), ts, axis=1).reshape(1, ns, 6, D_MODEL)

    w_pad = _pad_w_in(w_in[0])
    cw = _compress_weights(cmp_pe[0], cmp_w1[0], cmp_b1[0], cmp_w2[0])
    wo_hg, wo_nsa = _split_w_out(w_out[0])

    hg, qpad, kv4, kvw, gates, kvsel, kvwb = _inproj(
        x_prompt, modp[:, :, 1], modp[:, :, 0], attn_norm[0], w_pad, 512)
    hg_out, hg_state_p = _hgrn(hg, hg_lb, jnp.zeros((nbp, HG_HEADS, HG_DK, HG_DK), F32), hg_norm[0],
                               256, HG_CHUNK)
    n_pages_p = t // PAGE_SIZE
    ptp = jnp.arange(nbp * n_pages_p, dtype=jnp.int32).reshape(nbp, n_pages_p)
    kc, vc = _compress(ptp, kv4.reshape(nbp * n_pages_p, PAGE_SIZE, 4 * KV_W), *cw)
    nsa = _nsa_prompt(qpad, gates, kc, vc, kvsel, kvwb)
    x1p, h2p = _outproj(x_prompt, hg_out, nsa, modp[:, :, 2], modp[:, :, 4], modp[:, :, 3],
                        ffn_norm[0], wo_hg, wo_nsa, 512)

    xs = x_sample.reshape(1, ns, D_MODEL)
    hg_s, qpad_s, kv4_s, kvw_s, gates_s, _, _ = _inproj(
        xs, mods[:, :, 1], mods[:, :, 0], attn_norm[0], w_pad, ns)
    hg_out_s, hg_state_s = _hgrn(hg_s.reshape(nbs, ts, 4 * HG_WIDTH), hg_lb, state_hgrn[0], hg_norm[0], ts, ts)
    cache = cache_nsa_kv[0].transpose(0, 2, 3, 4, 1).reshape(-1, 4 * KV_W, PAGE_SIZE)
    win_buf = cache_win_kv[0].reshape(nbs, -1, 2 * KV_W)
    nsa_s = _nsa_sample(page_table, cache, win_buf, cw, qpad_s.reshape(nbs, ts, QPAD_W),
                        gates_s.reshape(nbs, ts, GZ_PAD), kv4_s.reshape(nbs, ts, 4 * KV_W),
                        kvw_s.reshape(nbs, ts, 2 * KV_W)).reshape(1, ns, QPAD_W)
    x1s, h2s = _outproj(xs, hg_out_s.reshape(1, ns, HG_WIDTH), nsa_s, mods[:, :, 2], mods[:, :, 4], mods[:, :, 3],
                        ffn_norm[0], wo_hg, wo_nsa, ns)

    h2 = jnp.concatenate([h2p, h2s], axis=0)
    top_e, top_w, rank, counts = _router(h2, w_router[0].astype(BF16), router_bias[0].reshape(1, -1), n_all)
    counts = counts[0].astype(jnp.int32)
    ends = jnp.cumsum(counts)
    starts = ends - counts
    dest = _moe_dest(top_e, rank, starts)[:, :TOP_K].reshape(-1)
    n_pairs = n_all * TOP_K
    n_blocks = n_pairs // MOE_BM
    items = _moe_items(counts, starts, ends, n_blocks)
    xs_sorted = _dispatch(dest.reshape(n_all // ROUTER_TM, 1, ROUTER_TM * TOP_K), h2, n_pairs)
    ys_sorted = _moe_gmm(items, starts, ends, xs_sorted, w_gate[0], w_up[0], w_down[0], n_blocks)
    dest_c = dest.reshape(n_all // COMBINE_TM, 1, COMBINE_TM * TOP_K)
    w_c = top_w[:, :TOP_K].reshape(n_all // COMBINE_TM, 1, COMBINE_TM * TOP_K)
    shared = (ws_gate[0].astype(BF16), ws_up[0].astype(BF16), ws_down[0].astype(BF16))
    fnorm = final_norm.reshape(1, -1)
    y_prompt = _combine(dest_c, w_c, x1p, h2, modp[:, :, 5], shared, fnorm, ys_sorted, 0)
    y_sample = _combine(dest_c, w_c, x1s, h2, mods[:, :, 5], shared, fnorm, ys_sorted, nbp * t // COMBINE_TM)

    wb = win_buf.shape[1]
    win_p = kvw[:, t - min(WINDOW, t):]
    win_s = jnp.concatenate([win_buf, kvw_s.reshape(nbs, ts, 2 * KV_W)], axis=1)[:, -wb:]
    kv_shape = (4, NSA_KV_HEADS, NSA_HEAD_DIM)
    win_shape = (2, NSA_KV_HEADS, NSA_HEAD_DIM)
    return (y_prompt,
            y_sample.reshape(nbs, ts, D_MODEL),
            kv4.reshape(1, nbp, t, *kv_shape),
            win_p.reshape(1, nbp, -1, *win_shape),
            hg_state_p[None],
            kv4_s.reshape(1, nbs, ts, *kv_shape),
            win_s.reshape(1, nbs, wb, *win_shape),
            hg_state_s[None])
```

```python
import functools

import jax
import jax.numpy as jnp
import numpy as np
from jax import lax
from jax.experimental import pallas as pl
from jax.experimental.pallas import tpu as pltpu

F32 = jnp.float32
BF16 = jnp.bfloat16

D_MODEL = 1024
HG_WIDTH = 512
HG_HEADS = 4
HG_DK = 128
HG_CHUNK = 32
NSA_WIDTH = 512
NSA_HEADS = 8
NSA_HEAD_DIM = 64
NSA_KV_HEADS = 2
NSA_GROUP = 4
KV_W = 128
CMP_BLOCK = 32
CMP_STRIDE = 16
CMP_HIDDEN = 256
SEL_BLOCK = 64
SEL_TOPN = 16
WINDOW = 512
Q_BLOCK = 128
N_EXPERTS = 256
TOP_K = 8
N_GROUPS = 8
TOPK_GROUPS = 4
MOE_D_FF = 256
ROUTED_SCALE = 2.5
RMS_EPS = 1e-6
PAGE_SIZE = 128
IN_COLS = 4 * HG_WIDTH + NSA_WIDTH + 4 * KV_W + 2 * KV_W + 3 * NSA_HEADS

LANES = 128
SUBLANES = 8
TOK_ROWS = D_MODEL // LANES
VMEM_LIMIT = 56 * 1024 * 1024

QPAD_W = NSA_HEADS * LANES
GZ_PAD = LANES
INP_COLS = 4 * HG_WIDTH + QPAD_W + 4 * KV_W + 2 * KV_W + GZ_PAD

NEG = -1e30
PAGES_PER_STEP = 32
MOE_BM = 256
ROUTER_TM = 256
COMBINE_TM = 128


def _cparams(sem):
    return pltpu.CompilerParams(dimension_semantics=sem, vmem_limit_bytes=VMEM_LIMIT)


def _dot(a, b):
    return jnp.dot(a, b, preferred_element_type=F32)


def _dot_nt(a, b):
    return lax.dot_general(a, b, (((1,), (1,)), ((), ())), preferred_element_type=F32)


def _dot_tn(a, b):
    return lax.dot_general(a, b, (((0,), (0,)), ((), ())), preferred_element_type=F32)


def _rms(x, g):
    return x * lax.rsqrt(jnp.mean(x * x, axis=-1, keepdims=True) + RMS_EPS) * g


def _silu(x):
    return x * jax.nn.sigmoid(x)


Q_SCALE = NSA_HEAD_DIM ** -0.5 * 1.4426950408889634


def _masked_softmax(s, valid):
    s = jnp.where(valid, s, NEG)
    m = jnp.max(s, axis=1, keepdims=True)
    p = jnp.exp2(s - m) * valid.astype(F32)
    return p / jnp.maximum(jnp.sum(p, axis=1, keepdims=True), 1e-30)


def _topk_mask(v, k):
    lane = lax.broadcasted_iota(jnp.int32, v.shape, 1).astype(F32)
    sel = jnp.zeros(v.shape, F32)
    for _ in range(k):
        m = jnp.max(v, axis=1, keepdims=True)
        idx = jnp.min(jnp.where(v == m, lane, 1e9), axis=1, keepdims=True)
        pick = lane == idx
        sel = jnp.where(pick, 1.0, sel)
        v = jnp.where(pick, 3.0 * NEG, v)
    return sel


def _mod_spec(mod, tm):
    if mod.shape[1] == 1:
        return pl.BlockSpec((1, 1, D_MODEL), lambda b, i: (b, 0, 0))
    return pl.BlockSpec((1, tm, D_MODEL), lambda b, i: (b, i, 0))


def _load_tok_tiles(ref, n_tok):
    return jnp.concatenate([ref[pl.ds(s, n_tok, stride=TOK_ROWS), :] for s in range(TOK_ROWS)], axis=1)


def _store_tok_tiles(ref, val, n_tok):
    for s in range(TOK_ROWS):
        ref[pl.ds(s, n_tok, stride=TOK_ROWS), :] = val[:, s * LANES:(s + 1) * LANES]


def _ada_kernel(c_ref, w_ref, b_ref, o_ref):
    s = _silu(c_ref[...]).astype(BF16)
    o_ref[...] = _dot(s, w_ref[...].astype(BF16)) + b_ref[...]


def _ada(c_all, w_ada, b_ada):
    n = c_all.shape[0]
    return pl.pallas_call(
        _ada_kernel,
        grid=(6,),
        in_specs=[pl.BlockSpec((n, D_MODEL), lambda j: (0, 0)),
                  pl.BlockSpec((D_MODEL, D_MODEL), lambda j: (0, j)),
                  pl.BlockSpec((1, D_MODEL), lambda j: (0, j))],
        out_specs=pl.BlockSpec((n, D_MODEL), lambda j: (0, j)),
        out_shape=jax.ShapeDtypeStruct((n, 6 * D_MODEL), F32),
        compiler_params=_cparams(("arbitrary",)),
        name="ada_mod",
    )(c_all, w_ada, b_ada.reshape(1, -1))


def _inproj_kernel(x_ref, sc_ref, sh_ref, g_ref, w_ref,
                   hg_ref, q_ref, kv4_ref, kvw_ref, gate_ref, kvsel_ref, kvwb_ref):
    h = _rms(x_ref[0], g_ref[...]) * (1.0 + sc_ref[0]) + sh_ref[0]
    z = _dot(h.astype(BF16), w_ref[...])
    c0 = 4 * HG_WIDTH
    hg_ref[0] = z[:, :c0]
    q_ref[0] = (z[:, c0:c0 + QPAD_W] * Q_SCALE).astype(BF16)
    c1 = c0 + QPAD_W
    kv4 = z[:, c1:c1 + 4 * KV_W]
    kv4_ref[0] = kv4
    kvsel_ref[0] = kv4[:, 2 * KV_W:].astype(BF16)
    c2 = c1 + 4 * KV_W
    kvw = z[:, c2:c2 + 2 * KV_W]
    kvw_ref[0] = kvw
    kvwb_ref[0] = kvw.astype(BF16)
    gate_ref[0] = jax.nn.sigmoid(z[:, c2 + 2 * KV_W:])


def _inproj(x, scale, shift, g_norm, w_pad, tm):
    nb, t, _ = x.shape
    mod_spec = _mod_spec(scale, tm)
    widths = [(4 * HG_WIDTH, F32), (QPAD_W, BF16), (4 * KV_W, F32), (2 * KV_W, F32), (GZ_PAD, F32),
              (2 * KV_W, BF16), (2 * KV_W, BF16)]
    return pl.pallas_call(
        _inproj_kernel,
        grid=(nb, t // tm),
        in_specs=[pl.BlockSpec((1, tm, D_MODEL), lambda b, i: (b, i, 0)),
                  mod_spec, mod_spec,
                  pl.BlockSpec((1, D_MODEL), lambda b, i: (0, 0)),
                  pl.BlockSpec((D_MODEL, INP_COLS), lambda b, i: (0, 0))],
        out_specs=[pl.BlockSpec((1, tm, w), lambda b, i: (b, i, 0)) for w, _ in widths],
        out_shape=[jax.ShapeDtypeStruct((nb, t, w), dt) for w, dt in widths],
        compiler_params=_cparams(("arbitrary", "arbitrary")),
        name="in_proj",
    )(x, scale, shift, g_norm.reshape(1, -1), w_pad)


def _pad_w_in(w_in):
    c0 = 4 * HG_WIDTH
    wq = w_in[:, c0:c0 + NSA_WIDTH].reshape(D_MODEL, NSA_HEADS, NSA_HEAD_DIM)
    zeros = jnp.zeros_like(wq)
    lo = jnp.concatenate([wq, zeros], axis=-1)
    hi = jnp.concatenate([zeros, wq], axis=-1)
    grp = (jnp.arange(NSA_HEADS) // NSA_GROUP)[None, :, None]
    wq_pad = jnp.where(grp == 0, lo, hi).reshape(D_MODEL, QPAD_W)
    c1 = c0 + NSA_WIDTH
    rest = w_in[:, c1:c1 + 6 * KV_W]
    gz = jnp.pad(w_in[:, c1 + 6 * KV_W:], ((0, 0), (0, GZ_PAD - 3 * NSA_HEADS)))
    return jnp.concatenate([w_in[:, :c0], wq_pad, rest, gz], axis=1).astype(BF16)


def _hgrn_kernel(q_ref, f_ref, v_ref, gt_ref, lb_ref, s0_ref, gn_ref, o_ref, s_out_ref, st_scr,
                 *, chunk, n_chunks):
    i = pl.program_id(1)

    @pl.when(i == 0)
    def _():
        for h in range(HG_HEADS):
            st_scr[h] = s0_ref[0, h].T

    lbr = lb_ref[...]
    e = jnp.exp(lbr - jnp.max(lbr, axis=0, keepdims=True))
    lb_all = e[0:1] / jnp.sum(e, axis=0, keepdims=True)
    row = lax.broadcasted_iota(jnp.int32, (chunk, HG_DK), 0)
    causal = (lax.broadcasted_iota(jnp.int32, (chunk, chunk), 0)
              >= lax.broadcasted_iota(jnp.int32, (chunk, chunk), 1))
    st = [st_scr[h] for h in range(HG_HEADS)]
    for c in range(n_chunks):
        sl = pl.ds(c * chunk, chunk)
        for h in range(HG_HEADS):
            hs = slice(h * HG_DK, (h + 1) * HG_DK)
            lb = lb_all[:, hs]
            z = f_ref[0, sl, hs]
            log_f = jnp.log(lb + (1.0 - lb) * jax.nn.sigmoid(z))
            kk = (1.0 - lb) * jax.nn.sigmoid(-z)
            a = log_f
            s = 1
            while s < chunk:
                a = a + jnp.where(row >= s, pltpu.roll(a, s, 0), 0.0)
                s *= 2
            qt = (q_ref[0, sl, hs] * jnp.exp(a)).astype(BF16)
            kt = (kk * jnp.exp(-a)).astype(BF16)
            v = v_ref[0, sl, hs].astype(BF16)
            att = jnp.where(causal, _dot_nt(qt, kt), 0.0)
            o = _dot(att.astype(BF16), v) + _dot_nt(qt, st[h].astype(BF16))
            a_end = a[chunk - 1:chunk, :]
            kd = (kk * jnp.exp(a_end - a)).astype(BF16)
            st[h] = st[h] * jnp.exp(a_end) + _dot_tn(v, kd)
            o = _rms(o, gn_ref[...]) * _silu(gt_ref[0, sl, hs])
            o_ref[0, sl, hs] = o.astype(o_ref.dtype)
    for h in range(HG_HEADS):
        st_scr[h] = st[h]

    @pl.when(i == pl.num_programs(1) - 1)
    def _():
        for h in range(HG_HEADS):
            s_out_ref[0, h] = st[h].T


def _hgrn(hg, hg_lb, s0, g_norm, tc, chunk):
    nb, t, _ = hg.shape
    part = lambda k: pl.BlockSpec((1, tc, HG_WIDTH), lambda b, i: (b, i, k))
    st_spec = pl.BlockSpec((1, HG_HEADS, HG_DK, HG_DK), lambda b, i: (b, 0, 0, 0))
    return pl.pallas_call(
        functools.partial(_hgrn_kernel, chunk=chunk, n_chunks=tc // chunk),
        grid=(nb, t // tc),
        in_specs=[part(0), part(1), part(2), part(3),
                  pl.BlockSpec(hg_lb.shape, lambda b, i: (0, 0)),
                  st_spec,
                  pl.BlockSpec((1, HG_DK), lambda b, i: (0, 0))],
        out_specs=[pl.BlockSpec((1, tc, HG_WIDTH), lambda b, i: (b, i, 0)), st_spec],
        out_shape=[jax.ShapeDtypeStruct((nb, t, HG_WIDTH), BF16),
                   jax.ShapeDtypeStruct((nb, HG_HEADS, HG_DK, HG_DK), F32)],
        scratch_shapes=[pltpu.VMEM((HG_HEADS, HG_DK, HG_DK), F32)],
        compiler_params=_cparams(("arbitrary", "arbitrary")),
        name="hgrn2",
    )(hg, hg, hg, hg, hg_lb, s0, g_norm.reshape(1, -1))


def _gelu_tanh(x):
    return 0.5 * x * (1.0 + jnp.tanh(0.7978845608028654 * (x + 0.044715 * x * x * x)))


def _page_copies(pt_ref, cache_ref, buf, sem, b, s, slot, n_pages, pps, col0, tail, transposed):
    copies = []
    base = s * pps
    nxt = pt_ref[b, jnp.minimum(base + pps, n_pages - 1)]
    for br in range(2):
        cols = pl.ds(col0 + br * KV_W, KV_W)
        for i in range(pps):
            pg = pt_ref[b, base + i]
            if transposed:
                copies.append(pltpu.make_async_copy(
                    cache_ref.at[pg, cols, :],
                    buf.at[slot, br, :, pl.ds(i * PAGE_SIZE, PAGE_SIZE)], sem.at[slot]))
            else:
                copies.append(pltpu.make_async_copy(
                    cache_ref.at[pg, :, cols],
                    buf.at[slot, br, pl.ds(i * PAGE_SIZE, PAGE_SIZE), :], sem.at[slot]))
        if tail and transposed:
            copies.append(pltpu.make_async_copy(
                cache_ref.at[nxt, cols, :],
                buf.at[slot, br, :, pl.ds(pps * PAGE_SIZE, PAGE_SIZE)], sem.at[slot]))
        elif tail:
            copies.append(pltpu.make_async_copy(
                cache_ref.at[nxt, pl.ds(0, CMP_STRIDE), cols],
                buf.at[slot, br, pl.ds(pps * PAGE_SIZE, CMP_STRIDE), :], sem.at[slot]))
    return copies


def _stream_pages(pt_ref, cache_ref, buf, sem, n_pages, n_steps, pps, col0, tail, transposed):
    b = pl.program_id(0)
    s = pl.program_id(1)
    n = b * n_steps + s
    total = pl.num_programs(0) * n_steps
    slot = n % 2
    args = (n_pages, pps, col0, tail, transposed)

    @pl.when(n == 0)
    def _():
        for cp in _page_copies(pt_ref, cache_ref, buf, sem, b, s, slot, *args):
            cp.start()

    @pl.when(n + 1 < total)
    def _():
        n1 = n + 1
        for cp in _page_copies(pt_ref, cache_ref, buf, sem, n1 // n_steps, n1 % n_steps, 1 - slot, *args):
            cp.start()

    for cp in _page_copies(pt_ref, cache_ref, buf, sem, b, s, slot, *args):
        cp.wait()
    return slot


def _compress_kernel(pt_ref, cache_ref, pe_ref, w1_ref, b1_ref, w2_ref, kc_ref, vc_ref, buf, sem, *rowbuf,
                     n_pages, n_steps, pps, transposed):
    groups = pps * PAGE_SIZE // CMP_STRIDE
    slot = _stream_pages(pt_ref, cache_ref, buf, sem, n_pages, n_steps, pps, 0, True, transposed)

    if transposed:
        rows_ref, = rowbuf
        for br in range(2):
            for i in range(pps + 1):
                n_rows = PAGE_SIZE if i < pps else CMP_STRIDE
                page_t = buf[slot, br, :, i * PAGE_SIZE:(i + 1) * PAGE_SIZE]
                rows_ref[br, i * PAGE_SIZE:i * PAGE_SIZE + n_rows, :] = page_t.T[:n_rows]
        read_rows = lambda br, l: rows_ref[br, pl.ds(l, groups, stride=CMP_STRIDE), :]
    else:
        read_rows = lambda br, l: buf[slot, br, pl.ds(l, groups, stride=CMP_STRIDE), :]

    low_half = lax.broadcasted_iota(jnp.int32, (groups, KV_W), 1) < NSA_HEAD_DIM
    for br, out_ref in ((0, kc_ref), (1, vc_ref)):
        acc0 = jnp.zeros((groups, CMP_HIDDEN), F32)
        acc1 = jnp.zeros((groups, CMP_HIDDEN), F32)
        for j in range(CMP_BLOCK // 2):
            a = read_rows(br, 2 * j) + pe_ref[br, 2 * j:2 * j + 1, :]
            b = pltpu.roll(read_rows(br, 2 * j + 1) + pe_ref[br, 2 * j + 1:2 * j + 2, :], NSA_HEAD_DIM, 1)
            acc0 = acc0 + _dot(jnp.where(low_half, a, b).astype(BF16), w1_ref[br, j, 0])
            acc1 = acc1 + _dot(jnp.where(low_half, b, a).astype(BF16), w1_ref[br, j, 1])
        hid = _gelu_tanh(jnp.concatenate([acc0, acc1], axis=1) + b1_ref[br]).astype(BF16)
        out_ref[0] = _dot(hid, w2_ref[br]).astype(out_ref.dtype)


def _compress(page_table, cache, pe2, w1cat, b1cat, w2bd, pps=PAGES_PER_STEP, transposed=False):
    nb, n_pages = page_table.shape
    n_steps = n_pages // pps
    groups = pps * PAGE_SIZE // CMP_STRIDE
    rows = pps * PAGE_SIZE + CMP_STRIDE
    const = lambda shape: pl.BlockSpec(shape, lambda b, s, pt: (0,) * len(shape))
    out_spec = pl.BlockSpec((1, groups, KV_W), lambda b, s, pt: (b, s, 0))
    out_sds = jax.ShapeDtypeStruct((nb, n_steps * groups, KV_W), BF16)
    if transposed:
        stage = [pltpu.VMEM((2, 2, KV_W, (pps + 1) * PAGE_SIZE), F32), pltpu.SemaphoreType.DMA((2,)),
                 pltpu.VMEM((2, rows, KV_W), F32)]
    else:
        stage = [pltpu.VMEM((2, 2, rows, KV_W), F32), pltpu.SemaphoreType.DMA((2,))]
    return pl.pallas_call(
        functools.partial(_compress_kernel, n_pages=n_pages, n_steps=n_steps, pps=pps, transposed=transposed),
        grid_spec=pltpu.PrefetchScalarGridSpec(
            num_scalar_prefetch=1,
            grid=(nb, n_steps),
            in_specs=[pl.BlockSpec(memory_space=pl.ANY),
                      const((2, CMP_BLOCK, KV_W)),
                      const((2, CMP_BLOCK // 2, 2, KV_W, CMP_HIDDEN)),
                      const((2, 1, 2 * CMP_HIDDEN)),
                      const((2, 2 * CMP_HIDDEN, KV_W))],
            out_specs=[out_spec, out_spec],
            scratch_shapes=stage),
        out_shape=[out_sds, out_sds],
        compiler_params=_cparams(("arbitrary", "arbitrary")),
        name="nsa_compress",
    )(page_table, cache, pe2, w1cat, b1cat, w2bd)


def _compress_weights(cmp_pe, cmp_w1, cmp_b1, cmp_w2):
    pe2 = jnp.concatenate([cmp_pe, cmp_pe], axis=-1)
    w1 = cmp_w1.reshape(2, CMP_BLOCK // 2, 2, NSA_HEAD_DIM, CMP_HIDDEN)
    even_odd = w1.reshape(2, CMP_BLOCK // 2, 2 * NSA_HEAD_DIM, CMP_HIDDEN)
    odd_even = w1[:, :, ::-1].reshape(2, CMP_BLOCK // 2, 2 * NSA_HEAD_DIM, CMP_HIDDEN)
    w1cat = jnp.stack([even_odd, odd_even], axis=2).astype(BF16)
    b1cat = jnp.concatenate([cmp_b1, cmp_b1], axis=-1)[:, None, :]
    z2 = jnp.zeros_like(cmp_w2)
    w2bd = jnp.concatenate([jnp.concatenate([cmp_w2, z2], axis=-1),
                            jnp.concatenate([z2, cmp_w2], axis=-1)], axis=1).astype(BF16)
    return pe2, w1cat, b1cat, w2bd


def _overlap_matrix(n_cmp, n_sel):
    cs = lax.broadcasted_iota(jnp.int32, (n_cmp, n_sel), 0) * CMP_STRIDE
    ss = lax.broadcasted_iota(jnp.int32, (n_cmp, n_sel), 1) * SEL_BLOCK
    return ((cs < ss + SEL_BLOCK) & (cs + CMP_BLOCK > ss)).astype(BF16)


SEL_CHUNK = 512
SPREAD_KEYS = 1024
WIN_SPAN = WINDOW + Q_BLOCK


def _nsa_prompt_kernel(q_ref, gt_ref, kc_ref, vc_ref, ks_ref, vs_ref, kw_ref, vw_ref, ex_ref, o_ref,
                       m_scr, acc_scr, chosen_scr, oc_scr, sa_scr, sb_scr, *, n_cmp, n_sel):
    j = pl.program_id(1)
    q0 = j * Q_BLOCK
    tok = lax.broadcasted_iota(jnp.int32, (Q_BLOCK, 1), 0) + q0
    tok4 = jnp.concatenate([tok] * NSA_GROUP, axis=0)
    n_chunks = j // (SEL_CHUNK // Q_BLOCK) + 1
    key_in_span = lax.broadcasted_iota(jnp.int32, (1, SPREAD_KEYS), 1)

    def load_q(g):
        return jnp.concatenate([q_ref[0, :, (NSA_GROUP * g + jh) * LANES:(NSA_GROUP * g + jh + 1) * LANES]
                                for jh in range(NSA_GROUP)], axis=0)

    def softmax_av(s, valid, v):
        s = jnp.where(valid, s, NEG)
        tiles = [s[:, i * LANES:(i + 1) * LANES] for i in range(s.shape[1] // LANES)]
        m = jnp.max(functools.reduce(jnp.maximum, tiles), axis=1, keepdims=True)
        p = jnp.where(valid, jnp.exp2(s - m), 0.0)
        acc = _dot(p.astype(BF16), jnp.concatenate([v, jnp.ones(v.shape, BF16)], axis=1))
        inv = 1.0 / jnp.maximum(acc[:, KV_W:], 1e-30)
        return p, acc[:, :KV_W] * inv, inv

    ov = _overlap_matrix(n_cmp, n_sel)
    cend = lax.broadcasted_iota(jnp.int32, (1, n_cmp), 1) * CMP_STRIDE + (CMP_BLOCK - 1)
    ws = pl.multiple_of(jnp.maximum(q0 - WINDOW, 0), Q_BLOCK)
    wpos = ws + lax.broadcasted_iota(jnp.int32, (1, WIN_SPAN), 1)
    d = tok4 - wpos
    in_window = (d >= 0) & (d < WINDOW)
    gates = gt_ref[0]
    imps = []
    for g in range(NSA_KV_HEADS):
        q = load_q(g)
        p, o_c, inv = softmax_av(_dot_nt(q, kc_ref[0]), cend <= tok4, vc_ref[0])
        _, o_w, _ = softmax_av(_dot_nt(q, kw_ref[0, pl.ds(ws, WIN_SPAN), :]), in_window,
                               vw_ref[0, pl.ds(ws, WIN_SPAN), :])
        psum = jnp.zeros((Q_BLOCK, n_cmp), F32)
        for jh in range(NSA_GROUP):
            h = NSA_GROUP * g + jh
            r = slice(jh * Q_BLOCK, (jh + 1) * Q_BLOCK)
            psum = psum + p[r] * jnp.concatenate([inv[r]] * (n_cmp // LANES), axis=1)
            oc_scr[g, r, :] = gates[:, 3 * h:3 * h + 1] * o_c[r] + gates[:, 3 * h + 2:3 * h + 3] * o_w[r]
        imps.append(_dot(psum.astype(BF16), ov))

    blk = lax.broadcasted_iota(jnp.int32, (1, n_sel), 1)
    cur = lax.shift_right_logical(tok, 6)
    forced = (blk == 0) | (blk == cur) | (blk == cur - 1)
    free = (blk * SEL_BLOCK <= tok) & jnp.logical_not(forced)
    forced2 = jnp.concatenate([forced] * NSA_KV_HEADS, axis=0)
    free2 = jnp.concatenate([free] * NSA_KV_HEADS, axis=0)
    best = _topk_mask(jnp.where(free2, jnp.concatenate(imps, axis=0), NEG), SEL_TOPN - 3)
    sel2 = jnp.where(forced2, 1.0, best).astype(BF16)

    ones_blk = jnp.ones((SEL_CHUNK, KV_W), BF16)

    for g in range(NSA_KV_HEADS):
        sel = sel2[g * Q_BLOCK:(g + 1) * Q_BLOCK]
        m_scr[...] = jnp.full(m_scr.shape, NEG, F32)
        acc_scr[...] = jnp.zeros(acc_scr.shape, F32)

        def spread(i, carry):
            c0 = pl.multiple_of(i * SPREAD_KEYS, SPREAD_KEYS)
            keep = (_dot(sel, ex_ref[:, pl.ds(c0, SPREAD_KEYS)]) > 0.5) & (key_in_span + c0 <= tok)
            chosen_scr[:, pl.ds(c0, SPREAD_KEYS)] = jnp.where(keep, 0.0, NEG)
            return carry

        lax.fori_loop(0, (n_chunks * SEL_CHUNK + SPREAD_KEYS - 1) // SPREAD_KEYS, spread, 0)

        heads = range(NSA_GROUP)
        rows = [pl.ds(jh * Q_BLOCK, Q_BLOCK) for jh in heads]

        def stage_scores(c, buf):
            k0 = pl.multiple_of(c * SEL_CHUNK, SEL_CHUNK)
            bias = chosen_scr[:, pl.ds(k0, SEL_CHUNK)]
            kblk = ks_ref[0, pl.ds(k0, SEL_CHUNK), :]
            for jh in heads:
                h = NSA_GROUP * g + jh
                buf[rows[jh], :] = _dot_nt(q_ref[0, :, h * LANES:(h + 1) * LANES], kblk) + bias

        def accumulate(c, buf):
            k0 = pl.multiple_of(c * SEL_CHUNK, SEL_CHUNK)
            vext = jnp.concatenate([vs_ref[0, pl.ds(k0, SEL_CHUNK), :], ones_blk], axis=1)
            tiles = [[buf[r, i * LANES:(i + 1) * LANES] for i in range(SEL_CHUNK // LANES)] for r in rows]
            m_old = [m_scr[r, :] for r in rows]
            m_new = [jnp.maximum(mo, jnp.max(functools.reduce(jnp.maximum, t), axis=1, keepdims=True))
                     for mo, t in zip(m_old, tiles)]
            probs = [jnp.concatenate([jnp.exp2(x - mn) for x in t], axis=1).astype(BF16)
                     for mn, t in zip(m_new, tiles)]
            pv = [_dot(p, vext) for p in probs]
            for r, mo, mn, y in zip(rows, m_old, m_new, pv):
                alpha = jnp.exp2(mo - mn)
                acc_scr[r, :] = jnp.concatenate([alpha, alpha], axis=1) * acc_scr[r, :] + y
                m_scr[r, :] = mn

        n_pairs = (n_chunks + 1) // 2
        stage_scores(0, sa_scr)

        def body(i, carry):
            stage_scores(2 * i + 1, sb_scr)
            accumulate(2 * i, sa_scr)
            stage_scores(jnp.minimum(2 * i + 2, 2 * n_pairs - 1), sa_scr)
            accumulate(2 * i + 1, sb_scr)
            return carry

        lax.fori_loop(0, n_pairs, body, 0)
        o_s = acc_scr[:, :KV_W] / jnp.maximum(acc_scr[:, KV_W:], 1e-30)

        for jh in range(NSA_GROUP):
            h = NSA_GROUP * g + jh
            r = slice(jh * Q_BLOCK, (jh + 1) * Q_BLOCK)
            o = oc_scr[g, r, :] + gates[:, 3 * h + 1:3 * h + 2] * o_s[r]
            o_ref[0, :, h * LANES:(h + 1) * LANES] = o.astype(o_ref.dtype)


def _nsa_prompt(qpad, gates, kc, vc, kvsel, kvwb):
    nb, t, _ = qpad.shape
    assert t % SPREAD_KEYS == 0 and SPREAD_KEYS == 2 * SEL_CHUNK and t >= WIN_SPAN
    n_cmp = kc.shape[1]
    n_sel = t // SEL_BLOCK
    full = lambda w, k: pl.BlockSpec((1, t, w), lambda b, j: (b, 0, k))
    t_pad = -(-t // SPREAD_KEYS) * SPREAD_KEYS
    expand = jnp.asarray(np.arange(n_sel)[:, None] == (np.arange(t_pad)[None, :] // SEL_BLOCK), BF16)
    return pl.pallas_call(
        functools.partial(_nsa_prompt_kernel, n_cmp=n_cmp, n_sel=n_sel),
        grid=(nb, t // Q_BLOCK),
        in_specs=[pl.BlockSpec((1, Q_BLOCK, QPAD_W), lambda b, j: (b, j, 0)),
                  pl.BlockSpec((1, Q_BLOCK, GZ_PAD), lambda b, j: (b, j, 0)),
                  pl.BlockSpec((1, n_cmp, KV_W), lambda b, j: (b, 0, 0)),
                  pl.BlockSpec((1, n_cmp, KV_W), lambda b, j: (b, 0, 0)),
                  full(KV_W, 0), full(KV_W, 1), full(KV_W, 0), full(KV_W, 1),
                  pl.BlockSpec((n_sel, t_pad), lambda b, j: (0, 0))],
        out_specs=pl.BlockSpec((1, Q_BLOCK, QPAD_W), lambda b, j: (b, j, 0)),
        out_shape=jax.ShapeDtypeStruct((nb, t, QPAD_W), BF16),
        scratch_shapes=[pltpu.VMEM((NSA_GROUP * Q_BLOCK, LANES), F32),
                        pltpu.VMEM((NSA_GROUP * Q_BLOCK, 2 * KV_W), F32),
                        pltpu.VMEM((Q_BLOCK, t_pad), F32),
                        pltpu.VMEM((NSA_KV_HEADS, NSA_GROUP * Q_BLOCK, KV_W), F32),
                        pltpu.VMEM((NSA_GROUP * Q_BLOCK, SEL_CHUNK), F32),
                        pltpu.VMEM((NSA_GROUP * Q_BLOCK, SEL_CHUNK), F32)],
        compiler_params=_cparams(("arbitrary", "arbitrary")),
        name="nsa_prompt",
    )(qpad, gates, kc, vc, kvsel, kvsel, kvwb, kvwb, expand)


def _nsa_sample_a_kernel(q_ref, g_ref, kc_ref, vc_ref, wb_ref, nw_ref, ocw_ref, sel_ref,
                         *, past_len, n_tok, n_sel, n_sel_pad):
    q = q_ref[0]
    rows = q.shape[0]
    n_cmp = kc_ref.shape[1]
    t_row = lax.broadcasted_iota(jnp.int32, (rows, 1), 0) & (n_tok - 1)
    qpos = past_len + t_row
    cend = lax.broadcasted_iota(jnp.int32, (1, n_cmp), 1) * CMP_STRIDE + (CMP_BLOCK - 1)
    p_c = _masked_softmax(_dot_nt(q, kc_ref[0]), cend <= qpos)
    o_c = _dot(p_c.astype(BF16), vc_ref[0])

    per_grp = NSA_GROUP * n_tok
    psum = jnp.concatenate(
        [sum(p_c[g * per_grp + jh * n_tok:g * per_grp + (jh + 1) * n_tok] for jh in range(NSA_GROUP))
         for g in range(NSA_KV_HEADS)], axis=0)
    imp = _dot(psum.astype(BF16), _overlap_matrix(n_cmp, n_sel_pad))
    blk = lax.broadcasted_iota(jnp.int32, (1, n_sel_pad), 1)
    tq = past_len + (lax.broadcasted_iota(jnp.int32, (NSA_KV_HEADS * n_tok, 1), 0) & (n_tok - 1))
    cur = lax.shift_right_logical(tq, 6)
    forced = (blk == 0) | (blk == cur) | (blk == cur - 1)
    allowed = blk * SEL_BLOCK <= tq
    v = jnp.where(forced, -NEG, jnp.where(allowed, imp, NEG))
    sel_ref[0] = _topk_mask(jnp.where(blk < n_sel, v, 2.0 * NEG), SEL_TOPN)

    wb = wb_ref.shape[1]
    kw = wb_ref[0, :, 0:KV_W].astype(BF16)
    vw = wb_ref[0, :, KV_W:2 * KV_W].astype(BF16)
    kn = nw_ref[0, :, 0:KV_W].astype(BF16)
    vn = nw_ref[0, :, KV_W:2 * KV_W].astype(BF16)
    i1 = lax.broadcasted_iota(jnp.int32, (1, wb), 1)
    d1 = t_row + wb - i1
    valid1 = (d1 >= 0) & (d1 < WINDOW) & (past_len - wb + i1 >= 0)
    i2 = lax.broadcasted_iota(jnp.int32, (1, nw_ref.shape[1]), 1)
    d2 = t_row - i2
    valid2 = (d2 >= 0) & (d2 < WINDOW) & (i2 < n_tok)
    s1 = jnp.where(valid1, _dot_nt(q, kw), NEG)
    s2 = jnp.where(valid2, _dot_nt(q, kn), NEG)
    m = jnp.maximum(jnp.max(s1, axis=1, keepdims=True), jnp.max(s2, axis=1, keepdims=True))
    p1 = jnp.exp2(s1 - m) * valid1.astype(F32)
    p2 = jnp.exp2(s2 - m) * valid2.astype(F32)
    den = jnp.maximum(jnp.sum(p1, axis=1, keepdims=True) + jnp.sum(p2, axis=1, keepdims=True), 1e-30)
    o_w = (_dot(p1.astype(BF16), vw) + _dot(p2.astype(BF16), vn)) / den
    g = g_ref[0]
    ocw_ref[0] = g[:, 0:1] * o_c + g[:, 2:3] * o_w


def _nsa_sample_a(q_rows, g_rows, kc, vc, win_buf, new_win, *, past_len, n_tok):
    nb, rows, _ = q_rows.shape
    n_sel = -(-(past_len + n_tok) // SEL_BLOCK)
    n_sel_pad = -(-n_sel // LANES) * LANES
    blk3 = lambda a: pl.BlockSpec((1,) + a.shape[1:], lambda b: (b, 0, 0))
    return pl.pallas_call(
        functools.partial(_nsa_sample_a_kernel, past_len=past_len, n_tok=n_tok, n_sel=n_sel, n_sel_pad=n_sel_pad),
        grid=(nb,),
        in_specs=[blk3(q_rows), blk3(g_rows), blk3(kc), blk3(vc), blk3(win_buf), blk3(new_win)],
        out_specs=[pl.BlockSpec((1, rows, KV_W), lambda b: (b, 0, 0)),
                   pl.BlockSpec((1, NSA_KV_HEADS * n_tok, n_sel_pad), lambda b: (b, 0, 0))],
        out_shape=[jax.ShapeDtypeStruct((nb, rows, KV_W), F32),
                   jax.ShapeDtypeStruct((nb, NSA_KV_HEADS * n_tok, n_sel_pad), F32)],
        compiler_params=_cparams(("arbitrary",)),
        name="nsa_sample_a",
    )(q_rows, g_rows, kc, vc, win_buf, new_win)


def _nsa_sample_b_kernel(pt_ref, cache_ref, q_ref, g_ref, sel_ref, seln_ref, ns_ref, ocw_ref, ex_ref, o_ref,
                         buf, sem, m_scr, l_scr, acc_scr, *, n_pages, n_steps, pps, n_tok):
    s = pl.program_id(1)
    slot = _stream_pages(pt_ref, cache_ref, buf, sem, n_pages, n_steps, pps, 2 * KV_W, False, True)
    q = q_ref[0]
    rows = q.shape[0]

    @pl.when(s == 0)
    def _():
        m_scr[...] = jnp.full(m_scr.shape, NEG, F32)
        l_scr[...] = jnp.zeros(l_scr.shape, F32)
        acc_scr[...] = jnp.zeros(acc_scr.shape, F32)

    def update(scores, msk, times_v):
        sc = jnp.where(msk, scores, NEG)
        m_old = m_scr[...]
        m_new = jnp.maximum(m_old, jnp.max(sc, axis=1, keepdims=True))
        p = jnp.exp2(sc - m_new) * msk.astype(F32)
        alpha = jnp.exp2(m_old - m_new)
        l_scr[...] = alpha * l_scr[...] + jnp.sum(p, axis=1, keepdims=True)
        acc_scr[...] = alpha * acc_scr[...] + times_v(p.astype(BF16))
        m_scr[...] = m_new

    chosen = _dot(sel_ref[0, 0], ex_ref[...]) > 0.5
    update(_dot(q, buf[slot, 0].astype(BF16)), chosen, lambda p: _dot_nt(p, buf[slot, 1].astype(BF16)))

    @pl.when(s == n_steps - 1)
    def _():
        t_row = lax.broadcasted_iota(jnp.int32, (rows, 1), 0) & (n_tok - 1)
        i2 = lax.broadcasted_iota(jnp.int32, (1, ns_ref.shape[1]), 1)
        msk = (seln_ref[0, 0][:, 0:1] > 0.5) & (i2 <= t_row) & (i2 < n_tok)
        update(_dot_nt(q, ns_ref[0, :, 0:KV_W].astype(BF16)), msk,
               lambda p: _dot(p, ns_ref[0, :, KV_W:2 * KV_W].astype(BF16)))
        o_s = acc_scr[...] / jnp.maximum(l_scr[...], 1e-30)
        o_ref[0] = ocw_ref[0] + g_ref[0][:, 1:2] * o_s


def _nsa_sample_b(page_table, cache, q_rows, g_rows, sel_steps, new_sel, ocw, *, n_tok, pps=PAGES_PER_STEP):
    nb, n_pages = page_table.shape
    n_steps = n_pages // pps
    rows = q_rows.shape[1]
    keys = pps * PAGE_SIZE
    expand = jnp.asarray(np.arange(LANES)[:, None] == (np.arange(keys)[None, :] // SEL_BLOCK), BF16)
    per_b = lambda a: pl.BlockSpec((1,) + a.shape[1:], lambda b, s, pt: (b, 0, 0))
    return pl.pallas_call(
        functools.partial(_nsa_sample_b_kernel, n_pages=n_pages, n_steps=n_steps, pps=pps, n_tok=n_tok),
        grid_spec=pltpu.PrefetchScalarGridSpec(
            num_scalar_prefetch=1,
            grid=(nb, n_steps),
            in_specs=[pl.BlockSpec(memory_space=pl.ANY),
                      per_b(q_rows), per_b(g_rows),
                      pl.BlockSpec((1, 1, rows, LANES), lambda b, s, pt: (b, s, 0, 0)),
                      pl.BlockSpec((1, 1, rows, LANES), lambda b, s, pt: (b, n_steps, 0, 0)),
                      per_b(new_sel), per_b(ocw),
                      pl.BlockSpec((LANES, keys), lambda b, s, pt: (0, 0))],
            out_specs=pl.BlockSpec((1, rows, KV_W), lambda b, s, pt: (b, 0, 0)),
            scratch_shapes=[pltpu.VMEM((2, 2, KV_W, keys), F32), pltpu.SemaphoreType.DMA((2,)),
                            pltpu.VMEM((rows, 1), F32), pltpu.VMEM((rows, 1), F32),
                            pltpu.VMEM((rows, KV_W), F32)]),
        out_shape=jax.ShapeDtypeStruct((nb, rows, KV_W), F32),
        compiler_params=_cparams(("arbitrary", "arbitrary")),
        name="nsa_sample_b",
    )(page_table, cache, q_rows, g_rows, sel_steps, sel_steps, new_sel, ocw, expand)


def _nsa_sample(page_table, cache, win_buf, cw, qpad, gates, kv4, kvw, pps=PAGES_PER_STEP):
    nb, ts, _ = qpad.shape
    past_len = page_table.shape[1] * PAGE_SIZE
    kc, vc = _compress(page_table, cache, *cw, pps=pps, transposed=True)
    rows = NSA_HEADS * ts
    q_rows = qpad.reshape(nb, ts, NSA_HEADS, LANES).transpose(0, 2, 1, 3).reshape(nb, rows, LANES)
    g_rows = gates[:, :, :3 * NSA_HEADS].reshape(nb, ts, NSA_HEADS, 3).transpose(0, 2, 1, 3)
    g_rows = jnp.pad(g_rows.reshape(nb, rows, 3), ((0, 0), (0, 0), (0, LANES - 3)))
    pad_rows = lambda a: jnp.pad(a, ((0, 0), (0, LANES - ts), (0, 0)))
    new_win = pad_rows(kvw)
    new_sel = pad_rows(kv4[:, :, 2 * KV_W:])
    ocw, sel = _nsa_sample_a(q_rows, g_rows, kc, vc, win_buf, new_win, past_len=past_len, n_tok=ts)
    n_steps = page_table.shape[1] // pps
    blk_per_step = pps * PAGE_SIZE // SEL_BLOCK
    n_past_blk = n_steps * blk_per_step
    sel_past = sel[:, :, :n_past_blk].reshape(nb, NSA_KV_HEADS, 1, ts, n_steps, blk_per_step)
    sel_past = jnp.broadcast_to(sel_past, (nb, NSA_KV_HEADS, NSA_GROUP, ts, n_steps, blk_per_step))
    sel_past = sel_past.transpose(0, 4, 1, 2, 3, 5).reshape(nb, n_steps, rows, blk_per_step)
    sel_past = jnp.pad(sel_past, ((0, 0), (0, 0), (0, 0), (0, LANES - blk_per_step)))
    sel_new = jnp.pad(sel[:, :, n_past_blk:], ((0, 0), (0, 0), (0, LANES)))[:, :, :LANES]
    sel_new = sel_new.reshape(nb, NSA_KV_HEADS, 1, ts, LANES)
    sel_new = jnp.broadcast_to(sel_new, (nb, NSA_KV_HEADS, NSA_GROUP, ts, LANES)).reshape(nb, 1, rows, LANES)
    sel_steps = jnp.concatenate([sel_past, sel_new], axis=1).astype(BF16)
    o_rows = _nsa_sample_b(page_table, cache, q_rows, g_rows, sel_steps, new_sel, ocw, n_tok=ts, pps=pps)
    return o_rows.reshape(nb, NSA_HEADS, ts, LANES).transpose(0, 2, 1, 3).reshape(nb, ts, QPAD_W).astype(BF16)


def _outproj_kernel(x_ref, hg_ref, nsa_ref, g1_ref, sc2_ref, sh2_ref, fn_ref, wo1_ref, wo2_ref,
                    x1_ref, h2_ref, *, tm):
    mix = _dot(hg_ref[0], wo1_ref[...]) + _dot(nsa_ref[0], wo2_ref[...])
    x1 = x_ref[0] + g1_ref[0] * mix
    x1_ref[0] = x1
    h2 = _rms(x1, fn_ref[...]) * (1.0 + sc2_ref[0]) + sh2_ref[0]
    _store_tok_tiles(h2_ref, h2, tm)


def _outproj(x, hg_out, nsa, gate1, scale2, shift2, ffn_norm, wo_hg, wo_nsa, tm):
    nb, t, _ = x.shape
    nt = t // tm
    mod_spec = _mod_spec(gate1, tm)
    tile = lambda w: pl.BlockSpec((1, tm, w), lambda b, i: (b, i, 0))
    return pl.pallas_call(
        functools.partial(_outproj_kernel, tm=tm),
        grid=(nb, nt),
        in_specs=[tile(D_MODEL), tile(HG_WIDTH), tile(QPAD_W), mod_spec, mod_spec, mod_spec,
                  pl.BlockSpec((1, D_MODEL), lambda b, i: (0, 0)),
                  pl.BlockSpec((HG_WIDTH, D_MODEL), lambda b, i: (0, 0)),
                  pl.BlockSpec((QPAD_W, D_MODEL), lambda b, i: (0, 0))],
        out_specs=[tile(D_MODEL), pl.BlockSpec((tm * TOK_ROWS, LANES), lambda b, i: (b * nt + i, 0))],
        out_shape=[jax.ShapeDtypeStruct((nb, t, D_MODEL), F32),
                   jax.ShapeDtypeStruct((nb * t * TOK_ROWS, LANES), F32)],
        compiler_params=_cparams(("arbitrary", "arbitrary")),
        name="out_proj",
    )(x, hg_out, nsa, gate1, scale2, shift2, ffn_norm.reshape(1, -1), wo_hg, wo_nsa)


def _split_w_out(w_out):
    wo_hg = w_out[:HG_WIDTH].astype(BF16)
    wn = w_out[HG_WIDTH:].reshape(NSA_HEADS, NSA_HEAD_DIM, D_MODEL)
    z = jnp.zeros_like(wn)
    grp = (jnp.arange(NSA_HEADS) // NSA_GROUP)[:, None, None]
    wn_pad = jnp.where(grp == 0, jnp.concatenate([wn, z], axis=1), jnp.concatenate([z, wn], axis=1))
    return wo_hg, wn_pad.reshape(QPAD_W, D_MODEL).astype(BF16)


def _router_kernel(h_ref, wr_ref, b_ref, e_ref, w_ref, r_ref, cnt_ref, run_scr, *, tm):
    @pl.when(pl.program_id(0) == 0)
    def _():
        run_scr[...] = jnp.zeros(run_scr.shape, F32)

    x = _load_tok_tiles(h_ref, tm).astype(BF16)
    scores = jax.nn.sigmoid(_dot(x, wr_ref[...]))
    biased = scores + b_ref[...]
    lane_i = lax.broadcasted_iota(jnp.int32, (tm, N_EXPERTS), 1)
    lane = lane_i.astype(F32)
    grp_of_lane = lax.shift_right_logical(lane_i, 5)
    per_group = N_EXPERTS // N_GROUPS

    gcol = lax.broadcasted_iota(jnp.int32, (tm, LANES), 1)
    gs = jnp.full((tm, LANES), 2.0 * NEG, F32)
    for g in range(N_GROUPS):
        mg = jnp.where(grp_of_lane == g, biased, NEG)
        m1 = jnp.max(mg, axis=1, keepdims=True)
        i1 = jnp.min(jnp.where(mg == m1, lane, 1e9), axis=1, keepdims=True)
        m2 = jnp.max(jnp.where(lane == i1, NEG, mg), axis=1, keepdims=True)
        gs = jnp.where(gcol == g, m1 + m2, gs)
    gsel = _topk_mask(gs, TOPK_GROUPS).astype(BF16)
    spread = (lax.broadcasted_iota(jnp.int32, (LANES, N_EXPERTS), 0)
              == lax.shift_right_logical(lax.broadcasted_iota(jnp.int32, (LANES, N_EXPERTS), 1), 5)).astype(BF16)
    v = jnp.where(_dot(gsel, spread) > 0.5, biased, NEG)

    onehot = jnp.zeros((tm, N_EXPERTS), F32)
    idxs, wts = [], []
    wsum = jnp.zeros((tm, 1), F32)
    for _ in range(TOP_K):
        m = jnp.max(v, axis=1, keepdims=True)
        idx = jnp.min(jnp.where(v == m, lane, 1e9), axis=1, keepdims=True)
        pick = lane == idx
        wk = jnp.sum(jnp.where(pick, scores, 0.0), axis=1, keepdims=True)
        onehot = jnp.where(pick, 1.0, onehot)
        v = jnp.where(pick, 3.0 * NEG, v)
        idxs.append(idx)
        wts.append(wk)
        wsum = wsum + wk

    earlier = (lax.broadcasted_iota(jnp.int32, (tm, tm), 0) > lax.broadcasted_iota(jnp.int32, (tm, tm), 1))
    before = _dot(earlier.astype(BF16), onehot.astype(BF16)) + run_scr[...]
    e_out = jnp.zeros((tm, LANES), jnp.int32)
    r_out = jnp.zeros((tm, LANES), jnp.int32)
    w_out = jnp.zeros((tm, LANES), F32)
    for k in range(TOP_K):
        rk = jnp.sum(jnp.where(lane == idxs[k], before, 0.0), axis=1, keepdims=True)
        e_out = jnp.where(gcol == k, idxs[k].astype(jnp.int32), e_out)
        r_out = jnp.where(gcol == k, rk.astype(jnp.int32), r_out)
        w_out = jnp.where(gcol == k, wts[k] / wsum * ROUTED_SCALE, w_out)
    e_ref[...] = e_out
    r_ref[...] = r_out
    w_ref[...] = w_out
    run_scr[...] = run_scr[...] + jnp.sum(onehot, axis=0, keepdims=True)
    cnt_ref[...] = run_scr[...]


def _router(h2, w_router, bias, n_tok, tm=ROUTER_TM):
    tile = pl.BlockSpec((tm, LANES), lambda i: (i, 0))
    return pl.pallas_call(
        functools.partial(_router_kernel, tm=tm),
        grid=(n_tok // tm,),
        in_specs=[pl.BlockSpec((tm * TOK_ROWS, LANES), lambda i: (i, 0)),
                  pl.BlockSpec((D_MODEL, N_EXPERTS), lambda i: (0, 0)),
                  pl.BlockSpec((1, N_EXPERTS), lambda i: (0, 0))],
        out_specs=[tile, tile, tile, pl.BlockSpec((1, N_EXPERTS), lambda i: (0, 0))],
        out_shape=[jax.ShapeDtypeStruct((n_tok, LANES), jnp.int32),
                   jax.ShapeDtypeStruct((n_tok, LANES), F32),
                   jax.ShapeDtypeStruct((n_tok, LANES), jnp.int32),
                   jax.ShapeDtypeStruct((1, N_EXPERTS), F32)],
        scratch_shapes=[pltpu.VMEM((1, N_EXPERTS), F32)],
        compiler_params=_cparams(("arbitrary",)),
        name="moe_router",
    )(h2, w_router, bias)


def _dest_kernel(e_ref, r_ref, st_ref, d_ref):
    e = e_ref[...]
    tm = e.shape[0]
    lane = lax.broadcasted_iota(jnp.int32, (tm, N_EXPERTS), 1)
    col = lax.broadcasted_iota(jnp.int32, (tm, LANES), 1)
    st = st_ref[...]
    out = r_ref[...]
    for k in range(TOP_K):
        sk = jnp.sum(jnp.where(lane == e[:, k:k + 1], st, 0.0), axis=1, keepdims=True)
        out = jnp.where(col == k, out + sk.astype(jnp.int32), out)
    d_ref[...] = out


def _moe_dest(top_e, rank, starts, tm=ROUTER_TM):
    n_tok = top_e.shape[0]
    tile = pl.BlockSpec((tm, LANES), lambda i: (i, 0))
    return pl.pallas_call(
        _dest_kernel,
        grid=(n_tok // tm,),
        in_specs=[tile, tile, pl.BlockSpec((1, N_EXPERTS), lambda i: (0, 0))],
        out_specs=tile,
        out_shape=jax.ShapeDtypeStruct((n_tok, LANES), jnp.int32),
        compiler_params=_cparams(("arbitrary",)),
        name="moe_dest",
    )(top_e, rank, starts.astype(F32).reshape(1, -1))


def _moe_layout(counts, n_pairs):
    padded = (counts + MOE_BM - 1) // MOE_BM * MOE_BM
    pad_end = jnp.cumsum(padded)
    starts = pad_end - padded
    n_blocks = (n_pairs + N_EXPERTS * (MOE_BM - 1)) // MOE_BM
    blk = jnp.arange(n_blocks, dtype=jnp.int32)
    used = blk * MOE_BM < pad_end[-1]
    blk_run = jnp.minimum(blk, pad_end[-1] // MOE_BM - 1).astype(jnp.int32)
    e_of = jnp.searchsorted(pad_end, blk_run * MOE_BM, side="right").astype(jnp.int32)
    e_of = jnp.minimum(e_of, N_EXPERTS - 1)
    shifted = jnp.concatenate([jnp.full((1,), -1, jnp.int32), e_of[:-1]])
    fresh = (used & (e_of != shifted)).astype(jnp.int32)
    return (starts.astype(jnp.int32), (starts + counts).astype(jnp.int32),
            (blk_run, e_of, fresh, used.astype(jnp.int32)), n_blocks)


def _dispatch_kernel(dest_ref, h_ref, xs_ref, sem, *, tm):
    n_pairs = tm * TOP_K

    def row_copy(src_tok, dst_row):
        return pltpu.make_async_copy(
            h_ref.at[pl.ds(pl.multiple_of(src_tok * TOK_ROWS, TOK_ROWS), TOK_ROWS), :],
            xs_ref.at[pl.ds(pl.multiple_of(dst_row * TOK_ROWS, TOK_ROWS), TOK_ROWS), :], sem)

    def issue(i, carry):
        for k in range(TOP_K):
            row_copy(i, dest_ref[0, 0, i * TOP_K + k]).start(priority=k % 2)
        return carry

    def drain(i, carry):
        for _ in range(TOP_K):
            row_copy(0, 0).wait()
        return carry

    lax.fori_loop(0, tm, issue, 0)
    lax.fori_loop(0, tm, drain, 0)


def _dispatch(dest_tiles, h2, n_pairs, tm=ROUTER_TM):
    n_tiles = dest_tiles.shape[0]
    return pl.pallas_call(
        functools.partial(_dispatch_kernel, tm=tm),
        grid=(n_tiles,),
        in_specs=[pl.BlockSpec((1, 1, tm * TOP_K), lambda i: (i, 0, 0), memory_space=pltpu.SMEM),
                  pl.BlockSpec((tm * TOK_ROWS, LANES), lambda i: (i, 0))],
        out_specs=pl.BlockSpec(memory_space=pl.ANY),
        out_shape=jax.ShapeDtypeStruct((n_pairs * TOK_ROWS, LANES), F32),
        scratch_shapes=[pltpu.SemaphoreType.DMA(())],
        compiler_params=_cparams(("arbitrary",)),
        name="moe_dispatch",
    )(dest_tiles, h2)


def _gmm_kernel(blk_ref, e_ref, fresh_ref, used_ref, end_ref,
                xs_ref, wg_ref, wu_ref, wd_ref, ys_ref, wg_bf, wu_bf, wd_bf, *, bm):
    i = pl.program_id(0)

    @pl.when(fresh_ref[i] == 1)
    def _():
        wg_bf[...] = wg_ref[0].astype(BF16)
        wu_bf[...] = wu_ref[0].astype(BF16)
        wd_bf[...] = wd_ref[0].astype(BF16)

    @pl.when(used_ref[i] == 1)
    def _():
        row = blk_ref[i] * bm + lax.broadcasted_iota(jnp.int32, (bm, 1), 0)
        real = row < end_ref[e_ref[i]]
        x = jnp.where(real, _load_tok_tiles(xs_ref, bm), 0.0).astype(BF16)
        hid = (_silu(_dot(x, wg_bf[...])) * _dot(x, wu_bf[...])).astype(BF16)
        _store_tok_tiles(ys_ref, _dot(hid, wd_bf[...]), bm)


def _moe_gmm(blocks, row_end, xs_sorted, w_gate, w_up, w_down, bm=MOE_BM):
    blk_run, e_of, fresh, used = blocks
    rows = pl.BlockSpec((bm * TOK_ROWS, LANES), lambda i, blk, e, *_: (blk[i], 0))
    wspec = lambda a: pl.BlockSpec((1,) + a.shape[1:], lambda i, blk, e, *_: (e[i], 0, 0))
    return pl.pallas_call(
        functools.partial(_gmm_kernel, bm=bm),
        grid_spec=pltpu.PrefetchScalarGridSpec(
            num_scalar_prefetch=5,
            grid=(blk_run.shape[0],),
            in_specs=[rows, wspec(w_gate), wspec(w_up), wspec(w_down)],
            out_specs=rows,
            scratch_shapes=[pltpu.VMEM(w_gate.shape[1:], BF16), pltpu.VMEM(w_up.shape[1:], BF16),
                            pltpu.VMEM(w_down.shape[1:], BF16)]),
        out_shape=jax.ShapeDtypeStruct(xs_sorted.shape, F32),
        compiler_params=_cparams(("arbitrary",)),
        name="moe_experts",
    )(blk_run, e_of, fresh, used, row_end, xs_sorted, w_gate, w_up, w_down)


def _combine_kernel(dest_ref, dnext_ref, w_ref, x1_ref, h_ref, g2_ref, wsg_ref, wsu_ref, wsd_ref, fn_ref, ys_ref,
                    o_ref, gbuf, sem, routed_scr, *, tm):
    n = pl.program_id(0) * pl.num_programs(1) + pl.program_id(1)
    total = pl.num_programs(0) * pl.num_programs(1)
    slot = n % 2

    def row_copy(src_row, p, sl):
        return pltpu.make_async_copy(
            ys_ref.at[pl.ds(pl.multiple_of(src_row * TOK_ROWS, TOK_ROWS), TOK_ROWS), :],
            gbuf.at[sl, pl.ds(pl.multiple_of(p * TOK_ROWS, TOK_ROWS), TOK_ROWS), :], sem.at[sl])

    def gather(rows_ref, sl):
        def issue(i, carry):
            for k in range(TOP_K):
                p = i * TOP_K + k
                row_copy(rows_ref[0, 0, p], p, sl).start(priority=k % 2)
            return carry
        lax.fori_loop(0, tm, issue, 0)

    def drain(i, carry):
        for _ in range(TOP_K):
            row_copy(0, 0, slot).wait()
        return carry

    @pl.when(n == 0)
    def _():
        gather(dest_ref, slot)

    @pl.when(n + 1 < total)
    def _():
        gather(dnext_ref, 1 - slot)

    h = _load_tok_tiles(h_ref, tm).astype(BF16)
    hid = (_silu(_dot(h, wsg_ref[...])) * _dot(h, wsu_ref[...])).astype(BF16)
    shared = _dot(hid, wsd_ref[...])
    lax.fori_loop(0, tm, drain, 0)

    def weigh(t, carry):
        acc = jnp.zeros((TOK_ROWS, LANES), F32)
        for k in range(TOP_K):
            p = t * TOP_K + k
            acc = acc + w_ref[0, 0, p] * gbuf[slot, pl.ds(pl.multiple_of(p * TOK_ROWS, TOK_ROWS), TOK_ROWS), :]
        routed_scr[pl.ds(pl.multiple_of(t * TOK_ROWS, TOK_ROWS), TOK_ROWS), :] = acc
        return carry

    lax.fori_loop(0, tm, weigh, 0)
    x2 = x1_ref[0] + g2_ref[0] * (_load_tok_tiles(routed_scr, tm) + shared)
    o_ref[0] = _rms(x2, fn_ref[...])


def _combine(dest_tiles, w_tiles, x1, h2, gate2, shared, fnorm, ys_sorted, tile0, tm=COMBINE_TM):
    nb, t, _ = x1.shape
    nt = t // tm
    flat = lambda b, i: tile0 + b * nt + i
    nxt = lambda b, i: tile0 + jnp.minimum(b * nt + i + 1, nb * nt - 1)
    mod_spec = _mod_spec(gate2, tm)
    const = lambda a: pl.BlockSpec(a.shape, lambda b, i: (0, 0))
    return pl.pallas_call(
        functools.partial(_combine_kernel, tm=tm),
        grid=(nb, nt),
        in_specs=[pl.BlockSpec((1, 1, tm * TOP_K), lambda b, i: (flat(b, i), 0, 0), memory_space=pltpu.SMEM),
                  pl.BlockSpec((1, 1, tm * TOP_K), lambda b, i: (nxt(b, i), 0, 0), memory_space=pltpu.SMEM),
                  pl.BlockSpec((1, 1, tm * TOP_K), lambda b, i: (flat(b, i), 0, 0), memory_space=pltpu.SMEM),
                  pl.BlockSpec((1, tm, D_MODEL), lambda b, i: (b, i, 0)),
                  pl.BlockSpec((tm * TOK_ROWS, LANES), lambda b, i: (flat(b, i), 0)),
                  mod_spec, const(shared[0]), const(shared[1]), const(shared[2]), const(fnorm),
                  pl.BlockSpec(memory_space=pl.ANY)],
        out_specs=pl.BlockSpec((1, tm, D_MODEL), lambda b, i: (b, i, 0)),
        out_shape=jax.ShapeDtypeStruct((nb, t, D_MODEL), F32),
        scratch_shapes=[pltpu.VMEM((2, tm * TOP_K * TOK_ROWS, LANES), F32), pltpu.SemaphoreType.DMA((2,)),
                        pltpu.VMEM((tm * TOK_ROWS, LANES), F32)],
        compiler_params=_cparams(("arbitrary", "arbitrary")),
        name="moe_combine",
    )(dest_tiles, dest_tiles, w_tiles, x1, h2, gate2, *shared, fnorm, ys_sorted)


def kernel(x_prompt, x_sample, c_prompt, c_sample, cache_nsa_kv, cache_win_kv, state_hgrn, page_table,
           attn_norm, ffn_norm, final_norm, hg_norm, w_ada, b_ada, w_in, hg_lb,
           cmp_pe, cmp_w1, cmp_b1, cmp_w2, w_out, w_router, router_bias,
           w_gate, w_up, w_down, ws_gate, ws_up, ws_down):
    nbp, t, _ = x_prompt.shape
    nbs, ts, _ = x_sample.shape
    ns = nbs * ts
    n_all = nbp * t + ns
    past_len = page_table.shape[1] * PAGE_SIZE

    c_all = jnp.concatenate([c_prompt, c_sample], axis=0)
    c_all = jnp.pad(c_all, ((0, -c_all.shape[0] % SUBLANES), (0, 0)))
    mod = _ada(c_all, w_ada[0], b_ada[0])
    modp = mod[:nbp].reshape(nbp, 1, 6, D_MODEL)
    mods = jnp.repeat(mod[nbp:nbp + nbs].reshape(nbs, 1, 6, D_MODEL), ts, axis=1).reshape(1, ns, 6, D_MODEL)

    w_pad = _pad_w_in(w_in[0])
    cw = _compress_weights(cmp_pe[0], cmp_w1[0], cmp_b1[0], cmp_w2[0])
    wo_hg, wo_nsa = _split_w_out(w_out[0])

    hg, qpad, kv4, kvw, gates, kvsel, kvwb = _inproj(
        x_prompt, modp[:, :, 1], modp[:, :, 0], attn_norm[0], w_pad, 512)
    hg_out, hg_state_p = _hgrn(hg, hg_lb, jnp.zeros((nbp, HG_HEADS, HG_DK, HG_DK), F32), hg_norm[0],
                               256, HG_CHUNK)
    n_pages_p = t // PAGE_SIZE
    ptp = jnp.arange(nbp * n_pages_p, dtype=jnp.int32).reshape(nbp, n_pages_p)
    kc, vc = _compress(ptp, kv4.reshape(nbp * n_pages_p, PAGE_SIZE, 4 * KV_W), *cw)
    nsa = _nsa_prompt(qpad, gates, kc, vc, kvsel, kvwb)
    x1p, h2p = _outproj(x_prompt, hg_out, nsa, modp[:, :, 2], modp[:, :, 4], modp[:, :, 3],
                        ffn_norm[0], wo_hg, wo_nsa, 512)

    xs = x_sample.reshape(1, ns, D_MODEL)
    hg_s, qpad_s, kv4_s, kvw_s, gates_s, _, _ = _inproj(
        xs, mods[:, :, 1], mods[:, :, 0], attn_norm[0], w_pad, ns)
    hg_out_s, hg_state_s = _hgrn(hg_s.reshape(nbs, ts, 4 * HG_WIDTH), hg_lb, state_hgrn[0], hg_norm[0], ts, ts)
    cache = cache_nsa_kv[0].transpose(0, 2, 3, 4, 1).reshape(-1, 4 * KV_W, PAGE_SIZE)
    win_buf = cache_win_kv[0].reshape(nbs, -1, 2 * KV_W)
    nsa_s = _nsa_sample(page_table, cache, win_buf, cw, qpad_s.reshape(nbs, ts, QPAD_W),
                        gates_s.reshape(nbs, ts, GZ_PAD), kv4_s.reshape(nbs, ts, 4 * KV_W),
                        kvw_s.reshape(nbs, ts, 2 * KV_W)).reshape(1, ns, QPAD_W)
    x1s, h2s = _outproj(xs, hg_out_s.reshape(1, ns, HG_WIDTH), nsa_s, mods[:, :, 2], mods[:, :, 4], mods[:, :, 3],
                        ffn_norm[0], wo_hg, wo_nsa, ns)

    h2 = jnp.concatenate([h2p, h2s], axis=0)
    top_e, top_w, rank, counts = _router(h2, w_router[0].astype(BF16), router_bias[0].reshape(1, -1), n_all)
    starts, row_end, blocks, n_blocks = _moe_layout(counts[0].astype(jnp.int32), n_all * TOP_K)
    dest = _moe_dest(top_e, rank, starts)[:, :TOP_K].reshape(-1)
    xs_sorted = _dispatch(dest.reshape(n_all // ROUTER_TM, 1, ROUTER_TM * TOP_K), h2, n_blocks * MOE_BM)
    ys_sorted = _moe_gmm(blocks, row_end, xs_sorted, w_gate[0], w_up[0], w_down[0])
    dest_c = dest.reshape(n_all // COMBINE_TM, 1, COMBINE_TM * TOP_K)
    w_c = top_w[:, :TOP_K].reshape(n_all // COMBINE_TM, 1, COMBINE_TM * TOP_K)
    shared = (ws_gate[0].astype(BF16), ws_up[0].astype(BF16), ws_down[0].astype(BF16))
    fnorm = final_norm.reshape(1, -1)
    y_prompt = _combine(dest_c, w_c, x1p, h2, modp[:, :, 5], shared, fnorm, ys_sorted, 0)
    y_sample = _combine(dest_c, w_c, x1s, h2, mods[:, :, 5], shared, fnorm, ys_sorted, nbp * t // COMBINE_TM)

    wb = win_buf.shape[1]
    win_p = kvw[:, t - min(WINDOW, t):]
    win_s = jnp.concatenate([win_buf, kvw_s.reshape(nbs, ts, 2 * KV_W)], axis=1)[:, -wb:]
    kv_shape = (4, NSA_KV_HEADS, NSA_HEAD_DIM)
    win_shape = (2, NSA_KV_HEADS, NSA_HEAD_DIM)
    return (y_prompt,
            y_sample.reshape(nbs, ts, D_MODEL),
            kv4.reshape(1, nbp, t, *kv_shape),
            win_p.reshape(1, nbp, -1, *win_shape),
            hg_state_p[None],
            kv4_s.reshape(1, nbs, ts, *kv_shape),
            win_s.reshape(1, nbs, wb, *win_shape),
            hg_state_s[None])
```

```python
import functools

import jax
import jax.numpy as jnp
import numpy as np
from jax import lax
from jax.experimental import pallas as pl
from jax.experimental.pallas import tpu as pltpu

F32 = jnp.float32
BF16 = jnp.bfloat16

D_MODEL = 1024
HG_WIDTH = 512
HG_HEADS = 4
HG_DK = 128
HG_CHUNK = 32
NSA_WIDTH = 512
NSA_HEADS = 8
NSA_HEAD_DIM = 64
NSA_KV_HEADS = 2
NSA_GROUP = 4
KV_W = 128
CMP_BLOCK = 32
CMP_STRIDE = 16
CMP_HIDDEN = 256
SEL_BLOCK = 64
SEL_TOPN = 16
WINDOW = 512
Q_BLOCK = 128
N_EXPERTS = 256
TOP_K = 8
N_GROUPS = 8
TOPK_GROUPS = 4
MOE_D_FF = 256
ROUTED_SCALE = 2.5
RMS_EPS = 1e-6
PAGE_SIZE = 128
IN_COLS = 4 * HG_WIDTH + NSA_WIDTH + 4 * KV_W + 2 * KV_W + 3 * NSA_HEADS

LANES = 128
SUBLANES = 8
TOK_ROWS = D_MODEL // LANES
VMEM_LIMIT = 56 * 1024 * 1024

QPAD_W = NSA_HEADS * LANES
GZ_PAD = LANES
INP_COLS = 4 * HG_WIDTH + QPAD_W + 4 * KV_W + 2 * KV_W + GZ_PAD

NEG = -1e30
PAGES_PER_STEP = 32
MOE_BM = 256
ROUTER_TM = 256
COMBINE_TM = 128


def _cparams(sem):
    return pltpu.CompilerParams(dimension_semantics=sem, vmem_limit_bytes=VMEM_LIMIT)


def _dot(a, b):
    return jnp.dot(a, b, preferred_element_type=F32)


def _dot_nt(a, b):
    return lax.dot_general(a, b, (((1,), (1,)), ((), ())), preferred_element_type=F32)


def _dot_tn(a, b):
    return lax.dot_general(a, b, (((0,), (0,)), ((), ())), preferred_element_type=F32)


def _rms(x, g):
    return x * lax.rsqrt(jnp.mean(x * x, axis=-1, keepdims=True) + RMS_EPS) * g


def _silu(x):
    return x * jax.nn.sigmoid(x)


Q_SCALE = NSA_HEAD_DIM ** -0.5 * 1.4426950408889634


def _masked_softmax(s, valid):
    s = jnp.where(valid, s, NEG)
    m = jnp.max(s, axis=1, keepdims=True)
    p = jnp.exp2(s - m) * valid.astype(F32)
    return p / jnp.maximum(jnp.sum(p, axis=1, keepdims=True), 1e-30)


def _topk_mask(v, k):
    lane = lax.broadcasted_iota(jnp.int32, v.shape, 1).astype(F32)
    sel = jnp.zeros(v.shape, F32)
    for _ in range(k):
        m = jnp.max(v, axis=1, keepdims=True)
        idx = jnp.min(jnp.where(v == m, lane, 1e9), axis=1, keepdims=True)
        pick = lane == idx
        sel = jnp.where(pick, 1.0, sel)
        v = jnp.where(pick, 3.0 * NEG, v)
    return sel


def _mod_spec(mod, tm):
    if mod.shape[1] == 1:
        return pl.BlockSpec((1, 1, D_MODEL), lambda b, i: (b, 0, 0))
    return pl.BlockSpec((1, tm, D_MODEL), lambda b, i: (b, i, 0))


def _load_tok_tiles(ref, n_tok):
    return jnp.concatenate([ref[pl.ds(s, n_tok, stride=TOK_ROWS), :] for s in range(TOK_ROWS)], axis=1)


def _store_tok_tiles(ref, val, n_tok):
    for s in range(TOK_ROWS):
        ref[pl.ds(s, n_tok, stride=TOK_ROWS), :] = val[:, s * LANES:(s + 1) * LANES]


def _ada_kernel(c_ref, w_ref, b_ref, o_ref):
    s = _silu(c_ref[...]).astype(BF16)
    o_ref[...] = _dot(s, w_ref[...].astype(BF16)) + b_ref[...]


def _ada(c_all, w_ada, b_ada):
    n = c_all.shape[0]
    return pl.pallas_call(
        _ada_kernel,
        grid=(6,),
        in_specs=[pl.BlockSpec((n, D_MODEL), lambda j: (0, 0)),
                  pl.BlockSpec((D_MODEL, D_MODEL), lambda j: (0, j)),
                  pl.BlockSpec((1, D_MODEL), lambda j: (0, j))],
        out_specs=pl.BlockSpec((n, D_MODEL), lambda j: (0, j)),
        out_shape=jax.ShapeDtypeStruct((n, 6 * D_MODEL), F32),
        compiler_params=_cparams(("arbitrary",)),
        name="ada_mod",
    )(c_all, w_ada, b_ada.reshape(1, -1))


def _inproj_kernel(x_ref, sc_ref, sh_ref, g_ref, w_ref,
                   hg_ref, q_ref, kv4_ref, kvw_ref, gate_ref, kvsel_ref, kvwb_ref):
    h = _rms(x_ref[0], g_ref[...]) * (1.0 + sc_ref[0]) + sh_ref[0]
    z = _dot(h.astype(BF16), w_ref[...])
    c0 = 4 * HG_WIDTH
    hg_ref[0] = z[:, :c0]
    q_ref[0] = (z[:, c0:c0 + QPAD_W] * Q_SCALE).astype(BF16)
    c1 = c0 + QPAD_W
    kv4 = z[:, c1:c1 + 4 * KV_W]
    kv4_ref[0] = kv4
    kvsel_ref[0] = kv4[:, 2 * KV_W:].astype(BF16)
    c2 = c1 + 4 * KV_W
    kvw = z[:, c2:c2 + 2 * KV_W]
    kvw_ref[0] = kvw
    kvwb_ref[0] = kvw.astype(BF16)
    gate_ref[0] = jax.nn.sigmoid(z[:, c2 + 2 * KV_W:])


def _inproj(x, scale, shift, g_norm, w_pad, tm):
    nb, t, _ = x.shape
    mod_spec = _mod_spec(scale, tm)
    widths = [(4 * HG_WIDTH, F32), (QPAD_W, BF16), (4 * KV_W, F32), (2 * KV_W, F32), (GZ_PAD, F32),
              (2 * KV_W, BF16), (2 * KV_W, BF16)]
    return pl.pallas_call(
        _inproj_kernel,
        grid=(nb, t // tm),
        in_specs=[pl.BlockSpec((1, tm, D_MODEL), lambda b, i: (b, i, 0)),
                  mod_spec, mod_spec,
                  pl.BlockSpec((1, D_MODEL), lambda b, i: (0, 0)),
                  pl.BlockSpec((D_MODEL, INP_COLS), lambda b, i: (0, 0))],
        out_specs=[pl.BlockSpec((1, tm, w), lambda b, i: (b, i, 0)) for w, _ in widths],
        out_shape=[jax.ShapeDtypeStruct((nb, t, w), dt) for w, dt in widths],
        compiler_params=_cparams(("arbitrary", "arbitrary")),
        name="in_proj",
    )(x, scale, shift, g_norm.reshape(1, -1), w_pad)


def _pad_w_in(w_in):
    c0 = 4 * HG_WIDTH
    wq = w_in[:, c0:c0 + NSA_WIDTH].reshape(D_MODEL, NSA_HEADS, NSA_HEAD_DIM)
    zeros = jnp.zeros_like(wq)
    lo = jnp.concatenate([wq, zeros], axis=-1)
    hi = jnp.concatenate([zeros, wq], axis=-1)
    grp = (jnp.arange(NSA_HEADS) // NSA_GROUP)[None, :, None]
    wq_pad = jnp.where(grp == 0, lo, hi).reshape(D_MODEL, QPAD_W)
    c1 = c0 + NSA_WIDTH
    rest = w_in[:, c1:c1 + 6 * KV_W]
    gz = jnp.pad(w_in[:, c1 + 6 * KV_W:], ((0, 0), (0, GZ_PAD - 3 * NSA_HEADS)))
    return jnp.concatenate([w_in[:, :c0], wq_pad, rest, gz], axis=1).astype(BF16)


def _hgrn_kernel(q_ref, f_ref, v_ref, gt_ref, lb_ref, s0_ref, gn_ref, o_ref, s_out_ref, st_scr,
                 *, chunk, n_chunks):
    i = pl.program_id(1)

    @pl.when(i == 0)
    def _():
        for h in range(HG_HEADS):
            st_scr[h] = s0_ref[0, h].T

    lbr = lb_ref[...]
    e = jnp.exp(lbr - jnp.max(lbr, axis=0, keepdims=True))
    lb_all = e[0:1] / jnp.sum(e, axis=0, keepdims=True)
    row = lax.broadcasted_iota(jnp.int32, (chunk, HG_DK), 0)
    causal = (lax.broadcasted_iota(jnp.int32, (chunk, chunk), 0)
              >= lax.broadcasted_iota(jnp.int32, (chunk, chunk), 1))
    st = [st_scr[h] for h in range(HG_HEADS)]
    for c in range(n_chunks):
        sl = pl.ds(c * chunk, chunk)
        for h in range(HG_HEADS):
            hs = slice(h * HG_DK, (h + 1) * HG_DK)
            lb = lb_all[:, hs]
            z = f_ref[0, sl, hs]
            log_f = jnp.log(lb + (1.0 - lb) * jax.nn.sigmoid(z))
            kk = (1.0 - lb) * jax.nn.sigmoid(-z)
            a = log_f
            s = 1
            while s < chunk:
                a = a + jnp.where(row >= s, pltpu.roll(a, s, 0), 0.0)
                s *= 2
            qt = (q_ref[0, sl, hs] * jnp.exp(a)).astype(BF16)
            kt = (kk * jnp.exp(-a)).astype(BF16)
            v = v_ref[0, sl, hs].astype(BF16)
            att = jnp.where(causal, _dot_nt(qt, kt), 0.0)
            o = _dot(att.astype(BF16), v) + _dot_nt(qt, st[h].astype(BF16))
            a_end = a[chunk - 1:chunk, :]
            kd = (kk * jnp.exp(a_end - a)).astype(BF16)
            st[h] = st[h] * jnp.exp(a_end) + _dot_tn(v, kd)
            o = _rms(o, gn_ref[...]) * _silu(gt_ref[0, sl, hs])
            o_ref[0, sl, hs] = o.astype(o_ref.dtype)
    for h in range(HG_HEADS):
        st_scr[h] = st[h]

    @pl.when(i == pl.num_programs(1) - 1)
    def _():
        for h in range(HG_HEADS):
            s_out_ref[0, h] = st[h].T


def _hgrn(hg, hg_lb, s0, g_norm, tc, chunk):
    nb, t, _ = hg.shape
    part = lambda k: pl.BlockSpec((1, tc, HG_WIDTH), lambda b, i: (b, i, k))
    st_spec = pl.BlockSpec((1, HG_HEADS, HG_DK, HG_DK), lambda b, i: (b, 0, 0, 0))
    return pl.pallas_call(
        functools.partial(_hgrn_kernel, chunk=chunk, n_chunks=tc // chunk),
        grid=(nb, t // tc),
        in_specs=[part(0), part(1), part(2), part(3),
                  pl.BlockSpec(hg_lb.shape, lambda b, i: (0, 0)),
                  st_spec,
                  pl.BlockSpec((1, HG_DK), lambda b, i: (0, 0))],
        out_specs=[pl.BlockSpec((1, tc, HG_WIDTH), lambda b, i: (b, i, 0)), st_spec],
        out_shape=[jax.ShapeDtypeStruct((nb, t, HG_WIDTH), BF16),
                   jax.ShapeDtypeStruct((nb, HG_HEADS, HG_DK, HG_DK), F32)],
        scratch_shapes=[pltpu.VMEM((HG_HEADS, HG_DK, HG_DK), F32)],
        compiler_params=_cparams(("arbitrary", "arbitrary")),
        name="hgrn2",
    )(hg, hg, hg, hg, hg_lb, s0, g_norm.reshape(1, -1))


def _gelu_tanh(x):
    return 0.5 * x * (1.0 + jnp.tanh(0.7978845608028654 * (x + 0.044715 * x * x * x)))


def _page_copies(pt_ref, cache_ref, buf, sem, b, s, slot, n_pages, pps, col0, tail, transposed):
    copies = []
    base = s * pps
    nxt = pt_ref[b, jnp.minimum(base + pps, n_pages - 1)]
    for br in range(2):
        cols = pl.ds(col0 + br * KV_W, KV_W)
        for i in range(pps):
            pg = pt_ref[b, base + i]
            if transposed:
                copies.append(pltpu.make_async_copy(
                    cache_ref.at[pg, cols, :],
                    buf.at[slot, br, :, pl.ds(i * PAGE_SIZE, PAGE_SIZE)], sem.at[slot]))
            else:
                copies.append(pltpu.make_async_copy(
                    cache_ref.at[pg, :, cols],
                    buf.at[slot, br, pl.ds(i * PAGE_SIZE, PAGE_SIZE), :], sem.at[slot]))
        if tail and transposed:
            copies.append(pltpu.make_async_copy(
                cache_ref.at[nxt, cols, :],
                buf.at[slot, br, :, pl.ds(pps * PAGE_SIZE, PAGE_SIZE)], sem.at[slot]))
        elif tail:
            copies.append(pltpu.make_async_copy(
                cache_ref.at[nxt, pl.ds(0, CMP_STRIDE), cols],
                buf.at[slot, br, pl.ds(pps * PAGE_SIZE, CMP_STRIDE), :], sem.at[slot]))
    return copies


def _stream_pages(pt_ref, cache_ref, buf, sem, n_pages, n_steps, pps, col0, tail, transposed):
    b = pl.program_id(0)
    s = pl.program_id(1)
    n = b * n_steps + s
    total = pl.num_programs(0) * n_steps
    slot = n % 2
    args = (n_pages, pps, col0, tail, transposed)

    @pl.when(n == 0)
    def _():
        for cp in _page_copies(pt_ref, cache_ref, buf, sem, b, s, slot, *args):
            cp.start()

    @pl.when(n + 1 < total)
    def _():
        n1 = n + 1
        for cp in _page_copies(pt_ref, cache_ref, buf, sem, n1 // n_steps, n1 % n_steps, 1 - slot, *args):
            cp.start()

    for cp in _page_copies(pt_ref, cache_ref, buf, sem, b, s, slot, *args):
        cp.wait()
    return slot


def _compress_kernel(pt_ref, cache_ref, pe_ref, w1_ref, b1_ref, w2_ref, kc_ref, vc_ref, buf, sem, *rowbuf,
                     n_pages, n_steps, pps, transposed):
    groups = pps * PAGE_SIZE // CMP_STRIDE
    slot = _stream_pages(pt_ref, cache_ref, buf, sem, n_pages, n_steps, pps, 0, True, transposed)

    if transposed:
        rows_ref, = rowbuf
        for br in range(2):
            for i in range(pps + 1):
                n_rows = PAGE_SIZE if i < pps else CMP_STRIDE
                page_t = buf[slot, br, :, i * PAGE_SIZE:(i + 1) * PAGE_SIZE]
                rows_ref[br, i * PAGE_SIZE:i * PAGE_SIZE + n_rows, :] = page_t.T[:n_rows]
        read_rows = lambda br, l: rows_ref[br, pl.ds(l, groups, stride=CMP_STRIDE), :]
    else:
        read_rows = lambda br, l: buf[slot, br, pl.ds(l, groups, stride=CMP_STRIDE), :]

    low_half = lax.broadcasted_iota(jnp.int32, (groups, KV_W), 1) < NSA_HEAD_DIM
    for br, out_ref in ((0, kc_ref), (1, vc_ref)):
        acc0 = jnp.zeros((groups, CMP_HIDDEN), F32)
        acc1 = jnp.zeros((groups, CMP_HIDDEN), F32)
        for j in range(CMP_BLOCK // 2):
            a = read_rows(br, 2 * j) + pe_ref[br, 2 * j:2 * j + 1, :]
            b = pltpu.roll(read_rows(br, 2 * j + 1) + pe_ref[br, 2 * j + 1:2 * j + 2, :], NSA_HEAD_DIM, 1)
            acc0 = acc0 + _dot(jnp.where(low_half, a, b).astype(BF16), w1_ref[br, j, 0])
            acc1 = acc1 + _dot(jnp.where(low_half, b, a).astype(BF16), w1_ref[br, j, 1])
        hid = _gelu_tanh(jnp.concatenate([acc0, acc1], axis=1) + b1_ref[br]).astype(BF16)
        out_ref[0] = _dot(hid, w2_ref[br]).astype(out_ref.dtype)


def _compress(page_table, cache, pe2, w1cat, b1cat, w2bd, pps=PAGES_PER_STEP, transposed=False):
    nb, n_pages = page_table.shape
    n_steps = n_pages // pps
    groups = pps * PAGE_SIZE // CMP_STRIDE
    rows = pps * PAGE_SIZE + CMP_STRIDE
    const = lambda shape: pl.BlockSpec(shape, lambda b, s, pt: (0,) * len(shape))
    out_spec = pl.BlockSpec((1, groups, KV_W), lambda b, s, pt: (b, s, 0))
    out_sds = jax.ShapeDtypeStruct((nb, n_steps * groups, KV_W), BF16)
    if transposed:
        stage = [pltpu.VMEM((2, 2, KV_W, (pps + 1) * PAGE_SIZE), F32), pltpu.SemaphoreType.DMA((2,)),
                 pltpu.VMEM((2, rows, KV_W), F32)]
    else:
        stage = [pltpu.VMEM((2, 2, rows, KV_W), F32), pltpu.SemaphoreType.DMA((2,))]
    return pl.pallas_call(
        functools.partial(_compress_kernel, n_pages=n_pages, n_steps=n_steps, pps=pps, transposed=transposed),
        grid_spec=pltpu.PrefetchScalarGridSpec(
            num_scalar_prefetch=1,
            grid=(nb, n_steps),
            in_specs=[pl.BlockSpec(memory_space=pl.ANY),
                      const((2, CMP_BLOCK, KV_W)),
                      const((2, CMP_BLOCK // 2, 2, KV_W, CMP_HIDDEN)),
                      const((2, 1, 2 * CMP_HIDDEN)),
                      const((2, 2 * CMP_HIDDEN, KV_W))],
            out_specs=[out_spec, out_spec],
            scratch_shapes=stage),
        out_shape=[out_sds, out_sds],
        compiler_params=_cparams(("arbitrary", "arbitrary")),
        name="nsa_compress",
    )(page_table, cache, pe2, w1cat, b1cat, w2bd)


def _compress_weights(cmp_pe, cmp_w1, cmp_b1, cmp_w2):
    pe2 = jnp.concatenate([cmp_pe, cmp_pe], axis=-1)
    w1 = cmp_w1.reshape(2, CMP_BLOCK // 2, 2, NSA_HEAD_DIM, CMP_HIDDEN)
    even_odd = w1.reshape(2, CMP_BLOCK // 2, 2 * NSA_HEAD_DIM, CMP_HIDDEN)
    odd_even = w1[:, :, ::-1].reshape(2, CMP_BLOCK // 2, 2 * NSA_HEAD_DIM, CMP_HIDDEN)
    w1cat = jnp.stack([even_odd, odd_even], axis=2).astype(BF16)
    b1cat = jnp.concatenate([cmp_b1, cmp_b1], axis=-1)[:, None, :]
    z2 = jnp.zeros_like(cmp_w2)
    w2bd = jnp.concatenate([jnp.concatenate([cmp_w2, z2], axis=-1),
                            jnp.concatenate([z2, cmp_w2], axis=-1)], axis=1).astype(BF16)
    return pe2, w1cat, b1cat, w2bd


def _overlap_matrix(n_cmp, n_sel):
    cs = lax.broadcasted_iota(jnp.int32, (n_cmp, n_sel), 0) * CMP_STRIDE
    ss = lax.broadcasted_iota(jnp.int32, (n_cmp, n_sel), 1) * SEL_BLOCK
    return ((cs < ss + SEL_BLOCK) & (cs + CMP_BLOCK > ss)).astype(BF16)


SEL_CHUNK = 512
SPREAD_KEYS = 1024
WIN_SPAN = WINDOW + Q_BLOCK


def _nsa_prompt_kernel(q_ref, gt_ref, kc_ref, vc_ref, ks_ref, vs_ref, kw_ref, vw_ref, ex_ref, o_ref,
                       m_scr, acc_scr, chosen_scr, oc_scr, sa_scr, sb_scr, *, n_cmp, n_sel):
    j = pl.program_id(1)
    q0 = j * Q_BLOCK
    tok = lax.broadcasted_iota(jnp.int32, (Q_BLOCK, 1), 0) + q0
    tok4 = jnp.concatenate([tok] * NSA_GROUP, axis=0)
    n_chunks = j // (SEL_CHUNK // Q_BLOCK) + 1
    key_in_span = lax.broadcasted_iota(jnp.int32, (1, SPREAD_KEYS), 1)

    def load_q(g):
        return jnp.concatenate([q_ref[0, :, (NSA_GROUP * g + jh) * LANES:(NSA_GROUP * g + jh + 1) * LANES]
                                for jh in range(NSA_GROUP)], axis=0)

    def softmax_av(s, valid, v):
        s = jnp.where(valid, s, NEG)
        tiles = [s[:, i * LANES:(i + 1) * LANES] for i in range(s.shape[1] // LANES)]
        m = jnp.max(functools.reduce(jnp.maximum, tiles), axis=1, keepdims=True)
        p = jnp.where(valid, jnp.exp2(s - m), 0.0)
        acc = _dot(p.astype(BF16), jnp.concatenate([v, jnp.ones(v.shape, BF16)], axis=1))
        inv = 1.0 / jnp.maximum(acc[:, KV_W:], 1e-30)
        return p, acc[:, :KV_W] * inv, inv

    ov = _overlap_matrix(n_cmp, n_sel)
    cend = lax.broadcasted_iota(jnp.int32, (1, n_cmp), 1) * CMP_STRIDE + (CMP_BLOCK - 1)
    ws = pl.multiple_of(jnp.maximum(q0 - WINDOW, 0), Q_BLOCK)
    wpos = ws + lax.broadcasted_iota(jnp.int32, (1, WIN_SPAN), 1)
    d = tok4 - wpos
    in_window = (d >= 0) & (d < WINDOW)
    gates = gt_ref[0]
    imps = []
    for g in range(NSA_KV_HEADS):
        q = load_q(g)
        p, o_c, inv = softmax_av(_dot_nt(q, kc_ref[0]), cend <= tok4, vc_ref[0])
        _, o_w, _ = softmax_av(_dot_nt(q, kw_ref[0, pl.ds(ws, WIN_SPAN), :]), in_window,
                               vw_ref[0, pl.ds(ws, WIN_SPAN), :])
        psum = jnp.zeros((Q_BLOCK, n_cmp), F32)
        for jh in range(NSA_GROUP):
            h = NSA_GROUP * g + jh
            r = slice(jh * Q_BLOCK, (jh + 1) * Q_BLOCK)
            psum = psum + p[r] * jnp.concatenate([inv[r]] * (n_cmp // LANES), axis=1)
            oc_scr[g, r, :] = gates[:, 3 * h:3 * h + 1] * o_c[r] + gates[:, 3 * h + 2:3 * h + 3] * o_w[r]
        imps.append(_dot(psum.astype(BF16), ov))

    blk = lax.broadcasted_iota(jnp.int32, (1, n_sel), 1)
    cur = lax.shift_right_logical(tok, 6)
    forced = (blk == 0) | (blk == cur) | (blk == cur - 1)
    free = (blk * SEL_BLOCK <= tok) & jnp.logical_not(forced)
    forced2 = jnp.concatenate([forced] * NSA_KV_HEADS, axis=0)
    free2 = jnp.concatenate([free] * NSA_KV_HEADS, axis=0)
    best = _topk_mask(jnp.where(free2, jnp.concatenate(imps, axis=0), NEG), SEL_TOPN - 3)
    sel2 = jnp.where(forced2, 1.0, best).astype(BF16)

    ones_blk = jnp.ones((SEL_CHUNK, KV_W), BF16)

    for g in range(NSA_KV_HEADS):
        sel = sel2[g * Q_BLOCK:(g + 1) * Q_BLOCK]
        m_scr[...] = jnp.full(m_scr.shape, NEG, F32)
        acc_scr[...] = jnp.zeros(acc_scr.shape, F32)

        def spread(i, carry):
            c0 = pl.multiple_of(i * SPREAD_KEYS, SPREAD_KEYS)
            keep = (_dot(sel, ex_ref[:, pl.ds(c0, SPREAD_KEYS)]) > 0.5) & (key_in_span + c0 <= tok)
            chosen_scr[:, pl.ds(c0, SPREAD_KEYS)] = jnp.where(keep, 0.0, NEG)
            return carry

        lax.fori_loop(0, (n_chunks * SEL_CHUNK + SPREAD_KEYS - 1) // SPREAD_KEYS, spread, 0)

        heads = range(NSA_GROUP)
        rows = [pl.ds(jh * Q_BLOCK, Q_BLOCK) for jh in heads]

        def stage_scores(c, buf):
            k0 = pl.multiple_of(c * SEL_CHUNK, SEL_CHUNK)
            bias = chosen_scr[:, pl.ds(k0, SEL_CHUNK)]
            kblk = ks_ref[0, pl.ds(k0, SEL_CHUNK), :]
            for jh in heads:
                h = NSA_GROUP * g + jh
                buf[rows[jh], :] = _dot_nt(q_ref[0, :, h * LANES:(h + 1) * LANES], kblk) + bias

        def accumulate(c, buf):
            k0 = pl.multiple_of(c * SEL_CHUNK, SEL_CHUNK)
            vext = jnp.concatenate([vs_ref[0, pl.ds(k0, SEL_CHUNK), :], ones_blk], axis=1)
            tiles = [[buf[r, i * LANES:(i + 1) * LANES] for i in range(SEL_CHUNK // LANES)] for r in rows]
            m_old = [m_scr[r, :] for r in rows]
            m_new = [jnp.maximum(mo, jnp.max(functools.reduce(jnp.maximum, t), axis=1, keepdims=True))
                     for mo, t in zip(m_old, tiles)]
            probs = [jnp.concatenate([jnp.exp2(x - mn) for x in t], axis=1).astype(BF16)
                     for mn, t in zip(m_new, tiles)]
            pv = [_dot(p, vext) for p in probs]
            for r, mo, mn, y in zip(rows, m_old, m_new, pv):
                alpha = jnp.exp2(mo - mn)
                acc_scr[r, :] = jnp.concatenate([alpha, alpha], axis=1) * acc_scr[r, :] + y
                m_scr[r, :] = mn

        n_pairs = (n_chunks + 1) // 2
        stage_scores(0, sa_scr)

        def body(i, carry):
            stage_scores(2 * i + 1, sb_scr)
            accumulate(2 * i, sa_scr)
            stage_scores(jnp.minimum(2 * i + 2, 2 * n_pairs - 1), sa_scr)
            accumulate(2 * i + 1, sb_scr)
            return carry

        lax.fori_loop(0, n_pairs, body, 0)
        o_s = acc_scr[:, :KV_W] / jnp.maximum(acc_scr[:, KV_W:], 1e-30)

        for jh in range(NSA_GROUP):
            h = NSA_GROUP * g + jh
            r = slice(jh * Q_BLOCK, (jh + 1) * Q_BLOCK)
            o = oc_scr[g, r, :] + gates[:, 3 * h + 1:3 * h + 2] * o_s[r]
            o_ref[0, :, h * LANES:(h + 1) * LANES] = o.astype(o_ref.dtype)


def _nsa_prompt(qpad, gates, kc, vc, kvsel, kvwb):
    nb, t, _ = qpad.shape
    assert t % SPREAD_KEYS == 0 and SPREAD_KEYS == 2 * SEL_CHUNK and t >= WIN_SPAN
    n_cmp = kc.shape[1]
    n_sel = t // SEL_BLOCK
    full = lambda w, k: pl.BlockSpec((1, t, w), lambda b, j: (b, 0, k))
    t_pad = -(-t // SPREAD_KEYS) * SPREAD_KEYS
    expand = jnp.asarray(np.arange(n_sel)[:, None] == (np.arange(t_pad)[None, :] // SEL_BLOCK), BF16)
    return pl.pallas_call(
        functools.partial(_nsa_prompt_kernel, n_cmp=n_cmp, n_sel=n_sel),
        grid=(nb, t // Q_BLOCK),
        in_specs=[pl.BlockSpec((1, Q_BLOCK, QPAD_W), lambda b, j: (b, j, 0)),
                  pl.BlockSpec((1, Q_BLOCK, GZ_PAD), lambda b, j: (b, j, 0)),
                  pl.BlockSpec((1, n_cmp, KV_W), lambda b, j: (b, 0, 0)),
                  pl.BlockSpec((1, n_cmp, KV_W), lambda b, j: (b, 0, 0)),
                  full(KV_W, 0), full(KV_W, 1), full(KV_W, 0), full(KV_W, 1),
                  pl.BlockSpec((n_sel, t_pad), lambda b, j: (0, 0))],
        out_specs=pl.BlockSpec((1, Q_BLOCK, QPAD_W), lambda b, j: (b, j, 0)),
        out_shape=jax.ShapeDtypeStruct((nb, t, QPAD_W), BF16),
        scratch_shapes=[pltpu.VMEM((NSA_GROUP * Q_BLOCK, LANES), F32),
                        pltpu.VMEM((NSA_GROUP * Q_BLOCK, 2 * KV_W), F32),
                        pltpu.VMEM((Q_BLOCK, t_pad), F32),
                        pltpu.VMEM((NSA_KV_HEADS, NSA_GROUP * Q_BLOCK, KV_W), F32),
                        pltpu.VMEM((NSA_GROUP * Q_BLOCK, SEL_CHUNK), F32),
                        pltpu.VMEM((NSA_GROUP * Q_BLOCK, SEL_CHUNK), F32)],
        compiler_params=_cparams(("arbitrary", "arbitrary")),
        name="nsa_prompt",
    )(qpad, gates, kc, vc, kvsel, kvsel, kvwb, kvwb, expand)


def _nsa_sample_a_kernel(q_ref, g_ref, kc_ref, vc_ref, wb_ref, nw_ref, ocw_ref, sel_ref,
                         *, past_len, n_tok, n_sel, n_sel_pad):
    q = q_ref[0]
    rows = q.shape[0]
    n_cmp = kc_ref.shape[1]
    t_row = lax.broadcasted_iota(jnp.int32, (rows, 1), 0) & (n_tok - 1)
    qpos = past_len + t_row
    cend = lax.broadcasted_iota(jnp.int32, (1, n_cmp), 1) * CMP_STRIDE + (CMP_BLOCK - 1)
    p_c = _masked_softmax(_dot_nt(q, kc_ref[0]), cend <= qpos)
    o_c = _dot(p_c.astype(BF16), vc_ref[0])

    per_grp = NSA_GROUP * n_tok
    psum = jnp.concatenate(
        [sum(p_c[g * per_grp + jh * n_tok:g * per_grp + (jh + 1) * n_tok] for jh in range(NSA_GROUP))
         for g in range(NSA_KV_HEADS)], axis=0)
    imp = _dot(psum.astype(BF16), _overlap_matrix(n_cmp, n_sel_pad))
    blk = lax.broadcasted_iota(jnp.int32, (1, n_sel_pad), 1)
    tq = past_len + (lax.broadcasted_iota(jnp.int32, (NSA_KV_HEADS * n_tok, 1), 0) & (n_tok - 1))
    cur = lax.shift_right_logical(tq, 6)
    forced = (blk == 0) | (blk == cur) | (blk == cur - 1)
    allowed = blk * SEL_BLOCK <= tq
    v = jnp.where(forced, -NEG, jnp.where(allowed, imp, NEG))
    sel_ref[0] = _topk_mask(jnp.where(blk < n_sel, v, 2.0 * NEG), SEL_TOPN)

    wb = wb_ref.shape[1]
    kw = wb_ref[0, :, 0:KV_W].astype(BF16)
    vw = wb_ref[0, :, KV_W:2 * KV_W].astype(BF16)
    kn = nw_ref[0, :, 0:KV_W].astype(BF16)
    vn = nw_ref[0, :, KV_W:2 * KV_W].astype(BF16)
    i1 = lax.broadcasted_iota(jnp.int32, (1, wb), 1)
    d1 = t_row + wb - i1
    valid1 = (d1 >= 0) & (d1 < WINDOW) & (past_len - wb + i1 >= 0)
    i2 = lax.broadcasted_iota(jnp.int32, (1, nw_ref.shape[1]), 1)
    d2 = t_row - i2
    valid2 = (d2 >= 0) & (d2 < WINDOW) & (i2 < n_tok)
    s1 = jnp.where(valid1, _dot_nt(q, kw), NEG)
    s2 = jnp.where(valid2, _dot_nt(q, kn), NEG)
    m = jnp.maximum(jnp.max(s1, axis=1, keepdims=True), jnp.max(s2, axis=1, keepdims=True))
    p1 = jnp.exp2(s1 - m) * valid1.astype(F32)
    p2 = jnp.exp2(s2 - m) * valid2.astype(F32)
    den = jnp.maximum(jnp.sum(p1, axis=1, keepdims=True) + jnp.sum(p2, axis=1, keepdims=True), 1e-30)
    o_w = (_dot(p1.astype(BF16), vw) + _dot(p2.astype(BF16), vn)) / den
    g = g_ref[0]
    ocw_ref[0] = g[:, 0:1] * o_c + g[:, 2:3] * o_w


def _nsa_sample_a(q_rows, g_rows, kc, vc, win_buf, new_win, *, past_len, n_tok):
    nb, rows, _ = q_rows.shape
    n_sel = -(-(past_len + n_tok) // SEL_BLOCK)
    n_sel_pad = -(-n_sel // LANES) * LANES
    blk3 = lambda a: pl.BlockSpec((1,) + a.shape[1:], lambda b: (b, 0, 0))
    return pl.pallas_call(
        functools.partial(_nsa_sample_a_kernel, past_len=past_len, n_tok=n_tok, n_sel=n_sel, n_sel_pad=n_sel_pad),
        grid=(nb,),
        in_specs=[blk3(q_rows), blk3(g_rows), blk3(kc), blk3(vc), blk3(win_buf), blk3(new_win)],
        out_specs=[pl.BlockSpec((1, rows, KV_W), lambda b: (b, 0, 0)),
                   pl.BlockSpec((1, NSA_KV_HEADS * n_tok, n_sel_pad), lambda b: (b, 0, 0))],
        out_shape=[jax.ShapeDtypeStruct((nb, rows, KV_W), F32),
                   jax.ShapeDtypeStruct((nb, NSA_KV_HEADS * n_tok, n_sel_pad), F32)],
        compiler_params=_cparams(("arbitrary",)),
        name="nsa_sample_a",
    )(q_rows, g_rows, kc, vc, win_buf, new_win)


def _nsa_sample_b_kernel(pt_ref, cache_ref, q_ref, g_ref, sel_ref, seln_ref, ns_ref, ocw_ref, ex_ref, o_ref,
                         buf, sem, m_scr, l_scr, acc_scr, *, n_pages, n_steps, pps, n_tok):
    s = pl.program_id(1)
    slot = _stream_pages(pt_ref, cache_ref, buf, sem, n_pages, n_steps, pps, 2 * KV_W, False, True)
    q = q_ref[0]
    rows = q.shape[0]

    @pl.when(s == 0)
    def _():
        m_scr[...] = jnp.full(m_scr.shape, NEG, F32)
        l_scr[...] = jnp.zeros(l_scr.shape, F32)
        acc_scr[...] = jnp.zeros(acc_scr.shape, F32)

    def update(scores, msk, times_v):
        sc = jnp.where(msk, scores, NEG)
        m_old = m_scr[...]
        m_new = jnp.maximum(m_old, jnp.max(sc, axis=1, keepdims=True))
        p = jnp.exp2(sc - m_new) * msk.astype(F32)
        alpha = jnp.exp2(m_old - m_new)
        l_scr[...] = alpha * l_scr[...] + jnp.sum(p, axis=1, keepdims=True)
        acc_scr[...] = alpha * acc_scr[...] + times_v(p.astype(BF16))
        m_scr[...] = m_new

    chosen = _dot(sel_ref[0, 0], ex_ref[...]) > 0.5
    update(_dot(q, buf[slot, 0].astype(BF16)), chosen, lambda p: _dot_nt(p, buf[slot, 1].astype(BF16)))

    @pl.when(s == n_steps - 1)
    def _():
        t_row = lax.broadcasted_iota(jnp.int32, (rows, 1), 0) & (n_tok - 1)
        i2 = lax.broadcasted_iota(jnp.int32, (1, ns_ref.shape[1]), 1)
        msk = (seln_ref[0, 0][:, 0:1] > 0.5) & (i2 <= t_row) & (i2 < n_tok)
        update(_dot_nt(q, ns_ref[0, :, 0:KV_W].astype(BF16)), msk,
               lambda p: _dot(p, ns_ref[0, :, KV_W:2 * KV_W].astype(BF16)))
        o_s = acc_scr[...] / jnp.maximum(l_scr[...], 1e-30)
        o_ref[0] = ocw_ref[0] + g_ref[0][:, 1:2] * o_s


def _nsa_sample_b(page_table, cache, q_rows, g_rows, sel_steps, new_sel, ocw, *, n_tok, pps=PAGES_PER_STEP):
    nb, n_pages = page_table.shape
    n_steps = n_pages // pps
    rows = q_rows.shape[1]
    keys = pps * PAGE_SIZE
    expand = jnp.asarray(np.arange(LANES)[:, None] == (np.arange(keys)[None, :] // SEL_BLOCK), BF16)
    per_b = lambda a: pl.BlockSpec((1,) + a.shape[1:], lambda b, s, pt: (b, 0, 0))
    return pl.pallas_call(
        functools.partial(_nsa_sample_b_kernel, n_pages=n_pages, n_steps=n_steps, pps=pps, n_tok=n_tok),
        grid_spec=pltpu.PrefetchScalarGridSpec(
            num_scalar_prefetch=1,
            grid=(nb, n_steps),
            in_specs=[pl.BlockSpec(memory_space=pl.ANY),
                      per_b(q_rows), per_b(g_rows),
                      pl.BlockSpec((1, 1, rows, LANES), lambda b, s, pt: (b, s, 0, 0)),
                      pl.BlockSpec((1, 1, rows, LANES), lambda b, s, pt: (b, n_steps, 0, 0)),
                      per_b(new_sel), per_b(ocw),
                      pl.BlockSpec((LANES, keys), lambda b, s, pt: (0, 0))],
            out_specs=pl.BlockSpec((1, rows, KV_W), lambda b, s, pt: (b, 0, 0)),
            scratch_shapes=[pltpu.VMEM((2, 2, KV_W, keys), F32), pltpu.SemaphoreType.DMA((2,)),
                            pltpu.VMEM((rows, 1), F32), pltpu.VMEM((rows, 1), F32),
                            pltpu.VMEM((rows, KV_W), F32)]),
        out_shape=jax.ShapeDtypeStruct((nb, rows, KV_W), F32),
        compiler_params=_cparams(("arbitrary", "arbitrary")),
        name="nsa_sample_b",
    )(page_table, cache, q_rows, g_rows, sel_steps, sel_steps, new_sel, ocw, expand)


def _nsa_sample(page_table, cache, win_buf, cw, qpad, gates, kv4, kvw, pps=PAGES_PER_STEP):
    nb, ts, _ = qpad.shape
    past_len = page_table.shape[1] * PAGE_SIZE
    kc, vc = _compress(page_table, cache, *cw, pps=pps, transposed=True)
    rows = NSA_HEADS * ts
    q_rows = qpad.reshape(nb, ts, NSA_HEADS, LANES).transpose(0, 2, 1, 3).reshape(nb, rows, LANES)
    g_rows = gates[:, :, :3 * NSA_HEADS].reshape(nb, ts, NSA_HEADS, 3).transpose(0, 2, 1, 3)
    g_rows = jnp.pad(g_rows.reshape(nb, rows, 3), ((0, 0), (0, 0), (0, LANES - 3)))
    pad_rows = lambda a: jnp.pad(a, ((0, 0), (0, LANES - ts), (0, 0)))
    new_win = pad_rows(kvw)
    new_sel = pad_rows(kv4[:, :, 2 * KV_W:])
    ocw, sel = _nsa_sample_a(q_rows, g_rows, kc, vc, win_buf, new_win, past_len=past_len, n_tok=ts)
    n_steps = page_table.shape[1] // pps
    blk_per_step = pps * PAGE_SIZE // SEL_BLOCK
    n_past_blk = n_steps * blk_per_step
    sel_past = sel[:, :, :n_past_blk].reshape(nb, NSA_KV_HEADS, 1, ts, n_steps, blk_per_step)
    sel_past = jnp.broadcast_to(sel_past, (nb, NSA_KV_HEADS, NSA_GROUP, ts, n_steps, blk_per_step))
    sel_past = sel_past.transpose(0, 4, 1, 2, 3, 5).reshape(nb, n_steps, rows, blk_per_step)
    sel_past = jnp.pad(sel_past, ((0, 0), (0, 0), (0, 0), (0, LANES - blk_per_step)))
    sel_new = jnp.pad(sel[:, :, n_past_blk:], ((0, 0), (0, 0), (0, LANES)))[:, :, :LANES]
    sel_new = sel_new.reshape(nb, NSA_KV_HEADS, 1, ts, LANES)
    sel_new = jnp.broadcast_to(sel_new, (nb, NSA_KV_HEADS, NSA_GROUP, ts, LANES)).reshape(nb, 1, rows, LANES)
    sel_steps = jnp.concatenate([sel_past, sel_new], axis=1).astype(BF16)
    o_rows = _nsa_sample_b(page_table, cache, q_rows, g_rows, sel_steps, new_sel, ocw, n_tok=ts, pps=pps)
    return o_rows.reshape(nb, NSA_HEADS, ts, LANES).transpose(0, 2, 1, 3).reshape(nb, ts, QPAD_W).astype(BF16)


def _outproj_kernel(x_ref, hg_ref, nsa_ref, g1_ref, sc2_ref, sh2_ref, fn_ref, wo1_ref, wo2_ref,
                    x1_ref, h2_ref, *, tm):
    mix = _dot(hg_ref[0], wo1_ref[...]) + _dot(nsa_ref[0], wo2_ref[...])
    x1 = x_ref[0] + g1_ref[0] * mix
    x1_ref[0] = x1
    h2 = _rms(x1, fn_ref[...]) * (1.0 + sc2_ref[0]) + sh2_ref[0]
    _store_tok_tiles(h2_ref, h2, tm)


def _outproj(x, hg_out, nsa, gate1, scale2, shift2, ffn_norm, wo_hg, wo_nsa, tm):
    nb, t, _ = x.shape
    nt = t // tm
    mod_spec = _mod_spec(gate1, tm)
    tile = lambda w: pl.BlockSpec((1, tm, w), lambda b, i: (b, i, 0))
    return pl.pallas_call(
        functools.partial(_outproj_kernel, tm=tm),
        grid=(nb, nt),
        in_specs=[tile(D_MODEL), tile(HG_WIDTH), tile(QPAD_W), mod_spec, mod_spec, mod_spec,
                  pl.BlockSpec((1, D_MODEL), lambda b, i: (0, 0)),
                  pl.BlockSpec((HG_WIDTH, D_MODEL), lambda b, i: (0, 0)),
                  pl.BlockSpec((QPAD_W, D_MODEL), lambda b, i: (0, 0))],
        out_specs=[tile(D_MODEL), pl.BlockSpec((tm * TOK_ROWS, LANES), lambda b, i: (b * nt + i, 0))],
        out_shape=[jax.ShapeDtypeStruct((nb, t, D_MODEL), F32),
                   jax.ShapeDtypeStruct((nb * t * TOK_ROWS, LANES), F32)],
        compiler_params=_cparams(("arbitrary", "arbitrary")),
        name="out_proj",
    )(x, hg_out, nsa, gate1, scale2, shift2, ffn_norm.reshape(1, -1), wo_hg, wo_nsa)


def _split_w_out(w_out):
    wo_hg = w_out[:HG_WIDTH].astype(BF16)
    wn = w_out[HG_WIDTH:].reshape(NSA_HEADS, NSA_HEAD_DIM, D_MODEL)
    z = jnp.zeros_like(wn)
    grp = (jnp.arange(NSA_HEADS) // NSA_GROUP)[:, None, None]
    wn_pad = jnp.where(grp == 0, jnp.concatenate([wn, z], axis=1), jnp.concatenate([z, wn], axis=1))
    return wo_hg, wn_pad.reshape(QPAD_W, D_MODEL).astype(BF16)


def _router_kernel(h_ref, wr_ref, b_ref, e_ref, w_ref, r_ref, cnt_ref, run_scr, *, tm):
    @pl.when(pl.program_id(0) == 0)
    def _():
        run_scr[...] = jnp.zeros(run_scr.shape, F32)

    x = _load_tok_tiles(h_ref, tm).astype(BF16)
    scores = jax.nn.sigmoid(_dot(x, wr_ref[...]))
    biased = scores + b_ref[...]
    lane_i = lax.broadcasted_iota(jnp.int32, (tm, N_EXPERTS), 1)
    lane = lane_i.astype(F32)
    grp_of_lane = lax.shift_right_logical(lane_i, 5)
    per_group = N_EXPERTS // N_GROUPS

    gcol = lax.broadcasted_iota(jnp.int32, (tm, LANES), 1)
    gs = jnp.full((tm, LANES), 2.0 * NEG, F32)
    for g in range(N_GROUPS):
        mg = jnp.where(grp_of_lane == g, biased, NEG)
        m1 = jnp.max(mg, axis=1, keepdims=True)
        i1 = jnp.min(jnp.where(mg == m1, lane, 1e9), axis=1, keepdims=True)
        m2 = jnp.max(jnp.where(lane == i1, NEG, mg), axis=1, keepdims=True)
        gs = jnp.where(gcol == g, m1 + m2, gs)
    gsel = _topk_mask(gs, TOPK_GROUPS).astype(BF16)
    spread = (lax.broadcasted_iota(jnp.int32, (LANES, N_EXPERTS), 0)
              == lax.shift_right_logical(lax.broadcasted_iota(jnp.int32, (LANES, N_EXPERTS), 1), 5)).astype(BF16)
    v = jnp.where(_dot(gsel, spread) > 0.5, biased, NEG)

    onehot = jnp.zeros((tm, N_EXPERTS), F32)
    idxs, wts = [], []
    wsum = jnp.zeros((tm, 1), F32)
    for _ in range(TOP_K):
        m = jnp.max(v, axis=1, keepdims=True)
        idx = jnp.min(jnp.where(v == m, lane, 1e9), axis=1, keepdims=True)
        pick = lane == idx
        wk = jnp.sum(jnp.where(pick, scores, 0.0), axis=1, keepdims=True)
        onehot = jnp.where(pick, 1.0, onehot)
        v = jnp.where(pick, 3.0 * NEG, v)
        idxs.append(idx)
        wts.append(wk)
        wsum = wsum + wk

    earlier = (lax.broadcasted_iota(jnp.int32, (tm, tm), 0) > lax.broadcasted_iota(jnp.int32, (tm, tm), 1))
    before = _dot(earlier.astype(BF16), onehot.astype(BF16)) + run_scr[...]
    e_out = jnp.zeros((tm, LANES), jnp.int32)
    r_out = jnp.zeros((tm, LANES), jnp.int32)
    w_out = jnp.zeros((tm, LANES), F32)
    for k in range(TOP_K):
        rk = jnp.sum(jnp.where(lane == idxs[k], before, 0.0), axis=1, keepdims=True)
        e_out = jnp.where(gcol == k, idxs[k].astype(jnp.int32), e_out)
        r_out = jnp.where(gcol == k, rk.astype(jnp.int32), r_out)
        w_out = jnp.where(gcol == k, wts[k] / wsum * ROUTED_SCALE, w_out)
    e_ref[...] = e_out
    r_ref[...] = r_out
    w_ref[...] = w_out
    run_scr[...] = run_scr[...] + jnp.sum(onehot, axis=0, keepdims=True)
    cnt_ref[...] = run_scr[...]


def _router(h2, w_router, bias, n_tok, tm=ROUTER_TM):
    tile = pl.BlockSpec((tm, LANES), lambda i: (i, 0))
    return pl.pallas_call(
        functools.partial(_router_kernel, tm=tm),
        grid=(n_tok // tm,),
        in_specs=[pl.BlockSpec((tm * TOK_ROWS, LANES), lambda i: (i, 0)),
                  pl.BlockSpec((D_MODEL, N_EXPERTS), lambda i: (0, 0)),
                  pl.BlockSpec((1, N_EXPERTS), lambda i: (0, 0))],
        out_specs=[tile, tile, tile, pl.BlockSpec((1, N_EXPERTS), lambda i: (0, 0))],
        out_shape=[jax.ShapeDtypeStruct((n_tok, LANES), jnp.int32),
                   jax.ShapeDtypeStruct((n_tok, LANES), F32),
                   jax.ShapeDtypeStruct((n_tok, LANES), jnp.int32),
                   jax.ShapeDtypeStruct((1, N_EXPERTS), F32)],
        scratch_shapes=[pltpu.VMEM((1, N_EXPERTS), F32)],
        compiler_params=_cparams(("arbitrary",)),
        name="moe_router",
    )(h2, w_router, bias)


def _dest_kernel(e_ref, r_ref, st_ref, d_ref):
    e = e_ref[...]
    tm = e.shape[0]
    lane = lax.broadcasted_iota(jnp.int32, (tm, N_EXPERTS), 1)
    col = lax.broadcasted_iota(jnp.int32, (tm, LANES), 1)
    st = st_ref[...]
    out = r_ref[...]
    for k in range(TOP_K):
        sk = jnp.sum(jnp.where(lane == e[:, k:k + 1], st, 0.0), axis=1, keepdims=True)
        out = jnp.where(col == k, out + sk.astype(jnp.int32), out)
    d_ref[...] = out


def _moe_dest(top_e, rank, starts, tm=ROUTER_TM):
    n_tok = top_e.shape[0]
    tile = pl.BlockSpec((tm, LANES), lambda i: (i, 0))
    return pl.pallas_call(
        _dest_kernel,
        grid=(n_tok // tm,),
        in_specs=[tile, tile, pl.BlockSpec((1, N_EXPERTS), lambda i: (0, 0))],
        out_specs=tile,
        out_shape=jax.ShapeDtypeStruct((n_tok, LANES), jnp.int32),
        compiler_params=_cparams(("arbitrary",)),
        name="moe_dest",
    )(top_e, rank, starts.astype(F32).reshape(1, -1))


def _moe_layout(counts, n_pairs):
    padded = (counts + MOE_BM - 1) // MOE_BM * MOE_BM
    pad_end = jnp.cumsum(padded)
    starts = pad_end - padded
    n_blocks = (n_pairs + N_EXPERTS * (MOE_BM - 1)) // MOE_BM
    blk = jnp.arange(n_blocks, dtype=jnp.int32)
    used = blk * MOE_BM < pad_end[-1]
    blk_run = jnp.minimum(blk, pad_end[-1] // MOE_BM - 1).astype(jnp.int32)
    e_of = jnp.searchsorted(pad_end, blk_run * MOE_BM, side="right").astype(jnp.int32)
    e_of = jnp.minimum(e_of, N_EXPERTS - 1)
    shifted = jnp.concatenate([jnp.full((1,), -1, jnp.int32), e_of[:-1]])
    fresh = (used & (e_of != shifted)).astype(jnp.int32)
    return (starts.astype(jnp.int32), (starts + counts).astype(jnp.int32),
            (blk_run, e_of, fresh, used.astype(jnp.int32)), n_blocks)


def _dispatch_kernel(dest_ref, h_ref, xs_ref, sem, *, tm):
    n_pairs = tm * TOP_K

    def row_copy(src_tok, dst_row):
        return pltpu.make_async_copy(
            h_ref.at[pl.ds(pl.multiple_of(src_tok * TOK_ROWS, TOK_ROWS), TOK_ROWS), :],
            xs_ref.at[pl.ds(pl.multiple_of(dst_row * TOK_ROWS, TOK_ROWS), TOK_ROWS), :], sem)

    def issue(i, carry):
        for k in range(TOP_K):
            row_copy(i, dest_ref[0, 0, i * TOP_K + k]).start(priority=k % 2)
        return carry

    def drain(i, carry):
        for _ in range(TOP_K):
            row_copy(0, 0).wait()
        return carry

    lax.fori_loop(0, tm, issue, 0)
    lax.fori_loop(0, tm, drain, 0)


def _dispatch(dest_tiles, h2, n_pairs, tm=ROUTER_TM):
    n_tiles = dest_tiles.shape[0]
    return pl.pallas_call(
        functools.partial(_dispatch_kernel, tm=tm),
        grid=(n_tiles,),
        in_specs=[pl.BlockSpec((1, 1, tm * TOP_K), lambda i: (i, 0, 0), memory_space=pltpu.SMEM),
                  pl.BlockSpec((tm * TOK_ROWS, LANES), lambda i: (i, 0))],
        out_specs=pl.BlockSpec(memory_space=pl.ANY),
        out_shape=jax.ShapeDtypeStruct((n_pairs * TOK_ROWS, LANES), F32),
        scratch_shapes=[pltpu.SemaphoreType.DMA(())],
        compiler_params=_cparams(("arbitrary",)),
        name="moe_dispatch",
    )(dest_tiles, h2)


GMM_SLOTS = 3


def _gmm_kernel(blk_ref, e_ref, fresh_ref, used_ref, end_ref,
                xs_ref, wg_ref, wu_ref, wd_ref, ys_ref, wg_bf, wu_bf, wd_bf, xbuf, sem, *, bm):
    i = pl.program_id(0)
    n = pl.num_programs(0)

    def fetch(j):
        rows = pl.ds(pl.multiple_of(blk_ref[j] * (bm * TOK_ROWS), bm * TOK_ROWS), bm * TOK_ROWS)
        return pltpu.make_async_copy(xs_ref.at[rows, :], xbuf.at[j % GMM_SLOTS], sem.at[j % GMM_SLOTS])

    @pl.when(i == 0)
    def _():
        for j in range(GMM_SLOTS - 1):
            @pl.when((j < n) & (used_ref[jnp.minimum(j, n - 1)] == 1))
            def _():
                fetch(j).start()

    ahead = jnp.minimum(i + GMM_SLOTS - 1, n - 1)

    @pl.when((i + GMM_SLOTS - 1 < n) & (used_ref[ahead] == 1))
    def _():
        fetch(ahead).start()

    @pl.when(fresh_ref[i] == 1)
    def _():
        wg_bf[...] = wg_ref[0].astype(BF16)
        wu_bf[...] = wu_ref[0].astype(BF16)
        wd_bf[...] = wd_ref[0].astype(BF16)

    @pl.when(used_ref[i] == 1)
    def _():
        fetch(i).wait()
        row = blk_ref[i] * bm + lax.broadcasted_iota(jnp.int32, (bm, 1), 0)
        real = row < end_ref[e_ref[i]]
        x = jnp.where(real, _load_tok_tiles(xbuf.at[i % GMM_SLOTS], bm), 0.0).astype(BF16)
        hid = (_silu(_dot(x, wg_bf[...])) * _dot(x, wu_bf[...])).astype(BF16)
        _store_tok_tiles(ys_ref, _dot(hid, wd_bf[...]), bm)


def _moe_gmm(blocks, row_end, xs_sorted, w_gate, w_up, w_down, bm=MOE_BM):
    blk_run, e_of, fresh, used = blocks
    rows = pl.BlockSpec((bm * TOK_ROWS, LANES), lambda i, blk, e, *_: (blk[i], 0))
    wspec = lambda a: pl.BlockSpec((1,) + a.shape[1:], lambda i, blk, e, *_: (e[i], 0, 0))
    return pl.pallas_call(
        functools.partial(_gmm_kernel, bm=bm),
        grid_spec=pltpu.PrefetchScalarGridSpec(
            num_scalar_prefetch=5,
            grid=(blk_run.shape[0],),
            in_specs=[pl.BlockSpec(memory_space=pl.ANY), wspec(w_gate), wspec(w_up), wspec(w_down)],
            out_specs=rows,
            scratch_shapes=[pltpu.VMEM(w_gate.shape[1:], BF16), pltpu.VMEM(w_up.shape[1:], BF16),
                            pltpu.VMEM(w_down.shape[1:], BF16),
                            pltpu.VMEM((GMM_SLOTS, bm * TOK_ROWS, LANES), F32),
                            pltpu.SemaphoreType.DMA((GMM_SLOTS,))]),
        out_shape=jax.ShapeDtypeStruct(xs_sorted.shape, F32),
        compiler_params=_cparams(("arbitrary",)),
        name="moe_experts",
    )(blk_run, e_of, fresh, used, row_end, xs_sorted, w_gate, w_up, w_down)


def _combine_kernel(dest_ref, dnext_ref, w_ref, x1_ref, h_ref, g2_ref, wsg_ref, wsu_ref, wsd_ref, fn_ref, ys_ref,
                    o_ref, gbuf, sem, routed_scr, *, tm):
    n = pl.program_id(0) * pl.num_programs(1) + pl.program_id(1)
    total = pl.num_programs(0) * pl.num_programs(1)
    slot = n % 2

    def row_copy(src_row, p, sl):
        return pltpu.make_async_copy(
            ys_ref.at[pl.ds(pl.multiple_of(src_row * TOK_ROWS, TOK_ROWS), TOK_ROWS), :],
            gbuf.at[sl, pl.ds(pl.multiple_of(p * TOK_ROWS, TOK_ROWS), TOK_ROWS), :], sem.at[sl])

    def gather(rows_ref, sl):
        def issue(i, carry):
            for k in range(TOP_K):
                p = i * TOP_K + k
                row_copy(rows_ref[0, 0, p], p, sl).start(priority=k % 2)
            return carry
        lax.fori_loop(0, tm, issue, 0)

    def drain(i, carry):
        for _ in range(TOP_K):
            row_copy(0, 0, slot).wait()
        return carry

    @pl.when(n == 0)
    def _():
        gather(dest_ref, slot)

    @pl.when(n + 1 < total)
    def _():
        gather(dnext_ref, 1 - slot)

    h = _load_tok_tiles(h_ref, tm).astype(BF16)
    hid = (_silu(_dot(h, wsg_ref[...])) * _dot(h, wsu_ref[...])).astype(BF16)
    shared = _dot(hid, wsd_ref[...])
    lax.fori_loop(0, tm, drain, 0)

    def weigh(t, carry):
        acc = jnp.zeros((TOK_ROWS, LANES), F32)
        for k in range(TOP_K):
            p = t * TOP_K + k
            acc = acc + w_ref[0, 0, p] * gbuf[slot, pl.ds(pl.multiple_of(p * TOK_ROWS, TOK_ROWS), TOK_ROWS), :]
        routed_scr[pl.ds(pl.multiple_of(t * TOK_ROWS, TOK_ROWS), TOK_ROWS), :] = acc
        return carry

    lax.fori_loop(0, tm, weigh, 0)
    x2 = x1_ref[0] + g2_ref[0] * (_load_tok_tiles(routed_scr, tm) + shared)
    o_ref[0] = _rms(x2, fn_ref[...])


def _combine(dest_tiles, w_tiles, x1, h2, gate2, shared, fnorm, ys_sorted, tile0, tm=COMBINE_TM):
    nb, t, _ = x1.shape
    nt = t // tm
    flat = lambda b, i: tile0 + b * nt + i
    nxt = lambda b, i: tile0 + jnp.minimum(b * nt + i + 1, nb * nt - 1)
    mod_spec = _mod_spec(gate2, tm)
    const = lambda a: pl.BlockSpec(a.shape, lambda b, i: (0, 0))
    return pl.pallas_call(
        functools.partial(_combine_kernel, tm=tm),
        grid=(nb, nt),
        in_specs=[pl.BlockSpec((1, 1, tm * TOP_K), lambda b, i: (flat(b, i), 0, 0), memory_space=pltpu.SMEM),
                  pl.BlockSpec((1, 1, tm * TOP_K), lambda b, i: (nxt(b, i), 0, 0), memory_space=pltpu.SMEM),
                  pl.BlockSpec((1, 1, tm * TOP_K), lambda b, i: (flat(b, i), 0, 0), memory_space=pltpu.SMEM),
                  pl.BlockSpec((1, tm, D_MODEL), lambda b, i: (b, i, 0)),
                  pl.BlockSpec((tm * TOK_ROWS, LANES), lambda b, i: (flat(b, i), 0)),
                  mod_spec, const(shared[0]), const(shared[1]), const(shared[2]), const(fnorm),
                  pl.BlockSpec(memory_space=pl.ANY)],
        out_specs=pl.BlockSpec((1, tm, D_MODEL), lambda b, i: (b, i, 0)),
        out_shape=jax.ShapeDtypeStruct((nb, t, D_MODEL), F32),
        scratch_shapes=[pltpu.VMEM((2, tm * TOP_K * TOK_ROWS, LANES), F32), pltpu.SemaphoreType.DMA((2,)),
                        pltpu.VMEM((tm * TOK_ROWS, LANES), F32)],
        compiler_params=_cparams(("arbitrary", "arbitrary")),
        name="moe_combine",
    )(dest_tiles, dest_tiles, w_tiles, x1, h2, gate2, *shared, fnorm, ys_sorted)


def kernel(x_prompt, x_sample, c_prompt, c_sample, cache_nsa_kv, cache_win_kv, state_hgrn, page_table,
           attn_norm, ffn_norm, final_norm, hg_norm, w_ada, b_ada, w_in, hg_lb,
           cmp_pe, cmp_w1, cmp_b1, cmp_w2, w_out, w_router, router_bias,
           w_gate, w_up, w_down, ws_gate, ws_up, ws_down):
    nbp, t, _ = x_prompt.shape
    nbs, ts, _ = x_sample.shape
    ns = nbs * ts
    n_all = nbp * t + ns
    past_len = page_table.shape[1] * PAGE_SIZE

    c_all = jnp.concatenate([c_prompt, c_sample], axis=0)
    c_all = jnp.pad(c_all, ((0, -c_all.shape[0] % SUBLANES), (0, 0)))
    mod = _ada(c_all, w_ada[0], b_ada[0])
    modp = mod[:nbp].reshape(nbp, 1, 6, D_MODEL)
    mods = jnp.repeat(mod[nbp:nbp + nbs].reshape(nbs, 1, 6, D_MODEL), ts, axis=1).reshape(1, ns, 6, D_MODEL)

    w_pad = _pad_w_in(w_in[0])
    cw = _compress_weights(cmp_pe[0], cmp_w1[0], cmp_b1[0], cmp_w2[0])
    wo_hg, wo_nsa = _split_w_out(w_out[0])

    hg, qpad, kv4, kvw, gates, kvsel, kvwb = _inproj(
        x_prompt, modp[:, :, 1], modp[:, :, 0], attn_norm[0], w_pad, 512)
    hg_out, hg_state_p = _hgrn(hg, hg_lb, jnp.zeros((nbp, HG_HEADS, HG_DK, HG_DK), F32), hg_norm[0],
                               256, HG_CHUNK)
    n_pages_p = t // PAGE_SIZE
    ptp = jnp.arange(nbp * n_pages_p, dtype=jnp.int32).reshape(nbp, n_pages_p)
    kc, vc = _compress(ptp, kv4.reshape(nbp * n_pages_p, PAGE_SIZE, 4 * KV_W), *cw)
    nsa = _nsa_prompt(qpad, gates, kc, vc, kvsel, kvwb)
    x1p, h2p = _outproj(x_prompt, hg_out, nsa, modp[:, :, 2], modp[:, :, 4], modp[:, :, 3],
                        ffn_norm[0], wo_hg, wo_nsa, 512)

    xs = x_sample.reshape(1, ns, D_MODEL)
    hg_s, qpad_s, kv4_s, kvw_s, gates_s, _, _ = _inproj(
        xs, mods[:, :, 1], mods[:, :, 0], attn_norm[0], w_pad, ns)
    hg_out_s, hg_state_s = _hgrn(hg_s.reshape(nbs, ts, 4 * HG_WIDTH), hg_lb, state_hgrn[0], hg_norm[0], ts, ts)
    cache = cache_nsa_kv[0].transpose(0, 2, 3, 4, 1).reshape(-1, 4 * KV_W, PAGE_SIZE)
    win_buf = cache_win_kv[0].reshape(nbs, -1, 2 * KV_W)
    nsa_s = _nsa_sample(page_table, cache, win_buf, cw, qpad_s.reshape(nbs, ts, QPAD_W),
                        gates_s.reshape(nbs, ts, GZ_PAD), kv4_s.reshape(nbs, ts, 4 * KV_W),
                        kvw_s.reshape(nbs, ts, 2 * KV_W)).reshape(1, ns, QPAD_W)
    x1s, h2s = _outproj(xs, hg_out_s.reshape(1, ns, HG_WIDTH), nsa_s, mods[:, :, 2], mods[:, :, 4], mods[:, :, 3],
                        ffn_norm[0], wo_hg, wo_nsa, ns)

    h2 = jnp.concatenate([h2p, h2s], axis=0)
    top_e, top_w, rank, counts = _router(h2, w_router[0].astype(BF16), router_bias[0].reshape(1, -1), n_all)
    starts, row_end, blocks, n_blocks = _moe_layout(counts[0].astype(jnp.int32), n_all * TOP_K)
    dest = _moe_dest(top_e, rank, starts)[:, :TOP_K].reshape(-1)
    xs_sorted = _dispatch(dest.reshape(n_all // ROUTER_TM, 1, ROUTER_TM * TOP_K), h2, n_blocks * MOE_BM)
    ys_sorted = _moe_gmm(blocks, row_end, xs_sorted, w_gate[0], w_up[0], w_down[0])
    dest_c = dest.reshape(n_all // COMBINE_TM, 1, COMBINE_TM * TOP_K)
    w_c = top_w[:, :TOP_K].reshape(n_all // COMBINE_TM, 1, COMBINE_TM * TOP_K)
    shared = (ws_gate[0].astype(BF16), ws_up[0].astype(BF16), ws_down[0].astype(BF16))
    fnorm = final_norm.reshape(1, -1)
    y_prompt = _combine(dest_c, w_c, x1p, h2, modp[:, :, 5], shared, fnorm, ys_sorted, 0)
    y_sample = _combine(dest_c, w_c, x1s, h2, mods[:, :, 5], shared, fnorm, ys_sorted, nbp * t // COMBINE_TM)

    wb = win_buf.shape[1]
    win_p = kvw[:, t - min(WINDOW, t):]
    win_s = jnp.concatenate([win_buf, kvw_s.reshape(nbs, ts, 2 * KV_W)], axis=1)[:, -wb:]
    kv_shape = (4, NSA_KV_HEADS, NSA_HEAD_DIM)
    win_shape = (2, NSA_KV_HEADS, NSA_HEAD_DIM)
    return (y_prompt,
            y_sample.reshape(nbs, ts, D_MODEL),
            kv4.reshape(1, nbp, t, *kv_shape),
            win_p.reshape(1, nbp, -1, *win_shape),
            hg_state_p[None],
            kv4_s.reshape(1, nbs, ts, *kv_shape),
            win_s.reshape(1, nbs, wb, *win_shape),
            hg_state_s[None])
```

```python
import functools

import jax
import jax.numpy as jnp
import numpy as np
from jax import lax
from jax.experimental import pallas as pl
from jax.experimental.pallas import tpu as pltpu

F32 = jnp.float32
BF16 = jnp.bfloat16

D_MODEL = 1024
HG_WIDTH = 512
HG_HEADS = 4
HG_DK = 128
HG_CHUNK = 32
NSA_WIDTH = 512
NSA_HEADS = 8
NSA_HEAD_DIM = 64
NSA_KV_HEADS = 2
NSA_GROUP = 4
KV_W = 128
CMP_BLOCK = 32
CMP_STRIDE = 16
CMP_HIDDEN = 256
SEL_BLOCK = 64
SEL_TOPN = 16
WINDOW = 512
Q_BLOCK = 128
N_EXPERTS = 256
TOP_K = 8
N_GROUPS = 8
TOPK_GROUPS = 4
MOE_D_FF = 256
ROUTED_SCALE = 2.5
RMS_EPS = 1e-6
PAGE_SIZE = 128
IN_COLS = 4 * HG_WIDTH + NSA_WIDTH + 4 * KV_W + 2 * KV_W + 3 * NSA_HEADS

LANES = 128
SUBLANES = 8
TOK_ROWS = D_MODEL // LANES
VMEM_LIMIT = 56 * 1024 * 1024

QPAD_W = NSA_HEADS * LANES
GZ_PAD = LANES
INP_COLS = 4 * HG_WIDTH + QPAD_W + 4 * KV_W + 2 * KV_W + GZ_PAD

NEG = -1e30
PAGES_PER_STEP = 32
MOE_BM = 256
ROUTER_TM = 256
COMBINE_TM = 128


def _cparams(sem):
    return pltpu.CompilerParams(dimension_semantics=sem, vmem_limit_bytes=VMEM_LIMIT)


def _dot(a, b):
    return jnp.dot(a, b, preferred_element_type=F32)


def _dot_nt(a, b):
    return lax.dot_general(a, b, (((1,), (1,)), ((), ())), preferred_element_type=F32)


def _dot_tn(a, b):
    return lax.dot_general(a, b, (((0,), (0,)), ((), ())), preferred_element_type=F32)


def _rms(x, g):
    return x * lax.rsqrt(jnp.mean(x * x, axis=-1, keepdims=True) + RMS_EPS) * g


def _silu(x):
    return x * jax.nn.sigmoid(x)


Q_SCALE = NSA_HEAD_DIM ** -0.5 * 1.4426950408889634


def _masked_softmax(s, valid):
    s = jnp.where(valid, s, NEG)
    m = jnp.max(s, axis=1, keepdims=True)
    p = jnp.exp2(s - m) * valid.astype(F32)
    return p / jnp.maximum(jnp.sum(p, axis=1, keepdims=True), 1e-30)


def _topk_mask(v, k):
    lane = lax.broadcasted_iota(jnp.int32, v.shape, 1).astype(F32)
    sel = jnp.zeros(v.shape, F32)
    for _ in range(k):
        m = jnp.max(v, axis=1, keepdims=True)
        idx = jnp.min(jnp.where(v == m, lane, 1e9), axis=1, keepdims=True)
        pick = lane == idx
        sel = jnp.where(pick, 1.0, sel)
        v = jnp.where(pick, 3.0 * NEG, v)
    return sel


def _mod_spec(mod, tm):
    if mod.shape[1] == 1:
        return pl.BlockSpec((1, 1, D_MODEL), lambda b, i: (b, 0, 0))
    return pl.BlockSpec((1, tm, D_MODEL), lambda b, i: (b, i, 0))


def _load_tok_tiles(ref, n_tok):
    return jnp.concatenate([ref[pl.ds(s, n_tok, stride=TOK_ROWS), :] for s in range(TOK_ROWS)], axis=1)


def _store_tok_tiles(ref, val, n_tok):
    for s in range(TOK_ROWS):
        ref[pl.ds(s, n_tok, stride=TOK_ROWS), :] = val[:, s * LANES:(s + 1) * LANES]


def _ada_kernel(c_ref, w_ref, b_ref, o_ref):
    s = _silu(c_ref[...]).astype(BF16)
    o_ref[...] = _dot(s, w_ref[...].astype(BF16)) + b_ref[...]


def _ada(c_all, w_ada, b_ada):
    n = c_all.shape[0]
    return pl.pallas_call(
        _ada_kernel,
        grid=(6,),
        in_specs=[pl.BlockSpec((n, D_MODEL), lambda j: (0, 0)),
                  pl.BlockSpec((D_MODEL, D_MODEL), lambda j: (0, j)),
                  pl.BlockSpec((1, D_MODEL), lambda j: (0, j))],
        out_specs=pl.BlockSpec((n, D_MODEL), lambda j: (0, j)),
        out_shape=jax.ShapeDtypeStruct((n, 6 * D_MODEL), F32),
        compiler_params=_cparams(("arbitrary",)),
        name="ada_mod",
    )(c_all, w_ada, b_ada.reshape(1, -1))


def _inproj_kernel(x_ref, sc_ref, sh_ref, g_ref, w_ref,
                   hg_ref, q_ref, kv4_ref, kvw_ref, gate_ref, kvsel_ref, kvwb_ref):
    h = _rms(x_ref[0], g_ref[...]) * (1.0 + sc_ref[0]) + sh_ref[0]
    z = _dot(h.astype(BF16), w_ref[...])
    c0 = 4 * HG_WIDTH
    hg_ref[0] = z[:, :c0]
    q_ref[0] = (z[:, c0:c0 + QPAD_W] * Q_SCALE).astype(BF16)
    c1 = c0 + QPAD_W
    kv4 = z[:, c1:c1 + 4 * KV_W]
    kv4_ref[0] = kv4
    kvsel_ref[0] = kv4[:, 2 * KV_W:].astype(BF16)
    c2 = c1 + 4 * KV_W
    kvw = z[:, c2:c2 + 2 * KV_W]
    kvw_ref[0] = kvw
    kvwb_ref[0] = kvw.astype(BF16)
    gate_ref[0] = jax.nn.sigmoid(z[:, c2 + 2 * KV_W:])


def _inproj(x, scale, shift, g_norm, w_pad, tm):
    nb, t, _ = x.shape
    mod_spec = _mod_spec(scale, tm)
    widths = [(4 * HG_WIDTH, F32), (QPAD_W, BF16), (4 * KV_W, F32), (2 * KV_W, F32), (GZ_PAD, F32),
              (2 * KV_W, BF16), (2 * KV_W, BF16)]
    return pl.pallas_call(
        _inproj_kernel,
        grid=(nb, t // tm),
        in_specs=[pl.BlockSpec((1, tm, D_MODEL), lambda b, i: (b, i, 0)),
                  mod_spec, mod_spec,
                  pl.BlockSpec((1, D_MODEL), lambda b, i: (0, 0)),
                  pl.BlockSpec((D_MODEL, INP_COLS), lambda b, i: (0, 0))],
        out_specs=[pl.BlockSpec((1, tm, w), lambda b, i: (b, i, 0)) for w, _ in widths],
        out_shape=[jax.ShapeDtypeStruct((nb, t, w), dt) for w, dt in widths],
        compiler_params=_cparams(("arbitrary", "arbitrary")),
        name="in_proj",
    )(x, scale, shift, g_norm.reshape(1, -1), w_pad)


def _pad_w_in(w_in):
    c0 = 4 * HG_WIDTH
    wq = w_in[:, c0:c0 + NSA_WIDTH].reshape(D_MODEL, NSA_HEADS, NSA_HEAD_DIM)
    zeros = jnp.zeros_like(wq)
    lo = jnp.concatenate([wq, zeros], axis=-1)
    hi = jnp.concatenate([zeros, wq], axis=-1)
    grp = (jnp.arange(NSA_HEADS) // NSA_GROUP)[None, :, None]
    wq_pad = jnp.where(grp == 0, lo, hi).reshape(D_MODEL, QPAD_W)
    c1 = c0 + NSA_WIDTH
    rest = w_in[:, c1:c1 + 6 * KV_W]
    gz = jnp.pad(w_in[:, c1 + 6 * KV_W:], ((0, 0), (0, GZ_PAD - 3 * NSA_HEADS)))
    return jnp.concatenate([w_in[:, :c0], wq_pad, rest, gz], axis=1).astype(BF16)


def _hgrn_kernel(q_ref, f_ref, v_ref, gt_ref, lb_ref, s0_ref, gn_ref, o_ref, s_out_ref, st_scr,
                 *, chunk, n_chunks):
    i = pl.program_id(1)

    @pl.when(i == 0)
    def _():
        for h in range(HG_HEADS):
            st_scr[h] = s0_ref[0, h].T

    lbr = lb_ref[...]
    e = jnp.exp(lbr - jnp.max(lbr, axis=0, keepdims=True))
    lb_all = e[0:1] / jnp.sum(e, axis=0, keepdims=True)
    row = lax.broadcasted_iota(jnp.int32, (chunk, HG_DK), 0)
    causal = (lax.broadcasted_iota(jnp.int32, (chunk, chunk), 0)
              >= lax.broadcasted_iota(jnp.int32, (chunk, chunk), 1))
    st = [st_scr[h] for h in range(HG_HEADS)]
    for c in range(n_chunks):
        sl = pl.ds(c * chunk, chunk)
        for h in range(HG_HEADS):
            hs = slice(h * HG_DK, (h + 1) * HG_DK)
            lb = lb_all[:, hs]
            z = f_ref[0, sl, hs]
            log_f = jnp.log(lb + (1.0 - lb) * jax.nn.sigmoid(z))
            kk = (1.0 - lb) * jax.nn.sigmoid(-z)
            a = log_f
            s = 1
            while s < chunk:
                a = a + jnp.where(row >= s, pltpu.roll(a, s, 0), 0.0)
                s *= 2
            qt = (q_ref[0, sl, hs] * jnp.exp(a)).astype(BF16)
            kt = (kk * jnp.exp(-a)).astype(BF16)
            v = v_ref[0, sl, hs].astype(BF16)
            att = jnp.where(causal, _dot_nt(qt, kt), 0.0)
            o = _dot(att.astype(BF16), v) + _dot_nt(qt, st[h].astype(BF16))
            a_end = a[chunk - 1:chunk, :]
            kd = (kk * jnp.exp(a_end - a)).astype(BF16)
            st[h] = st[h] * jnp.exp(a_end) + _dot_tn(v, kd)
            o = _rms(o, gn_ref[...]) * _silu(gt_ref[0, sl, hs])
            o_ref[0, sl, hs] = o.astype(o_ref.dtype)
    for h in range(HG_HEADS):
        st_scr[h] = st[h]

    @pl.when(i == pl.num_programs(1) - 1)
    def _():
        for h in range(HG_HEADS):
            s_out_ref[0, h] = st[h].T


def _hgrn(hg, hg_lb, s0, g_norm, tc, chunk):
    nb, t, _ = hg.shape
    part = lambda k: pl.BlockSpec((1, tc, HG_WIDTH), lambda b, i: (b, i, k))
    st_spec = pl.BlockSpec((1, HG_HEADS, HG_DK, HG_DK), lambda b, i: (b, 0, 0, 0))
    return pl.pallas_call(
        functools.partial(_hgrn_kernel, chunk=chunk, n_chunks=tc // chunk),
        grid=(nb, t // tc),
        in_specs=[part(0), part(1), part(2), part(3),
                  pl.BlockSpec(hg_lb.shape, lambda b, i: (0, 0)),
                  st_spec,
                  pl.BlockSpec((1, HG_DK), lambda b, i: (0, 0))],
        out_specs=[pl.BlockSpec((1, tc, HG_WIDTH), lambda b, i: (b, i, 0)), st_spec],
        out_shape=[jax.ShapeDtypeStruct((nb, t, HG_WIDTH), BF16),
                   jax.ShapeDtypeStruct((nb, HG_HEADS, HG_DK, HG_DK), F32)],
        scratch_shapes=[pltpu.VMEM((HG_HEADS, HG_DK, HG_DK), F32)],
        compiler_params=_cparams(("arbitrary", "arbitrary")),
        name="hgrn2",
    )(hg, hg, hg, hg, hg_lb, s0, g_norm.reshape(1, -1))


def _gelu_tanh(x):
    return 0.5 * x * (1.0 + jnp.tanh(0.7978845608028654 * (x + 0.044715 * x * x * x)))


def _page_copies(pt_ref, cache_ref, buf, sem, b, s, slot, n_pages, pps, col0, tail, transposed):
    copies = []
    base = s * pps
    nxt = pt_ref[b, jnp.minimum(base + pps, n_pages - 1)]
    for br in range(2):
        cols = pl.ds(col0 + br * KV_W, KV_W)
        for i in range(pps):
            pg = pt_ref[b, base + i]
            if transposed:
                copies.append(pltpu.make_async_copy(
                    cache_ref.at[pg, cols, :],
                    buf.at[slot, br, :, pl.ds(i * PAGE_SIZE, PAGE_SIZE)], sem.at[slot]))
            else:
                copies.append(pltpu.make_async_copy(
                    cache_ref.at[pg, :, cols],
                    buf.at[slot, br, pl.ds(i * PAGE_SIZE, PAGE_SIZE), :], sem.at[slot]))
        if tail and transposed:
            copies.append(pltpu.make_async_copy(
                cache_ref.at[nxt, cols, :],
                buf.at[slot, br, :, pl.ds(pps * PAGE_SIZE, PAGE_SIZE)], sem.at[slot]))
        elif tail:
            copies.append(pltpu.make_async_copy(
                cache_ref.at[nxt, pl.ds(0, CMP_STRIDE), cols],
                buf.at[slot, br, pl.ds(pps * PAGE_SIZE, CMP_STRIDE), :], sem.at[slot]))
    return copies


def _stream_pages(pt_ref, cache_ref, buf, sem, n_pages, n_steps, pps, col0, tail, transposed):
    b = pl.program_id(0)
    s = pl.program_id(1)
    n = b * n_steps + s
    total = pl.num_programs(0) * n_steps
    slot = n % 2
    args = (n_pages, pps, col0, tail, transposed)

    @pl.when(n == 0)
    def _():
        for cp in _page_copies(pt_ref, cache_ref, buf, sem, b, s, slot, *args):
            cp.start()

    @pl.when(n + 1 < total)
    def _():
        n1 = n + 1
        for cp in _page_copies(pt_ref, cache_ref, buf, sem, n1 // n_steps, n1 % n_steps, 1 - slot, *args):
            cp.start()

    for cp in _page_copies(pt_ref, cache_ref, buf, sem, b, s, slot, *args):
        cp.wait()
    return slot


def _compress_kernel(pt_ref, cache_ref, pe_ref, w1_ref, b1_ref, w2_ref, kc_ref, vc_ref, buf, sem, *rowbuf,
                     n_pages, n_steps, pps, transposed):
    groups = pps * PAGE_SIZE // CMP_STRIDE
    slot = _stream_pages(pt_ref, cache_ref, buf, sem, n_pages, n_steps, pps, 0, True, transposed)

    if transposed:
        rows_ref, = rowbuf
        for br in range(2):
            for i in range(pps + 1):
                n_rows = PAGE_SIZE if i < pps else CMP_STRIDE
                page_t = buf[slot, br, :, i * PAGE_SIZE:(i + 1) * PAGE_SIZE]
                rows_ref[br, i * PAGE_SIZE:i * PAGE_SIZE + n_rows, :] = page_t.T[:n_rows]
        read_rows = lambda br, l: rows_ref[br, pl.ds(l, groups, stride=CMP_STRIDE), :]
    else:
        read_rows = lambda br, l: buf[slot, br, pl.ds(l, groups, stride=CMP_STRIDE), :]

    low_half = lax.broadcasted_iota(jnp.int32, (groups, KV_W), 1) < NSA_HEAD_DIM
    for br, out_ref in ((0, kc_ref), (1, vc_ref)):
        acc0 = jnp.zeros((groups, CMP_HIDDEN), F32)
        acc1 = jnp.zeros((groups, CMP_HIDDEN), F32)
        for j in range(CMP_BLOCK // 2):
            a = read_rows(br, 2 * j) + pe_ref[br, 2 * j:2 * j + 1, :]
            b = pltpu.roll(read_rows(br, 2 * j + 1) + pe_ref[br, 2 * j + 1:2 * j + 2, :], NSA_HEAD_DIM, 1)
            acc0 = acc0 + _dot(jnp.where(low_half, a, b).astype(BF16), w1_ref[br, j, 0])
            acc1 = acc1 + _dot(jnp.where(low_half, b, a).astype(BF16), w1_ref[br, j, 1])
        hid = _gelu_tanh(jnp.concatenate([acc0, acc1], axis=1) + b1_ref[br]).astype(BF16)
        out_ref[0] = _dot(hid, w2_ref[br]).astype(out_ref.dtype)


def _compress(page_table, cache, pe2, w1cat, b1cat, w2bd, pps=PAGES_PER_STEP, transposed=False):
    nb, n_pages = page_table.shape
    n_steps = n_pages // pps
    groups = pps * PAGE_SIZE // CMP_STRIDE
    rows = pps * PAGE_SIZE + CMP_STRIDE
    const = lambda shape: pl.BlockSpec(shape, lambda b, s, pt: (0,) * len(shape))
    out_spec = pl.BlockSpec((1, groups, KV_W), lambda b, s, pt: (b, s, 0))
    out_sds = jax.ShapeDtypeStruct((nb, n_steps * groups, KV_W), BF16)
    if transposed:
        stage = [pltpu.VMEM((2, 2, KV_W, (pps + 1) * PAGE_SIZE), F32), pltpu.SemaphoreType.DMA((2,)),
                 pltpu.VMEM((2, rows, KV_W), F32)]
    else:
        stage = [pltpu.VMEM((2, 2, rows, KV_W), F32), pltpu.SemaphoreType.DMA((2,))]
    return pl.pallas_call(
        functools.partial(_compress_kernel, n_pages=n_pages, n_steps=n_steps, pps=pps, transposed=transposed),
        grid_spec=pltpu.PrefetchScalarGridSpec(
            num_scalar_prefetch=1,
            grid=(nb, n_steps),
            in_specs=[pl.BlockSpec(memory_space=pl.ANY),
                      const((2, CMP_BLOCK, KV_W)),
                      const((2, CMP_BLOCK // 2, 2, KV_W, CMP_HIDDEN)),
                      const((2, 1, 2 * CMP_HIDDEN)),
                      const((2, 2 * CMP_HIDDEN, KV_W))],
            out_specs=[out_spec, out_spec],
            scratch_shapes=stage),
        out_shape=[out_sds, out_sds],
        compiler_params=_cparams(("arbitrary", "arbitrary")),
        name="nsa_compress",
    )(page_table, cache, pe2, w1cat, b1cat, w2bd)


def _compress_weights(cmp_pe, cmp_w1, cmp_b1, cmp_w2):
    pe2 = jnp.concatenate([cmp_pe, cmp_pe], axis=-1)
    w1 = cmp_w1.reshape(2, CMP_BLOCK // 2, 2, NSA_HEAD_DIM, CMP_HIDDEN)
    even_odd = w1.reshape(2, CMP_BLOCK // 2, 2 * NSA_HEAD_DIM, CMP_HIDDEN)
    odd_even = w1[:, :, ::-1].reshape(2, CMP_BLOCK // 2, 2 * NSA_HEAD_DIM, CMP_HIDDEN)
    w1cat = jnp.stack([even_odd, odd_even], axis=2).astype(BF16)
    b1cat = jnp.concatenate([cmp_b1, cmp_b1], axis=-1)[:, None, :]
    z2 = jnp.zeros_like(cmp_w2)
    w2bd = jnp.concatenate([jnp.concatenate([cmp_w2, z2], axis=-1),
                            jnp.concatenate([z2, cmp_w2], axis=-1)], axis=1).astype(BF16)
    return pe2, w1cat, b1cat, w2bd


def _overlap_matrix(n_cmp, n_sel):
    cs = lax.broadcasted_iota(jnp.int32, (n_cmp, n_sel), 0) * CMP_STRIDE
    ss = lax.broadcasted_iota(jnp.int32, (n_cmp, n_sel), 1) * SEL_BLOCK
    return ((cs < ss + SEL_BLOCK) & (cs + CMP_BLOCK > ss)).astype(BF16)


SEL_CHUNK = 512
SPREAD_KEYS = 1024
WIN_SPAN = WINDOW + Q_BLOCK


def _nsa_prompt_kernel(q_ref, gt_ref, kc_ref, vc_ref, ks_ref, vs_ref, kw_ref, vw_ref, ex_ref, o_ref,
                       m_scr, acc_scr, chosen_scr, oc_scr, sa_scr, sb_scr, *, n_cmp, n_sel):
    j = pl.program_id(1)
    q0 = j * Q_BLOCK
    tok = lax.broadcasted_iota(jnp.int32, (Q_BLOCK, 1), 0) + q0
    tok4 = jnp.concatenate([tok] * NSA_GROUP, axis=0)
    n_chunks = j // (SEL_CHUNK // Q_BLOCK) + 1
    key_in_span = lax.broadcasted_iota(jnp.int32, (1, SPREAD_KEYS), 1)

    def load_q(g):
        return jnp.concatenate([q_ref[0, :, (NSA_GROUP * g + jh) * LANES:(NSA_GROUP * g + jh + 1) * LANES]
                                for jh in range(NSA_GROUP)], axis=0)

    def softmax_av(s, valid, v):
        s = jnp.where(valid, s, NEG)
        tiles = [s[:, i * LANES:(i + 1) * LANES] for i in range(s.shape[1] // LANES)]
        m = jnp.max(functools.reduce(jnp.maximum, tiles), axis=1, keepdims=True)
        p = jnp.where(valid, jnp.exp2(s - m), 0.0)
        acc = _dot(p.astype(BF16), jnp.concatenate([v, jnp.ones(v.shape, BF16)], axis=1))
        inv = 1.0 / jnp.maximum(acc[:, KV_W:], 1e-30)
        return p, acc[:, :KV_W] * inv, inv

    ov = _overlap_matrix(n_cmp, n_sel)
    cend = lax.broadcasted_iota(jnp.int32, (1, n_cmp), 1) * CMP_STRIDE + (CMP_BLOCK - 1)
    ws = pl.multiple_of(jnp.maximum(q0 - WINDOW, 0), Q_BLOCK)
    wpos = ws + lax.broadcasted_iota(jnp.int32, (1, WIN_SPAN), 1)
    d = tok4 - wpos
    in_window = (d >= 0) & (d < WINDOW)
    gates = gt_ref[0]
    imps = []
    for g in range(NSA_KV_HEADS):
        q = load_q(g)
        p, o_c, inv = softmax_av(_dot_nt(q, kc_ref[0]), cend <= tok4, vc_ref[0])
        _, o_w, _ = softmax_av(_dot_nt(q, kw_ref[0, pl.ds(ws, WIN_SPAN), :]), in_window,
                               vw_ref[0, pl.ds(ws, WIN_SPAN), :])
        psum = jnp.zeros((Q_BLOCK, n_cmp), F32)
        for jh in range(NSA_GROUP):
            h = NSA_GROUP * g + jh
            r = slice(jh * Q_BLOCK, (jh + 1) * Q_BLOCK)
            psum = psum + p[r] * jnp.concatenate([inv[r]] * (n_cmp // LANES), axis=1)
            oc_scr[g, r, :] = gates[:, 3 * h:3 * h + 1] * o_c[r] + gates[:, 3 * h + 2:3 * h + 3] * o_w[r]
        imps.append(_dot(psum.astype(BF16), ov))

    blk = lax.broadcasted_iota(jnp.int32, (1, n_sel), 1)
    cur = lax.shift_right_logical(tok, 6)
    forced = (blk == 0) | (blk == cur) | (blk == cur - 1)
    free = (blk * SEL_BLOCK <= tok) & jnp.logical_not(forced)
    forced2 = jnp.concatenate([forced] * NSA_KV_HEADS, axis=0)
    free2 = jnp.concatenate([free] * NSA_KV_HEADS, axis=0)
    best = _topk_mask(jnp.where(free2, jnp.concatenate(imps, axis=0), NEG), SEL_TOPN - 3)
    sel2 = jnp.where(forced2, 1.0, best).astype(BF16)

    ones_blk = jnp.ones((SEL_CHUNK, KV_W), BF16)

    for g in range(NSA_KV_HEADS):
        sel = sel2[g * Q_BLOCK:(g + 1) * Q_BLOCK]
        m_scr[...] = jnp.full(m_scr.shape, NEG, F32)
        acc_scr[...] = jnp.zeros(acc_scr.shape, F32)

        def spread(i, carry):
            c0 = pl.multiple_of(i * SPREAD_KEYS, SPREAD_KEYS)
            keep = (_dot(sel, ex_ref[:, pl.ds(c0, SPREAD_KEYS)]) > 0.5) & (key_in_span + c0 <= tok)
            chosen_scr[:, pl.ds(c0, SPREAD_KEYS)] = jnp.where(keep, 0.0, NEG)
            return carry

        lax.fori_loop(0, (n_chunks * SEL_CHUNK + SPREAD_KEYS - 1) // SPREAD_KEYS, spread, 0)

        heads = range(NSA_GROUP)
        rows = [pl.ds(jh * Q_BLOCK, Q_BLOCK) for jh in heads]

        def stage_scores(c, buf):
            k0 = pl.multiple_of(c * SEL_CHUNK, SEL_CHUNK)
            bias = chosen_scr[:, pl.ds(k0, SEL_CHUNK)]
            kblk = ks_ref[0, pl.ds(k0, SEL_CHUNK), :]
            for jh in heads:
                h = NSA_GROUP * g + jh
                buf[rows[jh], :] = _dot_nt(q_ref[0, :, h * LANES:(h + 1) * LANES], kblk) + bias

        def accumulate(c, buf):
            k0 = pl.multiple_of(c * SEL_CHUNK, SEL_CHUNK)
            vext = jnp.concatenate([vs_ref[0, pl.ds(k0, SEL_CHUNK), :], ones_blk], axis=1)
            tiles = [[buf[r, i * LANES:(i + 1) * LANES] for i in range(SEL_CHUNK // LANES)] for r in rows]
            m_old = [m_scr[r, :] for r in rows]
            m_new = [jnp.maximum(mo, jnp.max(functools.reduce(jnp.maximum, t), axis=1, keepdims=True))
                     for mo, t in zip(m_old, tiles)]
            probs = [jnp.concatenate([jnp.exp2(x - mn) for x in t], axis=1).astype(BF16)
                     for mn, t in zip(m_new, tiles)]
            pv = [_dot(p, vext) for p in probs]
            for r, mo, mn, y in zip(rows, m_old, m_new, pv):
                alpha = jnp.exp2(mo - mn)
                acc_scr[r, :] = jnp.concatenate([alpha, alpha], axis=1) * acc_scr[r, :] + y
                m_scr[r, :] = mn

        n_pairs = (n_chunks + 1) // 2
        stage_scores(0, sa_scr)

        def body(i, carry):
            stage_scores(2 * i + 1, sb_scr)
            accumulate(2 * i, sa_scr)
            stage_scores(jnp.minimum(2 * i + 2, 2 * n_pairs - 1), sa_scr)
            accumulate(2 * i + 1, sb_scr)
            return carry

        lax.fori_loop(0, n_pairs, body, 0)
        o_s = acc_scr[:, :KV_W] / jnp.maximum(acc_scr[:, KV_W:], 1e-30)

        for jh in range(NSA_GROUP):
            h = NSA_GROUP * g + jh
            r = slice(jh * Q_BLOCK, (jh + 1) * Q_BLOCK)
            o = oc_scr[g, r, :] + gates[:, 3 * h + 1:3 * h + 2] * o_s[r]
            o_ref[0, :, h * LANES:(h + 1) * LANES] = o.astype(o_ref.dtype)


def _nsa_prompt(qpad, gates, kc, vc, kvsel, kvwb):
    nb, t, _ = qpad.shape
    assert t % SPREAD_KEYS == 0 and SPREAD_KEYS == 2 * SEL_CHUNK and t >= WIN_SPAN
    n_cmp = kc.shape[1]
    n_sel = t // SEL_BLOCK
    full = lambda w, k: pl.BlockSpec((1, t, w), lambda b, j: (b, 0, k))
    t_pad = -(-t // SPREAD_KEYS) * SPREAD_KEYS
    expand = jnp.asarray(np.arange(n_sel)[:, None] == (np.arange(t_pad)[None, :] // SEL_BLOCK), BF16)
    return pl.pallas_call(
        functools.partial(_nsa_prompt_kernel, n_cmp=n_cmp, n_sel=n_sel),
        grid=(nb, t // Q_BLOCK),
        in_specs=[pl.BlockSpec((1, Q_BLOCK, QPAD_W), lambda b, j: (b, j, 0)),
                  pl.BlockSpec((1, Q_BLOCK, GZ_PAD), lambda b, j: (b, j, 0)),
                  pl.BlockSpec((1, n_cmp, KV_W), lambda b, j: (b, 0, 0)),
                  pl.BlockSpec((1, n_cmp, KV_W), lambda b, j: (b, 0, 0)),
                  full(KV_W, 0), full(KV_W, 1), full(KV_W, 0), full(KV_W, 1),
                  pl.BlockSpec((n_sel, t_pad), lambda b, j: (0, 0))],
        out_specs=pl.BlockSpec((1, Q_BLOCK, QPAD_W), lambda b, j: (b, j, 0)),
        out_shape=jax.ShapeDtypeStruct((nb, t, QPAD_W), BF16),
        scratch_shapes=[pltpu.VMEM((NSA_GROUP * Q_BLOCK, LANES), F32),
                        pltpu.VMEM((NSA_GROUP * Q_BLOCK, 2 * KV_W), F32),
                        pltpu.VMEM((Q_BLOCK, t_pad), F32),
                        pltpu.VMEM((NSA_KV_HEADS, NSA_GROUP * Q_BLOCK, KV_W), F32),
                        pltpu.VMEM((NSA_GROUP * Q_BLOCK, SEL_CHUNK), F32),
                        pltpu.VMEM((NSA_GROUP * Q_BLOCK, SEL_CHUNK), F32)],
        compiler_params=_cparams(("arbitrary", "arbitrary")),
        name="nsa_prompt",
    )(qpad, gates, kc, vc, kvsel, kvsel, kvwb, kvwb, expand)


def _nsa_sample_a_kernel(q_ref, g_ref, kc_ref, vc_ref, wb_ref, nw_ref, ocw_ref, sel_ref,
                         *, past_len, n_tok, n_sel, n_sel_pad):
    q = q_ref[0]
    rows = q.shape[0]
    n_cmp = kc_ref.shape[1]
    t_row = lax.broadcasted_iota(jnp.int32, (rows, 1), 0) & (n_tok - 1)
    qpos = past_len + t_row
    cend = lax.broadcasted_iota(jnp.int32, (1, n_cmp), 1) * CMP_STRIDE + (CMP_BLOCK - 1)
    p_c = _masked_softmax(_dot_nt(q, kc_ref[0]), cend <= qpos)
    o_c = _dot(p_c.astype(BF16), vc_ref[0])

    per_grp = NSA_GROUP * n_tok
    psum = jnp.concatenate(
        [sum(p_c[g * per_grp + jh * n_tok:g * per_grp + (jh + 1) * n_tok] for jh in range(NSA_GROUP))
         for g in range(NSA_KV_HEADS)], axis=0)
    imp = _dot(psum.astype(BF16), _overlap_matrix(n_cmp, n_sel_pad))
    blk = lax.broadcasted_iota(jnp.int32, (1, n_sel_pad), 1)
    tq = past_len + (lax.broadcasted_iota(jnp.int32, (NSA_KV_HEADS * n_tok, 1), 0) & (n_tok - 1))
    cur = lax.shift_right_logical(tq, 6)
    forced = (blk == 0) | (blk == cur) | (blk == cur - 1)
    allowed = blk * SEL_BLOCK <= tq
    v = jnp.where(forced, -NEG, jnp.where(allowed, imp, NEG))
    sel_ref[0] = _topk_mask(jnp.where(blk < n_sel, v, 2.0 * NEG), SEL_TOPN)

    wb = wb_ref.shape[1]
    kw = wb_ref[0, :, 0:KV_W].astype(BF16)
    vw = wb_ref[0, :, KV_W:2 * KV_W].astype(BF16)
    kn = nw_ref[0, :, 0:KV_W].astype(BF16)
    vn = nw_ref[0, :, KV_W:2 * KV_W].astype(BF16)
    i1 = lax.broadcasted_iota(jnp.int32, (1, wb), 1)
    d1 = t_row + wb - i1
    valid1 = (d1 >= 0) & (d1 < WINDOW) & (past_len - wb + i1 >= 0)
    i2 = lax.broadcasted_iota(jnp.int32, (1, nw_ref.shape[1]), 1)
    d2 = t_row - i2
    valid2 = (d2 >= 0) & (d2 < WINDOW) & (i2 < n_tok)
    s1 = jnp.where(valid1, _dot_nt(q, kw), NEG)
    s2 = jnp.where(valid2, _dot_nt(q, kn), NEG)
    m = jnp.maximum(jnp.max(s1, axis=1, keepdims=True), jnp.max(s2, axis=1, keepdims=True))
    p1 = jnp.exp2(s1 - m) * valid1.astype(F32)
    p2 = jnp.exp2(s2 - m) * valid2.astype(F32)
    den = jnp.maximum(jnp.sum(p1, axis=1, keepdims=True) + jnp.sum(p2, axis=1, keepdims=True), 1e-30)
    o_w = (_dot(p1.astype(BF16), vw) + _dot(p2.astype(BF16), vn)) / den
    g = g_ref[0]
    ocw_ref[0] = g[:, 0:1] * o_c + g[:, 2:3] * o_w


def _nsa_sample_a(q_rows, g_rows, kc, vc, win_buf, new_win, *, past_len, n_tok):
    nb, rows, _ = q_rows.shape
    n_sel = -(-(past_len + n_tok) // SEL_BLOCK)
    n_sel_pad = -(-n_sel // LANES) * LANES
    blk3 = lambda a: pl.BlockSpec((1,) + a.shape[1:], lambda b: (b, 0, 0))
    return pl.pallas_call(
        functools.partial(_nsa_sample_a_kernel, past_len=past_len, n_tok=n_tok, n_sel=n_sel, n_sel_pad=n_sel_pad),
        grid=(nb,),
        in_specs=[blk3(q_rows), blk3(g_rows), blk3(kc), blk3(vc), blk3(win_buf), blk3(new_win)],
        out_specs=[pl.BlockSpec((1, rows, KV_W), lambda b: (b, 0, 0)),
                   pl.BlockSpec((1, NSA_KV_HEADS * n_tok, n_sel_pad), lambda b: (b, 0, 0))],
        out_shape=[jax.ShapeDtypeStruct((nb, rows, KV_W), F32),
                   jax.ShapeDtypeStruct((nb, NSA_KV_HEADS * n_tok, n_sel_pad), F32)],
        compiler_params=_cparams(("arbitrary",)),
        name="nsa_sample_a",
    )(q_rows, g_rows, kc, vc, win_buf, new_win)


def _nsa_sample_b_kernel(pt_ref, cache_ref, q_ref, g_ref, sel_ref, seln_ref, ns_ref, ocw_ref, ex_ref, o_ref,
                         buf, sem, m_scr, l_scr, acc_scr, *, n_pages, n_steps, pps, n_tok):
    s = pl.program_id(1)
    slot = _stream_pages(pt_ref, cache_ref, buf, sem, n_pages, n_steps, pps, 2 * KV_W, False, True)
    q = q_ref[0]
    rows = q.shape[0]

    @pl.when(s == 0)
    def _():
        m_scr[...] = jnp.full(m_scr.shape, NEG, F32)
        l_scr[...] = jnp.zeros(l_scr.shape, F32)
        acc_scr[...] = jnp.zeros(acc_scr.shape, F32)

    def update(scores, msk, times_v):
        sc = jnp.where(msk, scores, NEG)
        m_old = m_scr[...]
        m_new = jnp.maximum(m_old, jnp.max(sc, axis=1, keepdims=True))
        p = jnp.exp2(sc - m_new) * msk.astype(F32)
        alpha = jnp.exp2(m_old - m_new)
        l_scr[...] = alpha * l_scr[...] + jnp.sum(p, axis=1, keepdims=True)
        acc_scr[...] = alpha * acc_scr[...] + times_v(p.astype(BF16))
        m_scr[...] = m_new

    chosen = _dot(sel_ref[0, 0], ex_ref[...]) > 0.5
    update(_dot(q, buf[slot, 0].astype(BF16)), chosen, lambda p: _dot_nt(p, buf[slot, 1].astype(BF16)))

    @pl.when(s == n_steps - 1)
    def _():
        t_row = lax.broadcasted_iota(jnp.int32, (rows, 1), 0) & (n_tok - 1)
        i2 = lax.broadcasted_iota(jnp.int32, (1, ns_ref.shape[1]), 1)
        msk = (seln_ref[0, 0][:, 0:1] > 0.5) & (i2 <= t_row) & (i2 < n_tok)
        update(_dot_nt(q, ns_ref[0, :, 0:KV_W].astype(BF16)), msk,
               lambda p: _dot(p, ns_ref[0, :, KV_W:2 * KV_W].astype(BF16)))
        o_s = acc_scr[...] / jnp.maximum(l_scr[...], 1e-30)
        o_ref[0] = ocw_ref[0] + g_ref[0][:, 1:2] * o_s


def _nsa_sample_b(page_table, cache, q_rows, g_rows, sel_steps, new_sel, ocw, *, n_tok, pps=PAGES_PER_STEP):
    nb, n_pages = page_table.shape
    n_steps = n_pages // pps
    rows = q_rows.shape[1]
    keys = pps * PAGE_SIZE
    expand = jnp.asarray(np.arange(LANES)[:, None] == (np.arange(keys)[None, :] // SEL_BLOCK), BF16)
    per_b = lambda a: pl.BlockSpec((1,) + a.shape[1:], lambda b, s, pt: (b, 0, 0))
    return pl.pallas_call(
        functools.partial(_nsa_sample_b_kernel, n_pages=n_pages, n_steps=n_steps, pps=pps, n_tok=n_tok),
        grid_spec=pltpu.PrefetchScalarGridSpec(
            num_scalar_prefetch=1,
            grid=(nb, n_steps),
            in_specs=[pl.BlockSpec(memory_space=pl.ANY),
                      per_b(q_rows), per_b(g_rows),
                      pl.BlockSpec((1, 1, rows, LANES), lambda b, s, pt: (b, s, 0, 0)),
                      pl.BlockSpec((1, 1, rows, LANES), lambda b, s, pt: (b, n_steps, 0, 0)),
                      per_b(new_sel), per_b(ocw),
                      pl.BlockSpec((LANES, keys), lambda b, s, pt: (0, 0))],
            out_specs=pl.BlockSpec((1, rows, KV_W), lambda b, s, pt: (b, 0, 0)),
            scratch_shapes=[pltpu.VMEM((2, 2, KV_W, keys), F32), pltpu.SemaphoreType.DMA((2,)),
                            pltpu.VMEM((rows, 1), F32), pltpu.VMEM((rows, 1), F32),
                            pltpu.VMEM((rows, KV_W), F32)]),
        out_shape=jax.ShapeDtypeStruct((nb, rows, KV_W), F32),
        compiler_params=_cparams(("arbitrary", "arbitrary")),
        name="nsa_sample_b",
    )(page_table, cache, q_rows, g_rows, sel_steps, sel_steps, new_sel, ocw, expand)


def _nsa_sample(page_table, cache, win_buf, cw, qpad, gates, kv4, kvw, pps=PAGES_PER_STEP):
    nb, ts, _ = qpad.shape
    past_len = page_table.shape[1] * PAGE_SIZE
    kc, vc = _compress(page_table, cache, *cw, pps=pps, transposed=True)
    rows = NSA_HEADS * ts
    q_rows = qpad.reshape(nb, ts, NSA_HEADS, LANES).transpose(0, 2, 1, 3).reshape(nb, rows, LANES)
    g_rows = gates[:, :, :3 * NSA_HEADS].reshape(nb, ts, NSA_HEADS, 3).transpose(0, 2, 1, 3)
    g_rows = jnp.pad(g_rows.reshape(nb, rows, 3), ((0, 0), (0, 0), (0, LANES - 3)))
    pad_rows = lambda a: jnp.pad(a, ((0, 0), (0, LANES - ts), (0, 0)))
    new_win = pad_rows(kvw)
    new_sel = pad_rows(kv4[:, :, 2 * KV_W:])
    ocw, sel = _nsa_sample_a(q_rows, g_rows, kc, vc, win_buf, new_win, past_len=past_len, n_tok=ts)
    n_steps = page_table.shape[1] // pps
    blk_per_step = pps * PAGE_SIZE // SEL_BLOCK
    n_past_blk = n_steps * blk_per_step
    sel_past = sel[:, :, :n_past_blk].reshape(nb, NSA_KV_HEADS, 1, ts, n_steps, blk_per_step)
    sel_past = jnp.broadcast_to(sel_past, (nb, NSA_KV_HEADS, NSA_GROUP, ts, n_steps, blk_per_step))
    sel_past = sel_past.transpose(0, 4, 1, 2, 3, 5).reshape(nb, n_steps, rows, blk_per_step)
    sel_past = jnp.pad(sel_past, ((0, 0), (0, 0), (0, 0), (0, LANES - blk_per_step)))
    sel_new = jnp.pad(sel[:, :, n_past_blk:], ((0, 0), (0, 0), (0, LANES)))[:, :, :LANES]
    sel_new = sel_new.reshape(nb, NSA_KV_HEADS, 1, ts, LANES)
    sel_new = jnp.broadcast_to(sel_new, (nb, NSA_KV_HEADS, NSA_GROUP, ts, LANES)).reshape(nb, 1, rows, LANES)
    sel_steps = jnp.concatenate([sel_past, sel_new], axis=1).astype(BF16)
    o_rows = _nsa_sample_b(page_table, cache, q_rows, g_rows, sel_steps, new_sel, ocw, n_tok=ts, pps=pps)
    return o_rows.reshape(nb, NSA_HEADS, ts, LANES).transpose(0, 2, 1, 3).reshape(nb, ts, QPAD_W).astype(BF16)


def _outproj_kernel(x_ref, hg_ref, nsa_ref, g1_ref, sc2_ref, sh2_ref, fn_ref, wo1_ref, wo2_ref, *rest, tm):
    x1_ref, h2_ref = rest[-2:]
    mix =_dot(hg_ref[0], wo1_ref[...]) + _dot(nsa_ref[0], wo2_ref[...])
    x1 = x_ref[0] + g1_ref[0] * mix
    x1_ref[0] = x1
    h2 = _rms(x1, fn_ref[...]) * (1.0 + sc2_ref[0]) + sh2_ref[0]
    _store_tok_tiles(h2_ref, h2, tm)


def _outproj(x, hg_out, nsa, gate1, scale2, shift2, ffn_norm, wo_hg, wo_nsa, tm, n_all, h2_all=None):
    nb, t, _ = x.shape
    nt = t // tm
    tile0 = 0 if h2_all is None else (n_all - nb * t) // tm
    mod_spec = _mod_spec(gate1, tm)
    tile = lambda w: pl.BlockSpec((1, tm, w), lambda b, i: (b, i, 0))
    in_specs = [tile(D_MODEL), tile(HG_WIDTH), tile(QPAD_W), mod_spec, mod_spec, mod_spec,
                pl.BlockSpec((1, D_MODEL), lambda b, i: (0, 0)),
                pl.BlockSpec((HG_WIDTH, D_MODEL), lambda b, i: (0, 0)),
                pl.BlockSpec((QPAD_W, D_MODEL), lambda b, i: (0, 0))]
    args = [x, hg_out, nsa, gate1, scale2, shift2, ffn_norm.reshape(1, -1), wo_hg, wo_nsa]
    aliases = {}
    if h2_all is not None:
        in_specs.append(pl.BlockSpec(memory_space=pl.ANY))
        args.append(h2_all)
        aliases = {len(args) - 1: 1}
    return pl.pallas_call(
        functools.partial(_outproj_kernel, tm=tm),
        grid=(nb, nt),
        in_specs=in_specs,
        out_specs=[tile(D_MODEL), pl.BlockSpec((tm * TOK_ROWS, LANES), lambda b, i: (tile0 + b * nt + i, 0))],
        out_shape=[jax.ShapeDtypeStruct((nb, t, D_MODEL), F32),
                   jax.ShapeDtypeStruct((n_all * TOK_ROWS, LANES), F32)],
        input_output_aliases=aliases,
        compiler_params=_cparams(("arbitrary", "arbitrary")),
        name="out_proj",
    )(*args)


def _split_w_out(w_out):
    wo_hg = w_out[:HG_WIDTH].astype(BF16)
    wn = w_out[HG_WIDTH:].reshape(NSA_HEADS, NSA_HEAD_DIM, D_MODEL)
    z = jnp.zeros_like(wn)
    grp = (jnp.arange(NSA_HEADS) // NSA_GROUP)[:, None, None]
    wn_pad = jnp.where(grp == 0, jnp.concatenate([wn, z], axis=1), jnp.concatenate([z, wn], axis=1))
    return wo_hg, wn_pad.reshape(QPAD_W, D_MODEL).astype(BF16)


def _router_kernel(h_ref, wr_ref, b_ref, e_ref, w_ref, r_ref, cnt_ref, run_scr, *, tm):
    @pl.when(pl.program_id(0) == 0)
    def _():
        run_scr[...] = jnp.zeros(run_scr.shape, F32)

    x = _load_tok_tiles(h_ref, tm).astype(BF16)
    scores = jax.nn.sigmoid(_dot(x, wr_ref[...]))
    biased = scores + b_ref[...]
    lane_i = lax.broadcasted_iota(jnp.int32, (tm, N_EXPERTS), 1)
    lane = lane_i.astype(F32)
    grp_of_lane = lax.shift_right_logical(lane_i, 5)
    per_group = N_EXPERTS // N_GROUPS

    gcol = lax.broadcasted_iota(jnp.int32, (tm, LANES), 1)
    gs = jnp.full((tm, LANES), 2.0 * NEG, F32)
    for g in range(N_GROUPS):
        mg = jnp.where(grp_of_lane == g, biased, NEG)
        m1 = jnp.max(mg, axis=1, keepdims=True)
        i1 = jnp.min(jnp.where(mg == m1, lane, 1e9), axis=1, keepdims=True)
        m2 = jnp.max(jnp.where(lane == i1, NEG, mg), axis=1, keepdims=True)
        gs = jnp.where(gcol == g, m1 + m2, gs)
    gsel = _topk_mask(gs, TOPK_GROUPS).astype(BF16)
    spread = (lax.broadcasted_iota(jnp.int32, (LANES, N_EXPERTS), 0)
              == lax.shift_right_logical(lax.broadcasted_iota(jnp.int32, (LANES, N_EXPERTS), 1), 5)).astype(BF16)
    v = jnp.where(_dot(gsel, spread) > 0.5, biased, NEG)

    onehot = jnp.zeros((tm, N_EXPERTS), F32)
    idxs, wts = [], []
    wsum = jnp.zeros((tm, 1), F32)
    for _ in range(TOP_K):
        m = jnp.max(v, axis=1, keepdims=True)
        idx = jnp.min(jnp.where(v == m, lane, 1e9), axis=1, keepdims=True)
        pick = lane == idx
        wk = jnp.sum(jnp.where(pick, scores, 0.0), axis=1, keepdims=True)
        onehot = jnp.where(pick, 1.0, onehot)
        v = jnp.where(pick, 3.0 * NEG, v)
        idxs.append(idx)
        wts.append(wk)
        wsum = wsum + wk

    earlier = (lax.broadcasted_iota(jnp.int32, (tm, tm), 0) > lax.broadcasted_iota(jnp.int32, (tm, tm), 1))
    before = _dot(earlier.astype(BF16), onehot.astype(BF16)) + run_scr[...]
    e_out = jnp.zeros((tm, LANES), jnp.int32)
    r_out = jnp.zeros((tm, LANES), jnp.int32)
    w_out = jnp.zeros((tm, LANES), F32)
    for k in range(TOP_K):
        rk = jnp.sum(jnp.where(lane == idxs[k], before, 0.0), axis=1, keepdims=True)
        e_out = jnp.where(gcol == k, idxs[k].astype(jnp.int32), e_out)
        r_out = jnp.where(gcol == k, rk.astype(jnp.int32), r_out)
        w_out = jnp.where(gcol == k, wts[k] / wsum * ROUTED_SCALE, w_out)
    e_ref[...] = e_out
    r_ref[...] = r_out
    w_ref[...] = w_out
    run_scr[...] = run_scr[...] + jnp.sum(onehot, axis=0, keepdims=True)
    cnt_ref[...] = run_scr[...]


def _router(h2, w_router, bias, n_tok, tm=ROUTER_TM):
    tile = pl.BlockSpec((tm, LANES), lambda i: (i, 0))
    return pl.pallas_call(
        functools.partial(_router_kernel, tm=tm),
        grid=(n_tok // tm,),
        in_specs=[pl.BlockSpec((tm * TOK_ROWS, LANES), lambda i: (i, 0)),
                  pl.BlockSpec((D_MODEL, N_EXPERTS), lambda i: (0, 0)),
                  pl.BlockSpec((1, N_EXPERTS), lambda i: (0, 0))],
        out_specs=[tile, tile, tile, pl.BlockSpec((1, N_EXPERTS), lambda i: (0, 0))],
        out_shape=[jax.ShapeDtypeStruct((n_tok, LANES), jnp.int32),
                   jax.ShapeDtypeStruct((n_tok, LANES), F32),
                   jax.ShapeDtypeStruct((n_tok, LANES), jnp.int32),
                   jax.ShapeDtypeStruct((1, N_EXPERTS), F32)],
        scratch_shapes=[pltpu.VMEM((1, N_EXPERTS), F32)],
        compiler_params=_cparams(("arbitrary",)),
        name="moe_router",
    )(h2, w_router, bias)


def _dest_kernel(e_ref, r_ref, st_ref, d_ref):
    e = e_ref[...]
    tm = e.shape[0]
    lane = lax.broadcasted_iota(jnp.int32, (tm, N_EXPERTS), 1)
    col = lax.broadcasted_iota(jnp.int32, (tm, LANES), 1)
    st = st_ref[...]
    out = r_ref[...]
    for k in range(TOP_K):
        sk = jnp.sum(jnp.where(lane == e[:, k:k + 1], st, 0.0), axis=1, keepdims=True)
        out = jnp.where(col == k, out + sk.astype(jnp.int32), out)
    d_ref[...] = out


def _moe_dest(top_e, rank, starts, tm=ROUTER_TM):
    n_tok = top_e.shape[0]
    tile = pl.BlockSpec((tm, LANES), lambda i: (i, 0))
    return pl.pallas_call(
        _dest_kernel,
        grid=(n_tok // tm,),
        in_specs=[tile, tile, pl.BlockSpec((1, N_EXPERTS), lambda i: (0, 0))],
        out_specs=tile,
        out_shape=jax.ShapeDtypeStruct((n_tok, LANES), jnp.int32),
        compiler_params=_cparams(("arbitrary",)),
        name="moe_dest",
    )(top_e, rank, starts.astype(F32).reshape(1, -1))


def _moe_layout(counts, n_pairs):
    padded = (counts + MOE_BM - 1) // MOE_BM * MOE_BM
    pad_end = jnp.cumsum(padded)
    starts = pad_end - padded
    n_blocks = (n_pairs + N_EXPERTS * (MOE_BM - 1)) // MOE_BM
    blk = jnp.arange(n_blocks, dtype=jnp.int32)
    used = blk * MOE_BM < pad_end[-1]
    blk_run = jnp.minimum(blk, pad_end[-1] // MOE_BM - 1).astype(jnp.int32)
    e_of = jnp.sum(pad_end[None, :] <= (blk_run * MOE_BM)[:, None], axis=1).astype(jnp.int32)
    e_of = jnp.minimum(e_of, N_EXPERTS - 1)
    shifted = jnp.concatenate([jnp.full((1,), -1, jnp.int32), e_of[:-1]])
    fresh = (used & (e_of != shifted)).astype(jnp.int32)
    w_slot = ((jnp.cumsum(fresh) - 1) % 2).astype(jnp.int32)
    ids = jnp.arange(N_EXPERTS, dtype=jnp.int32)
    later = jnp.where(counts > 0, ids, N_EXPERTS)
    next_nonempty = jnp.concatenate([lax.cummin(later, reverse=True)[1:], jnp.full((1,), N_EXPERTS, jnp.int32)])
    next_e = next_nonempty[e_of].astype(jnp.int32)
    return (starts.astype(jnp.int32), (starts + counts).astype(jnp.int32),
            (blk_run, e_of, fresh, used.astype(jnp.int32), w_slot, next_e), n_blocks)


def _dispatch_kernel(dest_ref, h_ref, xs_ref, sem, *, tm):
    n_pairs = tm * TOP_K

    def row_copy(src_tok, dst_row):
        return pltpu.make_async_copy(
            h_ref.at[pl.ds(pl.multiple_of(src_tok * TOK_ROWS, TOK_ROWS), TOK_ROWS), :],
            xs_ref.at[pl.ds(pl.multiple_of(dst_row * TOK_ROWS, TOK_ROWS), TOK_ROWS), :], sem)

    def issue(i, carry):
        for k in range(TOP_K):
            row_copy(i, dest_ref[0, 0, i * TOP_K + k]).start(priority=k % 2)
        return carry

    def drain(i, carry):
        for _ in range(TOP_K):
            row_copy(0, 0).wait()
        return carry

    lax.fori_loop(0, tm, issue, 0)
    lax.fori_loop(0, tm, drain, 0)


def _dispatch(dest_tiles, h2, n_pairs, tm=ROUTER_TM):
    n_tiles = dest_tiles.shape[0]
    return pl.pallas_call(
        functools.partial(_dispatch_kernel, tm=tm),
        grid=(n_tiles,),
        in_specs=[pl.BlockSpec((1, 1, tm * TOP_K), lambda i: (i, 0, 0), memory_space=pltpu.SMEM),
                  pl.BlockSpec((tm * TOK_ROWS, LANES), lambda i: (i, 0))],
        out_specs=pl.BlockSpec(memory_space=pl.ANY),
        out_shape=jax.ShapeDtypeStruct((n_pairs * TOK_ROWS, LANES), F32),
        scratch_shapes=[pltpu.SemaphoreType.DMA(())],
        compiler_params=_cparams(("arbitrary",)),
        name="moe_dispatch",
    )(dest_tiles, h2)


GMM_SLOTS = 3


def _gmm_kernel(blk_ref, e_ref, fresh_ref, used_ref, wslot_ref, next_ref, end_ref,
                xs_ref, wg_ref, wu_ref, wd_ref, ys_ref, wg_bf, wu_bf, wd_bf, xbuf, sem,
                wg_st, wu_st, wd_st, wsem, *, bm):
    i = pl.program_id(0)
    n = pl.num_programs(0)

    def fetch(j):
        rows = pl.ds(pl.multiple_of(blk_ref[j] * (bm * TOK_ROWS), bm * TOK_ROWS), bm * TOK_ROWS)
        return pltpu.make_async_copy(xs_ref.at[rows, :], xbuf.at[j % GMM_SLOTS], sem.at[j % GMM_SLOTS])

    def weight_copies(e, slot):
        return [pltpu.make_async_copy(src.at[e], dst.at[slot], wsem.at[slot])
                for src, dst in ((wg_ref, wg_st), (wu_ref, wu_st), (wd_ref, wd_st))]

    @pl.when(i == 0)
    def _():
        for cp in weight_copies(e_ref[0], wslot_ref[0]):
            cp.start()
        for j in range(GMM_SLOTS - 1):
            @pl.when((j < n) & (used_ref[jnp.minimum(j, n - 1)] == 1))
            def _():
                fetch(j).start()

    ahead = jnp.minimum(i + GMM_SLOTS - 1, n - 1)

    @pl.when((i + GMM_SLOTS - 1 < n) & (used_ref[ahead] == 1))
    def _():
        fetch(ahead).start()

    @pl.when(fresh_ref[i] == 1)
    def _():
        slot = wslot_ref[i]
        for cp in weight_copies(e_ref[i], slot):
            cp.wait()
        wg_bf[...] = wg_st[slot].astype(BF16)
        wu_bf[...] = wu_st[slot].astype(BF16)
        wd_bf[...] = wd_st[slot].astype(BF16)

        @pl.when(next_ref[i] < N_EXPERTS)
        def _():
            for cp in weight_copies(next_ref[i], 1 - slot):
                cp.start()

    @pl.when(used_ref[i] == 1)
    def _():
        fetch(i).wait()
        row = blk_ref[i] * bm + lax.broadcasted_iota(jnp.int32, (bm, 1), 0)
        real = row < end_ref[e_ref[i]]
        x = jnp.where(real, _load_tok_tiles(xbuf.at[i % GMM_SLOTS], bm), 0.0).astype(BF16)
        hid = (_silu(_dot(x, wg_bf[...])) * _dot(x, wu_bf[...])).astype(BF16)
        _store_tok_tiles(ys_ref, _dot(hid, wd_bf[...]), bm)


def _moe_gmm(blocks, row_end, xs_sorted, w_gate, w_up, w_down, bm=MOE_BM):
    rows = pl.BlockSpec((bm * TOK_ROWS, LANES), lambda i, blk, *_: (blk[i], 0))
    in_hbm = pl.BlockSpec(memory_space=pl.ANY)
    weights = (w_gate, w_up, w_down)
    return pl.pallas_call(
        functools.partial(_gmm_kernel, bm=bm),
        grid_spec=pltpu.PrefetchScalarGridSpec(
            num_scalar_prefetch=len(blocks) + 1,
            grid=(blocks[0].shape[0],),
            in_specs=[in_hbm, in_hbm, in_hbm, in_hbm],
            out_specs=rows,
            scratch_shapes=[pltpu.VMEM(w.shape[1:], BF16) for w in weights]
            + [pltpu.VMEM((GMM_SLOTS, bm * TOK_ROWS, LANES), F32), pltpu.SemaphoreType.DMA((GMM_SLOTS,))]
            + [pltpu.VMEM((2,) + w.shape[1:], F32) for w in weights]
            + [pltpu.SemaphoreType.DMA((2,))]),
        out_shape=jax.ShapeDtypeStruct(xs_sorted.shape, F32),
        compiler_params=_cparams(("arbitrary",)),
        name="moe_experts",
    )(*blocks, row_end, xs_sorted, *weights)


def _combine_kernel(dest_ref, dnext_ref, w_ref, x1_ref, h_ref, g2_ref, wsg_ref, wsu_ref, wsd_ref, fn_ref, ys_ref,
                    o_ref, gbuf, sem, routed_scr, *, tm):
    n = pl.program_id(0) * pl.num_programs(1) + pl.program_id(1)
    total = pl.num_programs(0) * pl.num_programs(1)
    slot = n % 2

    def row_copy(src_row, p, sl):
        return pltpu.make_async_copy(
            ys_ref.at[pl.ds(pl.multiple_of(src_row * TOK_ROWS, TOK_ROWS), TOK_ROWS), :],
            gbuf.at[sl, pl.ds(pl.multiple_of(p * TOK_ROWS, TOK_ROWS), TOK_ROWS), :], sem.at[sl])

    def gather(rows_ref, sl):
        def issue(i, carry):
            for k in range(TOP_K):
                p = i * TOP_K + k
                row_copy(rows_ref[0, 0, p], p, sl).start(priority=k % 2)
            return carry
        lax.fori_loop(0, tm, issue, 0)

    def drain(i, carry):
        for _ in range(TOP_K):
            row_copy(0, 0, slot).wait()
        return carry

    @pl.when(n == 0)
    def _():
        gather(dest_ref, slot)

    @pl.when(n + 1 < total)
    def _():
        gather(dnext_ref, 1 - slot)

    h = _load_tok_tiles(h_ref, tm).astype(BF16)
    hid = (_silu(_dot(h, wsg_ref[...])) * _dot(h, wsu_ref[...])).astype(BF16)
    shared = _dot(hid, wsd_ref[...])
    lax.fori_loop(0, tm, drain, 0)

    def weigh(t, carry):
        acc = jnp.zeros((TOK_ROWS, LANES), F32)
        for k in range(TOP_K):
            p = t * TOP_K + k
            acc = acc + w_ref[0, 0, p] * gbuf[slot, pl.ds(pl.multiple_of(p * TOK_ROWS, TOK_ROWS), TOK_ROWS), :]
        routed_scr[pl.ds(pl.multiple_of(t * TOK_ROWS, TOK_ROWS), TOK_ROWS), :] = acc
        return carry

    lax.fori_loop(0, tm, weigh, 0)
    x2 = x1_ref[0] + g2_ref[0] * (_load_tok_tiles(routed_scr, tm) + shared)
    o_ref[0] = _rms(x2, fn_ref[...])


def _combine(dest_tiles, w_tiles, x1, h2, gate2, shared, fnorm, ys_sorted, tile0, tm=COMBINE_TM):
    nb, t, _ = x1.shape
    nt = t // tm
    flat = lambda b, i: tile0 + b * nt + i
    nxt = lambda b, i: tile0 + jnp.minimum(b * nt + i + 1, nb * nt - 1)
    mod_spec = _mod_spec(gate2, tm)
    const = lambda a: pl.BlockSpec(a.shape, lambda b, i: (0, 0))
    return pl.pallas_call(
        functools.partial(_combine_kernel, tm=tm),
        grid=(nb, nt),
        in_specs=[pl.BlockSpec((1, 1, tm * TOP_K), lambda b, i: (flat(b, i), 0, 0), memory_space=pltpu.SMEM),
                  pl.BlockSpec((1, 1, tm * TOP_K), lambda b, i: (nxt(b, i), 0, 0), memory_space=pltpu.SMEM),
                  pl.BlockSpec((1, 1, tm * TOP_K), lambda b, i: (flat(b, i), 0, 0), memory_space=pltpu.SMEM),
                  pl.BlockSpec((1, tm, D_MODEL), lambda b, i: (b, i, 0)),
                  pl.BlockSpec((tm * TOK_ROWS, LANES), lambda b, i: (flat(b, i), 0)),
                  mod_spec, const(shared[0]), const(shared[1]), const(shared[2]), const(fnorm),
                  pl.BlockSpec(memory_space=pl.ANY)],
        out_specs=pl.BlockSpec((1, tm, D_MODEL), lambda b, i: (b, i, 0)),
        out_shape=jax.ShapeDtypeStruct((nb, t, D_MODEL), F32),
        scratch_shapes=[pltpu.VMEM((2, tm * TOP_K * TOK_ROWS, LANES), F32), pltpu.SemaphoreType.DMA((2,)),
                        pltpu.VMEM((tm * TOK_ROWS, LANES), F32)],
        compiler_params=_cparams(("arbitrary", "arbitrary")),
        name="moe_combine",
    )(dest_tiles, dest_tiles, w_tiles, x1, h2, gate2, *shared, fnorm, ys_sorted)


def kernel(x_prompt, x_sample, c_prompt, c_sample, cache_nsa_kv, cache_win_kv, state_hgrn, page_table,
           attn_norm, ffn_norm, final_norm, hg_norm, w_ada, b_ada, w_in, hg_lb,
           cmp_pe, cmp_w1, cmp_b1, cmp_w2, w_out, w_router, router_bias,
           w_gate, w_up, w_down, ws_gate, ws_up, ws_down):
    nbp, t, _ = x_prompt.shape
    nbs, ts, _ = x_sample.shape
    ns = nbs * ts
    n_all = nbp * t + ns
    past_len = page_table.shape[1] * PAGE_SIZE

    c_all = jnp.concatenate([c_prompt, c_sample], axis=0)
    c_all = jnp.pad(c_all, ((0, -c_all.shape[0] % SUBLANES), (0, 0)))
    mod = _ada(c_all, w_ada[0], b_ada[0])
    modp = mod[:nbp].reshape(nbp, 1, 6, D_MODEL)
    mods = jnp.repeat(mod[nbp:nbp + nbs].reshape(nbs, 1, 6, D_MODEL), ts, axis=1).reshape(1, ns, 6, D_MODEL)

    w_pad = _pad_w_in(w_in[0])
    cw = _compress_weights(cmp_pe[0], cmp_w1[0], cmp_b1[0], cmp_w2[0])
    wo_hg, wo_nsa = _split_w_out(w_out[0])

    hg, qpad, kv4, kvw, gates, kvsel, kvwb = _inproj(
        x_prompt, modp[:, :, 1], modp[:, :, 0], attn_norm[0], w_pad, 512)
    hg_out, hg_state_p = _hgrn(hg, hg_lb, jnp.zeros((nbp, HG_HEADS, HG_DK, HG_DK), F32), hg_norm[0],
                               256, HG_CHUNK)
    n_pages_p = t // PAGE_SIZE
    ptp = jnp.arange(nbp * n_pages_p, dtype=jnp.int32).reshape(nbp, n_pages_p)
    kc, vc = _compress(ptp, kv4.reshape(nbp * n_pages_p, PAGE_SIZE, 4 * KV_W), *cw)
    nsa = _nsa_prompt(qpad, gates, kc, vc, kvsel, kvwb)
    x1p, h2p = _outproj(x_prompt, hg_out, nsa, modp[:, :, 2], modp[:, :, 4], modp[:, :, 3],
                        ffn_norm[0], wo_hg, wo_nsa, 512, n_all)

    xs = x_sample.reshape(1, ns, D_MODEL)
    hg_s, qpad_s, kv4_s, kvw_s, gates_s, _, _ = _inproj(
        xs, mods[:, :, 1], mods[:, :, 0], attn_norm[0], w_pad, ns)
    hg_out_s, hg_state_s = _hgrn(hg_s.reshape(nbs, ts, 4 * HG_WIDTH), hg_lb, state_hgrn[0], hg_norm[0], ts, ts)
    cache = cache_nsa_kv[0].transpose(0, 2, 3, 4, 1).reshape(-1, 4 * KV_W, PAGE_SIZE)
    win_buf = cache_win_kv[0].reshape(nbs, -1, 2 * KV_W)
    nsa_s = _nsa_sample(page_table, cache, win_buf, cw, qpad_s.reshape(nbs, ts, QPAD_W),
                        gates_s.reshape(nbs, ts, GZ_PAD), kv4_s.reshape(nbs, ts, 4 * KV_W),
                        kvw_s.reshape(nbs, ts, 2 * KV_W)).reshape(1, ns, QPAD_W)
    x1s, h2 = _outproj(xs, hg_out_s.reshape(1, ns, HG_WIDTH), nsa_s, mods[:, :, 2], mods[:, :, 4], mods[:, :, 3],
                       ffn_norm[0], wo_hg, wo_nsa, ns, n_all, h2_all=h2p)

    top_e, top_w, rank, counts = _router(h2, w_router[0].astype(BF16), router_bias[0].reshape(1, -1), n_all)
    starts, row_end, blocks, n_blocks = _moe_layout(counts[0].astype(jnp.int32), n_all * TOP_K)
    dest = _moe_dest(top_e, rank, starts)[:, :TOP_K].reshape(-1)
    xs_sorted = _dispatch(dest.reshape(n_all // ROUTER_TM, 1, ROUTER_TM * TOP_K), h2, n_blocks * MOE_BM)
    ys_sorted = _moe_gmm(blocks, row_end, xs_sorted, w_gate[0], w_up[0], w_down[0])
    dest_c = dest.reshape(n_all // COMBINE_TM, 1, COMBINE_TM * TOP_K)
    w_c = top_w[:, :TOP_K].reshape(n_all // COMBINE_TM, 1, COMBINE_TM * TOP_K)
    shared = (ws_gate[0].astype(BF16), ws_up[0].astype(BF16), ws_down[0].astype(BF16))
    fnorm = final_norm.reshape(1, -1)
    y_prompt = _combine(dest_c, w_c, x1p, h2, modp[:, :, 5], shared, fnorm, ys_sorted, 0)
    y_sample = _combine(dest_c, w_c, x1s, h2, mods[:, :, 5], shared, fnorm, ys_sorted, nbp * t // COMBINE_TM)

    wb = win_buf.shape[1]
    win_p = kvw[:, t - min(WINDOW, t):]
    win_s = jnp.concatenate([win_buf, kvw_s.reshape(nbs, ts, 2 * KV_W)], axis=1)[:, -wb:]
    kv_shape = (4, NSA_KV_HEADS, NSA_HEAD_DIM)
    win_shape = (2, NSA_KV_HEADS, NSA_HEAD_DIM)
    return (y_prompt,
            y_sample.reshape(nbs, ts, D_MODEL),
            kv4.reshape(1, nbp, t, *kv_shape),
            win_p.reshape(1, nbp, -1, *win_shape),
            hg_state_p[None],
            kv4_s.reshape(1, nbs, ts, *kv_shape),
            win_s.reshape(1, nbs, wb, *win_shape),
            hg_state_s[None])
```

```python
import functools

import jax
import jax.numpy as jnp
import numpy as np
from jax import lax
from jax.experimental import pallas as pl
from jax.experimental.pallas import tpu as pltpu

F32 = jnp.float32
BF16 = jnp.bfloat16

D_MODEL = 1024
HG_WIDTH = 512
HG_HEADS = 4
HG_DK = 128
HG_CHUNK = 32
NSA_WIDTH = 512
NSA_HEADS = 8
NSA_HEAD_DIM = 64
NSA_KV_HEADS = 2
NSA_GROUP = 4
KV_W = 128
CMP_BLOCK = 32
CMP_STRIDE = 16
CMP_HIDDEN = 256
SEL_BLOCK = 64
SEL_TOPN = 16
WINDOW = 512
Q_BLOCK = 128
N_EXPERTS = 256
TOP_K = 8
N_GROUPS = 8
TOPK_GROUPS = 4
MOE_D_FF = 256
ROUTED_SCALE = 2.5
RMS_EPS = 1e-6
PAGE_SIZE = 128
IN_COLS = 4 * HG_WIDTH + NSA_WIDTH + 4 * KV_W + 2 * KV_W + 3 * NSA_HEADS

LANES = 128
SUBLANES = 8
TOK_ROWS = D_MODEL // LANES
VMEM_LIMIT = 56 * 1024 * 1024

QPAD_W = NSA_HEADS * LANES
GZ_PAD = LANES
INP_COLS = 4 * HG_WIDTH + QPAD_W + 4 * KV_W + 2 * KV_W + GZ_PAD

NEG = -1e30
PAGES_PER_STEP = 32
MOE_BM = 256
ROUTER_TM = 256
COMBINE_TM = 128


def _cparams(sem):
    return pltpu.CompilerParams(dimension_semantics=sem, vmem_limit_bytes=VMEM_LIMIT)


def _dot(a, b):
    return jnp.dot(a, b, preferred_element_type=F32)


def _dot_nt(a, b):
    return lax.dot_general(a, b, (((1,), (1,)), ((), ())), preferred_element_type=F32)


def _dot_tn(a, b):
    return lax.dot_general(a, b, (((0,), (0,)), ((), ())), preferred_element_type=F32)


def _rms(x, g):
    return x * lax.rsqrt(jnp.mean(x * x, axis=-1, keepdims=True) + RMS_EPS) * g


def _silu(x):
    return x * jax.nn.sigmoid(x)


Q_SCALE = NSA_HEAD_DIM ** -0.5 * 1.4426950408889634


def _masked_softmax(s, valid):
    s = jnp.where(valid, s, NEG)
    m = jnp.max(s, axis=1, keepdims=True)
    p = jnp.exp2(s - m) * valid.astype(F32)
    return p / jnp.maximum(jnp.sum(p, axis=1, keepdims=True), 1e-30)


def _topk_mask(v, k):
    lane = lax.broadcasted_iota(jnp.int32, v.shape, 1).astype(F32)
    sel = jnp.zeros(v.shape, F32)
    for _ in range(k):
        m = jnp.max(v, axis=1, keepdims=True)
        idx = jnp.min(jnp.where(v == m, lane, 1e9), axis=1, keepdims=True)
        pick = lane == idx
        sel = jnp.where(pick, 1.0, sel)
        v = jnp.where(pick, 3.0 * NEG, v)
    return sel


def _mod_spec(mod, tm):
    if mod.shape[1] == 1:
        return pl.BlockSpec((1, 1, D_MODEL), lambda b, i: (b, 0, 0))
    return pl.BlockSpec((1, tm, D_MODEL), lambda b, i: (b, i, 0))


def _load_tok_tiles(ref, n_tok):
    return jnp.concatenate([ref[pl.ds(s, n_tok, stride=TOK_ROWS), :] for s in range(TOK_ROWS)], axis=1)


def _store_tok_tiles(ref, val, n_tok):
    for s in range(TOK_ROWS):
        ref[pl.ds(s, n_tok, stride=TOK_ROWS), :] = val[:, s * LANES:(s + 1) * LANES]


def _ada_kernel(c_ref, w_ref, b_ref, o_ref):
    s = _silu(c_ref[...]).astype(BF16)
    o_ref[...] = _dot(s, w_ref[...].astype(BF16)) + b_ref[...]


def _ada(c_all, w_ada, b_ada):
    n = c_all.shape[0]
    return pl.pallas_call(
        _ada_kernel,
        grid=(6,),
        in_specs=[pl.BlockSpec((n, D_MODEL), lambda j: (0, 0)),
                  pl.BlockSpec((D_MODEL, D_MODEL), lambda j: (0, j)),
                  pl.BlockSpec((1, D_MODEL), lambda j: (0, j))],
        out_specs=pl.BlockSpec((n, D_MODEL), lambda j: (0, j)),
        out_shape=jax.ShapeDtypeStruct((n, 6 * D_MODEL), F32),
        compiler_params=_cparams(("arbitrary",)),
        name="ada_mod",
    )(c_all, w_ada, b_ada.reshape(1, -1))


def _inproj_kernel(x_ref, sc_ref, sh_ref, g_ref, w_ref,
                   hg_ref, q_ref, kv4_ref, kvw_ref, gate_ref, kvsel_ref, kvwb_ref):
    h = _rms(x_ref[0], g_ref[...]) * (1.0 + sc_ref[0]) + sh_ref[0]
    z = _dot(h.astype(BF16), w_ref[...])
    c0 = 4 * HG_WIDTH
    hg_ref[0] = z[:, :c0]
    q_ref[0] = (z[:, c0:c0 + QPAD_W] * Q_SCALE).astype(BF16)
    c1 = c0 + QPAD_W
    kv4 = z[:, c1:c1 + 4 * KV_W]
    kv4_ref[0] = kv4
    kvsel_ref[0] = kv4[:, 2 * KV_W:].astype(BF16)
    c2 = c1 + 4 * KV_W
    kvw = z[:, c2:c2 + 2 * KV_W]
    kvw_ref[0] = kvw
    kvwb_ref[0] = kvw.astype(BF16)
    gate_ref[0] = jax.nn.sigmoid(z[:, c2 + 2 * KV_W:])


def _inproj(x, scale, shift, g_norm, w_pad, tm):
    nb, t, _ = x.shape
    mod_spec = _mod_spec(scale, tm)
    widths = [(4 * HG_WIDTH, F32), (QPAD_W, BF16), (4 * KV_W, F32), (2 * KV_W, F32), (GZ_PAD, F32),
              (2 * KV_W, BF16), (2 * KV_W, BF16)]
    return pl.pallas_call(
        _inproj_kernel,
        grid=(nb, t // tm),
        in_specs=[pl.BlockSpec((1, tm, D_MODEL), lambda b, i: (b, i, 0)),
                  mod_spec, mod_spec,
                  pl.BlockSpec((1, D_MODEL), lambda b, i: (0, 0)),
                  pl.BlockSpec((D_MODEL, INP_COLS), lambda b, i: (0, 0))],
        out_specs=[pl.BlockSpec((1, tm, w), lambda b, i: (b, i, 0)) for w, _ in widths],
        out_shape=[jax.ShapeDtypeStruct((nb, t, w), dt) for w, dt in widths],
        compiler_params=_cparams(("arbitrary", "arbitrary")),
        name="in_proj",
    )(x, scale, shift, g_norm.reshape(1, -1), w_pad)


def _pad_w_in(w_in):
    c0 = 4 * HG_WIDTH
    wq = w_in[:, c0:c0 + NSA_WIDTH].reshape(D_MODEL, NSA_HEADS, NSA_HEAD_DIM)
    zeros = jnp.zeros_like(wq)
    lo = jnp.concatenate([wq, zeros], axis=-1)
    hi = jnp.concatenate([zeros, wq], axis=-1)
    grp = (jnp.arange(NSA_HEADS) // NSA_GROUP)[None, :, None]
    wq_pad = jnp.where(grp == 0, lo, hi).reshape(D_MODEL, QPAD_W)
    c1 = c0 + NSA_WIDTH
    rest = w_in[:, c1:c1 + 6 * KV_W]
    gz = jnp.pad(w_in[:, c1 + 6 * KV_W:], ((0, 0), (0, GZ_PAD - 3 * NSA_HEADS)))
    return jnp.concatenate([w_in[:, :c0], wq_pad, rest, gz], axis=1).astype(BF16)


def _hgrn_kernel(q_ref, f_ref, v_ref, gt_ref, lb_ref, s0_ref, gn_ref, o_ref, s_out_ref, st_scr,
                 *, chunk, n_chunks):
    i = pl.program_id(1)

    @pl.when(i == 0)
    def _():
        for h in range(HG_HEADS):
            st_scr[h] = s0_ref[0, h].T

    lbr = lb_ref[...]
    e = jnp.exp(lbr - jnp.max(lbr, axis=0, keepdims=True))
    lb_all = e[0:1] / jnp.sum(e, axis=0, keepdims=True)
    row = lax.broadcasted_iota(jnp.int32, (chunk, HG_DK), 0)
    causal = (lax.broadcasted_iota(jnp.int32, (chunk, chunk), 0)
              >= lax.broadcasted_iota(jnp.int32, (chunk, chunk), 1))
    st = [st_scr[h] for h in range(HG_HEADS)]
    for c in range(n_chunks):
        sl = pl.ds(c * chunk, chunk)
        for h in range(HG_HEADS):
            hs = slice(h * HG_DK, (h + 1) * HG_DK)
            lb = lb_all[:, hs]
            z = f_ref[0, sl, hs]
            log_f = jnp.log(lb + (1.0 - lb) * jax.nn.sigmoid(z))
            kk = (1.0 - lb) * jax.nn.sigmoid(-z)
            a = log_f
            s = 1
            while s < chunk:
                a = a + jnp.where(row >= s, pltpu.roll(a, s, 0), 0.0)
                s *= 2
            qt = (q_ref[0, sl, hs] * jnp.exp(a)).astype(BF16)
            kt = (kk * jnp.exp(-a)).astype(BF16)
            v = v_ref[0, sl, hs].astype(BF16)
            att = jnp.where(causal, _dot_nt(qt, kt), 0.0)
            o = _dot(att.astype(BF16), v) + _dot_nt(qt, st[h].astype(BF16))
            a_end = a[chunk - 1:chunk, :]
            kd = (kk * jnp.exp(a_end - a)).astype(BF16)
            st[h] = st[h] * jnp.exp(a_end) + _dot_tn(v, kd)
            o = _rms(o, gn_ref[...]) * _silu(gt_ref[0, sl, hs])
            o_ref[0, sl, hs] = o.astype(o_ref.dtype)
    for h in range(HG_HEADS):
        st_scr[h] = st[h]

    @pl.when(i == pl.num_programs(1) - 1)
    def _():
        for h in range(HG_HEADS):
            s_out_ref[0, h] = st[h].T


def _hgrn(hg, hg_lb, s0, g_norm, tc, chunk):
    nb, t, _ = hg.shape
    part = lambda k: pl.BlockSpec((1, tc, HG_WIDTH), lambda b, i: (b, i, k))
    st_spec = pl.BlockSpec((1, HG_HEADS, HG_DK, HG_DK), lambda b, i: (b, 0, 0, 0))
    return pl.pallas_call(
        functools.partial(_hgrn_kernel, chunk=chunk, n_chunks=tc // chunk),
        grid=(nb, t // tc),
        in_specs=[part(0), part(1), part(2), part(3),
                  pl.BlockSpec(hg_lb.shape, lambda b, i: (0, 0)),
                  st_spec,
                  pl.BlockSpec((1, HG_DK), lambda b, i: (0, 0))],
        out_specs=[pl.BlockSpec((1, tc, HG_WIDTH), lambda b, i: (b, i, 0)), st_spec],
        out_shape=[jax.ShapeDtypeStruct((nb, t, HG_WIDTH), BF16),
                   jax.ShapeDtypeStruct((nb, HG_HEADS, HG_DK, HG_DK), F32)],
        scratch_shapes=[pltpu.VMEM((HG_HEADS, HG_DK, HG_DK), F32)],
        compiler_params=_cparams(("arbitrary", "arbitrary")),
        name="hgrn2",
    )(hg, hg, hg, hg, hg_lb, s0, g_norm.reshape(1, -1))


def _gelu_tanh(x):
    return 0.5 * x * (1.0 + jnp.tanh(0.7978845608028654 * (x + 0.044715 * x * x * x)))


def _page_copies(pt_ref, cache_ref, buf, sem, b, s, slot, n_pages, pps, col0, tail, transposed):
    copies = []
    base = s * pps
    nxt = pt_ref[b, jnp.minimum(base + pps, n_pages - 1)]
    for br in range(2):
        cols = pl.ds(col0 + br * KV_W, KV_W)
        for i in range(pps):
            pg = pt_ref[b, base + i]
            if transposed:
                copies.append(pltpu.make_async_copy(
                    cache_ref.at[pg, cols, :],
                    buf.at[slot, br, :, pl.ds(i * PAGE_SIZE, PAGE_SIZE)], sem.at[slot]))
            else:
                copies.append(pltpu.make_async_copy(
                    cache_ref.at[pg, :, cols],
                    buf.at[slot, br, pl.ds(i * PAGE_SIZE, PAGE_SIZE), :], sem.at[slot]))
        if tail and transposed:
            copies.append(pltpu.make_async_copy(
                cache_ref.at[nxt, cols, :],
                buf.at[slot, br, :, pl.ds(pps * PAGE_SIZE, PAGE_SIZE)], sem.at[slot]))
        elif tail:
            copies.append(pltpu.make_async_copy(
                cache_ref.at[nxt, pl.ds(0, CMP_STRIDE), cols],
                buf.at[slot, br, pl.ds(pps * PAGE_SIZE, CMP_STRIDE), :], sem.at[slot]))
    return copies


def _stream_pages(pt_ref, cache_ref, buf, sem, n_pages, n_steps, pps, col0, tail, transposed):
    b = pl.program_id(0)
    s = pl.program_id(1)
    n = b * n_steps + s
    total = pl.num_programs(0) * n_steps
    slot = n % 2
    args = (n_pages, pps, col0, tail, transposed)

    @pl.when(n == 0)
    def _():
        for cp in _page_copies(pt_ref, cache_ref, buf, sem, b, s, slot, *args):
            cp.start()

    @pl.when(n + 1 < total)
    def _():
        n1 = n + 1
        for cp in _page_copies(pt_ref, cache_ref, buf, sem, n1 // n_steps, n1 % n_steps, 1 - slot, *args):
            cp.start()

    for cp in _page_copies(pt_ref, cache_ref, buf, sem, b, s, slot, *args):
        cp.wait()
    return slot


def _compress_kernel(pt_ref, cache_ref, pe_ref, w1_ref, b1_ref, w2_ref, kc_ref, vc_ref, buf, sem, *rowbuf,
                     n_pages, n_steps, pps, transposed):
    groups = pps * PAGE_SIZE // CMP_STRIDE

    def compress(read_rows):
        low_half = lax.broadcasted_iota(jnp.int32, (groups, KV_W), 1) < NSA_HEAD_DIM
        for br, out_ref in ((0, kc_ref), (1, vc_ref)):
            acc0 = jnp.zeros((groups, CMP_HIDDEN), F32)
            acc1 = jnp.zeros((groups, CMP_HIDDEN), F32)
            for j in range(CMP_BLOCK // 2):
                a = read_rows(br, 2 * j) + pe_ref[br, 2 * j:2 * j + 1, :]
                b = pltpu.roll(read_rows(br, 2 * j + 1) + pe_ref[br, 2 * j + 1:2 * j + 2, :], NSA_HEAD_DIM, 1)
                acc0 = acc0 + _dot(jnp.where(low_half, a, b).astype(BF16), w1_ref[br, j, 0])
                acc1 = acc1 + _dot(jnp.where(low_half, b, a).astype(BF16), w1_ref[br, j, 1])
            hid = _gelu_tanh(jnp.concatenate([acc0, acc1], axis=1) + b1_ref[br]).astype(BF16)
            out_ref[0] = _dot(hid, w2_ref[br]).astype(out_ref.dtype)

    if transposed:
        rows_ref, = rowbuf
        n = pl.program_id(0) * n_steps + pl.program_id(1)
        total = pt_ref.shape[0] * n_steps
        copies = lambda m: _page_copies(pt_ref, cache_ref, buf, sem, m // n_steps, m % n_steps, m % 2,
                                        n_pages, pps, 0, True, True)

        def to_rows(stage, half):
            for br in range(2):
                for i in range(pps + 1):
                    n_rows = PAGE_SIZE if i < pps else CMP_STRIDE
                    page_t = buf[stage, br, :, i * PAGE_SIZE:(i + 1) * PAGE_SIZE]
                    rows_ref[half, br, i * PAGE_SIZE:i * PAGE_SIZE + n_rows, :] = page_t.T[:n_rows]

        @pl.when(n == 0)
        def _():
            for cp in copies(0):
                cp.start()
            for cp in copies(0):
                cp.wait()
            to_rows(0, 0)
            if total > 1:
                for cp in copies(1):
                    cp.start()

        @pl.when(n + 1 < total)
        def _():
            for cp in copies(n + 1):
                cp.wait()

        @pl.when(n + 2 < total)
        def _():
            for cp in copies(n + 2):
                cp.start()

        for parity in range(2):
            @pl.when(n % 2 == parity)
            def _(parity=parity):
                if total > 1:
                    to_rows(1 - parity, 1 - parity)
                compress(lambda br, l: rows_ref[parity, br, pl.ds(l, groups, stride=CMP_STRIDE), :])
    else:
        slot = _stream_pages(pt_ref, cache_ref, buf, sem, n_pages, n_steps, pps, 0, True, False)
        compress(lambda br, l: buf[slot, br, pl.ds(l, groups, stride=CMP_STRIDE), :])


def _compress(page_table, cache, pe2, w1cat, b1cat, w2bd, pps=PAGES_PER_STEP, transposed=False):
    nb, n_pages = page_table.shape
    n_steps = n_pages // pps
    groups = pps * PAGE_SIZE // CMP_STRIDE
    rows = pps * PAGE_SIZE + CMP_STRIDE
    const = lambda shape: pl.BlockSpec(shape, lambda b, s, pt: (0,) * len(shape))
    out_spec = pl.BlockSpec((1, groups, KV_W), lambda b, s, pt: (b, s, 0))
    out_sds = jax.ShapeDtypeStruct((nb, n_steps * groups, KV_W), BF16)
    if transposed:
        stage = [pltpu.VMEM((2, 2, KV_W, (pps + 1) * PAGE_SIZE), F32), pltpu.SemaphoreType.DMA((2,)),
                 pltpu.VMEM((2, 2, rows, KV_W), F32)]
    else:
        stage = [pltpu.VMEM((2, 2, rows, KV_W), F32), pltpu.SemaphoreType.DMA((2,))]
    return pl.pallas_call(
        functools.partial(_compress_kernel, n_pages=n_pages, n_steps=n_steps, pps=pps, transposed=transposed),
        grid_spec=pltpu.PrefetchScalarGridSpec(
            num_scalar_prefetch=1,
            grid=(nb, n_steps),
            in_specs=[pl.BlockSpec(memory_space=pl.ANY),
                      const((2, CMP_BLOCK, KV_W)),
                      const((2, CMP_BLOCK // 2, 2, KV_W, CMP_HIDDEN)),
                      const((2, 1, 2 * CMP_HIDDEN)),
                      const((2, 2 * CMP_HIDDEN, KV_W))],
            out_specs=[out_spec, out_spec],
            scratch_shapes=stage),
        out_shape=[out_sds, out_sds],
        compiler_params=_cparams(("arbitrary", "arbitrary")),
        name="nsa_compress",
    )(page_table, cache, pe2, w1cat, b1cat, w2bd)


def _compress_weights(cmp_pe, cmp_w1, cmp_b1, cmp_w2):
    pe2 = jnp.concatenate([cmp_pe, cmp_pe], axis=-1)
    w1 = cmp_w1.reshape(2, CMP_BLOCK // 2, 2, NSA_HEAD_DIM, CMP_HIDDEN)
    even_odd = w1.reshape(2, CMP_BLOCK // 2, 2 * NSA_HEAD_DIM, CMP_HIDDEN)
    odd_even = w1[:, :, ::-1].reshape(2, CMP_BLOCK // 2, 2 * NSA_HEAD_DIM, CMP_HIDDEN)
    w1cat = jnp.stack([even_odd, odd_even], axis=2).astype(BF16)
    b1cat = jnp.concatenate([cmp_b1, cmp_b1], axis=-1)[:, None, :]
    z2 = jnp.zeros_like(cmp_w2)
    w2bd = jnp.concatenate([jnp.concatenate([cmp_w2, z2], axis=-1),
                            jnp.concatenate([z2, cmp_w2], axis=-1)], axis=1).astype(BF16)
    return pe2, w1cat, b1cat, w2bd


def _overlap_matrix(n_cmp, n_sel):
    cs = lax.broadcasted_iota(jnp.int32, (n_cmp, n_sel), 0) * CMP_STRIDE
    ss = lax.broadcasted_iota(jnp.int32, (n_cmp, n_sel), 1) * SEL_BLOCK
    return ((cs < ss + SEL_BLOCK) & (cs + CMP_BLOCK > ss)).astype(BF16)


SEL_CHUNK = 512
SPREAD_KEYS = 1024
WIN_SPAN = WINDOW + Q_BLOCK


def _nsa_prompt_kernel(q_ref, gt_ref, kc_ref, vc_ref, ks_ref, vs_ref, kw_ref, vw_ref, ex_ref, o_ref,
                       m_scr, acc_scr, chosen_scr, oc_scr, sa_scr, sb_scr, *, n_cmp, n_sel):
    j = pl.program_id(1)
    q0 = j * Q_BLOCK
    tok = lax.broadcasted_iota(jnp.int32, (Q_BLOCK, 1), 0) + q0
    tok4 = jnp.concatenate([tok] * NSA_GROUP, axis=0)
    n_chunks = j // (SEL_CHUNK // Q_BLOCK) + 1
    key_in_span = lax.broadcasted_iota(jnp.int32, (1, SPREAD_KEYS), 1)

    def load_q(g):
        return jnp.concatenate([q_ref[0, :, (NSA_GROUP * g + jh) * LANES:(NSA_GROUP * g + jh + 1) * LANES]
                                for jh in range(NSA_GROUP)], axis=0)

    def softmax_av(s, valid, v):
        s = jnp.where(valid, s, NEG)
        tiles = [s[:, i * LANES:(i + 1) * LANES] for i in range(s.shape[1] // LANES)]
        m = jnp.max(functools.reduce(jnp.maximum, tiles), axis=1, keepdims=True)
        p = jnp.where(valid, jnp.exp2(s - m), 0.0)
        acc = _dot(p.astype(BF16), jnp.concatenate([v, jnp.ones(v.shape, BF16)], axis=1))
        inv = 1.0 / jnp.maximum(acc[:, KV_W:], 1e-30)
        return p, acc[:, :KV_W] * inv, inv

    ov = _overlap_matrix(n_cmp, n_sel)
    cend = lax.broadcasted_iota(jnp.int32, (1, n_cmp), 1) * CMP_STRIDE + (CMP_BLOCK - 1)
    ws = pl.multiple_of(jnp.maximum(q0 - WINDOW, 0), Q_BLOCK)
    wpos = ws + lax.broadcasted_iota(jnp.int32, (1, WIN_SPAN), 1)
    d = tok4 - wpos
    in_window = (d >= 0) & (d < WINDOW)
    gates = gt_ref[0]
    imps = []
    for g in range(NSA_KV_HEADS):
        q = load_q(g)
        p, o_c, inv = softmax_av(_dot_nt(q, kc_ref[0]), cend <= tok4, vc_ref[0])
        _, o_w, _ = softmax_av(_dot_nt(q, kw_ref[0, pl.ds(ws, WIN_SPAN), :]), in_window,
                               vw_ref[0, pl.ds(ws, WIN_SPAN), :])
        psum = jnp.zeros((Q_BLOCK, n_cmp), F32)
        for jh in range(NSA_GROUP):
            h = NSA_GROUP * g + jh
            r = slice(jh * Q_BLOCK, (jh + 1) * Q_BLOCK)
            psum = psum + p[r] * jnp.concatenate([inv[r]] * (n_cmp // LANES), axis=1)
            oc_scr[g, r, :] = gates[:, 3 * h:3 * h + 1] * o_c[r] + gates[:, 3 * h + 2:3 * h + 3] * o_w[r]
        imps.append(_dot(psum.astype(BF16), ov))

    blk = lax.broadcasted_iota(jnp.int32, (1, n_sel), 1)
    cur = lax.shift_right_logical(tok, 6)
    forced = (blk == 0) | (blk == cur) | (blk == cur - 1)
    free = (blk * SEL_BLOCK <= tok) & jnp.logical_not(forced)
    forced2 = jnp.concatenate([forced] * NSA_KV_HEADS, axis=0)
    free2 = jnp.concatenate([free] * NSA_KV_HEADS, axis=0)
    best = _topk_mask(jnp.where(free2, jnp.concatenate(imps, axis=0), NEG), SEL_TOPN - 3)
    sel2 = jnp.where(forced2, 1.0, best).astype(BF16)

    ones_blk = jnp.ones((SEL_CHUNK, KV_W), BF16)

    for g in range(NSA_KV_HEADS):
        sel = sel2[g * Q_BLOCK:(g + 1) * Q_BLOCK]
        m_scr[...] = jnp.full(m_scr.shape, NEG, F32)
        acc_scr[...] = jnp.zeros(acc_scr.shape, F32)

        def spread(i, carry):
            c0 = pl.multiple_of(i * SPREAD_KEYS, SPREAD_KEYS)
            keep = (_dot(sel, ex_ref[:, pl.ds(c0, SPREAD_KEYS)]) > 0.5) & (key_in_span + c0 <= tok)
            chosen_scr[:, pl.ds(c0, SPREAD_KEYS)] = jnp.where(keep, 0.0, NEG)
            return carry

        lax.fori_loop(0, (n_chunks * SEL_CHUNK + SPREAD_KEYS - 1) // SPREAD_KEYS, spread, 0)

        heads = range(NSA_GROUP)
        rows = [pl.ds(jh * Q_BLOCK, Q_BLOCK) for jh in heads]

        def stage_scores(c, buf):
            k0 = pl.multiple_of(c * SEL_CHUNK, SEL_CHUNK)
            bias = chosen_scr[:, pl.ds(k0, SEL_CHUNK)]
            kblk = ks_ref[0, pl.ds(k0, SEL_CHUNK), :]
            for jh in heads:
                h = NSA_GROUP * g + jh
                buf[rows[jh], :] = _dot_nt(q_ref[0, :, h * LANES:(h + 1) * LANES], kblk) + bias

        def accumulate(c, buf):
            k0 = pl.multiple_of(c * SEL_CHUNK, SEL_CHUNK)
            vext = jnp.concatenate([vs_ref[0, pl.ds(k0, SEL_CHUNK), :], ones_blk], axis=1)
            tiles = [[buf[r, i * LANES:(i + 1) * LANES] for i in range(SEL_CHUNK // LANES)] for r in rows]
            m_old = [m_scr[r, :] for r in rows]
            m_new = [jnp.maximum(mo, jnp.max(functools.reduce(jnp.maximum, t), axis=1, keepdims=True))
                     for mo, t in zip(m_old, tiles)]
            probs = [jnp.concatenate([jnp.exp2(x - mn) for x in t], axis=1).astype(BF16)
                     for mn, t in zip(m_new, tiles)]
            pv = [_dot(p, vext) for p in probs]
            for r, mo, mn, y in zip(rows, m_old, m_new, pv):
                alpha = jnp.exp2(mo - mn)
                acc_scr[r, :] = jnp.concatenate([alpha, alpha], axis=1) * acc_scr[r, :] + y
                m_scr[r, :] = mn

        stage_scores(0, sa_scr)

        def body(i, carry):
            stage_scores(2 * i + 1, sb_scr)
            accumulate(2 * i, sa_scr)
            stage_scores(jnp.minimum(2 * i + 2, n_chunks - 1), sa_scr)
            accumulate(2 * i + 1, sb_scr)
            return carry

        lax.fori_loop(0, n_chunks // 2, body, 0)

        @pl.when(n_chunks % 2 == 1)
        def _():
            accumulate(n_chunks - 1, sa_scr)
        o_s = acc_scr[:, :KV_W] / jnp.maximum(acc_scr[:, KV_W:], 1e-30)

        for jh in range(NSA_GROUP):
            h = NSA_GROUP * g + jh
            r = slice(jh * Q_BLOCK, (jh + 1) * Q_BLOCK)
            o = oc_scr[g, r, :] + gates[:, 3 * h + 1:3 * h + 2] * o_s[r]
            o_ref[0, :, h * LANES:(h + 1) * LANES] = o.astype(o_ref.dtype)


def _nsa_prompt(qpad, gates, kc, vc, kvsel, kvwb):
    nb, t, _ = qpad.shape
    assert t % SPREAD_KEYS == 0 and SPREAD_KEYS == 2 * SEL_CHUNK and t >= WIN_SPAN
    n_cmp = kc.shape[1]
    n_sel = t // SEL_BLOCK
    full = lambda w, k: pl.BlockSpec((1, t, w), lambda b, j: (b, 0, k))
    t_pad = -(-t // SPREAD_KEYS) * SPREAD_KEYS
    expand = jnp.asarray(np.arange(n_sel)[:, None] == (np.arange(t_pad)[None, :] // SEL_BLOCK), BF16)
    return pl.pallas_call(
        functools.partial(_nsa_prompt_kernel, n_cmp=n_cmp, n_sel=n_sel),
        grid=(nb, t // Q_BLOCK),
        in_specs=[pl.BlockSpec((1, Q_BLOCK, QPAD_W), lambda b, j: (b, j, 0)),
                  pl.BlockSpec((1, Q_BLOCK, GZ_PAD), lambda b, j: (b, j, 0)),
                  pl.BlockSpec((1, n_cmp, KV_W), lambda b, j: (b, 0, 0)),
                  pl.BlockSpec((1, n_cmp, KV_W), lambda b, j: (b, 0, 0)),
                  full(KV_W, 0), full(KV_W, 1), full(KV_W, 0), full(KV_W, 1),
                  pl.BlockSpec((n_sel, t_pad), lambda b, j: (0, 0))],
        out_specs=pl.BlockSpec((1, Q_BLOCK, QPAD_W), lambda b, j: (b, j, 0)),
        out_shape=jax.ShapeDtypeStruct((nb, t, QPAD_W), BF16),
        scratch_shapes=[pltpu.VMEM((NSA_GROUP * Q_BLOCK, LANES), F32),
                        pltpu.VMEM((NSA_GROUP * Q_BLOCK, 2 * KV_W), F32),
                        pltpu.VMEM((Q_BLOCK, t_pad), F32),
                        pltpu.VMEM((NSA_KV_HEADS, NSA_GROUP * Q_BLOCK, KV_W), F32),
                        pltpu.VMEM((NSA_GROUP * Q_BLOCK, SEL_CHUNK), F32),
                        pltpu.VMEM((NSA_GROUP * Q_BLOCK, SEL_CHUNK), F32)],
        compiler_params=_cparams(("arbitrary", "arbitrary")),
        name="nsa_prompt",
    )(qpad, gates, kc, vc, kvsel, kvsel, kvwb, kvwb, expand)


def _nsa_sample_a_kernel(q_ref, g_ref, kc_ref, vc_ref, wb_ref, nw_ref, ocw_ref, sel_ref,
                         *, past_len, n_tok, n_sel, n_sel_pad):
    q = q_ref[0]
    rows = q.shape[0]
    n_cmp = kc_ref.shape[1]
    t_row = lax.broadcasted_iota(jnp.int32, (rows, 1), 0) & (n_tok - 1)
    qpos = past_len + t_row
    cend = lax.broadcasted_iota(jnp.int32, (1, n_cmp), 1) * CMP_STRIDE + (CMP_BLOCK - 1)
    p_c = _masked_softmax(_dot_nt(q, kc_ref[0]), cend <= qpos)
    o_c = _dot(p_c.astype(BF16), vc_ref[0])

    per_grp = NSA_GROUP * n_tok
    psum = jnp.concatenate(
        [sum(p_c[g * per_grp + jh * n_tok:g * per_grp + (jh + 1) * n_tok] for jh in range(NSA_GROUP))
         for g in range(NSA_KV_HEADS)], axis=0)
    imp = _dot(psum.astype(BF16), _overlap_matrix(n_cmp, n_sel_pad))
    blk = lax.broadcasted_iota(jnp.int32, (1, n_sel_pad), 1)
    tq = past_len + (lax.broadcasted_iota(jnp.int32, (NSA_KV_HEADS * n_tok, 1), 0) & (n_tok - 1))
    cur = lax.shift_right_logical(tq, 6)
    forced = (blk == 0) | (blk == cur) | (blk == cur - 1)
    allowed = blk * SEL_BLOCK <= tq
    v = jnp.where(forced, -NEG, jnp.where(allowed, imp, NEG))
    sel_ref[0] = _topk_mask(jnp.where(blk < n_sel, v, 2.0 * NEG), SEL_TOPN)

    wb = wb_ref.shape[1]
    kw = wb_ref[0, :, 0:KV_W].astype(BF16)
    vw = wb_ref[0, :, KV_W:2 * KV_W].astype(BF16)
    kn = nw_ref[0, :, 0:KV_W].astype(BF16)
    vn = nw_ref[0, :, KV_W:2 * KV_W].astype(BF16)
    i1 = lax.broadcasted_iota(jnp.int32, (1, wb), 1)
    d1 = t_row + wb - i1
    valid1 = (d1 >= 0) & (d1 < WINDOW) & (past_len - wb + i1 >= 0)
    i2 = lax.broadcasted_iota(jnp.int32, (1, nw_ref.shape[1]), 1)
    d2 = t_row - i2
    valid2 = (d2 >= 0) & (d2 < WINDOW) & (i2 < n_tok)
    s1 = jnp.where(valid1, _dot_nt(q, kw), NEG)
    s2 = jnp.where(valid2, _dot_nt(q, kn), NEG)
    m = jnp.maximum(jnp.max(s1, axis=1, keepdims=True), jnp.max(s2, axis=1, keepdims=True))
    p1 = jnp.exp2(s1 - m) * valid1.astype(F32)
    p2 = jnp.exp2(s2 - m) * valid2.astype(F32)
    den = jnp.maximum(jnp.sum(p1, axis=1, keepdims=True) + jnp.sum(p2, axis=1, keepdims=True), 1e-30)
    o_w = (_dot(p1.astype(BF16), vw) + _dot(p2.astype(BF16), vn)) / den
    g = g_ref[0]
    ocw_ref[0] = g[:, 0:1] * o_c + g[:, 2:3] * o_w


def _nsa_sample_a(q_rows, g_rows, kc, vc, win_buf, new_win, *, past_len, n_tok):
    nb, rows, _ = q_rows.shape
    n_sel = -(-(past_len + n_tok) // SEL_BLOCK)
    n_sel_pad = -(-n_sel // LANES) * LANES
    blk3 = lambda a: pl.BlockSpec((1,) + a.shape[1:], lambda b: (b, 0, 0))
    return pl.pallas_call(
        functools.partial(_nsa_sample_a_kernel, past_len=past_len, n_tok=n_tok, n_sel=n_sel, n_sel_pad=n_sel_pad),
        grid=(nb,),
        in_specs=[blk3(q_rows), blk3(g_rows), blk3(kc), blk3(vc), blk3(win_buf), blk3(new_win)],
        out_specs=[pl.BlockSpec((1, rows, KV_W), lambda b: (b, 0, 0)),
                   pl.BlockSpec((1, NSA_KV_HEADS * n_tok, n_sel_pad), lambda b: (b, 0, 0))],
        out_shape=[jax.ShapeDtypeStruct((nb, rows, KV_W), F32),
                   jax.ShapeDtypeStruct((nb, NSA_KV_HEADS * n_tok, n_sel_pad), F32)],
        compiler_params=_cparams(("arbitrary",)),
        name="nsa_sample_a",
    )(q_rows, g_rows, kc, vc, win_buf, new_win)


def _nsa_sample_b_kernel(pt_ref, cache_ref, q_ref, g_ref, sel_ref, seln_ref, ns_ref, ocw_ref, ex_ref, o_ref,
                         buf, sem, m_scr, l_scr, acc_scr, *, n_pages, n_steps, pps, n_tok):
    s = pl.program_id(1)
    slot = _stream_pages(pt_ref, cache_ref, buf, sem, n_pages, n_steps, pps, 2 * KV_W, False, True)
    q = q_ref[0]
    rows = q.shape[0]

    @pl.when(s == 0)
    def _():
        m_scr[...] = jnp.full(m_scr.shape, NEG, F32)
        l_scr[...] = jnp.zeros(l_scr.shape, F32)
        acc_scr[...] = jnp.zeros(acc_scr.shape, F32)

    def update(scores, msk, times_v):
        sc = jnp.where(msk, scores, NEG)
        m_old = m_scr[...]
        m_new = jnp.maximum(m_old, jnp.max(sc, axis=1, keepdims=True))
        p = jnp.exp2(sc - m_new) * msk.astype(F32)
        alpha = jnp.exp2(m_old - m_new)
        l_scr[...] = alpha * l_scr[...] + jnp.sum(p, axis=1, keepdims=True)
        acc_scr[...] = alpha * acc_scr[...] + times_v(p.astype(BF16))
        m_scr[...] = m_new

    chosen = _dot(sel_ref[0, 0], ex_ref[...]) > 0.5
    update(_dot(q, buf[slot, 0].astype(BF16)), chosen, lambda p: _dot_nt(p, buf[slot, 1].astype(BF16)))

    @pl.when(s == n_steps - 1)
    def _():
        t_row = lax.broadcasted_iota(jnp.int32, (rows, 1), 0) & (n_tok - 1)
        i2 = lax.broadcasted_iota(jnp.int32, (1, ns_ref.shape[1]), 1)
        msk = (seln_ref[0, 0][:, 0:1] > 0.5) & (i2 <= t_row) & (i2 < n_tok)
        update(_dot_nt(q, ns_ref[0, :, 0:KV_W].astype(BF16)), msk,
               lambda p: _dot(p, ns_ref[0, :, KV_W:2 * KV_W].astype(BF16)))
        o_s = acc_scr[...] / jnp.maximum(l_scr[...], 1e-30)
        o_ref[0] = ocw_ref[0] + g_ref[0][:, 1:2] * o_s


def _nsa_sample_b(page_table, cache, q_rows, g_rows, sel_steps, new_sel, ocw, *, n_tok, pps=PAGES_PER_STEP):
    nb, n_pages = page_table.shape
    n_steps = n_pages // pps
    rows = q_rows.shape[1]
    keys = pps * PAGE_SIZE
    expand = jnp.asarray(np.arange(LANES)[:, None] == (np.arange(keys)[None, :] // SEL_BLOCK), BF16)
    per_b = lambda a: pl.BlockSpec((1,) + a.shape[1:], lambda b, s, pt: (b, 0, 0))
    return pl.pallas_call(
        functools.partial(_nsa_sample_b_kernel, n_pages=n_pages, n_steps=n_steps, pps=pps, n_tok=n_tok),
        grid_spec=pltpu.PrefetchScalarGridSpec(
            num_scalar_prefetch=1,
            grid=(nb, n_steps),
            in_specs=[pl.BlockSpec(memory_space=pl.ANY),
                      per_b(q_rows), per_b(g_rows),
                      pl.BlockSpec((1, 1, rows, LANES), lambda b, s, pt: (b, s, 0, 0)),
                      pl.BlockSpec((1, 1, rows, LANES), lambda b, s, pt: (b, n_steps, 0, 0)),
                      per_b(new_sel), per_b(ocw),
                      pl.BlockSpec((LANES, keys), lambda b, s, pt: (0, 0))],
            out_specs=pl.BlockSpec((1, rows, KV_W), lambda b, s, pt: (b, 0, 0)),
            scratch_shapes=[pltpu.VMEM((2, 2, KV_W, keys), F32), pltpu.SemaphoreType.DMA((2,)),
                            pltpu.VMEM((rows, 1), F32), pltpu.VMEM((rows, 1), F32),
                            pltpu.VMEM((rows, KV_W), F32)]),
        out_shape=jax.ShapeDtypeStruct((nb, rows, KV_W), F32),
        compiler_params=_cparams(("arbitrary", "arbitrary")),
        name="nsa_sample_b",
    )(page_table, cache, q_rows, g_rows, sel_steps, sel_steps, new_sel, ocw, expand)


def _nsa_sample(page_table, cache, win_buf, cw, qpad, gates, kv4, kvw, pps=PAGES_PER_STEP):
    nb, ts, _ = qpad.shape
    past_len = page_table.shape[1] * PAGE_SIZE
    kc, vc = _compress(page_table, cache, *cw, pps=pps, transposed=True)
    rows = NSA_HEADS * ts
    q_rows = qpad.reshape(nb, ts, NSA_HEADS, LANES).transpose(0, 2, 1, 3).reshape(nb, rows, LANES)
    g_rows = gates[:, :, :3 * NSA_HEADS].reshape(nb, ts, NSA_HEADS, 3).transpose(0, 2, 1, 3)
    g_rows = jnp.pad(g_rows.reshape(nb, rows, 3), ((0, 0), (0, 0), (0, LANES - 3)))
    pad_rows = lambda a: jnp.pad(a, ((0, 0), (0, LANES - ts), (0, 0)))
    new_win = pad_rows(kvw)
    new_sel = pad_rows(kv4[:, :, 2 * KV_W:])
    ocw, sel = _nsa_sample_a(q_rows, g_rows, kc, vc, win_buf, new_win, past_len=past_len, n_tok=ts)
    n_steps = page_table.shape[1] // pps
    blk_per_step = pps * PAGE_SIZE // SEL_BLOCK
    n_past_blk = n_steps * blk_per_step
    sel_past = sel[:, :, :n_past_blk].reshape(nb, NSA_KV_HEADS, 1, ts, n_steps, blk_per_step)
    sel_past = jnp.broadcast_to(sel_past, (nb, NSA_KV_HEADS, NSA_GROUP, ts, n_steps, blk_per_step))
    sel_past = sel_past.transpose(0, 4, 1, 2, 3, 5).reshape(nb, n_steps, rows, blk_per_step)
    sel_past = jnp.pad(sel_past, ((0, 0), (0, 0), (0, 0), (0, LANES - blk_per_step)))
    sel_new = jnp.pad(sel[:, :, n_past_blk:], ((0, 0), (0, 0), (0, LANES)))[:, :, :LANES]
    sel_new = sel_new.reshape(nb, NSA_KV_HEADS, 1, ts, LANES)
    sel_new = jnp.broadcast_to(sel_new, (nb, NSA_KV_HEADS, NSA_GROUP, ts, LANES)).reshape(nb, 1, rows, LANES)
    sel_steps = jnp.concatenate([sel_past, sel_new], axis=1).astype(BF16)
    o_rows = _nsa_sample_b(page_table, cache, q_rows, g_rows, sel_steps, new_sel, ocw, n_tok=ts, pps=pps)
    return o_rows.reshape(nb, NSA_HEADS, ts, LANES).transpose(0, 2, 1, 3).reshape(nb, ts, QPAD_W).astype(BF16)


def _outproj_kernel(x_ref, hg_ref, nsa_ref, g1_ref, sc2_ref, sh2_ref, fn_ref, wo1_ref, wo2_ref, *rest, tm):
    x1_ref, h2_ref = rest[-2:]
    mix =_dot(hg_ref[0], wo1_ref[...]) + _dot(nsa_ref[0], wo2_ref[...])
    x1 = x_ref[0] + g1_ref[0] * mix
    x1_ref[0] = x1
    h2 = _rms(x1, fn_ref[...]) * (1.0 + sc2_ref[0]) + sh2_ref[0]
    _store_tok_tiles(h2_ref, h2, tm)


def _outproj(x, hg_out, nsa, gate1, scale2, shift2, ffn_norm, wo_hg, wo_nsa, tm, n_all, h2_all=None):
    nb, t, _ = x.shape
    nt = t // tm
    tile0 = 0 if h2_all is None else (n_all - nb * t) // tm
    mod_spec = _mod_spec(gate1, tm)
    tile = lambda w: pl.BlockSpec((1, tm, w), lambda b, i: (b, i, 0))
    in_specs = [tile(D_MODEL), tile(HG_WIDTH), tile(QPAD_W), mod_spec, mod_spec, mod_spec,
                pl.BlockSpec((1, D_MODEL), lambda b, i: (0, 0)),
                pl.BlockSpec((HG_WIDTH, D_MODEL), lambda b, i: (0, 0)),
                pl.BlockSpec((QPAD_W, D_MODEL), lambda b, i: (0, 0))]
    args = [x, hg_out, nsa, gate1, scale2, shift2, ffn_norm.reshape(1, -1), wo_hg, wo_nsa]
    aliases = {}
    if h2_all is not None:
        in_specs.append(pl.BlockSpec(memory_space=pl.ANY))
        args.append(h2_all)
        aliases = {len(args) - 1: 1}
    return pl.pallas_call(
        functools.partial(_outproj_kernel, tm=tm),
        grid=(nb, nt),
        in_specs=in_specs,
        out_specs=[tile(D_MODEL), pl.BlockSpec((tm * TOK_ROWS, LANES), lambda b, i: (tile0 + b * nt + i, 0))],
        out_shape=[jax.ShapeDtypeStruct((nb, t, D_MODEL), F32),
                   jax.ShapeDtypeStruct((n_all * TOK_ROWS, LANES), F32)],
        input_output_aliases=aliases,
        compiler_params=_cparams(("arbitrary", "arbitrary")),
        name="out_proj",
    )(*args)


def _split_w_out(w_out):
    wo_hg = w_out[:HG_WIDTH].astype(BF16)
    wn = w_out[HG_WIDTH:].reshape(NSA_HEADS, NSA_HEAD_DIM, D_MODEL)
    z = jnp.zeros_like(wn)
    grp = (jnp.arange(NSA_HEADS) // NSA_GROUP)[:, None, None]
    wn_pad = jnp.where(grp == 0, jnp.concatenate([wn, z], axis=1), jnp.concatenate([z, wn], axis=1))
    return wo_hg, wn_pad.reshape(QPAD_W, D_MODEL).astype(BF16)


def _router_kernel(h_ref, wr_ref, b_ref, e_ref, w_ref, r_ref, cnt_ref, run_scr, *, tm):
    @pl.when(pl.program_id(0) == 0)
    def _():
        run_scr[...] = jnp.zeros(run_scr.shape, F32)

    x = _load_tok_tiles(h_ref, tm).astype(BF16)
    scores = jax.nn.sigmoid(_dot(x, wr_ref[...]))
    biased = scores + b_ref[...]
    lane_i = lax.broadcasted_iota(jnp.int32, (tm, N_EXPERTS), 1)
    lane = lane_i.astype(F32)
    grp_of_lane = lax.shift_right_logical(lane_i, 5)
    per_group = N_EXPERTS // N_GROUPS

    gcol = lax.broadcasted_iota(jnp.int32, (tm, LANES), 1)
    gs = jnp.full((tm, LANES), 2.0 * NEG, F32)
    for g in range(N_GROUPS):
        mg = jnp.where(grp_of_lane == g, biased, NEG)
        m1 = jnp.max(mg, axis=1, keepdims=True)
        i1 = jnp.min(jnp.where(mg == m1, lane, 1e9), axis=1, keepdims=True)
        m2 = jnp.max(jnp.where(lane == i1, NEG, mg), axis=1, keepdims=True)
        gs = jnp.where(gcol == g, m1 + m2, gs)
    gsel = _topk_mask(gs, TOPK_GROUPS).astype(BF16)
    spread = (lax.broadcasted_iota(jnp.int32, (LANES, N_EXPERTS), 0)
              == lax.shift_right_logical(lax.broadcasted_iota(jnp.int32, (LANES, N_EXPERTS), 1), 5)).astype(BF16)
    v = jnp.where(_dot(gsel, spread) > 0.5, biased, NEG)

    onehot = jnp.zeros((tm, N_EXPERTS), F32)
    idxs, wts = [], []
    wsum = jnp.zeros((tm, 1), F32)
    for _ in range(TOP_K):
        m = jnp.max(v, axis=1, keepdims=True)
        idx = jnp.min(jnp.where(v == m, lane, 1e9), axis=1, keepdims=True)
        pick = lane == idx
        wk = jnp.sum(jnp.where(pick, scores, 0.0), axis=1, keepdims=True)
        onehot = jnp.where(pick, 1.0, onehot)
        v = jnp.where(pick, 3.0 * NEG, v)
        idxs.append(idx)
        wts.append(wk)
        wsum = wsum + wk

    earlier = (lax.broadcasted_iota(jnp.int32, (tm, tm), 0) > lax.broadcasted_iota(jnp.int32, (tm, tm), 1))
    before = _dot(earlier.astype(BF16), onehot.astype(BF16)) + run_scr[...]
    e_out = jnp.zeros((tm, LANES), jnp.int32)
    r_out = jnp.zeros((tm, LANES), jnp.int32)
    w_out = jnp.zeros((tm, LANES), F32)
    for k in range(TOP_K):
        rk = jnp.sum(jnp.where(lane == idxs[k], before, 0.0), axis=1, keepdims=True)
        e_out = jnp.where(gcol == k, idxs[k].astype(jnp.int32), e_out)
        r_out = jnp.where(gcol == k, rk.astype(jnp.int32), r_out)
        w_out = jnp.where(gcol == k, wts[k] / wsum * ROUTED_SCALE, w_out)
    e_ref[...] = e_out
    r_ref[...] = r_out
    w_ref[...] = w_out
    run_scr[...] = run_scr[...] + jnp.sum(onehot, axis=0, keepdims=True)
    cnt_ref[...] = run_scr[...]


def _router(h2, w_router, bias, n_tok, tm=ROUTER_TM):
    tile = pl.BlockSpec((tm, LANES), lambda i: (i, 0))
    return pl.pallas_call(
        functools.partial(_router_kernel, tm=tm),
        grid=(n_tok // tm,),
        in_specs=[pl.BlockSpec((tm * TOK_ROWS, LANES), lambda i: (i, 0)),
                  pl.BlockSpec((D_MODEL, N_EXPERTS), lambda i: (0, 0)),
                  pl.BlockSpec((1, N_EXPERTS), lambda i: (0, 0))],
        out_specs=[tile, tile, tile, pl.BlockSpec((1, N_EXPERTS), lambda i: (0, 0))],
        out_shape=[jax.ShapeDtypeStruct((n_tok, LANES), jnp.int32),
                   jax.ShapeDtypeStruct((n_tok, LANES), F32),
                   jax.ShapeDtypeStruct((n_tok, LANES), jnp.int32),
                   jax.ShapeDtypeStruct((1, N_EXPERTS), F32)],
        scratch_shapes=[pltpu.VMEM((1, N_EXPERTS), F32)],
        compiler_params=_cparams(("arbitrary",)),
        name="moe_router",
    )(h2, w_router, bias)


def _dest_kernel(e_ref, r_ref, st_ref, d_ref):
    e = e_ref[...]
    tm = e.shape[0]
    lane = lax.broadcasted_iota(jnp.int32, (tm, N_EXPERTS), 1)
    col = lax.broadcasted_iota(jnp.int32, (tm, LANES), 1)
    st = st_ref[...]
    out = r_ref[...]
    for k in range(TOP_K):
        sk = jnp.sum(jnp.where(lane == e[:, k:k + 1], st, 0.0), axis=1, keepdims=True)
        out = jnp.where(col == k, out + sk.astype(jnp.int32), out)
    d_ref[...] = out


def _moe_dest(top_e, rank, starts, tm=ROUTER_TM):
    n_tok = top_e.shape[0]
    tile = pl.BlockSpec((tm, LANES), lambda i: (i, 0))
    return pl.pallas_call(
        _dest_kernel,
        grid=(n_tok // tm,),
        in_specs=[tile, tile, pl.BlockSpec((1, N_EXPERTS), lambda i: (0, 0))],
        out_specs=tile,
        out_shape=jax.ShapeDtypeStruct((n_tok, LANES), jnp.int32),
        compiler_params=_cparams(("arbitrary",)),
        name="moe_dest",
    )(top_e, rank, starts.astype(F32).reshape(1, -1))


def _moe_layout(counts, n_pairs):
    padded = (counts + MOE_BM - 1) // MOE_BM * MOE_BM
    pad_end = jnp.cumsum(padded)
    starts = pad_end - padded
    n_blocks = (n_pairs + N_EXPERTS * (MOE_BM - 1)) // MOE_BM
    blk = jnp.arange(n_blocks, dtype=jnp.int32)
    used = blk * MOE_BM < pad_end[-1]
    blk_run = jnp.minimum(blk, pad_end[-1] // MOE_BM - 1).astype(jnp.int32)
    e_of = jnp.sum(pad_end[None, :] <= (blk_run * MOE_BM)[:, None], axis=1).astype(jnp.int32)
    e_of = jnp.minimum(e_of, N_EXPERTS - 1)
    shifted = jnp.concatenate([jnp.full((1,), -1, jnp.int32), e_of[:-1]])
    fresh = (used & (e_of != shifted)).astype(jnp.int32)
    w_slot = ((jnp.cumsum(fresh) - 1) % 2).astype(jnp.int32)
    ids = jnp.arange(N_EXPERTS, dtype=jnp.int32)
    later = jnp.where(counts > 0, ids, N_EXPERTS)
    next_nonempty = jnp.concatenate([lax.cummin(later, reverse=True)[1:], jnp.full((1,), N_EXPERTS, jnp.int32)])
    next_e = next_nonempty[e_of].astype(jnp.int32)
    return (starts.astype(jnp.int32), (starts + counts).astype(jnp.int32),
            (blk_run, e_of, fresh, used.astype(jnp.int32), w_slot, next_e), n_blocks)


def _dispatch_kernel(dest_ref, h_ref, xs_ref, sem, *, tm):
    n_pairs = tm * TOP_K

    def row_copy(src_tok, dst_row):
        return pltpu.make_async_copy(
            h_ref.at[pl.ds(pl.multiple_of(src_tok * TOK_ROWS, TOK_ROWS), TOK_ROWS), :],
            xs_ref.at[pl.ds(pl.multiple_of(dst_row * TOK_ROWS, TOK_ROWS), TOK_ROWS), :], sem)

    def issue(i, carry):
        for k in range(TOP_K):
            row_copy(i, dest_ref[0, 0, i * TOP_K + k]).start(priority=k % 2)
        return carry

    def drain(i, carry):
        for _ in range(TOP_K):
            row_copy(0, 0).wait()
        return carry

    lax.fori_loop(0, tm, issue, 0)
    lax.fori_loop(0, tm, drain, 0)


def _dispatch(dest_tiles, h2, n_pairs, tm=ROUTER_TM):
    n_tiles = dest_tiles.shape[0]
    return pl.pallas_call(
        functools.partial(_dispatch_kernel, tm=tm),
        grid=(n_tiles,),
        in_specs=[pl.BlockSpec((1, 1, tm * TOP_K), lambda i: (i, 0, 0), memory_space=pltpu.SMEM),
                  pl.BlockSpec((tm * TOK_ROWS, LANES), lambda i: (i, 0))],
        out_specs=pl.BlockSpec(memory_space=pl.ANY),
        out_shape=jax.ShapeDtypeStruct((n_pairs * TOK_ROWS, LANES), F32),
        scratch_shapes=[pltpu.SemaphoreType.DMA(())],
        compiler_params=_cparams(("arbitrary",)),
        name="moe_dispatch",
    )(dest_tiles, h2)


GMM_SLOTS = 3


def _gmm_kernel(blk_ref, e_ref, fresh_ref, used_ref, wslot_ref, next_ref, end_ref,
                xs_ref, wg_ref, wu_ref, wd_ref, ys_ref, wg_bf, wu_bf, wd_bf, xbuf, sem,
                wg_st, wu_st, wd_st, wsem, *, bm):
    i = pl.program_id(0)
    n = pl.num_programs(0)

    def fetch(j):
        rows = pl.ds(pl.multiple_of(blk_ref[j] * (bm * TOK_ROWS), bm * TOK_ROWS), bm * TOK_ROWS)
        return pltpu.make_async_copy(xs_ref.at[rows, :], xbuf.at[j % GMM_SLOTS], sem.at[j % GMM_SLOTS])

    def weight_copies(e, slot):
        return [pltpu.make_async_copy(src.at[e], dst.at[slot], wsem.at[slot])
                for src, dst in ((wg_ref, wg_st), (wu_ref, wu_st), (wd_ref, wd_st))]

    @pl.when(i == 0)
    def _():
        for cp in weight_copies(e_ref[0], wslot_ref[0]):
            cp.start()
        for j in range(GMM_SLOTS - 1):
            @pl.when((j < n) & (used_ref[jnp.minimum(j, n - 1)] == 1))
            def _():
                fetch(j).start()

    ahead = jnp.minimum(i + GMM_SLOTS - 1, n - 1)

    @pl.when((i + GMM_SLOTS - 1 < n) & (used_ref[ahead] == 1))
    def _():
        fetch(ahead).start()

    @pl.when(fresh_ref[i] == 1)
    def _():
        slot = wslot_ref[i]
        for cp in weight_copies(e_ref[i], slot):
            cp.wait()
        wg_bf[...] = wg_st[slot].astype(BF16)
        wu_bf[...] = wu_st[slot].astype(BF16)
        wd_bf[...] = wd_st[slot].astype(BF16)

        @pl.when(next_ref[i] < N_EXPERTS)
        def _():
            for cp in weight_copies(next_ref[i], 1 - slot):
                cp.start()

    @pl.when(used_ref[i] == 1)
    def _():
        fetch(i).wait()
        row = blk_ref[i] * bm + lax.broadcasted_iota(jnp.int32, (bm, 1), 0)
        real = row < end_ref[e_ref[i]]
        x = jnp.where(real, _load_tok_tiles(xbuf.at[i % GMM_SLOTS], bm), 0.0).astype(BF16)
        hid = (_silu(_dot(x, wg_bf[...])) * _dot(x, wu_bf[...])).astype(BF16)
        _store_tok_tiles(ys_ref, _dot(hid, wd_bf[...]), bm)


def _moe_gmm(blocks, row_end, xs_sorted, w_gate, w_up, w_down, bm=MOE_BM):
    rows = pl.BlockSpec((bm * TOK_ROWS, LANES), lambda i, blk, *_: (blk[i], 0))
    in_hbm = pl.BlockSpec(memory_space=pl.ANY)
    weights = (w_gate, w_up, w_down)
    return pl.pallas_call(
        functools.partial(_gmm_kernel, bm=bm),
        grid_spec=pltpu.PrefetchScalarGridSpec(
            num_scalar_prefetch=len(blocks) + 1,
            grid=(blocks[0].shape[0],),
            in_specs=[in_hbm, in_hbm, in_hbm, in_hbm],
            out_specs=rows,
            scratch_shapes=[pltpu.VMEM(w.shape[1:], BF16) for w in weights]
            + [pltpu.VMEM((GMM_SLOTS, bm * TOK_ROWS, LANES), F32), pltpu.SemaphoreType.DMA((GMM_SLOTS,))]
            + [pltpu.VMEM((2,) + w.shape[1:], F32) for w in weights]
            + [pltpu.SemaphoreType.DMA((2,))]),
        out_shape=jax.ShapeDtypeStruct(xs_sorted.shape, F32),
        compiler_params=_cparams(("arbitrary",)),
        name="moe_experts",
    )(*blocks, row_end, xs_sorted, *weights)


def _combine_kernel(dest_ref, dnext_ref, w_ref, x1_ref, h_ref, g2_ref, wsg_ref, wsu_ref, wsd_ref, fn_ref, ys_ref,
                    o_ref, gbuf, sem, routed_scr, *, tm):
    n = pl.program_id(0) * pl.num_programs(1) + pl.program_id(1)
    total = pl.num_programs(0) * pl.num_programs(1)
    slot = n % 2

    def row_copy(src_row, p, sl):
        return pltpu.make_async_copy(
            ys_ref.at[pl.ds(pl.multiple_of(src_row * TOK_ROWS, TOK_ROWS), TOK_ROWS), :],
            gbuf.at[sl, pl.ds(pl.multiple_of(p * TOK_ROWS, TOK_ROWS), TOK_ROWS), :], sem.at[sl])

    def gather(rows_ref, sl):
        def issue(i, carry):
            for k in range(TOP_K):
                p = i * TOP_K + k
                row_copy(rows_ref[0, 0, p], p, sl).start(priority=k % 2)
            return carry
        lax.fori_loop(0, tm, issue, 0)

    def drain(i, carry):
        for _ in range(TOP_K):
            row_copy(0, 0, slot).wait()
        return carry

    @pl.when(n == 0)
    def _():
        gather(dest_ref, slot)

    @pl.when(n + 1 < total)
    def _():
        gather(dnext_ref, 1 - slot)

    h = _load_tok_tiles(h_ref, tm).astype(BF16)
    hid = (_silu(_dot(h, wsg_ref[...])) * _dot(h, wsu_ref[...])).astype(BF16)
    shared = _dot(hid, wsd_ref[...])
    lax.fori_loop(0, tm, drain, 0)

    def weigh(t, carry):
        acc = jnp.zeros((TOK_ROWS, LANES), F32)
        for k in range(TOP_K):
            p = t * TOP_K + k
            acc = acc + w_ref[0, 0, p] * gbuf[slot, pl.ds(pl.multiple_of(p * TOK_ROWS, TOK_ROWS), TOK_ROWS), :]
        routed_scr[pl.ds(pl.multiple_of(t * TOK_ROWS, TOK_ROWS), TOK_ROWS), :] = acc
        return carry

    lax.fori_loop(0, tm, weigh, 0)
    x2 = x1_ref[0] + g2_ref[0] * (_load_tok_tiles(routed_scr, tm) + shared)
    o_ref[0] = _rms(x2, fn_ref[...])


def _combine(dest_tiles, w_tiles, x1, h2, gate2, shared, fnorm, ys_sorted, tile0, tm=COMBINE_TM):
    nb, t, _ = x1.shape
    nt = t // tm
    flat = lambda b, i: tile0 + b * nt + i
    nxt = lambda b, i: tile0 + jnp.minimum(b * nt + i + 1, nb * nt - 1)
    mod_spec = _mod_spec(gate2, tm)
    const = lambda a: pl.BlockSpec(a.shape, lambda b, i: (0, 0))
    return pl.pallas_call(
        functools.partial(_combine_kernel, tm=tm),
        grid=(nb, nt),
        in_specs=[pl.BlockSpec((1, 1, tm * TOP_K), lambda b, i: (flat(b, i), 0, 0), memory_space=pltpu.SMEM),
                  pl.BlockSpec((1, 1, tm * TOP_K), lambda b, i: (nxt(b, i), 0, 0), memory_space=pltpu.SMEM),
                  pl.BlockSpec((1, 1, tm * TOP_K), lambda b, i: (flat(b, i), 0, 0), memory_space=pltpu.SMEM),
                  pl.BlockSpec((1, tm, D_MODEL), lambda b, i: (b, i, 0)),
                  pl.BlockSpec((tm * TOK_ROWS, LANES), lambda b, i: (flat(b, i), 0)),
                  mod_spec, const(shared[0]), const(shared[1]), const(shared[2]), const(fnorm),
                  pl.BlockSpec(memory_space=pl.ANY)],
        out_specs=pl.BlockSpec((1, tm, D_MODEL), lambda b, i: (b, i, 0)),
        out_shape=jax.ShapeDtypeStruct((nb, t, D_MODEL), F32),
        scratch_shapes=[pltpu.VMEM((2, tm * TOP_K * TOK_ROWS, LANES), F32), pltpu.SemaphoreType.DMA((2,)),
                        pltpu.VMEM((tm * TOK_ROWS, LANES), F32)],
        compiler_params=_cparams(("arbitrary", "arbitrary")),
        name="moe_combine",
    )(dest_tiles, dest_tiles, w_tiles, x1, h2, gate2, *shared, fnorm, ys_sorted)


def kernel(x_prompt, x_sample, c_prompt, c_sample, cache_nsa_kv, cache_win_kv, state_hgrn, page_table,
           attn_norm, ffn_norm, final_norm, hg_norm, w_ada, b_ada, w_in, hg_lb,
           cmp_pe, cmp_w1, cmp_b1, cmp_w2, w_out, w_router, router_bias,
           w_gate, w_up, w_down, ws_gate, ws_up, ws_down):
    nbp, t, _ = x_prompt.shape
    nbs, ts, _ = x_sample.shape
    ns = nbs * ts
    n_all = nbp * t + ns
    past_len = page_table.shape[1] * PAGE_SIZE

    c_all = jnp.concatenate([c_prompt, c_sample], axis=0)
    c_all = jnp.pad(c_all, ((0, -c_all.shape[0] % SUBLANES), (0, 0)))
    mod = _ada(c_all, w_ada[0], b_ada[0])
    modp = mod[:nbp].reshape(nbp, 1, 6, D_MODEL)
    mods = jnp.repeat(mod[nbp:nbp + nbs].reshape(nbs, 1, 6, D_MODEL), ts, axis=1).reshape(1, ns, 6, D_MODEL)

    w_pad = _pad_w_in(w_in[0])
    cw = _compress_weights(cmp_pe[0], cmp_w1[0], cmp_b1[0], cmp_w2[0])
    wo_hg, wo_nsa = _split_w_out(w_out[0])

    hg, qpad, kv4, kvw, gates, kvsel, kvwb = _inproj(
        x_prompt, modp[:, :, 1], modp[:, :, 0], attn_norm[0], w_pad, 512)
    hg_out, hg_state_p = _hgrn(hg, hg_lb, jnp.zeros((nbp, HG_HEADS, HG_DK, HG_DK), F32), hg_norm[0],
                               256, HG_CHUNK)
    n_pages_p = t // PAGE_SIZE
    ptp = jnp.arange(nbp * n_pages_p, dtype=jnp.int32).reshape(nbp, n_pages_p)
    kc, vc = _compress(ptp, kv4.reshape(nbp * n_pages_p, PAGE_SIZE, 4 * KV_W), *cw)
    nsa = _nsa_prompt(qpad, gates, kc, vc, kvsel, kvwb)
    x1p, h2p = _outproj(x_prompt, hg_out, nsa, modp[:, :, 2], modp[:, :, 4], modp[:, :, 3],
                        ffn_norm[0], wo_hg, wo_nsa, 512, n_all)

    xs = x_sample.reshape(1, ns, D_MODEL)
    hg_s, qpad_s, kv4_s, kvw_s, gates_s, _, _ = _inproj(
        xs, mods[:, :, 1], mods[:, :, 0], attn_norm[0], w_pad, ns)
    hg_out_s, hg_state_s = _hgrn(hg_s.reshape(nbs, ts, 4 * HG_WIDTH), hg_lb, state_hgrn[0], hg_norm[0], ts, ts)
    cache = cache_nsa_kv[0].transpose(0, 2, 3, 4, 1).reshape(-1, 4 * KV_W, PAGE_SIZE)
    win_buf = cache_win_kv[0].reshape(nbs, -1, 2 * KV_W)
    nsa_s = _nsa_sample(page_table, cache, win_buf, cw, qpad_s.reshape(nbs, ts, QPAD_W),
                        gates_s.reshape(nbs, ts, GZ_PAD), kv4_s.reshape(nbs, ts, 4 * KV_W),
                        kvw_s.reshape(nbs, ts, 2 * KV_W)).reshape(1, ns, QPAD_W)
    x1s, h2 = _outproj(xs, hg_out_s.reshape(1, ns, HG_WIDTH), nsa_s, mods[:, :, 2], mods[:, :, 4], mods[:, :, 3],
                       ffn_norm[0], wo_hg, wo_nsa, ns, n_all, h2_all=h2p)

    top_e, top_w, rank, counts = _router(h2, w_router[0].astype(BF16), router_bias[0].reshape(1, -1), n_all)
    starts, row_end, blocks, n_blocks = _moe_layout(counts[0].astype(jnp.int32), n_all * TOP_K)
    dest = _moe_dest(top_e, rank, starts)[:, :TOP_K].reshape(-1)
    xs_sorted = _dispatch(dest.reshape(n_all // ROUTER_TM, 1, ROUTER_TM * TOP_K), h2, n_blocks * MOE_BM)
    ys_sorted = _moe_gmm(blocks, row_end, xs_sorted, w_gate[0], w_up[0], w_down[0])
    dest_c = dest.reshape(n_all // COMBINE_TM, 1, COMBINE_TM * TOP_K)
    w_c = top_w[:, :TOP_K].reshape(n_all // COMBINE_TM, 1, COMBINE_TM * TOP_K)
    shared = (ws_gate[0].astype(BF16), ws_up[0].astype(BF16), ws_down[0].astype(BF16))
    fnorm = final_norm.reshape(1, -1)
    y_prompt = _combine(dest_c, w_c, x1p, h2, modp[:, :, 5], shared, fnorm, ys_sorted, 0)
    y_sample = _combine(dest_c, w_c, x1s, h2, mods[:, :, 5], shared, fnorm, ys_sorted, nbp * t // COMBINE_TM)

    wb = win_buf.shape[1]
    win_p = kvw[:, t - min(WINDOW, t):]
    win_s = jnp.concatenate([win_buf, kvw_s.reshape(nbs, ts, 2 * KV_W)], axis=1)[:, -wb:]
    kv_shape = (4, NSA_KV_HEADS, NSA_HEAD_DIM)
    win_shape = (2, NSA_KV_HEADS, NSA_HEAD_DIM)
    return (y_prompt,
            y_sample.reshape(nbs, ts, D_MODEL),
            kv4.reshape(1, nbp, t, *kv_shape),
            win_p.reshape(1, nbp, -1, *win_shape),
            hg_state_p[None],
            kv4_s.reshape(1, nbs, ts, *kv_shape),
            win_s.reshape(1, nbs, wb, *win_shape),
            hg_state_s[None])
```

```python
import functools

import jax
import jax.numpy as jnp
import numpy as np
from jax import lax
from jax.experimental import pallas as pl
from jax.experimental.pallas import tpu as pltpu

F32 = jnp.float32
BF16 = jnp.bfloat16

D_MODEL = 1024
HG_WIDTH = 512
HG_HEADS = 4
HG_DK = 128
HG_CHUNK = 32
NSA_WIDTH = 512
NSA_HEADS = 8
NSA_HEAD_DIM = 64
NSA_KV_HEADS = 2
NSA_GROUP = 4
KV_W = 128
CMP_BLOCK = 32
CMP_STRIDE = 16
CMP_HIDDEN = 256
SEL_BLOCK = 64
SEL_TOPN = 16
WINDOW = 512
Q_BLOCK = 128
N_EXPERTS = 256
TOP_K = 8
N_GROUPS = 8
TOPK_GROUPS = 4
ROUTED_SCALE = 2.5
RMS_EPS = 1e-6
PAGE_SIZE = 128

LANES = 128
SUBLANES = 8
TOK_ROWS = D_MODEL // LANES
VMEM_LIMIT = 56 * 1024 * 1024

QPAD_W = NSA_HEADS * LANES
GZ_PAD = LANES
INP_COLS = 4 * HG_WIDTH + QPAD_W + 4 * KV_W + 2 * KV_W + GZ_PAD

SEL_SHIFT = SEL_BLOCK.bit_length() - 1
GROUP_SHIFT = (N_EXPERTS // N_GROUPS).bit_length() - 1
NEG = -1e30
PROJ_TM = 512
HGRN_TC = 256
PAGES_PER_STEP = 32
MOE_BM = 256
ROUTER_TM = 256
COMBINE_TM = 128


def _cparams(sem):
    return pltpu.CompilerParams(dimension_semantics=sem, vmem_limit_bytes=VMEM_LIMIT)


def _dot(a, b):
    return jnp.dot(a, b, preferred_element_type=F32)


def _dot_nt(a, b):
    return lax.dot_general(a, b, (((1,), (1,)), ((), ())), preferred_element_type=F32)


def _dot_tn(a, b):
    return lax.dot_general(a, b, (((0,), (0,)), ((), ())), preferred_element_type=F32)


def _rms(x, g):
    return x * lax.rsqrt(jnp.mean(x * x, axis=-1, keepdims=True) + RMS_EPS) * g


def _silu(x):
    return x * jax.nn.sigmoid(x)


Q_SCALE = NSA_HEAD_DIM ** -0.5 * 1.4426950408889634


def _masked_softmax(s, valid):
    s = jnp.where(valid, s, NEG)
    m = jnp.max(s, axis=1, keepdims=True)
    p = jnp.exp2(s - m) * valid.astype(F32)
    return p / jnp.maximum(jnp.sum(p, axis=1, keepdims=True), 1e-30)


def _topk_mask(v, k):
    lane = lax.broadcasted_iota(jnp.int32, v.shape, 1).astype(F32)
    sel = jnp.zeros(v.shape, F32)
    for _ in range(k):
        m = jnp.max(v, axis=1, keepdims=True)
        idx = jnp.min(jnp.where(v == m, lane, 1e9), axis=1, keepdims=True)
        pick = lane == idx
        sel = jnp.where(pick, 1.0, sel)
        v = jnp.where(pick, 3.0 * NEG, v)
    return sel


def _mod_spec(mod, tm):
    if mod.shape[1] == 1:
        return pl.BlockSpec((1, 1, D_MODEL), lambda b, i: (b, 0, 0))
    return pl.BlockSpec((1, tm, D_MODEL), lambda b, i: (b, i, 0))


def _load_tok_tiles(ref, n_tok):
    return jnp.concatenate([ref[pl.ds(s, n_tok, stride=TOK_ROWS), :] for s in range(TOK_ROWS)], axis=1)


def _store_tok_tiles(ref, val, n_tok):
    for s in range(TOK_ROWS):
        ref[pl.ds(s, n_tok, stride=TOK_ROWS), :] = val[:, s * LANES:(s + 1) * LANES]


def _ada_kernel(c_ref, w_ref, b_ref, o_ref):
    s = _silu(c_ref[...]).astype(BF16)
    o_ref[...] = _dot(s, w_ref[...].astype(BF16)) + b_ref[...]


def _ada(c_all, w_ada, b_ada):
    n = c_all.shape[0]
    return pl.pallas_call(
        _ada_kernel,
        grid=(6,),
        in_specs=[pl.BlockSpec((n, D_MODEL), lambda j: (0, 0)),
                  pl.BlockSpec((D_MODEL, D_MODEL), lambda j: (0, j)),
                  pl.BlockSpec((1, D_MODEL), lambda j: (0, j))],
        out_specs=pl.BlockSpec((n, D_MODEL), lambda j: (0, j)),
        out_shape=jax.ShapeDtypeStruct((n, 6 * D_MODEL), F32),
        compiler_params=_cparams(("arbitrary",)),
        name="ada_mod",
    )(c_all, w_ada, b_ada.reshape(1, -1))


def _inproj_kernel(x_ref, sc_ref, sh_ref, g_ref, w_ref,
                   hg_ref, q_ref, kv4_ref, kvw_ref, gate_ref, kvsel_ref, kvwb_ref):
    h = _rms(x_ref[0], g_ref[...]) * (1.0 + sc_ref[0]) + sh_ref[0]
    z = _dot(h.astype(BF16), w_ref[...])
    c0 = 4 * HG_WIDTH
    hg_ref[0] = z[:, :c0]
    q_ref[0] = (z[:, c0:c0 + QPAD_W] * Q_SCALE).astype(BF16)
    c1 = c0 + QPAD_W
    kv4 = z[:, c1:c1 + 4 * KV_W]
    kv4_ref[0] = kv4
    kvsel_ref[0] = kv4[:, 2 * KV_W:].astype(BF16)
    c2 = c1 + 4 * KV_W
    kvw = z[:, c2:c2 + 2 * KV_W]
    kvw_ref[0] = kvw
    kvwb_ref[0] = kvw.astype(BF16)
    gate_ref[0] = jax.nn.sigmoid(z[:, c2 + 2 * KV_W:])


def _inproj(x, scale, shift, g_norm, w_pad, tm):
    nb, t, _ = x.shape
    mod_spec = _mod_spec(scale, tm)
    widths = [(4 * HG_WIDTH, F32), (QPAD_W, BF16), (4 * KV_W, F32), (2 * KV_W, F32), (GZ_PAD, F32),
              (2 * KV_W, BF16), (2 * KV_W, BF16)]
    return pl.pallas_call(
        _inproj_kernel,
        grid=(nb, t // tm),
        in_specs=[pl.BlockSpec((1, tm, D_MODEL), lambda b, i: (b, i, 0)),
                  mod_spec, mod_spec,
                  pl.BlockSpec((1, D_MODEL), lambda b, i: (0, 0)),
                  pl.BlockSpec((D_MODEL, INP_COLS), lambda b, i: (0, 0))],
        out_specs=[pl.BlockSpec((1, tm, w), lambda b, i: (b, i, 0)) for w, _ in widths],
        out_shape=[jax.ShapeDtypeStruct((nb, t, w), dt) for w, dt in widths],
        compiler_params=_cparams(("arbitrary", "arbitrary")),
        name="in_proj",
    )(x, scale, shift, g_norm.reshape(1, -1), w_pad)


def _pad_w_in(w_in):
    c0 = 4 * HG_WIDTH
    wq = w_in[:, c0:c0 + NSA_WIDTH].reshape(D_MODEL, NSA_HEADS, NSA_HEAD_DIM)
    zeros = jnp.zeros_like(wq)
    lo = jnp.concatenate([wq, zeros], axis=-1)
    hi = jnp.concatenate([zeros, wq], axis=-1)
    grp = (jnp.arange(NSA_HEADS) // NSA_GROUP)[None, :, None]
    wq_pad = jnp.where(grp == 0, lo, hi).reshape(D_MODEL, QPAD_W)
    c1 = c0 + NSA_WIDTH
    rest = w_in[:, c1:c1 + 6 * KV_W]
    gz = jnp.pad(w_in[:, c1 + 6 * KV_W:], ((0, 0), (0, GZ_PAD - 3 * NSA_HEADS)))
    return jnp.concatenate([w_in[:, :c0], wq_pad, rest, gz], axis=1).astype(BF16)


def _hgrn_kernel(q_ref, f_ref, v_ref, gt_ref, lb_ref, s0_ref, gn_ref, o_ref, s_out_ref, st_scr,
                 *, chunk, n_chunks):
    i = pl.program_id(1)

    @pl.when(i == 0)
    def _():
        for h in range(HG_HEADS):
            st_scr[h] = s0_ref[0, h].T

    lbr = lb_ref[...]
    e = jnp.exp(lbr - jnp.max(lbr, axis=0, keepdims=True))
    lb_all = e[0:1] / jnp.sum(e, axis=0, keepdims=True)
    row = lax.broadcasted_iota(jnp.int32, (chunk, HG_DK), 0)
    causal = (lax.broadcasted_iota(jnp.int32, (chunk, chunk), 0)
              >= lax.broadcasted_iota(jnp.int32, (chunk, chunk), 1))
    st = [st_scr[h] for h in range(HG_HEADS)]
    for c in range(n_chunks):
        sl = pl.ds(c * chunk, chunk)
        for h in range(HG_HEADS):
            hs = slice(h * HG_DK, (h + 1) * HG_DK)
            lb = lb_all[:, hs]
            z = f_ref[0, sl, hs]
            log_f = jnp.log(lb + (1.0 - lb) * jax.nn.sigmoid(z))
            kk = (1.0 - lb) * jax.nn.sigmoid(-z)
            a = log_f
            s = 1
            while s < chunk:
                a = a + jnp.where(row >= s, pltpu.roll(a, s, 0), 0.0)
                s *= 2
            qt = (q_ref[0, sl, hs] * jnp.exp(a)).astype(BF16)
            kt = (kk * jnp.exp(-a)).astype(BF16)
            v = v_ref[0, sl, hs].astype(BF16)
            att = jnp.where(causal, _dot_nt(qt, kt), 0.0)
            o = _dot(att.astype(BF16), v) + _dot_nt(qt, st[h].astype(BF16))
            a_end = a[chunk - 1:chunk, :]
            kd = (kk * jnp.exp(a_end - a)).astype(BF16)
            st[h] = st[h] * jnp.exp(a_end) + _dot_tn(v, kd)
            o = _rms(o, gn_ref[...]) * _silu(gt_ref[0, sl, hs])
            o_ref[0, sl, hs] = o.astype(o_ref.dtype)
    for h in range(HG_HEADS):
        st_scr[h] = st[h]

    @pl.when(i == pl.num_programs(1) - 1)
    def _():
        for h in range(HG_HEADS):
            s_out_ref[0, h] = st[h].T


def _hgrn(hg, hg_lb, s0, g_norm, tc, chunk):
    nb, t, _ = hg.shape
    part = lambda k: pl.BlockSpec((1, tc, HG_WIDTH), lambda b, i: (b, i, k))
    st_spec = pl.BlockSpec((1, HG_HEADS, HG_DK, HG_DK), lambda b, i: (b, 0, 0, 0))
    return pl.pallas_call(
        functools.partial(_hgrn_kernel, chunk=chunk, n_chunks=tc // chunk),
        grid=(nb, t // tc),
        in_specs=[part(0), part(1), part(2), part(3),
                  pl.BlockSpec(hg_lb.shape, lambda b, i: (0, 0)),
                  st_spec,
                  pl.BlockSpec((1, HG_DK), lambda b, i: (0, 0))],
        out_specs=[pl.BlockSpec((1, tc, HG_WIDTH), lambda b, i: (b, i, 0)), st_spec],
        out_shape=[jax.ShapeDtypeStruct((nb, t, HG_WIDTH), BF16),
                   jax.ShapeDtypeStruct((nb, HG_HEADS, HG_DK, HG_DK), F32)],
        scratch_shapes=[pltpu.VMEM((HG_HEADS, HG_DK, HG_DK), F32)],
        compiler_params=_cparams(("arbitrary", "arbitrary")),
        name="hgrn2",
    )(hg, hg, hg, hg, hg_lb, s0, g_norm.reshape(1, -1))


def _gelu_tanh(x):
    return 0.5 * x * (1.0 + jnp.tanh(0.7978845608028654 * (x + 0.044715 * x * x * x)))


def _page_copies(pt_ref, cache_ref, buf, sem, b, s, slot, n_pages, pps, col0, tail, transposed):
    copies = []
    base = s * pps
    nxt = pt_ref[b, jnp.minimum(base + pps, n_pages - 1)]
    for br in range(2):
        cols = pl.ds(col0 + br * KV_W, KV_W)
        for i in range(pps):
            pg = pt_ref[b, base + i]
            if transposed:
                copies.append(pltpu.make_async_copy(
                    cache_ref.at[pg, cols, :],
                    buf.at[slot, br, :, pl.ds(i * PAGE_SIZE, PAGE_SIZE)], sem.at[slot]))
            else:
                copies.append(pltpu.make_async_copy(
                    cache_ref.at[pg, :, cols],
                    buf.at[slot, br, pl.ds(i * PAGE_SIZE, PAGE_SIZE), :], sem.at[slot]))
        if tail and transposed:
            copies.append(pltpu.make_async_copy(
                cache_ref.at[nxt, cols, :],
                buf.at[slot, br, :, pl.ds(pps * PAGE_SIZE, PAGE_SIZE)], sem.at[slot]))
        elif tail:
            copies.append(pltpu.make_async_copy(
                cache_ref.at[nxt, pl.ds(0, CMP_STRIDE), cols],
                buf.at[slot, br, pl.ds(pps * PAGE_SIZE, CMP_STRIDE), :], sem.at[slot]))
    return copies


def _stream_pages(pt_ref, cache_ref, buf, sem, n_pages, n_steps, pps, col0, tail, transposed):
    b = pl.program_id(0)
    s = pl.program_id(1)
    n = b * n_steps + s
    total = pl.num_programs(0) * n_steps
    slot = n % 2
    args = (n_pages, pps, col0, tail, transposed)

    @pl.when(n == 0)
    def _():
        for cp in _page_copies(pt_ref, cache_ref, buf, sem, b, s, slot, *args):
            cp.start()

    @pl.when(n + 1 < total)
    def _():
        n1 = n + 1
        for cp in _page_copies(pt_ref, cache_ref, buf, sem, n1 // n_steps, n1 % n_steps, 1 - slot, *args):
            cp.start()

    for cp in _page_copies(pt_ref, cache_ref, buf, sem, b, s, slot, *args):
        cp.wait()
    return slot


def _compress_kernel(pt_ref, cache_ref, pe_ref, w1_ref, b1_ref, w2_ref, kc_ref, vc_ref, buf, sem, *rowbuf,
                     n_pages, n_steps, pps, transposed):
    groups = pps * PAGE_SIZE // CMP_STRIDE

    def compress(read_rows):
        low_half = lax.broadcasted_iota(jnp.int32, (groups, KV_W), 1) < NSA_HEAD_DIM
        for br, out_ref in ((0, kc_ref), (1, vc_ref)):
            acc0 = jnp.zeros((groups, CMP_HIDDEN), F32)
            acc1 = jnp.zeros((groups, CMP_HIDDEN), F32)
            for j in range(CMP_BLOCK // 2):
                a = read_rows(br, 2 * j) + pe_ref[br, 2 * j:2 * j + 1, :]
                b = pltpu.roll(read_rows(br, 2 * j + 1) + pe_ref[br, 2 * j + 1:2 * j + 2, :], NSA_HEAD_DIM, 1)
                acc0 = acc0 + _dot(jnp.where(low_half, a, b).astype(BF16), w1_ref[br, j, 0])
                acc1 = acc1 + _dot(jnp.where(low_half, b, a).astype(BF16), w1_ref[br, j, 1])
            hid = _gelu_tanh(jnp.concatenate([acc0, acc1], axis=1) + b1_ref[br]).astype(BF16)
            out_ref[0] = _dot(hid, w2_ref[br]).astype(out_ref.dtype)

    if transposed:
        rows_ref, = rowbuf
        n = pl.program_id(0) * n_steps + pl.program_id(1)
        total = pt_ref.shape[0] * n_steps
        copies = lambda m: _page_copies(pt_ref, cache_ref, buf, sem, m // n_steps, m % n_steps, m % 2,
                                        n_pages, pps, 0, True, True)

        def to_rows(stage, half):
            for br in range(2):
                for i in range(pps + 1):
                    n_rows = PAGE_SIZE if i < pps else CMP_STRIDE
                    page_t = buf[stage, br, :, i * PAGE_SIZE:(i + 1) * PAGE_SIZE]
                    rows_ref[half, br, i * PAGE_SIZE:i * PAGE_SIZE + n_rows, :] = page_t.T[:n_rows]

        @pl.when(n == 0)
        def _():
            for cp in copies(0):
                cp.start()
            for cp in copies(0):
                cp.wait()
            to_rows(0, 0)
            if total > 1:
                for cp in copies(1):
                    cp.start()

        @pl.when(n + 1 < total)
        def _():
            for cp in copies(n + 1):
                cp.wait()

        @pl.when(n + 2 < total)
        def _():
            for cp in copies(n + 2):
                cp.start()

        for parity in range(2):
            @pl.when(n % 2 == parity)
            def _(parity=parity):
                if total > 1:
                    to_rows(1 - parity, 1 - parity)
                compress(lambda br, l: rows_ref[parity, br, pl.ds(l, groups, stride=CMP_STRIDE), :])
    else:
        slot = _stream_pages(pt_ref, cache_ref, buf, sem, n_pages, n_steps, pps, 0, True, False)
        compress(lambda br, l: buf[slot, br, pl.ds(l, groups, stride=CMP_STRIDE), :])


def _compress(page_table, cache, pe2, w1cat, b1cat, w2bd, pps=PAGES_PER_STEP, transposed=False):
    nb, n_pages = page_table.shape
    n_steps = n_pages // pps
    groups = pps * PAGE_SIZE // CMP_STRIDE
    rows = pps * PAGE_SIZE + CMP_STRIDE
    const = lambda shape: pl.BlockSpec(shape, lambda b, s, pt: (0,) * len(shape))
    out_spec = pl.BlockSpec((1, groups, KV_W), lambda b, s, pt: (b, s, 0))
    out_sds = jax.ShapeDtypeStruct((nb, n_steps * groups, KV_W), BF16)
    if transposed:
        stage = [pltpu.VMEM((2, 2, KV_W, (pps + 1) * PAGE_SIZE), F32), pltpu.SemaphoreType.DMA((2,)),
                 pltpu.VMEM((2, 2, rows, KV_W), F32)]
    else:
        stage = [pltpu.VMEM((2, 2, rows, KV_W), F32), pltpu.SemaphoreType.DMA((2,))]
    return pl.pallas_call(
        functools.partial(_compress_kernel, n_pages=n_pages, n_steps=n_steps, pps=pps, transposed=transposed),
        grid_spec=pltpu.PrefetchScalarGridSpec(
            num_scalar_prefetch=1,
            grid=(nb, n_steps),
            in_specs=[pl.BlockSpec(memory_space=pl.ANY),
                      const((2, CMP_BLOCK, KV_W)),
                      const((2, CMP_BLOCK // 2, 2, KV_W, CMP_HIDDEN)),
                      const((2, 1, 2 * CMP_HIDDEN)),
                      const((2, 2 * CMP_HIDDEN, KV_W))],
            out_specs=[out_spec, out_spec],
            scratch_shapes=stage),
        out_shape=[out_sds, out_sds],
        compiler_params=_cparams(("arbitrary", "arbitrary")),
        name="nsa_compress",
    )(page_table, cache, pe2, w1cat, b1cat, w2bd)


def _compress_weights(cmp_pe, cmp_w1, cmp_b1, cmp_w2):
    pe2 = jnp.concatenate([cmp_pe, cmp_pe], axis=-1)
    w1 = cmp_w1.reshape(2, CMP_BLOCK // 2, 2, NSA_HEAD_DIM, CMP_HIDDEN)
    even_odd = w1.reshape(2, CMP_BLOCK // 2, 2 * NSA_HEAD_DIM, CMP_HIDDEN)
    odd_even = w1[:, :, ::-1].reshape(2, CMP_BLOCK // 2, 2 * NSA_HEAD_DIM, CMP_HIDDEN)
    w1cat = jnp.stack([even_odd, odd_even], axis=2).astype(BF16)
    b1cat = jnp.concatenate([cmp_b1, cmp_b1], axis=-1)[:, None, :]
    z2 = jnp.zeros_like(cmp_w2)
    w2bd = jnp.concatenate([jnp.concatenate([cmp_w2, z2], axis=-1),
                            jnp.concatenate([z2, cmp_w2], axis=-1)], axis=1).astype(BF16)
    return pe2, w1cat, b1cat, w2bd


def _overlap_matrix(n_cmp, n_sel):
    cs = lax.broadcasted_iota(jnp.int32, (n_cmp, n_sel), 0) * CMP_STRIDE
    ss = lax.broadcasted_iota(jnp.int32, (n_cmp, n_sel), 1) * SEL_BLOCK
    return ((cs < ss + SEL_BLOCK) & (cs + CMP_BLOCK > ss)).astype(BF16)


SEL_CHUNK = 512
SPREAD_KEYS = 1024
WIN_SPAN = WINDOW + Q_BLOCK


def _nsa_prompt_kernel(q_ref, gt_ref, kc_ref, vc_ref, ks_ref, vs_ref, kw_ref, vw_ref, ex_ref, o_ref,
                       m_scr, acc_scr, chosen_scr, oc_scr, sa_scr, sb_scr, *, n_cmp, n_sel):
    j = pl.program_id(1)
    q0 = j * Q_BLOCK
    tok = lax.broadcasted_iota(jnp.int32, (Q_BLOCK, 1), 0) + q0
    tok4 = jnp.concatenate([tok] * NSA_GROUP, axis=0)
    n_chunks = j // (SEL_CHUNK // Q_BLOCK) + 1
    key_in_span = lax.broadcasted_iota(jnp.int32, (1, SPREAD_KEYS), 1)

    def load_q(g):
        return jnp.concatenate([q_ref[0, :, (NSA_GROUP * g + jh) * LANES:(NSA_GROUP * g + jh + 1) * LANES]
                                for jh in range(NSA_GROUP)], axis=0)

    def softmax_av(s, valid, v):
        s = jnp.where(valid, s, NEG)
        tiles = [s[:, i * LANES:(i + 1) * LANES] for i in range(s.shape[1] // LANES)]
        m = jnp.max(functools.reduce(jnp.maximum, tiles), axis=1, keepdims=True)
        p = jnp.where(valid, jnp.exp2(s - m), 0.0)
        acc = _dot(p.astype(BF16), jnp.concatenate([v, jnp.ones(v.shape, BF16)], axis=1))
        inv = 1.0 / jnp.maximum(acc[:, KV_W:], 1e-30)
        return p, acc[:, :KV_W] * inv, inv

    ov = _overlap_matrix(n_cmp, n_sel)
    cend = lax.broadcasted_iota(jnp.int32, (1, n_cmp), 1) * CMP_STRIDE + (CMP_BLOCK - 1)
    ws = pl.multiple_of(jnp.maximum(q0 - WINDOW, 0), Q_BLOCK)
    wpos = ws + lax.broadcasted_iota(jnp.int32, (1, WIN_SPAN), 1)
    d = tok4 - wpos
    in_window = (d >= 0) & (d < WINDOW)
    gates = gt_ref[0]
    imps = []
    for g in range(NSA_KV_HEADS):
        q = load_q(g)
        p, o_c, inv = softmax_av(_dot_nt(q, kc_ref[0]), cend <= tok4, vc_ref[0])
        _, o_w, _ = softmax_av(_dot_nt(q, kw_ref[0, pl.ds(ws, WIN_SPAN), :]), in_window,
                               vw_ref[0, pl.ds(ws, WIN_SPAN), :])
        psum = jnp.zeros((Q_BLOCK, n_cmp), F32)
        for jh in range(NSA_GROUP):
            h = NSA_GROUP * g + jh
            r = slice(jh * Q_BLOCK, (jh + 1) * Q_BLOCK)
            psum = psum + p[r] * jnp.concatenate([inv[r]] * (n_cmp // LANES), axis=1)
            oc_scr[g, r, :] = gates[:, 3 * h:3 * h + 1] * o_c[r] + gates[:, 3 * h + 2:3 * h + 3] * o_w[r]
        imps.append(_dot(psum.astype(BF16), ov))

    blk = lax.broadcasted_iota(jnp.int32, (1, n_sel), 1)
    cur = lax.shift_right_logical(tok, SEL_SHIFT)
    forced = (blk == 0) | (blk == cur) | (blk == cur - 1)
    free = (blk * SEL_BLOCK <= tok) & jnp.logical_not(forced)
    forced2 = jnp.concatenate([forced] * NSA_KV_HEADS, axis=0)
    free2 = jnp.concatenate([free] * NSA_KV_HEADS, axis=0)
    best = _topk_mask(jnp.where(free2, jnp.concatenate(imps, axis=0), NEG), SEL_TOPN - 3)
    sel2 = jnp.where(forced2, 1.0, best).astype(BF16)

    ones_blk = jnp.ones((SEL_CHUNK, KV_W), BF16)

    for g in range(NSA_KV_HEADS):
        sel = sel2[g * Q_BLOCK:(g + 1) * Q_BLOCK]
        m_scr[...] = jnp.full(m_scr.shape, NEG, F32)
        acc_scr[...] = jnp.zeros(acc_scr.shape, F32)

        def spread(i, carry):
            c0 = pl.multiple_of(i * SPREAD_KEYS, SPREAD_KEYS)
            keep = (_dot(sel, ex_ref[:, pl.ds(c0, SPREAD_KEYS)]) > 0.5) & (key_in_span + c0 <= tok)
            chosen_scr[:, pl.ds(c0, SPREAD_KEYS)] = jnp.where(keep, 0.0, NEG)
            return carry

        lax.fori_loop(0, (n_chunks * SEL_CHUNK + SPREAD_KEYS - 1) // SPREAD_KEYS, spread, 0)

        heads = range(NSA_GROUP)
        rows = [pl.ds(jh * Q_BLOCK, Q_BLOCK) for jh in heads]

        def stage_scores(c, buf):
            k0 = pl.multiple_of(c * SEL_CHUNK, SEL_CHUNK)
            bias = chosen_scr[:, pl.ds(k0, SEL_CHUNK)]
            kblk = ks_ref[0, pl.ds(k0, SEL_CHUNK), :]
            for jh in heads:
                h = NSA_GROUP * g + jh
                buf[rows[jh], :] = _dot_nt(q_ref[0, :, h * LANES:(h + 1) * LANES], kblk) + bias

        def accumulate(c, buf):
            k0 = pl.multiple_of(c * SEL_CHUNK, SEL_CHUNK)
            vext = jnp.concatenate([vs_ref[0, pl.ds(k0, SEL_CHUNK), :], ones_blk], axis=1)
            tiles = [[buf[r, i * LANES:(i + 1) * LANES] for i in range(SEL_CHUNK // LANES)] for r in rows]
            m_old = [m_scr[r, :] for r in rows]
            m_new = [jnp.maximum(mo, jnp.max(functools.reduce(jnp.maximum, t), axis=1, keepdims=True))
                     for mo, t in zip(m_old, tiles)]
            probs = [jnp.concatenate([jnp.exp2(x - mn) for x in t], axis=1).astype(BF16)
                     for mn, t in zip(m_new, tiles)]
            pv = [_dot(p, vext) for p in probs]
            for r, mo, mn, y in zip(rows, m_old, m_new, pv):
                alpha = jnp.exp2(mo - mn)
                acc_scr[r, :] = jnp.concatenate([alpha, alpha], axis=1) * acc_scr[r, :] + y
                m_scr[r, :] = mn

        stage_scores(0, sa_scr)

        def body(i, carry):
            stage_scores(2 * i + 1, sb_scr)
            accumulate(2 * i, sa_scr)
            stage_scores(jnp.minimum(2 * i + 2, n_chunks - 1), sa_scr)
            accumulate(2 * i + 1, sb_scr)
            return carry

        lax.fori_loop(0, n_chunks // 2, body, 0)

        @pl.when(n_chunks % 2 == 1)
        def _():
            accumulate(n_chunks - 1, sa_scr)
        o_s = acc_scr[:, :KV_W] / jnp.maximum(acc_scr[:, KV_W:], 1e-30)

        for jh in range(NSA_GROUP):
            h = NSA_GROUP * g + jh
            r = slice(jh * Q_BLOCK, (jh + 1) * Q_BLOCK)
            o = oc_scr[g, r, :] + gates[:, 3 * h + 1:3 * h + 2] * o_s[r]
            o_ref[0, :, h * LANES:(h + 1) * LANES] = o.astype(o_ref.dtype)


def _nsa_prompt(qpad, gates, kc, vc, kvsel, kvwb):
    nb, t, _ = qpad.shape
    assert t % SPREAD_KEYS == 0 and SPREAD_KEYS == 2 * SEL_CHUNK and t >= WIN_SPAN
    n_cmp = kc.shape[1]
    n_sel = t // SEL_BLOCK
    full = lambda w, k: pl.BlockSpec((1, t, w), lambda b, j: (b, 0, k))
    t_pad = -(-t // SPREAD_KEYS) * SPREAD_KEYS
    expand = jnp.asarray(np.arange(n_sel)[:, None] == (np.arange(t_pad)[None, :] // SEL_BLOCK), BF16)
    return pl.pallas_call(
        functools.partial(_nsa_prompt_kernel, n_cmp=n_cmp, n_sel=n_sel),
        grid=(nb, t // Q_BLOCK),
        in_specs=[pl.BlockSpec((1, Q_BLOCK, QPAD_W), lambda b, j: (b, j, 0)),
                  pl.BlockSpec((1, Q_BLOCK, GZ_PAD), lambda b, j: (b, j, 0)),
                  pl.BlockSpec((1, n_cmp, KV_W), lambda b, j: (b, 0, 0)),
                  pl.BlockSpec((1, n_cmp, KV_W), lambda b, j: (b, 0, 0)),
                  full(KV_W, 0), full(KV_W, 1), full(KV_W, 0), full(KV_W, 1),
                  pl.BlockSpec((n_sel, t_pad), lambda b, j: (0, 0))],
        out_specs=pl.BlockSpec((1, Q_BLOCK, QPAD_W), lambda b, j: (b, j, 0)),
        out_shape=jax.ShapeDtypeStruct((nb, t, QPAD_W), BF16),
        scratch_shapes=[pltpu.VMEM((NSA_GROUP * Q_BLOCK, LANES), F32),
                        pltpu.VMEM((NSA_GROUP * Q_BLOCK, 2 * KV_W), F32),
                        pltpu.VMEM((Q_BLOCK, t_pad), F32),
                        pltpu.VMEM((NSA_KV_HEADS, NSA_GROUP * Q_BLOCK, KV_W), F32),
                        pltpu.VMEM((NSA_GROUP * Q_BLOCK, SEL_CHUNK), F32),
                        pltpu.VMEM((NSA_GROUP * Q_BLOCK, SEL_CHUNK), F32)],
        compiler_params=_cparams(("arbitrary", "arbitrary")),
        name="nsa_prompt",
    )(qpad, gates, kc, vc, kvsel, kvsel, kvwb, kvwb, expand)


def _nsa_sample_a_kernel(q_ref, g_ref, kc_ref, vc_ref, wb_ref, nw_ref, ocw_ref, sel_ref,
                         *, past_len, n_tok, n_sel, n_sel_pad):
    q = q_ref[0]
    rows = q.shape[0]
    n_cmp = kc_ref.shape[1]
    t_row = lax.broadcasted_iota(jnp.int32, (rows, 1), 0) & (n_tok - 1)
    qpos = past_len + t_row
    cend = lax.broadcasted_iota(jnp.int32, (1, n_cmp), 1) * CMP_STRIDE + (CMP_BLOCK - 1)
    p_c = _masked_softmax(_dot_nt(q, kc_ref[0]), cend <= qpos)
    o_c = _dot(p_c.astype(BF16), vc_ref[0])

    per_grp = NSA_GROUP * n_tok
    psum = jnp.concatenate(
        [sum(p_c[g * per_grp + jh * n_tok:g * per_grp + (jh + 1) * n_tok] for jh in range(NSA_GROUP))
         for g in range(NSA_KV_HEADS)], axis=0)
    imp = _dot(psum.astype(BF16), _overlap_matrix(n_cmp, n_sel_pad))
    blk = lax.broadcasted_iota(jnp.int32, (1, n_sel_pad), 1)
    tq = past_len + (lax.broadcasted_iota(jnp.int32, (NSA_KV_HEADS * n_tok, 1), 0) & (n_tok - 1))
    cur = lax.shift_right_logical(tq, SEL_SHIFT)
    forced = (blk == 0) | (blk == cur) | (blk == cur - 1)
    allowed = blk * SEL_BLOCK <= tq
    v = jnp.where(forced, -NEG, jnp.where(allowed, imp, NEG))
    sel_ref[0] = _topk_mask(jnp.where(blk < n_sel, v, 2.0 * NEG), SEL_TOPN)

    wb = wb_ref.shape[1]
    kw = wb_ref[0, :, 0:KV_W].astype(BF16)
    vw = wb_ref[0, :, KV_W:2 * KV_W].astype(BF16)
    kn = nw_ref[0, :, 0:KV_W].astype(BF16)
    vn = nw_ref[0, :, KV_W:2 * KV_W].astype(BF16)
    i1 = lax.broadcasted_iota(jnp.int32, (1, wb), 1)
    d1 = t_row + wb - i1
    valid1 = (d1 >= 0) & (d1 < WINDOW) & (past_len - wb + i1 >= 0)
    i2 = lax.broadcasted_iota(jnp.int32, (1, nw_ref.shape[1]), 1)
    d2 = t_row - i2
    valid2 = (d2 >= 0) & (d2 < WINDOW) & (i2 < n_tok)
    s1 = jnp.where(valid1, _dot_nt(q, kw), NEG)
    s2 = jnp.where(valid2, _dot_nt(q, kn), NEG)
    m = jnp.maximum(jnp.max(s1, axis=1, keepdims=True), jnp.max(s2, axis=1, keepdims=True))
    p1 = jnp.exp2(s1 - m) * valid1.astype(F32)
    p2 = jnp.exp2(s2 - m) * valid2.astype(F32)
    den = jnp.maximum(jnp.sum(p1, axis=1, keepdims=True) + jnp.sum(p2, axis=1, keepdims=True), 1e-30)
    o_w = (_dot(p1.astype(BF16), vw) + _dot(p2.astype(BF16), vn)) / den
    g = g_ref[0]
    ocw_ref[0] = g[:, 0:1] * o_c + g[:, 2:3] * o_w


def _nsa_sample_a(q_rows, g_rows, kc, vc, win_buf, new_win, *, past_len, n_tok):
    nb, rows, _ = q_rows.shape
    n_sel = -(-(past_len + n_tok) // SEL_BLOCK)
    n_sel_pad = -(-n_sel // LANES) * LANES
    blk3 = lambda a: pl.BlockSpec((1,) + a.shape[1:], lambda b: (b, 0, 0))
    return pl.pallas_call(
        functools.partial(_nsa_sample_a_kernel, past_len=past_len, n_tok=n_tok, n_sel=n_sel, n_sel_pad=n_sel_pad),
        grid=(nb,),
        in_specs=[blk3(q_rows), blk3(g_rows), blk3(kc), blk3(vc), blk3(win_buf), blk3(new_win)],
        out_specs=[pl.BlockSpec((1, rows, KV_W), lambda b: (b, 0, 0)),
                   pl.BlockSpec((1, NSA_KV_HEADS * n_tok, n_sel_pad), lambda b: (b, 0, 0))],
        out_shape=[jax.ShapeDtypeStruct((nb, rows, KV_W), F32),
                   jax.ShapeDtypeStruct((nb, NSA_KV_HEADS * n_tok, n_sel_pad), F32)],
        compiler_params=_cparams(("arbitrary",)),
        name="nsa_sample_a",
    )(q_rows, g_rows, kc, vc, win_buf, new_win)


def _nsa_sample_b_kernel(pt_ref, cache_ref, q_ref, g_ref, sel_ref, seln_ref, ns_ref, ocw_ref, ex_ref, o_ref,
                         buf, sem, m_scr, l_scr, acc_scr, *, n_pages, n_steps, pps, n_tok):
    s = pl.program_id(1)
    slot = _stream_pages(pt_ref, cache_ref, buf, sem, n_pages, n_steps, pps, 2 * KV_W, False, True)
    q = q_ref[0]
    rows = q.shape[0]

    @pl.when(s == 0)
    def _():
        m_scr[...] = jnp.full(m_scr.shape, NEG, F32)
        l_scr[...] = jnp.zeros(l_scr.shape, F32)
        acc_scr[...] = jnp.zeros(acc_scr.shape, F32)

    def update(scores, msk, times_v):
        sc = jnp.where(msk, scores, NEG)
        m_old = m_scr[...]
        m_new = jnp.maximum(m_old, jnp.max(sc, axis=1, keepdims=True))
        p = jnp.exp2(sc - m_new) * msk.astype(F32)
        alpha = jnp.exp2(m_old - m_new)
        l_scr[...] = alpha * l_scr[...] + jnp.sum(p, axis=1, keepdims=True)
        acc_scr[...] = alpha * acc_scr[...] + times_v(p.astype(BF16))
        m_scr[...] = m_new

    chosen = _dot(sel_ref[0, 0], ex_ref[...]) > 0.5
    update(_dot(q, buf[slot, 0].astype(BF16)), chosen, lambda p: _dot_nt(p, buf[slot, 1].astype(BF16)))

    @pl.when(s == n_steps - 1)
    def _():
        t_row = lax.broadcasted_iota(jnp.int32, (rows, 1), 0) & (n_tok - 1)
        i2 = lax.broadcasted_iota(jnp.int32, (1, ns_ref.shape[1]), 1)
        msk = (seln_ref[0, 0][:, 0:1] > 0.5) & (i2 <= t_row) & (i2 < n_tok)
        update(_dot_nt(q, ns_ref[0, :, 0:KV_W].astype(BF16)), msk,
               lambda p: _dot(p, ns_ref[0, :, KV_W:2 * KV_W].astype(BF16)))
        o_s = acc_scr[...] / jnp.maximum(l_scr[...], 1e-30)
        o_ref[0] = ocw_ref[0] + g_ref[0][:, 1:2] * o_s


def _nsa_sample_b(page_table, cache, q_rows, g_rows, sel_steps, new_sel, ocw, *, n_tok, pps=PAGES_PER_STEP):
    nb, n_pages = page_table.shape
    n_steps = n_pages // pps
    rows = q_rows.shape[1]
    keys = pps * PAGE_SIZE
    expand = jnp.asarray(np.arange(LANES)[:, None] == (np.arange(keys)[None, :] // SEL_BLOCK), BF16)
    per_b = lambda a: pl.BlockSpec((1,) + a.shape[1:], lambda b, s, pt: (b, 0, 0))
    return pl.pallas_call(
        functools.partial(_nsa_sample_b_kernel, n_pages=n_pages, n_steps=n_steps, pps=pps, n_tok=n_tok),
        grid_spec=pltpu.PrefetchScalarGridSpec(
            num_scalar_prefetch=1,
            grid=(nb, n_steps),
            in_specs=[pl.BlockSpec(memory_space=pl.ANY),
                      per_b(q_rows), per_b(g_rows),
                      pl.BlockSpec((1, 1, rows, LANES), lambda b, s, pt: (b, s, 0, 0)),
                      pl.BlockSpec((1, 1, rows, LANES), lambda b, s, pt: (b, n_steps, 0, 0)),
                      per_b(new_sel), per_b(ocw),
                      pl.BlockSpec((LANES, keys), lambda b, s, pt: (0, 0))],
            out_specs=pl.BlockSpec((1, rows, KV_W), lambda b, s, pt: (b, 0, 0)),
            scratch_shapes=[pltpu.VMEM((2, 2, KV_W, keys), F32), pltpu.SemaphoreType.DMA((2,)),
                            pltpu.VMEM((rows, 1), F32), pltpu.VMEM((rows, 1), F32),
                            pltpu.VMEM((rows, KV_W), F32)]),
        out_shape=jax.ShapeDtypeStruct((nb, rows, KV_W), F32),
        compiler_params=_cparams(("arbitrary", "arbitrary")),
        name="nsa_sample_b",
    )(page_table, cache, q_rows, g_rows, sel_steps, sel_steps, new_sel, ocw, expand)


def _nsa_sample(page_table, cache, win_buf, cw, qpad, gates, kv4, kvw, pps=PAGES_PER_STEP):
    nb, ts, _ = qpad.shape
    past_len = page_table.shape[1] * PAGE_SIZE
    kc, vc = _compress(page_table, cache, *cw, pps=pps, transposed=True)
    rows = NSA_HEADS * ts
    q_rows = qpad.reshape(nb, ts, NSA_HEADS, LANES).transpose(0, 2, 1, 3).reshape(nb, rows, LANES)
    g_rows = gates[:, :, :3 * NSA_HEADS].reshape(nb, ts, NSA_HEADS, 3).transpose(0, 2, 1, 3)
    g_rows = jnp.pad(g_rows.reshape(nb, rows, 3), ((0, 0), (0, 0), (0, LANES - 3)))
    pad_rows = lambda a: jnp.pad(a, ((0, 0), (0, LANES - ts), (0, 0)))
    new_win = pad_rows(kvw)
    new_sel = pad_rows(kv4[:, :, 2 * KV_W:])
    ocw, sel = _nsa_sample_a(q_rows, g_rows, kc, vc, win_buf, new_win, past_len=past_len, n_tok=ts)
    n_steps = page_table.shape[1] // pps
    blk_per_step = pps * PAGE_SIZE // SEL_BLOCK
    n_past_blk = n_steps * blk_per_step
    sel_past = sel[:, :, :n_past_blk].reshape(nb, NSA_KV_HEADS, 1, ts, n_steps, blk_per_step)
    sel_past = jnp.broadcast_to(sel_past, (nb, NSA_KV_HEADS, NSA_GROUP, ts, n_steps, blk_per_step))
    sel_past = sel_past.transpose(0, 4, 1, 2, 3, 5).reshape(nb, n_steps, rows, blk_per_step)
    sel_past = jnp.pad(sel_past, ((0, 0), (0, 0), (0, 0), (0, LANES - blk_per_step)))
    sel_new = jnp.pad(sel[:, :, n_past_blk:], ((0, 0), (0, 0), (0, LANES)))[:, :, :LANES]
    sel_new = sel_new.reshape(nb, NSA_KV_HEADS, 1, ts, LANES)
    sel_new = jnp.broadcast_to(sel_new, (nb, NSA_KV_HEADS, NSA_GROUP, ts, LANES)).reshape(nb, 1, rows, LANES)
    sel_steps = jnp.concatenate([sel_past, sel_new], axis=1).astype(BF16)
    o_rows = _nsa_sample_b(page_table, cache, q_rows, g_rows, sel_steps, new_sel, ocw, n_tok=ts, pps=pps)
    return o_rows.reshape(nb, NSA_HEADS, ts, LANES).transpose(0, 2, 1, 3).reshape(nb, ts, QPAD_W).astype(BF16)


def _outproj_kernel(x_ref, hg_ref, nsa_ref, g1_ref, sc2_ref, sh2_ref, fn_ref, wo1_ref, wo2_ref,
                    x1_ref, h2_ref, *, tm):
    mix =_dot(hg_ref[0], wo1_ref[...]) + _dot(nsa_ref[0], wo2_ref[...])
    x1 = x_ref[0] + g1_ref[0] * mix
    x1_ref[0] = x1
    h2 = _rms(x1, fn_ref[...]) * (1.0 + sc2_ref[0]) + sh2_ref[0]
    _store_tok_tiles(h2_ref, h2, tm)


def _outproj(x, hg_out, nsa, gate1, scale2, shift2, ffn_norm, wo_hg, wo_nsa, tm):
    nb, t, _ = x.shape
    nt = t // tm
    mod_spec = _mod_spec(gate1, tm)
    tile = lambda w: pl.BlockSpec((1, tm, w), lambda b, i: (b, i, 0))
    return pl.pallas_call(
        functools.partial(_outproj_kernel, tm=tm),
        grid=(nb, nt),
        in_specs=[tile(D_MODEL), tile(HG_WIDTH), tile(QPAD_W), mod_spec, mod_spec, mod_spec,
                  pl.BlockSpec((1, D_MODEL), lambda b, i: (0, 0)),
                  pl.BlockSpec((HG_WIDTH, D_MODEL), lambda b, i: (0, 0)),
                  pl.BlockSpec((QPAD_W, D_MODEL), lambda b, i: (0, 0))],
        out_specs=[tile(D_MODEL), pl.BlockSpec((tm * TOK_ROWS, LANES), lambda b, i: (b * nt + i, 0))],
        out_shape=[jax.ShapeDtypeStruct((nb, t, D_MODEL), F32),
                   jax.ShapeDtypeStruct((nb * t * TOK_ROWS, LANES), F32)],
        compiler_params=_cparams(("arbitrary", "arbitrary")),
        name="out_proj",
    )(x, hg_out, nsa, gate1, scale2, shift2, ffn_norm.reshape(1, -1), wo_hg, wo_nsa)


def _split_w_out(w_out):
    wo_hg = w_out[:HG_WIDTH].astype(BF16)
    wn = w_out[HG_WIDTH:].reshape(NSA_HEADS, NSA_HEAD_DIM, D_MODEL)
    z = jnp.zeros_like(wn)
    grp = (jnp.arange(NSA_HEADS) // NSA_GROUP)[:, None, None]
    wn_pad = jnp.where(grp == 0, jnp.concatenate([wn, z], axis=1), jnp.concatenate([z, wn], axis=1))
    return wo_hg, wn_pad.reshape(QPAD_W, D_MODEL).astype(BF16)


def _router_kernel(h_ref, wr_ref, b_ref, e_ref, w_ref, r_ref, cnt_ref, run_scr, *, tm):
    @pl.when(pl.program_id(0) == 0)
    def _():
        run_scr[...] = jnp.zeros(run_scr.shape, F32)

    x = _load_tok_tiles(h_ref, tm).astype(BF16)
    scores = jax.nn.sigmoid(_dot(x, wr_ref[...]))
    biased = scores + b_ref[...]
    lane_i = lax.broadcasted_iota(jnp.int32, (tm, N_EXPERTS), 1)
    lane = lane_i.astype(F32)
    grp_of_lane = lax.shift_right_logical(lane_i, GROUP_SHIFT)

    gcol = lax.broadcasted_iota(jnp.int32, (tm, LANES), 1)
    gs = jnp.full((tm, LANES), 2.0 * NEG, F32)
    for g in range(N_GROUPS):
        mg = jnp.where(grp_of_lane == g, biased, NEG)
        m1 = jnp.max(mg, axis=1, keepdims=True)
        i1 = jnp.min(jnp.where(mg == m1, lane, 1e9), axis=1, keepdims=True)
        m2 = jnp.max(jnp.where(lane == i1, NEG, mg), axis=1, keepdims=True)
        gs = jnp.where(gcol == g, m1 + m2, gs)
    gsel = _topk_mask(gs, TOPK_GROUPS).astype(BF16)
    spread = (lax.broadcasted_iota(jnp.int32, (LANES, N_EXPERTS), 0)
              == lax.shift_right_logical(lax.broadcasted_iota(jnp.int32, (LANES, N_EXPERTS), 1), GROUP_SHIFT)
              ).astype(BF16)
    v = jnp.where(_dot(gsel, spread) > 0.5, biased, NEG)

    onehot = jnp.zeros((tm, N_EXPERTS), F32)
    idxs, wts = [], []
    wsum = jnp.zeros((tm, 1), F32)
    for _ in range(TOP_K):
        m = jnp.max(v, axis=1, keepdims=True)
        idx = jnp.min(jnp.where(v == m, lane, 1e9), axis=1, keepdims=True)
        pick = lane == idx
        wk = jnp.sum(jnp.where(pick, scores, 0.0), axis=1, keepdims=True)
        onehot = jnp.where(pick, 1.0, onehot)
        v = jnp.where(pick, 3.0 * NEG, v)
        idxs.append(idx)
        wts.append(wk)
        wsum = wsum + wk

    earlier = (lax.broadcasted_iota(jnp.int32, (tm, tm), 0) > lax.broadcasted_iota(jnp.int32, (tm, tm), 1))
    before = _dot(earlier.astype(BF16), onehot.astype(BF16)) + run_scr[...]
    e_out = jnp.zeros((tm, LANES), jnp.int32)
    r_out = jnp.zeros((tm, LANES), jnp.int32)
    w_out = jnp.zeros((tm, LANES), F32)
    for k in range(TOP_K):
        rk = jnp.sum(jnp.where(lane == idxs[k], before, 0.0), axis=1, keepdims=True)
        e_out = jnp.where(gcol == k, idxs[k].astype(jnp.int32), e_out)
        r_out = jnp.where(gcol == k, rk.astype(jnp.int32), r_out)
        w_out = jnp.where(gcol == k, wts[k] / wsum * ROUTED_SCALE, w_out)
    e_ref[...] = e_out
    r_ref[...] = r_out
    w_ref[...] = w_out
    run_scr[...] = run_scr[...] + jnp.sum(onehot, axis=0, keepdims=True)
    cnt_ref[...] = run_scr[...]


def _router(h2, w_router, bias, n_tok, tm=ROUTER_TM):
    tile = pl.BlockSpec((tm, LANES), lambda i: (i, 0))
    return pl.pallas_call(
        functools.partial(_router_kernel, tm=tm),
        grid=(n_tok // tm,),
        in_specs=[pl.BlockSpec((tm * TOK_ROWS, LANES), lambda i: (i, 0)),
                  pl.BlockSpec((D_MODEL, N_EXPERTS), lambda i: (0, 0)),
                  pl.BlockSpec((1, N_EXPERTS), lambda i: (0, 0))],
        out_specs=[tile, tile, tile, pl.BlockSpec((1, N_EXPERTS), lambda i: (0, 0))],
        out_shape=[jax.ShapeDtypeStruct((n_tok, LANES), jnp.int32),
                   jax.ShapeDtypeStruct((n_tok, LANES), F32),
                   jax.ShapeDtypeStruct((n_tok, LANES), jnp.int32),
                   jax.ShapeDtypeStruct((1, N_EXPERTS), F32)],
        scratch_shapes=[pltpu.VMEM((1, N_EXPERTS), F32)],
        compiler_params=_cparams(("arbitrary",)),
        name="moe_router",
    )(h2, w_router, bias)


def _dest_kernel(e_ref, r_ref, st_ref, d_ref):
    e = e_ref[...]
    tm = e.shape[0]
    lane = lax.broadcasted_iota(jnp.int32, (tm, N_EXPERTS), 1)
    col = lax.broadcasted_iota(jnp.int32, (tm, LANES), 1)
    st = st_ref[...]
    out = r_ref[...]
    for k in range(TOP_K):
        sk = jnp.sum(jnp.where(lane == e[:, k:k + 1], st, 0.0), axis=1, keepdims=True)
        out = jnp.where(col == k, out + sk.astype(jnp.int32), out)
    d_ref[...] = out


def _moe_dest(top_e, rank, starts, tm=ROUTER_TM):
    n_tok = top_e.shape[0]
    tile = pl.BlockSpec((tm, LANES), lambda i: (i, 0))
    return pl.pallas_call(
        _dest_kernel,
        grid=(n_tok // tm,),
        in_specs=[tile, tile, pl.BlockSpec((1, N_EXPERTS), lambda i: (0, 0))],
        out_specs=tile,
        out_shape=jax.ShapeDtypeStruct((n_tok, LANES), jnp.int32),
        compiler_params=_cparams(("arbitrary",)),
        name="moe_dest",
    )(top_e, rank, starts.astype(F32).reshape(1, -1))


def _moe_layout(counts, n_pairs):
    padded = (counts + MOE_BM - 1) // MOE_BM * MOE_BM
    pad_end = jnp.cumsum(padded)
    starts = pad_end - padded
    n_blocks = (n_pairs + N_EXPERTS * (MOE_BM - 1)) // MOE_BM
    n_used = pad_end[-1] // MOE_BM
    blk = jnp.arange(n_blocks, dtype=jnp.int32)
    used = blk < n_used
    e_of = jnp.sum(pad_end[None, :] <= (jnp.minimum(blk, n_used - 1) * MOE_BM)[:, None], axis=1).astype(jnp.int32)
    e_of = jnp.minimum(e_of, N_EXPERTS - 1)
    shifted = jnp.concatenate([jnp.full((1,), -1, jnp.int32), e_of[:-1]])
    fresh = (used & (e_of != shifted)).astype(jnp.int32)
    w_slot = ((jnp.cumsum(fresh) - 1) % 2).astype(jnp.int32)
    ids = jnp.arange(N_EXPERTS, dtype=jnp.int32)
    later = jnp.where(counts > 0, ids, N_EXPERTS)
    next_nonempty = jnp.concatenate([lax.cummin(later, reverse=True)[1:], jnp.full((1,), N_EXPERTS, jnp.int32)])
    next_e = next_nonempty[e_of].astype(jnp.int32)
    pad_table = jnp.concatenate([starts + counts, padded - counts, n_used[None]]).astype(jnp.int32)
    return (starts.astype(jnp.int32), pad_table,
            (e_of, fresh, used.astype(jnp.int32), w_slot, next_e), n_blocks)


def _dispatch_kernel(pad_ref, dest_ref, h_ref, xs_ref, sem, zeros, zsem, *, tm, n_blocks, bm):
    i = pl.program_id(0)
    n_steps = pl.num_programs(0)

    def row_copy(src_tok, dst_row):
        return pltpu.make_async_copy(
            h_ref.at[pl.ds(pl.multiple_of(src_tok * TOK_ROWS, TOK_ROWS), TOK_ROWS), :],
            xs_ref.at[pl.ds(pl.multiple_of(dst_row * TOK_ROWS, TOK_ROWS), TOK_ROWS), :], sem)

    def issue(t, carry):
        for k in range(TOP_K):
            row_copy(t, dest_ref[0, 0, t * TOP_K + k]).start(priority=k % 2)
        return carry

    def drain(t, carry):
        for _ in range(TOP_K):
            row_copy(0, 0).wait()
        return carry

    @pl.when(i == 0)
    def _():
        zeros[...] = jnp.zeros(zeros.shape, F32)

    def zero_fill(row0, n_rows_static):
        return pltpu.make_async_copy(
            zeros.at[pl.ds(0, n_rows_static * TOK_ROWS), :],
            xs_ref.at[pl.ds(pl.multiple_of(row0 * TOK_ROWS, TOK_ROWS), n_rows_static * TOK_ROWS), :], zsem)

    experts_per_step = -(-N_EXPERTS // n_steps)
    blocks_per_step = -(-n_blocks // n_steps)
    n_used = pad_ref[2 * N_EXPERTS]

    def fill_pass(act):
        def one_expert(j, carry):
            e = jnp.minimum(i * experts_per_step + j, N_EXPERTS - 1)
            live = i * experts_per_step + j < N_EXPERTS
            lo, length = pad_ref[e], pad_ref[N_EXPERTS + e]
            piece = bm // 2
            while piece >= 1:
                @pl.when(live & ((length & piece) != 0))
                def _(piece=piece):
                    act(zero_fill(lo + (length & ~(2 * piece - 1)), piece))
                piece //= 2
            return carry
        lax.fori_loop(0, experts_per_step, one_expert, 0)

        def one_block(j, carry):
            blk = n_used + i * blocks_per_step + j

            @pl.when(blk < n_blocks)
            def _():
                act(zero_fill(blk * bm, bm))
            return carry
        lax.fori_loop(0, blocks_per_step, one_block, 0)

    fill_pass(lambda cp: cp.start())
    lax.fori_loop(0, tm, issue, 0)
    lax.fori_loop(0, tm, drain, 0)
    fill_pass(lambda cp: cp.wait())


def _dispatch(dest_tiles, h2, pad_table, n_blocks, tm=ROUTER_TM, bm=MOE_BM):
    n_tiles = dest_tiles.shape[0]
    return pl.pallas_call(
        functools.partial(_dispatch_kernel, tm=tm, n_blocks=n_blocks, bm=bm),
        grid_spec=pltpu.PrefetchScalarGridSpec(
            num_scalar_prefetch=1,
            grid=(n_tiles,),
            in_specs=[pl.BlockSpec((1, 1, tm * TOP_K), lambda i, pad: (i, 0, 0), memory_space=pltpu.SMEM),
                      pl.BlockSpec((tm * TOK_ROWS, LANES), lambda i, pad: (i, 0))],
            out_specs=pl.BlockSpec(memory_space=pl.ANY),
            scratch_shapes=[pltpu.SemaphoreType.DMA(()), pltpu.VMEM((bm * TOK_ROWS, LANES), F32),
                            pltpu.SemaphoreType.DMA(())]),
        out_shape=jax.ShapeDtypeStruct((n_blocks * bm * TOK_ROWS, LANES), F32),
        compiler_params=_cparams(("arbitrary",)),
        name="moe_dispatch",
    )(pad_table, dest_tiles, h2)


GMM_SLOTS = 4


def _gmm_kernel(e_ref, fresh_ref, used_ref, wslot_ref, next_ref,
                xs_ref, wg_ref, wu_ref, wd_ref, ys_ref, wg_bf, wu_bf, wd_bf, xbuf, sem,
                wg_st, wu_st, wd_st, wsem, *, bm):
    i = pl.program_id(0)
    n = pl.num_programs(0)

    def fetch(j):
        rows = pl.ds(pl.multiple_of(j * (bm * TOK_ROWS), bm * TOK_ROWS), bm * TOK_ROWS)
        return pltpu.make_async_copy(xs_ref.at[rows, :], xbuf.at[j % GMM_SLOTS], sem.at[j % GMM_SLOTS])

    def weight_copies(e, slot):
        return [pltpu.make_async_copy(src.at[e], dst.at[slot], wsem.at[slot])
                for src, dst in ((wg_ref, wg_st), (wu_ref, wu_st), (wd_ref, wd_st))]

    @pl.when(i == 0)
    def _():
        for cp in weight_copies(e_ref[0], wslot_ref[0]):
            cp.start()
        for j in range(GMM_SLOTS - 1):
            @pl.when((j < n) & (used_ref[jnp.minimum(j, n - 1)] == 1))
            def _():
                fetch(j).start()

    ahead = jnp.minimum(i + GMM_SLOTS - 1, n - 1)

    @pl.when((i + GMM_SLOTS - 1 < n) & (used_ref[ahead] == 1))
    def _():
        fetch(ahead).start()

    @pl.when(fresh_ref[i] == 1)
    def _():
        slot = wslot_ref[i]
        for cp in weight_copies(e_ref[i], slot):
            cp.wait()
        wg_bf[...] = wg_st[slot].astype(BF16)
        wu_bf[...] = wu_st[slot].astype(BF16)
        wd_bf[...] = wd_st[slot].astype(BF16)

        @pl.when(next_ref[i] < N_EXPERTS)
        def _():
            for cp in weight_copies(next_ref[i], 1 - slot):
                cp.start()

    @pl.when(used_ref[i] == 1)
    def _():
        fetch(i).wait()
        x = _load_tok_tiles(xbuf.at[i % GMM_SLOTS], bm).astype(BF16)
        hid = (_silu(_dot(x, wg_bf[...])) * _dot(x, wu_bf[...])).astype(BF16)
        _store_tok_tiles(ys_ref, _dot(hid, wd_bf[...]), bm)

    @pl.when(used_ref[i] == 0)
    def _():
        ys_ref[...] = jnp.zeros(ys_ref.shape, F32)


def _moe_gmm(blocks, xs_sorted, w_gate, w_up, w_down, bm=MOE_BM):
    rows = pl.BlockSpec((bm * TOK_ROWS, LANES), lambda i, *_: (i, 0))
    in_hbm = pl.BlockSpec(memory_space=pl.ANY)
    weights = (w_gate, w_up, w_down)
    return pl.pallas_call(
        functools.partial(_gmm_kernel, bm=bm),
        grid_spec=pltpu.PrefetchScalarGridSpec(
            num_scalar_prefetch=len(blocks),
            grid=(blocks[0].shape[0],),
            in_specs=[in_hbm, in_hbm, in_hbm, in_hbm],
            out_specs=rows,
            scratch_shapes=[pltpu.VMEM(w.shape[1:], BF16) for w in weights]
            + [pltpu.VMEM((GMM_SLOTS, bm * TOK_ROWS, LANES), F32), pltpu.SemaphoreType.DMA((GMM_SLOTS,))]
            + [pltpu.VMEM((2,) + w.shape[1:], F32) for w in weights]
            + [pltpu.SemaphoreType.DMA((2,))]),
        out_shape=jax.ShapeDtypeStruct(xs_sorted.shape, F32),
        compiler_params=_cparams(("arbitrary",)),
        name="moe_experts",
    )(*blocks, xs_sorted, *weights)


def _combine_kernel(dest_ref, dnext_ref, w_ref, x1_ref, h_ref, g2_ref, wsg_ref, wsu_ref, wsd_ref, fn_ref, ys_ref,
                    o_ref, gbuf, sem, routed_scr, *, tm):
    n = pl.program_id(0) * pl.num_programs(1) + pl.program_id(1)
    total = pl.num_programs(0) * pl.num_programs(1)
    slot = n % 2

    def row_copy(src_row, p, sl):
        return pltpu.make_async_copy(
            ys_ref.at[pl.ds(pl.multiple_of(src_row * TOK_ROWS, TOK_ROWS), TOK_ROWS), :],
            gbuf.at[sl, pl.ds(pl.multiple_of(p * TOK_ROWS, TOK_ROWS), TOK_ROWS), :], sem.at[sl])

    def gather(rows_ref, sl):
        def issue(i, carry):
            for k in range(TOP_K):
                p = i * TOP_K + k
                row_copy(rows_ref[0, 0, p], p, sl).start(priority=k % 2)
            return carry
        lax.fori_loop(0, tm, issue, 0)

    def drain(i, carry):
        for _ in range(TOP_K):
            row_copy(0, 0, slot).wait()
        return carry

    @pl.when(n == 0)
    def _():
        gather(dest_ref, slot)

    @pl.when(n + 1 < total)
    def _():
        gather(dnext_ref, 1 - slot)

    h = _load_tok_tiles(h_ref, tm).astype(BF16)
    hid = (_silu(_dot(h, wsg_ref[...])) * _dot(h, wsu_ref[...])).astype(BF16)
    shared = _dot(hid, wsd_ref[...])
    lax.fori_loop(0, tm, drain, 0)

    def weigh(t, carry):
        acc = jnp.zeros((TOK_ROWS, LANES), F32)
        for k in range(TOP_K):
            p = t * TOP_K + k
            acc = acc + w_ref[0, 0, p] * gbuf[slot, pl.ds(pl.multiple_of(p * TOK_ROWS, TOK_ROWS), TOK_ROWS), :]
        routed_scr[pl.ds(pl.multiple_of(t * TOK_ROWS, TOK_ROWS), TOK_ROWS), :] = acc
        return carry

    lax.fori_loop(0, tm, weigh, 0)
    x2 = x1_ref[0] + g2_ref[0] * (_load_tok_tiles(routed_scr, tm) + shared)
    o_ref[0] = _rms(x2, fn_ref[...])


def _combine(dest_tiles, w_tiles, x1, h2, gate2, shared, fnorm, ys_sorted, tile0, tm=COMBINE_TM):
    nb, t, _ = x1.shape
    nt = t // tm
    flat = lambda b, i: tile0 + b * nt + i
    nxt = lambda b, i: tile0 + jnp.minimum(b * nt + i + 1, nb * nt - 1)
    mod_spec = _mod_spec(gate2, tm)
    const = lambda a: pl.BlockSpec(a.shape, lambda b, i: (0, 0))
    return pl.pallas_call(
        functools.partial(_combine_kernel, tm=tm),
        grid=(nb, nt),
        in_specs=[pl.BlockSpec((1, 1, tm * TOP_K), lambda b, i: (flat(b, i), 0, 0), memory_space=pltpu.SMEM),
                  pl.BlockSpec((1, 1, tm * TOP_K), lambda b, i: (nxt(b, i), 0, 0), memory_space=pltpu.SMEM),
                  pl.BlockSpec((1, 1, tm * TOP_K), lambda b, i: (flat(b, i), 0, 0), memory_space=pltpu.SMEM),
                  pl.BlockSpec((1, tm, D_MODEL), lambda b, i: (b, i, 0)),
                  pl.BlockSpec((tm * TOK_ROWS, LANES), lambda b, i: (flat(b, i), 0)),
                  mod_spec, const(shared[0]), const(shared[1]), const(shared[2]), const(fnorm),
                  pl.BlockSpec(memory_space=pl.ANY)],
        out_specs=pl.BlockSpec((1, tm, D_MODEL), lambda b, i: (b, i, 0)),
        out_shape=jax.ShapeDtypeStruct((nb, t, D_MODEL), F32),
        scratch_shapes=[pltpu.VMEM((2, tm * TOP_K * TOK_ROWS, LANES), F32), pltpu.SemaphoreType.DMA((2,)),
                        pltpu.VMEM((tm * TOK_ROWS, LANES), F32)],
        compiler_params=_cparams(("arbitrary", "arbitrary")),
        name="moe_combine",
    )(dest_tiles, dest_tiles, w_tiles, x1, h2, gate2, *shared, fnorm, ys_sorted)


def kernel(x_prompt, x_sample, c_prompt, c_sample, cache_nsa_kv, cache_win_kv, state_hgrn, page_table,
           attn_norm, ffn_norm, final_norm, hg_norm, w_ada, b_ada, w_in, hg_lb,
           cmp_pe, cmp_w1, cmp_b1, cmp_w2, w_out, w_router, router_bias,
           w_gate, w_up, w_down, ws_gate, ws_up, ws_down):
    nbp, t, _ = x_prompt.shape
    nbs, ts, _ = x_sample.shape
    ns = nbs * ts
    n_all = nbp * t + ns

    c_all = jnp.concatenate([c_prompt, c_sample], axis=0)
    c_all = jnp.pad(c_all, ((0, -c_all.shape[0] % SUBLANES), (0, 0)))
    mod = _ada(c_all, w_ada[0], b_ada[0])
    modp = mod[:nbp].reshape(nbp, 1, 6, D_MODEL)
    mods = jnp.repeat(mod[nbp:nbp + nbs].reshape(nbs, 1, 6, D_MODEL), ts, axis=1).reshape(1, ns, 6, D_MODEL)

    w_pad = _pad_w_in(w_in[0])
    cw = _compress_weights(cmp_pe[0], cmp_w1[0], cmp_b1[0], cmp_w2[0])
    wo_hg, wo_nsa = _split_w_out(w_out[0])

    hg, qpad, kv4, kvw, gates, kvsel, kvwb = _inproj(
        x_prompt, modp[:, :, 1], modp[:, :, 0], attn_norm[0], w_pad, PROJ_TM)
    hg_out, hg_state_p = _hgrn(hg, hg_lb, jnp.zeros((nbp, HG_HEADS, HG_DK, HG_DK), F32), hg_norm[0],
                               HGRN_TC, HG_CHUNK)
    n_pages_p = t // PAGE_SIZE
    ptp = jnp.arange(nbp * n_pages_p, dtype=jnp.int32).reshape(nbp, n_pages_p)
    kc, vc = _compress(ptp, kv4.reshape(nbp * n_pages_p, PAGE_SIZE, 4 * KV_W), *cw)
    nsa = _nsa_prompt(qpad, gates, kc, vc, kvsel, kvwb)
    x1p, h2p = _outproj(x_prompt, hg_out, nsa, modp[:, :, 2], modp[:, :, 4], modp[:, :, 3],
                        ffn_norm[0], wo_hg, wo_nsa, PROJ_TM)

    xs = x_sample.reshape(1, ns, D_MODEL)
    hg_s, qpad_s, kv4_s, kvw_s, gates_s, _, _ = _inproj(
        xs, mods[:, :, 1], mods[:, :, 0], attn_norm[0], w_pad, ns)
    hg_out_s, hg_state_s = _hgrn(hg_s.reshape(nbs, ts, 4 * HG_WIDTH), hg_lb, state_hgrn[0], hg_norm[0], ts, ts)
    cache = cache_nsa_kv[0].transpose(0, 2, 3, 4, 1).reshape(-1, 4 * KV_W, PAGE_SIZE)
    win_buf = cache_win_kv[0].reshape(nbs, -1, 2 * KV_W)
    nsa_s = _nsa_sample(page_table, cache, win_buf, cw, qpad_s.reshape(nbs, ts, QPAD_W),
                        gates_s.reshape(nbs, ts, GZ_PAD), kv4_s.reshape(nbs, ts, 4 * KV_W),
                        kvw_s.reshape(nbs, ts, 2 * KV_W)).reshape(1, ns, QPAD_W)
    x1s, h2s = _outproj(xs, hg_out_s.reshape(1, ns, HG_WIDTH), nsa_s, mods[:, :, 2], mods[:, :, 4], mods[:, :, 3],
                        ffn_norm[0], wo_hg, wo_nsa, ns)

    h2 = jnp.concatenate([h2p, h2s], axis=0)
    top_e, top_w, rank, counts = _router(h2, w_router[0].astype(BF16), router_bias[0].reshape(1, -1), n_all)
    starts, pad_table, blocks, n_blocks = _moe_layout(counts[0].astype(jnp.int32), n_all * TOP_K)
    dest = _moe_dest(top_e, rank, starts)[:, :TOP_K].reshape(-1)
    xs_sorted = _dispatch(dest.reshape(n_all // ROUTER_TM, 1, ROUTER_TM * TOP_K), h2, pad_table, n_blocks)
    ys_sorted = _moe_gmm(blocks, xs_sorted, w_gate[0], w_up[0], w_down[0])
    dest_c = dest.reshape(n_all // COMBINE_TM, 1, COMBINE_TM * TOP_K)
    w_c = top_w[:, :TOP_K].reshape(n_all // COMBINE_TM, 1, COMBINE_TM * TOP_K)
    shared = (ws_gate[0].astype(BF16), ws_up[0].astype(BF16), ws_down[0].astype(BF16))
    fnorm = final_norm.reshape(1, -1)
    y_prompt = _combine(dest_c, w_c, x1p, h2, modp[:, :, 5], shared, fnorm, ys_sorted, 0)
    y_sample = _combine(dest_c, w_c, x1s, h2, mods[:, :, 5], shared, fnorm, ys_sorted, nbp * t // COMBINE_TM)

    wb = win_buf.shape[1]
    win_p = kvw[:, t - min(WINDOW, t):]
    win_s = jnp.concatenate([win_buf, kvw_s.reshape(nbs, ts, 2 * KV_W)], axis=1)[:, -wb:]
    kv_shape = (4, NSA_KV_HEADS, NSA_HEAD_DIM)
    win_shape = (2, NSA_KV_HEADS, NSA_HEAD_DIM)
    return (y_prompt,
            y_sample.reshape(nbs, ts, D_MODEL),
            kv4.reshape(1, nbp, t, *kv_shape),
            win_p.reshape(1, nbp, -1, *win_shape),
            hg_state_p[None],
            kv4_s.reshape(1, nbs, ts, *kv_shape),
            win_s.reshape(1, nbs, wb, *win_shape),
            hg_state_s[None])
```

```python
import functools

import jax
import jax.numpy as jnp
import numpy as np
from jax import lax
from jax.experimental import pallas as pl
from jax.experimental.pallas import tpu as pltpu

F32 = jnp.float32
BF16 = jnp.bfloat16

D_MODEL = 1024
HG_WIDTH = 512
HG_HEADS = 4
HG_DK = 128
HG_CHUNK = 32
NSA_WIDTH = 512
NSA_HEADS = 8
NSA_HEAD_DIM = 64
NSA_KV_HEADS = 2
NSA_GROUP = 4
KV_W = 128
CMP_BLOCK = 32
CMP_STRIDE = 16
CMP_HIDDEN = 256
SEL_BLOCK = 64
SEL_TOPN = 16
WINDOW = 512
Q_BLOCK = 256
N_EXPERTS = 256
TOP_K = 8
N_GROUPS = 8
TOPK_GROUPS = 4
ROUTED_SCALE = 2.5
RMS_EPS = 1e-6
PAGE_SIZE = 128

LANES = 128
SUBLANES = 8
TOK_ROWS = D_MODEL // LANES
VMEM_LIMIT = 56 * 1024 * 1024

QPAD_W = NSA_HEADS * LANES
GZ_PAD = LANES
INP_COLS = 4 * HG_WIDTH + QPAD_W + 4 * KV_W + 2 * KV_W + GZ_PAD

SEL_SHIFT = SEL_BLOCK.bit_length() - 1
GROUP_SHIFT = (N_EXPERTS // N_GROUPS).bit_length() - 1
NEG = -1e30
PROJ_TM = 512
HGRN_TC = 256
PAGES_PER_STEP = 32
MOE_BM = 256
ROUTER_TM = 256
COMBINE_TM = 128


def _cparams(sem):
    return pltpu.CompilerParams(dimension_semantics=sem, vmem_limit_bytes=VMEM_LIMIT)


def _dot(a, b):
    return jnp.dot(a, b, preferred_element_type=F32)


def _dot_nt(a, b):
    return lax.dot_general(a, b, (((1,), (1,)), ((), ())), preferred_element_type=F32)


def _dot_tn(a, b):
    return lax.dot_general(a, b, (((0,), (0,)), ((), ())), preferred_element_type=F32)


def _rms(x, g):
    return x * lax.rsqrt(jnp.mean(x * x, axis=-1, keepdims=True) + RMS_EPS) * g


def _silu(x):
    return x * jax.nn.sigmoid(x)


Q_SCALE = NSA_HEAD_DIM ** -0.5 * 1.4426950408889634


def _masked_softmax(s, valid):
    s = jnp.where(valid, s, NEG)
    m = jnp.max(s, axis=1, keepdims=True)
    p = jnp.exp2(s - m) * valid.astype(F32)
    return p / jnp.maximum(jnp.sum(p, axis=1, keepdims=True), 1e-30)


def _topk_mask(v, k):
    lane = lax.broadcasted_iota(jnp.int32, v.shape, 1).astype(F32)
    sel = jnp.zeros(v.shape, F32)
    for _ in range(k):
        m = jnp.max(v, axis=1, keepdims=True)
        idx = jnp.min(jnp.where(v == m, lane, 1e9), axis=1, keepdims=True)
        pick = lane == idx
        sel = jnp.where(pick, 1.0, sel)
        v = jnp.where(pick, 3.0 * NEG, v)
    return sel


def _mod_spec(mod, tm):
    if mod.shape[1] == 1:
        return pl.BlockSpec((1, 1, D_MODEL), lambda b, i: (b, 0, 0))
    return pl.BlockSpec((1, tm, D_MODEL), lambda b, i: (b, i, 0))


def _load_tok_tiles(ref, n_tok):
    return jnp.concatenate([ref[pl.ds(s, n_tok, stride=TOK_ROWS), :] for s in range(TOK_ROWS)], axis=1)


def _store_tok_tiles(ref, val, n_tok):
    for s in range(TOK_ROWS):
        ref[pl.ds(s, n_tok, stride=TOK_ROWS), :] = val[:, s * LANES:(s + 1) * LANES]


def _ada_kernel(c_ref, w_ref, b_ref, o_ref):
    s = _silu(c_ref[...]).astype(BF16)
    o_ref[...] = _dot(s, w_ref[...].astype(BF16)) + b_ref[...]


def _ada(c_all, w_ada, b_ada):
    n = c_all.shape[0]
    return pl.pallas_call(
        _ada_kernel,
        grid=(6,),
        in_specs=[pl.BlockSpec((n, D_MODEL), lambda j: (0, 0)),
                  pl.BlockSpec((D_MODEL, D_MODEL), lambda j: (0, j)),
                  pl.BlockSpec((1, D_MODEL), lambda j: (0, j))],
        out_specs=pl.BlockSpec((n, D_MODEL), lambda j: (0, j)),
        out_shape=jax.ShapeDtypeStruct((n, 6 * D_MODEL), F32),
        compiler_params=_cparams(("arbitrary",)),
        name="ada_mod",
    )(c_all, w_ada, b_ada.reshape(1, -1))


def _inproj_kernel(x_ref, sc_ref, sh_ref, g_ref, w_ref,
                   hg_ref, q_ref, kv4_ref, kvw_ref, gate_ref, kvsel_ref, kvwb_ref):
    h = _rms(x_ref[0], g_ref[...]) * (1.0 + sc_ref[0]) + sh_ref[0]
    z = _dot(h.astype(BF16), w_ref[...])
    c0 = 4 * HG_WIDTH
    hg_ref[0] = z[:, :c0]
    q_ref[0] = (z[:, c0:c0 + QPAD_W] * Q_SCALE).astype(BF16)
    c1 = c0 + QPAD_W
    kv4 = z[:, c1:c1 + 4 * KV_W]
    kv4_ref[0] = kv4
    kvsel_ref[0] = kv4[:, 2 * KV_W:].astype(BF16)
    c2 = c1 + 4 * KV_W
    kvw = z[:, c2:c2 + 2 * KV_W]
    kvw_ref[0] = kvw
    kvwb_ref[0] = kvw.astype(BF16)
    gate_ref[0] = jax.nn.sigmoid(z[:, c2 + 2 * KV_W:])


def _inproj(x, scale, shift, g_norm, w_pad, tm):
    nb, t, _ = x.shape
    mod_spec = _mod_spec(scale, tm)
    widths = [(4 * HG_WIDTH, F32), (QPAD_W, BF16), (4 * KV_W, F32), (2 * KV_W, F32), (GZ_PAD, F32),
              (2 * KV_W, BF16), (2 * KV_W, BF16)]
    return pl.pallas_call(
        _inproj_kernel,
        grid=(nb, t // tm),
        in_specs=[pl.BlockSpec((1, tm, D_MODEL), lambda b, i: (b, i, 0)),
                  mod_spec, mod_spec,
                  pl.BlockSpec((1, D_MODEL), lambda b, i: (0, 0)),
                  pl.BlockSpec((D_MODEL, INP_COLS), lambda b, i: (0, 0))],
        out_specs=[pl.BlockSpec((1, tm, w), lambda b, i: (b, i, 0)) for w, _ in widths],
        out_shape=[jax.ShapeDtypeStruct((nb, t, w), dt) for w, dt in widths],
        compiler_params=_cparams(("arbitrary", "arbitrary")),
        name="in_proj",
    )(x, scale, shift, g_norm.reshape(1, -1), w_pad)


def _pad_w_in(w_in):
    c0 = 4 * HG_WIDTH
    wq = w_in[:, c0:c0 + NSA_WIDTH].reshape(D_MODEL, NSA_HEADS, NSA_HEAD_DIM)
    zeros = jnp.zeros_like(wq)
    lo = jnp.concatenate([wq, zeros], axis=-1)
    hi = jnp.concatenate([zeros, wq], axis=-1)
    grp = (jnp.arange(NSA_HEADS) // NSA_GROUP)[None, :, None]
    wq_pad = jnp.where(grp == 0, lo, hi).reshape(D_MODEL, QPAD_W)
    c1 = c0 + NSA_WIDTH
    rest = w_in[:, c1:c1 + 6 * KV_W]
    gz = jnp.pad(w_in[:, c1 + 6 * KV_W:], ((0, 0), (0, GZ_PAD - 3 * NSA_HEADS)))
    return jnp.concatenate([w_in[:, :c0], wq_pad, rest, gz], axis=1).astype(BF16)


def _hgrn_kernel(q_ref, f_ref, v_ref, gt_ref, lb_ref, s0_ref, gn_ref, o_ref, s_out_ref, st_scr,
                 *, chunk, n_chunks):
    i = pl.program_id(1)

    @pl.when(i == 0)
    def _():
        for h in range(HG_HEADS):
            st_scr[h] = s0_ref[0, h].T

    lbr = lb_ref[...]
    e = jnp.exp(lbr - jnp.max(lbr, axis=0, keepdims=True))
    lb_all = e[0:1] / jnp.sum(e, axis=0, keepdims=True)
    row = lax.broadcasted_iota(jnp.int32, (chunk, HG_DK), 0)
    causal = (lax.broadcasted_iota(jnp.int32, (chunk, chunk), 0)
              >= lax.broadcasted_iota(jnp.int32, (chunk, chunk), 1))
    st = [st_scr[h] for h in range(HG_HEADS)]
    for c in range(n_chunks):
        sl = pl.ds(c * chunk, chunk)
        for h in range(HG_HEADS):
            hs = slice(h * HG_DK, (h + 1) * HG_DK)
            lb = lb_all[:, hs]
            z = f_ref[0, sl, hs]
            log_f = jnp.log(lb + (1.0 - lb) * jax.nn.sigmoid(z))
            kk = (1.0 - lb) * jax.nn.sigmoid(-z)
            a = log_f
            s = 1
            while s < chunk:
                a = a + jnp.where(row >= s, pltpu.roll(a, s, 0), 0.0)
                s *= 2
            qt = (q_ref[0, sl, hs] * jnp.exp(a)).astype(BF16)
            kt = (kk * jnp.exp(-a)).astype(BF16)
            v = v_ref[0, sl, hs].astype(BF16)
            att = jnp.where(causal, _dot_nt(qt, kt), 0.0)
            o = _dot(att.astype(BF16), v) + _dot_nt(qt, st[h].astype(BF16))
            a_end = a[chunk - 1:chunk, :]
            kd = (kk * jnp.exp(a_end - a)).astype(BF16)
            st[h] = st[h] * jnp.exp(a_end) + _dot_tn(v, kd)
            o = _rms(o, gn_ref[...]) * _silu(gt_ref[0, sl, hs])
            o_ref[0, sl, hs] = o.astype(o_ref.dtype)
    for h in range(HG_HEADS):
        st_scr[h] = st[h]

    @pl.when(i == pl.num_programs(1) - 1)
    def _():
        for h in range(HG_HEADS):
            s_out_ref[0, h] = st[h].T


def _hgrn(hg, hg_lb, s0, g_norm, tc, chunk):
    nb, t, _ = hg.shape
    part = lambda k: pl.BlockSpec((1, tc, HG_WIDTH), lambda b, i: (b, i, k))
    st_spec = pl.BlockSpec((1, HG_HEADS, HG_DK, HG_DK), lambda b, i: (b, 0, 0, 0))
    return pl.pallas_call(
        functools.partial(_hgrn_kernel, chunk=chunk, n_chunks=tc // chunk),
        grid=(nb, t // tc),
        in_specs=[part(0), part(1), part(2), part(3),
                  pl.BlockSpec(hg_lb.shape, lambda b, i: (0, 0)),
                  st_spec,
                  pl.BlockSpec((1, HG_DK), lambda b, i: (0, 0))],
        out_specs=[pl.BlockSpec((1, tc, HG_WIDTH), lambda b, i: (b, i, 0)), st_spec],
        out_shape=[jax.ShapeDtypeStruct((nb, t, HG_WIDTH), BF16),
                   jax.ShapeDtypeStruct((nb, HG_HEADS, HG_DK, HG_DK), F32)],
        scratch_shapes=[pltpu.VMEM((HG_HEADS, HG_DK, HG_DK), F32)],
        compiler_params=_cparams(("arbitrary", "arbitrary")),
        name="hgrn2",
    )(hg, hg, hg, hg, hg_lb, s0, g_norm.reshape(1, -1))


def _gelu_tanh(x):
    return 0.5 * x * (1.0 + jnp.tanh(0.7978845608028654 * (x + 0.044715 * x * x * x)))


def _page_copies(pt_ref, cache_ref, buf, sem, b, s, slot, n_pages, pps, col0, tail, transposed):
    copies = []
    base = s * pps
    nxt = pt_ref[b, jnp.minimum(base + pps, n_pages - 1)]
    for br in range(2):
        cols = pl.ds(col0 + br * KV_W, KV_W)
        for i in range(pps):
            pg = pt_ref[b, base + i]
            if transposed:
                copies.append(pltpu.make_async_copy(
                    cache_ref.at[pg, cols, :],
                    buf.at[slot, br, :, pl.ds(i * PAGE_SIZE, PAGE_SIZE)], sem.at[slot]))
            else:
                copies.append(pltpu.make_async_copy(
                    cache_ref.at[pg, :, cols],
                    buf.at[slot, br, pl.ds(i * PAGE_SIZE, PAGE_SIZE), :], sem.at[slot]))
        if tail and transposed:
            copies.append(pltpu.make_async_copy(
                cache_ref.at[nxt, cols, :],
                buf.at[slot, br, :, pl.ds(pps * PAGE_SIZE, PAGE_SIZE)], sem.at[slot]))
        elif tail:
            copies.append(pltpu.make_async_copy(
                cache_ref.at[nxt, pl.ds(0, CMP_STRIDE), cols],
                buf.at[slot, br, pl.ds(pps * PAGE_SIZE, CMP_STRIDE), :], sem.at[slot]))
    return copies


def _stream_pages(pt_ref, cache_ref, buf, sem, n_pages, n_steps, pps, col0, tail, transposed):
    b = pl.program_id(0)
    s = pl.program_id(1)
    n = b * n_steps + s
    total = pl.num_programs(0) * n_steps
    slot = n % 2
    args = (n_pages, pps, col0, tail, transposed)

    @pl.when(n == 0)
    def _():
        for cp in _page_copies(pt_ref, cache_ref, buf, sem, b, s, slot, *args):
            cp.start()

    @pl.when(n + 1 < total)
    def _():
        n1 = n + 1
        for cp in _page_copies(pt_ref, cache_ref, buf, sem, n1 // n_steps, n1 % n_steps, 1 - slot, *args):
            cp.start()

    for cp in _page_copies(pt_ref, cache_ref, buf, sem, b, s, slot, *args):
        cp.wait()
    return slot


def _compress_kernel(pt_ref, cache_ref, pe_ref, w1_ref, b1_ref, w2_ref, kc_ref, vc_ref, buf, sem, *rowbuf,
                     n_pages, n_steps, pps, transposed):
    groups = pps * PAGE_SIZE // CMP_STRIDE

    def compress(read_rows):
        low_half = lax.broadcasted_iota(jnp.int32, (groups, KV_W), 1) < NSA_HEAD_DIM
        for br, out_ref in ((0, kc_ref), (1, vc_ref)):
            acc0 = jnp.zeros((groups, CMP_HIDDEN), F32)
            acc1 = jnp.zeros((groups, CMP_HIDDEN), F32)
            for j in range(CMP_BLOCK // 2):
                a = read_rows(br, 2 * j) + pe_ref[br, 2 * j:2 * j + 1, :]
                b = pltpu.roll(read_rows(br, 2 * j + 1) + pe_ref[br, 2 * j + 1:2 * j + 2, :], NSA_HEAD_DIM, 1)
                acc0 = acc0 + _dot(jnp.where(low_half, a, b).astype(BF16), w1_ref[br, j, 0])
                acc1 = acc1 + _dot(jnp.where(low_half, b, a).astype(BF16), w1_ref[br, j, 1])
            hid = _gelu_tanh(jnp.concatenate([acc0, acc1], axis=1) + b1_ref[br]).astype(BF16)
            out_ref[0] = _dot(hid, w2_ref[br]).astype(out_ref.dtype)

    if transposed:
        rows_ref, = rowbuf
        n = pl.program_id(0) * n_steps + pl.program_id(1)
        total = pt_ref.shape[0] * n_steps
        copies = lambda m: _page_copies(pt_ref, cache_ref, buf, sem, m // n_steps, m % n_steps, m % 2,
                                        n_pages, pps, 0, True, True)

        def to_rows(stage, half):
            for br in range(2):
                for i in range(pps + 1):
                    n_rows = PAGE_SIZE if i < pps else CMP_STRIDE
                    page_t = buf[stage, br, :, i * PAGE_SIZE:(i + 1) * PAGE_SIZE]
                    rows_ref[half, br, i * PAGE_SIZE:i * PAGE_SIZE + n_rows, :] = page_t.T[:n_rows]

        @pl.when(n == 0)
        def _():
            for cp in copies(0):
                cp.start()
            for cp in copies(0):
                cp.wait()
            to_rows(0, 0)
            if total > 1:
                for cp in copies(1):
                    cp.start()

        @pl.when(n + 1 < total)
        def _():
            for cp in copies(n + 1):
                cp.wait()

        @pl.when(n + 2 < total)
        def _():
            for cp in copies(n + 2):
                cp.start()

        for parity in range(2):
            @pl.when(n % 2 == parity)
            def _(parity=parity):
                if total > 1:
                    to_rows(1 - parity, 1 - parity)
                compress(lambda br, l: rows_ref[parity, br, pl.ds(l, groups, stride=CMP_STRIDE), :])
    else:
        slot = _stream_pages(pt_ref, cache_ref, buf, sem, n_pages, n_steps, pps, 0, True, False)
        compress(lambda br, l: buf[slot, br, pl.ds(l, groups, stride=CMP_STRIDE), :])


def _compress(page_table, cache, pe2, w1cat, b1cat, w2bd, pps=PAGES_PER_STEP, transposed=False):
    nb, n_pages = page_table.shape
    n_steps = n_pages // pps
    groups = pps * PAGE_SIZE // CMP_STRIDE
    rows = pps * PAGE_SIZE + CMP_STRIDE
    const = lambda shape: pl.BlockSpec(shape, lambda b, s, pt: (0,) * len(shape))
    out_spec = pl.BlockSpec((1, groups, KV_W), lambda b, s, pt: (b, s, 0))
    out_sds = jax.ShapeDtypeStruct((nb, n_steps * groups, KV_W), BF16)
    if transposed:
        stage = [pltpu.VMEM((2, 2, KV_W, (pps + 1) * PAGE_SIZE), F32), pltpu.SemaphoreType.DMA((2,)),
                 pltpu.VMEM((2, 2, rows, KV_W), F32)]
    else:
        stage = [pltpu.VMEM((2, 2, rows, KV_W), F32), pltpu.SemaphoreType.DMA((2,))]
    return pl.pallas_call(
        functools.partial(_compress_kernel, n_pages=n_pages, n_steps=n_steps, pps=pps, transposed=transposed),
        grid_spec=pltpu.PrefetchScalarGridSpec(
            num_scalar_prefetch=1,
            grid=(nb, n_steps),
            in_specs=[pl.BlockSpec(memory_space=pl.ANY),
                      const((2, CMP_BLOCK, KV_W)),
                      const((2, CMP_BLOCK // 2, 2, KV_W, CMP_HIDDEN)),
                      const((2, 1, 2 * CMP_HIDDEN)),
                      const((2, 2 * CMP_HIDDEN, KV_W))],
            out_specs=[out_spec, out_spec],
            scratch_shapes=stage),
        out_shape=[out_sds, out_sds],
        compiler_params=_cparams(("arbitrary", "arbitrary")),
        name="nsa_compress",
    )(page_table, cache, pe2, w1cat, b1cat, w2bd)


def _compress_weights(cmp_pe, cmp_w1, cmp_b1, cmp_w2):
    pe2 = jnp.concatenate([cmp_pe, cmp_pe], axis=-1)
    w1 = cmp_w1.reshape(2, CMP_BLOCK // 2, 2, NSA_HEAD_DIM, CMP_HIDDEN)
    even_odd = w1.reshape(2, CMP_BLOCK // 2, 2 * NSA_HEAD_DIM, CMP_HIDDEN)
    odd_even = w1[:, :, ::-1].reshape(2, CMP_BLOCK // 2, 2 * NSA_HEAD_DIM, CMP_HIDDEN)
    w1cat = jnp.stack([even_odd, odd_even], axis=2).astype(BF16)
    b1cat = jnp.concatenate([cmp_b1, cmp_b1], axis=-1)[:, None, :]
    z2 = jnp.zeros_like(cmp_w2)
    w2bd = jnp.concatenate([jnp.concatenate([cmp_w2, z2], axis=-1),
                            jnp.concatenate([z2, cmp_w2], axis=-1)], axis=1).astype(BF16)
    return pe2, w1cat, b1cat, w2bd


def _overlap_matrix(n_cmp, n_sel):
    cs = lax.broadcasted_iota(jnp.int32, (n_cmp, n_sel), 0) * CMP_STRIDE
    ss = lax.broadcasted_iota(jnp.int32, (n_cmp, n_sel), 1) * SEL_BLOCK
    return ((cs < ss + SEL_BLOCK) & (cs + CMP_BLOCK > ss)).astype(BF16)


SEL_CHUNK = 512
SPREAD_KEYS = 1024
WIN_SPAN = WINDOW + Q_BLOCK


def _nsa_prompt_kernel(q_ref, gt_ref, kc_ref, vc_ref, ks_ref, vs_ref, kw_ref, vw_ref, ex_ref, o_ref,
                       m_scr, acc_scr, chosen_scr, oc_scr, sa_scr, sb_scr, *, n_cmp, n_sel):
    j = pl.program_id(1)
    q0 = j * Q_BLOCK
    tok = lax.broadcasted_iota(jnp.int32, (Q_BLOCK, 1), 0) + q0
    tok4 = jnp.concatenate([tok] * NSA_GROUP, axis=0)
    n_chunks = j // (SEL_CHUNK // Q_BLOCK) + 1
    key_in_span = lax.broadcasted_iota(jnp.int32, (1, SPREAD_KEYS), 1)

    def load_q(g):
        return jnp.concatenate([q_ref[0, :, (NSA_GROUP * g + jh) * LANES:(NSA_GROUP * g + jh + 1) * LANES]
                                for jh in range(NSA_GROUP)], axis=0)

    def softmax_av(s, valid, v):
        s = jnp.where(valid, s, NEG)
        tiles = [s[:, i * LANES:(i + 1) * LANES] for i in range(s.shape[1] // LANES)]
        m = jnp.max(functools.reduce(jnp.maximum, tiles), axis=1, keepdims=True)
        p = jnp.where(valid, jnp.exp2(s - m), 0.0)
        acc = _dot(p.astype(BF16), jnp.concatenate([v, jnp.ones(v.shape, BF16)], axis=1))
        inv = 1.0 / jnp.maximum(acc[:, KV_W:], 1e-30)
        return p, acc[:, :KV_W] * inv, inv

    ov = _overlap_matrix(n_cmp, n_sel)
    cend = lax.broadcasted_iota(jnp.int32, (1, n_cmp), 1) * CMP_STRIDE + (CMP_BLOCK - 1)
    ws = pl.multiple_of(jnp.maximum(q0 - WINDOW, 0), Q_BLOCK)
    wpos = ws + lax.broadcasted_iota(jnp.int32, (1, WIN_SPAN), 1)
    d = tok4 - wpos
    in_window = (d >= 0) & (d < WINDOW)
    gates = gt_ref[0]
    imps = []
    for g in range(NSA_KV_HEADS):
        q = load_q(g)
        p, o_c, inv = softmax_av(_dot_nt(q, kc_ref[0]), cend <= tok4, vc_ref[0])
        _, o_w, _ = softmax_av(_dot_nt(q, kw_ref[0, pl.ds(ws, WIN_SPAN), :]), in_window,
                               vw_ref[0, pl.ds(ws, WIN_SPAN), :])
        psum = jnp.zeros((Q_BLOCK, n_cmp), F32)
        for jh in range(NSA_GROUP):
            h = NSA_GROUP * g + jh
            r = slice(jh * Q_BLOCK, (jh + 1) * Q_BLOCK)
            psum = psum + p[r] * jnp.concatenate([inv[r]] * (n_cmp // LANES), axis=1)
            oc_scr[g, r, :] = gates[:, 3 * h:3 * h + 1] * o_c[r] + gates[:, 3 * h + 2:3 * h + 3] * o_w[r]
        imps.append(_dot(psum.astype(BF16), ov))

    blk = lax.broadcasted_iota(jnp.int32, (1, n_sel), 1)
    cur = lax.shift_right_logical(tok, SEL_SHIFT)
    forced = (blk == 0) | (blk == cur) | (blk == cur - 1)
    free = (blk * SEL_BLOCK <= tok) & jnp.logical_not(forced)
    forced2 = jnp.concatenate([forced] * NSA_KV_HEADS, axis=0)
    free2 = jnp.concatenate([free] * NSA_KV_HEADS, axis=0)
    best = _topk_mask(jnp.where(free2, jnp.concatenate(imps, axis=0), NEG), SEL_TOPN - 3)
    sel2 = jnp.where(forced2, 1.0, best).astype(BF16)

    ones_blk = jnp.ones((SEL_CHUNK, KV_W), BF16)

    for g in range(NSA_KV_HEADS):
        sel = sel2[g * Q_BLOCK:(g + 1) * Q_BLOCK]
        m_scr[...] = jnp.full(m_scr.shape, NEG, F32)
        acc_scr[...] = jnp.zeros(acc_scr.shape, F32)

        def spread(i, carry):
            c0 = pl.multiple_of(i * SPREAD_KEYS, SPREAD_KEYS)
            keep = (_dot(sel, ex_ref[:, pl.ds(c0, SPREAD_KEYS)]) > 0.5) & (key_in_span + c0 <= tok)
            chosen_scr[:, pl.ds(c0, SPREAD_KEYS)] = jnp.where(keep, 0.0, NEG)
            return carry

        lax.fori_loop(0, (n_chunks * SEL_CHUNK + SPREAD_KEYS - 1) // SPREAD_KEYS, spread, 0)

        heads = range(NSA_GROUP)
        rows = [pl.ds(jh * Q_BLOCK, Q_BLOCK) for jh in heads]

        def stage_scores(c, buf):
            k0 = pl.multiple_of(c * SEL_CHUNK, SEL_CHUNK)
            bias = chosen_scr[:, pl.ds(k0, SEL_CHUNK)]
            kblk = ks_ref[0, pl.ds(k0, SEL_CHUNK), :]
            for jh in heads:
                h = NSA_GROUP * g + jh
                buf[rows[jh], :] = _dot_nt(q_ref[0, :, h * LANES:(h + 1) * LANES], kblk) + bias

        def accumulate(c, buf):
            k0 = pl.multiple_of(c * SEL_CHUNK, SEL_CHUNK)
            vext = jnp.concatenate([vs_ref[0, pl.ds(k0, SEL_CHUNK), :], ones_blk], axis=1)
            tiles = [[buf[r, i * LANES:(i + 1) * LANES] for i in range(SEL_CHUNK // LANES)] for r in rows]
            m_old = [m_scr[r, :] for r in rows]
            m_new = [jnp.maximum(mo, jnp.max(functools.reduce(jnp.maximum, t), axis=1, keepdims=True))
                     for mo, t in zip(m_old, tiles)]
            probs = [jnp.concatenate([jnp.exp2(x - mn) for x in t], axis=1).astype(BF16)
                     for mn, t in zip(m_new, tiles)]
            pv = [_dot(p, vext) for p in probs]
            for r, mo, mn, y in zip(rows, m_old, m_new, pv):
                alpha = jnp.exp2(mo - mn)
                acc_scr[r, :] = jnp.concatenate([alpha, alpha], axis=1) * acc_scr[r, :] + y
                m_scr[r, :] = mn

        stage_scores(0, sa_scr)

        def body(i, carry):
            stage_scores(2 * i + 1, sb_scr)
            accumulate(2 * i, sa_scr)
            stage_scores(jnp.minimum(2 * i + 2, n_chunks - 1), sa_scr)
            accumulate(2 * i + 1, sb_scr)
            return carry

        lax.fori_loop(0, n_chunks // 2, body, 0)

        @pl.when(n_chunks % 2 == 1)
        def _():
            accumulate(n_chunks - 1, sa_scr)
        o_s = acc_scr[:, :KV_W] / jnp.maximum(acc_scr[:, KV_W:], 1e-30)

        for jh in range(NSA_GROUP):
            h = NSA_GROUP * g + jh
            r = slice(jh * Q_BLOCK, (jh + 1) * Q_BLOCK)
            o = oc_scr[g, r, :] + gates[:, 3 * h + 1:3 * h + 2] * o_s[r]
            o_ref[0, :, h * LANES:(h + 1) * LANES] = o.astype(o_ref.dtype)


def _nsa_prompt(qpad, gates, kc, vc, kvsel, kvwb):
    nb, t, _ = qpad.shape
    assert t % SPREAD_KEYS == 0 and SPREAD_KEYS == 2 * SEL_CHUNK and t >= WIN_SPAN
    n_cmp = kc.shape[1]
    n_sel = t // SEL_BLOCK
    full = lambda w, k: pl.BlockSpec((1, t, w), lambda b, j: (b, 0, k))
    t_pad = -(-t // SPREAD_KEYS) * SPREAD_KEYS
    expand = jnp.asarray(np.arange(n_sel)[:, None] == (np.arange(t_pad)[None, :] // SEL_BLOCK), BF16)
    return pl.pallas_call(
        functools.partial(_nsa_prompt_kernel, n_cmp=n_cmp, n_sel=n_sel),
        grid=(nb, t // Q_BLOCK),
        in_specs=[pl.BlockSpec((1, Q_BLOCK, QPAD_W), lambda b, j: (b, j, 0)),
                  pl.BlockSpec((1, Q_BLOCK, GZ_PAD), lambda b, j: (b, j, 0)),
                  pl.BlockSpec((1, n_cmp, KV_W), lambda b, j: (b, 0, 0)),
                  pl.BlockSpec((1, n_cmp, KV_W), lambda b, j: (b, 0, 0)),
                  full(KV_W, 0), full(KV_W, 1), full(KV_W, 0), full(KV_W, 1),
                  pl.BlockSpec((n_sel, t_pad), lambda b, j: (0, 0))],
        out_specs=pl.BlockSpec((1, Q_BLOCK, QPAD_W), lambda b, j: (b, j, 0)),
        out_shape=jax.ShapeDtypeStruct((nb, t, QPAD_W), BF16),
        scratch_shapes=[pltpu.VMEM((NSA_GROUP * Q_BLOCK, LANES), F32),
                        pltpu.VMEM((NSA_GROUP * Q_BLOCK, 2 * KV_W), F32),
                        pltpu.VMEM((Q_BLOCK, t_pad), F32),
                        pltpu.VMEM((NSA_KV_HEADS, NSA_GROUP * Q_BLOCK, KV_W), F32),
                        pltpu.VMEM((NSA_GROUP * Q_BLOCK, SEL_CHUNK), F32),
                        pltpu.VMEM((NSA_GROUP * Q_BLOCK, SEL_CHUNK), F32)],
        compiler_params=_cparams(("arbitrary", "arbitrary")),
        name="nsa_prompt",
    )(qpad, gates, kc, vc, kvsel, kvsel, kvwb, kvwb, expand)


def _nsa_sample_a_kernel(q_ref, g_ref, kc_ref, vc_ref, wb_ref, nw_ref, ocw_ref, sel_ref,
                         *, past_len, n_tok, n_sel, n_sel_pad):
    q = q_ref[0]
    rows = q.shape[0]
    n_cmp = kc_ref.shape[1]
    t_row = lax.broadcasted_iota(jnp.int32, (rows, 1), 0) & (n_tok - 1)
    qpos = past_len + t_row
    cend = lax.broadcasted_iota(jnp.int32, (1, n_cmp), 1) * CMP_STRIDE + (CMP_BLOCK - 1)
    p_c = _masked_softmax(_dot_nt(q, kc_ref[0]), cend <= qpos)
    o_c = _dot(p_c.astype(BF16), vc_ref[0])

    per_grp = NSA_GROUP * n_tok
    psum = jnp.concatenate(
        [sum(p_c[g * per_grp + jh * n_tok:g * per_grp + (jh + 1) * n_tok] for jh in range(NSA_GROUP))
         for g in range(NSA_KV_HEADS)], axis=0)
    imp = _dot(psum.astype(BF16), _overlap_matrix(n_cmp, n_sel_pad))
    blk = lax.broadcasted_iota(jnp.int32, (1, n_sel_pad), 1)
    tq = past_len + (lax.broadcasted_iota(jnp.int32, (NSA_KV_HEADS * n_tok, 1), 0) & (n_tok - 1))
    cur = lax.shift_right_logical(tq, SEL_SHIFT)
    forced = (blk == 0) | (blk == cur) | (blk == cur - 1)
    allowed = blk * SEL_BLOCK <= tq
    v = jnp.where(forced, -NEG, jnp.where(allowed, imp, NEG))
    sel_ref[0] = _topk_mask(jnp.where(blk < n_sel, v, 2.0 * NEG), SEL_TOPN)

    wb = wb_ref.shape[1]
    kw = wb_ref[0, :, 0:KV_W].astype(BF16)
    vw = wb_ref[0, :, KV_W:2 * KV_W].astype(BF16)
    kn = nw_ref[0, :, 0:KV_W].astype(BF16)
    vn = nw_ref[0, :, KV_W:2 * KV_W].astype(BF16)
    i1 = lax.broadcasted_iota(jnp.int32, (1, wb), 1)
    d1 = t_row + wb - i1
    valid1 = (d1 >= 0) & (d1 < WINDOW) & (past_len - wb + i1 >= 0)
    i2 = lax.broadcasted_iota(jnp.int32, (1, nw_ref.shape[1]), 1)
    d2 = t_row - i2
    valid2 = (d2 >= 0) & (d2 < WINDOW) & (i2 < n_tok)
    s1 = jnp.where(valid1, _dot_nt(q, kw), NEG)
    s2 = jnp.where(valid2, _dot_nt(q, kn), NEG)
    m = jnp.maximum(jnp.max(s1, axis=1, keepdims=True), jnp.max(s2, axis=1, keepdims=True))
    p1 = jnp.exp2(s1 - m) * valid1.astype(F32)
    p2 = jnp.exp2(s2 - m) * valid2.astype(F32)
    den = jnp.maximum(jnp.sum(p1, axis=1, keepdims=True) + jnp.sum(p2, axis=1, keepdims=True), 1e-30)
    o_w = (_dot(p1.astype(BF16), vw) + _dot(p2.astype(BF16), vn)) / den
    g = g_ref[0]
    ocw_ref[0] = g[:, 0:1] * o_c + g[:, 2:3] * o_w


def _nsa_sample_a(q_rows, g_rows, kc, vc, win_buf, new_win, *, past_len, n_tok):
    nb, rows, _ = q_rows.shape
    n_sel = -(-(past_len + n_tok) // SEL_BLOCK)
    n_sel_pad = -(-n_sel // LANES) * LANES
    blk3 = lambda a: pl.BlockSpec((1,) + a.shape[1:], lambda b: (b, 0, 0))
    return pl.pallas_call(
        functools.partial(_nsa_sample_a_kernel, past_len=past_len, n_tok=n_tok, n_sel=n_sel, n_sel_pad=n_sel_pad),
        grid=(nb,),
        in_specs=[blk3(q_rows), blk3(g_rows), blk3(kc), blk3(vc), blk3(win_buf), blk3(new_win)],
        out_specs=[pl.BlockSpec((1, rows, KV_W), lambda b: (b, 0, 0)),
                   pl.BlockSpec((1, NSA_KV_HEADS * n_tok, n_sel_pad), lambda b: (b, 0, 0))],
        out_shape=[jax.ShapeDtypeStruct((nb, rows, KV_W), F32),
                   jax.ShapeDtypeStruct((nb, NSA_KV_HEADS * n_tok, n_sel_pad), F32)],
        compiler_params=_cparams(("arbitrary",)),
        name="nsa_sample_a",
    )(q_rows, g_rows, kc, vc, win_buf, new_win)


def _nsa_sample_b_kernel(pt_ref, cache_ref, q_ref, g_ref, sel_ref, seln_ref, ns_ref, ocw_ref, ex_ref, o_ref,
                         buf, sem, m_scr, l_scr, acc_scr, *, n_pages, n_steps, pps, n_tok):
    s = pl.program_id(1)
    slot = _stream_pages(pt_ref, cache_ref, buf, sem, n_pages, n_steps, pps, 2 * KV_W, False, True)
    q = q_ref[0]
    rows = q.shape[0]

    @pl.when(s == 0)
    def _():
        m_scr[...] = jnp.full(m_scr.shape, NEG, F32)
        l_scr[...] = jnp.zeros(l_scr.shape, F32)
        acc_scr[...] = jnp.zeros(acc_scr.shape, F32)

    def update(scores, msk, times_v):
        sc = jnp.where(msk, scores, NEG)
        m_old = m_scr[...]
        m_new = jnp.maximum(m_old, jnp.max(sc, axis=1, keepdims=True))
        p = jnp.exp2(sc - m_new) * msk.astype(F32)
        alpha = jnp.exp2(m_old - m_new)
        l_scr[...] = alpha * l_scr[...] + jnp.sum(p, axis=1, keepdims=True)
        acc_scr[...] = alpha * acc_scr[...] + times_v(p.astype(BF16))
        m_scr[...] = m_new

    chosen = _dot(sel_ref[0, 0], ex_ref[...]) > 0.5
    update(_dot(q, buf[slot, 0].astype(BF16)), chosen, lambda p: _dot_nt(p, buf[slot, 1].astype(BF16)))

    @pl.when(s == n_steps - 1)
    def _():
        t_row = lax.broadcasted_iota(jnp.int32, (rows, 1), 0) & (n_tok - 1)
        i2 = lax.broadcasted_iota(jnp.int32, (1, ns_ref.shape[1]), 1)
        msk = (seln_ref[0, 0][:, 0:1] > 0.5) & (i2 <= t_row) & (i2 < n_tok)
        update(_dot_nt(q, ns_ref[0, :, 0:KV_W].astype(BF16)), msk,
               lambda p: _dot(p, ns_ref[0, :, KV_W:2 * KV_W].astype(BF16)))
        o_s = acc_scr[...] / jnp.maximum(l_scr[...], 1e-30)
        o_ref[0] = ocw_ref[0] + g_ref[0][:, 1:2] * o_s


def _nsa_sample_b(page_table, cache, q_rows, g_rows, sel_steps, new_sel, ocw, *, n_tok, pps=PAGES_PER_STEP):
    nb, n_pages = page_table.shape
    n_steps = n_pages // pps
    rows = q_rows.shape[1]
    keys = pps * PAGE_SIZE
    expand = jnp.asarray(np.arange(LANES)[:, None] == (np.arange(keys)[None, :] // SEL_BLOCK), BF16)
    per_b = lambda a: pl.BlockSpec((1,) + a.shape[1:], lambda b, s, pt: (b, 0, 0))
    return pl.pallas_call(
        functools.partial(_nsa_sample_b_kernel, n_pages=n_pages, n_steps=n_steps, pps=pps, n_tok=n_tok),
        grid_spec=pltpu.PrefetchScalarGridSpec(
            num_scalar_prefetch=1,
            grid=(nb, n_steps),
            in_specs=[pl.BlockSpec(memory_space=pl.ANY),
                      per_b(q_rows), per_b(g_rows),
                      pl.BlockSpec((1, 1, rows, LANES), lambda b, s, pt: (b, s, 0, 0)),
                      pl.BlockSpec((1, 1, rows, LANES), lambda b, s, pt: (b, n_steps, 0, 0)),
                      per_b(new_sel), per_b(ocw),
                      pl.BlockSpec((LANES, keys), lambda b, s, pt: (0, 0))],
            out_specs=pl.BlockSpec((1, rows, KV_W), lambda b, s, pt: (b, 0, 0)),
            scratch_shapes=[pltpu.VMEM((2, 2, KV_W, keys), F32), pltpu.SemaphoreType.DMA((2,)),
                            pltpu.VMEM((rows, 1), F32), pltpu.VMEM((rows, 1), F32),
                            pltpu.VMEM((rows, KV_W), F32)]),
        out_shape=jax.ShapeDtypeStruct((nb, rows, KV_W), F32),
        compiler_params=_cparams(("arbitrary", "arbitrary")),
        name="nsa_sample_b",
    )(page_table, cache, q_rows, g_rows, sel_steps, sel_steps, new_sel, ocw, expand)


def _nsa_sample(page_table, cache, win_buf, cw, qpad, gates, kv4, kvw, pps=PAGES_PER_STEP):
    nb, ts, _ = qpad.shape
    past_len = page_table.shape[1] * PAGE_SIZE
    kc, vc = _compress(page_table, cache, *cw, pps=pps, transposed=True)
    rows = NSA_HEADS * ts
    q_rows = qpad.reshape(nb, ts, NSA_HEADS, LANES).transpose(0, 2, 1, 3).reshape(nb, rows, LANES)
    g_rows = gates[:, :, :3 * NSA_HEADS].reshape(nb, ts, NSA_HEADS, 3).transpose(0, 2, 1, 3)
    g_rows = jnp.pad(g_rows.reshape(nb, rows, 3), ((0, 0), (0, 0), (0, LANES - 3)))
    pad_rows = lambda a: jnp.pad(a, ((0, 0), (0, LANES - ts), (0, 0)))
    new_win = pad_rows(kvw)
    new_sel = pad_rows(kv4[:, :, 2 * KV_W:])
    ocw, sel = _nsa_sample_a(q_rows, g_rows, kc, vc, win_buf, new_win, past_len=past_len, n_tok=ts)
    n_steps = page_table.shape[1] // pps
    blk_per_step = pps * PAGE_SIZE // SEL_BLOCK
    n_past_blk = n_steps * blk_per_step
    sel_past = sel[:, :, :n_past_blk].reshape(nb, NSA_KV_HEADS, 1, ts, n_steps, blk_per_step)
    sel_past = jnp.broadcast_to(sel_past, (nb, NSA_KV_HEADS, NSA_GROUP, ts, n_steps, blk_per_step))
    sel_past = sel_past.transpose(0, 4, 1, 2, 3, 5).reshape(nb, n_steps, rows, blk_per_step)
    sel_past = jnp.pad(sel_past, ((0, 0), (0, 0), (0, 0), (0, LANES - blk_per_step)))
    sel_new = jnp.pad(sel[:, :, n_past_blk:], ((0, 0), (0, 0), (0, LANES)))[:, :, :LANES]
    sel_new = sel_new.reshape(nb, NSA_KV_HEADS, 1, ts, LANES)
    sel_new = jnp.broadcast_to(sel_new, (nb, NSA_KV_HEADS, NSA_GROUP, ts, LANES)).reshape(nb, 1, rows, LANES)
    sel_steps = jnp.concatenate([sel_past, sel_new], axis=1).astype(BF16)
    o_rows = _nsa_sample_b(page_table, cache, q_rows, g_rows, sel_steps, new_sel, ocw, n_tok=ts, pps=pps)
    return o_rows.reshape(nb, NSA_HEADS, ts, LANES).transpose(0, 2, 1, 3).reshape(nb, ts, QPAD_W).astype(BF16)


def _outproj_kernel(x_ref, hg_ref, nsa_ref, g1_ref, sc2_ref, sh2_ref, fn_ref, wo1_ref, wo2_ref,
                    x1_ref, h2_ref, *, tm):
    mix =_dot(hg_ref[0], wo1_ref[...]) + _dot(nsa_ref[0], wo2_ref[...])
    x1 = x_ref[0] + g1_ref[0] * mix
    x1_ref[0] = x1
    h2 = _rms(x1, fn_ref[...]) * (1.0 + sc2_ref[0]) + sh2_ref[0]
    _store_tok_tiles(h2_ref, h2, tm)


def _outproj(x, hg_out, nsa, gate1, scale2, shift2, ffn_norm, wo_hg, wo_nsa, tm):
    nb, t, _ = x.shape
    nt = t // tm
    mod_spec = _mod_spec(gate1, tm)
    tile = lambda w: pl.BlockSpec((1, tm, w), lambda b, i: (b, i, 0))
    return pl.pallas_call(
        functools.partial(_outproj_kernel, tm=tm),
        grid=(nb, nt),
        in_specs=[tile(D_MODEL), tile(HG_WIDTH), tile(QPAD_W), mod_spec, mod_spec, mod_spec,
                  pl.BlockSpec((1, D_MODEL), lambda b, i: (0, 0)),
                  pl.BlockSpec((HG_WIDTH, D_MODEL), lambda b, i: (0, 0)),
                  pl.BlockSpec((QPAD_W, D_MODEL), lambda b, i: (0, 0))],
        out_specs=[tile(D_MODEL), pl.BlockSpec((tm * TOK_ROWS, LANES), lambda b, i: (b * nt + i, 0))],
        out_shape=[jax.ShapeDtypeStruct((nb, t, D_MODEL), F32),
                   jax.ShapeDtypeStruct((nb * t * TOK_ROWS, LANES), F32)],
        compiler_params=_cparams(("arbitrary", "arbitrary")),
        name="out_proj",
    )(x, hg_out, nsa, gate1, scale2, shift2, ffn_norm.reshape(1, -1), wo_hg, wo_nsa)


def _split_w_out(w_out):
    wo_hg = w_out[:HG_WIDTH].astype(BF16)
    wn = w_out[HG_WIDTH:].reshape(NSA_HEADS, NSA_HEAD_DIM, D_MODEL)
    z = jnp.zeros_like(wn)
    grp = (jnp.arange(NSA_HEADS) // NSA_GROUP)[:, None, None]
    wn_pad = jnp.where(grp == 0, jnp.concatenate([wn, z], axis=1), jnp.concatenate([z, wn], axis=1))
    return wo_hg, wn_pad.reshape(QPAD_W, D_MODEL).astype(BF16)


def _router_kernel(h_ref, wr_ref, b_ref, e_ref, w_ref, r_ref, cnt_ref, run_scr, *, tm):
    @pl.when(pl.program_id(0) == 0)
    def _():
        run_scr[...] = jnp.zeros(run_scr.shape, F32)

    x = _load_tok_tiles(h_ref, tm).astype(BF16)
    scores = jax.nn.sigmoid(_dot(x, wr_ref[...]))
    biased = scores + b_ref[...]
    lane_i = lax.broadcasted_iota(jnp.int32, (tm, N_EXPERTS), 1)
    lane = lane_i.astype(F32)
    grp_of_lane = lax.shift_right_logical(lane_i, GROUP_SHIFT)

    gcol = lax.broadcasted_iota(jnp.int32, (tm, LANES), 1)
    gs = jnp.full((tm, LANES), 2.0 * NEG, F32)
    for g in range(N_GROUPS):
        mg = jnp.where(grp_of_lane == g, biased, NEG)
        m1 = jnp.max(mg, axis=1, keepdims=True)
        i1 = jnp.min(jnp.where(mg == m1, lane, 1e9), axis=1, keepdims=True)
        m2 = jnp.max(jnp.where(lane == i1, NEG, mg), axis=1, keepdims=True)
        gs = jnp.where(gcol == g, m1 + m2, gs)
    gsel = _topk_mask(gs, TOPK_GROUPS).astype(BF16)
    spread = (lax.broadcasted_iota(jnp.int32, (LANES, N_EXPERTS), 0)
              == lax.shift_right_logical(lax.broadcasted_iota(jnp.int32, (LANES, N_EXPERTS), 1), GROUP_SHIFT)
              ).astype(BF16)
    v = jnp.where(_dot(gsel, spread) > 0.5, biased, NEG)

    onehot = jnp.zeros((tm, N_EXPERTS), F32)
    idxs, wts = [], []
    wsum = jnp.zeros((tm, 1), F32)
    for _ in range(TOP_K):
        m = jnp.max(v, axis=1, keepdims=True)
        idx = jnp.min(jnp.where(v == m, lane, 1e9), axis=1, keepdims=True)
        pick = lane == idx
        wk = jnp.sum(jnp.where(pick, scores, 0.0), axis=1, keepdims=True)
        onehot = jnp.where(pick, 1.0, onehot)
        v = jnp.where(pick, 3.0 * NEG, v)
        idxs.append(idx)
        wts.append(wk)
        wsum = wsum + wk

    earlier = (lax.broadcasted_iota(jnp.int32, (tm, tm), 0) > lax.broadcasted_iota(jnp.int32, (tm, tm), 1))
    before = _dot(earlier.astype(BF16), onehot.astype(BF16)) + run_scr[...]
    e_out = jnp.zeros((tm, LANES), jnp.int32)
    r_out = jnp.zeros((tm, LANES), jnp.int32)
    w_out = jnp.zeros((tm, LANES), F32)
    for k in range(TOP_K):
        rk = jnp.sum(jnp.where(lane == idxs[k], before, 0.0), axis=1, keepdims=True)
        e_out = jnp.where(gcol == k, idxs[k].astype(jnp.int32), e_out)
        r_out = jnp.where(gcol == k, rk.astype(jnp.int32), r_out)
        w_out = jnp.where(gcol == k, wts[k] / wsum * ROUTED_SCALE, w_out)
    e_ref[...] = e_out
    r_ref[...] = r_out
    w_ref[...] = w_out
    run_scr[...] = run_scr[...] + jnp.sum(onehot, axis=0, keepdims=True)
    cnt_ref[...] = run_scr[...]


def _router(h2, w_router, bias, n_tok, tm=ROUTER_TM):
    tile = pl.BlockSpec((tm, LANES), lambda i: (i, 0))
    return pl.pallas_call(
        functools.partial(_router_kernel, tm=tm),
        grid=(n_tok // tm,),
        in_specs=[pl.BlockSpec((tm * TOK_ROWS, LANES), lambda i: (i, 0)),
                  pl.BlockSpec((D_MODEL, N_EXPERTS), lambda i: (0, 0)),
                  pl.BlockSpec((1, N_EXPERTS), lambda i: (0, 0))],
        out_specs=[tile, tile, tile, pl.BlockSpec((1, N_EXPERTS), lambda i: (0, 0))],
        out_shape=[jax.ShapeDtypeStruct((n_tok, LANES), jnp.int32),
                   jax.ShapeDtypeStruct((n_tok, LANES), F32),
                   jax.ShapeDtypeStruct((n_tok, LANES), jnp.int32),
                   jax.ShapeDtypeStruct((1, N_EXPERTS), F32)],
        scratch_shapes=[pltpu.VMEM((1, N_EXPERTS), F32)],
        compiler_params=_cparams(("arbitrary",)),
        name="moe_router",
    )(h2, w_router, bias)


def _dest_kernel(e_ref, r_ref, st_ref, d_ref):
    e = e_ref[...]
    tm = e.shape[0]
    lane = lax.broadcasted_iota(jnp.int32, (tm, N_EXPERTS), 1)
    col = lax.broadcasted_iota(jnp.int32, (tm, LANES), 1)
    st = st_ref[...]
    out = r_ref[...]
    for k in range(TOP_K):
        sk = jnp.sum(jnp.where(lane == e[:, k:k + 1], st, 0.0), axis=1, keepdims=True)
        out = jnp.where(col == k, out + sk.astype(jnp.int32), out)
    d_ref[...] = out


def _moe_dest(top_e, rank, starts, tm=ROUTER_TM):
    n_tok = top_e.shape[0]
    tile = pl.BlockSpec((tm, LANES), lambda i: (i, 0))
    return pl.pallas_call(
        _dest_kernel,
        grid=(n_tok // tm,),
        in_specs=[tile, tile, pl.BlockSpec((1, N_EXPERTS), lambda i: (0, 0))],
        out_specs=tile,
        out_shape=jax.ShapeDtypeStruct((n_tok, LANES), jnp.int32),
        compiler_params=_cparams(("arbitrary",)),
        name="moe_dest",
    )(top_e, rank, starts.astype(F32).reshape(1, -1))


def _moe_layout(counts, n_pairs):
    padded = (counts + MOE_BM - 1) // MOE_BM * MOE_BM
    pad_end = jnp.cumsum(padded)
    starts = pad_end - padded
    n_blocks = (n_pairs + N_EXPERTS * (MOE_BM - 1)) // MOE_BM
    n_used = pad_end[-1] // MOE_BM
    blk = jnp.arange(n_blocks, dtype=jnp.int32)
    used = blk < n_used
    e_of = jnp.sum(pad_end[None, :] <= (jnp.minimum(blk, n_used - 1) * MOE_BM)[:, None], axis=1).astype(jnp.int32)
    e_of = jnp.minimum(e_of, N_EXPERTS - 1)
    shifted = jnp.concatenate([jnp.full((1,), -1, jnp.int32), e_of[:-1]])
    fresh = (used & (e_of != shifted)).astype(jnp.int32)
    w_slot = ((jnp.cumsum(fresh) - 1) % 2).astype(jnp.int32)
    ids = jnp.arange(N_EXPERTS, dtype=jnp.int32)
    later = jnp.where(counts > 0, ids, N_EXPERTS)
    next_nonempty = jnp.concatenate([lax.cummin(later, reverse=True)[1:], jnp.full((1,), N_EXPERTS, jnp.int32)])
    next_e = next_nonempty[e_of].astype(jnp.int32)
    pad_table = jnp.concatenate([starts + counts, padded - counts, n_used[None]]).astype(jnp.int32)
    return (starts.astype(jnp.int32), pad_table,
            (e_of, fresh, used.astype(jnp.int32), w_slot, next_e), n_blocks)


def _dispatch_kernel(pad_ref, dest_ref, h_ref, xs_ref, sem, zeros, zsem, *, tm, n_blocks, bm):
    i = pl.program_id(0)
    n_steps = pl.num_programs(0)

    def row_copy(src_tok, dst_row):
        return pltpu.make_async_copy(
            h_ref.at[pl.ds(pl.multiple_of(src_tok * TOK_ROWS, TOK_ROWS), TOK_ROWS), :],
            xs_ref.at[pl.ds(pl.multiple_of(dst_row * TOK_ROWS, TOK_ROWS), TOK_ROWS), :], sem)

    def issue(t, carry):
        for k in range(TOP_K):
            row_copy(t, dest_ref[0, 0, t * TOP_K + k]).start(priority=k % 2)
        return carry

    def drain(t, carry):
        for _ in range(TOP_K):
            row_copy(0, 0).wait()
        return carry

    @pl.when(i == 0)
    def _():
        zeros[...] = jnp.zeros(zeros.shape, F32)

    def zero_fill(row0, n_rows_static):
        return pltpu.make_async_copy(
            zeros.at[pl.ds(0, n_rows_static * TOK_ROWS), :],
            xs_ref.at[pl.ds(pl.multiple_of(row0 * TOK_ROWS, TOK_ROWS), n_rows_static * TOK_ROWS), :], zsem)

    experts_per_step = -(-N_EXPERTS // n_steps)
    blocks_per_step = -(-n_blocks // n_steps)
    n_used = pad_ref[2 * N_EXPERTS]

    def fill_pass(act):
        def one_expert(j, carry):
            e = jnp.minimum(i * experts_per_step + j, N_EXPERTS - 1)
            live = i * experts_per_step + j < N_EXPERTS
            lo, length = pad_ref[e], pad_ref[N_EXPERTS + e]
            piece = bm // 2
            while piece >= 1:
                @pl.when(live & ((length & piece) != 0))
                def _(piece=piece):
                    act(zero_fill(lo + (length & ~(2 * piece - 1)), piece))
                piece //= 2
            return carry
        lax.fori_loop(0, experts_per_step, one_expert, 0)

        def one_block(j, carry):
            blk = n_used + i * blocks_per_step + j

            @pl.when(blk < n_blocks)
            def _():
                act(zero_fill(blk * bm, bm))
            return carry
        lax.fori_loop(0, blocks_per_step, one_block, 0)

    fill_pass(lambda cp: cp.start())
    lax.fori_loop(0, tm, issue, 0)
    lax.fori_loop(0, tm, drain, 0)
    fill_pass(lambda cp: cp.wait())


def _dispatch(dest_tiles, h2, pad_table, n_blocks, tm=ROUTER_TM, bm=MOE_BM):
    n_tiles = dest_tiles.shape[0]
    return pl.pallas_call(
        functools.partial(_dispatch_kernel, tm=tm, n_blocks=n_blocks, bm=bm),
        grid_spec=pltpu.PrefetchScalarGridSpec(
            num_scalar_prefetch=1,
            grid=(n_tiles,),
            in_specs=[pl.BlockSpec((1, 1, tm * TOP_K), lambda i, pad: (i, 0, 0), memory_space=pltpu.SMEM),
                      pl.BlockSpec((tm * TOK_ROWS, LANES), lambda i, pad: (i, 0))],
            out_specs=pl.BlockSpec(memory_space=pl.ANY),
            scratch_shapes=[pltpu.SemaphoreType.DMA(()), pltpu.VMEM((bm * TOK_ROWS, LANES), F32),
                            pltpu.SemaphoreType.DMA(())]),
        out_shape=jax.ShapeDtypeStruct((n_blocks * bm * TOK_ROWS, LANES), F32),
        compiler_params=_cparams(("arbitrary",)),
        name="moe_dispatch",
    )(pad_table, dest_tiles, h2)


GMM_SLOTS = 4


def _gmm_kernel(e_ref, fresh_ref, used_ref, wslot_ref, next_ref,
                xs_ref, wg_ref, wu_ref, wd_ref, ys_ref, wg_bf, wu_bf, wd_bf, xbuf, sem,
                wg_st, wu_st, wd_st, wsem, *, bm):
    i = pl.program_id(0)
    n = pl.num_programs(0)

    def fetch(j):
        rows = pl.ds(pl.multiple_of(j * (bm * TOK_ROWS), bm * TOK_ROWS), bm * TOK_ROWS)
        return pltpu.make_async_copy(xs_ref.at[rows, :], xbuf.at[j % GMM_SLOTS], sem.at[j % GMM_SLOTS])

    def weight_copies(e, slot):
        return [pltpu.make_async_copy(src.at[e], dst.at[slot], wsem.at[slot])
                for src, dst in ((wg_ref, wg_st), (wu_ref, wu_st), (wd_ref, wd_st))]

    @pl.when(i == 0)
    def _():
        for cp in weight_copies(e_ref[0], wslot_ref[0]):
            cp.start()
        for j in range(GMM_SLOTS - 1):
            @pl.when((j < n) & (used_ref[jnp.minimum(j, n - 1)] == 1))
            def _():
                fetch(j).start()

    ahead = jnp.minimum(i + GMM_SLOTS - 1, n - 1)

    @pl.when((i + GMM_SLOTS - 1 < n) & (used_ref[ahead] == 1))
    def _():
        fetch(ahead).start()

    @pl.when(fresh_ref[i] == 1)
    def _():
        slot = wslot_ref[i]
        for cp in weight_copies(e_ref[i], slot):
            cp.wait()
        wg_bf[...] = wg_st[slot].astype(BF16)
        wu_bf[...] = wu_st[slot].astype(BF16)
        wd_bf[...] = wd_st[slot].astype(BF16)

        @pl.when(next_ref[i] < N_EXPERTS)
        def _():
            for cp in weight_copies(next_ref[i], 1 - slot):
                cp.start()

    @pl.when(used_ref[i] == 1)
    def _():
        fetch(i).wait()
        x = _load_tok_tiles(xbuf.at[i % GMM_SLOTS], bm).astype(BF16)
        hid = (_silu(_dot(x, wg_bf[...])) * _dot(x, wu_bf[...])).astype(BF16)
        _store_tok_tiles(ys_ref, _dot(hid, wd_bf[...]), bm)

    @pl.when(used_ref[i] == 0)
    def _():
        ys_ref[...] = jnp.zeros(ys_ref.shape, F32)


def _moe_gmm(blocks, xs_sorted, w_gate, w_up, w_down, bm=MOE_BM):
    rows = pl.BlockSpec((bm * TOK_ROWS, LANES), lambda i, *_: (i, 0))
    in_hbm = pl.BlockSpec(memory_space=pl.ANY)
    weights = (w_gate, w_up, w_down)
    return pl.pallas_call(
        functools.partial(_gmm_kernel, bm=bm),
        grid_spec=pltpu.PrefetchScalarGridSpec(
            num_scalar_prefetch=len(blocks),
            grid=(blocks[0].shape[0],),
            in_specs=[in_hbm, in_hbm, in_hbm, in_hbm],
            out_specs=rows,
            scratch_shapes=[pltpu.VMEM(w.shape[1:], BF16) for w in weights]
            + [pltpu.VMEM((GMM_SLOTS, bm * TOK_ROWS, LANES), F32), pltpu.SemaphoreType.DMA((GMM_SLOTS,))]
            + [pltpu.VMEM((2,) + w.shape[1:], F32) for w in weights]
            + [pltpu.SemaphoreType.DMA((2,))]),
        out_shape=jax.ShapeDtypeStruct(xs_sorted.shape, F32),
        compiler_params=_cparams(("arbitrary",)),
        name="moe_experts",
    )(*blocks, xs_sorted, *weights)


def _combine_kernel(dest_ref, dnext_ref, w_ref, x1_ref, h_ref, g2_ref, wsg_ref, wsu_ref, wsd_ref, fn_ref, ys_ref,
                    o_ref, gbuf, sem, routed_scr, *, tm):
    n = pl.program_id(0) * pl.num_programs(1) + pl.program_id(1)
    total = pl.num_programs(0) * pl.num_programs(1)
    slot = n % 2

    def row_copy(src_row, p, sl):
        return pltpu.make_async_copy(
            ys_ref.at[pl.ds(pl.multiple_of(src_row * TOK_ROWS, TOK_ROWS), TOK_ROWS), :],
            gbuf.at[sl, pl.ds(pl.multiple_of(p * TOK_ROWS, TOK_ROWS), TOK_ROWS), :], sem.at[sl])

    def gather(rows_ref, sl):
        def issue(i, carry):
            for k in range(TOP_K):
                p = i * TOP_K + k
                row_copy(rows_ref[0, 0, p], p, sl).start(priority=k % 2)
            return carry
        lax.fori_loop(0, tm, issue, 0)

    def drain(i, carry):
        for _ in range(TOP_K):
            row_copy(0, 0, slot).wait()
        return carry

    @pl.when(n == 0)
    def _():
        gather(dest_ref, slot)

    @pl.when(n + 1 < total)
    def _():
        gather(dnext_ref, 1 - slot)

    h = _load_tok_tiles(h_ref, tm).astype(BF16)
    hid = (_silu(_dot(h, wsg_ref[...])) * _dot(h, wsu_ref[...])).astype(BF16)
    shared = _dot(hid, wsd_ref[...])
    lax.fori_loop(0, tm, drain, 0)

    def weigh(t, carry):
        acc = jnp.zeros((TOK_ROWS, LANES), F32)
        for k in range(TOP_K):
            p = t * TOP_K + k
            acc = acc + w_ref[0, 0, p] * gbuf[slot, pl.ds(pl.multiple_of(p * TOK_ROWS, TOK_ROWS), TOK_ROWS), :]
        routed_scr[pl.ds(pl.multiple_of(t * TOK_ROWS, TOK_ROWS), TOK_ROWS), :] = acc
        return carry

    lax.fori_loop(0, tm, weigh, 0)
    x2 = x1_ref[0] + g2_ref[0] * (_load_tok_tiles(routed_scr, tm) + shared)
    o_ref[0] = _rms(x2, fn_ref[...])


def _combine(dest_tiles, w_tiles, x1, h2, gate2, shared, fnorm, ys_sorted, tile0, tm=COMBINE_TM):
    nb, t, _ = x1.shape
    nt = t // tm
    flat = lambda b, i: tile0 + b * nt + i
    nxt = lambda b, i: tile0 + jnp.minimum(b * nt + i + 1, nb * nt - 1)
    mod_spec = _mod_spec(gate2, tm)
    const = lambda a: pl.BlockSpec(a.shape, lambda b, i: (0, 0))
    return pl.pallas_call(
        functools.partial(_combine_kernel, tm=tm),
        grid=(nb, nt),
        in_specs=[pl.BlockSpec((1, 1, tm * TOP_K), lambda b, i: (flat(b, i), 0, 0), memory_space=pltpu.SMEM),
                  pl.BlockSpec((1, 1, tm * TOP_K), lambda b, i: (nxt(b, i), 0, 0), memory_space=pltpu.SMEM),
                  pl.BlockSpec((1, 1, tm * TOP_K), lambda b, i: (flat(b, i), 0, 0), memory_space=pltpu.SMEM),
                  pl.BlockSpec((1, tm, D_MODEL), lambda b, i: (b, i, 0)),
                  pl.BlockSpec((tm * TOK_ROWS, LANES), lambda b, i: (flat(b, i), 0)),
                  mod_spec, const(shared[0]), const(shared[1]), const(shared[2]), const(fnorm),
                  pl.BlockSpec(memory_space=pl.ANY)],
        out_specs=pl.BlockSpec((1, tm, D_MODEL), lambda b, i: (b, i, 0)),
        out_shape=jax.ShapeDtypeStruct((nb, t, D_MODEL), F32),
        scratch_shapes=[pltpu.VMEM((2, tm * TOP_K * TOK_ROWS, LANES), F32), pltpu.SemaphoreType.DMA((2,)),
                        pltpu.VMEM((tm * TOK_ROWS, LANES), F32)],
        compiler_params=_cparams(("arbitrary", "arbitrary")),
        name="moe_combine",
    )(dest_tiles, dest_tiles, w_tiles, x1, h2, gate2, *shared, fnorm, ys_sorted)


def kernel(x_prompt, x_sample, c_prompt, c_sample, cache_nsa_kv, cache_win_kv, state_hgrn, page_table,
           attn_norm, ffn_norm, final_norm, hg_norm, w_ada, b_ada, w_in, hg_lb,
           cmp_pe, cmp_w1, cmp_b1, cmp_w2, w_out, w_router, router_bias,
           w_gate, w_up, w_down, ws_gate, ws_up, ws_down):
    nbp, t, _ = x_prompt.shape
    nbs, ts, _ = x_sample.shape
    ns = nbs * ts
    n_all = nbp * t + ns

    c_all = jnp.concatenate([c_prompt, c_sample], axis=0)
    c_all = jnp.pad(c_all, ((0, -c_all.shape[0] % SUBLANES), (0, 0)))
    mod = _ada(c_all, w_ada[0], b_ada[0])
    modp = mod[:nbp].reshape(nbp, 1, 6, D_MODEL)
    mods = jnp.repeat(mod[nbp:nbp + nbs].reshape(nbs, 1, 6, D_MODEL), ts, axis=1).reshape(1, ns, 6, D_MODEL)

    w_pad = _pad_w_in(w_in[0])
    cw = _compress_weights(cmp_pe[0], cmp_w1[0], cmp_b1[0], cmp_w2[0])
    wo_hg, wo_nsa = _split_w_out(w_out[0])

    hg, qpad, kv4, kvw, gates, kvsel, kvwb = _inproj(
        x_prompt, modp[:, :, 1], modp[:, :, 0], attn_norm[0], w_pad, PROJ_TM)
    hg_out, hg_state_p = _hgrn(hg, hg_lb, jnp.zeros((nbp, HG_HEADS, HG_DK, HG_DK), F32), hg_norm[0],
                               HGRN_TC, HG_CHUNK)
    n_pages_p = t // PAGE_SIZE
    ptp = jnp.arange(nbp * n_pages_p, dtype=jnp.int32).reshape(nbp, n_pages_p)
    kc, vc = _compress(ptp, kv4.reshape(nbp * n_pages_p, PAGE_SIZE, 4 * KV_W), *cw)
    nsa = _nsa_prompt(qpad, gates, kc, vc, kvsel, kvwb)
    x1p, h2p = _outproj(x_prompt, hg_out, nsa, modp[:, :, 2], modp[:, :, 4], modp[:, :, 3],
                        ffn_norm[0], wo_hg, wo_nsa, PROJ_TM)

    xs = x_sample.reshape(1, ns, D_MODEL)
    hg_s, qpad_s, kv4_s, kvw_s, gates_s, _, _ = _inproj(
        xs, mods[:, :, 1], mods[:, :, 0], attn_norm[0], w_pad, ns)
    hg_out_s, hg_state_s = _hgrn(hg_s.reshape(nbs, ts, 4 * HG_WIDTH), hg_lb, state_hgrn[0], hg_norm[0], ts, ts)
    cache = cache_nsa_kv[0].transpose(0, 2, 3, 4, 1).reshape(-1, 4 * KV_W, PAGE_SIZE)
    win_buf = cache_win_kv[0].reshape(nbs, -1, 2 * KV_W)
    nsa_s = _nsa_sample(page_table, cache, win_buf, cw, qpad_s.reshape(nbs, ts, QPAD_W),
                        gates_s.reshape(nbs, ts, GZ_PAD), kv4_s.reshape(nbs, ts, 4 * KV_W),
                        kvw_s.reshape(nbs, ts, 2 * KV_W)).reshape(1, ns, QPAD_W)
    x1s, h2s = _outproj(xs, hg_out_s.reshape(1, ns, HG_WIDTH), nsa_s, mods[:, :, 2], mods[:, :, 4], mods[:, :, 3],
                        ffn_norm[0], wo_hg, wo_nsa, ns)

    h2 = jnp.concatenate([h2p, h2s], axis=0)
    top_e, top_w, rank, counts = _router(h2, w_router[0].astype(BF16), router_bias[0].reshape(1, -1), n_all)
    starts, pad_table, blocks, n_blocks = _moe_layout(counts[0].astype(jnp.int32), n_all * TOP_K)
    dest = _moe_dest(top_e, rank, starts)[:, :TOP_K].reshape(-1)
    xs_sorted = _dispatch(dest.reshape(n_all // ROUTER_TM, 1, ROUTER_TM * TOP_K), h2, pad_table, n_blocks)
    ys_sorted = _moe_gmm(blocks, xs_sorted, w_gate[0], w_up[0], w_down[0])
    dest_c = dest.reshape(n_all // COMBINE_TM, 1, COMBINE_TM * TOP_K)
    w_c = top_w[:, :TOP_K].reshape(n_all // COMBINE_TM, 1, COMBINE_TM * TOP_K)
    shared = (ws_gate[0].astype(BF16), ws_up[0].astype(BF16), ws_down[0].astype(BF16))
    fnorm = final_norm.reshape(1, -1)
    y_prompt = _combine(dest_c, w_c, x1p, h2, modp[:, :, 5], shared, fnorm, ys_sorted, 0)
    y_sample = _combine(dest_c, w_c, x1s, h2, mods[:, :, 5], shared, fnorm, ys_sorted, nbp * t // COMBINE_TM)

    wb = win_buf.shape[1]
    win_p = kvw[:, t - min(WINDOW, t):]
    win_s = jnp.concatenate([win_buf, kvw_s.reshape(nbs, ts, 2 * KV_W)], axis=1)[:, -wb:]
    kv_shape = (4, NSA_KV_HEADS, NSA_HEAD_DIM)
    win_shape = (2, NSA_KV_HEADS, NSA_HEAD_DIM)
    return (y_prompt,
            y_sample.reshape(nbs, ts, D_MODEL),
            kv4.reshape(1, nbp, t, *kv_shape),
            win_p.reshape(1, nbp, -1, *win_shape),
            hg_state_p[None],
            kv4_s.reshape(1, nbs, ts, *kv_shape),
            win_s.reshape(1, nbs, wb, *win_shape),
            hg_state_s[None])
```

```python
import functools

import jax
import jax.numpy as jnp
import numpy as np
from jax import lax
from jax.experimental import pallas as pl
from jax.experimental.pallas import tpu as pltpu

F32 = jnp.float32
BF16 = jnp.bfloat16

D_MODEL = 1024
HG_WIDTH = 512
HG_HEADS = 4
HG_DK = 128
HG_CHUNK = 32
NSA_WIDTH = 512
NSA_HEADS = 8
NSA_HEAD_DIM = 64
NSA_KV_HEADS = 2
NSA_GROUP = 4
KV_W = 128
CMP_BLOCK = 32
CMP_STRIDE = 16
CMP_HIDDEN = 256
SEL_BLOCK = 64
SEL_TOPN = 16
WINDOW = 512
Q_BLOCK = 256
N_EXPERTS = 256
TOP_K = 8
N_GROUPS = 8
TOPK_GROUPS = 4
ROUTED_SCALE = 2.5
RMS_EPS = 1e-6
PAGE_SIZE = 128

LANES = 128
SUBLANES = 8
TOK_ROWS = D_MODEL // LANES
VMEM_LIMIT = 56 * 1024 * 1024

QPAD_W = NSA_HEADS * LANES
GZ_PAD = LANES
INP_COLS = 4 * HG_WIDTH + QPAD_W + 4 * KV_W + 2 * KV_W + GZ_PAD

SEL_SHIFT = SEL_BLOCK.bit_length() - 1
GROUP_SHIFT = (N_EXPERTS // N_GROUPS).bit_length() - 1
NEG = -1e30
PROJ_TM = 512
HGRN_TC = 256
PAGES_PER_STEP = 32
MOE_BM = 256
ROUTER_TM = 384
COMBINE_TM = 128


def _cparams(sem):
    return pltpu.CompilerParams(dimension_semantics=sem, vmem_limit_bytes=VMEM_LIMIT)


def _dot(a, b):
    return jnp.dot(a, b, preferred_element_type=F32)


def _dot_nt(a, b):
    return lax.dot_general(a, b, (((1,), (1,)), ((), ())), preferred_element_type=F32)


def _dot_tn(a, b):
    return lax.dot_general(a, b, (((0,), (0,)), ((), ())), preferred_element_type=F32)


def _rms(x, g):
    return x * lax.rsqrt(jnp.mean(x * x, axis=-1, keepdims=True) + RMS_EPS) * g


def _silu(x):
    return x * jax.nn.sigmoid(x)


Q_SCALE = NSA_HEAD_DIM ** -0.5 * 1.4426950408889634


def _masked_softmax(s, valid):
    s = jnp.where(valid, s, NEG)
    m = jnp.max(s, axis=1, keepdims=True)
    p = jnp.exp2(s - m) * valid.astype(F32)
    return p / jnp.maximum(jnp.sum(p, axis=1, keepdims=True), 1e-30)


def _topk_mask(v, k):
    lane = lax.broadcasted_iota(jnp.int32, v.shape, 1).astype(F32)
    sel = jnp.zeros(v.shape, F32)
    for _ in range(k):
        m = jnp.max(v, axis=1, keepdims=True)
        idx = jnp.min(jnp.where(v == m, lane, 1e9), axis=1, keepdims=True)
        pick = lane == idx
        sel = jnp.where(pick, 1.0, sel)
        v = jnp.where(pick, 3.0 * NEG, v)
    return sel


def _mod_spec(mod, tm):
    if mod.shape[1] == 1:
        return pl.BlockSpec((1, 1, D_MODEL), lambda b, i: (b, 0, 0))
    return pl.BlockSpec((1, tm, D_MODEL), lambda b, i: (b, i, 0))


def _load_tok_tiles(ref, n_tok):
    return jnp.concatenate([ref[pl.ds(s, n_tok, stride=TOK_ROWS), :] for s in range(TOK_ROWS)], axis=1)


def _store_tok_tiles(ref, val, n_tok):
    for s in range(TOK_ROWS):
        ref[pl.ds(s, n_tok, stride=TOK_ROWS), :] = val[:, s * LANES:(s + 1) * LANES]


def _ada_kernel(c_ref, w_ref, b_ref, o_ref):
    s = _silu(c_ref[...]).astype(BF16)
    o_ref[...] = _dot(s, w_ref[...].astype(BF16)) + b_ref[...]


def _ada(c_all, w_ada, b_ada):
    n = c_all.shape[0]
    return pl.pallas_call(
        _ada_kernel,
        grid=(6,),
        in_specs=[pl.BlockSpec((n, D_MODEL), lambda j: (0, 0)),
                  pl.BlockSpec((D_MODEL, D_MODEL), lambda j: (0, j)),
                  pl.BlockSpec((1, D_MODEL), lambda j: (0, j))],
        out_specs=pl.BlockSpec((n, D_MODEL), lambda j: (0, j)),
        out_shape=jax.ShapeDtypeStruct((n, 6 * D_MODEL), F32),
        compiler_params=_cparams(("arbitrary",)),
        name="ada_mod",
    )(c_all, w_ada, b_ada.reshape(1, -1))


def _inproj_kernel(x_ref, sc_ref, sh_ref, g_ref, w_ref,
                   hg_ref, q_ref, kv4_ref, kvw_ref, gate_ref, kvsel_ref, kvwb_ref):
    h = _rms(x_ref[0], g_ref[...]) * (1.0 + sc_ref[0]) + sh_ref[0]
    z = _dot(h.astype(BF16), w_ref[...])
    c0 = 4 * HG_WIDTH
    hg_ref[0] = z[:, :c0]
    q_ref[0] = (z[:, c0:c0 + QPAD_W] * Q_SCALE).astype(BF16)
    c1 = c0 + QPAD_W
    kv4 = z[:, c1:c1 + 4 * KV_W]
    kv4_ref[0] = kv4
    kvsel_ref[0] = kv4[:, 2 * KV_W:].astype(BF16)
    c2 = c1 + 4 * KV_W
    kvw = z[:, c2:c2 + 2 * KV_W]
    kvw_ref[0] = kvw
    kvwb_ref[0] = kvw.astype(BF16)
    gate_ref[0] = jax.nn.sigmoid(z[:, c2 + 2 * KV_W:])


def _inproj(x, scale, shift, g_norm, w_pad, tm):
    nb, t, _ = x.shape
    mod_spec = _mod_spec(scale, tm)
    widths = [(4 * HG_WIDTH, F32), (QPAD_W, BF16), (4 * KV_W, F32), (2 * KV_W, F32), (GZ_PAD, F32),
              (2 * KV_W, BF16), (2 * KV_W, BF16)]
    return pl.pallas_call(
        _inproj_kernel,
        grid=(nb, t // tm),
        in_specs=[pl.BlockSpec((1, tm, D_MODEL), lambda b, i: (b, i, 0)),
                  mod_spec, mod_spec,
                  pl.BlockSpec((1, D_MODEL), lambda b, i: (0, 0)),
                  pl.BlockSpec((D_MODEL, INP_COLS), lambda b, i: (0, 0))],
        out_specs=[pl.BlockSpec((1, tm, w), lambda b, i: (b, i, 0)) for w, _ in widths],
        out_shape=[jax.ShapeDtypeStruct((nb, t, w), dt) for w, dt in widths],
        compiler_params=_cparams(("arbitrary", "arbitrary")),
        name="in_proj",
    )(x, scale, shift, g_norm.reshape(1, -1), w_pad)


def _pad_w_in(w_in):
    c0 = 4 * HG_WIDTH
    wq = w_in[:, c0:c0 + NSA_WIDTH].reshape(D_MODEL, NSA_HEADS, NSA_HEAD_DIM)
    zeros = jnp.zeros_like(wq)
    lo = jnp.concatenate([wq, zeros], axis=-1)
    hi = jnp.concatenate([zeros, wq], axis=-1)
    grp = (jnp.arange(NSA_HEADS) // NSA_GROUP)[None, :, None]
    wq_pad = jnp.where(grp == 0, lo, hi).reshape(D_MODEL, QPAD_W)
    c1 = c0 + NSA_WIDTH
    rest = w_in[:, c1:c1 + 6 * KV_W]
    gz = jnp.pad(w_in[:, c1 + 6 * KV_W:], ((0, 0), (0, GZ_PAD - 3 * NSA_HEADS)))
    return jnp.concatenate([w_in[:, :c0], wq_pad, rest, gz], axis=1).astype(BF16)


def _hgrn_kernel(q_ref, f_ref, v_ref, gt_ref, lb_ref, s0_ref, gn_ref, o_ref, s_out_ref, st_scr,
                 *, chunk, n_chunks):
    i = pl.program_id(1)

    @pl.when(i == 0)
    def _():
        for h in range(HG_HEADS):
            st_scr[h] = s0_ref[0, h].T

    lbr = lb_ref[...]
    e = jnp.exp(lbr - jnp.max(lbr, axis=0, keepdims=True))
    lb_all = e[0:1] / jnp.sum(e, axis=0, keepdims=True)
    row = lax.broadcasted_iota(jnp.int32, (chunk, HG_DK), 0)
    causal = (lax.broadcasted_iota(jnp.int32, (chunk, chunk), 0)
              >= lax.broadcasted_iota(jnp.int32, (chunk, chunk), 1))
    st = [st_scr[h] for h in range(HG_HEADS)]
    for c in range(n_chunks):
        sl = pl.ds(c * chunk, chunk)
        for h in range(HG_HEADS):
            hs = slice(h * HG_DK, (h + 1) * HG_DK)
            lb = lb_all[:, hs]
            z = f_ref[0, sl, hs]
            log_f = jnp.log(lb + (1.0 - lb) * jax.nn.sigmoid(z))
            kk = (1.0 - lb) * jax.nn.sigmoid(-z)
            a = log_f
            s = 1
            while s < chunk:
                a = a + jnp.where(row >= s, pltpu.roll(a, s, 0), 0.0)
                s *= 2
            qt = (q_ref[0, sl, hs] * jnp.exp(a)).astype(BF16)
            kt = (kk * jnp.exp(-a)).astype(BF16)
            v = v_ref[0, sl, hs].astype(BF16)
            att = jnp.where(causal, _dot_nt(qt, kt), 0.0)
            o = _dot(att.astype(BF16), v) + _dot_nt(qt, st[h].astype(BF16))
            a_end = a[chunk - 1:chunk, :]
            kd = (kk * jnp.exp(a_end - a)).astype(BF16)
            st[h] = st[h] * jnp.exp(a_end) + _dot_tn(v, kd)
            o = _rms(o, gn_ref[...]) * _silu(gt_ref[0, sl, hs])
            o_ref[0, sl, hs] = o.astype(o_ref.dtype)
    for h in range(HG_HEADS):
        st_scr[h] = st[h]

    @pl.when(i == pl.num_programs(1) - 1)
    def _():
        for h in range(HG_HEADS):
            s_out_ref[0, h] = st[h].T


def _hgrn(hg, hg_lb, s0, g_norm, tc, chunk):
    nb, t, _ = hg.shape
    part = lambda k: pl.BlockSpec((1, tc, HG_WIDTH), lambda b, i: (b, i, k))
    st_spec = pl.BlockSpec((1, HG_HEADS, HG_DK, HG_DK), lambda b, i: (b, 0, 0, 0))
    return pl.pallas_call(
        functools.partial(_hgrn_kernel, chunk=chunk, n_chunks=tc // chunk),
        grid=(nb, t // tc),
        in_specs=[part(0), part(1), part(2), part(3),
                  pl.BlockSpec(hg_lb.shape, lambda b, i: (0, 0)),
                  st_spec,
                  pl.BlockSpec((1, HG_DK), lambda b, i: (0, 0))],
        out_specs=[pl.BlockSpec((1, tc, HG_WIDTH), lambda b, i: (b, i, 0)), st_spec],
        out_shape=[jax.ShapeDtypeStruct((nb, t, HG_WIDTH), BF16),
                   jax.ShapeDtypeStruct((nb, HG_HEADS, HG_DK, HG_DK), F32)],
        scratch_shapes=[pltpu.VMEM((HG_HEADS, HG_DK, HG_DK), F32)],
        compiler_params=_cparams(("arbitrary", "arbitrary")),
        name="hgrn2",
    )(hg, hg, hg, hg, hg_lb, s0, g_norm.reshape(1, -1))


def _gelu_tanh(x):
    return 0.5 * x * (1.0 + jnp.tanh(0.7978845608028654 * (x + 0.044715 * x * x * x)))


def _page_copies(pt_ref, cache_ref, buf, sem, b, s, slot, n_pages, pps, col0, tail, transposed):
    copies = []
    base = s * pps
    nxt = pt_ref[b, jnp.minimum(base + pps, n_pages - 1)]
    for br in range(2):
        cols = pl.ds(col0 + br * KV_W, KV_W)
        for i in range(pps):
            pg = pt_ref[b, base + i]
            if transposed:
                copies.append(pltpu.make_async_copy(
                    cache_ref.at[pg, cols, :],
                    buf.at[slot, br, :, pl.ds(i * PAGE_SIZE, PAGE_SIZE)], sem.at[slot]))
            else:
                copies.append(pltpu.make_async_copy(
                    cache_ref.at[pg, :, cols],
                    buf.at[slot, br, pl.ds(i * PAGE_SIZE, PAGE_SIZE), :], sem.at[slot]))
        if tail and transposed:
            copies.append(pltpu.make_async_copy(
                cache_ref.at[nxt, cols, :],
                buf.at[slot, br, :, pl.ds(pps * PAGE_SIZE, PAGE_SIZE)], sem.at[slot]))
        elif tail:
            copies.append(pltpu.make_async_copy(
                cache_ref.at[nxt, pl.ds(0, CMP_STRIDE), cols],
                buf.at[slot, br, pl.ds(pps * PAGE_SIZE, CMP_STRIDE), :], sem.at[slot]))
    return copies


def _stream_pages(pt_ref, cache_ref, buf, sem, n_pages, n_steps, pps, col0, tail, transposed):
    b = pl.program_id(0)
    s = pl.program_id(1)
    n = b * n_steps + s
    total = pl.num_programs(0) * n_steps
    slot = n % 2
    args = (n_pages, pps, col0, tail, transposed)

    @pl.when(n == 0)
    def _():
        for cp in _page_copies(pt_ref, cache_ref, buf, sem, b, s, slot, *args):
            cp.start()

    @pl.when(n + 1 < total)
    def _():
        n1 = n + 1
        for cp in _page_copies(pt_ref, cache_ref, buf, sem, n1 // n_steps, n1 % n_steps, 1 - slot, *args):
            cp.start()

    for cp in _page_copies(pt_ref, cache_ref, buf, sem, b, s, slot, *args):
        cp.wait()
    return slot


def _compress_kernel(pt_ref, cache_ref, pe_ref, w1_ref, b1_ref, w2_ref, kc_ref, vc_ref, buf, sem, *rowbuf,
                     n_pages, n_steps, pps, transposed):
    groups = pps * PAGE_SIZE // CMP_STRIDE

    def compress(read_rows):
        low_half = lax.broadcasted_iota(jnp.int32, (groups, KV_W), 1) < NSA_HEAD_DIM
        for br, out_ref in ((0, kc_ref), (1, vc_ref)):
            acc0 = jnp.zeros((groups, CMP_HIDDEN), F32)
            acc1 = jnp.zeros((groups, CMP_HIDDEN), F32)
            for j in range(CMP_BLOCK // 2):
                a = read_rows(br, 2 * j) + pe_ref[br, 2 * j:2 * j + 1, :]
                b = pltpu.roll(read_rows(br, 2 * j + 1) + pe_ref[br, 2 * j + 1:2 * j + 2, :], NSA_HEAD_DIM, 1)
                acc0 = acc0 + _dot(jnp.where(low_half, a, b).astype(BF16), w1_ref[br, j, 0])
                acc1 = acc1 + _dot(jnp.where(low_half, b, a).astype(BF16), w1_ref[br, j, 1])
            hid = _gelu_tanh(jnp.concatenate([acc0, acc1], axis=1) + b1_ref[br]).astype(BF16)
            out_ref[0] = _dot(hid, w2_ref[br]).astype(out_ref.dtype)

    if transposed:
        rows_ref, = rowbuf
        n = pl.program_id(0) * n_steps + pl.program_id(1)
        total = pt_ref.shape[0] * n_steps
        copies = lambda m: _page_copies(pt_ref, cache_ref, buf, sem, m // n_steps, m % n_steps, m % 2,
                                        n_pages, pps, 0, True, True)

        def to_rows(stage, half):
            for br in range(2):
                for i in range(pps + 1):
                    n_rows = PAGE_SIZE if i < pps else CMP_STRIDE
                    page_t = buf[stage, br, :, i * PAGE_SIZE:(i + 1) * PAGE_SIZE]
                    rows_ref[half, br, i * PAGE_SIZE:i * PAGE_SIZE + n_rows, :] = page_t.T[:n_rows]

        @pl.when(n == 0)
        def _():
            for cp in copies(0):
                cp.start()
            for cp in copies(0):
                cp.wait()
            to_rows(0, 0)
            if total > 1:
                for cp in copies(1):
                    cp.start()

        @pl.when(n + 1 < total)
        def _():
            for cp in copies(n + 1):
                cp.wait()

        @pl.when(n + 2 < total)
        def _():
            for cp in copies(n + 2):
                cp.start()

        for parity in range(2):
            @pl.when(n % 2 == parity)
            def _(parity=parity):
                if total > 1:
                    to_rows(1 - parity, 1 - parity)
                compress(lambda br, l: rows_ref[parity, br, pl.ds(l, groups, stride=CMP_STRIDE), :])
    else:
        slot = _stream_pages(pt_ref, cache_ref, buf, sem, n_pages, n_steps, pps, 0, True, False)
        compress(lambda br, l: buf[slot, br, pl.ds(l, groups, stride=CMP_STRIDE), :])


def _compress(page_table, cache, pe2, w1cat, b1cat, w2bd, pps=PAGES_PER_STEP, transposed=False):
    nb, n_pages = page_table.shape
    n_steps = n_pages // pps
    groups = pps * PAGE_SIZE // CMP_STRIDE
    rows = pps * PAGE_SIZE + CMP_STRIDE
    const = lambda shape: pl.BlockSpec(shape, lambda b, s, pt: (0,) * len(shape))
    out_spec = pl.BlockSpec((1, groups, KV_W), lambda b, s, pt: (b, s, 0))
    out_sds = jax.ShapeDtypeStruct((nb, n_steps * groups, KV_W), BF16)
    if transposed:
        stage = [pltpu.VMEM((2, 2, KV_W, (pps + 1) * PAGE_SIZE), F32), pltpu.SemaphoreType.DMA((2,)),
                 pltpu.VMEM((2, 2, rows, KV_W), F32)]
    else:
        stage = [pltpu.VMEM((2, 2, rows, KV_W), F32), pltpu.SemaphoreType.DMA((2,))]
    return pl.pallas_call(
        functools.partial(_compress_kernel, n_pages=n_pages, n_steps=n_steps, pps=pps, transposed=transposed),
        grid_spec=pltpu.PrefetchScalarGridSpec(
            num_scalar_prefetch=1,
            grid=(nb, n_steps),
            in_specs=[pl.BlockSpec(memory_space=pl.ANY),
                      const((2, CMP_BLOCK, KV_W)),
                      const((2, CMP_BLOCK // 2, 2, KV_W, CMP_HIDDEN)),
                      const((2, 1, 2 * CMP_HIDDEN)),
                      const((2, 2 * CMP_HIDDEN, KV_W))],
            out_specs=[out_spec, out_spec],
            scratch_shapes=stage),
        out_shape=[out_sds, out_sds],
        compiler_params=_cparams(("arbitrary", "arbitrary")),
        name="nsa_compress",
    )(page_table, cache, pe2, w1cat, b1cat, w2bd)


def _compress_weights(cmp_pe, cmp_w1, cmp_b1, cmp_w2):
    pe2 = jnp.concatenate([cmp_pe, cmp_pe], axis=-1)
    w1 = cmp_w1.reshape(2, CMP_BLOCK // 2, 2, NSA_HEAD_DIM, CMP_HIDDEN)
    even_odd = w1.reshape(2, CMP_BLOCK // 2, 2 * NSA_HEAD_DIM, CMP_HIDDEN)
    odd_even = w1[:, :, ::-1].reshape(2, CMP_BLOCK // 2, 2 * NSA_HEAD_DIM, CMP_HIDDEN)
    w1cat = jnp.stack([even_odd, odd_even], axis=2).astype(BF16)
    b1cat = jnp.concatenate([cmp_b1, cmp_b1], axis=-1)[:, None, :]
    z2 = jnp.zeros_like(cmp_w2)
    w2bd = jnp.concatenate([jnp.concatenate([cmp_w2, z2], axis=-1),
                            jnp.concatenate([z2, cmp_w2], axis=-1)], axis=1).astype(BF16)
    return pe2, w1cat, b1cat, w2bd


def _overlap_matrix(n_cmp, n_sel):
    cs = lax.broadcasted_iota(jnp.int32, (n_cmp, n_sel), 0) * CMP_STRIDE
    ss = lax.broadcasted_iota(jnp.int32, (n_cmp, n_sel), 1) * SEL_BLOCK
    return ((cs < ss + SEL_BLOCK) & (cs + CMP_BLOCK > ss)).astype(BF16)


SEL_CHUNK = 512
SPREAD_KEYS = 1024
WIN_SPAN = WINDOW + Q_BLOCK


def _nsa_prompt_kernel(q_ref, gt_ref, kc_ref, vc_ref, ks_ref, vs_ref, kw_ref, vw_ref, ex_ref, o_ref,
                       m_scr, acc_scr, chosen_scr, oc_scr, sa_scr, sb_scr, *, n_cmp, n_sel):
    j = pl.program_id(1)
    q0 = j * Q_BLOCK
    tok = lax.broadcasted_iota(jnp.int32, (Q_BLOCK, 1), 0) + q0
    tok4 = jnp.concatenate([tok] * NSA_GROUP, axis=0)
    n_chunks = j // (SEL_CHUNK // Q_BLOCK) + 1
    key_in_span = lax.broadcasted_iota(jnp.int32, (1, SPREAD_KEYS), 1)

    def load_q(g):
        return jnp.concatenate([q_ref[0, :, (NSA_GROUP * g + jh) * LANES:(NSA_GROUP * g + jh + 1) * LANES]
                                for jh in range(NSA_GROUP)], axis=0)

    def softmax_av(s, valid, v):
        s = jnp.where(valid, s, NEG)
        tiles = [s[:, i * LANES:(i + 1) * LANES] for i in range(s.shape[1] // LANES)]
        m = jnp.max(functools.reduce(jnp.maximum, tiles), axis=1, keepdims=True)
        p = jnp.where(valid, jnp.exp2(s - m), 0.0)
        acc = _dot(p.astype(BF16), jnp.concatenate([v, jnp.ones(v.shape, BF16)], axis=1))
        inv = 1.0 / jnp.maximum(acc[:, KV_W:], 1e-30)
        return p, acc[:, :KV_W] * inv, inv

    ov = _overlap_matrix(n_cmp, n_sel)
    cend = lax.broadcasted_iota(jnp.int32, (1, n_cmp), 1) * CMP_STRIDE + (CMP_BLOCK - 1)
    ws = pl.multiple_of(jnp.maximum(q0 - WINDOW, 0), Q_BLOCK)
    wpos = ws + lax.broadcasted_iota(jnp.int32, (1, WIN_SPAN), 1)
    d = tok4 - wpos
    in_window = (d >= 0) & (d < WINDOW)
    gates = gt_ref[0]
    imps = []
    for g in range(NSA_KV_HEADS):
        q = load_q(g)
        p, o_c, inv = softmax_av(_dot_nt(q, kc_ref[0]), cend <= tok4, vc_ref[0])
        _, o_w, _ = softmax_av(_dot_nt(q, kw_ref[0, pl.ds(ws, WIN_SPAN), :]), in_window,
                               vw_ref[0, pl.ds(ws, WIN_SPAN), :])
        psum = jnp.zeros((Q_BLOCK, n_cmp), F32)
        for jh in range(NSA_GROUP):
            h = NSA_GROUP * g + jh
            r = slice(jh * Q_BLOCK, (jh + 1) * Q_BLOCK)
            psum = psum + p[r] * jnp.concatenate([inv[r]] * (n_cmp // LANES), axis=1)
            oc_scr[g, r, :] = gates[:, 3 * h:3 * h + 1] * o_c[r] + gates[:, 3 * h + 2:3 * h + 3] * o_w[r]
        imps.append(_dot(psum.astype(BF16), ov))

    blk = lax.broadcasted_iota(jnp.int32, (1, n_sel), 1)
    cur = lax.shift_right_logical(tok, SEL_SHIFT)
    forced = (blk == 0) | (blk == cur) | (blk == cur - 1)
    free = (blk * SEL_BLOCK <= tok) & jnp.logical_not(forced)
    forced2 = jnp.concatenate([forced] * NSA_KV_HEADS, axis=0)
    free2 = jnp.concatenate([free] * NSA_KV_HEADS, axis=0)
    best = _topk_mask(jnp.where(free2, jnp.concatenate(imps, axis=0), NEG), SEL_TOPN - 3)
    sel2 = jnp.where(forced2, 1.0, best).astype(BF16)

    ones_blk = jnp.ones((SEL_CHUNK, KV_W), BF16)

    for g in range(NSA_KV_HEADS):
        sel = sel2[g * Q_BLOCK:(g + 1) * Q_BLOCK]
        m_scr[...] = jnp.full(m_scr.shape, NEG, F32)
        acc_scr[...] = jnp.zeros(acc_scr.shape, F32)

        def spread(i, carry):
            c0 = pl.multiple_of(i * SPREAD_KEYS, SPREAD_KEYS)
            keep = (_dot(sel, ex_ref[:, pl.ds(c0, SPREAD_KEYS)]) > 0.5) & (key_in_span + c0 <= tok)
            chosen_scr[:, pl.ds(c0, SPREAD_KEYS)] = jnp.where(keep, 0.0, NEG)
            return carry

        lax.fori_loop(0, (n_chunks * SEL_CHUNK + SPREAD_KEYS - 1) // SPREAD_KEYS, spread, 0)

        heads = range(NSA_GROUP)
        rows = [pl.ds(jh * Q_BLOCK, Q_BLOCK) for jh in heads]

        def stage_scores(c, buf):
            k0 = pl.multiple_of(c * SEL_CHUNK, SEL_CHUNK)
            bias = chosen_scr[:, pl.ds(k0, SEL_CHUNK)]
            kblk = ks_ref[0, pl.ds(k0, SEL_CHUNK), :]
            for jh in heads:
                h = NSA_GROUP * g + jh
                buf[rows[jh], :] = _dot_nt(q_ref[0, :, h * LANES:(h + 1) * LANES], kblk) + bias

        def accumulate(c, buf):
            k0 = pl.multiple_of(c * SEL_CHUNK, SEL_CHUNK)
            vext = jnp.concatenate([vs_ref[0, pl.ds(k0, SEL_CHUNK), :], ones_blk], axis=1)
            tiles = [[buf[r, i * LANES:(i + 1) * LANES] for i in range(SEL_CHUNK // LANES)] for r in rows]
            m_old = [m_scr[r, :] for r in rows]
            m_new = [jnp.maximum(mo, jnp.max(functools.reduce(jnp.maximum, t), axis=1, keepdims=True))
                     for mo, t in zip(m_old, tiles)]
            probs = [jnp.concatenate([jnp.exp2(x - mn) for x in t], axis=1).astype(BF16)
                     for mn, t in zip(m_new, tiles)]
            pv = [_dot(p, vext) for p in probs]
            for r, mo, mn, y in zip(rows, m_old, m_new, pv):
                alpha = jnp.exp2(mo - mn)
                acc_scr[r, :] = jnp.concatenate([alpha, alpha], axis=1) * acc_scr[r, :] + y
                m_scr[r, :] = mn

        stage_scores(0, sa_scr)

        def body(i, carry):
            stage_scores(2 * i + 1, sb_scr)
            accumulate(2 * i, sa_scr)
            stage_scores(jnp.minimum(2 * i + 2, n_chunks - 1), sa_scr)
            accumulate(2 * i + 1, sb_scr)
            return carry

        lax.fori_loop(0, n_chunks // 2, body, 0)

        @pl.when(n_chunks % 2 == 1)
        def _():
            accumulate(n_chunks - 1, sa_scr)
        o_s = acc_scr[:, :KV_W] / jnp.maximum(acc_scr[:, KV_W:], 1e-30)

        for jh in range(NSA_GROUP):
            h = NSA_GROUP * g + jh
            r = slice(jh * Q_BLOCK, (jh + 1) * Q_BLOCK)
            o = oc_scr[g, r, :] + gates[:, 3 * h + 1:3 * h + 2] * o_s[r]
            o_ref[0, :, h * LANES:(h + 1) * LANES] = o.astype(o_ref.dtype)


def _nsa_prompt(qpad, gates, kc, vc, kvsel, kvwb):
    nb, t, _ = qpad.shape
    assert t % SPREAD_KEYS == 0 and SPREAD_KEYS == 2 * SEL_CHUNK and t >= WIN_SPAN
    n_cmp = kc.shape[1]
    n_sel = t // SEL_BLOCK
    full = lambda w, k: pl.BlockSpec((1, t, w), lambda b, j: (b, 0, k))
    t_pad = -(-t // SPREAD_KEYS) * SPREAD_KEYS
    expand = jnp.asarray(np.arange(n_sel)[:, None] == (np.arange(t_pad)[None, :] // SEL_BLOCK), BF16)
    return pl.pallas_call(
        functools.partial(_nsa_prompt_kernel, n_cmp=n_cmp, n_sel=n_sel),
        grid=(nb, t // Q_BLOCK),
        in_specs=[pl.BlockSpec((1, Q_BLOCK, QPAD_W), lambda b, j: (b, j, 0)),
                  pl.BlockSpec((1, Q_BLOCK, GZ_PAD), lambda b, j: (b, j, 0)),
                  pl.BlockSpec((1, n_cmp, KV_W), lambda b, j: (b, 0, 0)),
                  pl.BlockSpec((1, n_cmp, KV_W), lambda b, j: (b, 0, 0)),
                  full(KV_W, 0), full(KV_W, 1), full(KV_W, 0), full(KV_W, 1),
                  pl.BlockSpec((n_sel, t_pad), lambda b, j: (0, 0))],
        out_specs=pl.BlockSpec((1, Q_BLOCK, QPAD_W), lambda b, j: (b, j, 0)),
        out_shape=jax.ShapeDtypeStruct((nb, t, QPAD_W), BF16),
        scratch_shapes=[pltpu.VMEM((NSA_GROUP * Q_BLOCK, LANES), F32),
                        pltpu.VMEM((NSA_GROUP * Q_BLOCK, 2 * KV_W), F32),
                        pltpu.VMEM((Q_BLOCK, t_pad), F32),
                        pltpu.VMEM((NSA_KV_HEADS, NSA_GROUP * Q_BLOCK, KV_W), F32),
                        pltpu.VMEM((NSA_GROUP * Q_BLOCK, SEL_CHUNK), F32),
                        pltpu.VMEM((NSA_GROUP * Q_BLOCK, SEL_CHUNK), F32)],
        compiler_params=_cparams(("arbitrary", "arbitrary")),
        name="nsa_prompt",
    )(qpad, gates, kc, vc, kvsel, kvsel, kvwb, kvwb, expand)


def _nsa_sample_a_kernel(q_ref, g_ref, kc_ref, vc_ref, wb_ref, nw_ref, ocw_ref, sel_ref,
                         *, past_len, n_tok, n_sel, n_sel_pad):
    q = q_ref[0]
    rows = q.shape[0]
    n_cmp = kc_ref.shape[1]
    t_row = lax.broadcasted_iota(jnp.int32, (rows, 1), 0) & (n_tok - 1)
    qpos = past_len + t_row
    cend = lax.broadcasted_iota(jnp.int32, (1, n_cmp), 1) * CMP_STRIDE + (CMP_BLOCK - 1)
    p_c = _masked_softmax(_dot_nt(q, kc_ref[0]), cend <= qpos)
    o_c = _dot(p_c.astype(BF16), vc_ref[0])

    per_grp = NSA_GROUP * n_tok
    psum = jnp.concatenate(
        [sum(p_c[g * per_grp + jh * n_tok:g * per_grp + (jh + 1) * n_tok] for jh in range(NSA_GROUP))
         for g in range(NSA_KV_HEADS)], axis=0)
    imp = _dot(psum.astype(BF16), _overlap_matrix(n_cmp, n_sel_pad))
    blk = lax.broadcasted_iota(jnp.int32, (1, n_sel_pad), 1)
    tq = past_len + (lax.broadcasted_iota(jnp.int32, (NSA_KV_HEADS * n_tok, 1), 0) & (n_tok - 1))
    cur = lax.shift_right_logical(tq, SEL_SHIFT)
    forced = (blk == 0) | (blk == cur) | (blk == cur - 1)
    allowed = blk * SEL_BLOCK <= tq
    v = jnp.where(forced, -NEG, jnp.where(allowed, imp, NEG))
    sel_ref[0] = _topk_mask(jnp.where(blk < n_sel, v, 2.0 * NEG), SEL_TOPN)

    wb = wb_ref.shape[1]
    kw = wb_ref[0, :, 0:KV_W].astype(BF16)
    vw = wb_ref[0, :, KV_W:2 * KV_W].astype(BF16)
    kn = nw_ref[0, :, 0:KV_W].astype(BF16)
    vn = nw_ref[0, :, KV_W:2 * KV_W].astype(BF16)
    i1 = lax.broadcasted_iota(jnp.int32, (1, wb), 1)
    d1 = t_row + wb - i1
    valid1 = (d1 >= 0) & (d1 < WINDOW) & (past_len - wb + i1 >= 0)
    i2 = lax.broadcasted_iota(jnp.int32, (1, nw_ref.shape[1]), 1)
    d2 = t_row - i2
    valid2 = (d2 >= 0) & (d2 < WINDOW) & (i2 < n_tok)
    s1 = jnp.where(valid1, _dot_nt(q, kw), NEG)
    s2 = jnp.where(valid2, _dot_nt(q, kn), NEG)
    m = jnp.maximum(jnp.max(s1, axis=1, keepdims=True), jnp.max(s2, axis=1, keepdims=True))
    p1 = jnp.exp2(s1 - m) * valid1.astype(F32)
    p2 = jnp.exp2(s2 - m) * valid2.astype(F32)
    den = jnp.maximum(jnp.sum(p1, axis=1, keepdims=True) + jnp.sum(p2, axis=1, keepdims=True), 1e-30)
    o_w = (_dot(p1.astype(BF16), vw) + _dot(p2.astype(BF16), vn)) / den
    g = g_ref[0]
    ocw_ref[0] = g[:, 0:1] * o_c + g[:, 2:3] * o_w


def _nsa_sample_a(q_rows, g_rows, kc, vc, win_buf, new_win, *, past_len, n_tok):
    nb, rows, _ = q_rows.shape
    n_sel = -(-(past_len + n_tok) // SEL_BLOCK)
    n_sel_pad = -(-n_sel // LANES) * LANES
    blk3 = lambda a: pl.BlockSpec((1,) + a.shape[1:], lambda b: (b, 0, 0))
    return pl.pallas_call(
        functools.partial(_nsa_sample_a_kernel, past_len=past_len, n_tok=n_tok, n_sel=n_sel, n_sel_pad=n_sel_pad),
        grid=(nb,),
        in_specs=[blk3(q_rows), blk3(g_rows), blk3(kc), blk3(vc), blk3(win_buf), blk3(new_win)],
        out_specs=[pl.BlockSpec((1, rows, KV_W), lambda b: (b, 0, 0)),
                   pl.BlockSpec((1, NSA_KV_HEADS * n_tok, n_sel_pad), lambda b: (b, 0, 0))],
        out_shape=[jax.ShapeDtypeStruct((nb, rows, KV_W), F32),
                   jax.ShapeDtypeStruct((nb, NSA_KV_HEADS * n_tok, n_sel_pad), F32)],
        compiler_params=_cparams(("arbitrary",)),
        name="nsa_sample_a",
    )(q_rows, g_rows, kc, vc, win_buf, new_win)


def _nsa_sample_b_kernel(pt_ref, cache_ref, q_ref, g_ref, sel_ref, seln_ref, ns_ref, ocw_ref, ex_ref, o_ref,
                         buf, sem, m_scr, l_scr, acc_scr, *, n_pages, n_steps, pps, n_tok):
    s = pl.program_id(1)
    slot = _stream_pages(pt_ref, cache_ref, buf, sem, n_pages, n_steps, pps, 2 * KV_W, False, True)
    q = q_ref[0]
    rows = q.shape[0]

    @pl.when(s == 0)
    def _():
        m_scr[...] = jnp.full(m_scr.shape, NEG, F32)
        l_scr[...] = jnp.zeros(l_scr.shape, F32)
        acc_scr[...] = jnp.zeros(acc_scr.shape, F32)

    def update(scores, msk, times_v):
        sc = jnp.where(msk, scores, NEG)
        m_old = m_scr[...]
        m_new = jnp.maximum(m_old, jnp.max(sc, axis=1, keepdims=True))
        p = jnp.exp2(sc - m_new) * msk.astype(F32)
        alpha = jnp.exp2(m_old - m_new)
        l_scr[...] = alpha * l_scr[...] + jnp.sum(p, axis=1, keepdims=True)
        acc_scr[...] = alpha * acc_scr[...] + times_v(p.astype(BF16))
        m_scr[...] = m_new

    chosen = _dot(sel_ref[0, 0], ex_ref[...]) > 0.5
    update(_dot(q, buf[slot, 0].astype(BF16)), chosen, lambda p: _dot_nt(p, buf[slot, 1].astype(BF16)))

    @pl.when(s == n_steps - 1)
    def _():
        t_row = lax.broadcasted_iota(jnp.int32, (rows, 1), 0) & (n_tok - 1)
        i2 = lax.broadcasted_iota(jnp.int32, (1, ns_ref.shape[1]), 1)
        msk = (seln_ref[0, 0][:, 0:1] > 0.5) & (i2 <= t_row) & (i2 < n_tok)
        update(_dot_nt(q, ns_ref[0, :, 0:KV_W].astype(BF16)), msk,
               lambda p: _dot(p, ns_ref[0, :, KV_W:2 * KV_W].astype(BF16)))
        o_s = acc_scr[...] / jnp.maximum(l_scr[...], 1e-30)
        o_ref[0] = ocw_ref[0] + g_ref[0][:, 1:2] * o_s


def _nsa_sample_b(page_table, cache, q_rows, g_rows, sel_steps, new_sel, ocw, *, n_tok, pps=PAGES_PER_STEP):
    nb, n_pages = page_table.shape
    n_steps = n_pages // pps
    rows = q_rows.shape[1]
    keys = pps * PAGE_SIZE
    expand = jnp.asarray(np.arange(LANES)[:, None] == (np.arange(keys)[None, :] // SEL_BLOCK), BF16)
    per_b = lambda a: pl.BlockSpec((1,) + a.shape[1:], lambda b, s, pt: (b, 0, 0))
    return pl.pallas_call(
        functools.partial(_nsa_sample_b_kernel, n_pages=n_pages, n_steps=n_steps, pps=pps, n_tok=n_tok),
        grid_spec=pltpu.PrefetchScalarGridSpec(
            num_scalar_prefetch=1,
            grid=(nb, n_steps),
            in_specs=[pl.BlockSpec(memory_space=pl.ANY),
                      per_b(q_rows), per_b(g_rows),
                      pl.BlockSpec((1, 1, rows, LANES), lambda b, s, pt: (b, s, 0, 0)),
                      pl.BlockSpec((1, 1, rows, LANES), lambda b, s, pt: (b, n_steps, 0, 0)),
                      per_b(new_sel), per_b(ocw),
                      pl.BlockSpec((LANES, keys), lambda b, s, pt: (0, 0))],
            out_specs=pl.BlockSpec((1, rows, KV_W), lambda b, s, pt: (b, 0, 0)),
            scratch_shapes=[pltpu.VMEM((2, 2, KV_W, keys), F32), pltpu.SemaphoreType.DMA((2,)),
                            pltpu.VMEM((rows, 1), F32), pltpu.VMEM((rows, 1), F32),
                            pltpu.VMEM((rows, KV_W), F32)]),
        out_shape=jax.ShapeDtypeStruct((nb, rows, KV_W), F32),
        compiler_params=_cparams(("arbitrary", "arbitrary")),
        name="nsa_sample_b",
    )(page_table, cache, q_rows, g_rows, sel_steps, sel_steps, new_sel, ocw, expand)


def _nsa_sample(page_table, cache, win_buf, cw, qpad, gates, kv4, kvw, pps=PAGES_PER_STEP):
    nb, ts, _ = qpad.shape
    past_len = page_table.shape[1] * PAGE_SIZE
    kc, vc = _compress(page_table, cache, *cw, pps=pps, transposed=True)
    rows = NSA_HEADS * ts
    q_rows = qpad.reshape(nb, ts, NSA_HEADS, LANES).transpose(0, 2, 1, 3).reshape(nb, rows, LANES)
    g_rows = gates[:, :, :3 * NSA_HEADS].reshape(nb, ts, NSA_HEADS, 3).transpose(0, 2, 1, 3)
    g_rows = jnp.pad(g_rows.reshape(nb, rows, 3), ((0, 0), (0, 0), (0, LANES - 3)))
    pad_rows = lambda a: jnp.pad(a, ((0, 0), (0, LANES - ts), (0, 0)))
    new_win = pad_rows(kvw)
    new_sel = pad_rows(kv4[:, :, 2 * KV_W:])
    ocw, sel = _nsa_sample_a(q_rows, g_rows, kc, vc, win_buf, new_win, past_len=past_len, n_tok=ts)
    n_steps = page_table.shape[1] // pps
    blk_per_step = pps * PAGE_SIZE // SEL_BLOCK
    n_past_blk = n_steps * blk_per_step
    sel_past = sel[:, :, :n_past_blk].reshape(nb, NSA_KV_HEADS, 1, ts, n_steps, blk_per_step)
    sel_past = jnp.broadcast_to(sel_past, (nb, NSA_KV_HEADS, NSA_GROUP, ts, n_steps, blk_per_step))
    sel_past = sel_past.transpose(0, 4, 1, 2, 3, 5).reshape(nb, n_steps, rows, blk_per_step)
    sel_past = jnp.pad(sel_past, ((0, 0), (0, 0), (0, 0), (0, LANES - blk_per_step)))
    sel_new = jnp.pad(sel[:, :, n_past_blk:], ((0, 0), (0, 0), (0, LANES)))[:, :, :LANES]
    sel_new = sel_new.reshape(nb, NSA_KV_HEADS, 1, ts, LANES)
    sel_new = jnp.broadcast_to(sel_new, (nb, NSA_KV_HEADS, NSA_GROUP, ts, LANES)).reshape(nb, 1, rows, LANES)
    sel_steps = jnp.concatenate([sel_past, sel_new], axis=1).astype(BF16)
    o_rows = _nsa_sample_b(page_table, cache, q_rows, g_rows, sel_steps, new_sel, ocw, n_tok=ts, pps=pps)
    return o_rows.reshape(nb, NSA_HEADS, ts, LANES).transpose(0, 2, 1, 3).reshape(nb, ts, QPAD_W).astype(BF16)


def _outproj_kernel(x_ref, hg_ref, nsa_ref, g1_ref, sc2_ref, sh2_ref, fn_ref, wo1_ref, wo2_ref,
                    x1_ref, h2_ref, *, tm):
    mix =_dot(hg_ref[0], wo1_ref[...]) + _dot(nsa_ref[0], wo2_ref[...])
    x1 = x_ref[0] + g1_ref[0] * mix
    x1_ref[0] = x1
    h2 = _rms(x1, fn_ref[...]) * (1.0 + sc2_ref[0]) + sh2_ref[0]
    _store_tok_tiles(h2_ref, h2, tm)


def _outproj(x, hg_out, nsa, gate1, scale2, shift2, ffn_norm, wo_hg, wo_nsa, tm):
    nb, t, _ = x.shape
    nt = t // tm
    mod_spec = _mod_spec(gate1, tm)
    tile = lambda w: pl.BlockSpec((1, tm, w), lambda b, i: (b, i, 0))
    return pl.pallas_call(
        functools.partial(_outproj_kernel, tm=tm),
        grid=(nb, nt),
        in_specs=[tile(D_MODEL), tile(HG_WIDTH), tile(QPAD_W), mod_spec, mod_spec, mod_spec,
                  pl.BlockSpec((1, D_MODEL), lambda b, i: (0, 0)),
                  pl.BlockSpec((HG_WIDTH, D_MODEL), lambda b, i: (0, 0)),
                  pl.BlockSpec((QPAD_W, D_MODEL), lambda b, i: (0, 0))],
        out_specs=[tile(D_MODEL), pl.BlockSpec((tm * TOK_ROWS, LANES), lambda b, i: (b * nt + i, 0))],
        out_shape=[jax.ShapeDtypeStruct((nb, t, D_MODEL), F32),
                   jax.ShapeDtypeStruct((nb * t * TOK_ROWS, LANES), F32)],
        compiler_params=_cparams(("arbitrary", "arbitrary")),
        name="out_proj",
    )(x, hg_out, nsa, gate1, scale2, shift2, ffn_norm.reshape(1, -1), wo_hg, wo_nsa)


def _split_w_out(w_out):
    wo_hg = w_out[:HG_WIDTH].astype(BF16)
    wn = w_out[HG_WIDTH:].reshape(NSA_HEADS, NSA_HEAD_DIM, D_MODEL)
    z = jnp.zeros_like(wn)
    grp = (jnp.arange(NSA_HEADS) // NSA_GROUP)[:, None, None]
    wn_pad = jnp.where(grp == 0, jnp.concatenate([wn, z], axis=1), jnp.concatenate([z, wn], axis=1))
    return wo_hg, wn_pad.reshape(QPAD_W, D_MODEL).astype(BF16)


def _router_kernel(h_ref, wr_ref, b_ref, e_ref, w_ref, r_ref, cnt_ref, run_scr, *, tm):
    @pl.when(pl.program_id(0) == 0)
    def _():
        run_scr[...] = jnp.zeros(run_scr.shape, F32)

    x = _load_tok_tiles(h_ref, tm).astype(BF16)
    scores = jax.nn.sigmoid(_dot(x, wr_ref[...]))
    biased = scores + b_ref[...]
    lane_i = lax.broadcasted_iota(jnp.int32, (tm, N_EXPERTS), 1)
    lane = lane_i.astype(F32)
    grp_of_lane = lax.shift_right_logical(lane_i, GROUP_SHIFT)

    gcol = lax.broadcasted_iota(jnp.int32, (tm, LANES), 1)
    gs = jnp.full((tm, LANES), 2.0 * NEG, F32)
    for g in range(N_GROUPS):
        mg = jnp.where(grp_of_lane == g, biased, NEG)
        m1 = jnp.max(mg, axis=1, keepdims=True)
        i1 = jnp.min(jnp.where(mg == m1, lane, 1e9), axis=1, keepdims=True)
        m2 = jnp.max(jnp.where(lane == i1, NEG, mg), axis=1, keepdims=True)
        gs = jnp.where(gcol == g, m1 + m2, gs)
    gsel = _topk_mask(gs, TOPK_GROUPS).astype(BF16)
    spread = (lax.broadcasted_iota(jnp.int32, (LANES, N_EXPERTS), 0)
              == lax.shift_right_logical(lax.broadcasted_iota(jnp.int32, (LANES, N_EXPERTS), 1), GROUP_SHIFT)
              ).astype(BF16)
    v = jnp.where(_dot(gsel, spread) > 0.5, biased, NEG)

    onehot = jnp.zeros((tm, N_EXPERTS), F32)
    idxs, wts = [], []
    wsum = jnp.zeros((tm, 1), F32)
    for _ in range(TOP_K):
        m = jnp.max(v, axis=1, keepdims=True)
        idx = jnp.min(jnp.where(v == m, lane, 1e9), axis=1, keepdims=True)
        pick = lane == idx
        wk = jnp.sum(jnp.where(pick, scores, 0.0), axis=1, keepdims=True)
        onehot = jnp.where(pick, 1.0, onehot)
        v = jnp.where(pick, 3.0 * NEG, v)
        idxs.append(idx)
        wts.append(wk)
        wsum = wsum + wk

    earlier = (lax.broadcasted_iota(jnp.int32, (tm, tm), 0) > lax.broadcasted_iota(jnp.int32, (tm, tm), 1))
    before = _dot(earlier.astype(BF16), onehot.astype(BF16)) + run_scr[...]
    e_out = jnp.zeros((tm, LANES), jnp.int32)
    r_out = jnp.zeros((tm, LANES), jnp.int32)
    w_out = jnp.zeros((tm, LANES), F32)
    for k in range(TOP_K):
        rk = jnp.sum(jnp.where(lane == idxs[k], before, 0.0), axis=1, keepdims=True)
        e_out = jnp.where(gcol == k, idxs[k].astype(jnp.int32), e_out)
        r_out = jnp.where(gcol == k, rk.astype(jnp.int32), r_out)
        w_out = jnp.where(gcol == k, wts[k] / wsum * ROUTED_SCALE, w_out)
    e_ref[...] = e_out
    r_ref[...] = r_out
    w_ref[...] = w_out
    run_scr[...] = run_scr[...] + jnp.sum(onehot, axis=0, keepdims=True)
    cnt_ref[...] = run_scr[...]


def _router(h2, w_router, bias, n_tok, tm=ROUTER_TM):
    tile = pl.BlockSpec((tm, LANES), lambda i: (i, 0))
    return pl.pallas_call(
        functools.partial(_router_kernel, tm=tm),
        grid=(n_tok // tm,),
        in_specs=[pl.BlockSpec((tm * TOK_ROWS, LANES), lambda i: (i, 0)),
                  pl.BlockSpec((D_MODEL, N_EXPERTS), lambda i: (0, 0)),
                  pl.BlockSpec((1, N_EXPERTS), lambda i: (0, 0))],
        out_specs=[tile, tile, tile, pl.BlockSpec((1, N_EXPERTS), lambda i: (0, 0))],
        out_shape=[jax.ShapeDtypeStruct((n_tok, LANES), jnp.int32),
                   jax.ShapeDtypeStruct((n_tok, LANES), F32),
                   jax.ShapeDtypeStruct((n_tok, LANES), jnp.int32),
                   jax.ShapeDtypeStruct((1, N_EXPERTS), F32)],
        scratch_shapes=[pltpu.VMEM((1, N_EXPERTS), F32)],
        compiler_params=_cparams(("arbitrary",)),
        name="moe_router",
    )(h2, w_router, bias)


def _dest_kernel(e_ref, r_ref, st_ref, d_ref):
    e = e_ref[...]
    tm = e.shape[0]
    lane = lax.broadcasted_iota(jnp.int32, (tm, N_EXPERTS), 1)
    col = lax.broadcasted_iota(jnp.int32, (tm, LANES), 1)
    st = st_ref[...]
    out = r_ref[...]
    for k in range(TOP_K):
        sk = jnp.sum(jnp.where(lane == e[:, k:k + 1], st, 0.0), axis=1, keepdims=True)
        out = jnp.where(col == k, out + sk.astype(jnp.int32), out)
    d_ref[...] = out


def _moe_dest(top_e, rank, starts, tm=ROUTER_TM):
    n_tok = top_e.shape[0]
    tile = pl.BlockSpec((tm, LANES), lambda i: (i, 0))
    return pl.pallas_call(
        _dest_kernel,
        grid=(n_tok // tm,),
        in_specs=[tile, tile, pl.BlockSpec((1, N_EXPERTS), lambda i: (0, 0))],
        out_specs=tile,
        out_shape=jax.ShapeDtypeStruct((n_tok, LANES), jnp.int32),
        compiler_params=_cparams(("arbitrary",)),
        name="moe_dest",
    )(top_e, rank, starts.astype(F32).reshape(1, -1))


def _moe_layout(counts, n_pairs):
    padded = (counts + MOE_BM - 1) // MOE_BM * MOE_BM
    pad_end = jnp.cumsum(padded)
    starts = pad_end - padded
    n_blocks = (n_pairs + N_EXPERTS * (MOE_BM - 1)) // MOE_BM
    n_used = pad_end[-1] // MOE_BM
    blk = jnp.arange(n_blocks, dtype=jnp.int32)
    used = blk < n_used
    e_of = jnp.sum(pad_end[None, :] <= (jnp.minimum(blk, n_used - 1) * MOE_BM)[:, None], axis=1).astype(jnp.int32)
    e_of = jnp.minimum(e_of, N_EXPERTS - 1)
    shifted = jnp.concatenate([jnp.full((1,), -1, jnp.int32), e_of[:-1]])
    fresh = (used & (e_of != shifted)).astype(jnp.int32)
    w_slot = ((jnp.cumsum(fresh) - 1) % 2).astype(jnp.int32)
    ids = jnp.arange(N_EXPERTS, dtype=jnp.int32)
    later = jnp.where(counts > 0, ids, N_EXPERTS)
    next_nonempty = jnp.concatenate([lax.cummin(later, reverse=True)[1:], jnp.full((1,), N_EXPERTS, jnp.int32)])
    next_e = next_nonempty[e_of].astype(jnp.int32)
    pad_table = jnp.concatenate([starts + counts, padded - counts, n_used[None]]).astype(jnp.int32)
    return (starts.astype(jnp.int32), pad_table,
            (e_of, fresh, used.astype(jnp.int32), w_slot, next_e), n_blocks)


def _dispatch_kernel(pad_ref, dest_ref, h_ref, xs_ref, sem, zeros, zsem, *, tm, n_blocks, bm):
    i = pl.program_id(0)
    n_steps = pl.num_programs(0)

    def row_copy(src_tok, dst_row):
        return pltpu.make_async_copy(
            h_ref.at[pl.ds(pl.multiple_of(src_tok * TOK_ROWS, TOK_ROWS), TOK_ROWS), :],
            xs_ref.at[pl.ds(pl.multiple_of(dst_row * TOK_ROWS, TOK_ROWS), TOK_ROWS), :], sem)

    def issue(t, carry):
        for k in range(TOP_K):
            row_copy(t, dest_ref[0, 0, t * TOP_K + k]).start(priority=k % 2)
        return carry

    def drain(t, carry):
        for _ in range(TOP_K):
            row_copy(0, 0).wait()
        return carry

    @pl.when(i == 0)
    def _():
        zeros[...] = jnp.zeros(zeros.shape, F32)

    def zero_fill(row0, n_rows_static):
        return pltpu.make_async_copy(
            zeros.at[pl.ds(0, n_rows_static * TOK_ROWS), :],
            xs_ref.at[pl.ds(pl.multiple_of(row0 * TOK_ROWS, TOK_ROWS), n_rows_static * TOK_ROWS), :], zsem)

    experts_per_step = -(-N_EXPERTS // n_steps)
    blocks_per_step = -(-n_blocks // n_steps)
    n_used = pad_ref[2 * N_EXPERTS]

    def fill_pass(act):
        def one_expert(j, carry):
            e = jnp.minimum(i * experts_per_step + j, N_EXPERTS - 1)
            live = i * experts_per_step + j < N_EXPERTS
            lo, length = pad_ref[e], pad_ref[N_EXPERTS + e]
            piece = bm // 2
            while piece >= 1:
                @pl.when(live & ((length & piece) != 0))
                def _(piece=piece):
                    act(zero_fill(lo + (length & ~(2 * piece - 1)), piece))
                piece //= 2
            return carry
        lax.fori_loop(0, experts_per_step, one_expert, 0)

        def one_block(j, carry):
            blk = n_used + i * blocks_per_step + j

            @pl.when(blk < n_blocks)
            def _():
                act(zero_fill(blk * bm, bm))
            return carry
        lax.fori_loop(0, blocks_per_step, one_block, 0)

    fill_pass(lambda cp: cp.start())
    lax.fori_loop(0, tm, issue, 0)
    lax.fori_loop(0, tm, drain, 0)
    fill_pass(lambda cp: cp.wait())


def _dispatch(dest_tiles, h2, pad_table, n_blocks, tm=ROUTER_TM, bm=MOE_BM):
    n_tiles = dest_tiles.shape[0]
    return pl.pallas_call(
        functools.partial(_dispatch_kernel, tm=tm, n_blocks=n_blocks, bm=bm),
        grid_spec=pltpu.PrefetchScalarGridSpec(
            num_scalar_prefetch=1,
            grid=(n_tiles,),
            in_specs=[pl.BlockSpec((1, 1, tm * TOP_K), lambda i, pad: (i, 0, 0), memory_space=pltpu.SMEM),
                      pl.BlockSpec((tm * TOK_ROWS, LANES), lambda i, pad: (i, 0))],
            out_specs=pl.BlockSpec(memory_space=pl.ANY),
            scratch_shapes=[pltpu.SemaphoreType.DMA(()), pltpu.VMEM((bm * TOK_ROWS, LANES), F32),
                            pltpu.SemaphoreType.DMA(())]),
        out_shape=jax.ShapeDtypeStruct((n_blocks * bm * TOK_ROWS, LANES), F32),
        compiler_params=_cparams(("arbitrary",)),
        name="moe_dispatch",
    )(pad_table, dest_tiles, h2)


GMM_SLOTS = 4


def _gmm_kernel(e_ref, fresh_ref, used_ref, wslot_ref, next_ref,
                xs_ref, wg_ref, wu_ref, wd_ref, ys_ref, wg_bf, wu_bf, wd_bf, xbuf, sem,
                wg_st, wu_st, wd_st, wsem, *, bm):
    i = pl.program_id(0)
    n = pl.num_programs(0)

    def fetch(j):
        rows = pl.ds(pl.multiple_of(j * (bm * TOK_ROWS), bm * TOK_ROWS), bm * TOK_ROWS)
        return pltpu.make_async_copy(xs_ref.at[rows, :], xbuf.at[j % GMM_SLOTS], sem.at[j % GMM_SLOTS])

    def weight_copies(e, slot):
        return [pltpu.make_async_copy(src.at[e], dst.at[slot], wsem.at[slot])
                for src, dst in ((wg_ref, wg_st), (wu_ref, wu_st), (wd_ref, wd_st))]

    @pl.when(i == 0)
    def _():
        for cp in weight_copies(e_ref[0], wslot_ref[0]):
            cp.start()
        for j in range(GMM_SLOTS - 1):
            @pl.when((j < n) & (used_ref[jnp.minimum(j, n - 1)] == 1))
            def _():
                fetch(j).start()

    ahead = jnp.minimum(i + GMM_SLOTS - 1, n - 1)

    @pl.when((i + GMM_SLOTS - 1 < n) & (used_ref[ahead] == 1))
    def _():
        fetch(ahead).start()

    @pl.when(fresh_ref[i] == 1)
    def _():
        slot = wslot_ref[i]
        for cp in weight_copies(e_ref[i], slot):
            cp.wait()
        wg_bf[...] = wg_st[slot].astype(BF16)
        wu_bf[...] = wu_st[slot].astype(BF16)
        wd_bf[...] = wd_st[slot].astype(BF16)

        @pl.when(next_ref[i] < N_EXPERTS)
        def _():
            for cp in weight_copies(next_ref[i], 1 - slot):
                cp.start()

    @pl.when(used_ref[i] == 1)
    def _():
        fetch(i).wait()
        x = _load_tok_tiles(xbuf.at[i % GMM_SLOTS], bm).astype(BF16)
        hid = (_silu(_dot(x, wg_bf[...])) * _dot(x, wu_bf[...])).astype(BF16)
        _store_tok_tiles(ys_ref, _dot(hid, wd_bf[...]), bm)

    @pl.when(used_ref[i] == 0)
    def _():
        ys_ref[...] = jnp.zeros(ys_ref.shape, F32)


def _moe_gmm(blocks, xs_sorted, w_gate, w_up, w_down, bm=MOE_BM):
    rows = pl.BlockSpec((bm * TOK_ROWS, LANES), lambda i, *_: (i, 0))
    in_hbm = pl.BlockSpec(memory_space=pl.ANY)
    weights = (w_gate, w_up, w_down)
    return pl.pallas_call(
        functools.partial(_gmm_kernel, bm=bm),
        grid_spec=pltpu.PrefetchScalarGridSpec(
            num_scalar_prefetch=len(blocks),
            grid=(blocks[0].shape[0],),
            in_specs=[in_hbm, in_hbm, in_hbm, in_hbm],
            out_specs=rows,
            scratch_shapes=[pltpu.VMEM(w.shape[1:], BF16) for w in weights]
            + [pltpu.VMEM((GMM_SLOTS, bm * TOK_ROWS, LANES), F32), pltpu.SemaphoreType.DMA((GMM_SLOTS,))]
            + [pltpu.VMEM((2,) + w.shape[1:], F32) for w in weights]
            + [pltpu.SemaphoreType.DMA((2,))]),
        out_shape=jax.ShapeDtypeStruct(xs_sorted.shape, F32),
        compiler_params=_cparams(("arbitrary",)),
        name="moe_experts",
    )(*blocks, xs_sorted, *weights)


def _combine_kernel(dest_ref, dnext_ref, w_ref, x1_ref, h_ref, g2_ref, wsg_ref, wsu_ref, wsd_ref, fn_ref, ys_ref,
                    o_ref, gbuf, sem, routed_scr, *, tm):
    n = pl.program_id(0) * pl.num_programs(1) + pl.program_id(1)
    total = pl.num_programs(0) * pl.num_programs(1)
    slot = n % 2

    def row_copy(src_row, p, sl):
        return pltpu.make_async_copy(
            ys_ref.at[pl.ds(pl.multiple_of(src_row * TOK_ROWS, TOK_ROWS), TOK_ROWS), :],
            gbuf.at[sl, pl.ds(pl.multiple_of(p * TOK_ROWS, TOK_ROWS), TOK_ROWS), :], sem.at[sl])

    def gather(rows_ref, sl):
        def issue(i, carry):
            for k in range(TOP_K):
                p = i * TOP_K + k
                row_copy(rows_ref[0, 0, p], p, sl).start(priority=k % 2)
            return carry
        lax.fori_loop(0, tm, issue, 0)

    def drain(i, carry):
        for _ in range(TOP_K):
            row_copy(0, 0, slot).wait()
        return carry

    @pl.when(n == 0)
    def _():
        gather(dest_ref, slot)

    @pl.when(n + 1 < total)
    def _():
        gather(dnext_ref, 1 - slot)

    h = _load_tok_tiles(h_ref, tm).astype(BF16)
    hid = (_silu(_dot(h, wsg_ref[...])) * _dot(h, wsu_ref[...])).astype(BF16)
    shared = _dot(hid, wsd_ref[...])
    lax.fori_loop(0, tm, drain, 0)

    def weigh(t, carry):
        acc = jnp.zeros((TOK_ROWS, LANES), F32)
        for k in range(TOP_K):
            p = t * TOP_K + k
            acc = acc + w_ref[0, 0, p] * gbuf[slot, pl.ds(pl.multiple_of(p * TOK_ROWS, TOK_ROWS), TOK_ROWS), :]
        routed_scr[pl.ds(pl.multiple_of(t * TOK_ROWS, TOK_ROWS), TOK_ROWS), :] = acc
        return carry

    lax.fori_loop(0, tm, weigh, 0)
    x2 = x1_ref[0] + g2_ref[0] * (_load_tok_tiles(routed_scr, tm) + shared)
    o_ref[0] = _rms(x2, fn_ref[...])


def _combine(dest_tiles, w_tiles, x1, h2, gate2, shared, fnorm, ys_sorted, tile0, tm=COMBINE_TM):
    nb, t, _ = x1.shape
    nt = t // tm
    flat = lambda b, i: tile0 + b * nt + i
    nxt = lambda b, i: tile0 + jnp.minimum(b * nt + i + 1, nb * nt - 1)
    mod_spec = _mod_spec(gate2, tm)
    const = lambda a: pl.BlockSpec(a.shape, lambda b, i: (0, 0))
    return pl.pallas_call(
        functools.partial(_combine_kernel, tm=tm),
        grid=(nb, nt),
        in_specs=[pl.BlockSpec((1, 1, tm * TOP_K), lambda b, i: (flat(b, i), 0, 0), memory_space=pltpu.SMEM),
                  pl.BlockSpec((1, 1, tm * TOP_K), lambda b, i: (nxt(b, i), 0, 0), memory_space=pltpu.SMEM),
                  pl.BlockSpec((1, 1, tm * TOP_K), lambda b, i: (flat(b, i), 0, 0), memory_space=pltpu.SMEM),
                  pl.BlockSpec((1, tm, D_MODEL), lambda b, i: (b, i, 0)),
                  pl.BlockSpec((tm * TOK_ROWS, LANES), lambda b, i: (flat(b, i), 0)),
                  mod_spec, const(shared[0]), const(shared[1]), const(shared[2]), const(fnorm),
                  pl.BlockSpec(memory_space=pl.ANY)],
        out_specs=pl.BlockSpec((1, tm, D_MODEL), lambda b, i: (b, i, 0)),
        out_shape=jax.ShapeDtypeStruct((nb, t, D_MODEL), F32),
        scratch_shapes=[pltpu.VMEM((2, tm * TOP_K * TOK_ROWS, LANES), F32), pltpu.SemaphoreType.DMA((2,)),
                        pltpu.VMEM((tm * TOK_ROWS, LANES), F32)],
        compiler_params=_cparams(("arbitrary", "arbitrary")),
        name="moe_combine",
    )(dest_tiles, dest_tiles, w_tiles, x1, h2, gate2, *shared, fnorm, ys_sorted)


def kernel(x_prompt, x_sample, c_prompt, c_sample, cache_nsa_kv, cache_win_kv, state_hgrn, page_table,
           attn_norm, ffn_norm, final_norm, hg_norm, w_ada, b_ada, w_in, hg_lb,
           cmp_pe, cmp_w1, cmp_b1, cmp_w2, w_out, w_router, router_bias,
           w_gate, w_up, w_down, ws_gate, ws_up, ws_down):
    nbp, t, _ = x_prompt.shape
    nbs, ts, _ = x_sample.shape
    ns = nbs * ts
    n_all = nbp * t + ns

    c_all = jnp.concatenate([c_prompt, c_sample], axis=0)
    c_all = jnp.pad(c_all, ((0, -c_all.shape[0] % SUBLANES), (0, 0)))
    mod = _ada(c_all, w_ada[0], b_ada[0])
    modp = mod[:nbp].reshape(nbp, 1, 6, D_MODEL)
    mods = jnp.repeat(mod[nbp:nbp + nbs].reshape(nbs, 1, 6, D_MODEL), ts, axis=1).reshape(1, ns, 6, D_MODEL)

    w_pad = _pad_w_in(w_in[0])
    cw = _compress_weights(cmp_pe[0], cmp_w1[0], cmp_b1[0], cmp_w2[0])
    wo_hg, wo_nsa = _split_w_out(w_out[0])

    hg, qpad, kv4, kvw, gates, kvsel, kvwb = _inproj(
        x_prompt, modp[:, :, 1], modp[:, :, 0], attn_norm[0], w_pad, PROJ_TM)
    hg_out, hg_state_p = _hgrn(hg, hg_lb, jnp.zeros((nbp, HG_HEADS, HG_DK, HG_DK), F32), hg_norm[0],
                               HGRN_TC, HG_CHUNK)
    n_pages_p = t // PAGE_SIZE
    ptp = jnp.arange(nbp * n_pages_p, dtype=jnp.int32).reshape(nbp, n_pages_p)
    kc, vc = _compress(ptp, kv4.reshape(nbp * n_pages_p, PAGE_SIZE, 4 * KV_W), *cw)
    nsa = _nsa_prompt(qpad, gates, kc, vc, kvsel, kvwb)
    x1p, h2p = _outproj(x_prompt, hg_out, nsa, modp[:, :, 2], modp[:, :, 4], modp[:, :, 3],
                        ffn_norm[0], wo_hg, wo_nsa, PROJ_TM)

    xs = x_sample.reshape(1, ns, D_MODEL)
    hg_s, qpad_s, kv4_s, kvw_s, gates_s, _, _ = _inproj(
        xs, mods[:, :, 1], mods[:, :, 0], attn_norm[0], w_pad, ns)
    hg_out_s, hg_state_s = _hgrn(hg_s.reshape(nbs, ts, 4 * HG_WIDTH), hg_lb, state_hgrn[0], hg_norm[0], ts, ts)
    cache = cache_nsa_kv[0].transpose(0, 2, 3, 4, 1).reshape(-1, 4 * KV_W, PAGE_SIZE)
    win_buf = cache_win_kv[0].reshape(nbs, -1, 2 * KV_W)
    nsa_s = _nsa_sample(page_table, cache, win_buf, cw, qpad_s.reshape(nbs, ts, QPAD_W),
                        gates_s.reshape(nbs, ts, GZ_PAD), kv4_s.reshape(nbs, ts, 4 * KV_W),
                        kvw_s.reshape(nbs, ts, 2 * KV_W)).reshape(1, ns, QPAD_W)
    x1s, h2s = _outproj(xs, hg_out_s.reshape(1, ns, HG_WIDTH), nsa_s, mods[:, :, 2], mods[:, :, 4], mods[:, :, 3],
                        ffn_norm[0], wo_hg, wo_nsa, ns)

    h2 = jnp.concatenate([h2p, h2s], axis=0)
    top_e, top_w, rank, counts = _router(h2, w_router[0].astype(BF16), router_bias[0].reshape(1, -1), n_all)
    starts, pad_table, blocks, n_blocks = _moe_layout(counts[0].astype(jnp.int32), n_all * TOP_K)
    dest = _moe_dest(top_e, rank, starts)[:, :TOP_K].reshape(-1)
    xs_sorted = _dispatch(dest.reshape(n_all // ROUTER_TM, 1, ROUTER_TM * TOP_K), h2, pad_table, n_blocks)
    ys_sorted = _moe_gmm(blocks, xs_sorted, w_gate[0], w_up[0], w_down[0])
    dest_c = dest.reshape(n_all // COMBINE_TM, 1, COMBINE_TM * TOP_K)
    w_c = top_w[:, :TOP_K].reshape(n_all // COMBINE_TM, 1, COMBINE_TM * TOP_K)
    shared = (ws_gate[0].astype(BF16), ws_up[0].astype(BF16), ws_down[0].astype(BF16))
    fnorm = final_norm.reshape(1, -1)
    y_prompt = _combine(dest_c, w_c, x1p, h2, modp[:, :, 5], shared, fnorm, ys_sorted, 0)
    y_sample = _combine(dest_c, w_c, x1s, h2, mods[:, :, 5], shared, fnorm, ys_sorted, nbp * t // COMBINE_TM)

    wb = win_buf.shape[1]
    win_p = kvw[:, t - min(WINDOW, t):]
    win_s = jnp.concatenate([win_buf, kvw_s.reshape(nbs, ts, 2 * KV_W)], axis=1)[:, -wb:]
    kv_shape = (4, NSA_KV_HEADS, NSA_HEAD_DIM)
    win_shape = (2, NSA_KV_HEADS, NSA_HEAD_DIM)
    return (y_prompt,
            y_sample.reshape(nbs, ts, D_MODEL),
            kv4.reshape(1, nbp, t, *kv_shape),
            win_p.reshape(1, nbp, -1, *win_shape),
            hg_state_p[None],
            kv4_s.reshape(1, nbs, ts, *kv_shape),
            win_s.reshape(1, nbs, wb, *win_shape),
            hg_state_s[None])
```

```python
import functools

import jax
import jax.numpy as jnp
import numpy as np
from jax import lax
from jax.experimental import pallas as pl
from jax.experimental.pallas import tpu as pltpu

F32 = jnp.float32
BF16 = jnp.bfloat16

D_MODEL = 1024
HG_WIDTH = 512
HG_HEADS = 4
HG_DK = 128
HG_CHUNK = 32
NSA_WIDTH = 512
NSA_HEADS = 8
NSA_HEAD_DIM = 64
NSA_KV_HEADS = 2
NSA_GROUP = 4
KV_W = 128
CMP_BLOCK = 32
CMP_STRIDE = 16
CMP_HIDDEN = 256
SEL_BLOCK = 64
SEL_TOPN = 16
WINDOW = 512
Q_BLOCK = 256
N_EXPERTS = 256
TOP_K = 8
N_GROUPS = 8
TOPK_GROUPS = 4
ROUTED_SCALE = 2.5
RMS_EPS = 1e-6
PAGE_SIZE = 128

LANES = 128
SUBLANES = 8
TOK_ROWS = D_MODEL // LANES
VMEM_LIMIT = 56 * 1024 * 1024

QPAD_W = NSA_HEADS * LANES
GZ_PAD = LANES
INP_COLS = 4 * HG_WIDTH + QPAD_W + 4 * KV_W + 2 * KV_W + GZ_PAD

SEL_SHIFT = SEL_BLOCK.bit_length() - 1
GROUP_SHIFT = (N_EXPERTS // N_GROUPS).bit_length() - 1
NEG = -1e30
PROJ_TM = 512
HGRN_TC = 256
PAGES_PER_STEP = 32
MOE_BM = 384
ROUTER_TM = 384
COMBINE_TM = 128


def _cparams(sem):
    return pltpu.CompilerParams(dimension_semantics=sem, vmem_limit_bytes=VMEM_LIMIT)


def _dot(a, b):
    return jnp.dot(a, b, preferred_element_type=F32)


def _dot_nt(a, b):
    return lax.dot_general(a, b, (((1,), (1,)), ((), ())), preferred_element_type=F32)


def _dot_tn(a, b):
    return lax.dot_general(a, b, (((0,), (0,)), ((), ())), preferred_element_type=F32)


def _rms(x, g):
    return x * lax.rsqrt(jnp.mean(x * x, axis=-1, keepdims=True) + RMS_EPS) * g


def _silu(x):
    return x * jax.nn.sigmoid(x)


Q_SCALE = NSA_HEAD_DIM ** -0.5 * 1.4426950408889634


def _masked_softmax(s, valid):
    s = jnp.where(valid, s, NEG)
    m = jnp.max(s, axis=1, keepdims=True)
    p = jnp.exp2(s - m) * valid.astype(F32)
    return p / jnp.maximum(jnp.sum(p, axis=1, keepdims=True), 1e-30)


def _topk_mask(v, k):
    lane = lax.broadcasted_iota(jnp.int32, v.shape, 1).astype(F32)
    sel = jnp.zeros(v.shape, F32)
    for _ in range(k):
        m = jnp.max(v, axis=1, keepdims=True)
        idx = jnp.min(jnp.where(v == m, lane, 1e9), axis=1, keepdims=True)
        pick = lane == idx
        sel = jnp.where(pick, 1.0, sel)
        v = jnp.where(pick, 3.0 * NEG, v)
    return sel


def _mod_spec(mod, tm):
    if mod.shape[1] == 1:
        return pl.BlockSpec((1, 1, D_MODEL), lambda b, i: (b, 0, 0))
    return pl.BlockSpec((1, tm, D_MODEL), lambda b, i: (b, i, 0))


def _load_tok_tiles(ref, n_tok):
    return jnp.concatenate([ref[pl.ds(s, n_tok, stride=TOK_ROWS), :] for s in range(TOK_ROWS)], axis=1)


def _store_tok_tiles(ref, val, n_tok):
    for s in range(TOK_ROWS):
        ref[pl.ds(s, n_tok, stride=TOK_ROWS), :] = val[:, s * LANES:(s + 1) * LANES]


def _ada_kernel(c_ref, w_ref, b_ref, o_ref):
    s = _silu(c_ref[...]).astype(BF16)
    o_ref[...] = _dot(s, w_ref[...].astype(BF16)) + b_ref[...]


def _ada(c_all, w_ada, b_ada):
    n = c_all.shape[0]
    return pl.pallas_call(
        _ada_kernel,
        grid=(6,),
        in_specs=[pl.BlockSpec((n, D_MODEL), lambda j: (0, 0)),
                  pl.BlockSpec((D_MODEL, D_MODEL), lambda j: (0, j)),
                  pl.BlockSpec((1, D_MODEL), lambda j: (0, j))],
        out_specs=pl.BlockSpec((n, D_MODEL), lambda j: (0, j)),
        out_shape=jax.ShapeDtypeStruct((n, 6 * D_MODEL), F32),
        compiler_params=_cparams(("arbitrary",)),
        name="ada_mod",
    )(c_all, w_ada, b_ada.reshape(1, -1))


def _inproj_kernel(x_ref, sc_ref, sh_ref, g_ref, w_ref,
                   hg_ref, q_ref, kv4_ref, kvw_ref, gate_ref, kvsel_ref, kvwb_ref):
    h = _rms(x_ref[0], g_ref[...]) * (1.0 + sc_ref[0]) + sh_ref[0]
    z = _dot(h.astype(BF16), w_ref[...])
    c0 = 4 * HG_WIDTH
    hg_ref[0] = z[:, :c0]
    q_ref[0] = (z[:, c0:c0 + QPAD_W] * Q_SCALE).astype(BF16)
    c1 = c0 + QPAD_W
    kv4 = z[:, c1:c1 + 4 * KV_W]
    kv4_ref[0] = kv4
    kvsel_ref[0] = kv4[:, 2 * KV_W:].astype(BF16)
    c2 = c1 + 4 * KV_W
    kvw = z[:, c2:c2 + 2 * KV_W]
    kvw_ref[0] = kvw
    kvwb_ref[0] = kvw.astype(BF16)
    gate_ref[0] = jax.nn.sigmoid(z[:, c2 + 2 * KV_W:])


def _inproj(x, scale, shift, g_norm, w_pad, tm):
    nb, t, _ = x.shape
    mod_spec = _mod_spec(scale, tm)
    widths = [(4 * HG_WIDTH, F32), (QPAD_W, BF16), (4 * KV_W, F32), (2 * KV_W, F32), (GZ_PAD, F32),
              (2 * KV_W, BF16), (2 * KV_W, BF16)]
    return pl.pallas_call(
        _inproj_kernel,
        grid=(nb, t // tm),
        in_specs=[pl.BlockSpec((1, tm, D_MODEL), lambda b, i: (b, i, 0)),
                  mod_spec, mod_spec,
                  pl.BlockSpec((1, D_MODEL), lambda b, i: (0, 0)),
                  pl.BlockSpec((D_MODEL, INP_COLS), lambda b, i: (0, 0))],
        out_specs=[pl.BlockSpec((1, tm, w), lambda b, i: (b, i, 0)) for w, _ in widths],
        out_shape=[jax.ShapeDtypeStruct((nb, t, w), dt) for w, dt in widths],
        compiler_params=_cparams(("arbitrary", "arbitrary")),
        name="in_proj",
    )(x, scale, shift, g_norm.reshape(1, -1), w_pad)


def _pad_w_in(w_in):
    c0 = 4 * HG_WIDTH
    wq = w_in[:, c0:c0 + NSA_WIDTH].reshape(D_MODEL, NSA_HEADS, NSA_HEAD_DIM)
    zeros = jnp.zeros_like(wq)
    lo = jnp.concatenate([wq, zeros], axis=-1)
    hi = jnp.concatenate([zeros, wq], axis=-1)
    grp = (jnp.arange(NSA_HEADS) // NSA_GROUP)[None, :, None]
    wq_pad = jnp.where(grp == 0, lo, hi).reshape(D_MODEL, QPAD_W)
    c1 = c0 + NSA_WIDTH
    rest = w_in[:, c1:c1 + 6 * KV_W]
    gz = jnp.pad(w_in[:, c1 + 6 * KV_W:], ((0, 0), (0, GZ_PAD - 3 * NSA_HEADS)))
    return jnp.concatenate([w_in[:, :c0], wq_pad, rest, gz], axis=1).astype(BF16)


def _hgrn_kernel(q_ref, f_ref, v_ref, gt_ref, lb_ref, s0_ref, gn_ref, o_ref, s_out_ref, st_scr,
                 *, chunk, n_chunks):
    i = pl.program_id(1)

    @pl.when(i == 0)
    def _():
        for h in range(HG_HEADS):
            st_scr[h] = s0_ref[0, h].T

    lbr = lb_ref[...]
    e = jnp.exp(lbr - jnp.max(lbr, axis=0, keepdims=True))
    lb_all = e[0:1] / jnp.sum(e, axis=0, keepdims=True)
    row = lax.broadcasted_iota(jnp.int32, (chunk, HG_DK), 0)
    causal = (lax.broadcasted_iota(jnp.int32, (chunk, chunk), 0)
              >= lax.broadcasted_iota(jnp.int32, (chunk, chunk), 1))
    st = [st_scr[h] for h in range(HG_HEADS)]
    for c in range(n_chunks):
        sl = pl.ds(c * chunk, chunk)
        for h in range(HG_HEADS):
            hs = slice(h * HG_DK, (h + 1) * HG_DK)
            lb = lb_all[:, hs]
            z = f_ref[0, sl, hs]
            log_f = jnp.log(lb + (1.0 - lb) * jax.nn.sigmoid(z))
            kk = (1.0 - lb) * jax.nn.sigmoid(-z)
            a = log_f
            s = 1
            while s < chunk:
                a = a + jnp.where(row >= s, pltpu.roll(a, s, 0), 0.0)
                s *= 2
            qt = (q_ref[0, sl, hs] * jnp.exp(a)).astype(BF16)
            kt = (kk * jnp.exp(-a)).astype(BF16)
            v = v_ref[0, sl, hs].astype(BF16)
            att = jnp.where(causal, _dot_nt(qt, kt), 0.0)
            o = _dot(att.astype(BF16), v) + _dot_nt(qt, st[h].astype(BF16))
            a_end = a[chunk - 1:chunk, :]
            kd = (kk * jnp.exp(a_end - a)).astype(BF16)
            st[h] = st[h] * jnp.exp(a_end) + _dot_tn(v, kd)
            o = _rms(o, gn_ref[...]) * _silu(gt_ref[0, sl, hs])
            o_ref[0, sl, hs] = o.astype(o_ref.dtype)
    for h in range(HG_HEADS):
        st_scr[h] = st[h]

    @pl.when(i == pl.num_programs(1) - 1)
    def _():
        for h in range(HG_HEADS):
            s_out_ref[0, h] = st[h].T


def _hgrn(hg, hg_lb, s0, g_norm, tc, chunk):
    nb, t, _ = hg.shape
    part = lambda k: pl.BlockSpec((1, tc, HG_WIDTH), lambda b, i: (b, i, k))
    st_spec = pl.BlockSpec((1, HG_HEADS, HG_DK, HG_DK), lambda b, i: (b, 0, 0, 0))
    return pl.pallas_call(
        functools.partial(_hgrn_kernel, chunk=chunk, n_chunks=tc // chunk),
        grid=(nb, t // tc),
        in_specs=[part(0), part(1), part(2), part(3),
                  pl.BlockSpec(hg_lb.shape, lambda b, i: (0, 0)),
                  st_spec,
                  pl.BlockSpec((1, HG_DK), lambda b, i: (0, 0))],
        out_specs=[pl.BlockSpec((1, tc, HG_WIDTH), lambda b, i: (b, i, 0)), st_spec],
        out_shape=[jax.ShapeDtypeStruct((nb, t, HG_WIDTH), BF16),
                   jax.ShapeDtypeStruct((nb, HG_HEADS, HG_DK, HG_DK), F32)],
        scratch_shapes=[pltpu.VMEM((HG_HEADS, HG_DK, HG_DK), F32)],
        compiler_params=_cparams(("arbitrary", "arbitrary")),
        name="hgrn2",
    )(hg, hg, hg, hg, hg_lb, s0, g_norm.reshape(1, -1))


def _gelu_tanh(x):
    return 0.5 * x * (1.0 + jnp.tanh(0.7978845608028654 * (x + 0.044715 * x * x * x)))


def _page_copies(pt_ref, cache_ref, buf, sem, b, s, slot, n_pages, pps, col0, tail, transposed):
    copies = []
    base = s * pps
    nxt = pt_ref[b, jnp.minimum(base + pps, n_pages - 1)]
    for br in range(2):
        cols = pl.ds(col0 + br * KV_W, KV_W)
        for i in range(pps):
            pg = pt_ref[b, base + i]
            if transposed:
                copies.append(pltpu.make_async_copy(
                    cache_ref.at[pg, cols, :],
                    buf.at[slot, br, :, pl.ds(i * PAGE_SIZE, PAGE_SIZE)], sem.at[slot]))
            else:
                copies.append(pltpu.make_async_copy(
                    cache_ref.at[pg, :, cols],
                    buf.at[slot, br, pl.ds(i * PAGE_SIZE, PAGE_SIZE), :], sem.at[slot]))
        if tail and transposed:
            copies.append(pltpu.make_async_copy(
                cache_ref.at[nxt, cols, :],
                buf.at[slot, br, :, pl.ds(pps * PAGE_SIZE, PAGE_SIZE)], sem.at[slot]))
        elif tail:
            copies.append(pltpu.make_async_copy(
                cache_ref.at[nxt, pl.ds(0, CMP_STRIDE), cols],
                buf.at[slot, br, pl.ds(pps * PAGE_SIZE, CMP_STRIDE), :], sem.at[slot]))
    return copies


def _stream_pages(pt_ref, cache_ref, buf, sem, n_pages, n_steps, pps, col0, tail, transposed):
    b = pl.program_id(0)
    s = pl.program_id(1)
    n = b * n_steps + s
    total = pl.num_programs(0) * n_steps
    slot = n % 2
    args = (n_pages, pps, col0, tail, transposed)

    @pl.when(n == 0)
    def _():
        for cp in _page_copies(pt_ref, cache_ref, buf, sem, b, s, slot, *args):
            cp.start()

    @pl.when(n + 1 < total)
    def _():
        n1 = n + 1
        for cp in _page_copies(pt_ref, cache_ref, buf, sem, n1 // n_steps, n1 % n_steps, 1 - slot, *args):
            cp.start()

    for cp in _page_copies(pt_ref, cache_ref, buf, sem, b, s, slot, *args):
        cp.wait()
    return slot


def _compress_kernel(pt_ref, cache_ref, pe_ref, w1_ref, b1_ref, w2_ref, kc_ref, vc_ref, buf, sem, *rowbuf,
                     n_pages, n_steps, pps, transposed):
    groups = pps * PAGE_SIZE // CMP_STRIDE

    def compress(read_rows):
        low_half = lax.broadcasted_iota(jnp.int32, (groups, KV_W), 1) < NSA_HEAD_DIM
        for br, out_ref in ((0, kc_ref), (1, vc_ref)):
            acc0 = jnp.zeros((groups, CMP_HIDDEN), F32)
            acc1 = jnp.zeros((groups, CMP_HIDDEN), F32)
            for j in range(CMP_BLOCK // 2):
                a = read_rows(br, 2 * j) + pe_ref[br, 2 * j:2 * j + 1, :]
                b = pltpu.roll(read_rows(br, 2 * j + 1) + pe_ref[br, 2 * j + 1:2 * j + 2, :], NSA_HEAD_DIM, 1)
                acc0 = acc0 + _dot(jnp.where(low_half, a, b).astype(BF16), w1_ref[br, j, 0])
                acc1 = acc1 + _dot(jnp.where(low_half, b, a).astype(BF16), w1_ref[br, j, 1])
            hid = _gelu_tanh(jnp.concatenate([acc0, acc1], axis=1) + b1_ref[br]).astype(BF16)
            out_ref[0] = _dot(hid, w2_ref[br]).astype(out_ref.dtype)

    if transposed:
        rows_ref, = rowbuf
        n = pl.program_id(0) * n_steps + pl.program_id(1)
        total = pt_ref.shape[0] * n_steps
        copies = lambda m: _page_copies(pt_ref, cache_ref, buf, sem, m // n_steps, m % n_steps, m % 2,
                                        n_pages, pps, 0, True, True)

        def to_rows(stage, half):
            for br in range(2):
                for i in range(pps + 1):
                    n_rows = PAGE_SIZE if i < pps else CMP_STRIDE
                    page_t = buf[stage, br, :, i * PAGE_SIZE:(i + 1) * PAGE_SIZE]
                    rows_ref[half, br, i * PAGE_SIZE:i * PAGE_SIZE + n_rows, :] = page_t.T[:n_rows]

        @pl.when(n == 0)
        def _():
            for cp in copies(0):
                cp.start()
            for cp in copies(0):
                cp.wait()
            to_rows(0, 0)
            if total > 1:
                for cp in copies(1):
                    cp.start()

        @pl.when(n + 1 < total)
        def _():
            for cp in copies(n + 1):
                cp.wait()

        @pl.when(n + 2 < total)
        def _():
            for cp in copies(n + 2):
                cp.start()

        for parity in range(2):
            @pl.when(n % 2 == parity)
            def _(parity=parity):
                if total > 1:
                    to_rows(1 - parity, 1 - parity)
                compress(lambda br, l: rows_ref[parity, br, pl.ds(l, groups, stride=CMP_STRIDE), :])
    else:
        slot = _stream_pages(pt_ref, cache_ref, buf, sem, n_pages, n_steps, pps, 0, True, False)
        compress(lambda br, l: buf[slot, br, pl.ds(l, groups, stride=CMP_STRIDE), :])


def _compress(page_table, cache, pe2, w1cat, b1cat, w2bd, pps=PAGES_PER_STEP, transposed=False):
    nb, n_pages = page_table.shape
    n_steps = n_pages // pps
    groups = pps * PAGE_SIZE // CMP_STRIDE
    rows = pps * PAGE_SIZE + CMP_STRIDE
    const = lambda shape: pl.BlockSpec(shape, lambda b, s, pt: (0,) * len(shape))
    out_spec = pl.BlockSpec((1, groups, KV_W), lambda b, s, pt: (b, s, 0))
    out_sds = jax.ShapeDtypeStruct((nb, n_steps * groups, KV_W), BF16)
    if transposed:
        stage = [pltpu.VMEM((2, 2, KV_W, (pps + 1) * PAGE_SIZE), F32), pltpu.SemaphoreType.DMA((2,)),
                 pltpu.VMEM((2, 2, rows, KV_W), F32)]
    else:
        stage = [pltpu.VMEM((2, 2, rows, KV_W), F32), pltpu.SemaphoreType.DMA((2,))]
    return pl.pallas_call(
        functools.partial(_compress_kernel, n_pages=n_pages, n_steps=n_steps, pps=pps, transposed=transposed),
        grid_spec=pltpu.PrefetchScalarGridSpec(
            num_scalar_prefetch=1,
            grid=(nb, n_steps),
            in_specs=[pl.BlockSpec(memory_space=pl.ANY),
                      const((2, CMP_BLOCK, KV_W)),
                      const((2, CMP_BLOCK // 2, 2, KV_W, CMP_HIDDEN)),
                      const((2, 1, 2 * CMP_HIDDEN)),
                      const((2, 2 * CMP_HIDDEN, KV_W))],
            out_specs=[out_spec, out_spec],
            scratch_shapes=stage),
        out_shape=[out_sds, out_sds],
        compiler_params=_cparams(("arbitrary", "arbitrary")),
        name="nsa_compress",
    )(page_table, cache, pe2, w1cat, b1cat, w2bd)


def _compress_weights(cmp_pe, cmp_w1, cmp_b1, cmp_w2):
    pe2 = jnp.concatenate([cmp_pe, cmp_pe], axis=-1)
    w1 = cmp_w1.reshape(2, CMP_BLOCK // 2, 2, NSA_HEAD_DIM, CMP_HIDDEN)
    even_odd = w1.reshape(2, CMP_BLOCK // 2, 2 * NSA_HEAD_DIM, CMP_HIDDEN)
    odd_even = w1[:, :, ::-1].reshape(2, CMP_BLOCK // 2, 2 * NSA_HEAD_DIM, CMP_HIDDEN)
    w1cat = jnp.stack([even_odd, odd_even], axis=2).astype(BF16)
    b1cat = jnp.concatenate([cmp_b1, cmp_b1], axis=-1)[:, None, :]
    z2 = jnp.zeros_like(cmp_w2)
    w2bd = jnp.concatenate([jnp.concatenate([cmp_w2, z2], axis=-1),
                            jnp.concatenate([z2, cmp_w2], axis=-1)], axis=1).astype(BF16)
    return pe2, w1cat, b1cat, w2bd


def _overlap_matrix(n_cmp, n_sel):
    cs = lax.broadcasted_iota(jnp.int32, (n_cmp, n_sel), 0) * CMP_STRIDE
    ss = lax.broadcasted_iota(jnp.int32, (n_cmp, n_sel), 1) * SEL_BLOCK
    return ((cs < ss + SEL_BLOCK) & (cs + CMP_BLOCK > ss)).astype(BF16)


SEL_CHUNK = 512
SPREAD_KEYS = 1024
WIN_SPAN = WINDOW + Q_BLOCK


def _nsa_prompt_kernel(q_ref, gt_ref, kc_ref, vc_ref, ks_ref, vs_ref, kw_ref, vw_ref, ex_ref, o_ref,
                       m_scr, acc_scr, chosen_scr, oc_scr, sa_scr, sb_scr, *, n_cmp, n_sel):
    j = pl.program_id(1)
    q0 = j * Q_BLOCK
    tok = lax.broadcasted_iota(jnp.int32, (Q_BLOCK, 1), 0) + q0
    tok4 = jnp.concatenate([tok] * NSA_GROUP, axis=0)
    n_chunks = j // (SEL_CHUNK // Q_BLOCK) + 1
    key_in_span = lax.broadcasted_iota(jnp.int32, (1, SPREAD_KEYS), 1)

    def load_q(g):
        return jnp.concatenate([q_ref[0, :, (NSA_GROUP * g + jh) * LANES:(NSA_GROUP * g + jh + 1) * LANES]
                                for jh in range(NSA_GROUP)], axis=0)

    def softmax_av(s, valid, v):
        s = jnp.where(valid, s, NEG)
        tiles = [s[:, i * LANES:(i + 1) * LANES] for i in range(s.shape[1] // LANES)]
        m = jnp.max(functools.reduce(jnp.maximum, tiles), axis=1, keepdims=True)
        p = jnp.where(valid, jnp.exp2(s - m), 0.0)
        acc = _dot(p.astype(BF16), jnp.concatenate([v, jnp.ones(v.shape, BF16)], axis=1))
        inv = 1.0 / jnp.maximum(acc[:, KV_W:], 1e-30)
        return p, acc[:, :KV_W] * inv, inv

    ov = _overlap_matrix(n_cmp, n_sel)
    cend = lax.broadcasted_iota(jnp.int32, (1, n_cmp), 1) * CMP_STRIDE + (CMP_BLOCK - 1)
    ws = pl.multiple_of(jnp.maximum(q0 - WINDOW, 0), Q_BLOCK)
    wpos = ws + lax.broadcasted_iota(jnp.int32, (1, WIN_SPAN), 1)
    d = tok4 - wpos
    in_window = (d >= 0) & (d < WINDOW)
    gates = gt_ref[0]
    imps = []
    for g in range(NSA_KV_HEADS):
        q = load_q(g)
        p, o_c, inv = softmax_av(_dot_nt(q, kc_ref[0]), cend <= tok4, vc_ref[0])
        _, o_w, _ = softmax_av(_dot_nt(q, kw_ref[0, pl.ds(ws, WIN_SPAN), :]), in_window,
                               vw_ref[0, pl.ds(ws, WIN_SPAN), :])
        psum = jnp.zeros((Q_BLOCK, n_cmp), F32)
        for jh in range(NSA_GROUP):
            h = NSA_GROUP * g + jh
            r = slice(jh * Q_BLOCK, (jh + 1) * Q_BLOCK)
            psum = psum + p[r] * jnp.concatenate([inv[r]] * (n_cmp // LANES), axis=1)
            oc_scr[g, r, :] = gates[:, 3 * h:3 * h + 1] * o_c[r] + gates[:, 3 * h + 2:3 * h + 3] * o_w[r]
        imps.append(_dot(psum.astype(BF16), ov))

    blk = lax.broadcasted_iota(jnp.int32, (1, n_sel), 1)
    cur = lax.shift_right_logical(tok, SEL_SHIFT)
    forced = (blk == 0) | (blk == cur) | (blk == cur - 1)
    free = (blk * SEL_BLOCK <= tok) & jnp.logical_not(forced)
    forced2 = jnp.concatenate([forced] * NSA_KV_HEADS, axis=0)
    free2 = jnp.concatenate([free] * NSA_KV_HEADS, axis=0)
    best = _topk_mask(jnp.where(free2, jnp.concatenate(imps, axis=0), NEG), SEL_TOPN - 3)
    sel2 = jnp.where(forced2, 1.0, best).astype(BF16)

    ones_blk = jnp.ones((SEL_CHUNK, KV_W), BF16)

    for g in range(NSA_KV_HEADS):
        sel = sel2[g * Q_BLOCK:(g + 1) * Q_BLOCK]
        m_scr[...] = jnp.full(m_scr.shape, NEG, F32)
        acc_scr[...] = jnp.zeros(acc_scr.shape, F32)

        def spread(i, carry):
            c0 = pl.multiple_of(i * SPREAD_KEYS, SPREAD_KEYS)
            keep = (_dot(sel, ex_ref[:, pl.ds(c0, SPREAD_KEYS)]) > 0.5) & (key_in_span + c0 <= tok)
            chosen_scr[:, pl.ds(c0, SPREAD_KEYS)] = jnp.where(keep, 0.0, NEG)
            return carry

        lax.fori_loop(0, (n_chunks * SEL_CHUNK + SPREAD_KEYS - 1) // SPREAD_KEYS, spread, 0)

        heads = range(NSA_GROUP)
        rows = [pl.ds(jh * Q_BLOCK, Q_BLOCK) for jh in heads]

        def stage_scores(c, buf):
            k0 = pl.multiple_of(c * SEL_CHUNK, SEL_CHUNK)
            bias = chosen_scr[:, pl.ds(k0, SEL_CHUNK)]
            kblk = ks_ref[0, pl.ds(k0, SEL_CHUNK), :]
            for jh in heads:
                h = NSA_GROUP * g + jh
                buf[rows[jh], :] = _dot_nt(q_ref[0, :, h * LANES:(h + 1) * LANES], kblk) + bias

        def accumulate(c, buf):
            k0 = pl.multiple_of(c * SEL_CHUNK, SEL_CHUNK)
            vext = jnp.concatenate([vs_ref[0, pl.ds(k0, SEL_CHUNK), :], ones_blk], axis=1)
            tiles = [[buf[r, i * LANES:(i + 1) * LANES] for i in range(SEL_CHUNK // LANES)] for r in rows]
            m_old = [m_scr[r, :] for r in rows]
            m_new = [jnp.maximum(mo, jnp.max(functools.reduce(jnp.maximum, t), axis=1, keepdims=True))
                     for mo, t in zip(m_old, tiles)]
            probs = [jnp.concatenate([jnp.exp2(x - mn) for x in t], axis=1).astype(BF16)
                     for mn, t in zip(m_new, tiles)]
            pv = [_dot(p, vext) for p in probs]
            for r, mo, mn, y in zip(rows, m_old, m_new, pv):
                alpha = jnp.exp2(mo - mn)
                acc_scr[r, :] = jnp.concatenate([alpha, alpha], axis=1) * acc_scr[r, :] + y
                m_scr[r, :] = mn

        stage_scores(0, sa_scr)

        def body(i, carry):
            stage_scores(2 * i + 1, sb_scr)
            accumulate(2 * i, sa_scr)
            stage_scores(jnp.minimum(2 * i + 2, n_chunks - 1), sa_scr)
            accumulate(2 * i + 1, sb_scr)
            return carry

        lax.fori_loop(0, n_chunks // 2, body, 0)

        @pl.when(n_chunks % 2 == 1)
        def _():
            accumulate(n_chunks - 1, sa_scr)
        o_s = acc_scr[:, :KV_W] / jnp.maximum(acc_scr[:, KV_W:], 1e-30)

        for jh in range(NSA_GROUP):
            h = NSA_GROUP * g + jh
            r = slice(jh * Q_BLOCK, (jh + 1) * Q_BLOCK)
            o = oc_scr[g, r, :] + gates[:, 3 * h + 1:3 * h + 2] * o_s[r]
            o_ref[0, :, h * LANES:(h + 1) * LANES] = o.astype(o_ref.dtype)


def _nsa_prompt(qpad, gates, kc, vc, kvsel, kvwb):
    nb, t, _ = qpad.shape
    assert t % SPREAD_KEYS == 0 and SPREAD_KEYS == 2 * SEL_CHUNK and t >= WIN_SPAN
    n_cmp = kc.shape[1]
    n_sel = t // SEL_BLOCK
    full = lambda w, k: pl.BlockSpec((1, t, w), lambda b, j: (b, 0, k))
    t_pad = -(-t // SPREAD_KEYS) * SPREAD_KEYS
    expand = jnp.asarray(np.arange(n_sel)[:, None] == (np.arange(t_pad)[None, :] // SEL_BLOCK), BF16)
    return pl.pallas_call(
        functools.partial(_nsa_prompt_kernel, n_cmp=n_cmp, n_sel=n_sel),
        grid=(nb, t // Q_BLOCK),
        in_specs=[pl.BlockSpec((1, Q_BLOCK, QPAD_W), lambda b, j: (b, j, 0)),
                  pl.BlockSpec((1, Q_BLOCK, GZ_PAD), lambda b, j: (b, j, 0)),
                  pl.BlockSpec((1, n_cmp, KV_W), lambda b, j: (b, 0, 0)),
                  pl.BlockSpec((1, n_cmp, KV_W), lambda b, j: (b, 0, 0)),
                  full(KV_W, 0), full(KV_W, 1), full(KV_W, 0), full(KV_W, 1),
                  pl.BlockSpec((n_sel, t_pad), lambda b, j: (0, 0))],
        out_specs=pl.BlockSpec((1, Q_BLOCK, QPAD_W), lambda b, j: (b, j, 0)),
        out_shape=jax.ShapeDtypeStruct((nb, t, QPAD_W), BF16),
        scratch_shapes=[pltpu.VMEM((NSA_GROUP * Q_BLOCK, LANES), F32),
                        pltpu.VMEM((NSA_GROUP * Q_BLOCK, 2 * KV_W), F32),
                        pltpu.VMEM((Q_BLOCK, t_pad), F32),
                        pltpu.VMEM((NSA_KV_HEADS, NSA_GROUP * Q_BLOCK, KV_W), F32),
                        pltpu.VMEM((NSA_GROUP * Q_BLOCK, SEL_CHUNK), F32),
                        pltpu.VMEM((NSA_GROUP * Q_BLOCK, SEL_CHUNK), F32)],
        compiler_params=_cparams(("arbitrary", "arbitrary")),
        name="nsa_prompt",
    )(qpad, gates, kc, vc, kvsel, kvsel, kvwb, kvwb, expand)


def _nsa_sample_a_kernel(q_ref, g_ref, kc_ref, vc_ref, wb_ref, nw_ref, ocw_ref, sel_ref,
                         *, past_len, n_tok, n_sel, n_sel_pad):
    q = q_ref[0]
    rows = q.shape[0]
    n_cmp = kc_ref.shape[1]
    t_row = lax.broadcasted_iota(jnp.int32, (rows, 1), 0) & (n_tok - 1)
    qpos = past_len + t_row
    cend = lax.broadcasted_iota(jnp.int32, (1, n_cmp), 1) * CMP_STRIDE + (CMP_BLOCK - 1)
    p_c = _masked_softmax(_dot_nt(q, kc_ref[0]), cend <= qpos)
    o_c = _dot(p_c.astype(BF16), vc_ref[0])

    per_grp = NSA_GROUP * n_tok
    psum = jnp.concatenate(
        [sum(p_c[g * per_grp + jh * n_tok:g * per_grp + (jh + 1) * n_tok] for jh in range(NSA_GROUP))
         for g in range(NSA_KV_HEADS)], axis=0)
    imp = _dot(psum.astype(BF16), _overlap_matrix(n_cmp, n_sel_pad))
    blk = lax.broadcasted_iota(jnp.int32, (1, n_sel_pad), 1)
    tq = past_len + (lax.broadcasted_iota(jnp.int32, (NSA_KV_HEADS * n_tok, 1), 0) & (n_tok - 1))
    cur = lax.shift_right_logical(tq, SEL_SHIFT)
    forced = (blk == 0) | (blk == cur) | (blk == cur - 1)
    allowed = blk * SEL_BLOCK <= tq
    v = jnp.where(forced, -NEG, jnp.where(allowed, imp, NEG))
    sel_ref[0] = _topk_mask(jnp.where(blk < n_sel, v, 2.0 * NEG), SEL_TOPN)

    wb = wb_ref.shape[1]
    kw = wb_ref[0, :, 0:KV_W].astype(BF16)
    vw = wb_ref[0, :, KV_W:2 * KV_W].astype(BF16)
    kn = nw_ref[0, :, 0:KV_W].astype(BF16)
    vn = nw_ref[0, :, KV_W:2 * KV_W].astype(BF16)
    i1 = lax.broadcasted_iota(jnp.int32, (1, wb), 1)
    d1 = t_row + wb - i1
    valid1 = (d1 >= 0) & (d1 < WINDOW) & (past_len - wb + i1 >= 0)
    i2 = lax.broadcasted_iota(jnp.int32, (1, nw_ref.shape[1]), 1)
    d2 = t_row - i2
    valid2 = (d2 >= 0) & (d2 < WINDOW) & (i2 < n_tok)
    s1 = jnp.where(valid1, _dot_nt(q, kw), NEG)
    s2 = jnp.where(valid2, _dot_nt(q, kn), NEG)
    m = jnp.maximum(jnp.max(s1, axis=1, keepdims=True), jnp.max(s2, axis=1, keepdims=True))
    p1 = jnp.exp2(s1 - m) * valid1.astype(F32)
    p2 = jnp.exp2(s2 - m) * valid2.astype(F32)
    den = jnp.maximum(jnp.sum(p1, axis=1, keepdims=True) + jnp.sum(p2, axis=1, keepdims=True), 1e-30)
    o_w = (_dot(p1.astype(BF16), vw) + _dot(p2.astype(BF16), vn)) / den
    g = g_ref[0]
    ocw_ref[0] = g[:, 0:1] * o_c + g[:, 2:3] * o_w


def _nsa_sample_a(q_rows, g_rows, kc, vc, win_buf, new_win, *, past_len, n_tok):
    nb, rows, _ = q_rows.shape
    n_sel = -(-(past_len + n_tok) // SEL_BLOCK)
    n_sel_pad = -(-n_sel // LANES) * LANES
    blk3 = lambda a: pl.BlockSpec((1,) + a.shape[1:], lambda b: (b, 0, 0))
    return pl.pallas_call(
        functools.partial(_nsa_sample_a_kernel, past_len=past_len, n_tok=n_tok, n_sel=n_sel, n_sel_pad=n_sel_pad),
        grid=(nb,),
        in_specs=[blk3(q_rows), blk3(g_rows), blk3(kc), blk3(vc), blk3(win_buf), blk3(new_win)],
        out_specs=[pl.BlockSpec((1, rows, KV_W), lambda b: (b, 0, 0)),
                   pl.BlockSpec((1, NSA_KV_HEADS * n_tok, n_sel_pad), lambda b: (b, 0, 0))],
        out_shape=[jax.ShapeDtypeStruct((nb, rows, KV_W), F32),
                   jax.ShapeDtypeStruct((nb, NSA_KV_HEADS * n_tok, n_sel_pad), F32)],
        compiler_params=_cparams(("arbitrary",)),
        name="nsa_sample_a",
    )(q_rows, g_rows, kc, vc, win_buf, new_win)


def _nsa_sample_b_kernel(pt_ref, cache_ref, q_ref, g_ref, sel_ref, seln_ref, ns_ref, ocw_ref, ex_ref, o_ref,
                         buf, sem, m_scr, l_scr, acc_scr, *, n_pages, n_steps, pps, n_tok):
    s = pl.program_id(1)
    slot = _stream_pages(pt_ref, cache_ref, buf, sem, n_pages, n_steps, pps, 2 * KV_W, False, True)
    q = q_ref[0]
    rows = q.shape[0]

    @pl.when(s == 0)
    def _():
        m_scr[...] = jnp.full(m_scr.shape, NEG, F32)
        l_scr[...] = jnp.zeros(l_scr.shape, F32)
        acc_scr[...] = jnp.zeros(acc_scr.shape, F32)

    def update(scores, msk, times_v):
        sc = jnp.where(msk, scores, NEG)
        m_old = m_scr[...]
        m_new = jnp.maximum(m_old, jnp.max(sc, axis=1, keepdims=True))
        p = jnp.exp2(sc - m_new) * msk.astype(F32)
        alpha = jnp.exp2(m_old - m_new)
        l_scr[...] = alpha * l_scr[...] + jnp.sum(p, axis=1, keepdims=True)
        acc_scr[...] = alpha * acc_scr[...] + times_v(p.astype(BF16))
        m_scr[...] = m_new

    chosen = _dot(sel_ref[0, 0], ex_ref[...]) > 0.5
    update(_dot(q, buf[slot, 0].astype(BF16)), chosen, lambda p: _dot_nt(p, buf[slot, 1].astype(BF16)))

    @pl.when(s == n_steps - 1)
    def _():
        t_row = lax.broadcasted_iota(jnp.int32, (rows, 1), 0) & (n_tok - 1)
        i2 = lax.broadcasted_iota(jnp.int32, (1, ns_ref.shape[1]), 1)
        msk = (seln_ref[0, 0][:, 0:1] > 0.5) & (i2 <= t_row) & (i2 < n_tok)
        update(_dot_nt(q, ns_ref[0, :, 0:KV_W].astype(BF16)), msk,
               lambda p: _dot(p, ns_ref[0, :, KV_W:2 * KV_W].astype(BF16)))
        o_s = acc_scr[...] / jnp.maximum(l_scr[...], 1e-30)
        o_ref[0] = ocw_ref[0] + g_ref[0][:, 1:2] * o_s


def _nsa_sample_b(page_table, cache, q_rows, g_rows, sel_steps, new_sel, ocw, *, n_tok, pps=PAGES_PER_STEP):
    nb, n_pages = page_table.shape
    n_steps = n_pages // pps
    rows = q_rows.shape[1]
    keys = pps * PAGE_SIZE
    expand = jnp.asarray(np.arange(LANES)[:, None] == (np.arange(keys)[None, :] // SEL_BLOCK), BF16)
    per_b = lambda a: pl.BlockSpec((1,) + a.shape[1:], lambda b, s, pt: (b, 0, 0))
    return pl.pallas_call(
        functools.partial(_nsa_sample_b_kernel, n_pages=n_pages, n_steps=n_steps, pps=pps, n_tok=n_tok),
        grid_spec=pltpu.PrefetchScalarGridSpec(
            num_scalar_prefetch=1,
            grid=(nb, n_steps),
            in_specs=[pl.BlockSpec(memory_space=pl.ANY),
                      per_b(q_rows), per_b(g_rows),
                      pl.BlockSpec((1, 1, rows, LANES), lambda b, s, pt: (b, s, 0, 0)),
                      pl.BlockSpec((1, 1, rows, LANES), lambda b, s, pt: (b, n_steps, 0, 0)),
                      per_b(new_sel), per_b(ocw),
                      pl.BlockSpec((LANES, keys), lambda b, s, pt: (0, 0))],
            out_specs=pl.BlockSpec((1, rows, KV_W), lambda b, s, pt: (b, 0, 0)),
            scratch_shapes=[pltpu.VMEM((2, 2, KV_W, keys), F32), pltpu.SemaphoreType.DMA((2,)),
                            pltpu.VMEM((rows, 1), F32), pltpu.VMEM((rows, 1), F32),
                            pltpu.VMEM((rows, KV_W), F32)]),
        out_shape=jax.ShapeDtypeStruct((nb, rows, KV_W), F32),
        compiler_params=_cparams(("arbitrary", "arbitrary")),
        name="nsa_sample_b",
    )(page_table, cache, q_rows, g_rows, sel_steps, sel_steps, new_sel, ocw, expand)


def _nsa_sample(page_table, cache, win_buf, cw, qpad, gates, kv4, kvw, pps=PAGES_PER_STEP):
    nb, ts, _ = qpad.shape
    past_len = page_table.shape[1] * PAGE_SIZE
    kc, vc = _compress(page_table, cache, *cw, pps=pps, transposed=True)
    rows = NSA_HEADS * ts
    q_rows = qpad.reshape(nb, ts, NSA_HEADS, LANES).transpose(0, 2, 1, 3).reshape(nb, rows, LANES)
    g_rows = gates[:, :, :3 * NSA_HEADS].reshape(nb, ts, NSA_HEADS, 3).transpose(0, 2, 1, 3)
    g_rows = jnp.pad(g_rows.reshape(nb, rows, 3), ((0, 0), (0, 0), (0, LANES - 3)))
    pad_rows = lambda a: jnp.pad(a, ((0, 0), (0, LANES - ts), (0, 0)))
    new_win = pad_rows(kvw)
    new_sel = pad_rows(kv4[:, :, 2 * KV_W:])
    ocw, sel = _nsa_sample_a(q_rows, g_rows, kc, vc, win_buf, new_win, past_len=past_len, n_tok=ts)
    n_steps = page_table.shape[1] // pps
    blk_per_step = pps * PAGE_SIZE // SEL_BLOCK
    n_past_blk = n_steps * blk_per_step
    sel_past = sel[:, :, :n_past_blk].reshape(nb, NSA_KV_HEADS, 1, ts, n_steps, blk_per_step)
    sel_past = jnp.broadcast_to(sel_past, (nb, NSA_KV_HEADS, NSA_GROUP, ts, n_steps, blk_per_step))
    sel_past = sel_past.transpose(0, 4, 1, 2, 3, 5).reshape(nb, n_steps, rows, blk_per_step)
    sel_past = jnp.pad(sel_past, ((0, 0), (0, 0), (0, 0), (0, LANES - blk_per_step)))
    sel_new = jnp.pad(sel[:, :, n_past_blk:], ((0, 0), (0, 0), (0, LANES)))[:, :, :LANES]
    sel_new = sel_new.reshape(nb, NSA_KV_HEADS, 1, ts, LANES)
    sel_new = jnp.broadcast_to(sel_new, (nb, NSA_KV_HEADS, NSA_GROUP, ts, LANES)).reshape(nb, 1, rows, LANES)
    sel_steps = jnp.concatenate([sel_past, sel_new], axis=1).astype(BF16)
    o_rows = _nsa_sample_b(page_table, cache, q_rows, g_rows, sel_steps, new_sel, ocw, n_tok=ts, pps=pps)
    return o_rows.reshape(nb, NSA_HEADS, ts, LANES).transpose(0, 2, 1, 3).reshape(nb, ts, QPAD_W).astype(BF16)


def _outproj_kernel(x_ref, hg_ref, nsa_ref, g1_ref, sc2_ref, sh2_ref, fn_ref, wo1_ref, wo2_ref,
                    x1_ref, h2_ref, *, tm):
    mix =_dot(hg_ref[0], wo1_ref[...]) + _dot(nsa_ref[0], wo2_ref[...])
    x1 = x_ref[0] + g1_ref[0] * mix
    x1_ref[0] = x1
    h2 = _rms(x1, fn_ref[...]) * (1.0 + sc2_ref[0]) + sh2_ref[0]
    _store_tok_tiles(h2_ref, h2, tm)


def _outproj(x, hg_out, nsa, gate1, scale2, shift2, ffn_norm, wo_hg, wo_nsa, tm):
    nb, t, _ = x.shape
    nt = t // tm
    mod_spec = _mod_spec(gate1, tm)
    tile = lambda w: pl.BlockSpec((1, tm, w), lambda b, i: (b, i, 0))
    return pl.pallas_call(
        functools.partial(_outproj_kernel, tm=tm),
        grid=(nb, nt),
        in_specs=[tile(D_MODEL), tile(HG_WIDTH), tile(QPAD_W), mod_spec, mod_spec, mod_spec,
                  pl.BlockSpec((1, D_MODEL), lambda b, i: (0, 0)),
                  pl.BlockSpec((HG_WIDTH, D_MODEL), lambda b, i: (0, 0)),
                  pl.BlockSpec((QPAD_W, D_MODEL), lambda b, i: (0, 0))],
        out_specs=[tile(D_MODEL), pl.BlockSpec((tm * TOK_ROWS, LANES), lambda b, i: (b * nt + i, 0))],
        out_shape=[jax.ShapeDtypeStruct((nb, t, D_MODEL), F32),
                   jax.ShapeDtypeStruct((nb * t * TOK_ROWS, LANES), F32)],
        compiler_params=_cparams(("arbitrary", "arbitrary")),
        name="out_proj",
    )(x, hg_out, nsa, gate1, scale2, shift2, ffn_norm.reshape(1, -1), wo_hg, wo_nsa)


def _split_w_out(w_out):
    wo_hg = w_out[:HG_WIDTH].astype(BF16)
    wn = w_out[HG_WIDTH:].reshape(NSA_HEADS, NSA_HEAD_DIM, D_MODEL)
    z = jnp.zeros_like(wn)
    grp = (jnp.arange(NSA_HEADS) // NSA_GROUP)[:, None, None]
    wn_pad = jnp.where(grp == 0, jnp.concatenate([wn, z], axis=1), jnp.concatenate([z, wn], axis=1))
    return wo_hg, wn_pad.reshape(QPAD_W, D_MODEL).astype(BF16)


def _router_kernel(h_ref, wr_ref, b_ref, e_ref, w_ref, r_ref, cnt_ref, run_scr, *, tm):
    @pl.when(pl.program_id(0) == 0)
    def _():
        run_scr[...] = jnp.zeros(run_scr.shape, F32)

    x = _load_tok_tiles(h_ref, tm).astype(BF16)
    scores = jax.nn.sigmoid(_dot(x, wr_ref[...]))
    biased = scores + b_ref[...]
    lane_i = lax.broadcasted_iota(jnp.int32, (tm, N_EXPERTS), 1)
    lane = lane_i.astype(F32)
    grp_of_lane = lax.shift_right_logical(lane_i, GROUP_SHIFT)

    gcol = lax.broadcasted_iota(jnp.int32, (tm, LANES), 1)
    gs = jnp.full((tm, LANES), 2.0 * NEG, F32)
    for g in range(N_GROUPS):
        mg = jnp.where(grp_of_lane == g, biased, NEG)
        m1 = jnp.max(mg, axis=1, keepdims=True)
        i1 = jnp.min(jnp.where(mg == m1, lane, 1e9), axis=1, keepdims=True)
        m2 = jnp.max(jnp.where(lane == i1, NEG, mg), axis=1, keepdims=True)
        gs = jnp.where(gcol == g, m1 + m2, gs)
    gsel = _topk_mask(gs, TOPK_GROUPS).astype(BF16)
    spread = (lax.broadcasted_iota(jnp.int32, (LANES, N_EXPERTS), 0)
              == lax.shift_right_logical(lax.broadcasted_iota(jnp.int32, (LANES, N_EXPERTS), 1), GROUP_SHIFT)
              ).astype(BF16)
    v = jnp.where(_dot(gsel, spread) > 0.5, biased, NEG)

    onehot = jnp.zeros((tm, N_EXPERTS), F32)
    idxs, wts = [], []
    wsum = jnp.zeros((tm, 1), F32)
    for _ in range(TOP_K):
        m = jnp.max(v, axis=1, keepdims=True)
        idx = jnp.min(jnp.where(v == m, lane, 1e9), axis=1, keepdims=True)
        pick = lane == idx
        wk = jnp.sum(jnp.where(pick, scores, 0.0), axis=1, keepdims=True)
        onehot = jnp.where(pick, 1.0, onehot)
        v = jnp.where(pick, 3.0 * NEG, v)
        idxs.append(idx)
        wts.append(wk)
        wsum = wsum + wk

    earlier = (lax.broadcasted_iota(jnp.int32, (tm, tm), 0) > lax.broadcasted_iota(jnp.int32, (tm, tm), 1))
    before = _dot(earlier.astype(BF16), onehot.astype(BF16)) + run_scr[...]
    e_out = jnp.zeros((tm, LANES), jnp.int32)
    r_out = jnp.zeros((tm, LANES), jnp.int32)
    w_out = jnp.zeros((tm, LANES), F32)
    for k in range(TOP_K):
        rk = jnp.sum(jnp.where(lane == idxs[k], before, 0.0), axis=1, keepdims=True)
        e_out = jnp.where(gcol == k, idxs[k].astype(jnp.int32), e_out)
        r_out = jnp.where(gcol == k, rk.astype(jnp.int32), r_out)
        w_out = jnp.where(gcol == k, wts[k] / wsum * ROUTED_SCALE, w_out)
    e_ref[...] = e_out
    r_ref[...] = r_out
    w_ref[...] = w_out
    run_scr[...] = run_scr[...] + jnp.sum(onehot, axis=0, keepdims=True)
    cnt_ref[...] = run_scr[...]


def _router(h2, w_router, bias, n_tok, tm=ROUTER_TM):
    tile = pl.BlockSpec((tm, LANES), lambda i: (i, 0))
    return pl.pallas_call(
        functools.partial(_router_kernel, tm=tm),
        grid=(n_tok // tm,),
        in_specs=[pl.BlockSpec((tm * TOK_ROWS, LANES), lambda i: (i, 0)),
                  pl.BlockSpec((D_MODEL, N_EXPERTS), lambda i: (0, 0)),
                  pl.BlockSpec((1, N_EXPERTS), lambda i: (0, 0))],
        out_specs=[tile, tile, tile, pl.BlockSpec((1, N_EXPERTS), lambda i: (0, 0))],
        out_shape=[jax.ShapeDtypeStruct((n_tok, LANES), jnp.int32),
                   jax.ShapeDtypeStruct((n_tok, LANES), F32),
                   jax.ShapeDtypeStruct((n_tok, LANES), jnp.int32),
                   jax.ShapeDtypeStruct((1, N_EXPERTS), F32)],
        scratch_shapes=[pltpu.VMEM((1, N_EXPERTS), F32)],
        compiler_params=_cparams(("arbitrary",)),
        name="moe_router",
    )(h2, w_router, bias)


def _dest_kernel(e_ref, r_ref, st_ref, d_ref):
    e = e_ref[...]
    tm = e.shape[0]
    lane = lax.broadcasted_iota(jnp.int32, (tm, N_EXPERTS), 1)
    col = lax.broadcasted_iota(jnp.int32, (tm, LANES), 1)
    st = st_ref[...]
    out = r_ref[...]
    for k in range(TOP_K):
        sk = jnp.sum(jnp.where(lane == e[:, k:k + 1], st, 0.0), axis=1, keepdims=True)
        out = jnp.where(col == k, out + sk.astype(jnp.int32), out)
    d_ref[...] = out


def _moe_dest(top_e, rank, starts, tm=ROUTER_TM):
    n_tok = top_e.shape[0]
    tile = pl.BlockSpec((tm, LANES), lambda i: (i, 0))
    return pl.pallas_call(
        _dest_kernel,
        grid=(n_tok // tm,),
        in_specs=[tile, tile, pl.BlockSpec((1, N_EXPERTS), lambda i: (0, 0))],
        out_specs=tile,
        out_shape=jax.ShapeDtypeStruct((n_tok, LANES), jnp.int32),
        compiler_params=_cparams(("arbitrary",)),
        name="moe_dest",
    )(top_e, rank, starts.astype(F32).reshape(1, -1))


def _moe_layout(counts, n_pairs):
    padded = (counts + MOE_BM - 1) // MOE_BM * MOE_BM
    pad_end = jnp.cumsum(padded)
    starts = pad_end - padded
    n_blocks = (n_pairs + N_EXPERTS * (MOE_BM - 1)) // MOE_BM
    n_used = pad_end[-1] // MOE_BM
    blk = jnp.arange(n_blocks, dtype=jnp.int32)
    used = blk < n_used
    e_of = jnp.sum(pad_end[None, :] <= (jnp.minimum(blk, n_used - 1) * MOE_BM)[:, None], axis=1).astype(jnp.int32)
    e_of = jnp.minimum(e_of, N_EXPERTS - 1)
    shifted = jnp.concatenate([jnp.full((1,), -1, jnp.int32), e_of[:-1]])
    fresh = (used & (e_of != shifted)).astype(jnp.int32)
    w_slot = ((jnp.cumsum(fresh) - 1) % 2).astype(jnp.int32)
    ids = jnp.arange(N_EXPERTS, dtype=jnp.int32)
    later = jnp.where(counts > 0, ids, N_EXPERTS)
    next_nonempty = jnp.concatenate([lax.cummin(later, reverse=True)[1:], jnp.full((1,), N_EXPERTS, jnp.int32)])
    next_e = next_nonempty[e_of].astype(jnp.int32)
    pad_table = jnp.concatenate([starts + counts, padded - counts, n_used[None]]).astype(jnp.int32)
    return (starts.astype(jnp.int32), pad_table,
            (e_of, fresh, used.astype(jnp.int32), w_slot, next_e), n_blocks)


def _dispatch_kernel(pad_ref, dest_ref, h_ref, xs_ref, sem, zeros, zsem, *, tm, n_blocks, bm):
    i = pl.program_id(0)
    n_steps = pl.num_programs(0)

    def row_copy(src_tok, dst_row):
        return pltpu.make_async_copy(
            h_ref.at[pl.ds(pl.multiple_of(src_tok * TOK_ROWS, TOK_ROWS), TOK_ROWS), :],
            xs_ref.at[pl.ds(pl.multiple_of(dst_row * TOK_ROWS, TOK_ROWS), TOK_ROWS), :], sem)

    def issue(t, carry):
        for k in range(TOP_K):
            row_copy(t, dest_ref[0, 0, t * TOP_K + k]).start(priority=k % 2)
        return carry

    def drain(t, carry):
        for _ in range(TOP_K):
            row_copy(0, 0).wait()
        return carry

    @pl.when(i == 0)
    def _():
        zeros[...] = jnp.zeros(zeros.shape, F32)

    def zero_fill(row0, n_rows_static):
        return pltpu.make_async_copy(
            zeros.at[pl.ds(0, n_rows_static * TOK_ROWS), :],
            xs_ref.at[pl.ds(pl.multiple_of(row0 * TOK_ROWS, TOK_ROWS), n_rows_static * TOK_ROWS), :], zsem)

    experts_per_step = -(-N_EXPERTS // n_steps)
    blocks_per_step = -(-n_blocks // n_steps)
    n_used = pad_ref[2 * N_EXPERTS]

    def fill_pass(act):
        def one_expert(j, carry):
            e = jnp.minimum(i * experts_per_step + j, N_EXPERTS - 1)
            live = i * experts_per_step + j < N_EXPERTS
            lo, length = pad_ref[e], pad_ref[N_EXPERTS + e]
            piece = 1 << ((bm - 1).bit_length() - 1)
            while piece >= 1:
                @pl.when(live & ((length & piece) != 0))
                def _(piece=piece):
                    act(zero_fill(lo + (length & ~(2 * piece - 1)), piece))
                piece //= 2
            return carry
        lax.fori_loop(0, experts_per_step, one_expert, 0)

        def one_block(j, carry):
            blk = n_used + i * blocks_per_step + j

            @pl.when(blk < n_blocks)
            def _():
                act(zero_fill(blk * bm, bm))
            return carry
        lax.fori_loop(0, blocks_per_step, one_block, 0)

    fill_pass(lambda cp: cp.start())
    lax.fori_loop(0, tm, issue, 0)
    lax.fori_loop(0, tm, drain, 0)
    fill_pass(lambda cp: cp.wait())


def _dispatch(dest_tiles, h2, pad_table, n_blocks, tm=ROUTER_TM, bm=MOE_BM):
    n_tiles = dest_tiles.shape[0]
    return pl.pallas_call(
        functools.partial(_dispatch_kernel, tm=tm, n_blocks=n_blocks, bm=bm),
        grid_spec=pltpu.PrefetchScalarGridSpec(
            num_scalar_prefetch=1,
            grid=(n_tiles,),
            in_specs=[pl.BlockSpec((1, 1, tm * TOP_K), lambda i, pad: (i, 0, 0), memory_space=pltpu.SMEM),
                      pl.BlockSpec((tm * TOK_ROWS, LANES), lambda i, pad: (i, 0))],
            out_specs=pl.BlockSpec(memory_space=pl.ANY),
            scratch_shapes=[pltpu.SemaphoreType.DMA(()), pltpu.VMEM((bm * TOK_ROWS, LANES), F32),
                            pltpu.SemaphoreType.DMA(())]),
        out_shape=jax.ShapeDtypeStruct((n_blocks * bm * TOK_ROWS, LANES), F32),
        compiler_params=_cparams(("arbitrary",)),
        name="moe_dispatch",
    )(pad_table, dest_tiles, h2)


GMM_SLOTS = 4


def _gmm_kernel(e_ref, fresh_ref, used_ref, wslot_ref, next_ref,
                xs_ref, wg_ref, wu_ref, wd_ref, ys_ref, wg_bf, wu_bf, wd_bf, xbuf, sem,
                wg_st, wu_st, wd_st, wsem, *, bm):
    i = pl.program_id(0)
    n = pl.num_programs(0)

    def fetch(j):
        rows = pl.ds(pl.multiple_of(j * (bm * TOK_ROWS), bm * TOK_ROWS), bm * TOK_ROWS)
        return pltpu.make_async_copy(xs_ref.at[rows, :], xbuf.at[j % GMM_SLOTS], sem.at[j % GMM_SLOTS])

    def weight_copies(e, slot):
        return [pltpu.make_async_copy(src.at[e], dst.at[slot], wsem.at[slot])
                for src, dst in ((wg_ref, wg_st), (wu_ref, wu_st), (wd_ref, wd_st))]

    @pl.when(i == 0)
    def _():
        for cp in weight_copies(e_ref[0], wslot_ref[0]):
            cp.start()
        for j in range(GMM_SLOTS - 1):
            @pl.when((j < n) & (used_ref[jnp.minimum(j, n - 1)] == 1))
            def _():
                fetch(j).start()

    ahead = jnp.minimum(i + GMM_SLOTS - 1, n - 1)

    @pl.when((i + GMM_SLOTS - 1 < n) & (used_ref[ahead] == 1))
    def _():
        fetch(ahead).start()

    @pl.when(fresh_ref[i] == 1)
    def _():
        slot = wslot_ref[i]
        for cp in weight_copies(e_ref[i], slot):
            cp.wait()
        wg_bf[...] = wg_st[slot].astype(BF16)
        wu_bf[...] = wu_st[slot].astype(BF16)
        wd_bf[...] = wd_st[slot].astype(BF16)

        @pl.when(next_ref[i] < N_EXPERTS)
        def _():
            for cp in weight_copies(next_ref[i], 1 - slot):
                cp.start()

    @pl.when(used_ref[i] == 1)
    def _():
        fetch(i).wait()
        x = _load_tok_tiles(xbuf.at[i % GMM_SLOTS], bm).astype(BF16)
        hid = (_silu(_dot(x, wg_bf[...])) * _dot(x, wu_bf[...])).astype(BF16)
        _store_tok_tiles(ys_ref, _dot(hid, wd_bf[...]), bm)

    @pl.when(used_ref[i] == 0)
    def _():
        ys_ref[...] = jnp.zeros(ys_ref.shape, F32)


def _moe_gmm(blocks, xs_sorted, w_gate, w_up, w_down, bm=MOE_BM):
    rows = pl.BlockSpec((bm * TOK_ROWS, LANES), lambda i, *_: (i, 0))
    in_hbm = pl.BlockSpec(memory_space=pl.ANY)
    weights = (w_gate, w_up, w_down)
    return pl.pallas_call(
        functools.partial(_gmm_kernel, bm=bm),
        grid_spec=pltpu.PrefetchScalarGridSpec(
            num_scalar_prefetch=len(blocks),
            grid=(blocks[0].shape[0],),
            in_specs=[in_hbm, in_hbm, in_hbm, in_hbm],
            out_specs=rows,
            scratch_shapes=[pltpu.VMEM(w.shape[1:], BF16) for w in weights]
            + [pltpu.VMEM((GMM_SLOTS, bm * TOK_ROWS, LANES), F32), pltpu.SemaphoreType.DMA((GMM_SLOTS,))]
            + [pltpu.VMEM((2,) + w.shape[1:], F32) for w in weights]
            + [pltpu.SemaphoreType.DMA((2,))]),
        out_shape=jax.ShapeDtypeStruct(xs_sorted.shape, F32),
        compiler_params=_cparams(("arbitrary",)),
        name="moe_experts",
    )(*blocks, xs_sorted, *weights)


def _combine_kernel(dest_ref, dnext_ref, w_ref, x1_ref, h_ref, g2_ref, wsg_ref, wsu_ref, wsd_ref, fn_ref, ys_ref,
                    o_ref, gbuf, sem, routed_scr, *, tm):
    n = pl.program_id(0) * pl.num_programs(1) + pl.program_id(1)
    total = pl.num_programs(0) * pl.num_programs(1)
    slot = n % 2

    def row_copy(src_row, p, sl):
        return pltpu.make_async_copy(
            ys_ref.at[pl.ds(pl.multiple_of(src_row * TOK_ROWS, TOK_ROWS), TOK_ROWS), :],
            gbuf.at[sl, pl.ds(pl.multiple_of(p * TOK_ROWS, TOK_ROWS), TOK_ROWS), :], sem.at[sl])

    def gather(rows_ref, sl):
        def issue(i, carry):
            for k in range(TOP_K):
                p = i * TOP_K + k
                row_copy(rows_ref[0, 0, p], p, sl).start(priority=k % 2)
            return carry
        lax.fori_loop(0, tm, issue, 0)

    def drain(i, carry):
        for _ in range(TOP_K):
            row_copy(0, 0, slot).wait()
        return carry

    @pl.when(n == 0)
    def _():
        gather(dest_ref, slot)

    @pl.when(n + 1 < total)
    def _():
        gather(dnext_ref, 1 - slot)

    h = _load_tok_tiles(h_ref, tm).astype(BF16)
    hid = (_silu(_dot(h, wsg_ref[...])) * _dot(h, wsu_ref[...])).astype(BF16)
    shared = _dot(hid, wsd_ref[...])
    lax.fori_loop(0, tm, drain, 0)

    def weigh(t, carry):
        acc = jnp.zeros((TOK_ROWS, LANES), F32)
        for k in range(TOP_K):
            p = t * TOP_K + k
            acc = acc + w_ref[0, 0, p] * gbuf[slot, pl.ds(pl.multiple_of(p * TOK_ROWS, TOK_ROWS), TOK_ROWS), :]
        routed_scr[pl.ds(pl.multiple_of(t * TOK_ROWS, TOK_ROWS), TOK_ROWS), :] = acc
        return carry

    lax.fori_loop(0, tm, weigh, 0)
    x2 = x1_ref[0] + g2_ref[0] * (_load_tok_tiles(routed_scr, tm) + shared)
    o_ref[0] = _rms(x2, fn_ref[...])


def _combine(dest_tiles, w_tiles, x1, h2, gate2, shared, fnorm, ys_sorted, tile0, tm=COMBINE_TM):
    nb, t, _ = x1.shape
    nt = t // tm
    flat = lambda b, i: tile0 + b * nt + i
    nxt = lambda b, i: tile0 + jnp.minimum(b * nt + i + 1, nb * nt - 1)
    mod_spec = _mod_spec(gate2, tm)
    const = lambda a: pl.BlockSpec(a.shape, lambda b, i: (0, 0))
    return pl.pallas_call(
        functools.partial(_combine_kernel, tm=tm),
        grid=(nb, nt),
        in_specs=[pl.BlockSpec((1, 1, tm * TOP_K), lambda b, i: (flat(b, i), 0, 0), memory_space=pltpu.SMEM),
                  pl.BlockSpec((1, 1, tm * TOP_K), lambda b, i: (nxt(b, i), 0, 0), memory_space=pltpu.SMEM),
                  pl.BlockSpec((1, 1, tm * TOP_K), lambda b, i: (flat(b, i), 0, 0), memory_space=pltpu.SMEM),
                  pl.BlockSpec((1, tm, D_MODEL), lambda b, i: (b, i, 0)),
                  pl.BlockSpec((tm * TOK_ROWS, LANES), lambda b, i: (flat(b, i), 0)),
                  mod_spec, const(shared[0]), const(shared[1]), const(shared[2]), const(fnorm),
                  pl.BlockSpec(memory_space=pl.ANY)],
        out_specs=pl.BlockSpec((1, tm, D_MODEL), lambda b, i: (b, i, 0)),
        out_shape=jax.ShapeDtypeStruct((nb, t, D_MODEL), F32),
        scratch_shapes=[pltpu.VMEM((2, tm * TOP_K * TOK_ROWS, LANES), F32), pltpu.SemaphoreType.DMA((2,)),
                        pltpu.VMEM((tm * TOK_ROWS, LANES), F32)],
        compiler_params=_cparams(("arbitrary", "arbitrary")),
        name="moe_combine",
    )(dest_tiles, dest_tiles, w_tiles, x1, h2, gate2, *shared, fnorm, ys_sorted)


def kernel(x_prompt, x_sample, c_prompt, c_sample, cache_nsa_kv, cache_win_kv, state_hgrn, page_table,
           attn_norm, ffn_norm, final_norm, hg_norm, w_ada, b_ada, w_in, hg_lb,
           cmp_pe, cmp_w1, cmp_b1, cmp_w2, w_out, w_router, router_bias,
           w_gate, w_up, w_down, ws_gate, ws_up, ws_down):
    nbp, t, _ = x_prompt.shape
    nbs, ts, _ = x_sample.shape
    ns = nbs * ts
    n_all = nbp * t + ns

    c_all = jnp.concatenate([c_prompt, c_sample], axis=0)
    c_all = jnp.pad(c_all, ((0, -c_all.shape[0] % SUBLANES), (0, 0)))
    mod = _ada(c_all, w_ada[0], b_ada[0])
    modp = mod[:nbp].reshape(nbp, 1, 6, D_MODEL)
    mods = jnp.repeat(mod[nbp:nbp + nbs].reshape(nbs, 1, 6, D_MODEL), ts, axis=1).reshape(1, ns, 6, D_MODEL)

    w_pad = _pad_w_in(w_in[0])
    cw = _compress_weights(cmp_pe[0], cmp_w1[0], cmp_b1[0], cmp_w2[0])
    wo_hg, wo_nsa = _split_w_out(w_out[0])

    hg, qpad, kv4, kvw, gates, kvsel, kvwb = _inproj(
        x_prompt, modp[:, :, 1], modp[:, :, 0], attn_norm[0], w_pad, PROJ_TM)
    hg_out, hg_state_p = _hgrn(hg, hg_lb, jnp.zeros((nbp, HG_HEADS, HG_DK, HG_DK), F32), hg_norm[0],
                               HGRN_TC, HG_CHUNK)
    n_pages_p = t // PAGE_SIZE
    ptp = jnp.arange(nbp * n_pages_p, dtype=jnp.int32).reshape(nbp, n_pages_p)
    kc, vc = _compress(ptp, kv4.reshape(nbp * n_pages_p, PAGE_SIZE, 4 * KV_W), *cw)
    nsa = _nsa_prompt(qpad, gates, kc, vc, kvsel, kvwb)
    x1p, h2p = _outproj(x_prompt, hg_out, nsa, modp[:, :, 2], modp[:, :, 4], modp[:, :, 3],
                        ffn_norm[0], wo_hg, wo_nsa, PROJ_TM)

    xs = x_sample.reshape(1, ns, D_MODEL)
    hg_s, qpad_s, kv4_s, kvw_s, gates_s, _, _ = _inproj(
        xs, mods[:, :, 1], mods[:, :, 0], attn_norm[0], w_pad, ns)
    hg_out_s, hg_state_s = _hgrn(hg_s.reshape(nbs, ts, 4 * HG_WIDTH), hg_lb, state_hgrn[0], hg_norm[0], ts, ts)
    cache = cache_nsa_kv[0].transpose(0, 2, 3, 4, 1).reshape(-1, 4 * KV_W, PAGE_SIZE)
    win_buf = cache_win_kv[0].reshape(nbs, -1, 2 * KV_W)
    nsa_s = _nsa_sample(page_table, cache, win_buf, cw, qpad_s.reshape(nbs, ts, QPAD_W),
                        gates_s.reshape(nbs, ts, GZ_PAD), kv4_s.reshape(nbs, ts, 4 * KV_W),
                        kvw_s.reshape(nbs, ts, 2 * KV_W)).reshape(1, ns, QPAD_W)
    x1s, h2s = _outproj(xs, hg_out_s.reshape(1, ns, HG_WIDTH), nsa_s, mods[:, :, 2], mods[:, :, 4], mods[:, :, 3],
                        ffn_norm[0], wo_hg, wo_nsa, ns)

    h2 = jnp.concatenate([h2p, h2s], axis=0)
    top_e, top_w, rank, counts = _router(h2, w_router[0].astype(BF16), router_bias[0].reshape(1, -1), n_all)
    starts, pad_table, blocks, n_blocks = _moe_layout(counts[0].astype(jnp.int32), n_all * TOP_K)
    dest = _moe_dest(top_e, rank, starts)[:, :TOP_K].reshape(-1)
    xs_sorted = _dispatch(dest.reshape(n_all // ROUTER_TM, 1, ROUTER_TM * TOP_K), h2, pad_table, n_blocks)
    ys_sorted = _moe_gmm(blocks, xs_sorted, w_gate[0], w_up[0], w_down[0])
    dest_c = dest.reshape(n_all // COMBINE_TM, 1, COMBINE_TM * TOP_K)
    w_c = top_w[:, :TOP_K].reshape(n_all // COMBINE_TM, 1, COMBINE_TM * TOP_K)
    shared = (ws_gate[0].astype(BF16), ws_up[0].astype(BF16), ws_down[0].astype(BF16))
    fnorm = final_norm.reshape(1, -1)
    y_prompt = _combine(dest_c, w_c, x1p, h2, modp[:, :, 5], shared, fnorm, ys_sorted, 0)
    y_sample = _combine(dest_c, w_c, x1s, h2, mods[:, :, 5], shared, fnorm, ys_sorted, nbp * t // COMBINE_TM)

    wb = win_buf.shape[1]
    win_p = kvw[:, t - min(WINDOW, t):]
    win_s = jnp.concatenate([win_buf, kvw_s.reshape(nbs, ts, 2 * KV_W)], axis=1)[:, -wb:]
    kv_shape = (4, NSA_KV_HEADS, NSA_HEAD_DIM)
    win_shape = (2, NSA_KV_HEADS, NSA_HEAD_DIM)
    return (y_prompt,
            y_sample.reshape(nbs, ts, D_MODEL),
            kv4.reshape(1, nbp, t, *kv_shape),
            win_p.reshape(1, nbp, -1, *win_shape),
            hg_state_p[None],
            kv4_s.reshape(1, nbs, ts, *kv_shape),
            win_s.reshape(1, nbs, wb, *win_shape),
            hg_state_s[None])
```
